```python
import math
import jax, jax.numpy as jnp
from jax import lax
import numpy as np

D_MODEL = 1024
BATCH = 8
SEQ = 4096
DEPTH = 1

SSD_HEADS = 8
SSD_HEAD_DIM = 64
SSD_INNER = SSD_HEADS * SSD_HEAD_DIM
SSD_GROUPS = 2
SSD_STATE = 128
SSD_CONV = 4
SSD_CHUNK = 128
SSD_XBC = SSD_INNER + 2 * SSD_GROUPS * SSD_STATE
MLA_HEADS = 8
MLA_NOPE = 64
MLA_ROPE = 32
MLA_QK = MLA_NOPE + MLA_ROPE
MLA_V = 64
MLA_Q_RANK = 384
MLA_KV_RANK = 256
ROPE_THETA = 10000.0
ATTN_BLOCK = 128
MIX_WIDTH = SSD_INNER + MLA_HEADS * MLA_V
IN_WIDTH = SSD_INNER + SSD_XBC + SSD_HEADS + MLA_Q_RANK + MLA_KV_RANK + MLA_ROPE
IN_SPLITS = (SSD_INNER,
             SSD_INNER + SSD_XBC,
             SSD_INNER + SSD_XBC + SSD_HEADS,
             SSD_INNER + SSD_XBC + SSD_HEADS + MLA_Q_RANK,
             SSD_INNER + SSD_XBC + SSD_HEADS + MLA_Q_RANK + MLA_KV_RANK)
MEM_TOKENS = 256
MEM_HEADS = 4
MEM_HEAD_DIM = D_MODEL // MEM_HEADS
D_FF = 4 * D_MODEL
LN_EPS = 1e-5
RMS_EPS = 1e-6
DEEPNORM_ALPHA = (2.0 * DEPTH) ** 0.25
DEEPNORM_BETA = (8.0 * DEPTH) ** -0.25

kernel_name = "hybrid_ssd_mla_memxattn_deepnorm_layer"


def layer_norm(x, g, b):
    xf = x.astype(jnp.float32)
    mu = jnp.mean(xf, axis=-1, keepdims=True)
    var = jnp.mean(jnp.square(xf - mu), axis=-1, keepdims=True)
    return ((xf - mu) * lax.rsqrt(var + LN_EPS) * g.astype(jnp.float32) + b.astype(jnp.float32)).astype(x.dtype)


def rms_norm(x, g):
    xf = x.astype(jnp.float32)
    ms = jnp.mean(jnp.square(xf), axis=-1, keepdims=True)
    return (xf * lax.rsqrt(ms + RMS_EPS) * g.astype(jnp.float32)).astype(x.dtype)


def grouped_rms_norm(y, g, groups):
    b, s, c = y.shape
    yg = y.reshape(b, s, groups, c // groups)
    yg = yg * lax.rsqrt(jnp.mean(jnp.square(yg), axis=-1, keepdims=True) + RMS_EPS)
    return yg.reshape(b, s, c) * g.astype(jnp.float32)


def apply_rope(x, cos, sin):
    half = x.shape[-1] // 2
    xf = x.astype(jnp.float32)
    x1, x2 = xf[..., :half], xf[..., half:]
    return jnp.concatenate([x1 * cos - x2 * sin, x2 * cos + x1 * sin], axis=-1).astype(x.dtype)


def causal_depthwise_conv(u, w, b):
    c = u.shape[-1]
    y = lax.conv_general_dilated(u, w[:, None, :].astype(u.dtype), window_strides=(1,),
                                 padding=[(SSD_CONV - 1, 0)],
                                 dimension_numbers=("NWC", "WIO", "NWC"),
                                 feature_group_count=c)
    return y + b


def segsum(a):
    t = a.shape[-1]
    aa = jnp.broadcast_to(a[..., :, None], a.shape + (t,))
    aa = jnp.where(jnp.tril(jnp.ones((t, t), dtype=bool), -1), aa, 0.0)
    ss = jnp.cumsum(aa, axis=-2)
    return jnp.where(jnp.tril(jnp.ones((t, t), dtype=bool)), ss, -jnp.inf)


def ssd_chunked_scan(x, dt, a_head, bm, cm):
    b, s, h, p = x.shape
    g, n = bm.shape[-2:]
    e = h // g
    L = SSD_CHUNK
    c = s // L
    xf = (x.astype(jnp.float32) * dt[..., None]).reshape(b, c, L, g, e, p)
    a = jnp.moveaxis((dt * a_head).reshape(b, c, L, g, e), 2, -1)
    bc = bm.astype(jnp.float32).reshape(b, c, L, g, n)
    cc = cm.astype(jnp.float32).reshape(b, c, L, g, n)
    a_cs = jnp.cumsum(a, axis=-1)
    decay_ls = jnp.exp(segsum(a))
    cb = jnp.einsum("bclgn,bcsgn->bcgls", cc, bc)
    y_diag = jnp.einsum("bcgls,bcgels,bcsgep->bclgep", cb, decay_ls, xf)
    decay_to_end = jnp.exp(a_cs[..., -1:] - a_cs)
    states = jnp.einsum("bclgn,bcgel,bclgep->bcgepn", bc, decay_to_end, xf)
    chunk_decay = jnp.exp(a_cs[..., -1])

    def step(carry, inp):
        st, dec = inp
        return carry * dec[..., None, None] + st, carry

    init = jnp.zeros((b, g, e, p, n), jnp.float32)
    _, prev = lax.scan(step, init, (jnp.moveaxis(states, 1, 0), jnp.moveaxis(chunk_decay, 1, 0)))
    prev = jnp.moveaxis(prev, 0, 1)
    y_off = jnp.einsum("bclgn,bcgepn,bcgel->bclgep", cc, prev, jnp.exp(a_cs))
    return (y_diag + y_off).reshape(b, s, h, p)


def causal_block_attention(q, k, v, scale):
    b, s, h, d = q.shape
    nb = s // ATTN_BLOCK
    qb = jnp.moveaxis(q.reshape(b, nb, ATTN_BLOCK, h, d), 1, 0)
    k_pos = jnp.arange(s)

    def one_block(args):
        q_blk, i = args
        q_pos = i * ATTN_BLOCK + jnp.arange(ATTN_BLOCK)
        sc = jnp.einsum("bqhd,bkhd->bhqk", q_blk, k).astype(jnp.float32) * scale
        sc = jnp.where(k_pos[None, :] <= q_pos[:, None], sc, -jnp.inf)
        pr = jax.nn.softmax(sc, axis=-1).astype(v.dtype)
        return jnp.einsum("bhqk,bkhd->bqhd", pr, v)

    out = lax.map(one_block, (qb, jnp.arange(nb)))
    return jnp.moveaxis(out, 0, 1).reshape(b, s, h, v.shape[-1])


def hybrid_mixer(h, cos, sin, w_in, conv_w, conv_b, dt_bias, a_log, d_skip, ssd_norm_g,
                 q_norm_g, w_q_up, kv_norm_g, w_kv_up, w_out):
    b, s, _ = h.shape
    proj = h @ w_in
    z, xbc, dt_raw, q_lat, kv_lat, k_r = jnp.split(proj, IN_SPLITS, axis=-1)
    xbc = jax.nn.silu(causal_depthwise_conv(xbc, conv_w, conv_b))
    xs, bm, cm = jnp.split(xbc, [SSD_INNER, SSD_INNER + SSD_GROUPS * SSD_STATE], axis=-1)
    xs = xs.reshape(b, s, SSD_HEADS, SSD_HEAD_DIM)
    bm = bm.reshape(b, s, SSD_GROUPS, SSD_STATE)
    cm = cm.reshape(b, s, SSD_GROUPS, SSD_STATE)
    dt = jax.nn.softplus(dt_raw.astype(jnp.float32) + dt_bias.astype(jnp.float32))
    a_head = -jnp.exp(a_log.astype(jnp.float32))
    y = ssd_chunked_scan(xs, dt, a_head, bm, cm) + xs.astype(jnp.float32) * d_skip.astype(jnp.float32)[:, None]
    y = y.reshape(b, s, SSD_INNER) * jax.nn.silu(z.astype(jnp.float32))
    y = grouped_rms_norm(y, ssd_norm_g, SSD_GROUPS).astype(h.dtype)
    q = (rms_norm(q_lat, q_norm_g) @ w_q_up).reshape(b, s, MLA_HEADS, MLA_QK)
    q = jnp.concatenate([q[..., :MLA_NOPE], apply_rope(q[..., MLA_NOPE:], cos, sin)], axis=-1)
    kv = (rms_norm(kv_lat, kv_norm_g) @ w_kv_up).reshape(b, s, MLA_HEADS, MLA_NOPE + MLA_V)
    k_pe = apply_rope(k_r[:, :, None, :], cos, sin)
    k = jnp.concatenate([kv[..., :MLA_NOPE],
                         jnp.broadcast_to(k_pe, (b, s, MLA_HEADS, MLA_ROPE))], axis=-1)
    v = kv[..., MLA_NOPE:]
    o = causal_block_attention(q, k, v, MLA_QK ** -0.5).reshape(b, s, MLA_HEADS * MLA_V)
    return jnp.concatenate([y, o], axis=-1) @ w_out


def memory_cross_attention(h, mem, w_q, w_k, w_v, w_o):
    b, s, _ = h.shape
    m = mem.shape[1]
    q = (h @ w_q).reshape(b, s, MEM_HEADS, MEM_HEAD_DIM)
    k = (mem @ w_k).reshape(b, m, MEM_HEADS, MEM_HEAD_DIM)
    v = (mem @ w_v).reshape(b, m, MEM_HEADS, MEM_HEAD_DIM)
    sc = jnp.einsum("bshd,bmhd->bhsm", q, k).astype(jnp.float32) * (MEM_HEAD_DIM ** -0.5)
    pr = jax.nn.softmax(sc, axis=-1).astype(v.dtype)
    o = jnp.einsum("bhsm,bmhd->bshd", pr, v).reshape(b, s, D_MODEL)
    return o @ w_o


def sq_relu_mlp(h, w_up, w_down):
    return jnp.square(jax.nn.relu(h @ w_up)) @ w_down


def _fwd_setup_inputs(seed: int = 0) -> dict:
    key = jax.random.key(seed)
    ks = jax.random.split(key, 32)
    f32 = jnp.float32

    def w(k, shape, fan_in, scale=1.0):
        return jax.random.normal(k, shape, f32) * (fan_in ** -0.5) * scale

    def gain(k, shape):
        return 1.0 + 0.02 * jax.random.normal(k, shape, f32)

    def bias(k, shape):
        return 0.02 * jax.random.normal(k, shape, f32)

    x = jax.random.normal(ks[0], (BATCH, SEQ, D_MODEL), f32)
    mem = jax.random.normal(ks[1], (BATCH, MEM_TOKENS, D_MODEL), f32)
    start = jax.random.randint(ks[2], (BATCH, 1), 0, 4096, dtype=jnp.int32)
    positions = (start + jnp.arange(SEQ, dtype=jnp.int32)[None, :]).astype(jnp.int32)

    dt0 = jnp.exp(jax.random.uniform(ks[3], (DEPTH, SSD_HEADS), f32,
                                     minval=math.log(1e-3), maxval=math.log(1e-1)))
    dt_bias = dt0 + jnp.log(-jnp.expm1(-dt0))
    a_log = jnp.log(jax.random.uniform(ks[4], (DEPTH, SSD_HEADS), f32, minval=1.0, maxval=16.0))
    v_col = (jnp.arange(MLA_NOPE + MLA_V) >= MLA_NOPE)
    kv_scale = jnp.tile(jnp.where(v_col, DEEPNORM_BETA, 1.0), MLA_HEADS).astype(f32)

    return {
        "x": x,
        "mem": mem,
        "positions": positions,
        "ln_in_g": gain(ks[5], (D_MODEL,)),
        "ln_in_b": bias(ks[6], (D_MODEL,)),
        "w_in": w(ks[7], (DEPTH, D_MODEL, IN_WIDTH), D_MODEL),
        "conv_w": w(ks[8], (DEPTH, SSD_CONV, SSD_XBC), SSD_CONV),
        "conv_b": bias(ks[9], (DEPTH, SSD_XBC)),
        "dt_bias": dt_bias,
        "a_log": a_log,
        "d_skip": gain(ks[10], (DEPTH, SSD_HEADS)),
        "ssd_norm_g": gain(ks[11], (DEPTH, SSD_INNER)),
        "q_norm_g": gain(ks[12], (DEPTH, MLA_Q_RANK)),
        "w_q_up": w(ks[13], (DEPTH, MLA_Q_RANK, MLA_HEADS * MLA_QK), MLA_Q_RANK),
        "kv_norm_g": gain(ks[14], (DEPTH, MLA_KV_RANK)),
        "w_kv_up": w(ks[15], (DEPTH, MLA_KV_RANK, MLA_HEADS * (MLA_NOPE + MLA_V)), MLA_KV_RANK) * kv_scale,
        "w_mix_out": w(ks[16], (DEPTH, MIX_WIDTH, D_MODEL), MIX_WIDTH, DEEPNORM_BETA),
        "ln1_g": gain(ks[17], (DEPTH, D_MODEL)),
        "ln1_b": bias(ks[18], (DEPTH, D_MODEL)),
        "w_mem_q": w(ks[19], (DEPTH, D_MODEL, D_MODEL), D_MODEL),
        "w_mem_k": w(ks[20], (DEPTH, D_MODEL, D_MODEL), D_MODEL),
        "w_mem_v": w(ks[21], (DEPTH, D_MODEL, D_MODEL), D_MODEL, DEEPNORM_BETA),
        "w_mem_o": w(ks[22], (DEPTH, D_MODEL, D_MODEL), D_MODEL, DEEPNORM_BETA),
        "ln2_g": gain(ks[23], (DEPTH, D_MODEL)),
        "ln2_b": bias(ks[24], (DEPTH, D_MODEL)),
        "w_up": w(ks[25], (DEPTH, D_MODEL, D_FF), D_MODEL, DEEPNORM_BETA),
        "w_down": w(ks[26], (DEPTH, D_FF, D_MODEL), D_FF, DEEPNORM_BETA),
        "ln3_g": gain(ks[27], (DEPTH, D_MODEL)),
        "ln3_b": bias(ks[28], (DEPTH, D_MODEL)),
    }


def _fwd_reference(x, mem, positions, ln_in_g, ln_in_b, w_in, conv_w, conv_b, dt_bias, a_log, d_skip,
              ssd_norm_g, q_norm_g, w_q_up, kv_norm_g, w_kv_up, w_mix_out, ln1_g, ln1_b,
              w_mem_q, w_mem_k, w_mem_v, w_mem_o, ln2_g, ln2_b, w_up, w_down, ln3_g, ln3_b):
    half = MLA_ROPE // 2
    inv_freq = jnp.power(ROPE_THETA, -jnp.arange(half, dtype=jnp.float32) / half)
    ang = positions.astype(jnp.float32)[..., None] * inv_freq
    cos = jnp.cos(ang)[:, :, None, :]
    sin = jnp.sin(ang)[:, :, None, :]

    h = layer_norm(x, ln_in_g, ln_in_b)
    for l in range(DEPTH):
        mix = hybrid_mixer(h, cos, sin, w_in[l], conv_w[l], conv_b[l], dt_bias[l], a_log[l],
                           d_skip[l], ssd_norm_g[l], q_norm_g[l], w_q_up[l], kv_norm_g[l],
                           w_kv_up[l], w_mix_out[l])
        h = layer_norm(DEEPNORM_ALPHA * h + mix, ln1_g[l], ln1_b[l])
        xa = memory_cross_attention(h, mem, w_mem_q[l], w_mem_k[l], w_mem_v[l], w_mem_o[l])
        h = layer_norm(DEEPNORM_ALPHA * h + xa, ln2_g[l], ln2_b[l])
        ff = sq_relu_mlp(h, w_up[l], w_down[l])
        h = layer_norm(DEEPNORM_ALPHA * h + ff, ln3_g[l], ln3_b[l])
    return h


import jax as _jax
import jax.numpy as _jnp

TWIN_FORMAT = 'train_step'
FWD_PARAMS = ['x', 'mem', 'positions', 'ln_in_g', 'ln_in_b', 'w_in', 'conv_w', 'conv_b', 'dt_bias', 'a_log', 'd_skip', 'ssd_norm_g', 'q_norm_g', 'w_q_up', 'kv_norm_g', 'w_kv_up', 'w_mix_out', 'ln1_g', 'ln1_b', 'w_mem_q', 'w_mem_k', 'w_mem_v', 'w_mem_o', 'ln2_g', 'ln2_b', 'w_up', 'w_down', 'ln3_g', 'ln3_b']
TWIN_WEIGHTS = ['ln_in_g', 'ln_in_b', 'w_in', 'conv_w', 'conv_b', 'dt_bias', 'a_log', 'd_skip', 'ssd_norm_g', 'q_norm_g', 'w_q_up', 'kv_norm_g', 'w_kv_up', 'w_mix_out', 'ln1_g', 'ln1_b', 'w_mem_q', 'w_mem_k', 'w_mem_v', 'w_mem_o', 'ln2_g', 'ln2_b', 'w_up', 'w_down', 'ln3_g', 'ln3_b']
TWIN_DIFF_INPUT = 'x'
TWIN_INPUTS = ['x', 'mem', 'positions', 'ln_in_g', 'ln_in_b', 'w_in', 'conv_w', 'conv_b', 'dt_bias', 'a_log', 'd_skip', 'ssd_norm_g', 'q_norm_g', 'w_q_up', 'kv_norm_g', 'w_kv_up', 'w_mix_out', 'ln1_g', 'ln1_b', 'w_mem_q', 'w_mem_k', 'w_mem_v', 'w_mem_o', 'ln2_g', 'ln2_b', 'w_up', 'w_down', 'ln3_g', 'ln3_b', 'loss_target', 'm_ln_in_g', 'm_ln_in_b', 'm_w_in', 'm_conv_w', 'm_conv_b', 'm_dt_bias', 'm_a_log', 'm_d_skip', 'm_ssd_norm_g', 'm_q_norm_g', 'm_w_q_up', 'm_kv_norm_g', 'm_w_kv_up', 'm_w_mix_out', 'm_ln1_g', 'm_ln1_b', 'm_w_mem_q', 'm_w_mem_k', 'm_w_mem_v', 'm_w_mem_o', 'm_ln2_g', 'm_ln2_b', 'm_w_up', 'm_w_down', 'm_ln3_g', 'm_ln3_b', 'v_ln_in_g', 'v_ln_in_b', 'v_w_in', 'v_conv_w', 'v_conv_b', 'v_dt_bias', 'v_a_log', 'v_d_skip', 'v_ssd_norm_g', 'v_q_norm_g', 'v_w_q_up', 'v_kv_norm_g', 'v_w_kv_up', 'v_w_mix_out', 'v_ln1_g', 'v_ln1_b', 'v_w_mem_q', 'v_w_mem_k', 'v_w_mem_v', 'v_w_mem_o', 'v_ln2_g', 'v_ln2_b', 'v_w_up', 'v_w_down', 'v_ln3_g', 'v_ln3_b']
TWIN_OUTPUTS = ['loss', 'grad_x', 'grad_ln_in_g', 'grad_ln_in_b', 'grad_w_in', 'grad_conv_w', 'grad_conv_b', 'grad_dt_bias', 'grad_a_log', 'grad_d_skip', 'grad_ssd_norm_g', 'grad_q_norm_g', 'grad_w_q_up', 'grad_kv_norm_g', 'grad_w_kv_up', 'grad_w_mix_out', 'grad_ln1_g', 'grad_ln1_b', 'grad_w_mem_q', 'grad_w_mem_k', 'grad_w_mem_v', 'grad_w_mem_o', 'grad_ln2_g', 'grad_ln2_b', 'grad_w_up', 'grad_w_down', 'grad_ln3_g', 'grad_ln3_b', 'delta_ln_in_g', 'delta_ln_in_b', 'delta_w_in', 'delta_conv_w', 'delta_conv_b', 'delta_dt_bias', 'delta_a_log', 'delta_d_skip', 'delta_ssd_norm_g', 'delta_q_norm_g', 'delta_w_q_up', 'delta_kv_norm_g', 'delta_w_kv_up', 'delta_w_mix_out', 'delta_ln1_g', 'delta_ln1_b', 'delta_w_mem_q', 'delta_w_mem_k', 'delta_w_mem_v', 'delta_w_mem_o', 'delta_ln2_g', 'delta_ln2_b', 'delta_w_up', 'delta_w_down', 'delta_ln3_g', 'delta_ln3_b', 'new_m_ln_in_g', 'new_m_ln_in_b', 'new_m_w_in', 'new_m_conv_w', 'new_m_conv_b', 'new_m_dt_bias', 'new_m_a_log', 'new_m_d_skip', 'new_m_ssd_norm_g', 'new_m_q_norm_g', 'new_m_w_q_up', 'new_m_kv_norm_g', 'new_m_w_kv_up', 'new_m_w_mix_out', 'new_m_ln1_g', 'new_m_ln1_b', 'new_m_w_mem_q', 'new_m_w_mem_k', 'new_m_w_mem_v', 'new_m_w_mem_o', 'new_m_ln2_g', 'new_m_ln2_b', 'new_m_w_up', 'new_m_w_down', 'new_m_ln3_g', 'new_m_ln3_b', 'new_v_ln_in_g', 'new_v_ln_in_b', 'new_v_w_in', 'new_v_conv_w', 'new_v_conv_b', 'new_v_dt_bias', 'new_v_a_log', 'new_v_d_skip', 'new_v_ssd_norm_g', 'new_v_q_norm_g', 'new_v_w_q_up', 'new_v_kv_norm_g', 'new_v_w_kv_up', 'new_v_w_mix_out', 'new_v_ln1_g', 'new_v_ln1_b', 'new_v_w_mem_q', 'new_v_w_mem_k', 'new_v_w_mem_v', 'new_v_w_mem_o', 'new_v_ln2_g', 'new_v_ln2_b', 'new_v_w_up', 'new_v_w_down', 'new_v_ln3_g', 'new_v_ln3_b']
TWIN_LEAF_KINDS = {'loss': 'loss', 'grad_x': 'grad_x', 'grad_ln_in_g': 'grad_w', 'grad_ln_in_b': 'grad_w', 'grad_w_in': 'grad_w', 'grad_conv_w': 'grad_w', 'grad_conv_b': 'grad_w', 'grad_dt_bias': 'grad_w', 'grad_a_log': 'grad_w', 'grad_d_skip': 'grad_w', 'grad_ssd_norm_g': 'grad_w', 'grad_q_norm_g': 'grad_w', 'grad_w_q_up': 'grad_w', 'grad_kv_norm_g': 'grad_w', 'grad_w_kv_up': 'grad_w', 'grad_w_mix_out': 'grad_w', 'grad_ln1_g': 'grad_w', 'grad_ln1_b': 'grad_w', 'grad_w_mem_q': 'grad_w', 'grad_w_mem_k': 'grad_w', 'grad_w_mem_v': 'grad_w', 'grad_w_mem_o': 'grad_w', 'grad_ln2_g': 'grad_w', 'grad_ln2_b': 'grad_w', 'grad_w_up': 'grad_w', 'grad_w_down': 'grad_w', 'grad_ln3_g': 'grad_w', 'grad_ln3_b': 'grad_w', 'delta_ln_in_g': 'delta_w', 'delta_ln_in_b': 'delta_w', 'delta_w_in': 'delta_w', 'delta_conv_w': 'delta_w', 'delta_conv_b': 'delta_w', 'delta_dt_bias': 'delta_w', 'delta_a_log': 'delta_w', 'delta_d_skip': 'delta_w', 'delta_ssd_norm_g': 'delta_w', 'delta_q_norm_g': 'delta_w', 'delta_w_q_up': 'delta_w', 'delta_kv_norm_g': 'delta_w', 'delta_w_kv_up': 'delta_w', 'delta_w_mix_out': 'delta_w', 'delta_ln1_g': 'delta_w', 'delta_ln1_b': 'delta_w', 'delta_w_mem_q': 'delta_w', 'delta_w_mem_k': 'delta_w', 'delta_w_mem_v': 'delta_w', 'delta_w_mem_o': 'delta_w', 'delta_ln2_g': 'delta_w', 'delta_ln2_b': 'delta_w', 'delta_w_up': 'delta_w', 'delta_w_down': 'delta_w', 'delta_ln3_g': 'delta_w', 'delta_ln3_b': 'delta_w', 'new_m_ln_in_g': 'new_m', 'new_m_ln_in_b': 'new_m', 'new_m_w_in': 'new_m', 'new_m_conv_w': 'new_m', 'new_m_conv_b': 'new_m', 'new_m_dt_bias': 'new_m', 'new_m_a_log': 'new_m', 'new_m_d_skip': 'new_m', 'new_m_ssd_norm_g': 'new_m', 'new_m_q_norm_g': 'new_m', 'new_m_w_q_up': 'new_m', 'new_m_kv_norm_g': 'new_m', 'new_m_w_kv_up': 'new_m', 'new_m_w_mix_out': 'new_m', 'new_m_ln1_g': 'new_m', 'new_m_ln1_b': 'new_m', 'new_m_w_mem_q': 'new_m', 'new_m_w_mem_k': 'new_m', 'new_m_w_mem_v': 'new_m', 'new_m_w_mem_o': 'new_m', 'new_m_ln2_g': 'new_m', 'new_m_ln2_b': 'new_m', 'new_m_w_up': 'new_m', 'new_m_w_down': 'new_m', 'new_m_ln3_g': 'new_m', 'new_m_ln3_b': 'new_m', 'new_v_ln_in_g': 'new_v', 'new_v_ln_in_b': 'new_v', 'new_v_w_in': 'new_v', 'new_v_conv_w': 'new_v', 'new_v_conv_b': 'new_v', 'new_v_dt_bias': 'new_v', 'new_v_a_log': 'new_v', 'new_v_d_skip': 'new_v', 'new_v_ssd_norm_g': 'new_v', 'new_v_q_norm_g': 'new_v', 'new_v_w_q_up': 'new_v', 'new_v_kv_norm_g': 'new_v', 'new_v_w_kv_up': 'new_v', 'new_v_w_mix_out': 'new_v', 'new_v_ln1_g': 'new_v', 'new_v_ln1_b': 'new_v', 'new_v_w_mem_q': 'new_v', 'new_v_w_mem_k': 'new_v', 'new_v_w_mem_v': 'new_v', 'new_v_w_mem_o': 'new_v', 'new_v_ln2_g': 'new_v', 'new_v_ln2_b': 'new_v', 'new_v_w_up': 'new_v', 'new_v_w_down': 'new_v', 'new_v_ln3_g': 'new_v', 'new_v_ln3_b': 'new_v'}


def _forward(args):
    return _fwd_reference(*[args[k] for k in FWD_PARAMS])


def _output_shape():
    out = _jax.eval_shape(lambda: _forward(_fwd_setup_inputs(0)))
    return out.shape, out.dtype

N_MICROBATCH = 1
ADAM_LR = 0.001
ADAM_B1 = 0.9
ADAM_B2 = 0.999
ADAM_EPS = 1e-08
ADAM_WD = 0.01
ADAM_STEP = 10
PER_EXAMPLE_BATCH_AXIS = {'x': 0, 'mem': 0, 'positions': 0, 'loss_target': 0}
SHARED_INPUTS = []
_WEIGHT_DTYPES = {'ln_in_g': _jnp.float32, 'ln_in_b': _jnp.float32, 'w_in': _jnp.float32, 'conv_w': _jnp.float32, 'conv_b': _jnp.float32, 'dt_bias': _jnp.float32, 'a_log': _jnp.float32, 'd_skip': _jnp.float32, 'ssd_norm_g': _jnp.float32, 'q_norm_g': _jnp.float32, 'w_q_up': _jnp.float32, 'kv_norm_g': _jnp.float32, 'w_kv_up': _jnp.float32, 'w_mix_out': _jnp.float32, 'ln1_g': _jnp.float32, 'ln1_b': _jnp.float32, 'w_mem_q': _jnp.float32, 'w_mem_k': _jnp.float32, 'w_mem_v': _jnp.float32, 'w_mem_o': _jnp.float32, 'ln2_g': _jnp.float32, 'ln2_b': _jnp.float32, 'w_up': _jnp.float32, 'w_down': _jnp.float32, 'ln3_g': _jnp.float32, 'ln3_b': _jnp.float32}
MOMENT_SCALE = {'ln_in_g': 7.941087e-01, 'ln_in_b': 4.484283e-01, 'w_in': 6.278407e-02, 'conv_w': 6.437875e-02, 'conv_b': 9.229164e-02, 'dt_bias': 4.325637e-01, 'a_log': 2.985339e-01, 'd_skip': 2.677477e-01, 'ssd_norm_g': 8.512856e-02, 'q_norm_g': 1.053711e-02, 'w_q_up': 7.470210e-03, 'kv_norm_g': 1.816091e-02, 'w_kv_up': 1.441242e-02, 'w_mix_out': 1.059505e-01, 'ln1_g': 9.624966e-01, 'ln1_b': 4.656568e-01, 'w_mem_q': 5.585395e-03, 'w_mem_k': 5.611291e-03, 'w_mem_v': 1.086960e-02, 'w_mem_o': 1.061582e-02, 'ln2_g': 9.621782e-01, 'ln2_b': 4.669835e-01, 'w_up': 3.652402e-02, 'w_down': 8.249844e-02, 'ln3_g': 3.206526e+01, 'ln3_b': 3.564374e+00}


def _to_microbatches(a, axis):
    t = _jnp.moveaxis(a, axis, 0)
    t = t.reshape((N_MICROBATCH, t.shape[0] // N_MICROBATCH) + t.shape[1:])
    return _jnp.moveaxis(t, 1, axis + 1)


def setup_inputs(seed: int = 0) -> dict:
    inp = _fwd_setup_inputs(seed)
    key = _jax.random.fold_in(_jax.random.key(seed), 7919)
    shape, _ = _output_shape()
    out = dict(inp)
    out["loss_target"] = _jax.random.normal(_jax.random.fold_in(key, 0), shape, _jnp.float32)
    for i, name in enumerate(TWIN_WEIGHTS):
        w = inp[name].astype(_jnp.float32)
        if MOMENT_SCALE is None:
            s = _jnp.sqrt(_jnp.mean(_jnp.square(w)) + 1e-30)
        else:
            s = MOMENT_SCALE[name]
        km, kv = _jax.random.split(_jax.random.fold_in(key, i + 1))
        out[name] = w
        out["m_" + name] = s * _jax.random.normal(km, w.shape, _jnp.float32)
        out["v_" + name] = (s * s) * _jax.random.uniform(kv, w.shape, _jnp.float32, 0.5, 1.5)
    if N_MICROBATCH > 1:
        for name, axis in PER_EXAMPLE_BATCH_AXIS.items():
            out[name] = _to_microbatches(out[name], axis)
    return {'x': out['x'], 'mem': out['mem'], 'positions': out['positions'], 'ln_in_g': out['ln_in_g'], 'ln_in_b': out['ln_in_b'], 'w_in': out['w_in'], 'conv_w': out['conv_w'], 'conv_b': out['conv_b'], 'dt_bias': out['dt_bias'], 'a_log': out['a_log'], 'd_skip': out['d_skip'], 'ssd_norm_g': out['ssd_norm_g'], 'q_norm_g': out['q_norm_g'], 'w_q_up': out['w_q_up'], 'kv_norm_g': out['kv_norm_g'], 'w_kv_up': out['w_kv_up'], 'w_mix_out': out['w_mix_out'], 'ln1_g': out['ln1_g'], 'ln1_b': out['ln1_b'], 'w_mem_q': out['w_mem_q'], 'w_mem_k': out['w_mem_k'], 'w_mem_v': out['w_mem_v'], 'w_mem_o': out['w_mem_o'], 'ln2_g': out['ln2_g'], 'ln2_b': out['ln2_b'], 'w_up': out['w_up'], 'w_down': out['w_down'], 'ln3_g': out['ln3_g'], 'ln3_b': out['ln3_b'], 'loss_target': out['loss_target'], 'm_ln_in_g': out['m_ln_in_g'], 'm_ln_in_b': out['m_ln_in_b'], 'm_w_in': out['m_w_in'], 'm_conv_w': out['m_conv_w'], 'm_conv_b': out['m_conv_b'], 'm_dt_bias': out['m_dt_bias'], 'm_a_log': out['m_a_log'], 'm_d_skip': out['m_d_skip'], 'm_ssd_norm_g': out['m_ssd_norm_g'], 'm_q_norm_g': out['m_q_norm_g'], 'm_w_q_up': out['m_w_q_up'], 'm_kv_norm_g': out['m_kv_norm_g'], 'm_w_kv_up': out['m_w_kv_up'], 'm_w_mix_out': out['m_w_mix_out'], 'm_ln1_g': out['m_ln1_g'], 'm_ln1_b': out['m_ln1_b'], 'm_w_mem_q': out['m_w_mem_q'], 'm_w_mem_k': out['m_w_mem_k'], 'm_w_mem_v': out['m_w_mem_v'], 'm_w_mem_o': out['m_w_mem_o'], 'm_ln2_g': out['m_ln2_g'], 'm_ln2_b': out['m_ln2_b'], 'm_w_up': out['m_w_up'], 'm_w_down': out['m_w_down'], 'm_ln3_g': out['m_ln3_g'], 'm_ln3_b': out['m_ln3_b'], 'v_ln_in_g': out['v_ln_in_g'], 'v_ln_in_b': out['v_ln_in_b'], 'v_w_in': out['v_w_in'], 'v_conv_w': out['v_conv_w'], 'v_conv_b': out['v_conv_b'], 'v_dt_bias': out['v_dt_bias'], 'v_a_log': out['v_a_log'], 'v_d_skip': out['v_d_skip'], 'v_ssd_norm_g': out['v_ssd_norm_g'], 'v_q_norm_g': out['v_q_norm_g'], 'v_w_q_up': out['v_w_q_up'], 'v_kv_norm_g': out['v_kv_norm_g'], 'v_w_kv_up': out['v_w_kv_up'], 'v_w_mix_out': out['v_w_mix_out'], 'v_ln1_g': out['v_ln1_g'], 'v_ln1_b': out['v_ln1_b'], 'v_w_mem_q': out['v_w_mem_q'], 'v_w_mem_k': out['v_w_mem_k'], 'v_w_mem_v': out['v_w_mem_v'], 'v_w_mem_o': out['v_w_mem_o'], 'v_ln2_g': out['v_ln2_g'], 'v_ln2_b': out['v_ln2_b'], 'v_w_up': out['v_w_up'], 'v_w_down': out['v_w_down'], 'v_ln3_g': out['v_ln3_g'], 'v_ln3_b': out['v_ln3_b']}


def _loss(weights, diff, rest, loss_target):
    with _jax.named_scope("forward"):
        args = {**rest, TWIN_DIFF_INPUT: diff, **{k: w.astype(_WEIGHT_DTYPES[k]) for k, w in weights.items()}}
        y = _forward(args)
    with _jax.named_scope("loss_head"):
        err = _jnp.square(y.astype(_jnp.float32) - loss_target)
        return 0.5 * _jnp.sum(_jnp.mean(err, axis=-1)) if err.ndim else 0.5 * err


def _adamw(w, g, m, v):
    m = ADAM_B1 * m + (1.0 - ADAM_B1) * g
    v = ADAM_B2 * v + (1.0 - ADAM_B2) * _jnp.square(g)
    m_hat = m / (1.0 - ADAM_B1 ** ADAM_STEP)
    v_hat = v / (1.0 - ADAM_B2 ** ADAM_STEP)
    delta = -ADAM_LR * (m_hat / (_jnp.sqrt(v_hat) + ADAM_EPS) + ADAM_WD * w)
    return delta, m, v


def reference(x, mem, positions, ln_in_g, ln_in_b, w_in, conv_w, conv_b, dt_bias, a_log, d_skip, ssd_norm_g, q_norm_g, w_q_up, kv_norm_g, w_kv_up, w_mix_out, ln1_g, ln1_b, w_mem_q, w_mem_k, w_mem_v, w_mem_o, ln2_g, ln2_b, w_up, w_down, ln3_g, ln3_b, loss_target, m_ln_in_g, m_ln_in_b, m_w_in, m_conv_w, m_conv_b, m_dt_bias, m_a_log, m_d_skip, m_ssd_norm_g, m_q_norm_g, m_w_q_up, m_kv_norm_g, m_w_kv_up, m_w_mix_out, m_ln1_g, m_ln1_b, m_w_mem_q, m_w_mem_k, m_w_mem_v, m_w_mem_o, m_ln2_g, m_ln2_b, m_w_up, m_w_down, m_ln3_g, m_ln3_b, v_ln_in_g, v_ln_in_b, v_w_in, v_conv_w, v_conv_b, v_dt_bias, v_a_log, v_d_skip, v_ssd_norm_g, v_q_norm_g, v_w_q_up, v_kv_norm_g, v_w_kv_up, v_w_mix_out, v_ln1_g, v_ln1_b, v_w_mem_q, v_w_mem_k, v_w_mem_v, v_w_mem_o, v_ln2_g, v_ln2_b, v_w_up, v_w_down, v_ln3_g, v_ln3_b):
    given = dict(x=x, mem=mem, positions=positions, ln_in_g=ln_in_g, ln_in_b=ln_in_b, w_in=w_in, conv_w=conv_w, conv_b=conv_b, dt_bias=dt_bias, a_log=a_log, d_skip=d_skip, ssd_norm_g=ssd_norm_g, q_norm_g=q_norm_g, w_q_up=w_q_up, kv_norm_g=kv_norm_g, w_kv_up=w_kv_up, w_mix_out=w_mix_out, ln1_g=ln1_g, ln1_b=ln1_b, w_mem_q=w_mem_q, w_mem_k=w_mem_k, w_mem_v=w_mem_v, w_mem_o=w_mem_o, ln2_g=ln2_g, ln2_b=ln2_b, w_up=w_up, w_down=w_down, ln3_g=ln3_g, ln3_b=ln3_b, loss_target=loss_target, m_ln_in_g=m_ln_in_g, m_ln_in_b=m_ln_in_b, m_w_in=m_w_in, m_conv_w=m_conv_w, m_conv_b=m_conv_b, m_dt_bias=m_dt_bias, m_a_log=m_a_log, m_d_skip=m_d_skip, m_ssd_norm_g=m_ssd_norm_g, m_q_norm_g=m_q_norm_g, m_w_q_up=m_w_q_up, m_kv_norm_g=m_kv_norm_g, m_w_kv_up=m_w_kv_up, m_w_mix_out=m_w_mix_out, m_ln1_g=m_ln1_g, m_ln1_b=m_ln1_b, m_w_mem_q=m_w_mem_q, m_w_mem_k=m_w_mem_k, m_w_mem_v=m_w_mem_v, m_w_mem_o=m_w_mem_o, m_ln2_g=m_ln2_g, m_ln2_b=m_ln2_b, m_w_up=m_w_up, m_w_down=m_w_down, m_ln3_g=m_ln3_g, m_ln3_b=m_ln3_b, v_ln_in_g=v_ln_in_g, v_ln_in_b=v_ln_in_b, v_w_in=v_w_in, v_conv_w=v_conv_w, v_conv_b=v_conv_b, v_dt_bias=v_dt_bias, v_a_log=v_a_log, v_d_skip=v_d_skip, v_ssd_norm_g=v_ssd_norm_g, v_q_norm_g=v_q_norm_g, v_w_q_up=v_w_q_up, v_kv_norm_g=v_kv_norm_g, v_w_kv_up=v_w_kv_up, v_w_mix_out=v_w_mix_out, v_ln1_g=v_ln1_g, v_ln1_b=v_ln1_b, v_w_mem_q=v_w_mem_q, v_w_mem_k=v_w_mem_k, v_w_mem_v=v_w_mem_v, v_w_mem_o=v_w_mem_o, v_ln2_g=v_ln2_g, v_ln2_b=v_ln2_b, v_w_up=v_w_up, v_w_down=v_w_down, v_ln3_g=v_ln3_g, v_ln3_b=v_ln3_b)
    weights = {n: given[n] for n in TWIN_WEIGHTS}
    shared = {n: given[n] for n in SHARED_INPUTS}
    per_example = {n: given[n] for n in ['x', 'mem', 'positions']}
    grad_fn = _jax.value_and_grad(_loss, argnums=(0, 1))

    def one_microbatch(ex, loss_target):
        ex = dict(ex)
        diff = ex.pop(TWIN_DIFF_INPUT)
        return grad_fn(weights, diff, {**shared, **ex}, loss_target)

    if N_MICROBATCH == 1:
        loss, (grad_w, grad_x) = one_microbatch(per_example, given["loss_target"])
    else:
        def body(carry, xs):
            loss_sum, grad_sum = carry
            l_k, (gw_k, gx_k) = one_microbatch(xs[0], xs[1])
            with _jax.named_scope("update"):
                return (loss_sum + l_k, _jax.tree.map(_jnp.add, grad_sum, gw_k)), gx_k

        init = (_jnp.zeros((), _jnp.float32), _jax.tree.map(_jnp.zeros_like, weights))
        (loss, grad_w), grad_x = _jax.lax.scan(body, init, (per_example, given["loss_target"]))
    with _jax.named_scope("update"):
        delta_w, new_m, new_v = {}, {}, {}
        for n in TWIN_WEIGHTS:
            delta_w[n], new_m[n], new_v[n] = _adamw(weights[n], grad_w[n], given["m_" + n], given["v_" + n])
    return (loss, grad_x, *[grad_w[n] for n in TWIN_WEIGHTS], *[delta_w[n] for n in TWIN_WEIGHTS],
            *[new_m[n] for n in TWIN_WEIGHTS], *[new_v[n] for n in TWIN_WEIGHTS])
```

```python
import functools
import math

import jax
import jax.numpy as jnp
import numpy as np
from jax import lax
from jax.experimental import pallas as pl
from jax.experimental.pallas import tpu as pltpu

F32 = jnp.float32
BF16 = jnp.bfloat16
MESH = pl.DeviceIdType.MESH

D_MODEL = 1024
SSD_HEADS = 8
SSD_INNER = 512
SSD_CHUNK = 128
SSD_STATE = 128
MLA_HEADS = 8
MLA_NOPE = 64
MLA_ROPE = 32
MLA_QK = 96
MLA_Q_RANK = 384
MLA_KV_RANK = 256
ROPE_THETA = 10000.0
MEM_HEADS = 4
MEM_HEAD_DIM = 256
LN_EPS = 1e-5
RMS_EPS = 1e-6
ALPHA = 2.0 ** 0.25
ADAM_LR = 0.001
ADAM_B1 = 0.9
ADAM_B2 = 0.999
ADAM_EPS = 1e-08
ADAM_WD = 0.01
ADAM_STEP = 10

LANES = 128
IN_W = 2560
SEG_XBC = (0, 1024)
SEG_Z = (1024, 512)
SEG_QLAT = (1536, 384)
SEG_DT = (1920, 128)
SEG_KVLAT = (2048, 256)
SEG_KR = (2304, 128)
VMEM_LIMIT = 56 * 1024 * 1024
ATTN_TILE = 512
ROW_TILE = 256
NEG = -1e30

NN = (((1,), (0,)), ((), ()))
NT = (((1,), (1,)), ((), ()))
TN = (((0,), (0,)), ((), ()))


def _dot(a, b, dims=NN):
    return lax.dot_general(a.astype(BF16), b.astype(BF16), dims, preferred_element_type=F32)


def _dot_exact(a, b):
    return lax.dot_general(a, b, NN, precision=lax.Precision.HIGHEST, preferred_element_type=F32)


def _pick(dim, pref):
    t = min(pref, dim)
    t -= t % LANES
    while t >= LANES:
        if dim % t == 0:
            return t
        t -= LANES
    return dim


def _params(sem):
    return pltpu.CompilerParams(dimension_semantics=sem, vmem_limit_bytes=VMEM_LIMIT)


def _mm(a, b, *, form, name, a_pro=None, tm=1024, tn=512, tk=512):
    if form == "nn":
        (m, k), (_, n) = a.shape, b.shape
    elif form == "nt":
        (m, k), (n, _) = a.shape, b.shape
    else:
        (k, m), (_, n) = a.shape, b.shape
    tm, tn, tk = _pick(m, tm), _pick(n, tn), _pick(k, tk)
    dims = {"nn": NN, "nt": NT, "tn": TN}[form]

    def body(a_ref, b_ref, o_ref):
        @pl.when(pl.program_id(2) == 0)
        def _():
            o_ref[...] = jnp.zeros_like(o_ref)

        av = a_ref[...]
        if a_pro is not None:
            av = a_pro(av)
        o_ref[...] += _dot(av, b_ref[...], dims)

    if form == "tn":
        a_spec = pl.BlockSpec((tk, tm), lambda i, j, kk: (kk, i))
    else:
        a_spec = pl.BlockSpec((tm, tk), lambda i, j, kk: (i, kk))
    if form == "nt":
        b_spec = pl.BlockSpec((tn, tk), lambda i, j, kk: (j, kk))
    else:
        b_spec = pl.BlockSpec((tk, tn), lambda i, j, kk: (kk, j))
    return pl.pallas_call(
        body, name=name, grid=(m // tm, n // tn, k // tk),
        in_specs=[a_spec, b_spec],
        out_specs=pl.BlockSpec((tm, tn), lambda i, j, kk: (i, j)),
        out_shape=jax.ShapeDtypeStruct((m, n), F32),
        compiler_params=_params(("parallel", "parallel", "arbitrary")),
    )(a, b)


class _Ctx:
    def __init__(self, i, n):
        self.i, self.n = i, n


def _rowwise(fn, rows, consts, row_outs, acc_outs=(), *, tr, name, n_rows=None):
    norm = []
    for r in rows:
        kind = "tile"
        if isinstance(r, tuple) and isinstance(r[0], str):
            kind, r = r[0], r[1:]
        row0 = 0
        if isinstance(r, tuple) and len(r) == 4:
            r, row0 = r[:3], r[3]
        arr, col0, width = r if isinstance(r, tuple) else (r, 0, r.shape[1])
        assert col0 % width == 0
        norm.append((kind, arr, col0 // width, width, row0))
    n_rows = n_rows or next(a.shape[0] for k, a, _, _, _ in norm if k == "tile")
    tr = min(tr, n_rows)
    while n_rows % tr:
        tr -= 8
    n = n_rows // tr
    arrs, specs = [], []
    for kind, arr, cb, width, row0 in norm:
        if kind == "tile":
            assert row0 % tr == 0
            specs.append(pl.BlockSpec((tr, width), lambda i, cb=cb, rb=row0 // tr: (i + rb, cb)))
        elif kind == "prev":
            specs.append(pl.BlockSpec((8, width), lambda i, cb=cb: (jnp.maximum(i * (tr // 8) - 1, 0), cb)))
        else:
            specs.append(pl.BlockSpec((8, width), lambda i, cb=cb: (jnp.minimum((i + 1) * (tr // 8), n_rows // 8 - 1), cb)))
        arrs.append(arr)
    for c in consts:
        specs.append(pl.BlockSpec(c.shape, lambda i, nd=c.ndim: (0,) * nd))
        arrs.append(c)
    n_in, n_ro = len(arrs), len(row_outs)
    row_outs = [w if isinstance(w, tuple) else (w, F32) for w in row_outs]
    out_shape = [jax.ShapeDtypeStruct((n_rows, w), dt) for w, dt in row_outs]
    out_specs = [pl.BlockSpec((tr, w), lambda i: (i, 0)) for w, _ in row_outs]
    out_shape += [jax.ShapeDtypeStruct(s, F32) for s in acc_outs]
    out_specs += [pl.BlockSpec(s, lambda i: (0, 0)) for s in acc_outs]

    def body(*refs):
        i = pl.program_id(0)
        vals = [r[...] for r in refs[:n_in]]
        outs = fn(_Ctx(i, n), *vals)
        if not isinstance(outs, (tuple, list)):
            outs = (outs,)
        o_refs = refs[n_in:]
        for o_ref, o in zip(o_refs[:n_ro], outs[:n_ro]):
            o_ref[...] = o.astype(o_ref.dtype)
        if acc_outs:
            @pl.when(i == 0)
            def _():
                for o_ref in o_refs[n_ro:]:
                    o_ref[...] = jnp.zeros_like(o_ref)

            for o_ref, o in zip(o_refs[n_ro:], outs[n_ro:]):
                o_ref[...] += jnp.broadcast_to(o, o_ref.shape)

    res = pl.pallas_call(
        body, name=name, grid=(n,), in_specs=specs, out_specs=out_specs, out_shape=out_shape,
        compiler_params=_params(("arbitrary",)),
    )(*arrs)
    return res


def _sum0(v):
    return jnp.sum(v, axis=0, keepdims=True)


def _mean1(v):
    return jnp.mean(v, axis=-1, keepdims=True)


def _sigmoid(v):
    return 1.0 / (1.0 + jnp.exp(-v))


def _ln_stats(t):
    xc = t - _mean1(t)
    rstd = lax.rsqrt(_mean1(xc * xc) + LN_EPS)
    return xc * rstd, rstd


def _ln_bwd(xhat, rstd, dy, g):
    dxh = dy * g
    dx = rstd * (dxh - _mean1(dxh) - xhat * _mean1(dxh * xhat))
    return dx, _sum0(dy * xhat), _sum0(dy)


def _rms_fwd(v, g):
    return v * lax.rsqrt(_mean1(v * v) + RMS_EPS) * g


def _rms_bwd(v, dy, g):
    rs = lax.rsqrt(_mean1(v * v) + RMS_EPS)
    vh = v * rs
    dyg = dy * g
    return rs * (dyg - vh * _mean1(dyg * vh)), _sum0(dy * vh)


def _lane(shape):
    return lax.broadcasted_iota(jnp.int32, shape, len(shape) - 1)


def _shift_down(u, halo, s, is_first):
    tr = u.shape[0]
    rolled = pltpu.roll(u, s, 0)
    hr = jnp.where(is_first, 0.0, pltpu.roll(halo, s, 0))
    row = lax.broadcasted_iota(jnp.int32, hr.shape, 0)
    top = jnp.where(row < s, hr, rolled[0:8])
    if tr == 8:
        return top
    return jnp.concatenate([top, rolled[8:]], axis=0)


def _shift_up(d, halo, s, is_last):
    tr = d.shape[0]
    rolled = pltpu.roll(d, tr - s, 0)
    hr = jnp.where(is_last, 0.0, pltpu.roll(halo, 8 - s, 0))
    row = lax.broadcasted_iota(jnp.int32, hr.shape, 0)
    bot = jnp.where(row >= 8 - s, hr, rolled[tr - 8:])
    if tr == 8:
        return bot
    return jnp.concatenate([rolled[:tr - 8], bot], axis=0)


def _rope(v, ta, tb, tc):
    return v * ta + pltpu.roll(v, 16, 1) * tb + pltpu.roll(v, LANES - 16, 1) * tc


def _rope_bwd(d, ta, tb, tc):
    return d * ta + pltpu.roll(d * tb, LANES - 16, 1) + pltpu.roll(d * tc, 16, 1)


def _ssd_common(dtv, a_row):
    L = SSD_CHUNK
    a = dtv * a_row
    r = lax.broadcasted_iota(jnp.int32, (L, L), 0)
    c = lax.broadcasted_iota(jnp.int32, (L, L), 1)
    tril = r >= c
    cs = _dot_exact(tril.astype(F32), a)
    cs_t = cs.T
    cs_last = cs[L - 1:L, :]
    return dict(a=a, tril=tril, cs=cs, cs_t=cs_t, ecs=jnp.exp(cs), dte=jnp.exp(cs_last - cs),
                elast=jnp.exp(cs_last))


def _pair_sel(v, h0, lo):
    return jnp.where(lo, v[:, h0:h0 + 1], v[:, h0 + 1:h0 + 2])


def _ssd_pair(cm, h0, cb, xp, dtv, bmat, cmat, hp, lo):
    L = SSD_CHUNK
    x = xp * _pair_sel(dtv, h0, lo)
    lam0 = jnp.exp(jnp.where(cm["tril"], cm["cs"][:, h0:h0 + 1] - cm["cs_t"][h0:h0 + 1, :], NEG))
    lam1 = jnp.exp(jnp.where(cm["tril"], cm["cs"][:, h0 + 1:h0 + 2] - cm["cs_t"][h0 + 1:h0 + 2, :], NEG))
    m0, m1 = cb * lam0, cb * lam1
    ydiag = jnp.where(lo, _dot(m0, x), _dot(m1, x))
    ecs_p = _pair_sel(cm["ecs"], h0, lo)
    dte_p = _pair_sel(cm["dte"], h0, lo)
    yoff = _dot(cmat, hp, NT) * ecs_p
    xd = x * dte_p
    st = _dot(xd, bmat, TN)
    rlo = lax.broadcasted_iota(jnp.int32, (LANES, SSD_STATE), 0) < 64
    decay = jnp.where(rlo, cm["elast"][:, h0:h0 + 1], cm["elast"][:, h0 + 1:h0 + 2])
    h_next = hp * decay + st
    return dict(x=x, lam0=lam0, lam1=lam1, m0=m0, m1=m1, y=ydiag + yoff, yoff=yoff, ecs_p=ecs_p, dte_p=dte_p,
                xd=xd, decay=decay, h_next=h_next)


def _ssd_fwd(xbc, dt, a_row, *, name):
    S = xbc.shape[0]
    L = SSD_CHUNK
    nc = S // L

    def body(xs_ref, bm_ref, cm_ref, dt_ref, a_ref, y_ref, hs_ref, h_scr):
        @pl.when(pl.program_id(0) == 0)
        def _():
            h_scr[...] = jnp.zeros_like(h_scr)

        dtv = dt_ref[...]
        cm = _ssd_common(dtv, a_ref[...])
        lo = _lane((L, LANES)) < 64
        ys = []
        for g in range(2):
            bmat = bm_ref[:, g * 128:(g + 1) * 128]
            cmat = cm_ref[:, g * 128:(g + 1) * 128]
            cb = _dot(cmat, bmat, NT)
            for pr in range(2):
                p4 = 2 * g + pr
                hp = h_scr[p4]
                hs_ref[0, p4 * 128:(p4 + 1) * 128, :] = hp
                t = _ssd_pair(cm, 2 * p4, cb, xs_ref[:, p4 * 128:(p4 + 1) * 128], dtv, bmat, cmat, hp, lo)
                ys.append(t["y"])
                h_scr[p4] = t["h_next"]
        y_ref[...] = jnp.concatenate(ys, axis=1)

    return pl.pallas_call(
        body, name=name, grid=(nc,),
        in_specs=[pl.BlockSpec((L, 512), lambda c: (c, 0)), pl.BlockSpec((L, 256), lambda c: (c, 2)),
                  pl.BlockSpec((L, 256), lambda c: (c, 3)), pl.BlockSpec((L, 128), lambda c: (c, 0)),
                  pl.BlockSpec((1, 128), lambda c: (0, 0))],
        out_specs=[pl.BlockSpec((L, 512), lambda c: (c, 0)), pl.BlockSpec((1, 512, 128), lambda c: (c, 0, 0))],
        out_shape=[jax.ShapeDtypeStruct((S, 512), F32), jax.ShapeDtypeStruct((nc, 512, 128), F32)],
        scratch_shapes=[pltpu.VMEM((4, 128, 128), F32)],
        compiler_params=_params(("arbitrary",)),
    )(xbc, xbc, xbc, dt, a_row)


def _ssd_bwd(xbc, dt, a_row, hs, dy, *, name):
    S = xbc.shape[0]
    L = SSD_CHUNK
    nc = S // L

    def body(xs_ref, bm_ref, cm_ref, dt_ref, a_ref, hs_ref, dy_ref, dxs_ref, dbc_ref, ddt_ref, da_ref, g_scr):
        @pl.when(pl.program_id(0) == 0)
        def _():
            g_scr[...] = jnp.zeros_like(g_scr)
            da_ref[...] = jnp.zeros_like(da_ref)

        dtv = dt_ref[...]
        a_row_v = a_ref[...]
        cm = _ssd_common(dtv, a_row_v)
        lo = _lane((L, LANES)) < 64
        lane_row = _lane((1, LANES))
        ri = lax.broadcasted_iota(jnp.int32, (L, L), 0)
        ci = lax.broadcasted_iota(jnp.int32, (L, L), 1)
        triu = (ri <= ci).astype(F32)
        stril = ri > ci

        def halves(v, mask):
            return (jnp.sum(jnp.where(mask, v, 0.0), axis=1, keepdims=True),
                    jnp.sum(jnp.where(mask, 0.0, v), axis=1, keepdims=True))

        i_all = jnp.zeros((L, LANES), F32)
        yo_all = jnp.zeros((L, LANES), F32)
        w_all = jnp.zeros((L, LANES), F32)
        ddt_x = jnp.zeros((L, LANES), F32)
        e_row = jnp.zeros((1, LANES), F32)
        rlo = lax.broadcasted_iota(jnp.int32, (LANES, SSD_STATE), 0) < 64
        dxs, dbs, dcs = [], [], []
        for g in range(2):
            bmat = bm_ref[:, g * 128:(g + 1) * 128]
            cmat = cm_ref[:, g * 128:(g + 1) * 128]
            cb = _dot(cmat, bmat, NT)
            dcb = jnp.zeros((L, L), F32)
            db = jnp.zeros((L, SSD_STATE), F32)
            dc = jnp.zeros((L, SSD_STATE), F32)
            for pr in range(2):
                p4 = 2 * g + pr
                h0 = 2 * p4
                hp = hs_ref[0, p4 * 128:(p4 + 1) * 128, :]
                xp = xs_ref[:, p4 * 128:(p4 + 1) * 128]
                t = _ssd_pair(cm, h0, cb, xp, dtv, bmat, cmat, hp, lo)
                gst = g_scr[p4]
                dyp = dy_ref[:, p4 * 128:(p4 + 1) * 128]
                dy0 = jnp.where(lo, dyp, 0.0)
                dy1 = dyp - dy0
                bg = _dot(bmat, gst, NT)
                dx = _dot(t["m0"], dy0, TN) + _dot(t["m1"], dy1, TN) + bg * t["dte_p"]
                dm0, dm1 = _dot(dy0, t["x"], NT), _dot(dy1, t["x"], NT)
                dcb = dcb + dm0 * t["lam0"] + dm1 * t["lam1"]
                dye = dyp * t["ecs_p"]
                dc = dc + _dot(dye, hp)
                db = db + _dot(t["xd"], gst)
                i0 = jnp.sum(jnp.where(stril, _dot(triu, dm0 * t["m0"]), 0.0), axis=1, keepdims=True)
                i1 = jnp.sum(jnp.where(stril, _dot(triu, dm1 * t["m1"]), 0.0), axis=1, keepdims=True)
                yo0, yo1 = halves(dyp * t["yoff"], lo)
                w0, w1 = halves(t["xd"] * bg, lo)
                gh = gst * (hp * t["decay"])
                e0 = _sum0(jnp.sum(jnp.where(rlo, gh, 0.0), axis=1, keepdims=True))
                e1 = _sum0(jnp.sum(jnp.where(rlo, 0.0, gh), axis=1, keepdims=True))
                x0, x1 = halves(dx * xp, lo)
                oh0 = (lane_row == h0).astype(F32)
                oh1 = (lane_row == h0 + 1).astype(F32)
                i_all = i_all + i0 * oh0 + i1 * oh1
                yo_all = yo_all + yo0 * oh0 + yo1 * oh1
                w_all = w_all + w0 * oh0 + w1 * oh1
                e_row = e_row + e0 * oh0 + e1 * oh1
                ddt_x = ddt_x + x0 * oh0 + x1 * oh1
                dxs.append(dx * _pair_sel(dtv, h0, lo))
                g_scr[p4] = gst * t["decay"] + _dot(dye, cmat, TN)
            dbs.append(db + _dot(dcb, cmat, TN))
            dcs.append(dc + _dot(dcb, bmat))
        da = i_all + _dot_exact(triu, yo_all) + _dot_exact(stril.astype(F32), w_all) + e_row
        ddt_ref[...] = da * a_row_v + ddt_x
        da_ref[...] += _sum0(da * dtv)
        dxs_ref[...] = jnp.concatenate(dxs, axis=1)
        dbc_ref[...] = jnp.concatenate(dbs + dcs, axis=1)

    rev = lambda c: nc - 1 - c
    return pl.pallas_call(
        body, name=name, grid=(nc,),
        in_specs=[pl.BlockSpec((L, 512), lambda c: (rev(c), 0)), pl.BlockSpec((L, 256), lambda c: (rev(c), 2)),
                  pl.BlockSpec((L, 256), lambda c: (rev(c), 3)), pl.BlockSpec((L, 128), lambda c: (rev(c), 0)),
                  pl.BlockSpec((1, 128), lambda c: (0, 0)), pl.BlockSpec((1, 512, 128), lambda c: (rev(c), 0, 0)),
                  pl.BlockSpec((L, 512), lambda c: (rev(c), 0))],
        out_specs=[pl.BlockSpec((L, 512), lambda c: (rev(c), 0)), pl.BlockSpec((L, 512), lambda c: (rev(c), 0)),
                   pl.BlockSpec((L, 128), lambda c: (rev(c), 0)), pl.BlockSpec((1, 128), lambda c: (0, 0))],
        out_shape=[jax.ShapeDtypeStruct((S, 512), F32), jax.ShapeDtypeStruct((S, 512), F32),
                   jax.ShapeDtypeStruct((S, 128), F32), jax.ShapeDtypeStruct((1, 128), F32)],
        scratch_shapes=[pltpu.VMEM((4, 128, 128), F32)],
        compiler_params=_params(("arbitrary",)),
    )(xbc, xbc, xbc, dt, a_row, hs, dy)


MLA_SCALE = MLA_QK ** -0.5


def _causal_scores(q, k, qi, ki, t):
    s = _dot(q, k, NT) * MLA_SCALE
    row = qi * t + lax.broadcasted_iota(jnp.int32, (t, t), 0)
    col = ki * t + lax.broadcasted_iota(jnp.int32, (t, t), 1)
    return jnp.where(col <= row, s, NEG)


def _mla_fwd(q, k, kv, *, name):
    S = q.shape[0]
    t = min(ATTN_TILE, S)
    nq = S // t

    def body(q_ref, k_ref, v_ref, o_ref, lse_ref, m_scr, l_scr, acc_scr):
        qi, ki = pl.program_id(1), pl.program_id(2)

        @pl.when(ki == 0)
        def _():
            m_scr[...] = jnp.full_like(m_scr, NEG)
            l_scr[...] = jnp.zeros_like(l_scr)
            acc_scr[...] = jnp.zeros_like(acc_scr)

        @pl.when(ki <= qi)
        def _():
            s = _causal_scores(q_ref[...], k_ref[...], qi, ki, t)
            m_old = m_scr[:, 0:1]
            m_new = jnp.maximum(m_old, jnp.max(s, axis=1, keepdims=True))
            p = jnp.exp(s - m_new)
            corr = jnp.exp(m_old - m_new)
            l_scr[...] = jnp.broadcast_to(corr * l_scr[:, 0:1] + jnp.sum(p, axis=1, keepdims=True), l_scr.shape)
            acc_scr[...] = corr * acc_scr[...] + _dot(p, v_ref[...])
            m_scr[...] = jnp.broadcast_to(m_new, m_scr.shape)

        @pl.when(ki == nq - 1)
        def _():
            l = l_scr[:, 0:1]
            o_ref[...] = acc_scr[...] / l
            lse_ref[0] = jnp.broadcast_to(m_scr[:, 0:1] + jnp.log(l), (t, LANES))

    return pl.pallas_call(
        body, name=name, grid=(MLA_HEADS, nq, nq),
        in_specs=[pl.BlockSpec((t, 128), lambda h, qi, ki: (qi, h)),
                  pl.BlockSpec((t, 128), lambda h, qi, ki: (jnp.minimum(ki, qi), h)),
                  pl.BlockSpec((t, 128), lambda h, qi, ki: (jnp.minimum(ki, qi), 2 * h + 1))],
        out_specs=[pl.BlockSpec((t, 128), lambda h, qi, ki: (qi, h)),
                   pl.BlockSpec((1, t, 128), lambda h, qi, ki: (h, qi, 0))],
        out_shape=[jax.ShapeDtypeStruct((S, MLA_HEADS * 128), F32), jax.ShapeDtypeStruct((MLA_HEADS, S, 128), F32)],
        scratch_shapes=[pltpu.VMEM((t, 128), F32), pltpu.VMEM((t, 128), F32), pltpu.VMEM((t, 128), F32)],
        compiler_params=_params(("parallel", "parallel", "arbitrary")),
    )(q, k, kv)


def _mla_bwd_dkv(q, k, kv, o, do, lse, *, name):
    S = q.shape[0]
    t = min(ATTN_TILE, S)
    nq = S // t

    def body(q_ref, k_ref, v_ref, o_ref, do_ref, lse_ref, dkv_ref):
        ki, qi = pl.program_id(1), pl.program_id(2)

        @pl.when(qi == 0)
        def _():
            dkv_ref[...] = jnp.zeros_like(dkv_ref)

        @pl.when(qi >= ki)
        def _():
            qv, dov = q_ref[...], do_ref[...]
            s = _causal_scores(qv, k_ref[...], qi, ki, t)
            p = jnp.exp(s - lse_ref[0][:, 0:1])
            dv = _dot(p, dov, TN)
            dp = _dot(dov, v_ref[...], NT)
            delta = jnp.sum(dov * o_ref[...], axis=1, keepdims=True)
            ds = p * (dp - delta) * MLA_SCALE
            dkv_ref[...] += jnp.concatenate([_dot(ds, qv, TN), dv], axis=1)

    qmap = lambda h, ki, qi: (jnp.maximum(qi, ki), h)
    return pl.pallas_call(
        body, name=name, grid=(MLA_HEADS, nq, nq),
        in_specs=[pl.BlockSpec((t, 128), qmap),
                  pl.BlockSpec((t, 128), lambda h, ki, qi: (ki, h)),
                  pl.BlockSpec((t, 128), lambda h, ki, qi: (ki, 2 * h + 1)),
                  pl.BlockSpec((t, 128), qmap), pl.BlockSpec((t, 128), qmap),
                  pl.BlockSpec((1, t, 128), lambda h, ki, qi: (h, jnp.maximum(qi, ki), 0))],
        out_specs=pl.BlockSpec((t, 256), lambda h, ki, qi: (ki, h)),
        out_shape=jax.ShapeDtypeStruct((S, MLA_HEADS * 256), F32),
        compiler_params=_params(("parallel", "parallel", "arbitrary")),
    )(q, k, kv, o, do, lse)


def _mla_bwd_dq(q, k, kv, o, do, lse, *, name):
    S = q.shape[0]
    t = min(ATTN_TILE, S)
    nq = S // t

    def body(q_ref, k_ref, v_ref, o_ref, do_ref, lse_ref, dq_ref):
        qi, ki = pl.program_id(1), pl.program_id(2)

        @pl.when(ki == 0)
        def _():
            dq_ref[...] = jnp.zeros_like(dq_ref)

        @pl.when(ki <= qi)
        def _():
            dov, kv = do_ref[...], k_ref[...]
            s = _causal_scores(q_ref[...], kv, qi, ki, t)
            p = jnp.exp(s - lse_ref[0][:, 0:1])
            dp = _dot(dov, v_ref[...], NT)
            delta = jnp.sum(dov * o_ref[...], axis=1, keepdims=True)
            ds = p * (dp - delta) * MLA_SCALE
            dq_ref[...] += _dot(ds, kv)

    qmap = lambda h, qi, ki: (qi, h)
    return pl.pallas_call(
        body, name=name, grid=(MLA_HEADS, nq, nq),
        in_specs=[pl.BlockSpec((t, 128), qmap),
                  pl.BlockSpec((t, 128), lambda h, qi, ki: (jnp.minimum(ki, qi), h)),
                  pl.BlockSpec((t, 128), lambda h, qi, ki: (jnp.minimum(ki, qi), 2 * h + 1)),
                  pl.BlockSpec((t, 128), qmap), pl.BlockSpec((t, 128), qmap),
                  pl.BlockSpec((1, t, 128), lambda h, qi, ki: (h, qi, 0))],
        out_specs=pl.BlockSpec((t, 128), qmap),
        out_shape=jax.ShapeDtypeStruct((S, MLA_HEADS * 128), F32),
        compiler_params=_params(("parallel", "parallel", "arbitrary")),
    )(q, k, kv, o, do, lse)


def _fn_ln(ctx, x, g, b):
    xhat, _ = _ln_stats(x)
    return (xhat * g + b,)


def _fn_conv_fwd(ctx, u, up, dtr, w8, cb, dtb):
    first = ctx.i == 0
    y = u * w8[3:4] + cb
    for s in (1, 2, 3):
        y = y + _shift_down(u, up, s, first) * w8[3 - s:4 - s]
    act = y * _sigmoid(y)
    v = dtr + dtb
    e = jnp.exp(-jnp.abs(v))
    one_p = 1.0 + e
    log1p = jnp.where(one_p == 1.0, e, jnp.log(one_p) * e / (one_p - 1.0))
    return y, act, jnp.maximum(v, 0.0) + log1p


def _fn_ssd_post(ctx, y, xs, z, dexp, g):
    yg = (y + xs * dexp) * (z * _sigmoid(z))
    outs = []
    for k in range(2):
        v = yg[:, 256 * k:256 * (k + 1)]
        outs.append(v * lax.rsqrt(_mean1(v * v) + RMS_EPS))
    return (jnp.concatenate(outs, axis=1) * g,)


def _fn_ssd_post_bwd(ctx, dyn, y, xs, z, dexp, g):
    yt = y + xs * dexp
    sig = _sigmoid(z)
    sz = z * sig
    yg = yt * sz
    dyh = dyn * g
    yh, dyg = [], []
    for k in range(2):
        sl = slice(256 * k, 256 * (k + 1))
        v = yg[:, sl]
        rs = lax.rsqrt(_mean1(v * v) + RMS_EPS)
        vh = v * rs
        yh.append(vh)
        dyg.append(rs * (dyh[:, sl] - vh * _mean1(dyh[:, sl] * vh)))
    yh = jnp.concatenate(yh, axis=1)
    dyg = jnp.concatenate(dyg, axis=1)
    dyt = dyg * sz
    dz = dyg * yt * (sig * (1.0 + z * (1.0 - sig)))
    return dyt, dz, dyt * dexp, _sum0(dyt * xs), _sum0(dyn * yh)


def _fn_mla_pre(ctx, ql, kvl, gq, gkv):
    return _rms_fwd(ql, gq), _rms_fwd(kvl, gkv)


def _fn_mla_pre_bwd(ctx, ql, kvl, dqn, dkvn, gq, gkv):
    dql, dgq = _rms_bwd(ql, dqn, gq)
    dkvl, dgkv = _rms_bwd(kvl, dkvn, gkv)
    return dql, dkvl, dgq, dgkv


def _fn_rope(ctx, qp, kv, kr, ta, tb, tc):
    kpe = _rope(kr, ta, tb, tc)
    qs, ks = [], []
    for h in range(MLA_HEADS):
        qs.append(_rope(qp[:, 128 * h:128 * (h + 1)], ta, tb, tc))
        ks.append(kv[:, 256 * h:256 * h + 128] + kpe)
    return jnp.concatenate(qs, axis=1), jnp.concatenate(ks, axis=1)


def _fn_rope_bwd(ctx, dq, dkv, ta, tb, tc):
    qs = []
    ksum = jnp.zeros_like(ta)
    for h in range(MLA_HEADS):
        qs.append(_rope_bwd(dq[:, 128 * h:128 * (h + 1)], ta, tb, tc))
        ksum = ksum + dkv[:, 256 * h:256 * h + 128]
    lane = _lane(ksum.shape)
    dkr = jnp.where((lane >= 64) & (lane < 96), _rope_bwd(ksum, ta, tb, tc), 0.0)
    return jnp.concatenate(qs, axis=1), dkr


MEM_SCALE = MEM_HEAD_DIM ** -0.5


def _mem_probs(qh, kh):
    s = _dot(qh, kh, NT) * MEM_SCALE
    p = jnp.exp(s - jnp.max(s, axis=1, keepdims=True))
    return p / jnp.sum(p, axis=1, keepdims=True)


def _fn_mem_fwd(ctx, q, km, vm):
    outs = []
    for h in range(MEM_HEADS):
        sl = slice(256 * h, 256 * (h + 1))
        outs.append(_dot(_mem_probs(q[:, sl], km[:, sl]), vm[:, sl]))
    return (jnp.concatenate(outs, axis=1),)


def _fn_mem_bwd(ctx, q, do, km, vm):
    dqs, dks, dvs = [], [], []
    for h in range(MEM_HEADS):
        sl = slice(256 * h, 256 * (h + 1))
        p = _mem_probs(q[:, sl], km[:, sl])
        dvs.append(_dot(p, do[:, sl], TN))
        dp = _dot(do[:, sl], vm[:, sl], NT)
        ds = p * (dp - jnp.sum(dp * p, axis=1, keepdims=True)) * MEM_SCALE
        dqs.append(_dot(ds, km[:, sl]))
        dks.append(_dot(ds, q[:, sl], TN))
    return jnp.concatenate(dqs, axis=1), jnp.concatenate(dks, axis=1), jnp.concatenate(dvs, axis=1)


def _fn_res_ln(ctx, h, r, g, b):
    xhat, _ = _ln_stats(ALPHA * h + r)
    return (xhat * g + b,)


def _fn_res_ln_bwd(ctx, h, r, d1, d2, g):
    xhat, rstd = _ln_stats(ALPHA * h + r)
    return _ln_bwd(xhat, rstd, ALPHA * d1 + d2, g)


def _fn_in_ln_bwd(ctx, x, d1, d2, g):
    xhat, rstd = _ln_stats(x)
    return _ln_bwd(xhat, rstd, ALPHA * d1 + d2, g)


def _fn_final(ctx, h2, ff, tgt, g, b):
    xhat, rstd = _ln_stats(ALPHA * h2 + ff)
    e = xhat * g + b - tgt
    loss = 0.5 * _sum0(jnp.sum(e * e, axis=1, keepdims=True)) / D_MODEL
    dx, dg, db = _ln_bwd(xhat, rstd, e / D_MODEL, g)
    return dx, dg, db, loss


def _fn_du(ctx, u, da):
    return (da * 2.0 * jnp.maximum(u, 0.0),)


def _relu2(u):
    r = jnp.maximum(u, 0.0)
    return r * r


def _fn_conv_bwd_a(ctx, y, dxs1, dxs2, dbc, dtr, ddt, dtb):
    sig = _sigmoid(y)
    dact = jnp.concatenate([dxs1 + dxs2, dbc], axis=1)
    dyc = dact * (sig * (1.0 + y * (1.0 - sig)))
    ddtr = ddt * _sigmoid(dtr + dtb)
    return dyc, ddtr, _sum0(dyc), _sum0(ddtr)


def _fn_conv_bwd_b(ctx, d, dn, u, up, w8):
    first, last = ctx.i == 0, ctx.i == ctx.n - 1
    du = d * w8[3:4]
    row = lax.broadcasted_iota(jnp.int32, w8.shape, 0)
    dw = jnp.where(row == 3, _sum0(d * u), 0.0)
    for s in (1, 2, 3):
        du = du + _shift_up(d, dn, s, last) * w8[3 - s:4 - s]
        dw = dw + jnp.where(row == 3 - s, _sum0(d * _shift_down(u, up, s, first)), 0.0)
    return du, dw


def _fn_adam(ctx, w, g, m, v):
    m = ADAM_B1 * m + (1.0 - ADAM_B1) * g
    v = ADAM_B2 * v + (1.0 - ADAM_B2) * (g * g)
    m_hat = m / (1.0 - ADAM_B1 ** ADAM_STEP)
    v_hat = v / (1.0 - ADAM_B2 ** ADAM_STEP)
    return -ADAM_LR * (m_hat / (jnp.sqrt(v_hat) + ADAM_EPS) + ADAM_WD * w), m, v


def _fn_add2(ctx, a, b):
    s = a + b
    return s, s


def _fn_add4(ctx, a, r0, r1, r2):
    return (((a + r0.astype(F32)) + r1.astype(F32)) + r2.astype(F32),)


def _z(r, c, dt):
    return jnp.zeros((r, c), dt)


def _pad_w_in(w):
    r, dt = w.shape[0], w.dtype
    return jnp.concatenate([w[:, 512:1536], w[:, 0:512], w[:, 1544:1928], w[:, 1536:1544], _z(r, 120, dt),
                            w[:, 1928:2184], _z(r, 64, dt), w[:, 2184:2216], _z(r, 32, dt), _z(r, 128, dt)], axis=1)


def _unpad_w_in(d):
    return jnp.concatenate([d[:, 1024:1536], d[:, 0:1024], d[:, 1920:1928], d[:, 1536:1920], d[:, 2048:2304],
                            d[:, 2368:2400]], axis=1)


def _pad_heads(w, width):
    r = w.shape[0]
    w3 = w.reshape(r, MLA_HEADS, width)
    return jnp.pad(w3, ((0, 0), (0, 0), (0, 128 - width))).reshape(r, MLA_HEADS * 128)


def _pad_w_kv(w):
    r = w.shape[0]
    w4 = w.reshape(r, MLA_HEADS, 2, 64)
    return jnp.pad(w4, ((0, 0), (0, 0), (0, 0), (0, 64))).reshape(r, MLA_HEADS * 256)


def _unpad_w_kv(d):
    r = d.shape[0]
    return d.reshape(r, MLA_HEADS, 2, 128)[:, :, :, :64].reshape(r, MLA_HEADS * 128)


def _pad_w_mix(w):
    wo = jnp.pad(w[512:1024].reshape(MLA_HEADS, 64, D_MODEL), ((0, 0), (0, 64), (0, 0))).reshape(1024, D_MODEL)
    return jnp.concatenate([wo, w[0:512]], axis=0)


def _unpad_w_mix(d):
    do = d[:1024].reshape(MLA_HEADS, 128, D_MODEL)[:, :64].reshape(512, D_MODEL)
    return jnp.concatenate([d[1024:1536], do], axis=0)


def _row(v, width=None):
    v = v.reshape(1, -1).astype(F32)
    if width is not None and v.shape[1] < width:
        v = jnp.pad(v, ((0, 0), (0, width - v.shape[1])))
    return v


def _local_step(x, mem, positions, target, W, P):
    S = x.shape[0]
    tr = ROW_TILE
    w_in_p = _pad_w_in(W["w_in"])
    w_q_p = _pad_heads(W["w_q_up"], MLA_QK)
    w_kv_p = _pad_w_kv(W["w_kv_up"])
    w_mix_p = _pad_w_mix(W["w_mix_out"])
    conv_w8 = jnp.pad(P["conv_w"].astype(F32), ((0, 4), (0, 0)))
    conv_b = _row(P["conv_b"])
    dt_b = _row(P["dt_bias"], 128)
    a_head = -jnp.exp(P["a_log"].reshape(-1).astype(F32))
    a_row = _row(a_head, 128)
    dexp = jnp.repeat(P["d_skip"].reshape(-1).astype(F32), 64).reshape(1, 512)
    g_ssd, g_q, g_kv = _row(P["ssd_norm_g"]), _row(P["q_norm_g"]), _row(P["kv_norm_g"])
    g_in, b_in = _row(P["ln_in_g"]), _row(P["ln_in_b"])
    g1, b1, g2, b2, g3, b3 = (_row(P[k]) for k in ("ln1_g", "ln1_b", "ln2_g", "ln2_b", "ln3_g", "ln3_b"))

    half = MLA_ROPE // 2
    inv_freq = jnp.power(ROPE_THETA, -jnp.arange(half, dtype=F32) / half)
    ang = positions.reshape(S, 1).astype(F32) * inv_freq
    cos, sin = jnp.cos(ang), jnp.sin(ang)
    zc = lambda n: jnp.zeros((S, n), F32)
    rope_a = jnp.concatenate([jnp.ones((S, 64), F32), cos, cos, zc(32)], axis=1)
    rope_b = jnp.concatenate([zc(80), sin, zc(32)], axis=1)
    rope_c = jnp.concatenate([zc(64), -sin, zc(48)], axis=1)

    (h0,) = _rowwise(_fn_ln, [x], [g_in, b_in], [D_MODEL], tr=tr, name="ln_in")
    proj = _mm(h0, w_in_p, form="nn", name="mm_in")
    conv_y, xbc, dt = _rowwise(
        _fn_conv_fwd, [(proj,) + SEG_XBC, ("prev", proj) + SEG_XBC, (proj,) + SEG_DT], [conv_w8, conv_b, dt_b],
        [1024, 1024, 128], tr=tr, name="conv_fwd")
    y_ssd, hs = _ssd_fwd(xbc, dt, a_row, name="ssd_fwd")
    (y_n,) = _rowwise(_fn_ssd_post, [y_ssd, (xbc, 0, 512), (proj,) + SEG_Z], [dexp, g_ssd], [512], tr=tr,
                      name="ssd_post")
    q_n, kv_n = _rowwise(_fn_mla_pre, [(proj,) + SEG_QLAT, (proj,) + SEG_KVLAT], [g_q, g_kv], [384, 256], tr=tr,
                         name="mla_pre")
    qp = _mm(q_n, w_q_p, form="nn", name="mm_q_up")
    kvp = _mm(kv_n, w_kv_p, form="nn", name="mm_kv_up")
    q_rot, k_full = _rowwise(_fn_rope, [qp, kvp, (proj,) + SEG_KR, rope_a, rope_b, rope_c], [],
                             [1024, 1024], tr=tr, name="rope")
    o_att, lse = _mla_fwd(q_rot, k_full, kvp, name="mla_fwd")
    cat = jnp.concatenate([o_att, y_n], axis=1)
    mix = _mm(cat, w_mix_p, form="nn", name="mm_mix")
    (h1,) = _rowwise(_fn_res_ln, [h0, mix], [g1, b1], [D_MODEL], tr=tr, name="ln1")
    qm = _mm(h1, W["w_mem_q"], form="nn", name="mm_mem_q")
    km = _mm(mem, W["w_mem_k"], form="nn", name="mm_mem_k")
    vm = _mm(mem, W["w_mem_v"], form="nn", name="mm_mem_v")
    (om,) = _rowwise(_fn_mem_fwd, [qm], [km, vm], [D_MODEL], tr=tr, name="mem_fwd")
    xa = _mm(om, W["w_mem_o"], form="nn", name="mm_mem_o")
    (h2,) = _rowwise(_fn_res_ln, [h1, xa], [g2, b2], [D_MODEL], tr=tr, name="ln2")
    u = _mm(h2, W["w_up"], form="nn", name="mm_up")
    ff = _mm(u, W["w_down"], form="nn", a_pro=_relu2, name="mm_down")

    dt3, dg3, db3, loss = _rowwise(_fn_final, [h2, ff, target], [g3, b3], [D_MODEL],
                                   [(1, D_MODEL), (1, D_MODEL), (1, 128)], tr=tr, name="ln3_loss")
    da = _mm(dt3, W["w_down"], form="nt", name="mm_down_dx")
    dw_down = _mm(u, dt3, form="tn", a_pro=_relu2, name="mm_down_dw")
    (du,) = _rowwise(_fn_du, [u, da], [], [4 * D_MODEL], tr=128, name="mlp_du")
    dw_up = _mm(h2, du, form="tn", name="mm_up_dw")
    dh2 = _mm(du, W["w_up"], form="nt", name="mm_up_dx")
    dt2, dg2, db2 = _rowwise(_fn_res_ln_bwd, [h1, xa, dt3, dh2], [g2], [D_MODEL], [(1, D_MODEL)] * 2, tr=tr,
                             name="ln2_bwd")
    dom = _mm(dt2, W["w_mem_o"], form="nt", name="mm_mem_o_dx")
    dw_mem_o = _mm(om, dt2, form="tn", name="mm_mem_o_dw")
    dqm, dkm, dvm = _rowwise(_fn_mem_bwd, [qm, dom], [km, vm], [D_MODEL], [(256, D_MODEL)] * 2, tr=tr,
                             name="mem_bwd")
    dw_mem_q = _mm(h1, dqm, form="tn", name="mm_mem_q_dw")
    dw_mem_k = _mm(mem, dkm, form="tn", name="mm_mem_k_dw")
    dw_mem_v = _mm(mem, dvm, form="tn", name="mm_mem_v_dw")
    dh1 = _mm(dqm, W["w_mem_q"], form="nt", name="mm_mem_q_dx")
    dt1, dg1, db1 = _rowwise(_fn_res_ln_bwd, [h0, mix, dt2, dh1], [g1], [D_MODEL], [(1, D_MODEL)] * 2, tr=tr,
                             name="ln1_bwd")
    dcat = _mm(dt1, w_mix_p, form="nt", name="mm_mix_dx")
    dw_mix_p = _mm(cat, dt1, form="tn", name="mm_mix_dw")
    dkv_att = _mla_bwd_dkv(q_rot, k_full, kvp, o_att, dcat, lse, name="mla_bwd_dkv")
    dq_rot = _mla_bwd_dq(q_rot, k_full, kvp, o_att, dcat, lse, name="mla_bwd_dq")
    dqp, dkr = _rowwise(_fn_rope_bwd, [dq_rot, dkv_att, rope_a, rope_b, rope_c], [], [1024, 128], tr=tr,
                        name="rope_bwd")
    dw_q_p = _mm(q_n, dqp, form="tn", name="mm_q_up_dw")
    dq_n = _mm(dqp, w_q_p, form="nt", name="mm_q_up_dx")
    dw_kv_p = _mm(kv_n, dkv_att, form="tn", name="mm_kv_up_dw")
    dkv_n = _mm(dkv_att, w_kv_p, form="nt", name="mm_kv_up_dx")
    dq_lat, dkv_lat, dg_q, dg_kv = _rowwise(
        _fn_mla_pre_bwd, [(proj,) + SEG_QLAT, (proj,) + SEG_KVLAT, dq_n, dkv_n], [g_q, g_kv], [384, 256],
        [(1, 384), (1, 256)], tr=tr, name="mla_pre_bwd")
    dy_ssd, dz, dxs_skip, ddexp, dg_ssd = _rowwise(
        _fn_ssd_post_bwd, [(dcat, 1024, 512), y_ssd, (xbc, 0, 512), (proj,) + SEG_Z], [dexp, g_ssd],
        [512, 512, 512], [(1, 512)] * 2, tr=tr, name="ssd_post_bwd")
    dxs, dbc, ddt, da_head = _ssd_bwd(xbc, dt, a_row, hs, dy_ssd, name="ssd_bwd")
    dyc, ddtr, dconv_b, ddt_b = _rowwise(
        _fn_conv_bwd_a, [conv_y, dxs, dxs_skip, dbc, (proj,) + SEG_DT, ddt], [dt_b], [1024, 128],
        [(1, 1024), (1, 128)], tr=tr, name="conv_bwd_a")
    dxbc, dconv_w8 = _rowwise(
        _fn_conv_bwd_b, [dyc, ("next", dyc, 0, 1024), (proj,) + SEG_XBC, ("prev", proj) + SEG_XBC], [conv_w8], [1024],
        [(8, 1024)], tr=tr, name="conv_bwd_b")
    dproj = jnp.concatenate([dxbc, dz, dq_lat, ddtr, dkv_lat, dkr, jnp.zeros((S, 128), F32)], axis=1)
    dw_in_p = _mm(h0, dproj, form="tn", name="mm_in_dw")
    dh0 = _mm(dproj, w_in_p, form="nt", name="mm_in_dx")
    grad_x, dg_in, db_in = _rowwise(_fn_in_ln_bwd, [x, dt1, dh0], [g_in], [D_MODEL], [(1, D_MODEL)] * 2, tr=tr,
                                    name="ln_in_bwd")

    big = {
        "w_in": _unpad_w_in(dw_in_p),
        "w_q_up": dw_q_p.reshape(384, MLA_HEADS, 128)[:, :, :MLA_QK].reshape(384, MLA_HEADS * MLA_QK),
        "w_kv_up": _unpad_w_kv(dw_kv_p),
        "w_mix_out": _unpad_w_mix(dw_mix_p),
        "w_mem_q": dw_mem_q, "w_mem_k": dw_mem_k, "w_mem_v": dw_mem_v, "w_mem_o": dw_mem_o,
        "w_up": dw_up, "w_down": dw_down,
        "conv_w": dconv_w8[0:4],
    }
    small = {
        "ln_in_g": dg_in, "ln_in_b": db_in, "conv_b": dconv_b, "dt_bias": ddt_b[:, :8],
        "a_log": da_head[:, :8] * a_head.reshape(1, 8),
        "d_skip": ddexp.reshape(8, 64).sum(axis=1).reshape(1, 8),
        "ssd_norm_g": dg_ssd, "q_norm_g": dg_q, "kv_norm_g": dg_kv,
        "ln1_g": dg1, "ln1_b": db1, "ln2_g": dg2, "ln2_b": db2, "ln3_g": dg3, "ln3_b": db3,
    }
    return loss[0, 0], grad_x, big, small


BIG = {
    "w_in": (1024, 2216, 1), "w_q_up": (384, 768, 1), "w_kv_up": (256, 1024, 1), "w_mix_out": (1024, 1024, 0),
    "w_mem_q": (1024, 1024, 0), "w_mem_k": (1024, 1024, 0), "w_mem_v": (1024, 1024, 0), "w_mem_o": (1024, 1024, 0),
    "w_up": (1024, 4096, 1), "w_down": (4096, 1024, 0), "conv_w": (4, 1024, 1),
}
BIG_ORDER = list(BIG)
SMALL_ORDER = ["ln_in_g", "ln_in_b", "conv_b", "dt_bias", "a_log", "d_skip", "ssd_norm_g", "q_norm_g", "kv_norm_g",
               "ln1_g", "ln1_b", "ln2_g", "ln2_b", "ln3_g", "ln3_b"]
N_SHARD = 4
PACK_COLS = 1024
PACK_ROWS = 4032
HALF_ROWS = PACK_ROWS // 2


def _shard_shape(name):
    r, c, ax = BIG[name]
    return (r // N_SHARD, c) if ax == 0 else (r, c // N_SHARD)


def _split_shards(name, full):
    r, c, ax = BIG[name]
    if ax == 0:
        return full.reshape(N_SHARD, -1)
    return full.reshape(r, N_SHARD, c // N_SHARD).transpose(1, 0, 2).reshape(N_SHARD, -1)


def _join_shards(name, parts):
    r, c, ax = BIG[name]
    if ax == 0:
        return parts.reshape(r, c)
    return parts.reshape(N_SHARD, r, c // N_SHARD).transpose(1, 0, 2).reshape(r, c)


HBM = pl.BlockSpec(memory_space=pl.ANY)


def _place():
    x, y, c = lax.axis_index("x"), lax.axis_index("y"), lax.axis_index("c")
    chips = [(1 - x, y), (x, 1 - y), (1 - x, 1 - y)]
    return x, y, c, chips


def _gather_weights(wp):
    R, C = wp.shape
    H = R // 2

    def body(w_ref, out_ref, send_sems, recv_sems, local_sem):
        x, y, c, chips = _place()
        sib = (x, y, 1 - c)

        def half(k, hc):
            return out_ref.at[k, pl.ds(hc * H, H), :]

        def copy(j, src, dst, to):
            return pltpu.make_async_remote_copy(src_ref=src, dst_ref=dst, send_sem=send_sems.at[j],
                                                recv_sem=recv_sems.at[j], device_id=to, device_id_type=MESH)

        me = 2 * x + y
        mine = pltpu.make_async_copy(w_ref, out_ref.at[me], local_sem)
        mine.start()
        sends = [copy(j, w_ref.at[pl.ds(c * H, H), :], half(me, c), (px, py, c)) for j, (px, py) in enumerate(chips)]
        for cp in sends:
            cp.start()
        fwds = []
        for j, (px, py) in enumerate(chips):
            k = 2 * px + py
            copy(j, half(k, c), half(k, c), (px, py, c)).wait_recv()
            f = copy(3 + j, half(k, c), half(k, c), sib)
            f.start()
            fwds.append(f)
        for j, (px, py) in enumerate(chips):
            k = 2 * px + py
            copy(3 + j, half(k, 1 - c), half(k, 1 - c), sib).wait_recv()
        for cp in sends + fwds:
            cp.wait_send()
        mine.wait()

    return pl.pallas_call(
        body, name="gather_weights", in_specs=[HBM], out_specs=HBM,
        out_shape=jax.ShapeDtypeStruct((N_SHARD, R, C), wp.dtype),
        scratch_shapes=[pltpu.SemaphoreType.DMA((6,)), pltpu.SemaphoreType.DMA((6,)), pltpu.SemaphoreType.DMA],
    )(wp)


def _pair_exchange(gp):
    n, R, C = gp.shape
    H = R // 2

    def body(g_ref, mine_ref, theirs_ref, send_sem, recv_sem, local_sem):
        x, y, c, _ = _place()
        keep = pltpu.make_async_copy(g_ref.at[:, pl.ds(c * H, H), :], mine_ref, local_sem)
        keep.start()
        swap = pltpu.make_async_remote_copy(src_ref=g_ref.at[:, pl.ds((1 - c) * H, H), :], dst_ref=theirs_ref,
                                            send_sem=send_sem, recv_sem=recv_sem, device_id=(x, y, 1 - c),
                                            device_id_type=MESH)
        swap.start()
        swap.wait()
        keep.wait()

    return pl.pallas_call(
        body, name="pair_exchange", in_specs=[HBM], out_specs=[HBM, HBM],
        out_shape=[jax.ShapeDtypeStruct((n, H, C), gp.dtype)] * 2,
        scratch_shapes=[pltpu.SemaphoreType.DMA, pltpu.SemaphoreType.DMA, pltpu.SemaphoreType.DMA],
    )(gp)


def _chip_exchange(pb, pf):
    n, H, C = pb.shape

    def body(pb_ref, pf_ref, own_ref, got_ref, send_sems, recv_sems, local_sem):
        x, y, c, chips = _place()
        keep = pltpu.make_async_copy(pf_ref.at[2 * x + y], own_ref, local_sem)
        keep.start()
        sends = []
        for j, (px, py) in enumerate(chips):
            cp = pltpu.make_async_remote_copy(src_ref=pb_ref.at[2 * px + py], dst_ref=got_ref.at[j],
                                              send_sem=send_sems.at[j], recv_sem=recv_sems.at[j],
                                              device_id=(px, py, c), device_id_type=MESH)
            cp.start()
            sends.append(cp)
        for cp in sends:
            cp.wait()
        keep.wait()

    return pl.pallas_call(
        body, name="chip_exchange", in_specs=[HBM, HBM], out_specs=[HBM, HBM],
        out_shape=[jax.ShapeDtypeStruct((H, C), F32), jax.ShapeDtypeStruct((3, H, C), BF16)],
        scratch_shapes=[pltpu.SemaphoreType.DMA((3,)), pltpu.SemaphoreType.DMA((3,)), pltpu.SemaphoreType.DMA],
    )(pb, pf)


def _pair_join(q):
    H, C = q.shape

    def body(q_ref, out_ref, send_sem, recv_sem, local_sem):
        x, y, c, _ = _place()
        keep = pltpu.make_async_copy(q_ref, out_ref.at[pl.ds(c * H, H), :], local_sem)
        keep.start()
        push = pltpu.make_async_remote_copy(src_ref=q_ref, dst_ref=out_ref.at[pl.ds(c * H, H), :],
                                            send_sem=send_sem, recv_sem=recv_sem, device_id=(x, y, 1 - c),
                                            device_id_type=MESH)
        push.start()
        push.wait()
        keep.wait()

    return pl.pallas_call(
        body, name="pair_join", in_specs=[HBM], out_specs=HBM,
        out_shape=jax.ShapeDtypeStruct((2 * H, C), F32),
        scratch_shapes=[pltpu.SemaphoreType.DMA, pltpu.SemaphoreType.DMA, pltpu.SemaphoreType.DMA],
    )(q)


N_DEV = 8


def _small_all_reduce(g):
    r, cdim = g.shape

    def body(g_ref, out_ref, buf, send_sems, recv_sems):
        x, y, c, _ = _place()
        me = 4 * x + 2 * y + c
        buf[me] = g_ref[...]
        copies = []
        for d in range(1, N_DEV):
            to = me ^ d
            cp = pltpu.make_async_remote_copy(src_ref=g_ref, dst_ref=buf.at[me], send_sem=send_sems.at[d - 1],
                                              recv_sem=recv_sems.at[d - 1],
                                              device_id=(to // 4, (to // 2) % 2, to % 2), device_id_type=MESH)
            cp.start()
            copies.append(cp)
        for cp in copies:
            cp.wait()
        acc = buf[0]
        for d in range(1, N_DEV):
            acc = acc + buf[d]
        out_ref[...] = acc

    return pl.pallas_call(
        body, name="small_all_reduce",
        in_specs=[pl.BlockSpec(memory_space=pltpu.VMEM)], out_specs=pl.BlockSpec(memory_space=pltpu.VMEM),
        out_shape=jax.ShapeDtypeStruct((r, cdim), F32),
        scratch_shapes=[pltpu.VMEM((N_DEV, r, cdim), F32), pltpu.SemaphoreType.DMA((N_DEV - 1,)),
                        pltpu.SemaphoreType.DMA((N_DEV - 1,))],
    )(g)


def _adam(w, g, m, v, name):
    shape = w.shape
    w2, g2, m2, v2 = (t.reshape(-1, shape[-1]) for t in (w, g, m, v))
    d, mn, vn = _rowwise(_fn_adam, [w2, g2, m2, v2], [], [shape[-1]] * 3, tr=256, name=name)
    return d.reshape(shape), mn.reshape(shape), vn.reshape(shape)


def kernel(x, mem, positions, ln_in_g, ln_in_b, w_in, conv_w, conv_b, dt_bias, a_log, d_skip, ssd_norm_g, q_norm_g, w_q_up, kv_norm_g, w_kv_up, w_mix_out, ln1_g, ln1_b, w_mem_q, w_mem_k, w_mem_v, w_mem_o, ln2_g, ln2_b, w_up, w_down, ln3_g, ln3_b, loss_target, m_ln_in_g, m_ln_in_b, m_w_in, m_conv_w, m_conv_b, m_dt_bias, m_a_log, m_d_skip, m_ssd_norm_g, m_q_norm_g, m_w_q_up, m_kv_norm_g, m_w_kv_up, m_w_mix_out, m_ln1_g, m_ln1_b, m_w_mem_q, m_w_mem_k, m_w_mem_v, m_w_mem_o, m_ln2_g, m_ln2_b, m_w_up, m_w_down, m_ln3_g, m_ln3_b, v_ln_in_g, v_ln_in_b, v_w_in, v_conv_w, v_conv_b, v_dt_bias, v_a_log, v_d_skip, v_ssd_norm_g, v_q_norm_g, v_w_q_up, v_kv_norm_g, v_w_kv_up, v_w_mix_out, v_ln1_g, v_ln1_b, v_w_mem_q, v_w_mem_k, v_w_mem_v, v_w_mem_o, v_ln2_g, v_ln2_b, v_w_up, v_w_down, v_ln3_g, v_ln3_b):
    args = dict(locals())
    weights = BIG_ORDER + SMALL_ORDER

    flat = []
    for n in BIG_ORDER:
        s = args[n].reshape(-1)
        if n == "conv_w":
            flat.append(lax.bitcast_convert_type(s.astype(F32), BF16).reshape(-1))
        else:
            flat.append(s.astype(BF16))
    flat = jnp.concatenate(flat)
    wp = jnp.pad(flat, (0, PACK_ROWS * PACK_COLS - flat.shape[0])).reshape(PACK_ROWS, PACK_COLS)
    gathered = _gather_weights(wp).reshape(N_SHARD, -1)
    W, off = {}, 0
    for n in BIG_ORDER:
        sr, sc = _shard_shape(n)
        cnt = sr * sc
        if n == "conv_w":
            part = lax.bitcast_convert_type(gathered[:, off:off + 2 * cnt].reshape(N_SHARD, cnt, 2), F32)
            off += 2 * cnt
        else:
            part = gathered[:, off:off + cnt]
            off += cnt
        W[n] = _join_shards(n, part)
    P = {n: args[n] for n in SMALL_ORDER}
    P["conv_w"] = W.pop("conv_w")

    loss, grad_x, gbig, gsmall = _local_step(x[0], mem[0], positions[0], loss_target[0], W, P)
    loss = lax.psum(loss, ("x", "y", "c"))

    gflat = jnp.concatenate([_split_shards(n, gbig[n]) for n in BIG_ORDER], axis=1)
    gp = jnp.pad(gflat, ((0, 0), (0, PACK_ROWS * PACK_COLS - gflat.shape[1]))).reshape(N_SHARD, PACK_ROWS, PACK_COLS)
    mine, theirs = _pair_exchange(gp)
    pf, pb = _rowwise(_fn_add2, [mine.reshape(-1, PACK_COLS), theirs.reshape(-1, PACK_COLS)], [],
                      [PACK_COLS, (PACK_COLS, BF16)], tr=288, name="pair_sum")
    pf = pf.reshape(N_SHARD, HALF_ROWS, PACK_COLS)
    pb = pb.reshape(N_SHARD, HALF_ROWS, PACK_COLS)
    own, got = _chip_exchange(pb, pf)
    got = got.reshape(3 * HALF_ROWS, PACK_COLS)
    (q,) = _rowwise(_fn_add4, [own] + [(got, 0, PACK_COLS, j * HALF_ROWS) for j in range(3)], [], [PACK_COLS],
                    tr=288, name="chip_sum", n_rows=HALF_ROWS)
    red = _pair_join(q).reshape(-1)

    gs = jnp.concatenate([_row(gsmall[n], PACK_COLS) for n in SMALL_ORDER] + [jnp.zeros((1, PACK_COLS), F32)], axis=0)
    gs = _small_all_reduce(gs)

    grads, deltas, new_m, new_v = {}, {}, {}, {}
    off = 0
    for n in BIG_ORDER:
        sr, sc = _shard_shape(n)
        g = red[off:off + sr * sc].reshape(args[n].shape)
        off += sr * sc
        grads[n] = g
        deltas[n], new_m[n], new_v[n] = _adam(args[n], g, args["m_" + n], args["v_" + n], "adam_" + n)
    pack = lambda pre: jnp.concatenate([_row(args[pre + n], PACK_COLS) for n in SMALL_ORDER]
                                       + [jnp.zeros((1, PACK_COLS), F32)], axis=0)
    ds, ms, vs = _rowwise(_fn_adam, [pack(""), gs, pack("m_"), pack("v_")], [], [PACK_COLS] * 3, tr=16,
                          name="adam_small")
    for i, n in enumerate(SMALL_ORDER):
        cnt = args[n].size
        take = lambda t: t[i, :cnt].reshape(args[n].shape)
        grads[n], deltas[n], new_m[n], new_v[n] = take(gs), take(ds), take(ms), take(vs)

    order = ["ln_in_g", "ln_in_b", "w_in", "conv_w", "conv_b", "dt_bias", "a_log", "d_skip", "ssd_norm_g",
             "q_norm_g", "w_q_up", "kv_norm_g", "w_kv_up", "w_mix_out", "ln1_g", "ln1_b", "w_mem_q", "w_mem_k",
             "w_mem_v", "w_mem_o", "ln2_g", "ln2_b", "w_up", "w_down", "ln3_g", "ln3_b"]
    assert sorted(order) == sorted(weights)
    return (loss, grad_x[None], *[grads[n] for n in order], *[deltas[n] for n in order],
            *[new_m[n] for n in order], *[new_v[n] for n in order])
```

```python
import functools
import math

import jax
import jax.numpy as jnp
import numpy as np
from jax import lax
from jax.experimental import pallas as pl
from jax.experimental.pallas import tpu as pltpu

F32 = jnp.float32
BF16 = jnp.bfloat16
MESH = pl.DeviceIdType.MESH

D_MODEL = 1024
SSD_HEADS = 8
SSD_INNER = 512
SSD_CHUNK = 128
SSD_STATE = 128
MLA_HEADS = 8
MLA_NOPE = 64
MLA_ROPE = 32
MLA_QK = 96
MLA_Q_RANK = 384
MLA_KV_RANK = 256
ROPE_THETA = 10000.0
MEM_HEADS = 4
MEM_HEAD_DIM = 256
LN_EPS = 1e-5
RMS_EPS = 1e-6
ALPHA = 2.0 ** 0.25
ADAM_LR = 0.001
ADAM_B1 = 0.9
ADAM_B2 = 0.999
ADAM_EPS = 1e-08
ADAM_WD = 0.01
ADAM_STEP = 10

LANES = 128
IN_W = 2560
SEG_XBC = (0, 1024)
SEG_Z = (1024, 512)
SEG_QLAT = (1536, 384)
SEG_DT = (1920, 128)
SEG_KVLAT = (2048, 256)
SEG_KR = (2304, 128)
VMEM_LIMIT = 56 * 1024 * 1024
ATTN_TILE = 512
ROW_TILE = 256
NEG = -1e30

NN = (((1,), (0,)), ((), ()))
NT = (((1,), (1,)), ((), ()))
TN = (((0,), (0,)), ((), ()))


def _dot(a, b, dims=NN):
    return lax.dot_general(a.astype(BF16), b.astype(BF16), dims, preferred_element_type=F32)


def _dot_exact(a, b):
    return lax.dot_general(a, b, NN, precision=lax.Precision.HIGHEST, preferred_element_type=F32)


def _pick(dim, pref):
    t = min(pref, dim)
    t -= t % LANES
    while t >= LANES:
        if dim % t == 0:
            return t
        t -= LANES
    return dim


def _params(sem):
    return pltpu.CompilerParams(dimension_semantics=sem, vmem_limit_bytes=VMEM_LIMIT)


def _mm(a, b, *, form, name, a_pro=None, tm=1024, tn=512, tk=512):
    if form == "nn":
        (m, k), (_, n) = a.shape, b.shape
    elif form == "nt":
        (m, k), (n, _) = a.shape, b.shape
    else:
        (k, m), (_, n) = a.shape, b.shape
    tm, tn, tk = _pick(m, tm), _pick(n, tn), _pick(k, tk)
    dims = {"nn": NN, "nt": NT, "tn": TN}[form]

    def body(a_ref, b_ref, o_ref):
        @pl.when(pl.program_id(2) == 0)
        def _():
            o_ref[...] = jnp.zeros_like(o_ref)

        av = a_ref[...]
        if a_pro is not None:
            av = a_pro(av)
        o_ref[...] += _dot(av, b_ref[...], dims)

    if form == "tn":
        a_spec = pl.BlockSpec((tk, tm), lambda i, j, kk: (kk, i))
    else:
        a_spec = pl.BlockSpec((tm, tk), lambda i, j, kk: (i, kk))
    if form == "nt":
        b_spec = pl.BlockSpec((tn, tk), lambda i, j, kk: (j, kk))
    else:
        b_spec = pl.BlockSpec((tk, tn), lambda i, j, kk: (kk, j))
    return pl.pallas_call(
        body, name=name, grid=(m // tm, n // tn, k // tk),
        in_specs=[a_spec, b_spec],
        out_specs=pl.BlockSpec((tm, tn), lambda i, j, kk: (i, j)),
        out_shape=jax.ShapeDtypeStruct((m, n), F32),
        compiler_params=_params(("parallel", "parallel", "arbitrary")),
    )(a, b)


class _Ctx:
    def __init__(self, i, n):
        self.i, self.n = i, n


def _rowwise(fn, rows, consts, row_outs, acc_outs=(), *, tr, name, n_rows=None):
    norm = []
    for r in rows:
        kind = "tile"
        if isinstance(r, tuple) and isinstance(r[0], str):
            kind, r = r[0], r[1:]
        row0 = 0
        if isinstance(r, tuple) and len(r) == 4:
            r, row0 = r[:3], r[3]
        arr, col0, width = r if isinstance(r, tuple) else (r, 0, r.shape[1])
        assert col0 % width == 0
        norm.append((kind, arr, col0 // width, width, row0))
    n_rows = n_rows or next(a.shape[0] for k, a, _, _, _ in norm if k == "tile")
    tr = min(tr, n_rows)
    while n_rows % tr:
        tr -= 8
    n = n_rows // tr
    arrs, specs = [], []
    for kind, arr, cb, width, row0 in norm:
        if kind == "tile":
            assert row0 % tr == 0
            specs.append(pl.BlockSpec((tr, width), lambda i, cb=cb, rb=row0 // tr: (i + rb, cb)))
        elif kind == "prev":
            specs.append(pl.BlockSpec((8, width), lambda i, cb=cb: (jnp.maximum(i * (tr // 8) - 1, 0), cb)))
        else:
            specs.append(pl.BlockSpec((8, width), lambda i, cb=cb: (jnp.minimum((i + 1) * (tr // 8), n_rows // 8 - 1), cb)))
        arrs.append(arr)
    for c in consts:
        specs.append(pl.BlockSpec(c.shape, lambda i, nd=c.ndim: (0,) * nd))
        arrs.append(c)
    n_in, n_ro = len(arrs), len(row_outs)
    row_outs = [w if isinstance(w, tuple) else (w, F32) for w in row_outs]
    out_shape = [jax.ShapeDtypeStruct((n_rows, w), dt) for w, dt in row_outs]
    out_specs = [pl.BlockSpec((tr, w), lambda i: (i, 0)) for w, _ in row_outs]
    out_shape += [jax.ShapeDtypeStruct(s, F32) for s in acc_outs]
    out_specs += [pl.BlockSpec(s, lambda i: (0, 0)) for s in acc_outs]

    def body(*refs):
        i = pl.program_id(0)
        vals = [r[...] for r in refs[:n_in]]
        outs = fn(_Ctx(i, n), *vals)
        if not isinstance(outs, (tuple, list)):
            outs = (outs,)
        o_refs = refs[n_in:]
        for o_ref, o in zip(o_refs[:n_ro], outs[:n_ro]):
            o_ref[...] = o.astype(o_ref.dtype)
        if acc_outs:
            @pl.when(i == 0)
            def _():
                for o_ref in o_refs[n_ro:]:
                    o_ref[...] = jnp.zeros_like(o_ref)

            for o_ref, o in zip(o_refs[n_ro:], outs[n_ro:]):
                o_ref[...] += jnp.broadcast_to(o, o_ref.shape)

    res = pl.pallas_call(
        body, name=name, grid=(n,), in_specs=specs, out_specs=out_specs, out_shape=out_shape,
        compiler_params=_params(("arbitrary",)),
    )(*arrs)
    return res


def _sum0(v):
    return jnp.sum(v, axis=0, keepdims=True)


def _mean1(v):
    return jnp.mean(v, axis=-1, keepdims=True)


def _sigmoid(v):
    return 1.0 / (1.0 + jnp.exp(-v))


def _ln_stats(t):
    xc = t - _mean1(t)
    rstd = lax.rsqrt(_mean1(xc * xc) + LN_EPS)
    return xc * rstd, rstd


def _ln_bwd(xhat, rstd, dy, g):
    dxh = dy * g
    dx = rstd * (dxh - _mean1(dxh) - xhat * _mean1(dxh * xhat))
    return dx, _sum0(dy * xhat), _sum0(dy)


def _rms_fwd(v, g):
    return v * lax.rsqrt(_mean1(v * v) + RMS_EPS) * g


def _rms_bwd(v, dy, g):
    rs = lax.rsqrt(_mean1(v * v) + RMS_EPS)
    vh = v * rs
    dyg = dy * g
    return rs * (dyg - vh * _mean1(dyg * vh)), _sum0(dy * vh)


def _lane(shape):
    return lax.broadcasted_iota(jnp.int32, shape, len(shape) - 1)


def _shift_down(u, halo, s, is_first):
    tr = u.shape[0]
    rolled = pltpu.roll(u, s, 0)
    hr = jnp.where(is_first, 0.0, pltpu.roll(halo, s, 0))
    row = lax.broadcasted_iota(jnp.int32, hr.shape, 0)
    top = jnp.where(row < s, hr, rolled[0:8])
    if tr == 8:
        return top
    return jnp.concatenate([top, rolled[8:]], axis=0)


def _shift_up(d, halo, s, is_last):
    tr = d.shape[0]
    rolled = pltpu.roll(d, tr - s, 0)
    hr = jnp.where(is_last, 0.0, pltpu.roll(halo, 8 - s, 0))
    row = lax.broadcasted_iota(jnp.int32, hr.shape, 0)
    bot = jnp.where(row >= 8 - s, hr, rolled[tr - 8:])
    if tr == 8:
        return bot
    return jnp.concatenate([rolled[:tr - 8], bot], axis=0)


def _rope(v, ta, tb, tc):
    return v * ta + pltpu.roll(v, 16, 1) * tb + pltpu.roll(v, LANES - 16, 1) * tc


def _rope_bwd(d, ta, tb, tc):
    return d * ta + pltpu.roll(d * tb, LANES - 16, 1) + pltpu.roll(d * tc, 16, 1)


def _ssd_common(dtv, a_row):
    L = SSD_CHUNK
    a = dtv * a_row
    r = lax.broadcasted_iota(jnp.int32, (L, L), 0)
    c = lax.broadcasted_iota(jnp.int32, (L, L), 1)
    tril = r >= c
    cs = _dot_exact(tril.astype(F32), a)
    cs_t = cs.T
    cs_last = cs[L - 1:L, :]
    return dict(a=a, tril=tril, cs=cs, cs_t=cs_t, ecs=jnp.exp(cs), dte=jnp.exp(cs_last - cs),
                elast=jnp.exp(cs_last))


def _pair_sel(v, h0, lo):
    return jnp.where(lo, v[:, h0:h0 + 1], v[:, h0 + 1:h0 + 2])


def _ssd_pair(cm, h0, cb, xp, dtv, bmat, cmat, hp, lo):
    L = SSD_CHUNK
    x = xp * _pair_sel(dtv, h0, lo)
    lam0 = jnp.exp(jnp.where(cm["tril"], cm["cs"][:, h0:h0 + 1] - cm["cs_t"][h0:h0 + 1, :], NEG))
    lam1 = jnp.exp(jnp.where(cm["tril"], cm["cs"][:, h0 + 1:h0 + 2] - cm["cs_t"][h0 + 1:h0 + 2, :], NEG))
    m0, m1 = cb * lam0, cb * lam1
    ydiag = jnp.where(lo, _dot(m0, x), _dot(m1, x))
    ecs_p = _pair_sel(cm["ecs"], h0, lo)
    dte_p = _pair_sel(cm["dte"], h0, lo)
    yoff = _dot(cmat, hp, NT) * ecs_p
    xd = x * dte_p
    st = _dot(xd, bmat, TN)
    rlo = lax.broadcasted_iota(jnp.int32, (LANES, SSD_STATE), 0) < 64
    decay = jnp.where(rlo, cm["elast"][:, h0:h0 + 1], cm["elast"][:, h0 + 1:h0 + 2])
    h_next = hp * decay + st
    return dict(x=x, lam0=lam0, lam1=lam1, m0=m0, m1=m1, y=ydiag + yoff, yoff=yoff, ecs_p=ecs_p, dte_p=dte_p,
                xd=xd, decay=decay, h_next=h_next)


def _ssd_fwd(xbc, dt, a_row, *, name):
    S = xbc.shape[0]
    L = SSD_CHUNK
    nc = S // L

    def body(xs_ref, bm_ref, cm_ref, dt_ref, a_ref, y_ref, hs_ref, h_scr):
        @pl.when(pl.program_id(0) == 0)
        def _():
            h_scr[...] = jnp.zeros_like(h_scr)

        dtv = dt_ref[...]
        cm = _ssd_common(dtv, a_ref[...])
        lo = _lane((L, LANES)) < 64
        ys = []
        for g in range(2):
            bmat = bm_ref[:, g * 128:(g + 1) * 128]
            cmat = cm_ref[:, g * 128:(g + 1) * 128]
            cb = _dot(cmat, bmat, NT)
            for pr in range(2):
                p4 = 2 * g + pr
                hp = h_scr[p4]
                hs_ref[0, p4 * 128:(p4 + 1) * 128, :] = hp
                t = _ssd_pair(cm, 2 * p4, cb, xs_ref[:, p4 * 128:(p4 + 1) * 128], dtv, bmat, cmat, hp, lo)
                ys.append(t["y"])
                h_scr[p4] = t["h_next"]
        y_ref[...] = jnp.concatenate(ys, axis=1)

    return pl.pallas_call(
        body, name=name, grid=(nc,),
        in_specs=[pl.BlockSpec((L, 512), lambda c: (c, 0)), pl.BlockSpec((L, 256), lambda c: (c, 2)),
                  pl.BlockSpec((L, 256), lambda c: (c, 3)), pl.BlockSpec((L, 128), lambda c: (c, 0)),
                  pl.BlockSpec((1, 128), lambda c: (0, 0))],
        out_specs=[pl.BlockSpec((L, 512), lambda c: (c, 0)), pl.BlockSpec((1, 512, 128), lambda c: (c, 0, 0))],
        out_shape=[jax.ShapeDtypeStruct((S, 512), F32), jax.ShapeDtypeStruct((nc, 512, 128), F32)],
        scratch_shapes=[pltpu.VMEM((4, 128, 128), F32)],
        compiler_params=_params(("arbitrary",)),
    )(xbc, xbc, xbc, dt, a_row)


def _ssd_bwd(xbc, dt, a_row, hs, dy, *, name):
    S = xbc.shape[0]
    L = SSD_CHUNK
    nc = S // L

    def body(xs_ref, bm_ref, cm_ref, dt_ref, a_ref, hs_ref, dy_ref, dxs_ref, dbc_ref, ddt_ref, da_ref, g_scr):
        @pl.when(pl.program_id(0) == 0)
        def _():
            g_scr[...] = jnp.zeros_like(g_scr)
            da_ref[...] = jnp.zeros_like(da_ref)

        dtv = dt_ref[...]
        a_row_v = a_ref[...]
        cm = _ssd_common(dtv, a_row_v)
        lo = _lane((L, LANES)) < 64
        lane_row = _lane((1, LANES))
        ri = lax.broadcasted_iota(jnp.int32, (L, L), 0)
        ci = lax.broadcasted_iota(jnp.int32, (L, L), 1)
        triu = (ri <= ci).astype(F32)
        stril = ri > ci

        def halves(v, mask):
            return (jnp.sum(jnp.where(mask, v, 0.0), axis=1, keepdims=True),
                    jnp.sum(jnp.where(mask, 0.0, v), axis=1, keepdims=True))

        i_all = jnp.zeros((L, LANES), F32)
        yo_all = jnp.zeros((L, LANES), F32)
        w_all = jnp.zeros((L, LANES), F32)
        ddt_x = jnp.zeros((L, LANES), F32)
        e_row = jnp.zeros((1, LANES), F32)
        rlo = lax.broadcasted_iota(jnp.int32, (LANES, SSD_STATE), 0) < 64
        dxs, dbs, dcs = [], [], []
        for g in range(2):
            bmat = bm_ref[:, g * 128:(g + 1) * 128]
            cmat = cm_ref[:, g * 128:(g + 1) * 128]
            cb = _dot(cmat, bmat, NT)
            dcb = jnp.zeros((L, L), F32)
            db = jnp.zeros((L, SSD_STATE), F32)
            dc = jnp.zeros((L, SSD_STATE), F32)
            for pr in range(2):
                p4 = 2 * g + pr
                h0 = 2 * p4
                hp = hs_ref[0, p4 * 128:(p4 + 1) * 128, :]
                xp = xs_ref[:, p4 * 128:(p4 + 1) * 128]
                t = _ssd_pair(cm, h0, cb, xp, dtv, bmat, cmat, hp, lo)
                gst = g_scr[p4]
                dyp = dy_ref[:, p4 * 128:(p4 + 1) * 128]
                dy0 = jnp.where(lo, dyp, 0.0)
                dy1 = dyp - dy0
                bg = _dot(bmat, gst, NT)
                dx = _dot(t["m0"], dy0, TN) + _dot(t["m1"], dy1, TN) + bg * t["dte_p"]
                dm0, dm1 = _dot(dy0, t["x"], NT), _dot(dy1, t["x"], NT)
                dcb = dcb + dm0 * t["lam0"] + dm1 * t["lam1"]
                dye = dyp * t["ecs_p"]
                dc = dc + _dot(dye, hp)
                db = db + _dot(t["xd"], gst)
                i0 = jnp.sum(jnp.where(stril, _dot(triu, dm0 * t["m0"]), 0.0), axis=1, keepdims=True)
                i1 = jnp.sum(jnp.where(stril, _dot(triu, dm1 * t["m1"]), 0.0), axis=1, keepdims=True)
                yo0, yo1 = halves(dyp * t["yoff"], lo)
                w0, w1 = halves(t["xd"] * bg, lo)
                gh = gst * (hp * t["decay"])
                e0 = _sum0(jnp.sum(jnp.where(rlo, gh, 0.0), axis=1, keepdims=True))
                e1 = _sum0(jnp.sum(jnp.where(rlo, 0.0, gh), axis=1, keepdims=True))
                x0, x1 = halves(dx * xp, lo)
                oh0 = (lane_row == h0).astype(F32)
                oh1 = (lane_row == h0 + 1).astype(F32)
                i_all = i_all + i0 * oh0 + i1 * oh1
                yo_all = yo_all + yo0 * oh0 + yo1 * oh1
                w_all = w_all + w0 * oh0 + w1 * oh1
                e_row = e_row + e0 * oh0 + e1 * oh1
                ddt_x = ddt_x + x0 * oh0 + x1 * oh1
                dxs.append(dx * _pair_sel(dtv, h0, lo))
                g_scr[p4] = gst * t["decay"] + _dot(dye, cmat, TN)
            dbs.append(db + _dot(dcb, cmat, TN))
            dcs.append(dc + _dot(dcb, bmat))
        da = i_all + _dot_exact(triu, yo_all) + _dot_exact(stril.astype(F32), w_all) + e_row
        ddt_ref[...] = da * a_row_v + ddt_x
        da_ref[...] += _sum0(da * dtv)
        dxs_ref[...] = jnp.concatenate(dxs, axis=1)
        dbc_ref[...] = jnp.concatenate(dbs + dcs, axis=1)

    rev = lambda c: nc - 1 - c
    return pl.pallas_call(
        body, name=name, grid=(nc,),
        in_specs=[pl.BlockSpec((L, 512), lambda c: (rev(c), 0)), pl.BlockSpec((L, 256), lambda c: (rev(c), 2)),
                  pl.BlockSpec((L, 256), lambda c: (rev(c), 3)), pl.BlockSpec((L, 128), lambda c: (rev(c), 0)),
                  pl.BlockSpec((1, 128), lambda c: (0, 0)), pl.BlockSpec((1, 512, 128), lambda c: (rev(c), 0, 0)),
                  pl.BlockSpec((L, 512), lambda c: (rev(c), 0))],
        out_specs=[pl.BlockSpec((L, 512), lambda c: (rev(c), 0)), pl.BlockSpec((L, 512), lambda c: (rev(c), 0)),
                   pl.BlockSpec((L, 128), lambda c: (rev(c), 0)), pl.BlockSpec((1, 128), lambda c: (0, 0))],
        out_shape=[jax.ShapeDtypeStruct((S, 512), F32), jax.ShapeDtypeStruct((S, 512), F32),
                   jax.ShapeDtypeStruct((S, 128), F32), jax.ShapeDtypeStruct((1, 128), F32)],
        scratch_shapes=[pltpu.VMEM((4, 128, 128), F32)],
        compiler_params=_params(("arbitrary",)),
    )(xbc, xbc, xbc, dt, a_row, hs, dy)


MLA_SCALE = MLA_QK ** -0.5


def _causal_scores(q, k, qi, ki, t):
    s = _dot(q, k, NT) * MLA_SCALE
    row = qi * t + lax.broadcasted_iota(jnp.int32, (t, t), 0)
    col = ki * t + lax.broadcasted_iota(jnp.int32, (t, t), 1)
    return jnp.where(col <= row, s, NEG)


def _mla_fwd(q, k, kv, *, name):
    S = q.shape[0]
    t = min(ATTN_TILE, S)
    nq = S // t

    def body(q_ref, k_ref, v_ref, o_ref, lse_ref, m_scr, l_scr, acc_scr):
        qi, ki = pl.program_id(1), pl.program_id(2)

        @pl.when(ki == 0)
        def _():
            m_scr[...] = jnp.full_like(m_scr, NEG)
            l_scr[...] = jnp.zeros_like(l_scr)
            acc_scr[...] = jnp.zeros_like(acc_scr)

        @pl.when(ki <= qi)
        def _():
            s = _causal_scores(q_ref[...], k_ref[...], qi, ki, t)
            m_old = m_scr[:, 0:1]
            m_new = jnp.maximum(m_old, jnp.max(s, axis=1, keepdims=True))
            p = jnp.exp(s - m_new)
            corr = jnp.exp(m_old - m_new)
            l_scr[...] = jnp.broadcast_to(corr * l_scr[:, 0:1] + jnp.sum(p, axis=1, keepdims=True), l_scr.shape)
            acc_scr[...] = corr * acc_scr[...] + _dot(p, v_ref[...])
            m_scr[...] = jnp.broadcast_to(m_new, m_scr.shape)

        @pl.when(ki == nq - 1)
        def _():
            l = l_scr[:, 0:1]
            o_ref[...] = acc_scr[...] / l
            lse_ref[0] = jnp.broadcast_to(m_scr[:, 0:1] + jnp.log(l), (t, LANES))

    return pl.pallas_call(
        body, name=name, grid=(MLA_HEADS, nq, nq),
        in_specs=[pl.BlockSpec((t, 128), lambda h, qi, ki: (qi, h)),
                  pl.BlockSpec((t, 128), lambda h, qi, ki: (jnp.minimum(ki, qi), h)),
                  pl.BlockSpec((t, 128), lambda h, qi, ki: (jnp.minimum(ki, qi), 2 * h + 1))],
        out_specs=[pl.BlockSpec((t, 128), lambda h, qi, ki: (qi, h)),
                   pl.BlockSpec((1, t, 128), lambda h, qi, ki: (h, qi, 0))],
        out_shape=[jax.ShapeDtypeStruct((S, MLA_HEADS * 128), F32), jax.ShapeDtypeStruct((MLA_HEADS, S, 128), F32)],
        scratch_shapes=[pltpu.VMEM((t, 128), F32), pltpu.VMEM((t, 128), F32), pltpu.VMEM((t, 128), F32)],
        compiler_params=_params(("parallel", "parallel", "arbitrary")),
    )(q, k, kv)


def _mla_bwd_dkv(q, k, kv, o, do, lse, *, name):
    S = q.shape[0]
    t = min(ATTN_TILE, S)
    nq = S // t

    def body(q_ref, k_ref, v_ref, o_ref, do_ref, lse_ref, dkv_ref):
        ki, qi = pl.program_id(1), pl.program_id(2)

        @pl.when(qi == 0)
        def _():
            dkv_ref[...] = jnp.zeros_like(dkv_ref)

        @pl.when(qi >= ki)
        def _():
            qv, dov = q_ref[...], do_ref[...]
            s = _causal_scores(qv, k_ref[...], qi, ki, t)
            p = jnp.exp(s - lse_ref[0][:, 0:1])
            dv = _dot(p, dov, TN)
            dp = _dot(dov, v_ref[...], NT)
            delta = jnp.sum(dov * o_ref[...], axis=1, keepdims=True)
            ds = p * (dp - delta) * MLA_SCALE
            dkv_ref[...] += jnp.concatenate([_dot(ds, qv, TN), dv], axis=1)

    qmap = lambda h, ki, qi: (jnp.maximum(qi, ki), h)
    return pl.pallas_call(
        body, name=name, grid=(MLA_HEADS, nq, nq),
        in_specs=[pl.BlockSpec((t, 128), qmap),
                  pl.BlockSpec((t, 128), lambda h, ki, qi: (ki, h)),
                  pl.BlockSpec((t, 128), lambda h, ki, qi: (ki, 2 * h + 1)),
                  pl.BlockSpec((t, 128), qmap), pl.BlockSpec((t, 128), qmap),
                  pl.BlockSpec((1, t, 128), lambda h, ki, qi: (h, jnp.maximum(qi, ki), 0))],
        out_specs=pl.BlockSpec((t, 256), lambda h, ki, qi: (ki, h)),
        out_shape=jax.ShapeDtypeStruct((S, MLA_HEADS * 256), F32),
        compiler_params=_params(("parallel", "parallel", "arbitrary")),
    )(q, k, kv, o, do, lse)


def _mla_bwd_dq(q, k, kv, o, do, lse, *, name):
    S = q.shape[0]
    t = min(ATTN_TILE, S)
    nq = S // t

    def body(q_ref, k_ref, v_ref, o_ref, do_ref, lse_ref, dq_ref):
        qi, ki = pl.program_id(1), pl.program_id(2)

        @pl.when(ki == 0)
        def _():
            dq_ref[...] = jnp.zeros_like(dq_ref)

        @pl.when(ki <= qi)
        def _():
            dov, kv = do_ref[...], k_ref[...]
            s = _causal_scores(q_ref[...], kv, qi, ki, t)
            p = jnp.exp(s - lse_ref[0][:, 0:1])
            dp = _dot(dov, v_ref[...], NT)
            delta = jnp.sum(dov * o_ref[...], axis=1, keepdims=True)
            ds = p * (dp - delta) * MLA_SCALE
            dq_ref[...] += _dot(ds, kv)

    qmap = lambda h, qi, ki: (qi, h)
    return pl.pallas_call(
        body, name=name, grid=(MLA_HEADS, nq, nq),
        in_specs=[pl.BlockSpec((t, 128), qmap),
                  pl.BlockSpec((t, 128), lambda h, qi, ki: (jnp.minimum(ki, qi), h)),
                  pl.BlockSpec((t, 128), lambda h, qi, ki: (jnp.minimum(ki, qi), 2 * h + 1)),
                  pl.BlockSpec((t, 128), qmap), pl.BlockSpec((t, 128), qmap),
                  pl.BlockSpec((1, t, 128), lambda h, qi, ki: (h, qi, 0))],
        out_specs=pl.BlockSpec((t, 128), qmap),
        out_shape=jax.ShapeDtypeStruct((S, MLA_HEADS * 128), F32),
        compiler_params=_params(("parallel", "parallel", "arbitrary")),
    )(q, k, kv, o, do, lse)


def _fn_ln(ctx, x, g, b):
    xhat, _ = _ln_stats(x)
    return (xhat * g + b,)


def _fn_conv_fwd(ctx, u, up, dtr, w8, cb, dtb):
    first = ctx.i == 0
    y = u * w8[3:4] + cb
    for s in (1, 2, 3):
        y = y + _shift_down(u, up, s, first) * w8[3 - s:4 - s]
    act = y * _sigmoid(y)
    v = dtr + dtb
    e = jnp.exp(-jnp.abs(v))
    one_p = 1.0 + e
    log1p = jnp.where(one_p == 1.0, e, jnp.log(one_p) * e / (one_p - 1.0))
    return y, act, jnp.maximum(v, 0.0) + log1p


def _fn_ssd_post(ctx, y, xs, z, dexp, g):
    yg = (y + xs * dexp) * (z * _sigmoid(z))
    outs = []
    for k in range(2):
        v = yg[:, 256 * k:256 * (k + 1)]
        outs.append(v * lax.rsqrt(_mean1(v * v) + RMS_EPS))
    return (jnp.concatenate(outs, axis=1) * g,)


def _fn_ssd_post_bwd(ctx, dyn, y, xs, z, dexp, g):
    yt = y + xs * dexp
    sig = _sigmoid(z)
    sz = z * sig
    yg = yt * sz
    dyh = dyn * g
    yh, dyg = [], []
    for k in range(2):
        sl = slice(256 * k, 256 * (k + 1))
        v = yg[:, sl]
        rs = lax.rsqrt(_mean1(v * v) + RMS_EPS)
        vh = v * rs
        yh.append(vh)
        dyg.append(rs * (dyh[:, sl] - vh * _mean1(dyh[:, sl] * vh)))
    yh = jnp.concatenate(yh, axis=1)
    dyg = jnp.concatenate(dyg, axis=1)
    dyt = dyg * sz
    dz = dyg * yt * (sig * (1.0 + z * (1.0 - sig)))
    return dyt, dz, dyt * dexp, _sum0(dyt * xs), _sum0(dyn * yh)


def _fn_mla_pre(ctx, ql, kvl, gq, gkv):
    return _rms_fwd(ql, gq), _rms_fwd(kvl, gkv)


def _fn_mla_pre_bwd(ctx, ql, kvl, dqn, dkvn, gq, gkv):
    dql, dgq = _rms_bwd(ql, dqn, gq)
    dkvl, dgkv = _rms_bwd(kvl, dkvn, gkv)
    return dql, dkvl, dgq, dgkv


def _fn_rope(ctx, qp, kv, kr, ta, tb, tc):
    kpe = _rope(kr, ta, tb, tc)
    qs, ks = [], []
    for h in range(MLA_HEADS):
        qs.append(_rope(qp[:, 128 * h:128 * (h + 1)], ta, tb, tc))
        ks.append(kv[:, 256 * h:256 * h + 128] + kpe)
    return jnp.concatenate(qs, axis=1), jnp.concatenate(ks, axis=1)


def _fn_rope_bwd(ctx, dq, dkv, ta, tb, tc):
    qs = []
    ksum = jnp.zeros_like(ta)
    for h in range(MLA_HEADS):
        qs.append(_rope_bwd(dq[:, 128 * h:128 * (h + 1)], ta, tb, tc))
        ksum = ksum + dkv[:, 256 * h:256 * h + 128]
    lane = _lane(ksum.shape)
    dkr = jnp.where((lane >= 64) & (lane < 96), _rope_bwd(ksum, ta, tb, tc), 0.0)
    return jnp.concatenate(qs, axis=1), dkr


MEM_SCALE = MEM_HEAD_DIM ** -0.5


def _mem_probs(qh, kh):
    s = _dot(qh, kh, NT) * MEM_SCALE
    p = jnp.exp(s - jnp.max(s, axis=1, keepdims=True))
    return p / jnp.sum(p, axis=1, keepdims=True)


def _fn_mem_fwd(ctx, q, km, vm):
    outs = []
    for h in range(MEM_HEADS):
        sl = slice(256 * h, 256 * (h + 1))
        outs.append(_dot(_mem_probs(q[:, sl], km[:, sl]), vm[:, sl]))
    return (jnp.concatenate(outs, axis=1),)


def _fn_mem_bwd(ctx, q, do, km, vm):
    dqs, dks, dvs = [], [], []
    for h in range(MEM_HEADS):
        sl = slice(256 * h, 256 * (h + 1))
        p = _mem_probs(q[:, sl], km[:, sl])
        dvs.append(_dot(p, do[:, sl], TN))
        dp = _dot(do[:, sl], vm[:, sl], NT)
        ds = p * (dp - jnp.sum(dp * p, axis=1, keepdims=True)) * MEM_SCALE
        dqs.append(_dot(ds, km[:, sl]))
        dks.append(_dot(ds, q[:, sl], TN))
    return jnp.concatenate(dqs, axis=1), jnp.concatenate(dks, axis=1), jnp.concatenate(dvs, axis=1)


def _fn_res_ln(ctx, h, r, g, b):
    xhat, _ = _ln_stats(ALPHA * h + r)
    return (xhat * g + b,)


def _fn_res_ln_bwd(ctx, h, r, d1, d2, g):
    xhat, rstd = _ln_stats(ALPHA * h + r)
    return _ln_bwd(xhat, rstd, ALPHA * d1 + d2, g)


def _fn_in_ln_bwd(ctx, x, d1, d2, g):
    xhat, rstd = _ln_stats(x)
    return _ln_bwd(xhat, rstd, ALPHA * d1 + d2, g)


def _fn_final(ctx, h2, ff, tgt, g, b):
    xhat, rstd = _ln_stats(ALPHA * h2 + ff)
    e = xhat * g + b - tgt
    loss = 0.5 * _sum0(jnp.sum(e * e, axis=1, keepdims=True)) / D_MODEL
    dx, dg, db = _ln_bwd(xhat, rstd, e / D_MODEL, g)
    return dx, dg, db, loss


def _fn_du(ctx, u, da):
    return (da * 2.0 * jnp.maximum(u, 0.0),)


def _relu2(u):
    r = jnp.maximum(u, 0.0)
    return r * r


def _fn_conv_bwd_a(ctx, y, dxs1, dxs2, dbc, dtr, ddt, dtb):
    sig = _sigmoid(y)
    dact = jnp.concatenate([dxs1 + dxs2, dbc], axis=1)
    dyc = dact * (sig * (1.0 + y * (1.0 - sig)))
    ddtr = ddt * _sigmoid(dtr + dtb)
    return dyc, ddtr, _sum0(dyc), _sum0(ddtr)


def _fn_conv_bwd_b(ctx, d, dn, u, up, w8):
    first, last = ctx.i == 0, ctx.i == ctx.n - 1
    du = d * w8[3:4]
    row = lax.broadcasted_iota(jnp.int32, w8.shape, 0)
    dw = jnp.where(row == 3, _sum0(d * u), 0.0)
    for s in (1, 2, 3):
        du = du + _shift_up(d, dn, s, last) * w8[3 - s:4 - s]
        dw = dw + jnp.where(row == 3 - s, _sum0(d * _shift_down(u, up, s, first)), 0.0)
    return du, dw


def _fn_adam(ctx, w, g, m, v):
    m = ADAM_B1 * m + (1.0 - ADAM_B1) * g
    v = ADAM_B2 * v + (1.0 - ADAM_B2) * (g * g)
    m_hat = m / (1.0 - ADAM_B1 ** ADAM_STEP)
    v_hat = v / (1.0 - ADAM_B2 ** ADAM_STEP)
    return -ADAM_LR * (m_hat / (jnp.sqrt(v_hat) + ADAM_EPS) + ADAM_WD * w), m, v


def _fn_add2(ctx, a, b):
    s = a + b
    return s, s


def _fn_add4(ctx, a, r0, r1, r2):
    return (((a + r0.astype(F32)) + r1.astype(F32)) + r2.astype(F32),)


def _z(r, c, dt):
    return jnp.zeros((r, c), dt)


def _pad_w_in(w):
    r, dt = w.shape[0], w.dtype
    return jnp.concatenate([w[:, 512:1536], w[:, 0:512], w[:, 1544:1928], w[:, 1536:1544], _z(r, 120, dt),
                            w[:, 1928:2184], _z(r, 64, dt), w[:, 2184:2216], _z(r, 32, dt), _z(r, 128, dt)], axis=1)


def _unpad_w_in(d):
    return jnp.concatenate([d[:, 1024:1536], d[:, 0:1024], d[:, 1920:1928], d[:, 1536:1920], d[:, 2048:2304],
                            d[:, 2368:2400]], axis=1)


def _pad_heads(w, width):
    r = w.shape[0]
    w3 = w.reshape(r, MLA_HEADS, width)
    return jnp.pad(w3, ((0, 0), (0, 0), (0, 128 - width))).reshape(r, MLA_HEADS * 128)


def _pad_w_kv(w):
    r = w.shape[0]
    w4 = w.reshape(r, MLA_HEADS, 2, 64)
    return jnp.pad(w4, ((0, 0), (0, 0), (0, 0), (0, 64))).reshape(r, MLA_HEADS * 256)


def _unpad_w_kv(d):
    r = d.shape[0]
    return d.reshape(r, MLA_HEADS, 2, 128)[:, :, :, :64].reshape(r, MLA_HEADS * 128)


def _pad_w_mix(w):
    wo = jnp.pad(w[512:1024].reshape(MLA_HEADS, 64, D_MODEL), ((0, 0), (0, 64), (0, 0))).reshape(1024, D_MODEL)
    return jnp.concatenate([wo, w[0:512]], axis=0)


def _unpad_w_mix(d):
    do = d[:1024].reshape(MLA_HEADS, 128, D_MODEL)[:, :64].reshape(512, D_MODEL)
    return jnp.concatenate([d[1024:1536], do], axis=0)


def _row(v, width=None):
    v = v.reshape(1, -1).astype(F32)
    if width is not None and v.shape[1] < width:
        v = jnp.pad(v, ((0, 0), (0, width - v.shape[1])))
    return v


def _local_step(x, mem, positions, target, W, P):
    S = x.shape[0]
    tr = ROW_TILE
    w_in_p = _pad_w_in(W["w_in"])
    w_q_p = _pad_heads(W["w_q_up"], MLA_QK)
    w_kv_p = _pad_w_kv(W["w_kv_up"])
    w_mix_p = _pad_w_mix(W["w_mix_out"])
    conv_w8 = jnp.pad(P["conv_w"].astype(F32), ((0, 4), (0, 0)))
    conv_b = _row(P["conv_b"])
    dt_b = _row(P["dt_bias"], 128)
    a_head = -jnp.exp(P["a_log"].reshape(-1).astype(F32))
    a_row = _row(a_head, 128)
    dexp = jnp.repeat(P["d_skip"].reshape(-1).astype(F32), 64).reshape(1, 512)
    g_ssd, g_q, g_kv = _row(P["ssd_norm_g"]), _row(P["q_norm_g"]), _row(P["kv_norm_g"])
    g_in, b_in = _row(P["ln_in_g"]), _row(P["ln_in_b"])
    g1, b1, g2, b2, g3, b3 = (_row(P[k]) for k in ("ln1_g", "ln1_b", "ln2_g", "ln2_b", "ln3_g", "ln3_b"))

    half = MLA_ROPE // 2
    inv_freq = jnp.power(ROPE_THETA, -jnp.arange(half, dtype=F32) / half)
    ang = positions.reshape(S, 1).astype(F32) * inv_freq
    cos, sin = jnp.cos(ang), jnp.sin(ang)
    zc = lambda n: jnp.zeros((S, n), F32)
    rope_a = jnp.concatenate([jnp.ones((S, 64), F32), cos, cos, zc(32)], axis=1)
    rope_b = jnp.concatenate([zc(80), sin, zc(32)], axis=1)
    rope_c = jnp.concatenate([zc(64), -sin, zc(48)], axis=1)

    (h0,) = _rowwise(_fn_ln, [x], [g_in, b_in], [D_MODEL], tr=tr, name="ln_in")
    proj = _mm(h0, w_in_p, form="nn", name="mm_in")
    conv_y, xbc, dt = _rowwise(
        _fn_conv_fwd, [(proj,) + SEG_XBC, ("prev", proj) + SEG_XBC, (proj,) + SEG_DT], [conv_w8, conv_b, dt_b],
        [1024, 1024, 128], tr=tr, name="conv_fwd")
    y_ssd, hs = _ssd_fwd(xbc, dt, a_row, name="ssd_fwd")
    (y_n,) = _rowwise(_fn_ssd_post, [y_ssd, (xbc, 0, 512), (proj,) + SEG_Z], [dexp, g_ssd], [512], tr=tr,
                      name="ssd_post")
    q_n, kv_n = _rowwise(_fn_mla_pre, [(proj,) + SEG_QLAT, (proj,) + SEG_KVLAT], [g_q, g_kv], [384, 256], tr=tr,
                         name="mla_pre")
    qp = _mm(q_n, w_q_p, form="nn", name="mm_q_up")
    kvp = _mm(kv_n, w_kv_p, form="nn", name="mm_kv_up")
    q_rot, k_full = _rowwise(_fn_rope, [qp, kvp, (proj,) + SEG_KR, rope_a, rope_b, rope_c], [],
                             [1024, 1024], tr=tr, name="rope")
    o_att, lse = _mla_fwd(q_rot, k_full, kvp, name="mla_fwd")
    cat = jnp.concatenate([o_att, y_n], axis=1)
    mix = _mm(cat, w_mix_p, form="nn", name="mm_mix")
    (h1,) = _rowwise(_fn_res_ln, [h0, mix], [g1, b1], [D_MODEL], tr=tr, name="ln1")
    qm = _mm(h1, W["w_mem_q"], form="nn", name="mm_mem_q")
    km = _mm(mem, W["w_mem_k"], form="nn", name="mm_mem_k")
    vm = _mm(mem, W["w_mem_v"], form="nn", name="mm_mem_v")
    (om,) = _rowwise(_fn_mem_fwd, [qm], [km, vm], [D_MODEL], tr=tr, name="mem_fwd")
    xa = _mm(om, W["w_mem_o"], form="nn", name="mm_mem_o")
    (h2,) = _rowwise(_fn_res_ln, [h1, xa], [g2, b2], [D_MODEL], tr=tr, name="ln2")
    u = _mm(h2, W["w_up"], form="nn", name="mm_up")
    ff = _mm(u, W["w_down"], form="nn", a_pro=_relu2, name="mm_down")

    dt3, dg3, db3, loss = _rowwise(_fn_final, [h2, ff, target], [g3, b3], [D_MODEL],
                                   [(1, D_MODEL), (1, D_MODEL), (1, 128)], tr=tr, name="ln3_loss")
    da = _mm(dt3, W["w_down"], form="nt", name="mm_down_dx")
    dw_down = _mm(u, dt3, form="tn", a_pro=_relu2, name="mm_down_dw")
    (du,) = _rowwise(_fn_du, [u, da], [], [4 * D_MODEL], tr=128, name="mlp_du")
    dw_up = _mm(h2, du, form="tn", name="mm_up_dw")
    dh2 = _mm(du, W["w_up"], form="nt", name="mm_up_dx")
    dt2, dg2, db2 = _rowwise(_fn_res_ln_bwd, [h1, xa, dt3, dh2], [g2], [D_MODEL], [(1, D_MODEL)] * 2, tr=tr,
                             name="ln2_bwd")
    dom = _mm(dt2, W["w_mem_o"], form="nt", name="mm_mem_o_dx")
    dw_mem_o = _mm(om, dt2, form="tn", name="mm_mem_o_dw")
    dqm, dkm, dvm = _rowwise(_fn_mem_bwd, [qm, dom], [km, vm], [D_MODEL], [(256, D_MODEL)] * 2, tr=tr,
                             name="mem_bwd")
    dw_mem_q = _mm(h1, dqm, form="tn", name="mm_mem_q_dw")
    dw_mem_k = _mm(mem, dkm, form="tn", name="mm_mem_k_dw")
    dw_mem_v = _mm(mem, dvm, form="tn", name="mm_mem_v_dw")
    dh1 = _mm(dqm, W["w_mem_q"], form="nt", name="mm_mem_q_dx")
    dt1, dg1, db1 = _rowwise(_fn_res_ln_bwd, [h0, mix, dt2, dh1], [g1], [D_MODEL], [(1, D_MODEL)] * 2, tr=tr,
                             name="ln1_bwd")
    dcat = _mm(dt1, w_mix_p, form="nt", name="mm_mix_dx")
    dw_mix_p = _mm(cat, dt1, form="tn", name="mm_mix_dw")
    dkv_att = _mla_bwd_dkv(q_rot, k_full, kvp, o_att, dcat, lse, name="mla_bwd_dkv")
    dq_rot = _mla_bwd_dq(q_rot, k_full, kvp, o_att, dcat, lse, name="mla_bwd_dq")
    dqp, dkr = _rowwise(_fn_rope_bwd, [dq_rot, dkv_att, rope_a, rope_b, rope_c], [], [1024, 128], tr=tr,
                        name="rope_bwd")
    dw_q_p = _mm(q_n, dqp, form="tn", name="mm_q_up_dw")
    dq_n = _mm(dqp, w_q_p, form="nt", name="mm_q_up_dx")
    dw_kv_p = _mm(kv_n, dkv_att, form="tn", name="mm_kv_up_dw")
    dkv_n = _mm(dkv_att, w_kv_p, form="nt", name="mm_kv_up_dx")
    dq_lat, dkv_lat, dg_q, dg_kv = _rowwise(
        _fn_mla_pre_bwd, [(proj,) + SEG_QLAT, (proj,) + SEG_KVLAT, dq_n, dkv_n], [g_q, g_kv], [384, 256],
        [(1, 384), (1, 256)], tr=tr, name="mla_pre_bwd")
    dy_ssd, dz, dxs_skip, ddexp, dg_ssd = _rowwise(
        _fn_ssd_post_bwd, [(dcat, 1024, 512), y_ssd, (xbc, 0, 512), (proj,) + SEG_Z], [dexp, g_ssd],
        [512, 512, 512], [(1, 512)] * 2, tr=tr, name="ssd_post_bwd")
    dxs, dbc, ddt, da_head = _ssd_bwd(xbc, dt, a_row, hs, dy_ssd, name="ssd_bwd")
    dyc, ddtr, dconv_b, ddt_b = _rowwise(
        _fn_conv_bwd_a, [conv_y, dxs, dxs_skip, dbc, (proj,) + SEG_DT, ddt], [dt_b], [1024, 128],
        [(1, 1024), (1, 128)], tr=tr, name="conv_bwd_a")
    dxbc, dconv_w8 = _rowwise(
        _fn_conv_bwd_b, [dyc, ("next", dyc, 0, 1024), (proj,) + SEG_XBC, ("prev", proj) + SEG_XBC], [conv_w8], [1024],
        [(8, 1024)], tr=tr, name="conv_bwd_b")
    dproj = jnp.concatenate([dxbc, dz, dq_lat, ddtr, dkv_lat, dkr, jnp.zeros((S, 128), F32)], axis=1)
    dw_in_p = _mm(h0, dproj, form="tn", name="mm_in_dw")
    dh0 = _mm(dproj, w_in_p, form="nt", name="mm_in_dx")
    grad_x, dg_in, db_in = _rowwise(_fn_in_ln_bwd, [x, dt1, dh0], [g_in], [D_MODEL], [(1, D_MODEL)] * 2, tr=tr,
                                    name="ln_in_bwd")

    big = {
        "w_in": _unpad_w_in(dw_in_p),
        "w_q_up": dw_q_p.reshape(384, MLA_HEADS, 128)[:, :, :MLA_QK].reshape(384, MLA_HEADS * MLA_QK),
        "w_kv_up": _unpad_w_kv(dw_kv_p),
        "w_mix_out": _unpad_w_mix(dw_mix_p),
        "w_mem_q": dw_mem_q, "w_mem_k": dw_mem_k, "w_mem_v": dw_mem_v, "w_mem_o": dw_mem_o,
        "w_up": dw_up, "w_down": dw_down,
        "conv_w": dconv_w8[0:4],
    }
    small = {
        "ln_in_g": dg_in, "ln_in_b": db_in, "conv_b": dconv_b, "dt_bias": ddt_b[:, :8],
        "a_log": da_head[:, :8] * a_head.reshape(1, 8),
        "d_skip": ddexp.reshape(8, 64).sum(axis=1).reshape(1, 8),
        "ssd_norm_g": dg_ssd, "q_norm_g": dg_q, "kv_norm_g": dg_kv,
        "ln1_g": dg1, "ln1_b": db1, "ln2_g": dg2, "ln2_b": db2, "ln3_g": dg3, "ln3_b": db3,
    }
    return loss[0, 0], grad_x, big, small


BIG = {
    "w_in": (1024, 2216, 1), "w_q_up": (384, 768, 1), "w_kv_up": (256, 1024, 1), "w_mix_out": (1024, 1024, 0),
    "w_mem_q": (1024, 1024, 0), "w_mem_k": (1024, 1024, 0), "w_mem_v": (1024, 1024, 0), "w_mem_o": (1024, 1024, 0),
    "w_up": (1024, 4096, 1), "w_down": (4096, 1024, 0), "conv_w": (4, 1024, 1),
}
BIG_ORDER = list(BIG)
SMALL_ORDER = ["ln_in_g", "ln_in_b", "conv_b", "dt_bias", "a_log", "d_skip", "ssd_norm_g", "q_norm_g", "kv_norm_g",
               "ln1_g", "ln1_b", "ln2_g", "ln2_b", "ln3_g", "ln3_b"]
N_SHARD = 4
PACK_COLS = 1024
PACK_ROWS = 4032
HALF_ROWS = PACK_ROWS // 2
GATHER_CHUNKS = 3
CHIP_CHUNKS = 3
PAIR_CHUNKS = 4


def _shard_shape(name):
    r, c, ax = BIG[name]
    return (r // N_SHARD, c) if ax == 0 else (r, c // N_SHARD)


def _split_shards(name, full):
    r, c, ax = BIG[name]
    if ax == 0:
        return full.reshape(N_SHARD, -1)
    return full.reshape(r, N_SHARD, c // N_SHARD).transpose(1, 0, 2).reshape(N_SHARD, -1)


def _join_shards(name, parts):
    r, c, ax = BIG[name]
    if ax == 0:
        return parts.reshape(r, c)
    return parts.reshape(N_SHARD, r, c // N_SHARD).transpose(1, 0, 2).reshape(r, c)


HBM = pl.BlockSpec(memory_space=pl.ANY)


def _place():
    x, y, c = lax.axis_index("x"), lax.axis_index("y"), lax.axis_index("c")
    chips = [(1 - x, y), (x, 1 - y), (1 - x, 1 - y)]
    return x, y, c, chips


def _gather_weights(wp):
    R, C = wp.shape
    H = R // 2
    nq = GATHER_CHUNKS
    CH = H // nq

    def body(w_ref, out_ref, send_sems, recv_sems):
        x, y, c, chips = _place()
        sib = (x, y, 1 - c)

        def piece(k, hc, q):
            return out_ref.at[k, pl.ds(hc * H + q * CH, CH), :]

        def copy(j, src, dst, to):
            return pltpu.make_async_remote_copy(src_ref=src, dst_ref=dst, send_sem=send_sems.at[j],
                                                recv_sem=recv_sems.at[j], device_id=to, device_id_type=MESH)

        me = 2 * x + y
        sends = []
        for q in range(nq):
            for j, (px, py) in enumerate(chips):
                cp = copy(j * nq + q, w_ref.at[pl.ds(c * H + q * CH, CH), :], piece(me, c, q), (px, py, c))
                cp.start()
                sends.append(cp)
        fwds = []
        for q in range(nq):
            for j, (px, py) in enumerate(chips):
                k = 2 * px + py
                copy(j * nq + q, piece(k, c, q), piece(k, c, q), (px, py, c)).wait_recv()
                f = copy((3 + j) * nq + q, piece(k, c, q), piece(k, c, q), sib)
                f.start()
                fwds.append(f)
        for q in range(nq):
            for j, (px, py) in enumerate(chips):
                k = 2 * px + py
                copy((3 + j) * nq + q, piece(k, 1 - c, q), piece(k, 1 - c, q), sib).wait_recv()
        for cp in sends + fwds:
            cp.wait_send()

    out = pl.pallas_call(
        body, name="gather_weights", in_specs=[HBM], out_specs=HBM,
        out_shape=jax.ShapeDtypeStruct((N_SHARD, R, C), wp.dtype),
        scratch_shapes=[pltpu.SemaphoreType.DMA((6 * nq,)), pltpu.SemaphoreType.DMA((6 * nq,))],
    )(wp)
    me = 2 * lax.axis_index("x") + lax.axis_index("y")
    return lax.dynamic_update_slice(out, wp[None], (me, 0, 0))


def _pair_exchange(gp):
    n, R, C = gp.shape
    H = R // 2
    nq = PAIR_CHUNKS
    CH = H // nq

    def body(g_ref, theirs_ref, send_sems, recv_sems):
        x, y, c, _ = _place()
        swaps = []
        for k in range(n):
            for q in range(nq):
                cp = pltpu.make_async_remote_copy(
                    src_ref=g_ref.at[k, pl.ds((1 - c) * H + q * CH, CH), :], dst_ref=theirs_ref.at[k, pl.ds(q * CH, CH), :],
                    send_sem=send_sems.at[k * nq + q], recv_sem=recv_sems.at[k * nq + q], device_id=(x, y, 1 - c),
                    device_id_type=MESH)
                cp.start()
                swaps.append(cp)
        for cp in swaps:
            cp.wait()

    theirs = pl.pallas_call(
        body, name="pair_exchange", in_specs=[HBM], out_specs=HBM,
        out_shape=jax.ShapeDtypeStruct((n, H, C), gp.dtype),
        scratch_shapes=[pltpu.SemaphoreType.DMA((n * nq,)), pltpu.SemaphoreType.DMA((n * nq,))],
    )(gp)
    mine = lax.dynamic_slice(gp, (0, lax.axis_index("c") * H, 0), (n, H, C))
    return mine, theirs


def _chip_exchange(pb):
    n, H, C = pb.shape
    nq = CHIP_CHUNKS
    CH = H // nq

    def body(pb_ref, got_ref, send_sems, recv_sems):
        x, y, c, chips = _place()
        sends = []
        for q in range(nq):
            for j, (px, py) in enumerate(chips):
                cp = pltpu.make_async_remote_copy(
                    src_ref=pb_ref.at[2 * px + py, pl.ds(q * CH, CH), :], dst_ref=got_ref.at[j, pl.ds(q * CH, CH), :],
                    send_sem=send_sems.at[j * nq + q], recv_sem=recv_sems.at[j * nq + q],
                    device_id=(px, py, c), device_id_type=MESH)
                cp.start()
                sends.append(cp)
        for cp in sends:
            cp.wait()

    return pl.pallas_call(
        body, name="chip_exchange", in_specs=[HBM], out_specs=HBM,
        out_shape=jax.ShapeDtypeStruct((3, H, C), BF16),
        scratch_shapes=[pltpu.SemaphoreType.DMA((3 * nq,)), pltpu.SemaphoreType.DMA((3 * nq,))],
    )(pb)


def _pair_join(q):
    H, C = q.shape
    nq = PAIR_CHUNKS
    CH = H // nq

    def body(q_ref, theirs_ref, send_sems, recv_sems):
        x, y, c, _ = _place()
        pushes = []
        for j in range(nq):
            cp = pltpu.make_async_remote_copy(
                src_ref=q_ref.at[pl.ds(j * CH, CH), :], dst_ref=theirs_ref.at[pl.ds(j * CH, CH), :],
                send_sem=send_sems.at[j], recv_sem=recv_sems.at[j], device_id=(x, y, 1 - c), device_id_type=MESH)
            cp.start()
            pushes.append(cp)
        for cp in pushes:
            cp.wait()

    theirs = pl.pallas_call(
        body, name="pair_join", in_specs=[HBM], out_specs=HBM,
        out_shape=jax.ShapeDtypeStruct((H, C), F32),
        scratch_shapes=[pltpu.SemaphoreType.DMA((nq,)), pltpu.SemaphoreType.DMA((nq,))],
    )(q)
    c = lax.axis_index("c")
    out = jnp.zeros((2 * H, C), F32)
    out = lax.dynamic_update_slice(out, q, (c * H, 0))
    return lax.dynamic_update_slice(out, theirs, ((1 - c) * H, 0))


N_DEV = 8


def _small_all_reduce(g):
    r, cdim = g.shape

    def body(g_ref, out_ref, buf, send_sems, recv_sems):
        x, y, c, _ = _place()
        me = 4 * x + 2 * y + c
        buf[me] = g_ref[...]
        copies = []
        for d in range(1, N_DEV):
            to = me ^ d
            cp = pltpu.make_async_remote_copy(src_ref=g_ref, dst_ref=buf.at[me], send_sem=send_sems.at[d - 1],
                                              recv_sem=recv_sems.at[d - 1],
                                              device_id=(to // 4, (to // 2) % 2, to % 2), device_id_type=MESH)
            cp.start()
            copies.append(cp)
        for cp in copies:
            cp.wait()
        acc = buf[0]
        for d in range(1, N_DEV):
            acc = acc + buf[d]
        out_ref[...] = acc

    return pl.pallas_call(
        body, name="small_all_reduce",
        in_specs=[pl.BlockSpec(memory_space=pltpu.VMEM)], out_specs=pl.BlockSpec(memory_space=pltpu.VMEM),
        out_shape=jax.ShapeDtypeStruct((r, cdim), F32),
        scratch_shapes=[pltpu.VMEM((N_DEV, r, cdim), F32), pltpu.SemaphoreType.DMA((N_DEV - 1,)),
                        pltpu.SemaphoreType.DMA((N_DEV - 1,))],
    )(g)


def _adam(w, g, m, v, name):
    shape = w.shape
    w2, g2, m2, v2 = (t.reshape(-1, shape[-1]) for t in (w, g, m, v))
    d, mn, vn = _rowwise(_fn_adam, [w2, g2, m2, v2], [], [shape[-1]] * 3, tr=256, name=name)
    return d.reshape(shape), mn.reshape(shape), vn.reshape(shape)


def kernel(x, mem, positions, ln_in_g, ln_in_b, w_in, conv_w, conv_b, dt_bias, a_log, d_skip, ssd_norm_g, q_norm_g, w_q_up, kv_norm_g, w_kv_up, w_mix_out, ln1_g, ln1_b, w_mem_q, w_mem_k, w_mem_v, w_mem_o, ln2_g, ln2_b, w_up, w_down, ln3_g, ln3_b, loss_target, m_ln_in_g, m_ln_in_b, m_w_in, m_conv_w, m_conv_b, m_dt_bias, m_a_log, m_d_skip, m_ssd_norm_g, m_q_norm_g, m_w_q_up, m_kv_norm_g, m_w_kv_up, m_w_mix_out, m_ln1_g, m_ln1_b, m_w_mem_q, m_w_mem_k, m_w_mem_v, m_w_mem_o, m_ln2_g, m_ln2_b, m_w_up, m_w_down, m_ln3_g, m_ln3_b, v_ln_in_g, v_ln_in_b, v_w_in, v_conv_w, v_conv_b, v_dt_bias, v_a_log, v_d_skip, v_ssd_norm_g, v_q_norm_g, v_w_q_up, v_kv_norm_g, v_w_kv_up, v_w_mix_out, v_ln1_g, v_ln1_b, v_w_mem_q, v_w_mem_k, v_w_mem_v, v_w_mem_o, v_ln2_g, v_ln2_b, v_w_up, v_w_down, v_ln3_g, v_ln3_b):
    args = dict(locals())
    weights = BIG_ORDER + SMALL_ORDER

    flat = []
    for n in BIG_ORDER:
        s = args[n].reshape(-1)
        if n == "conv_w":
            flat.append(lax.bitcast_convert_type(s.astype(F32), BF16).reshape(-1))
        else:
            flat.append(s.astype(BF16))
    flat = jnp.concatenate(flat)
    wp = jnp.pad(flat, (0, PACK_ROWS * PACK_COLS - flat.shape[0])).reshape(PACK_ROWS, PACK_COLS)
    gathered = _gather_weights(wp).reshape(N_SHARD, -1)
    W, off = {}, 0
    for n in BIG_ORDER:
        sr, sc = _shard_shape(n)
        cnt = sr * sc
        if n == "conv_w":
            part = lax.bitcast_convert_type(gathered[:, off:off + 2 * cnt].reshape(N_SHARD, cnt, 2), F32)
            off += 2 * cnt
        else:
            part = gathered[:, off:off + cnt]
            off += cnt
        W[n] = _join_shards(n, part)
    P = {n: args[n] for n in SMALL_ORDER}
    P["conv_w"] = W.pop("conv_w")

    loss, grad_x, gbig, gsmall = _local_step(x[0], mem[0], positions[0], loss_target[0], W, P)
    loss = lax.psum(loss, ("x", "y", "c"))

    gflat = jnp.concatenate([_split_shards(n, gbig[n]) for n in BIG_ORDER], axis=1)
    gp = jnp.pad(gflat, ((0, 0), (0, PACK_ROWS * PACK_COLS - gflat.shape[1]))).reshape(N_SHARD, PACK_ROWS, PACK_COLS)
    mine, theirs = _pair_exchange(gp)
    pf, pb = _rowwise(_fn_add2, [mine.reshape(-1, PACK_COLS), theirs.reshape(-1, PACK_COLS)], [],
                      [PACK_COLS, (PACK_COLS, BF16)], tr=288, name="pair_sum")
    pf = pf.reshape(N_SHARD, HALF_ROWS, PACK_COLS)
    pb = pb.reshape(N_SHARD, HALF_ROWS, PACK_COLS)
    got = _chip_exchange(pb).reshape(3 * HALF_ROWS, PACK_COLS)
    own = lax.dynamic_index_in_dim(pf, 2 * lax.axis_index("x") + lax.axis_index("y"), axis=0, keepdims=False)
    (q,) = _rowwise(_fn_add4, [own] + [(got, 0, PACK_COLS, j * HALF_ROWS) for j in range(3)], [], [PACK_COLS],
                    tr=288, name="chip_sum", n_rows=HALF_ROWS)
    red = _pair_join(q).reshape(-1)

    gs = jnp.concatenate([_row(gsmall[n], PACK_COLS) for n in SMALL_ORDER] + [jnp.zeros((1, PACK_COLS), F32)], axis=0)
    gs = _small_all_reduce(gs)

    grads, deltas, new_m, new_v = {}, {}, {}, {}
    off = 0
    for n in BIG_ORDER:
        sr, sc = _shard_shape(n)
        g = red[off:off + sr * sc].reshape(args[n].shape)
        off += sr * sc
        grads[n] = g
        deltas[n], new_m[n], new_v[n] = _adam(args[n], g, args["m_" + n], args["v_" + n], "adam_" + n)
    pack = lambda pre: jnp.concatenate([_row(args[pre + n], PACK_COLS) for n in SMALL_ORDER]
                                       + [jnp.zeros((1, PACK_COLS), F32)], axis=0)
    ds, ms, vs = _rowwise(_fn_adam, [pack(""), gs, pack("m_"), pack("v_")], [], [PACK_COLS] * 3, tr=16,
                          name="adam_small")
    for i, n in enumerate(SMALL_ORDER):
        cnt = args[n].size
        take = lambda t: t[i, :cnt].reshape(args[n].shape)
        grads[n], deltas[n], new_m[n], new_v[n] = take(gs), take(ds), take(ms), take(vs)

    order = ["ln_in_g", "ln_in_b", "w_in", "conv_w", "conv_b", "dt_bias", "a_log", "d_skip", "ssd_norm_g",
             "q_norm_g", "w_q_up", "kv_norm_g", "w_kv_up", "w_mix_out", "ln1_g", "ln1_b", "w_mem_q", "w_mem_k",
             "w_mem_v", "w_mem_o", "ln2_g", "ln2_b", "w_up", "w_down", "ln3_g", "ln3_b"]
    assert sorted(order) == sorted(weights)
    return (loss, grad_x[None], *[grads[n] for n in order], *[deltas[n] for n in order],
            *[new_m[n] for n in order], *[new_v[n] for n in order])
```

```python
import functools
import math

import jax
import jax.numpy as jnp
import numpy as np
from jax import lax
from jax.experimental import pallas as pl
from jax.experimental.pallas import tpu as pltpu

F32 = jnp.float32
BF16 = jnp.bfloat16
MESH = pl.DeviceIdType.MESH

D_MODEL = 1024
SSD_HEADS = 8
SSD_INNER = 512
SSD_CHUNK = 128
SSD_STATE = 128
MLA_HEADS = 8
MLA_NOPE = 64
MLA_ROPE = 32
MLA_QK = 96
MLA_Q_RANK = 384
MLA_KV_RANK = 256
ROPE_THETA = 10000.0
MEM_HEADS = 4
MEM_HEAD_DIM = 256
LN_EPS = 1e-5
RMS_EPS = 1e-6
ALPHA = 2.0 ** 0.25
ADAM_LR = 0.001
ADAM_B1 = 0.9
ADAM_B2 = 0.999
ADAM_EPS = 1e-08
ADAM_WD = 0.01
ADAM_STEP = 10

LANES = 128
IN_W = 2560
SEG_XBC = (0, 1024)
SEG_Z = (1024, 512)
SEG_QLAT = (1536, 384)
SEG_DT = (1920, 128)
SEG_KVLAT = (2048, 256)
SEG_KR = (2304, 128)
VMEM_LIMIT = 56 * 1024 * 1024
ATTN_TILE = 512
ROW_TILE = 256
NEG = -1e30

NN = (((1,), (0,)), ((), ()))
NT = (((1,), (1,)), ((), ()))
TN = (((0,), (0,)), ((), ()))


def _dot(a, b, dims=NN):
    return lax.dot_general(a.astype(BF16), b.astype(BF16), dims, preferred_element_type=F32)


def _dot_exact(a, b):
    return lax.dot_general(a, b, NN, precision=lax.Precision.HIGHEST, preferred_element_type=F32)


def _pick(dim, pref):
    t = min(pref, dim)
    t -= t % LANES
    while t >= LANES:
        if dim % t == 0:
            return t
        t -= LANES
    return dim


def _params(sem):
    return pltpu.CompilerParams(dimension_semantics=sem, vmem_limit_bytes=VMEM_LIMIT)


def _mm(a, b, *, form, name, a_pro=None, out_dtype=F32, tm=1024, tn=1024, tk=1024):
    if form == "nn":
        (m, k), (_, n) = a.shape, b.shape
    elif form == "nt":
        (m, k), (n, _) = a.shape, b.shape
    else:
        (k, m), (_, n) = a.shape, b.shape
    tm, tn, tk = _pick(m, tm), _pick(n, tn), _pick(k, tk)
    dims = {"nn": NN, "nt": NT, "tn": TN}[form]
    nk = k // tk
    direct = out_dtype == F32

    def body(a_ref, b_ref, o_ref, *scratch):
        acc_ref = o_ref if direct else scratch[0]

        @pl.when(pl.program_id(2) == 0)
        def _():
            acc_ref[...] = jnp.zeros_like(acc_ref)

        av = a_ref[...]
        if a_pro is not None:
            av = a_pro(av)
        acc_ref[...] += _dot(av, b_ref[...], dims)
        if not direct:
            @pl.when(pl.program_id(2) == nk - 1)
            def _():
                o_ref[...] = acc_ref[...].astype(o_ref.dtype)

    if form == "tn":
        a_spec = pl.BlockSpec((tk, tm), lambda i, j, kk: (kk, i))
    else:
        a_spec = pl.BlockSpec((tm, tk), lambda i, j, kk: (i, kk))
    if form == "nt":
        b_spec = pl.BlockSpec((tn, tk), lambda i, j, kk: (j, kk))
    else:
        b_spec = pl.BlockSpec((tk, tn), lambda i, j, kk: (kk, j))
    return pl.pallas_call(
        body, name=name, grid=(m // tm, n // tn, k // tk),
        in_specs=[a_spec, b_spec],
        out_specs=pl.BlockSpec((tm, tn), lambda i, j, kk: (i, j)),
        out_shape=jax.ShapeDtypeStruct((m, n), out_dtype),
        scratch_shapes=[] if direct else [pltpu.VMEM((tm, tn), F32)],
        compiler_params=_params(("parallel", "parallel", "arbitrary")),
    )(a, b)


class _Ctx:
    def __init__(self, i, n):
        self.i, self.n = i, n


def _rowwise(fn, rows, consts, row_outs, acc_outs=(), *, tr, name, n_rows=None):
    norm = []
    for r in rows:
        kind = "tile"
        if isinstance(r, tuple) and isinstance(r[0], str):
            kind, r = r[0], r[1:]
        row0 = 0
        if isinstance(r, tuple) and len(r) == 4:
            r, row0 = r[:3], r[3]
        arr, col0, width = r if isinstance(r, tuple) else (r, 0, r.shape[1])
        assert col0 % width == 0
        norm.append((kind, arr, col0 // width, width, row0))
    n_rows = n_rows or next(a.shape[0] for k, a, _, _, _ in norm if k == "tile")
    tr = min(tr, n_rows)
    while n_rows % tr:
        tr -= 8
    n = n_rows // tr
    arrs, specs = [], []
    for kind, arr, cb, width, row0 in norm:
        if kind == "tile":
            assert row0 % tr == 0
            specs.append(pl.BlockSpec((tr, width), lambda i, cb=cb, rb=row0 // tr: (i + rb, cb)))
        elif kind == "prev":
            specs.append(pl.BlockSpec((8, width), lambda i, cb=cb: (jnp.maximum(i * (tr // 8) - 1, 0), cb)))
        else:
            specs.append(pl.BlockSpec((8, width), lambda i, cb=cb: (jnp.minimum((i + 1) * (tr // 8), n_rows // 8 - 1), cb)))
        arrs.append(arr)
    for c in consts:
        specs.append(pl.BlockSpec(c.shape, lambda i, nd=c.ndim: (0,) * nd))
        arrs.append(c)
    n_in, n_ro = len(arrs), len(row_outs)
    row_outs = [w if isinstance(w, tuple) else (w, F32) for w in row_outs]
    out_shape = [jax.ShapeDtypeStruct((n_rows, w), dt) for w, dt in row_outs]
    out_specs = [pl.BlockSpec((tr, w), lambda i: (i, 0)) for w, _ in row_outs]
    out_shape += [jax.ShapeDtypeStruct(s, F32) for s in acc_outs]
    out_specs += [pl.BlockSpec(s, lambda i: (0, 0)) for s in acc_outs]

    def body(*refs):
        i = pl.program_id(0)
        vals = [r[...] for r in refs[:n_in]]
        outs = fn(_Ctx(i, n), *vals)
        if not isinstance(outs, (tuple, list)):
            outs = (outs,)
        o_refs = refs[n_in:]
        for o_ref, o in zip(o_refs[:n_ro], outs[:n_ro]):
            o_ref[...] = o.astype(o_ref.dtype)
        if acc_outs:
            @pl.when(i == 0)
            def _():
                for o_ref in o_refs[n_ro:]:
                    o_ref[...] = jnp.zeros_like(o_ref)

            for o_ref, o in zip(o_refs[n_ro:], outs[n_ro:]):
                o_ref[...] += jnp.broadcast_to(o, o_ref.shape)

    res = pl.pallas_call(
        body, name=name, grid=(n,), in_specs=specs, out_specs=out_specs, out_shape=out_shape,
        compiler_params=_params(("arbitrary",)),
    )(*arrs)
    return res


def _sum0(v):
    return jnp.sum(v, axis=0, keepdims=True)


def _mean1(v):
    return jnp.mean(v, axis=-1, keepdims=True)


def _sigmoid(v):
    return 1.0 / (1.0 + jnp.exp(-v))


def _ln_stats(t):
    xc = t - _mean1(t)
    rstd = lax.rsqrt(_mean1(xc * xc) + LN_EPS)
    return xc * rstd, rstd


def _ln_bwd(xhat, rstd, dy, g):
    dxh = dy * g
    dx = rstd * (dxh - _mean1(dxh) - xhat * _mean1(dxh * xhat))
    return dx, _sum0(dy * xhat), _sum0(dy)


def _rms_fwd(v, g):
    return v * lax.rsqrt(_mean1(v * v) + RMS_EPS) * g


def _rms_bwd(v, dy, g):
    rs = lax.rsqrt(_mean1(v * v) + RMS_EPS)
    vh = v * rs
    dyg = dy * g
    return rs * (dyg - vh * _mean1(dyg * vh)), _sum0(dy * vh)


def _lane(shape):
    return lax.broadcasted_iota(jnp.int32, shape, len(shape) - 1)


def _shift_down(u, halo, s, is_first):
    tr = u.shape[0]
    rolled = pltpu.roll(u, s, 0)
    hr = jnp.where(is_first, 0.0, pltpu.roll(halo, s, 0))
    row = lax.broadcasted_iota(jnp.int32, hr.shape, 0)
    top = jnp.where(row < s, hr, rolled[0:8])
    if tr == 8:
        return top
    return jnp.concatenate([top, rolled[8:]], axis=0)


def _shift_up(d, halo, s, is_last):
    tr = d.shape[0]
    rolled = pltpu.roll(d, tr - s, 0)
    hr = jnp.where(is_last, 0.0, pltpu.roll(halo, 8 - s, 0))
    row = lax.broadcasted_iota(jnp.int32, hr.shape, 0)
    bot = jnp.where(row >= 8 - s, hr, rolled[tr - 8:])
    if tr == 8:
        return bot
    return jnp.concatenate([rolled[:tr - 8], bot], axis=0)


def _rope(v, ta, tb, tc):
    return v * ta + pltpu.roll(v, 16, 1) * tb + pltpu.roll(v, LANES - 16, 1) * tc


def _rope_bwd(d, ta, tb, tc):
    return d * ta + pltpu.roll(d * tb, LANES - 16, 1) + pltpu.roll(d * tc, 16, 1)


def _ssd_common(dtv, a_row):
    L = SSD_CHUNK
    a = dtv * a_row
    r = lax.broadcasted_iota(jnp.int32, (L, L), 0)
    c = lax.broadcasted_iota(jnp.int32, (L, L), 1)
    tril = r >= c
    cs = _dot_exact(tril.astype(F32), a)
    cs_t = cs.T
    cs_last = cs[L - 1:L, :]
    return dict(a=a, tril=tril, cs=cs, cs_t=cs_t, ecs=jnp.exp(cs), dte=jnp.exp(cs_last - cs),
                elast=jnp.exp(cs_last))


def _pair_sel(v, h0, lo):
    return jnp.where(lo, v[:, h0:h0 + 1], v[:, h0 + 1:h0 + 2])


def _ssd_pair(cm, h0, cb, xp, dtv, bmat, cmat, hp, lo):
    L = SSD_CHUNK
    x = xp * _pair_sel(dtv, h0, lo)
    lam0 = jnp.exp(jnp.where(cm["tril"], cm["cs"][:, h0:h0 + 1] - cm["cs_t"][h0:h0 + 1, :], NEG))
    lam1 = jnp.exp(jnp.where(cm["tril"], cm["cs"][:, h0 + 1:h0 + 2] - cm["cs_t"][h0 + 1:h0 + 2, :], NEG))
    m0, m1 = cb * lam0, cb * lam1
    ydiag = jnp.where(lo, _dot(m0, x), _dot(m1, x))
    ecs_p = _pair_sel(cm["ecs"], h0, lo)
    dte_p = _pair_sel(cm["dte"], h0, lo)
    yoff = _dot(cmat, hp, NT) * ecs_p
    xd = x * dte_p
    st = _dot(xd, bmat, TN)
    rlo = lax.broadcasted_iota(jnp.int32, (LANES, SSD_STATE), 0) < 64
    decay = jnp.where(rlo, cm["elast"][:, h0:h0 + 1], cm["elast"][:, h0 + 1:h0 + 2])
    h_next = hp * decay + st
    return dict(x=x, lam0=lam0, lam1=lam1, m0=m0, m1=m1, y=ydiag + yoff, yoff=yoff, ecs_p=ecs_p, dte_p=dte_p,
                xd=xd, decay=decay, h_next=h_next)


def _ssd_fwd(xbc, dt, a_row, *, name):
    S = xbc.shape[0]
    L = SSD_CHUNK
    nc = S // L

    def body(xs_ref, bm_ref, cm_ref, dt_ref, a_ref, y_ref, hs_ref, h_scr):
        @pl.when(pl.program_id(0) == 0)
        def _():
            h_scr[...] = jnp.zeros_like(h_scr)

        dtv = dt_ref[...]
        cm = _ssd_common(dtv, a_ref[...])
        lo = _lane((L, LANES)) < 64
        ys = []
        for g in range(2):
            bmat = bm_ref[:, g * 128:(g + 1) * 128]
            cmat = cm_ref[:, g * 128:(g + 1) * 128]
            cb = _dot(cmat, bmat, NT)
            for pr in range(2):
                p4 = 2 * g + pr
                hp = h_scr[p4]
                hs_ref[0, p4 * 128:(p4 + 1) * 128, :] = hp
                t = _ssd_pair(cm, 2 * p4, cb, xs_ref[:, p4 * 128:(p4 + 1) * 128], dtv, bmat, cmat, hp, lo)
                ys.append(t["y"])
                h_scr[p4] = t["h_next"]
        y_ref[...] = jnp.concatenate(ys, axis=1)

    return pl.pallas_call(
        body, name=name, grid=(nc,),
        in_specs=[pl.BlockSpec((L, 512), lambda c: (c, 0)), pl.BlockSpec((L, 256), lambda c: (c, 2)),
                  pl.BlockSpec((L, 256), lambda c: (c, 3)), pl.BlockSpec((L, 128), lambda c: (c, 0)),
                  pl.BlockSpec((1, 128), lambda c: (0, 0))],
        out_specs=[pl.BlockSpec((L, 512), lambda c: (c, 0)), pl.BlockSpec((1, 512, 128), lambda c: (c, 0, 0))],
        out_shape=[jax.ShapeDtypeStruct((S, 512), F32), jax.ShapeDtypeStruct((nc, 512, 128), F32)],
        scratch_shapes=[pltpu.VMEM((4, 128, 128), F32)],
        compiler_params=_params(("arbitrary",)),
    )(xbc, xbc, xbc, dt, a_row)


def _ssd_bwd(xbc, dt, a_row, hs, dy, *, name):
    S = xbc.shape[0]
    L = SSD_CHUNK
    nc = S // L

    def body(xs_ref, bm_ref, cm_ref, dt_ref, a_ref, hs_ref, dy_ref, dxs_ref, dbc_ref, ddt_ref, da_ref, g_scr):
        @pl.when(pl.program_id(0) == 0)
        def _():
            g_scr[...] = jnp.zeros_like(g_scr)
            da_ref[...] = jnp.zeros_like(da_ref)

        dtv = dt_ref[...]
        a_row_v = a_ref[...]
        cm = _ssd_common(dtv, a_row_v)
        lo = _lane((L, LANES)) < 64
        lane_row = _lane((1, LANES))
        ri = lax.broadcasted_iota(jnp.int32, (L, L), 0)
        ci = lax.broadcasted_iota(jnp.int32, (L, L), 1)
        triu = (ri <= ci).astype(F32)
        stril = ri > ci

        def halves(v, mask):
            return (jnp.sum(jnp.where(mask, v, 0.0), axis=1, keepdims=True),
                    jnp.sum(jnp.where(mask, 0.0, v), axis=1, keepdims=True))

        i_all = jnp.zeros((L, LANES), F32)
        yo_all = jnp.zeros((L, LANES), F32)
        w_all = jnp.zeros((L, LANES), F32)
        ddt_x = jnp.zeros((L, LANES), F32)
        e_row = jnp.zeros((1, LANES), F32)
        rlo = lax.broadcasted_iota(jnp.int32, (LANES, SSD_STATE), 0) < 64
        dxs, dbs, dcs = [], [], []
        for g in range(2):
            bmat = bm_ref[:, g * 128:(g + 1) * 128]
            cmat = cm_ref[:, g * 128:(g + 1) * 128]
            cb = _dot(cmat, bmat, NT)
            dcb = jnp.zeros((L, L), F32)
            db = jnp.zeros((L, SSD_STATE), F32)
            dc = jnp.zeros((L, SSD_STATE), F32)
            for pr in range(2):
                p4 = 2 * g + pr
                h0 = 2 * p4
                hp = hs_ref[0, p4 * 128:(p4 + 1) * 128, :]
                xp = xs_ref[:, p4 * 128:(p4 + 1) * 128]
                t = _ssd_pair(cm, h0, cb, xp, dtv, bmat, cmat, hp, lo)
                gst = g_scr[p4]
                dyp = dy_ref[:, p4 * 128:(p4 + 1) * 128]
                dy0 = jnp.where(lo, dyp, 0.0)
                dy1 = dyp - dy0
                bg = _dot(bmat, gst, NT)
                dx = _dot(t["m0"], dy0, TN) + _dot(t["m1"], dy1, TN) + bg * t["dte_p"]
                dm0, dm1 = _dot(dy0, t["x"], NT), _dot(dy1, t["x"], NT)
                dcb = dcb + dm0 * t["lam0"] + dm1 * t["lam1"]
                dye = dyp * t["ecs_p"]
                dc = dc + _dot(dye, hp)
                db = db + _dot(t["xd"], gst)
                i0 = jnp.sum(jnp.where(stril, _dot(triu, dm0 * t["m0"]), 0.0), axis=1, keepdims=True)
                i1 = jnp.sum(jnp.where(stril, _dot(triu, dm1 * t["m1"]), 0.0), axis=1, keepdims=True)
                yo0, yo1 = halves(dyp * t["yoff"], lo)
                w0, w1 = halves(t["xd"] * bg, lo)
                gh = gst * (hp * t["decay"])
                e0 = _sum0(jnp.sum(jnp.where(rlo, gh, 0.0), axis=1, keepdims=True))
                e1 = _sum0(jnp.sum(jnp.where(rlo, 0.0, gh), axis=1, keepdims=True))
                x0, x1 = halves(dx * xp, lo)
                oh0 = (lane_row == h0).astype(F32)
                oh1 = (lane_row == h0 + 1).astype(F32)
                i_all = i_all + i0 * oh0 + i1 * oh1
                yo_all = yo_all + yo0 * oh0 + yo1 * oh1
                w_all = w_all + w0 * oh0 + w1 * oh1
                e_row = e_row + e0 * oh0 + e1 * oh1
                ddt_x = ddt_x + x0 * oh0 + x1 * oh1
                dxs.append(dx * _pair_sel(dtv, h0, lo))
                g_scr[p4] = gst * t["decay"] + _dot(dye, cmat, TN)
            dbs.append(db + _dot(dcb, cmat, TN))
            dcs.append(dc + _dot(dcb, bmat))
        da = i_all + _dot_exact(triu, yo_all) + _dot_exact(stril.astype(F32), w_all) + e_row
        ddt_ref[...] = da * a_row_v + ddt_x
        da_ref[...] += _sum0(da * dtv)
        dxs_ref[...] = jnp.concatenate(dxs, axis=1)
        dbc_ref[...] = jnp.concatenate(dbs + dcs, axis=1)

    rev = lambda c: nc - 1 - c
    return pl.pallas_call(
        body, name=name, grid=(nc,),
        in_specs=[pl.BlockSpec((L, 512), lambda c: (rev(c), 0)), pl.BlockSpec((L, 256), lambda c: (rev(c), 2)),
                  pl.BlockSpec((L, 256), lambda c: (rev(c), 3)), pl.BlockSpec((L, 128), lambda c: (rev(c), 0)),
                  pl.BlockSpec((1, 128), lambda c: (0, 0)), pl.BlockSpec((1, 512, 128), lambda c: (rev(c), 0, 0)),
                  pl.BlockSpec((L, 512), lambda c: (rev(c), 0))],
        out_specs=[pl.BlockSpec((L, 512), lambda c: (rev(c), 0)), pl.BlockSpec((L, 512), lambda c: (rev(c), 0)),
                   pl.BlockSpec((L, 128), lambda c: (rev(c), 0)), pl.BlockSpec((1, 128), lambda c: (0, 0))],
        out_shape=[jax.ShapeDtypeStruct((S, 512), F32), jax.ShapeDtypeStruct((S, 512), F32),
                   jax.ShapeDtypeStruct((S, 128), F32), jax.ShapeDtypeStruct((1, 128), F32)],
        scratch_shapes=[pltpu.VMEM((4, 128, 128), F32)],
        compiler_params=_params(("arbitrary",)),
    )(xbc, xbc, xbc, dt, a_row, hs, dy)


MLA_SCALE = MLA_QK ** -0.5


def _causal_scores(q, k, qi, ki, t):
    s = _dot(q, k, NT) * MLA_SCALE
    row = qi * t + lax.broadcasted_iota(jnp.int32, (t, t), 0)
    col = ki * t + lax.broadcasted_iota(jnp.int32, (t, t), 1)
    return jnp.where(col <= row, s, NEG)


def _mla_fwd(q, k, kv, *, name):
    S = q.shape[0]
    t = min(ATTN_TILE, S)
    nq = S // t

    def body(q_ref, k_ref, v_ref, o_ref, lse_ref, m_scr, l_scr, acc_scr):
        qi, ki = pl.program_id(1), pl.program_id(2)

        @pl.when(ki == 0)
        def _():
            m_scr[...] = jnp.full_like(m_scr, NEG)
            l_scr[...] = jnp.zeros_like(l_scr)
            acc_scr[...] = jnp.zeros_like(acc_scr)

        @pl.when(ki <= qi)
        def _():
            s = _causal_scores(q_ref[...], k_ref[...], qi, ki, t)
            m_old = m_scr[:, 0:1]
            m_new = jnp.maximum(m_old, jnp.max(s, axis=1, keepdims=True))
            p = jnp.exp(s - m_new)
            corr = jnp.exp(m_old - m_new)
            l_scr[...] = jnp.broadcast_to(corr * l_scr[:, 0:1] + jnp.sum(p, axis=1, keepdims=True), l_scr.shape)
            acc_scr[...] = corr * acc_scr[...] + _dot(p, v_ref[...])
            m_scr[...] = jnp.broadcast_to(m_new, m_scr.shape)

        @pl.when(ki == nq - 1)
        def _():
            l = l_scr[:, 0:1]
            o_ref[...] = acc_scr[...] / l
            lse_ref[0] = jnp.broadcast_to(m_scr[:, 0:1] + jnp.log(l), (t, LANES))

    return pl.pallas_call(
        body, name=name, grid=(MLA_HEADS, nq, nq),
        in_specs=[pl.BlockSpec((t, 128), lambda h, qi, ki: (qi, h)),
                  pl.BlockSpec((t, 128), lambda h, qi, ki: (jnp.minimum(ki, qi), h)),
                  pl.BlockSpec((t, 128), lambda h, qi, ki: (jnp.minimum(ki, qi), 2 * h + 1))],
        out_specs=[pl.BlockSpec((t, 128), lambda h, qi, ki: (qi, h)),
                   pl.BlockSpec((1, t, 128), lambda h, qi, ki: (h, qi, 0))],
        out_shape=[jax.ShapeDtypeStruct((S, MLA_HEADS * 128), F32), jax.ShapeDtypeStruct((MLA_HEADS, S, 128), F32)],
        scratch_shapes=[pltpu.VMEM((t, 128), F32), pltpu.VMEM((t, 128), F32), pltpu.VMEM((t, 128), F32)],
        compiler_params=_params(("parallel", "parallel", "arbitrary")),
    )(q, k, kv)


def _mla_bwd_dkv(q, k, kv, o, do, lse, *, name):
    S = q.shape[0]
    t = min(ATTN_TILE, S)
    nq = S // t

    def body(q_ref, k_ref, v_ref, o_ref, do_ref, lse_ref, dkv_ref):
        ki, qi = pl.program_id(1), pl.program_id(2)

        @pl.when(qi == 0)
        def _():
            dkv_ref[...] = jnp.zeros_like(dkv_ref)

        @pl.when(qi >= ki)
        def _():
            qv, dov = q_ref[...], do_ref[...]
            s = _causal_scores(qv, k_ref[...], qi, ki, t)
            p = jnp.exp(s - lse_ref[0][:, 0:1])
            dv = _dot(p, dov, TN)
            dp = _dot(dov, v_ref[...], NT)
            delta = jnp.sum(dov * o_ref[...], axis=1, keepdims=True)
            ds = p * (dp - delta) * MLA_SCALE
            dkv_ref[...] += jnp.concatenate([_dot(ds, qv, TN), dv], axis=1)

    qmap = lambda h, ki, qi: (jnp.maximum(qi, ki), h)
    return pl.pallas_call(
        body, name=name, grid=(MLA_HEADS, nq, nq),
        in_specs=[pl.BlockSpec((t, 128), qmap),
                  pl.BlockSpec((t, 128), lambda h, ki, qi: (ki, h)),
                  pl.BlockSpec((t, 128), lambda h, ki, qi: (ki, 2 * h + 1)),
                  pl.BlockSpec((t, 128), qmap), pl.BlockSpec((t, 128), qmap),
                  pl.BlockSpec((1, t, 128), lambda h, ki, qi: (h, jnp.maximum(qi, ki), 0))],
        out_specs=pl.BlockSpec((t, 256), lambda h, ki, qi: (ki, h)),
        out_shape=jax.ShapeDtypeStruct((S, MLA_HEADS * 256), F32),
        compiler_params=_params(("parallel", "parallel", "arbitrary")),
    )(q, k, kv, o, do, lse)


def _mla_bwd_dq(q, k, kv, o, do, lse, *, name):
    S = q.shape[0]
    t = min(ATTN_TILE, S)
    nq = S // t

    def body(q_ref, k_ref, v_ref, o_ref, do_ref, lse_ref, dq_ref):
        qi, ki = pl.program_id(1), pl.program_id(2)

        @pl.when(ki == 0)
        def _():
            dq_ref[...] = jnp.zeros_like(dq_ref)

        @pl.when(ki <= qi)
        def _():
            dov, kv = do_ref[...], k_ref[...]
            s = _causal_scores(q_ref[...], kv, qi, ki, t)
            p = jnp.exp(s - lse_ref[0][:, 0:1])
            dp = _dot(dov, v_ref[...], NT)
            delta = jnp.sum(dov * o_ref[...], axis=1, keepdims=True)
            ds = p * (dp - delta) * MLA_SCALE
            dq_ref[...] += _dot(ds, kv)

    qmap = lambda h, qi, ki: (qi, h)
    return pl.pallas_call(
        body, name=name, grid=(MLA_HEADS, nq, nq),
        in_specs=[pl.BlockSpec((t, 128), qmap),
                  pl.BlockSpec((t, 128), lambda h, qi, ki: (jnp.minimum(ki, qi), h)),
                  pl.BlockSpec((t, 128), lambda h, qi, ki: (jnp.minimum(ki, qi), 2 * h + 1)),
                  pl.BlockSpec((t, 128), qmap), pl.BlockSpec((t, 128), qmap),
                  pl.BlockSpec((1, t, 128), lambda h, qi, ki: (h, qi, 0))],
        out_specs=pl.BlockSpec((t, 128), qmap),
        out_shape=jax.ShapeDtypeStruct((S, MLA_HEADS * 128), F32),
        compiler_params=_params(("parallel", "parallel", "arbitrary")),
    )(q, k, kv, o, do, lse)


def _att_mask(s_t, qi, kb, t):
    krow = kb * t + lax.broadcasted_iota(jnp.int32, (t, t), 0)
    qcol = qi * t + lax.broadcasted_iota(jnp.int32, (t, t), 1)
    return jnp.where(krow <= qcol, s_t, NEG)


def _rows(ref, blk, t):
    return ref[pl.ds(pl.multiple_of(blk * t, t), t), :]


def _cols(ref, blk, t):
    return ref[:, pl.ds(pl.multiple_of(blk * t, t), t)]


def _attn_fwd(q, k, v_t, *, name):
    S = q.shape[0]
    t = min(ATTN_TILE, S)
    nq = S // t

    def body(q_ref, k_ref, vt_ref, o_ref, lse_ref):
        qi = pl.program_id(1)
        qv = q_ref[...]

        def step(kb, carry, masked):
            m, l, acc = carry
            s_t = lax.dot_general(_rows(k_ref, kb, t), qv, NT, preferred_element_type=F32)
            if masked:
                s_t = _att_mask(s_t, qi, kb, t)
            m_new = jnp.maximum(m, jnp.max(s_t, axis=0, keepdims=True))
            p_t = jnp.exp(s_t - m_new)
            corr = jnp.exp(m - m_new)
            l = corr * l + jnp.sum(p_t, axis=0, keepdims=True)
            acc = corr * acc + lax.dot_general(_cols(vt_ref, kb, t), p_t.astype(BF16), NN, preferred_element_type=F32)
            return m_new, l, acc

        init = (jnp.full((1, t), NEG, F32), jnp.zeros((1, t), F32), jnp.zeros((LANES, t), F32))
        carry = lax.fori_loop(0, qi, lambda kb, c: step(kb, c, False), init)
        m, l, acc = step(qi, carry, True)
        o_ref[...] = acc / l
        lse_ref[0] = m + jnp.log(l)

    return pl.pallas_call(
        body, name=name, grid=(MLA_HEADS, nq),
        in_specs=[pl.BlockSpec((t, LANES), lambda h, qi: (qi, h)),
                  pl.BlockSpec((S, LANES), lambda h, qi: (0, h)),
                  pl.BlockSpec((LANES, S), lambda h, qi: (h, 0))],
        out_specs=[pl.BlockSpec((LANES, t), lambda h, qi: (h, qi)),
                   pl.BlockSpec((1, 1, t), lambda h, qi: (h, 0, qi))],
        out_shape=[jax.ShapeDtypeStruct((MLA_HEADS * LANES, S), F32), jax.ShapeDtypeStruct((MLA_HEADS, 1, S), F32)],
        compiler_params=_params(("parallel", "arbitrary")),
    )(q, k, v_t)


def _attn_bwd_dq(q, k, k_t, v, o_t, do_t, lse, *, name):
    S = q.shape[0]
    t = min(ATTN_TILE, S)
    nq = S // t

    def body(q_ref, k_ref, kt_ref, v_ref, o_ref, do_ref, lse_ref, dq_ref, delta_ref):
        qi = pl.program_id(1)
        qv = q_ref[...]
        dov = do_ref[...]
        delta = jnp.sum(dov * o_ref[...], axis=0, keepdims=True)
        delta_ref[0] = delta
        dob = dov.astype(BF16)
        lse_v = lse_ref[0]

        def step(kb, acc, masked):
            s_t = lax.dot_general(_rows(k_ref, kb, t), qv, NT, preferred_element_type=F32)
            if masked:
                s_t = _att_mask(s_t, qi, kb, t)
            p_t = jnp.exp(s_t - lse_v)
            dp_t = lax.dot_general(_rows(v_ref, kb, t), dob, NN, preferred_element_type=F32)
            ds_t = (p_t * (dp_t - delta)).astype(BF16)
            return acc + lax.dot_general(_cols(kt_ref, kb, t), ds_t, NN, preferred_element_type=F32)

        acc = lax.fori_loop(0, qi, lambda kb, c: step(kb, c, False), jnp.zeros((LANES, t), F32))
        dq_ref[...] = step(qi, acc, True)

    tile_t = pl.BlockSpec((LANES, t), lambda h, qi: (h, qi))
    stat = pl.BlockSpec((1, 1, t), lambda h, qi: (h, 0, qi))
    seq = pl.BlockSpec((S, LANES), lambda h, qi: (0, h))
    return pl.pallas_call(
        body, name=name, grid=(MLA_HEADS, nq),
        in_specs=[pl.BlockSpec((t, LANES), lambda h, qi: (qi, h)), seq,
                  pl.BlockSpec((LANES, S), lambda h, qi: (h, 0)), seq, tile_t, tile_t, stat],
        out_specs=[tile_t, stat],
        out_shape=[jax.ShapeDtypeStruct((MLA_HEADS * LANES, S), F32), jax.ShapeDtypeStruct((MLA_HEADS, 1, S), F32)],
        compiler_params=_params(("parallel", "arbitrary")),
    )(q, k, k_t, v, o_t, do_t, lse)


def _attn_bwd_dkv(q, q_t, k, v, do_t, lse, delta, *, name):
    S = q.shape[0]
    t = min(ATTN_TILE, S)
    nq = S // t

    def body(q_ref, qt_ref, k_ref, v_ref, do_ref, lse_ref, delta_ref, dk_ref, dv_ref):
        ki = pl.program_id(1)
        kv, vv = k_ref[...], v_ref[...]

        def step(qb, carry, masked):
            dk, dv = carry
            s_t = lax.dot_general(kv, _rows(q_ref, qb, t), NT, preferred_element_type=F32)
            if masked:
                s_t = _att_mask(s_t, qb, ki, t)
            p_t = jnp.exp(s_t - _cols(lse_ref.at[0], qb, t))
            dob = _cols(do_ref, qb, t).astype(BF16)
            dv = dv + lax.dot_general(dob, p_t.astype(BF16), NT, preferred_element_type=F32)
            dp_t = lax.dot_general(vv, dob, NN, preferred_element_type=F32)
            ds_t = (p_t * (dp_t - _cols(delta_ref.at[0], qb, t))).astype(BF16)
            dk = dk + lax.dot_general(_cols(qt_ref, qb, t), ds_t, NT, preferred_element_type=F32)
            return dk, dv

        zero = jnp.zeros((LANES, t), F32)
        carry = step(ki, (zero, zero), True)
        dk, dv = lax.fori_loop(ki + 1, nq, lambda qb, c: step(qb, c, False), carry)
        dk_ref[...] = dk
        dv_ref[...] = dv

    tile = pl.BlockSpec((t, LANES), lambda h, ki: (ki, h))
    tile_t = pl.BlockSpec((LANES, t), lambda h, ki: (h, ki))
    seq = pl.BlockSpec((S, LANES), lambda h, ki: (0, h))
    seq_t = pl.BlockSpec((LANES, S), lambda h, ki: (h, 0))
    stat = pl.BlockSpec((1, 1, S), lambda h, ki: (h, 0, 0))
    return pl.pallas_call(
        body, name=name, grid=(MLA_HEADS, nq),
        in_specs=[seq, seq_t, tile, tile, seq_t, stat, stat],
        out_specs=[tile_t, tile_t],
        out_shape=[jax.ShapeDtypeStruct((MLA_HEADS * LANES, S), F32)] * 2,
        compiler_params=_params(("parallel", "arbitrary")),
    )(q, q_t, k, v, do_t, lse, delta)


def _fn_ln(ctx, x, g, b):
    xhat, _ = _ln_stats(x)
    return (xhat * g + b,)


def _fn_conv_fwd(ctx, u, up, dtr, w8, cb, dtb):
    first = ctx.i == 0
    y = u * w8[3:4] + cb
    for s in (1, 2, 3):
        y = y + _shift_down(u, up, s, first) * w8[3 - s:4 - s]
    act = y * _sigmoid(y)
    v = dtr + dtb
    e = jnp.exp(-jnp.abs(v))
    one_p = 1.0 + e
    log1p = jnp.where(one_p == 1.0, e, jnp.log(one_p) * e / (one_p - 1.0))
    return y, act, jnp.maximum(v, 0.0) + log1p


def _fn_ssd_post(ctx, y, xs, z, dexp, g):
    yg = (y + xs * dexp) * (z * _sigmoid(z))
    outs = []
    for k in range(2):
        v = yg[:, 256 * k:256 * (k + 1)]
        outs.append(v * lax.rsqrt(_mean1(v * v) + RMS_EPS))
    return (jnp.concatenate(outs, axis=1) * g,)


def _fn_ssd_post_bwd(ctx, dyn, y, xs, z, dexp, g):
    yt = y + xs * dexp
    sig = _sigmoid(z)
    sz = z * sig
    yg = yt * sz
    dyh = dyn * g
    yh, dyg = [], []
    for k in range(2):
        sl = slice(256 * k, 256 * (k + 1))
        v = yg[:, sl]
        rs = lax.rsqrt(_mean1(v * v) + RMS_EPS)
        vh = v * rs
        yh.append(vh)
        dyg.append(rs * (dyh[:, sl] - vh * _mean1(dyh[:, sl] * vh)))
    yh = jnp.concatenate(yh, axis=1)
    dyg = jnp.concatenate(dyg, axis=1)
    dyt = dyg * sz
    dz = dyg * yt * (sig * (1.0 + z * (1.0 - sig)))
    return dyt, dz, dyt * dexp, _sum0(dyt * xs), _sum0(dyn * yh)


def _fn_mla_pre(ctx, ql, kvl, gq, gkv):
    return _rms_fwd(ql, gq), _rms_fwd(kvl, gkv)


def _fn_mla_pre_bwd(ctx, ql, kvl, dqn, dkvn_k, dkvn_v, gq, gkv):
    dql, dgq = _rms_bwd(ql, dqn, gq)
    dkvl, dgkv = _rms_bwd(kvl, dkvn_k + dkvn_v, gkv)
    return dql, dkvl, dgq, dgkv


def _fn_rope(ctx, qp, kn, kr, ta, tb, tc):
    kpe = _rope(kr, ta, tb, tc)
    qs, ks = [], []
    for h in range(MLA_HEADS):
        sl = slice(128 * h, 128 * (h + 1))
        qs.append(_rope(qp[:, sl], ta, tb, tc) * MLA_SCALE)
        ks.append(kn[:, sl] + kpe)
    return jnp.concatenate(qs, axis=1), jnp.concatenate(ks, axis=1)


def _fn_rope_bwd(ctx, dq, dk, ta, tb, tc):
    qs = []
    ksum = jnp.zeros_like(ta)
    for h in range(MLA_HEADS):
        sl = slice(128 * h, 128 * (h + 1))
        qs.append(_rope_bwd(dq[:, sl] * MLA_SCALE, ta, tb, tc))
        ksum = ksum + dk[:, sl]
    lane = _lane(ksum.shape)
    dkr = jnp.where((lane >= 64) & (lane < 96), _rope_bwd(ksum, ta, tb, tc), 0.0)
    return jnp.concatenate(qs, axis=1), dkr


MEM_SCALE = MEM_HEAD_DIM ** -0.5


def _mem_probs(qh, kh):
    s = _dot(qh, kh, NT) * MEM_SCALE
    p = jnp.exp(s - jnp.max(s, axis=1, keepdims=True))
    return p / jnp.sum(p, axis=1, keepdims=True)


def _fn_mem_fwd(ctx, q, km, vm):
    outs = []
    for h in range(MEM_HEADS):
        sl = slice(256 * h, 256 * (h + 1))
        outs.append(_dot(_mem_probs(q[:, sl], km[:, sl]), vm[:, sl]))
    return (jnp.concatenate(outs, axis=1),)


def _fn_mem_bwd(ctx, q, do, km, vm):
    dqs, dks, dvs = [], [], []
    for h in range(MEM_HEADS):
        sl = slice(256 * h, 256 * (h + 1))
        p = _mem_probs(q[:, sl], km[:, sl])
        dvs.append(_dot(p, do[:, sl], TN))
        dp = _dot(do[:, sl], vm[:, sl], NT)
        ds = p * (dp - jnp.sum(dp * p, axis=1, keepdims=True)) * MEM_SCALE
        dqs.append(_dot(ds, km[:, sl]))
        dks.append(_dot(ds, q[:, sl], TN))
    return jnp.concatenate(dqs, axis=1), jnp.concatenate(dks, axis=1), jnp.concatenate(dvs, axis=1)


def _fn_res_ln(ctx, h, r, g, b):
    xhat, _ = _ln_stats(ALPHA * h + r)
    return (xhat * g + b,)


def _fn_res_ln_bwd(ctx, h, r, d1, d2, g):
    xhat, rstd = _ln_stats(ALPHA * h + r)
    return _ln_bwd(xhat, rstd, ALPHA * d1 + d2, g)


def _fn_res2_ln(ctx, h, r1, r2, g, b):
    xhat, _ = _ln_stats(ALPHA * h + (r1 + r2))
    return (xhat * g + b,)


def _fn_res2_ln_bwd(ctx, h, r1, r2, d1, d2, g):
    xhat, rstd = _ln_stats(ALPHA * h + (r1 + r2))
    return _ln_bwd(xhat, rstd, ALPHA * d1 + d2, g)


def _fn_in_ln_bwd(ctx, x, d1, d2, g):
    xhat, rstd = _ln_stats(x)
    return _ln_bwd(xhat, rstd, ALPHA * d1 + d2, g)


def _fn_final(ctx, h2, ff, tgt, g, b):
    xhat, rstd = _ln_stats(ALPHA * h2 + ff)
    e = xhat * g + b - tgt
    loss = 0.5 * _sum0(jnp.sum(e * e, axis=1, keepdims=True)) / D_MODEL
    dx, dg, db = _ln_bwd(xhat, rstd, e / D_MODEL, g)
    return dx, dg, db, loss


def _fn_du(ctx, u, da):
    return (da * 2.0 * jnp.maximum(u, 0.0),)


def _relu2(u):
    r = jnp.maximum(u, 0.0)
    return r * r


def _fn_conv_bwd_a(ctx, y, dxs1, dxs2, dbc, dtr, ddt, dtb):
    sig = _sigmoid(y)
    dact = jnp.concatenate([dxs1 + dxs2, dbc], axis=1)
    dyc = dact * (sig * (1.0 + y * (1.0 - sig)))
    ddtr = ddt * _sigmoid(dtr + dtb)
    return dyc, ddtr, _sum0(dyc), _sum0(ddtr)


def _fn_conv_bwd_b(ctx, d, dn, u, up, w8):
    first, last = ctx.i == 0, ctx.i == ctx.n - 1
    du = d * w8[3:4]
    row = lax.broadcasted_iota(jnp.int32, w8.shape, 0)
    dw = jnp.where(row == 3, _sum0(d * u), 0.0)
    for s in (1, 2, 3):
        du = du + _shift_up(d, dn, s, last) * w8[3 - s:4 - s]
        dw = dw + jnp.where(row == 3 - s, _sum0(d * _shift_down(u, up, s, first)), 0.0)
    return du, dw


def _fn_adam(ctx, w, g, m, v):
    m = ADAM_B1 * m + (1.0 - ADAM_B1) * g
    v = ADAM_B2 * v + (1.0 - ADAM_B2) * (g * g)
    m_hat = m / (1.0 - ADAM_B1 ** ADAM_STEP)
    v_hat = v / (1.0 - ADAM_B2 ** ADAM_STEP)
    return -ADAM_LR * (m_hat / (jnp.sqrt(v_hat) + ADAM_EPS) + ADAM_WD * w), m, v


def _fn_add2(ctx, a, b):
    s = a + b
    return s, s


def _fn_add4(ctx, a, r0, r1, r2):
    return (((a + r0.astype(F32)) + r1.astype(F32)) + r2.astype(F32),)


def _z(r, c, dt):
    return jnp.zeros((r, c), dt)


def _pad_w_in(w):
    r, dt = w.shape[0], w.dtype
    return jnp.concatenate([w[:, 512:1536], w[:, 0:512], w[:, 1544:1928], w[:, 1536:1544], _z(r, 120, dt),
                            w[:, 1928:2184], _z(r, 64, dt), w[:, 2184:2216], _z(r, 32, dt), _z(r, 128, dt)], axis=1)


def _unpad_w_in(d):
    return jnp.concatenate([d[:, 1024:1536], d[:, 0:1024], d[:, 1920:1928], d[:, 1536:1920], d[:, 2048:2304],
                            d[:, 2368:2400]], axis=1)


def _pad_heads(w, width):
    r = w.shape[0]
    w3 = w.reshape(r, MLA_HEADS, width)
    return jnp.pad(w3, ((0, 0), (0, 0), (0, 128 - width))).reshape(r, MLA_HEADS * 128)


def _pad_w_kv(w):
    r = w.shape[0]
    w4 = w.reshape(r, MLA_HEADS, 2, 64)
    return jnp.pad(w4, ((0, 0), (0, 0), (0, 0), (0, 64))).reshape(r, MLA_HEADS * 256)


def _unpad_w_kv(d):
    r = d.shape[0]
    return d.reshape(r, MLA_HEADS, 2, 128)[:, :, :, :64].reshape(r, MLA_HEADS * 128)


def _pad_w_mix(w):
    wo = jnp.pad(w[512:1024].reshape(MLA_HEADS, 64, D_MODEL), ((0, 0), (0, 64), (0, 0))).reshape(1024, D_MODEL)
    return jnp.concatenate([wo, w[0:512]], axis=0)


def _unpad_w_mix(d):
    do = d[:1024].reshape(MLA_HEADS, 128, D_MODEL)[:, :64].reshape(512, D_MODEL)
    return jnp.concatenate([d[1024:1536], do], axis=0)


def _row(v, width=None):
    v = v.reshape(1, -1).astype(F32)
    if width is not None and v.shape[1] < width:
        v = jnp.pad(v, ((0, 0), (0, width - v.shape[1])))
    return v


def _local_step(x, mem, positions, target, W, P):
    S = x.shape[0]
    tr = ROW_TILE
    w_in_p = _pad_w_in(W["w_in"])
    w_q_p = _pad_heads(W["w_q_up"], MLA_QK)
    w_kv3 = W["w_kv_up"].reshape(MLA_KV_RANK, MLA_HEADS, 128)
    w_k_p = _pad_heads(w_kv3[:, :, :64].reshape(MLA_KV_RANK, 512), 64)
    w_v_p = _pad_heads(w_kv3[:, :, 64:].reshape(MLA_KV_RANK, 512), 64)
    w_v_pt = w_v_p.T
    w_mix_y = W["w_mix_out"][0:512]
    w_mix_o = jnp.pad(W["w_mix_out"][512:1024].reshape(MLA_HEADS, 64, D_MODEL),
                      ((0, 0), (0, 64), (0, 0))).reshape(MLA_HEADS * 128, D_MODEL)
    conv_w8 = jnp.pad(P["conv_w"].astype(F32), ((0, 4), (0, 0)))
    conv_b = _row(P["conv_b"])
    dt_b = _row(P["dt_bias"], 128)
    a_head = -jnp.exp(P["a_log"].reshape(-1).astype(F32))
    a_row = _row(a_head, 128)
    dexp = jnp.repeat(P["d_skip"].reshape(-1).astype(F32), 64).reshape(1, 512)
    g_ssd, g_q, g_kv = _row(P["ssd_norm_g"]), _row(P["q_norm_g"]), _row(P["kv_norm_g"])
    g_in, b_in = _row(P["ln_in_g"]), _row(P["ln_in_b"])
    g1, b1, g2, b2, g3, b3 = (_row(P[k]) for k in ("ln1_g", "ln1_b", "ln2_g", "ln2_b", "ln3_g", "ln3_b"))

    half = MLA_ROPE // 2
    inv_freq = jnp.power(ROPE_THETA, -jnp.arange(half, dtype=F32) / half)
    ang = positions.reshape(S, 1).astype(F32) * inv_freq
    cos, sin = jnp.cos(ang), jnp.sin(ang)
    zc = lambda n: jnp.zeros((S, n), F32)
    rope_a = jnp.concatenate([jnp.ones((S, 64), F32), cos, cos, zc(32)], axis=1)
    rope_b = jnp.concatenate([zc(80), sin, zc(32)], axis=1)
    rope_c = jnp.concatenate([zc(64), -sin, zc(48)], axis=1)

    (h0,) = _rowwise(_fn_ln, [x], [g_in, b_in], [D_MODEL], tr=tr, name="ln_in")
    proj = _mm(h0, w_in_p, form="nn", name="mm_in")
    conv_y, xbc, dt = _rowwise(
        _fn_conv_fwd, [(proj,) + SEG_XBC, ("prev", proj) + SEG_XBC, (proj,) + SEG_DT], [conv_w8, conv_b, dt_b],
        [1024, 1024, 128], tr=tr, name="conv_fwd")
    y_ssd, hs = _ssd_fwd(xbc, dt, a_row, name="ssd_fwd")
    (y_n,) = _rowwise(_fn_ssd_post, [y_ssd, (xbc, 0, 512), (proj,) + SEG_Z], [dexp, g_ssd], [512], tr=tr,
                      name="ssd_post")
    q_n, kv_n = _rowwise(_fn_mla_pre, [(proj,) + SEG_QLAT, (proj,) + SEG_KVLAT], [g_q, g_kv], [384, 256], tr=tr,
                         name="mla_pre")
    qp = _mm(q_n, w_q_p, form="nn", name="mm_q_up")
    kn = _mm(kv_n, w_k_p, form="nn", name="mm_k_up")
    v_nat = _mm(kv_n, w_v_p, form="nn", out_dtype=BF16, name="mm_v_up")
    v_t = _mm(w_v_pt, kv_n, form="nt", out_dtype=BF16, name="mm_v_up_t")
    q_rot, k_full = _rowwise(_fn_rope, [qp, kn, (proj,) + SEG_KR, rope_a, rope_b, rope_c], [],
                             [(1024, BF16), (1024, BF16)], tr=tr, name="rope")
    o_t, lse = _attn_fwd(q_rot, k_full, v_t, name="attn_fwd")
    mix_o = _mm(o_t, w_mix_o, form="tn", name="mm_mix_o")
    mix_y = _mm(y_n, w_mix_y, form="nn", name="mm_mix_y")
    (h1,) = _rowwise(_fn_res2_ln, [h0, mix_o, mix_y], [g1, b1], [D_MODEL], tr=tr, name="ln1")
    qm = _mm(h1, W["w_mem_q"], form="nn", name="mm_mem_q")
    km = _mm(mem, W["w_mem_k"], form="nn", name="mm_mem_k")
    vm = _mm(mem, W["w_mem_v"], form="nn", name="mm_mem_v")
    (om,) = _rowwise(_fn_mem_fwd, [qm], [km, vm], [D_MODEL], tr=tr, name="mem_fwd")
    xa = _mm(om, W["w_mem_o"], form="nn", name="mm_mem_o")
    (h2,) = _rowwise(_fn_res_ln, [h1, xa], [g2, b2], [D_MODEL], tr=tr, name="ln2")
    u = _mm(h2, W["w_up"], form="nn", name="mm_up")
    ff = _mm(u, W["w_down"], form="nn", a_pro=_relu2, name="mm_down")

    dt3, dg3, db3, loss = _rowwise(_fn_final, [h2, ff, target], [g3, b3], [D_MODEL],
                                   [(1, D_MODEL), (1, D_MODEL), (1, 128)], tr=tr, name="ln3_loss")
    da = _mm(dt3, W["w_down"], form="nt", name="mm_down_dx")
    dw_down = _mm(u, dt3, form="tn", a_pro=_relu2, name="mm_down_dw")
    (du,) = _rowwise(_fn_du, [u, da], [], [4 * D_MODEL], tr=128, name="mlp_du")
    dw_up = _mm(h2, du, form="tn", name="mm_up_dw")
    dh2 = _mm(du, W["w_up"], form="nt", name="mm_up_dx")
    dt2, dg2, db2 = _rowwise(_fn_res_ln_bwd, [h1, xa, dt3, dh2], [g2], [D_MODEL], [(1, D_MODEL)] * 2, tr=tr,
                             name="ln2_bwd")
    dom = _mm(dt2, W["w_mem_o"], form="nt", name="mm_mem_o_dx")
    dw_mem_o = _mm(om, dt2, form="tn", name="mm_mem_o_dw")
    dqm, dkm, dvm = _rowwise(_fn_mem_bwd, [qm, dom], [km, vm], [D_MODEL], [(256, D_MODEL)] * 2, tr=tr,
                             name="mem_bwd")
    dw_mem_q = _mm(h1, dqm, form="tn", name="mm_mem_q_dw")
    dw_mem_k = _mm(mem, dkm, form="tn", name="mm_mem_k_dw")
    dw_mem_v = _mm(mem, dvm, form="tn", name="mm_mem_v_dw")
    dh1 = _mm(dqm, W["w_mem_q"], form="nt", name="mm_mem_q_dx")
    dt1, dg1, db1 = _rowwise(_fn_res2_ln_bwd, [h0, mix_o, mix_y, dt2, dh1], [g1], [D_MODEL], [(1, D_MODEL)] * 2,
                             tr=tr, name="ln1_bwd")
    do_t = _mm(w_mix_o, dt1, form="nt", name="mm_mix_o_dx")
    dy_n = _mm(dt1, w_mix_y, form="nt", name="mm_mix_y_dx")
    dw_mix_o = _mm(o_t, dt1, form="nn", name="mm_mix_o_dw")
    dw_mix_y = _mm(y_n, dt1, form="tn", name="mm_mix_y_dw")
    dq_t, delta = _attn_bwd_dq(q_rot, k_full, k_full.T, v_nat, o_t, do_t, lse, name="attn_bwd_dq")
    dk_t, dv_t = _attn_bwd_dkv(q_rot, q_rot.T, k_full, v_nat, do_t, lse, delta, name="attn_bwd_dkv")
    dk = dk_t.T
    dqp, dkr = _rowwise(_fn_rope_bwd, [dq_t.T, dk, rope_a, rope_b, rope_c], [], [1024, 128], tr=tr,
                        name="rope_bwd")
    dw_q_p = _mm(q_n, dqp, form="tn", name="mm_q_up_dw")
    dq_n = _mm(dqp, w_q_p, form="nt", name="mm_q_up_dx")
    dw_k_p = _mm(kv_n, dk, form="tn", name="mm_k_up_dw")
    dkv_n1 = _mm(dk, w_k_p, form="nt", name="mm_k_up_dx")
    dw_v_pt = _mm(dv_t, kv_n, form="nn", name="mm_v_up_dw")
    dkv_n2 = _mm(dv_t, w_v_pt, form="tn", name="mm_v_up_dx")
    dq_lat, dkv_lat, dg_q, dg_kv = _rowwise(
        _fn_mla_pre_bwd, [(proj,) + SEG_QLAT, (proj,) + SEG_KVLAT, dq_n, dkv_n1, dkv_n2], [g_q, g_kv], [384, 256],
        [(1, 384), (1, 256)], tr=tr, name="mla_pre_bwd")
    dy_ssd, dz, dxs_skip, ddexp, dg_ssd = _rowwise(
        _fn_ssd_post_bwd, [dy_n, y_ssd, (xbc, 0, 512), (proj,) + SEG_Z], [dexp, g_ssd],
        [512, 512, 512], [(1, 512)] * 2, tr=tr, name="ssd_post_bwd")
    dxs, dbc, ddt, da_head = _ssd_bwd(xbc, dt, a_row, hs, dy_ssd, name="ssd_bwd")
    dyc, ddtr, dconv_b, ddt_b = _rowwise(
        _fn_conv_bwd_a, [conv_y, dxs, dxs_skip, dbc, (proj,) + SEG_DT, ddt], [dt_b], [1024, 128],
        [(1, 1024), (1, 128)], tr=tr, name="conv_bwd_a")
    dxbc, dconv_w8 = _rowwise(
        _fn_conv_bwd_b, [dyc, ("next", dyc, 0, 1024), (proj,) + SEG_XBC, ("prev", proj) + SEG_XBC], [conv_w8], [1024],
        [(8, 1024)], tr=tr, name="conv_bwd_b")
    dproj = jnp.concatenate([dxbc, dz, dq_lat, ddtr, dkv_lat, dkr, jnp.zeros((S, 128), F32)], axis=1)
    dw_in_p = _mm(h0, dproj, form="tn", name="mm_in_dw")
    dh0 = _mm(dproj, w_in_p, form="nt", name="mm_in_dx")
    grad_x, dg_in, db_in = _rowwise(_fn_in_ln_bwd, [x, dt1, dh0], [g_in], [D_MODEL], [(1, D_MODEL)] * 2, tr=tr,
                                    name="ln_in_bwd")

    big = {
        "w_in": _unpad_w_in(dw_in_p),
        "w_q_up": dw_q_p.reshape(384, MLA_HEADS, 128)[:, :, :MLA_QK].reshape(384, MLA_HEADS * MLA_QK),
        "w_kv_up": jnp.concatenate([dw_k_p.reshape(MLA_KV_RANK, MLA_HEADS, 128)[:, :, :64],
                                    dw_v_pt.T.reshape(MLA_KV_RANK, MLA_HEADS, 128)[:, :, :64]], axis=2).reshape(
                                        MLA_KV_RANK, MLA_HEADS * 128),
        "w_mix_out": jnp.concatenate([dw_mix_y, dw_mix_o.reshape(MLA_HEADS, 128, D_MODEL)[:, :64].reshape(
            512, D_MODEL)], axis=0),
        "w_mem_q": dw_mem_q, "w_mem_k": dw_mem_k, "w_mem_v": dw_mem_v, "w_mem_o": dw_mem_o,
        "w_up": dw_up, "w_down": dw_down,
        "conv_w": dconv_w8[0:4],
    }
    small = {
        "ln_in_g": dg_in, "ln_in_b": db_in, "conv_b": dconv_b, "dt_bias": ddt_b[:, :8],
        "a_log": da_head[:, :8] * a_head.reshape(1, 8),
        "d_skip": ddexp.reshape(8, 64).sum(axis=1).reshape(1, 8),
        "ssd_norm_g": dg_ssd, "q_norm_g": dg_q, "kv_norm_g": dg_kv,
        "ln1_g": dg1, "ln1_b": db1, "ln2_g": dg2, "ln2_b": db2, "ln3_g": dg3, "ln3_b": db3,
    }
    return loss[0, 0], grad_x, big, small


BIG = {
    "w_in": (1024, 2216, 1), "w_q_up": (384, 768, 1), "w_kv_up": (256, 1024, 1), "w_mix_out": (1024, 1024, 0),
    "w_mem_q": (1024, 1024, 0), "w_mem_k": (1024, 1024, 0), "w_mem_v": (1024, 1024, 0), "w_mem_o": (1024, 1024, 0),
    "w_up": (1024, 4096, 1), "w_down": (4096, 1024, 0), "conv_w": (4, 1024, 1),
}
BIG_ORDER = list(BIG)
SMALL_ORDER = ["ln_in_g", "ln_in_b", "conv_b", "dt_bias", "a_log", "d_skip", "ssd_norm_g", "q_norm_g", "kv_norm_g",
               "ln1_g", "ln1_b", "ln2_g", "ln2_b", "ln3_g", "ln3_b"]
N_SHARD = 4
PACK_COLS = 1024
PACK_ROWS = 4032
HALF_ROWS = PACK_ROWS // 2
GATHER_CHUNKS = 3
CHIP_CHUNKS = 3
PAIR_CHUNKS = 4


def _shard_shape(name):
    r, c, ax = BIG[name]
    return (r // N_SHARD, c) if ax == 0 else (r, c // N_SHARD)


def _split_shards(name, full):
    r, c, ax = BIG[name]
    if ax == 0:
        return full.reshape(N_SHARD, -1)
    return full.reshape(r, N_SHARD, c // N_SHARD).transpose(1, 0, 2).reshape(N_SHARD, -1)


def _join_shards(name, parts):
    r, c, ax = BIG[name]
    if ax == 0:
        return parts.reshape(r, c)
    return parts.reshape(N_SHARD, r, c // N_SHARD).transpose(1, 0, 2).reshape(r, c)


HBM = pl.BlockSpec(memory_space=pl.ANY)


def _place():
    x, y, c = lax.axis_index("x"), lax.axis_index("y"), lax.axis_index("c")
    chips = [(1 - x, y), (x, 1 - y), (1 - x, 1 - y)]
    return x, y, c, chips


def _gather_weights(wp):
    R, C = wp.shape
    H = R // 2
    nq = GATHER_CHUNKS
    CH = H // nq

    def body(w_ref, out_ref, send_sems, recv_sems):
        x, y, c, chips = _place()
        sib = (x, y, 1 - c)

        def piece(k, hc, q):
            return out_ref.at[k, pl.ds(hc * H + q * CH, CH), :]

        def copy(j, src, dst, to):
            return pltpu.make_async_remote_copy(src_ref=src, dst_ref=dst, send_sem=send_sems.at[j],
                                                recv_sem=recv_sems.at[j], device_id=to, device_id_type=MESH)

        me = 2 * x + y
        sends = []
        for q in range(nq):
            for j, (px, py) in enumerate(chips):
                cp = copy(j * nq + q, w_ref.at[pl.ds(c * H + q * CH, CH), :], piece(me, c, q), (px, py, c))
                cp.start()
                sends.append(cp)
        fwds = []
        for q in range(nq):
            for j, (px, py) in enumerate(chips):
                k = 2 * px + py
                copy(j * nq + q, piece(k, c, q), piece(k, c, q), (px, py, c)).wait_recv()
                f = copy((3 + j) * nq + q, piece(k, c, q), piece(k, c, q), sib)
                f.start()
                fwds.append(f)
        for q in range(nq):
            for j, (px, py) in enumerate(chips):
                k = 2 * px + py
                copy((3 + j) * nq + q, piece(k, 1 - c, q), piece(k, 1 - c, q), sib).wait_recv()
        for cp in sends + fwds:
            cp.wait_send()

    out = pl.pallas_call(
        body, name="gather_weights", in_specs=[HBM], out_specs=HBM,
        out_shape=jax.ShapeDtypeStruct((N_SHARD, R, C), wp.dtype),
        scratch_shapes=[pltpu.SemaphoreType.DMA((6 * nq,)), pltpu.SemaphoreType.DMA((6 * nq,))],
    )(wp)
    me = 2 * lax.axis_index("x") + lax.axis_index("y")
    return lax.dynamic_update_slice(out, wp[None], (me, 0, 0))


def _pair_exchange(gp):
    n, R, C = gp.shape
    H = R // 2
    nq = PAIR_CHUNKS
    CH = H // nq

    def body(g_ref, theirs_ref, send_sems, recv_sems):
        x, y, c, _ = _place()
        swaps = []
        for k in range(n):
            for q in range(nq):
                cp = pltpu.make_async_remote_copy(
                    src_ref=g_ref.at[k, pl.ds((1 - c) * H + q * CH, CH), :], dst_ref=theirs_ref.at[k, pl.ds(q * CH, CH), :],
                    send_sem=send_sems.at[k * nq + q], recv_sem=recv_sems.at[k * nq + q], device_id=(x, y, 1 - c),
                    device_id_type=MESH)
                cp.start()
                swaps.append(cp)
        for cp in swaps:
            cp.wait()

    theirs = pl.pallas_call(
        body, name="pair_exchange", in_specs=[HBM], out_specs=HBM,
        out_shape=jax.ShapeDtypeStruct((n, H, C), gp.dtype),
        scratch_shapes=[pltpu.SemaphoreType.DMA((n * nq,)), pltpu.SemaphoreType.DMA((n * nq,))],
    )(gp)
    mine = lax.dynamic_slice(gp, (0, lax.axis_index("c") * H, 0), (n, H, C))
    return mine, theirs


def _chip_exchange(pb):
    n, H, C = pb.shape
    nq = CHIP_CHUNKS
    CH = H // nq

    def body(pb_ref, got_ref, send_sems, recv_sems):
        x, y, c, chips = _place()
        sends = []
        for q in range(nq):
            for j, (px, py) in enumerate(chips):
                cp = pltpu.make_async_remote_copy(
                    src_ref=pb_ref.at[2 * px + py, pl.ds(q * CH, CH), :], dst_ref=got_ref.at[j, pl.ds(q * CH, CH), :],
                    send_sem=send_sems.at[j * nq + q], recv_sem=recv_sems.at[j * nq + q],
                    device_id=(px, py, c), device_id_type=MESH)
                cp.start()
                sends.append(cp)
        for cp in sends:
            cp.wait()

    return pl.pallas_call(
        body, name="chip_exchange", in_specs=[HBM], out_specs=HBM,
        out_shape=jax.ShapeDtypeStruct((3, H, C), BF16),
        scratch_shapes=[pltpu.SemaphoreType.DMA((3 * nq,)), pltpu.SemaphoreType.DMA((3 * nq,))],
    )(pb)


def _pair_join(q):
    H, C = q.shape
    nq = PAIR_CHUNKS
    CH = H // nq

    def body(q_ref, theirs_ref, send_sems, recv_sems):
        x, y, c, _ = _place()
        pushes = []
        for j in range(nq):
            cp = pltpu.make_async_remote_copy(
                src_ref=q_ref.at[pl.ds(j * CH, CH), :], dst_ref=theirs_ref.at[pl.ds(j * CH, CH), :],
                send_sem=send_sems.at[j], recv_sem=recv_sems.at[j], device_id=(x, y, 1 - c), device_id_type=MESH)
            cp.start()
            pushes.append(cp)
        for cp in pushes:
            cp.wait()

    theirs = pl.pallas_call(
        body, name="pair_join", in_specs=[HBM], out_specs=HBM,
        out_shape=jax.ShapeDtypeStruct((H, C), F32),
        scratch_shapes=[pltpu.SemaphoreType.DMA((nq,)), pltpu.SemaphoreType.DMA((nq,))],
    )(q)
    c = lax.axis_index("c")
    out = jnp.zeros((2 * H, C), F32)
    out = lax.dynamic_update_slice(out, q, (c * H, 0))
    return lax.dynamic_update_slice(out, theirs, ((1 - c) * H, 0))


N_DEV = 8


def _small_all_reduce(g):
    r, cdim = g.shape

    def body(g_ref, out_ref, buf, send_sems, recv_sems):
        x, y, c, _ = _place()
        me = 4 * x + 2 * y + c
        buf[me] = g_ref[...]
        copies = []
        for d in range(1, N_DEV):
            to = me ^ d
            cp = pltpu.make_async_remote_copy(src_ref=g_ref, dst_ref=buf.at[me], send_sem=send_sems.at[d - 1],
                                              recv_sem=recv_sems.at[d - 1],
                                              device_id=(to // 4, (to // 2) % 2, to % 2), device_id_type=MESH)
            cp.start()
            copies.append(cp)
        for cp in copies:
            cp.wait()
        acc = buf[0]
        for d in range(1, N_DEV):
            acc = acc + buf[d]
        out_ref[...] = acc

    return pl.pallas_call(
        body, name="small_all_reduce",
        in_specs=[pl.BlockSpec(memory_space=pltpu.VMEM)], out_specs=pl.BlockSpec(memory_space=pltpu.VMEM),
        out_shape=jax.ShapeDtypeStruct((r, cdim), F32),
        scratch_shapes=[pltpu.VMEM((N_DEV, r, cdim), F32), pltpu.SemaphoreType.DMA((N_DEV - 1,)),
                        pltpu.SemaphoreType.DMA((N_DEV - 1,))],
    )(g)


def _adam(w, g, m, v, name):
    shape = w.shape
    w2, g2, m2, v2 = (t.reshape(-1, shape[-1]) for t in (w, g, m, v))
    d, mn, vn = _rowwise(_fn_adam, [w2, g2, m2, v2], [], [shape[-1]] * 3, tr=256, name=name)
    return d.reshape(shape), mn.reshape(shape), vn.reshape(shape)


def kernel(x, mem, positions, ln_in_g, ln_in_b, w_in, conv_w, conv_b, dt_bias, a_log, d_skip, ssd_norm_g, q_norm_g, w_q_up, kv_norm_g, w_kv_up, w_mix_out, ln1_g, ln1_b, w_mem_q, w_mem_k, w_mem_v, w_mem_o, ln2_g, ln2_b, w_up, w_down, ln3_g, ln3_b, loss_target, m_ln_in_g, m_ln_in_b, m_w_in, m_conv_w, m_conv_b, m_dt_bias, m_a_log, m_d_skip, m_ssd_norm_g, m_q_norm_g, m_w_q_up, m_kv_norm_g, m_w_kv_up, m_w_mix_out, m_ln1_g, m_ln1_b, m_w_mem_q, m_w_mem_k, m_w_mem_v, m_w_mem_o, m_ln2_g, m_ln2_b, m_w_up, m_w_down, m_ln3_g, m_ln3_b, v_ln_in_g, v_ln_in_b, v_w_in, v_conv_w, v_conv_b, v_dt_bias, v_a_log, v_d_skip, v_ssd_norm_g, v_q_norm_g, v_w_q_up, v_kv_norm_g, v_w_kv_up, v_w_mix_out, v_ln1_g, v_ln1_b, v_w_mem_q, v_w_mem_k, v_w_mem_v, v_w_mem_o, v_ln2_g, v_ln2_b, v_w_up, v_w_down, v_ln3_g, v_ln3_b):
    args = dict(locals())
    weights = BIG_ORDER + SMALL_ORDER

    flat = []
    for n in BIG_ORDER:
        s = args[n].reshape(-1)
        if n == "conv_w":
            flat.append(lax.bitcast_convert_type(s.astype(F32), BF16).reshape(-1))
        else:
            flat.append(s.astype(BF16))
    flat = jnp.concatenate(flat)
    wp = jnp.pad(flat, (0, PACK_ROWS * PACK_COLS - flat.shape[0])).reshape(PACK_ROWS, PACK_COLS)
    gathered = _gather_weights(wp).reshape(N_SHARD, -1)
    W, off = {}, 0
    for n in BIG_ORDER:
        sr, sc = _shard_shape(n)
        cnt = sr * sc
        if n == "conv_w":
            part = lax.bitcast_convert_type(gathered[:, off:off + 2 * cnt].reshape(N_SHARD, cnt, 2), F32)
            off += 2 * cnt
        else:
            part = gathered[:, off:off + cnt]
            off += cnt
        W[n] = _join_shards(n, part)
    P = {n: args[n] for n in SMALL_ORDER}
    P["conv_w"] = W.pop("conv_w")

    loss, grad_x, gbig, gsmall = _local_step(x[0], mem[0], positions[0], loss_target[0], W, P)
    loss = lax.psum(loss, ("x", "y", "c"))

    gflat = jnp.concatenate([_split_shards(n, gbig[n]) for n in BIG_ORDER], axis=1)
    gp = jnp.pad(gflat, ((0, 0), (0, PACK_ROWS * PACK_COLS - gflat.shape[1]))).reshape(N_SHARD, PACK_ROWS, PACK_COLS)
    mine, theirs = _pair_exchange(gp)
    pf, pb = _rowwise(_fn_add2, [mine.reshape(-1, PACK_COLS), theirs.reshape(-1, PACK_COLS)], [],
                      [PACK_COLS, (PACK_COLS, BF16)], tr=288, name="pair_sum")
    pf = pf.reshape(N_SHARD, HALF_ROWS, PACK_COLS)
    pb = pb.reshape(N_SHARD, HALF_ROWS, PACK_COLS)
    got = _chip_exchange(pb).reshape(3 * HALF_ROWS, PACK_COLS)
    own = lax.dynamic_index_in_dim(pf, 2 * lax.axis_index("x") + lax.axis_index("y"), axis=0, keepdims=False)
    (q,) = _rowwise(_fn_add4, [own] + [(got, 0, PACK_COLS, j * HALF_ROWS) for j in range(3)], [], [PACK_COLS],
                    tr=288, name="chip_sum", n_rows=HALF_ROWS)
    red = _pair_join(q).reshape(-1)

    gs = jnp.concatenate([_row(gsmall[n], PACK_COLS) for n in SMALL_ORDER] + [jnp.zeros((1, PACK_COLS), F32)], axis=0)
    gs = _small_all_reduce(gs)

    grads, deltas, new_m, new_v = {}, {}, {}, {}
    off = 0
    for n in BIG_ORDER:
        sr, sc = _shard_shape(n)
        g = red[off:off + sr * sc].reshape(args[n].shape)
        off += sr * sc
        grads[n] = g
        deltas[n], new_m[n], new_v[n] = _adam(args[n], g, args["m_" + n], args["v_" + n], "adam_" + n)
    pack = lambda pre: jnp.concatenate([_row(args[pre + n], PACK_COLS) for n in SMALL_ORDER]
                                       + [jnp.zeros((1, PACK_COLS), F32)], axis=0)
    ds, ms, vs = _rowwise(_fn_adam, [pack(""), gs, pack("m_"), pack("v_")], [], [PACK_COLS] * 3, tr=16,
                          name="adam_small")
    for i, n in enumerate(SMALL_ORDER):
        cnt = args[n].size
        take = lambda t: t[i, :cnt].reshape(args[n].shape)
        grads[n], deltas[n], new_m[n], new_v[n] = take(gs), take(ds), take(ms), take(vs)

    order = ["ln_in_g", "ln_in_b", "w_in", "conv_w", "conv_b", "dt_bias", "a_log", "d_skip", "ssd_norm_g",
             "q_norm_g", "w_q_up", "kv_norm_g", "w_kv_up", "w_mix_out", "ln1_g", "ln1_b", "w_mem_q", "w_mem_k",
             "w_mem_v", "w_mem_o", "ln2_g", "ln2_b", "w_up", "w_down", "ln3_g", "ln3_b"]
    assert sorted(order) == sorted(weights)
    return (loss, grad_x[None], *[grads[n] for n in order], *[deltas[n] for n in order],
            *[new_m[n] for n in order], *[new_v[n] for n in order])
```

```python
import functools
import math

import jax
import jax.numpy as jnp
import numpy as np
from jax import lax
from jax.experimental import pallas as pl
from jax.experimental.pallas import tpu as pltpu

F32 = jnp.float32
BF16 = jnp.bfloat16
MESH = pl.DeviceIdType.MESH

D_MODEL = 1024
SSD_HEADS = 8
SSD_INNER = 512
SSD_CHUNK = 128
SSD_STATE = 128
MLA_HEADS = 8
MLA_NOPE = 64
MLA_ROPE = 32
MLA_QK = 96
MLA_Q_RANK = 384
MLA_KV_RANK = 256
ROPE_THETA = 10000.0
MEM_HEADS = 4
MEM_HEAD_DIM = 256
LN_EPS = 1e-5
RMS_EPS = 1e-6
ALPHA = 2.0 ** 0.25
ADAM_LR = 0.001
ADAM_B1 = 0.9
ADAM_B2 = 0.999
ADAM_EPS = 1e-08
ADAM_WD = 0.01
ADAM_STEP = 10

LANES = 128
IN_W = 2560
SEG_XBC = (0, 1024)
SEG_Z = (1024, 512)
SEG_QLAT = (1536, 384)
SEG_DT = (1920, 128)
SEG_KVLAT = (2048, 256)
SEG_KR = (2304, 128)
VMEM_LIMIT = 56 * 1024 * 1024
ATTN_TILE = 512
ROW_TILE = 256
NEG = -1e30

NN = (((1,), (0,)), ((), ()))
NT = (((1,), (1,)), ((), ()))
TN = (((0,), (0,)), ((), ()))


def _dot(a, b, dims=NN):
    return lax.dot_general(a.astype(BF16), b.astype(BF16), dims, preferred_element_type=F32)


def _dot_exact(a, b):
    return lax.dot_general(a, b, NN, precision=lax.Precision.HIGHEST, preferred_element_type=F32)


def _pick(dim, pref):
    t = min(pref, dim)
    t -= t % LANES
    while t >= LANES:
        if dim % t == 0:
            return t
        t -= LANES
    return dim


def _params(sem):
    return pltpu.CompilerParams(dimension_semantics=sem, vmem_limit_bytes=VMEM_LIMIT)


def _pack_caps(wname):
    r, c, ax = BIG[wname]
    return (r // N_SHARD, c) if ax == 0 else (r, c // N_SHARD)


def _pack_index(wname, rb, cb, br, bc):
    r, c, ax = BIG[wname]
    r0 = PACK_A_ROW[wname]
    assert r0 % br == 0
    if ax == 0:
        per = (r // N_SHARD) // br
        return rb // per, r0 // br + rb % per, cb
    per = (c // N_SHARD) // bc
    return cb // per, r0 // br + rb, cb % per


def _mm(a, b, *, form, name, a_pro=None, out_dtype=F32, tm=1024, tn=1024, tk=1024, b_pack=None, b_rows=None,
        out_pack=None):
    b_shape = BIG[b_pack][:2] if b_pack else b.shape
    if b_pack and form == "nt":
        b_shape = (b_rows or b_shape[0], b_shape[1])
    if form == "nn":
        (m, k), (_, n) = a.shape, b_shape
    elif form == "nt":
        (m, k), (n, _) = a.shape, b_shape
    else:
        (k, m), (_, n) = a.shape, b_shape
    if b_pack:
        rcap, ccap = _pack_caps(b_pack)
        tk, tn = (min(tk, rcap), min(tn, ccap)) if form == "nn" else (min(tk, ccap), min(tn, rcap))
    if out_pack:
        rcap, ccap = _pack_caps(out_pack[0])
        tm, tn = min(tm, rcap), min(tn, ccap)
    tm, tn, tk = _pick(m, tm), _pick(n, tn), _pick(k, tk)
    dims = {"nn": NN, "nt": NT, "tn": TN}[form]
    nk = k // tk
    direct = out_dtype == F32

    def body(a_ref, b_ref, *rest):
        o_ref = rest[1] if out_pack else rest[0]
        o_blk = o_ref.at[0] if out_pack else o_ref
        acc_ref = o_blk if direct else rest[-1]

        @pl.when(pl.program_id(2) == 0)
        def _():
            acc_ref[...] = jnp.zeros_like(acc_ref)

        av = a_ref[...]
        if a_pro is not None:
            av = a_pro(av)
        acc_ref[...] += _dot(av, b_ref[0] if b_pack else b_ref[...], dims)
        if not direct:
            @pl.when(pl.program_id(2) == nk - 1)
            def _():
                o_blk[...] = acc_ref[...].astype(o_blk.dtype)

    if form == "tn":
        a_spec = pl.BlockSpec((tk, tm), lambda i, j, kk: (kk, i))
    else:
        a_spec = pl.BlockSpec((tm, tk), lambda i, j, kk: (i, kk))
    if b_pack and form == "nn":
        b_spec = pl.BlockSpec((1, tk, tn), lambda i, j, kk: _pack_index(b_pack, kk, j, tk, tn))
    elif b_pack:
        b_spec = pl.BlockSpec((1, tn, tk), lambda i, j, kk: _pack_index(b_pack, j, kk, tn, tk))
    elif form == "nt":
        b_spec = pl.BlockSpec((tn, tk), lambda i, j, kk: (j, kk))
    else:
        b_spec = pl.BlockSpec((tk, tn), lambda i, j, kk: (kk, j))
    scratch = [] if direct else [pltpu.VMEM((tm, tn), F32)]
    sem = _params(("parallel", "parallel", "arbitrary"))
    grid = (m // tm, n // tn, nk)
    if out_pack:
        wname, buf = out_pack
        return pl.pallas_call(
            body, name=name, grid=grid, in_specs=[a_spec, b_spec, HBM],
            out_specs=pl.BlockSpec((1, tm, tn), lambda i, j, kk: _pack_index(wname, i, j, tm, tn)),
            out_shape=jax.ShapeDtypeStruct(buf.shape, buf.dtype), input_output_aliases={2: 0},
            scratch_shapes=scratch, compiler_params=sem,
        )(a, b, buf)
    return pl.pallas_call(
        body, name=name, grid=grid, in_specs=[a_spec, b_spec],
        out_specs=pl.BlockSpec((tm, tn), lambda i, j, kk: (i, j)),
        out_shape=jax.ShapeDtypeStruct((m, n), out_dtype),
        scratch_shapes=scratch, compiler_params=sem,
    )(a, b)


class _Ctx:
    def __init__(self, i, n):
        self.i, self.n = i, n


def _rowwise(fn, rows, consts, row_outs, acc_outs=(), *, tr, name, n_rows=None):
    norm = []
    for r in rows:
        kind = "tile"
        if isinstance(r, tuple) and isinstance(r[0], str):
            kind, r = r[0], r[1:]
        row0 = 0
        if isinstance(r, tuple) and len(r) == 4:
            r, row0 = r[:3], r[3]
        arr, col0, width = r if isinstance(r, tuple) else (r, 0, r.shape[1])
        assert col0 % width == 0
        norm.append((kind, arr, col0 // width, width, row0))
    n_rows = n_rows or next(a.shape[0] for k, a, _, _, _ in norm if k == "tile")
    tr = min(tr, n_rows)
    while n_rows % tr:
        tr -= 8
    n = n_rows // tr
    arrs, specs = [], []
    for kind, arr, cb, width, row0 in norm:
        if kind == "tile":
            assert row0 % tr == 0
            specs.append(pl.BlockSpec((tr, width), lambda i, cb=cb, rb=row0 // tr: (i + rb, cb)))
        elif kind == "prev":
            specs.append(pl.BlockSpec((8, width), lambda i, cb=cb: (jnp.maximum(i * (tr // 8) - 1, 0), cb)))
        else:
            specs.append(pl.BlockSpec((8, width), lambda i, cb=cb: (jnp.minimum((i + 1) * (tr // 8), n_rows // 8 - 1), cb)))
        arrs.append(arr)
    for c in consts:
        specs.append(pl.BlockSpec(c.shape, lambda i, nd=c.ndim: (0,) * nd))
        arrs.append(c)
    n_in, n_ro = len(arrs), len(row_outs)
    row_outs = [w if isinstance(w, tuple) else (w, F32) for w in row_outs]
    out_shape = [jax.ShapeDtypeStruct((n_rows, w), dt) for w, dt in row_outs]
    out_specs = [pl.BlockSpec((tr, w), lambda i: (i, 0)) for w, _ in row_outs]
    out_shape += [jax.ShapeDtypeStruct(s, F32) for s in acc_outs]
    out_specs += [pl.BlockSpec(s, lambda i: (0, 0)) for s in acc_outs]

    def body(*refs):
        i = pl.program_id(0)
        vals = [r[...] for r in refs[:n_in]]
        outs = fn(_Ctx(i, n), *vals)
        if not isinstance(outs, (tuple, list)):
            outs = (outs,)
        o_refs = refs[n_in:]
        for o_ref, o in zip(o_refs[:n_ro], outs[:n_ro]):
            o_ref[...] = o.astype(o_ref.dtype)
        if acc_outs:
            @pl.when(i == 0)
            def _():
                for o_ref in o_refs[n_ro:]:
                    o_ref[...] = jnp.zeros_like(o_ref)

            for o_ref, o in zip(o_refs[n_ro:], outs[n_ro:]):
                o_ref[...] += jnp.broadcast_to(o, o_ref.shape)

    res = pl.pallas_call(
        body, name=name, grid=(n,), in_specs=specs, out_specs=out_specs, out_shape=out_shape,
        compiler_params=_params(("arbitrary",)),
    )(*arrs)
    return res


def _sum0(v):
    return jnp.sum(v, axis=0, keepdims=True)


def _mean1(v):
    return jnp.mean(v, axis=-1, keepdims=True)


def _sigmoid(v):
    return 1.0 / (1.0 + jnp.exp(-v))


def _ln_stats(t):
    xc = t - _mean1(t)
    rstd = lax.rsqrt(_mean1(xc * xc) + LN_EPS)
    return xc * rstd, rstd


def _ln_bwd(xhat, rstd, dy, g):
    dxh = dy * g
    dx = rstd * (dxh - _mean1(dxh) - xhat * _mean1(dxh * xhat))
    return dx, _sum0(dy * xhat), _sum0(dy)


def _rms_fwd(v, g):
    return v * lax.rsqrt(_mean1(v * v) + RMS_EPS) * g


def _rms_bwd(v, dy, g):
    rs = lax.rsqrt(_mean1(v * v) + RMS_EPS)
    vh = v * rs
    dyg = dy * g
    return rs * (dyg - vh * _mean1(dyg * vh)), _sum0(dy * vh)


def _lane(shape):
    return lax.broadcasted_iota(jnp.int32, shape, len(shape) - 1)


def _shift_down(u, halo, s, is_first):
    tr = u.shape[0]
    rolled = pltpu.roll(u, s, 0)
    hr = jnp.where(is_first, 0.0, pltpu.roll(halo, s, 0))
    row = lax.broadcasted_iota(jnp.int32, hr.shape, 0)
    top = jnp.where(row < s, hr, rolled[0:8])
    if tr == 8:
        return top
    return jnp.concatenate([top, rolled[8:]], axis=0)


def _shift_up(d, halo, s, is_last):
    tr = d.shape[0]
    rolled = pltpu.roll(d, tr - s, 0)
    hr = jnp.where(is_last, 0.0, pltpu.roll(halo, 8 - s, 0))
    row = lax.broadcasted_iota(jnp.int32, hr.shape, 0)
    bot = jnp.where(row >= 8 - s, hr, rolled[tr - 8:])
    if tr == 8:
        return bot
    return jnp.concatenate([rolled[:tr - 8], bot], axis=0)


def _rope(v, ta, tb, tc):
    return v * ta + pltpu.roll(v, 16, 1) * tb + pltpu.roll(v, LANES - 16, 1) * tc


def _rope_bwd(d, ta, tb, tc):
    return d * ta + pltpu.roll(d * tb, LANES - 16, 1) + pltpu.roll(d * tc, 16, 1)


def _ssd_common(dtv, a_row):
    L = SSD_CHUNK
    a = dtv * a_row
    r = lax.broadcasted_iota(jnp.int32, (L, L), 0)
    c = lax.broadcasted_iota(jnp.int32, (L, L), 1)
    tril = r >= c
    cs = _dot_exact(tril.astype(F32), a)
    cs_t = cs.T
    cs_last = cs[L - 1:L, :]
    return dict(a=a, tril=tril, cs=cs, cs_t=cs_t, ecs=jnp.exp(cs), dte=jnp.exp(cs_last - cs),
                elast=jnp.exp(cs_last))


def _pair_sel(v, h0, lo):
    return jnp.where(lo, v[:, h0:h0 + 1], v[:, h0 + 1:h0 + 2])


def _ssd_pair(cm, h0, cb, xp, dtv, bmat, cmat, hp, lo):
    L = SSD_CHUNK
    x = xp * _pair_sel(dtv, h0, lo)
    lam0 = jnp.exp(jnp.where(cm["tril"], cm["cs"][:, h0:h0 + 1] - cm["cs_t"][h0:h0 + 1, :], NEG))
    lam1 = jnp.exp(jnp.where(cm["tril"], cm["cs"][:, h0 + 1:h0 + 2] - cm["cs_t"][h0 + 1:h0 + 2, :], NEG))
    m0, m1 = cb * lam0, cb * lam1
    ydiag = jnp.where(lo, _dot(m0, x), _dot(m1, x))
    ecs_p = _pair_sel(cm["ecs"], h0, lo)
    dte_p = _pair_sel(cm["dte"], h0, lo)
    yoff = _dot(cmat, hp, NT) * ecs_p
    xd = x * dte_p
    st = _dot(xd, bmat, TN)
    rlo = lax.broadcasted_iota(jnp.int32, (LANES, SSD_STATE), 0) < 64
    decay = jnp.where(rlo, cm["elast"][:, h0:h0 + 1], cm["elast"][:, h0 + 1:h0 + 2])
    h_next = hp * decay + st
    return dict(x=x, lam0=lam0, lam1=lam1, m0=m0, m1=m1, y=ydiag + yoff, yoff=yoff, ecs_p=ecs_p, dte_p=dte_p,
                xd=xd, decay=decay, h_next=h_next)


def _ssd_fwd(xbc, dt, a_row, *, name):
    S = xbc.shape[0]
    L = SSD_CHUNK
    nc = S // L

    def body(xs_ref, bm_ref, cm_ref, dt_ref, a_ref, y_ref, hs_ref, h_scr):
        @pl.when(pl.program_id(0) == 0)
        def _():
            h_scr[...] = jnp.zeros_like(h_scr)

        dtv = dt_ref[...]
        cm = _ssd_common(dtv, a_ref[...])
        lo = _lane((L, LANES)) < 64
        ys = []
        for g in range(2):
            bmat = bm_ref[:, g * 128:(g + 1) * 128]
            cmat = cm_ref[:, g * 128:(g + 1) * 128]
            cb = _dot(cmat, bmat, NT)
            for pr in range(2):
                p4 = 2 * g + pr
                hp = h_scr[p4]
                hs_ref[0, p4 * 128:(p4 + 1) * 128, :] = hp
                t = _ssd_pair(cm, 2 * p4, cb, xs_ref[:, p4 * 128:(p4 + 1) * 128], dtv, bmat, cmat, hp, lo)
                ys.append(t["y"])
                h_scr[p4] = t["h_next"]
        y_ref[...] = jnp.concatenate(ys, axis=1)

    return pl.pallas_call(
        body, name=name, grid=(nc,),
        in_specs=[pl.BlockSpec((L, 512), lambda c: (c, 0)), pl.BlockSpec((L, 256), lambda c: (c, 2)),
                  pl.BlockSpec((L, 256), lambda c: (c, 3)), pl.BlockSpec((L, 128), lambda c: (c, 0)),
                  pl.BlockSpec((1, 128), lambda c: (0, 0))],
        out_specs=[pl.BlockSpec((L, 512), lambda c: (c, 0)), pl.BlockSpec((1, 512, 128), lambda c: (c, 0, 0))],
        out_shape=[jax.ShapeDtypeStruct((S, 512), F32), jax.ShapeDtypeStruct((nc, 512, 128), F32)],
        scratch_shapes=[pltpu.VMEM((4, 128, 128), F32)],
        compiler_params=_params(("arbitrary",)),
    )(xbc, xbc, xbc, dt, a_row)


def _ssd_bwd(xbc, dt, a_row, hs, dy, *, name):
    S = xbc.shape[0]
    L = SSD_CHUNK
    nc = S // L

    def body(xs_ref, bm_ref, cm_ref, dt_ref, a_ref, hs_ref, dy_ref, dxs_ref, dbc_ref, ddt_ref, da_ref, g_scr):
        @pl.when(pl.program_id(0) == 0)
        def _():
            g_scr[...] = jnp.zeros_like(g_scr)
            da_ref[...] = jnp.zeros_like(da_ref)

        dtv = dt_ref[...]
        a_row_v = a_ref[...]
        cm = _ssd_common(dtv, a_row_v)
        lo = _lane((L, LANES)) < 64
        lane_row = _lane((1, LANES))
        ri = lax.broadcasted_iota(jnp.int32, (L, L), 0)
        ci = lax.broadcasted_iota(jnp.int32, (L, L), 1)
        triu = (ri <= ci).astype(F32)
        stril = ri > ci

        def halves(v, mask):
            return (jnp.sum(jnp.where(mask, v, 0.0), axis=1, keepdims=True),
                    jnp.sum(jnp.where(mask, 0.0, v), axis=1, keepdims=True))

        i_all = jnp.zeros((L, LANES), F32)
        yo_all = jnp.zeros((L, LANES), F32)
        w_all = jnp.zeros((L, LANES), F32)
        ddt_x = jnp.zeros((L, LANES), F32)
        e_row = jnp.zeros((1, LANES), F32)
        rlo = lax.broadcasted_iota(jnp.int32, (LANES, SSD_STATE), 0) < 64
        dxs, dbs, dcs = [], [], []
        for g in range(2):
            bmat = bm_ref[:, g * 128:(g + 1) * 128]
            cmat = cm_ref[:, g * 128:(g + 1) * 128]
            cb = _dot(cmat, bmat, NT)
            dcb = jnp.zeros((L, L), F32)
            db = jnp.zeros((L, SSD_STATE), F32)
            dc = jnp.zeros((L, SSD_STATE), F32)
            for pr in range(2):
                p4 = 2 * g + pr
                h0 = 2 * p4
                hp = hs_ref[0, p4 * 128:(p4 + 1) * 128, :]
                xp = xs_ref[:, p4 * 128:(p4 + 1) * 128]
                t = _ssd_pair(cm, h0, cb, xp, dtv, bmat, cmat, hp, lo)
                gst = g_scr[p4]
                dyp = dy_ref[:, p4 * 128:(p4 + 1) * 128]
                dy0 = jnp.where(lo, dyp, 0.0)
                dy1 = dyp - dy0
                bg = _dot(bmat, gst, NT)
                dx = _dot(t["m0"], dy0, TN) + _dot(t["m1"], dy1, TN) + bg * t["dte_p"]
                dm0, dm1 = _dot(dy0, t["x"], NT), _dot(dy1, t["x"], NT)
                dcb = dcb + dm0 * t["lam0"] + dm1 * t["lam1"]
                dye = dyp * t["ecs_p"]
                dc = dc + _dot(dye, hp)
                db = db + _dot(t["xd"], gst)
                i0 = jnp.sum(jnp.where(stril, _dot(triu, dm0 * t["m0"]), 0.0), axis=1, keepdims=True)
                i1 = jnp.sum(jnp.where(stril, _dot(triu, dm1 * t["m1"]), 0.0), axis=1, keepdims=True)
                yo0, yo1 = halves(dyp * t["yoff"], lo)
                w0, w1 = halves(t["xd"] * bg, lo)
                gh = gst * (hp * t["decay"])
                e0 = _sum0(jnp.sum(jnp.where(rlo, gh, 0.0), axis=1, keepdims=True))
                e1 = _sum0(jnp.sum(jnp.where(rlo, 0.0, gh), axis=1, keepdims=True))
                x0, x1 = halves(dx * xp, lo)
                oh0 = (lane_row == h0).astype(F32)
                oh1 = (lane_row == h0 + 1).astype(F32)
                i_all = i_all + i0 * oh0 + i1 * oh1
                yo_all = yo_all + yo0 * oh0 + yo1 * oh1
                w_all = w_all + w0 * oh0 + w1 * oh1
                e_row = e_row + e0 * oh0 + e1 * oh1
                ddt_x = ddt_x + x0 * oh0 + x1 * oh1
                dxs.append(dx * _pair_sel(dtv, h0, lo))
                g_scr[p4] = gst * t["decay"] + _dot(dye, cmat, TN)
            dbs.append(db + _dot(dcb, cmat, TN))
            dcs.append(dc + _dot(dcb, bmat))
        da = i_all + _dot_exact(triu, yo_all) + _dot_exact(stril.astype(F32), w_all) + e_row
        ddt_ref[...] = da * a_row_v + ddt_x
        da_ref[...] += _sum0(da * dtv)
        dxs_ref[...] = jnp.concatenate(dxs, axis=1)
        dbc_ref[...] = jnp.concatenate(dbs + dcs, axis=1)

    rev = lambda c: nc - 1 - c
    return pl.pallas_call(
        body, name=name, grid=(nc,),
        in_specs=[pl.BlockSpec((L, 512), lambda c: (rev(c), 0)), pl.BlockSpec((L, 256), lambda c: (rev(c), 2)),
                  pl.BlockSpec((L, 256), lambda c: (rev(c), 3)), pl.BlockSpec((L, 128), lambda c: (rev(c), 0)),
                  pl.BlockSpec((1, 128), lambda c: (0, 0)), pl.BlockSpec((1, 512, 128), lambda c: (rev(c), 0, 0)),
                  pl.BlockSpec((L, 512), lambda c: (rev(c), 0))],
        out_specs=[pl.BlockSpec((L, 512), lambda c: (rev(c), 0)), pl.BlockSpec((L, 512), lambda c: (rev(c), 0)),
                   pl.BlockSpec((L, 128), lambda c: (rev(c), 0)), pl.BlockSpec((1, 128), lambda c: (0, 0))],
        out_shape=[jax.ShapeDtypeStruct((S, 512), F32), jax.ShapeDtypeStruct((S, 512), F32),
                   jax.ShapeDtypeStruct((S, 128), F32), jax.ShapeDtypeStruct((1, 128), F32)],
        scratch_shapes=[pltpu.VMEM((4, 128, 128), F32)],
        compiler_params=_params(("arbitrary",)),
    )(xbc, xbc, xbc, dt, a_row, hs, dy)


MLA_SCALE = MLA_QK ** -0.5


def _causal_scores(q, k, qi, ki, t):
    s = _dot(q, k, NT) * MLA_SCALE
    row = qi * t + lax.broadcasted_iota(jnp.int32, (t, t), 0)
    col = ki * t + lax.broadcasted_iota(jnp.int32, (t, t), 1)
    return jnp.where(col <= row, s, NEG)


def _mla_fwd(q, k, kv, *, name):
    S = q.shape[0]
    t = min(ATTN_TILE, S)
    nq = S // t

    def body(q_ref, k_ref, v_ref, o_ref, lse_ref, m_scr, l_scr, acc_scr):
        qi, ki = pl.program_id(1), pl.program_id(2)

        @pl.when(ki == 0)
        def _():
            m_scr[...] = jnp.full_like(m_scr, NEG)
            l_scr[...] = jnp.zeros_like(l_scr)
            acc_scr[...] = jnp.zeros_like(acc_scr)

        @pl.when(ki <= qi)
        def _():
            s = _causal_scores(q_ref[...], k_ref[...], qi, ki, t)
            m_old = m_scr[:, 0:1]
            m_new = jnp.maximum(m_old, jnp.max(s, axis=1, keepdims=True))
            p = jnp.exp(s - m_new)
            corr = jnp.exp(m_old - m_new)
            l_scr[...] = jnp.broadcast_to(corr * l_scr[:, 0:1] + jnp.sum(p, axis=1, keepdims=True), l_scr.shape)
            acc_scr[...] = corr * acc_scr[...] + _dot(p, v_ref[...])
            m_scr[...] = jnp.broadcast_to(m_new, m_scr.shape)

        @pl.when(ki == nq - 1)
        def _():
            l = l_scr[:, 0:1]
            o_ref[...] = acc_scr[...] / l
            lse_ref[0] = jnp.broadcast_to(m_scr[:, 0:1] + jnp.log(l), (t, LANES))

    return pl.pallas_call(
        body, name=name, grid=(MLA_HEADS, nq, nq),
        in_specs=[pl.BlockSpec((t, 128), lambda h, qi, ki: (qi, h)),
                  pl.BlockSpec((t, 128), lambda h, qi, ki: (jnp.minimum(ki, qi), h)),
                  pl.BlockSpec((t, 128), lambda h, qi, ki: (jnp.minimum(ki, qi), 2 * h + 1))],
        out_specs=[pl.BlockSpec((t, 128), lambda h, qi, ki: (qi, h)),
                   pl.BlockSpec((1, t, 128), lambda h, qi, ki: (h, qi, 0))],
        out_shape=[jax.ShapeDtypeStruct((S, MLA_HEADS * 128), F32), jax.ShapeDtypeStruct((MLA_HEADS, S, 128), F32)],
        scratch_shapes=[pltpu.VMEM((t, 128), F32), pltpu.VMEM((t, 128), F32), pltpu.VMEM((t, 128), F32)],
        compiler_params=_params(("parallel", "parallel", "arbitrary")),
    )(q, k, kv)


def _mla_bwd_dkv(q, k, kv, o, do, lse, *, name):
    S = q.shape[0]
    t = min(ATTN_TILE, S)
    nq = S // t

    def body(q_ref, k_ref, v_ref, o_ref, do_ref, lse_ref, dkv_ref):
        ki, qi = pl.program_id(1), pl.program_id(2)

        @pl.when(qi == 0)
        def _():
            dkv_ref[...] = jnp.zeros_like(dkv_ref)

        @pl.when(qi >= ki)
        def _():
            qv, dov = q_ref[...], do_ref[...]
            s = _causal_scores(qv, k_ref[...], qi, ki, t)
            p = jnp.exp(s - lse_ref[0][:, 0:1])
            dv = _dot(p, dov, TN)
            dp = _dot(dov, v_ref[...], NT)
            delta = jnp.sum(dov * o_ref[...], axis=1, keepdims=True)
            ds = p * (dp - delta) * MLA_SCALE
            dkv_ref[...] += jnp.concatenate([_dot(ds, qv, TN), dv], axis=1)

    qmap = lambda h, ki, qi: (jnp.maximum(qi, ki), h)
    return pl.pallas_call(
        body, name=name, grid=(MLA_HEADS, nq, nq),
        in_specs=[pl.BlockSpec((t, 128), qmap),
                  pl.BlockSpec((t, 128), lambda h, ki, qi: (ki, h)),
                  pl.BlockSpec((t, 128), lambda h, ki, qi: (ki, 2 * h + 1)),
                  pl.BlockSpec((t, 128), qmap), pl.BlockSpec((t, 128), qmap),
                  pl.BlockSpec((1, t, 128), lambda h, ki, qi: (h, jnp.maximum(qi, ki), 0))],
        out_specs=pl.BlockSpec((t, 256), lambda h, ki, qi: (ki, h)),
        out_shape=jax.ShapeDtypeStruct((S, MLA_HEADS * 256), F32),
        compiler_params=_params(("parallel", "parallel", "arbitrary")),
    )(q, k, kv, o, do, lse)


def _mla_bwd_dq(q, k, kv, o, do, lse, *, name):
    S = q.shape[0]
    t = min(ATTN_TILE, S)
    nq = S // t

    def body(q_ref, k_ref, v_ref, o_ref, do_ref, lse_ref, dq_ref):
        qi, ki = pl.program_id(1), pl.program_id(2)

        @pl.when(ki == 0)
        def _():
            dq_ref[...] = jnp.zeros_like(dq_ref)

        @pl.when(ki <= qi)
        def _():
            dov, kv = do_ref[...], k_ref[...]
            s = _causal_scores(q_ref[...], kv, qi, ki, t)
            p = jnp.exp(s - lse_ref[0][:, 0:1])
            dp = _dot(dov, v_ref[...], NT)
            delta = jnp.sum(dov * o_ref[...], axis=1, keepdims=True)
            ds = p * (dp - delta) * MLA_SCALE
            dq_ref[...] += _dot(ds, kv)

    qmap = lambda h, qi, ki: (qi, h)
    return pl.pallas_call(
        body, name=name, grid=(MLA_HEADS, nq, nq),
        in_specs=[pl.BlockSpec((t, 128), qmap),
                  pl.BlockSpec((t, 128), lambda h, qi, ki: (jnp.minimum(ki, qi), h)),
                  pl.BlockSpec((t, 128), lambda h, qi, ki: (jnp.minimum(ki, qi), 2 * h + 1)),
                  pl.BlockSpec((t, 128), qmap), pl.BlockSpec((t, 128), qmap),
                  pl.BlockSpec((1, t, 128), lambda h, qi, ki: (h, qi, 0))],
        out_specs=pl.BlockSpec((t, 128), qmap),
        out_shape=jax.ShapeDtypeStruct((S, MLA_HEADS * 128), F32),
        compiler_params=_params(("parallel", "parallel", "arbitrary")),
    )(q, k, kv, o, do, lse)


HBM = pl.BlockSpec(memory_space=pl.ANY)


class _Step:
    def __init__(self, inputs, out_shapes, n_sems, start, finish, mid=None):
        self.inputs, self.out_shapes, self.n_sems = inputs, out_shapes, n_sems
        self.start, self.finish, self.mid = start, finish, mid


def _place():
    x, y, c = lax.axis_index("x"), lax.axis_index("y"), lax.axis_index("c")
    chips = [(1 - x, y), (x, 1 - y), (1 - x, 1 - y)]
    return x, y, c, chips


def _chunks(rows, tile):
    return next(n for n in (4, 3, 2, 1) if rows % (n * tile) == 0)


def _remote(src, dst, sems, j, to):
    return pltpu.make_async_remote_copy(src_ref=src, dst_ref=dst, send_sem=sems[0].at[j], recv_sem=sems[1].at[j],
                                        device_id=to, device_id_type=MESH)


def _gather_step(wp):
    R, C = wp.shape
    H = R // 2
    nq = _chunks(H, 16)
    CH = H // nq

    def copies(ins, outs, sems):
        x, y, c, chips = _place()
        sib, me = (x, y, 1 - c), 2 * x + y
        w_ref, out_ref = ins[0], outs[0]

        def piece(k, hc, q):
            return out_ref.at[k, pl.ds(hc * H + q * CH, CH), :]

        sends, landed, fwds, fwd_landed = [], [], [], []
        for q in range(nq):
            for j, (px, py) in enumerate(chips):
                k = 2 * px + py
                sends.append(_remote(w_ref.at[pl.ds(c * H + q * CH, CH), :], piece(me, c, q), sems, j * nq + q,
                                     (px, py, c)))
                landed.append(_remote(piece(k, c, q), piece(k, c, q), sems, j * nq + q, (px, py, c)))
                fwds.append(_remote(piece(k, c, q), piece(k, c, q), sems, (3 + j) * nq + q, sib))
                fwd_landed.append(_remote(piece(k, 1 - c, q), piece(k, 1 - c, q), sems, (3 + j) * nq + q, sib))
        return sends, landed, fwds, fwd_landed

    def start(ins, outs, sems):
        for cp in copies(ins, outs, sems)[0]:
            cp.start()

    def mid(ins, outs, sems):
        _, landed, fwds, _ = copies(ins, outs, sems)
        for arrived, onward in zip(landed, fwds):
            arrived.wait_recv()
            onward.start()

    def finish(ins, outs, sems):
        sends, _, fwds, fwd_landed = copies(ins, outs, sems)
        for cp in fwd_landed:
            cp.wait_recv()
        for cp in sends + fwds:
            cp.wait_send()

    return _Step([wp], [jax.ShapeDtypeStruct((N_SHARD, R, C), wp.dtype)], 6 * nq, start, finish, mid)


def _pair_exchange_step(gp):
    n, R, C = gp.shape
    H = R // 2
    nq = _chunks(H, 8)
    CH = H // nq

    def copies(ins, outs, sems):
        x, y, c, _ = _place()
        return [_remote(ins[0].at[k, pl.ds((1 - c) * H + q * CH, CH), :], outs[0].at[k, pl.ds(q * CH, CH), :], sems,
                        k * nq + q, (x, y, 1 - c)) for k in range(n) for q in range(nq)]

    def start(ins, outs, sems):
        for cp in copies(ins, outs, sems):
            cp.start()

    def finish(ins, outs, sems):
        for cp in copies(ins, outs, sems):
            cp.wait()

    return _Step([gp], [jax.ShapeDtypeStruct((n, H, C), gp.dtype)], n * nq, start, finish)


def _chip_exchange_step(pb):
    n, H, C = pb.shape
    nq = _chunks(H, 16)
    CH = H // nq

    def copies(ins, outs, sems):
        x, y, c, chips = _place()
        return [_remote(ins[0].at[2 * px + py, pl.ds(q * CH, CH), :], outs[0].at[j, pl.ds(q * CH, CH), :], sems,
                        j * nq + q, (px, py, c)) for q in range(nq) for j, (px, py) in enumerate(chips)]

    def start(ins, outs, sems):
        for cp in copies(ins, outs, sems):
            cp.start()

    def finish(ins, outs, sems):
        for cp in copies(ins, outs, sems):
            cp.wait()

    return _Step([pb], [jax.ShapeDtypeStruct((3, H, C), pb.dtype)], 3 * nq, start, finish)


def _pair_join_step(q):
    H, C = q.shape
    nq = _chunks(H, 8)
    CH = H // nq

    def copies(ins, outs, sems):
        x, y, c, _ = _place()
        return [_remote(ins[0].at[pl.ds(j * CH, CH), :], outs[0].at[pl.ds(j * CH, CH), :], sems, j, (x, y, 1 - c))
                for j in range(nq)]

    def start(ins, outs, sems):
        for cp in copies(ins, outs, sems):
            cp.start()

    def finish(ins, outs, sems):
        for cp in copies(ins, outs, sems):
            cp.wait()

    return _Step([q], [jax.ShapeDtypeStruct((H, C), q.dtype)], nq, start, finish)


def _sem_scratch(step):
    return [pltpu.SemaphoreType.DMA((step.n_sems,)), pltpu.SemaphoreType.DMA((step.n_sems,))]


def _run_step(step, name):
    ni, no = len(step.inputs), len(step.out_shapes)

    def body(*refs):
        ins, outs, sems = refs[:ni], refs[ni:ni + no], refs[ni + no:]
        step.start(ins, outs, sems)
        if step.mid is not None:
            step.mid(ins, outs, sems)
        step.finish(ins, outs, sems)

    return pl.pallas_call(body, name=name, in_specs=[HBM] * ni, out_specs=[HBM] * no, out_shape=step.out_shapes,
                          scratch_shapes=_sem_scratch(step))(*step.inputs)


def _call_with_step(core, step, flags, args, *, name, grid, in_specs, out_specs, out_shape, sem):
    if step is None:
        return pl.pallas_call(core, name=name, grid=grid, in_specs=in_specs, out_specs=out_specs,
                              out_shape=out_shape, compiler_params=_params(sem))(*args)
    n_in, n_out = len(in_specs), len(out_specs)
    si, so = len(step.inputs), len(step.out_shapes)

    def body(*refs):
        ins, s_ins = refs[:n_in], refs[n_in:n_in + si]
        outs = refs[n_in + si:n_in + si + n_out]
        s_outs = refs[n_in + si + n_out:n_in + si + n_out + so]
        sems = refs[n_in + si + n_out + so:]
        first, middle, last = flags()

        @pl.when(first)
        def _():
            step.start(s_ins, s_outs, sems)

        if step.mid is not None:
            @pl.when(middle)
            def _():
                step.mid(s_ins, s_outs, sems)

        core(*ins, *outs)

        @pl.when(last)
        def _():
            step.finish(s_ins, s_outs, sems)

    return pl.pallas_call(
        body, name=name, grid=grid, in_specs=list(in_specs) + [HBM] * si, out_specs=list(out_specs) + [HBM] * so,
        out_shape=list(out_shape) + list(step.out_shapes), scratch_shapes=_sem_scratch(step),
        compiler_params=_params(("arbitrary",) * len(grid)))(*args, *step.inputs)


def _attn_flags(nq):
    h, qi = pl.program_id(0), pl.program_id(1)
    return ((h == 0) & (qi == 0), (h == MLA_HEADS // 2) & (qi == 0), (h == MLA_HEADS - 1) & (qi == nq - 1))


def _att_mask(s_t, qi, kb, t):
    krow = kb * t + lax.broadcasted_iota(jnp.int32, (t, t), 0)
    qcol = qi * t + lax.broadcasted_iota(jnp.int32, (t, t), 1)
    return jnp.where(krow <= qcol, s_t, NEG)


def _rows(ref, blk, t):
    return ref[pl.ds(pl.multiple_of(blk * t, t), t), :]


def _cols(ref, blk, t):
    return ref[:, pl.ds(pl.multiple_of(blk * t, t), t)]


def _attn_fwd(q, k, v_t, *, name, hosted=None):
    S = q.shape[0]
    t = min(ATTN_TILE, S)
    nq = S // t

    def body(q_ref, k_ref, vt_ref, o_ref, lse_ref):
        qi = pl.program_id(1)
        qv = q_ref[...]

        def step(kb, carry, masked):
            m, l, acc = carry
            s_t = lax.dot_general(_rows(k_ref, kb, t), qv, NT, preferred_element_type=F32)
            if masked:
                s_t = _att_mask(s_t, qi, kb, t)
            m_new = jnp.maximum(m, jnp.max(s_t, axis=0, keepdims=True))
            p_t = jnp.exp(s_t - m_new)
            corr = jnp.exp(m - m_new)
            l = corr * l + jnp.sum(p_t, axis=0, keepdims=True)
            acc = corr * acc + lax.dot_general(_cols(vt_ref, kb, t), p_t.astype(BF16), NN, preferred_element_type=F32)
            return m_new, l, acc

        init = (jnp.full((1, t), NEG, F32), jnp.zeros((1, t), F32), jnp.zeros((LANES, t), F32))
        carry = lax.fori_loop(0, qi, lambda kb, c: step(kb, c, False), init)
        m, l, acc = step(qi, carry, True)
        o_ref[...] = acc / l
        lse_ref[0] = m + jnp.log(l)

    return _call_with_step(
        body, hosted, lambda: _attn_flags(nq), (q, k, v_t), name=name, grid=(MLA_HEADS, nq),
        in_specs=[pl.BlockSpec((t, LANES), lambda h, qi: (qi, h)),
                  pl.BlockSpec((S, LANES), lambda h, qi: (0, h)),
                  pl.BlockSpec((LANES, S), lambda h, qi: (h, 0))],
        out_specs=[pl.BlockSpec((LANES, t), lambda h, qi: (h, qi)),
                   pl.BlockSpec((1, 1, t), lambda h, qi: (h, 0, qi))],
        out_shape=[jax.ShapeDtypeStruct((MLA_HEADS * LANES, S), F32), jax.ShapeDtypeStruct((MLA_HEADS, 1, S), F32)],
        sem=("parallel", "arbitrary"))


def _attn_bwd_dq(q, k, k_t, v, o_t, do_t, lse, *, name, hosted=None):
    S = q.shape[0]
    t = min(ATTN_TILE, S)
    nq = S // t

    def body(q_ref, k_ref, kt_ref, v_ref, o_ref, do_ref, lse_ref, dq_ref, delta_ref):
        qi = pl.program_id(1)
        qv = q_ref[...]
        dov = do_ref[...]
        delta = jnp.sum(dov * o_ref[...], axis=0, keepdims=True)
        delta_ref[0] = delta
        dob = dov.astype(BF16)
        lse_v = lse_ref[0]

        def step(kb, acc, masked):
            s_t = lax.dot_general(_rows(k_ref, kb, t), qv, NT, preferred_element_type=F32)
            if masked:
                s_t = _att_mask(s_t, qi, kb, t)
            p_t = jnp.exp(s_t - lse_v)
            dp_t = lax.dot_general(_rows(v_ref, kb, t), dob, NN, preferred_element_type=F32)
            ds_t = (p_t * (dp_t - delta)).astype(BF16)
            return acc + lax.dot_general(_cols(kt_ref, kb, t), ds_t, NN, preferred_element_type=F32)

        acc = lax.fori_loop(0, qi, lambda kb, c: step(kb, c, False), jnp.zeros((LANES, t), F32))
        dq_ref[...] = step(qi, acc, True)

    tile_t = pl.BlockSpec((LANES, t), lambda h, qi: (h, qi))
    stat = pl.BlockSpec((1, 1, t), lambda h, qi: (h, 0, qi))
    seq = pl.BlockSpec((S, LANES), lambda h, qi: (0, h))
    return _call_with_step(
        body, hosted, lambda: _attn_flags(nq), (q, k, k_t, v, o_t, do_t, lse), name=name, grid=(MLA_HEADS, nq),
        in_specs=[pl.BlockSpec((t, LANES), lambda h, qi: (qi, h)), seq,
                  pl.BlockSpec((LANES, S), lambda h, qi: (h, 0)), seq, tile_t, tile_t, stat],
        out_specs=[tile_t, stat],
        out_shape=[jax.ShapeDtypeStruct((MLA_HEADS * LANES, S), F32), jax.ShapeDtypeStruct((MLA_HEADS, 1, S), F32)],
        sem=("parallel", "arbitrary"))


def _attn_bwd_dkv(q, q_t, k, v, do_t, lse, delta, *, name, hosted=None):
    S = q.shape[0]
    t = min(ATTN_TILE, S)
    nq = S // t

    def body(q_ref, qt_ref, k_ref, v_ref, do_ref, lse_ref, delta_ref, dk_ref, dv_ref):
        ki = pl.program_id(1)
        kv, vv = k_ref[...], v_ref[...]

        def step(qb, carry, masked):
            dk, dv = carry
            s_t = lax.dot_general(kv, _rows(q_ref, qb, t), NT, preferred_element_type=F32)
            if masked:
                s_t = _att_mask(s_t, qb, ki, t)
            p_t = jnp.exp(s_t - _cols(lse_ref.at[0], qb, t))
            dob = _cols(do_ref, qb, t).astype(BF16)
            dv = dv + lax.dot_general(dob, p_t.astype(BF16), NT, preferred_element_type=F32)
            dp_t = lax.dot_general(vv, dob, NN, preferred_element_type=F32)
            ds_t = (p_t * (dp_t - _cols(delta_ref.at[0], qb, t))).astype(BF16)
            dk = dk + lax.dot_general(_cols(qt_ref, qb, t), ds_t, NT, preferred_element_type=F32)
            return dk, dv

        zero = jnp.zeros((LANES, t), F32)
        carry = step(ki, (zero, zero), True)
        dk, dv = lax.fori_loop(ki + 1, nq, lambda qb, c: step(qb, c, False), carry)
        dk_ref[...] = dk
        dv_ref[...] = dv

    tile = pl.BlockSpec((t, LANES), lambda h, ki: (ki, h))
    tile_t = pl.BlockSpec((LANES, t), lambda h, ki: (h, ki))
    seq = pl.BlockSpec((S, LANES), lambda h, ki: (0, h))
    seq_t = pl.BlockSpec((LANES, S), lambda h, ki: (h, 0))
    stat = pl.BlockSpec((1, 1, S), lambda h, ki: (h, 0, 0))
    return _call_with_step(
        body, hosted, lambda: _attn_flags(nq), (q, q_t, k, v, do_t, lse, delta), name=name, grid=(MLA_HEADS, nq),
        in_specs=[seq, seq_t, tile, tile, seq_t, stat, stat],
        out_specs=[tile_t, tile_t],
        out_shape=[jax.ShapeDtypeStruct((MLA_HEADS * LANES, S), F32)] * 2,
        sem=("parallel", "arbitrary"))


def _fn_ln(ctx, x, g, b):
    xhat, _ = _ln_stats(x)
    return (xhat * g + b,)


def _fn_conv_fwd(ctx, u, up, dtr, w8, cb, dtb):
    first = ctx.i == 0
    y = u * w8[3:4] + cb
    for s in (1, 2, 3):
        y = y + _shift_down(u, up, s, first) * w8[3 - s:4 - s]
    act = y * _sigmoid(y)
    v = dtr + dtb
    e = jnp.exp(-jnp.abs(v))
    one_p = 1.0 + e
    log1p = jnp.where(one_p == 1.0, e, jnp.log(one_p) * e / (one_p - 1.0))
    return y, act, jnp.maximum(v, 0.0) + log1p


def _fn_ssd_post(ctx, y, xs, z, dexp, g):
    yg = (y + xs * dexp) * (z * _sigmoid(z))
    outs = []
    for k in range(2):
        v = yg[:, 256 * k:256 * (k + 1)]
        outs.append(v * lax.rsqrt(_mean1(v * v) + RMS_EPS))
    return (jnp.concatenate(outs, axis=1) * g,)


def _fn_ssd_post_bwd(ctx, dyn, y, xs, z, dexp, g):
    yt = y + xs * dexp
    sig = _sigmoid(z)
    sz = z * sig
    yg = yt * sz
    dyh = dyn * g
    yh, dyg = [], []
    for k in range(2):
        sl = slice(256 * k, 256 * (k + 1))
        v = yg[:, sl]
        rs = lax.rsqrt(_mean1(v * v) + RMS_EPS)
        vh = v * rs
        yh.append(vh)
        dyg.append(rs * (dyh[:, sl] - vh * _mean1(dyh[:, sl] * vh)))
    yh = jnp.concatenate(yh, axis=1)
    dyg = jnp.concatenate(dyg, axis=1)
    dyt = dyg * sz
    dz = dyg * yt * (sig * (1.0 + z * (1.0 - sig)))
    return dyt, dz, dyt * dexp, _sum0(dyt * xs), _sum0(dyn * yh)


def _fn_mla_pre(ctx, ql, kvl, gq, gkv):
    return _rms_fwd(ql, gq), _rms_fwd(kvl, gkv)


def _fn_mla_pre_bwd(ctx, ql, kvl, dqn, dkvn_k, dkvn_v, gq, gkv):
    dql, dgq = _rms_bwd(ql, dqn, gq)
    dkvl, dgkv = _rms_bwd(kvl, dkvn_k + dkvn_v, gkv)
    return dql, dkvl, dgq, dgkv


def _fn_rope(ctx, qp, kn, kr, ta, tb, tc):
    kpe = _rope(kr, ta, tb, tc)
    qs, ks = [], []
    for h in range(MLA_HEADS):
        sl = slice(128 * h, 128 * (h + 1))
        qs.append(_rope(qp[:, sl], ta, tb, tc) * MLA_SCALE)
        ks.append(kn[:, sl] + kpe)
    return jnp.concatenate(qs, axis=1), jnp.concatenate(ks, axis=1)


def _fn_rope_bwd(ctx, dq, dk, ta, tb, tc):
    qs = []
    ksum = jnp.zeros_like(ta)
    for h in range(MLA_HEADS):
        sl = slice(128 * h, 128 * (h + 1))
        qs.append(_rope_bwd(dq[:, sl] * MLA_SCALE, ta, tb, tc))
        ksum = ksum + dk[:, sl]
    lane = _lane(ksum.shape)
    dkr = jnp.where((lane >= 64) & (lane < 96), _rope_bwd(ksum, ta, tb, tc), 0.0)
    return jnp.concatenate(qs, axis=1), dkr


MEM_SCALE = MEM_HEAD_DIM ** -0.5


def _mem_probs(qh, kh):
    s = _dot(qh, kh, NT) * MEM_SCALE
    p = jnp.exp(s - jnp.max(s, axis=1, keepdims=True))
    return p / jnp.sum(p, axis=1, keepdims=True)


def _fn_mem_fwd(ctx, q, km, vm):
    outs = []
    for h in range(MEM_HEADS):
        sl = slice(256 * h, 256 * (h + 1))
        outs.append(_dot(_mem_probs(q[:, sl], km[:, sl]), vm[:, sl]))
    return (jnp.concatenate(outs, axis=1),)


def _fn_mem_bwd(ctx, q, do, km, vm):
    dqs, dks, dvs = [], [], []
    for h in range(MEM_HEADS):
        sl = slice(256 * h, 256 * (h + 1))
        p = _mem_probs(q[:, sl], km[:, sl])
        dvs.append(_dot(p, do[:, sl], TN))
        dp = _dot(do[:, sl], vm[:, sl], NT)
        ds = p * (dp - jnp.sum(dp * p, axis=1, keepdims=True)) * MEM_SCALE
        dqs.append(_dot(ds, km[:, sl]))
        dks.append(_dot(ds, q[:, sl], TN))
    return jnp.concatenate(dqs, axis=1), jnp.concatenate(dks, axis=1), jnp.concatenate(dvs, axis=1)


def _fn_res_ln(ctx, h, r, g, b):
    xhat, _ = _ln_stats(ALPHA * h + r)
    return (xhat * g + b,)


def _fn_res_ln_bwd(ctx, h, r, d1, d2, g):
    xhat, rstd = _ln_stats(ALPHA * h + r)
    return _ln_bwd(xhat, rstd, ALPHA * d1 + d2, g)


def _fn_res2_ln(ctx, h, r1, r2, g, b):
    xhat, _ = _ln_stats(ALPHA * h + (r1 + r2))
    return (xhat * g + b,)


def _fn_res2_ln_bwd(ctx, h, r1, r2, d1, d2, g):
    xhat, rstd = _ln_stats(ALPHA * h + (r1 + r2))
    return _ln_bwd(xhat, rstd, ALPHA * d1 + d2, g)


def _fn_in_ln_bwd(ctx, x, d1, d2, g):
    xhat, rstd = _ln_stats(x)
    return _ln_bwd(xhat, rstd, ALPHA * d1 + d2, g)


def _fn_final(ctx, h2, ff, tgt, g, b):
    xhat, rstd = _ln_stats(ALPHA * h2 + ff)
    e = xhat * g + b - tgt
    loss = 0.5 * _sum0(jnp.sum(e * e, axis=1, keepdims=True)) / D_MODEL
    dx, dg, db = _ln_bwd(xhat, rstd, e / D_MODEL, g)
    return dx, dg, db, loss


def _fn_du(ctx, u, da):
    return (da * 2.0 * jnp.maximum(u, 0.0),)


def _relu2(u):
    r = jnp.maximum(u, 0.0)
    return r * r


def _fn_conv_bwd_a(ctx, y, dxs1, dxs2, dbc, dtr, ddt, dtb):
    sig = _sigmoid(y)
    dact = jnp.concatenate([dxs1 + dxs2, dbc], axis=1)
    dyc = dact * (sig * (1.0 + y * (1.0 - sig)))
    ddtr = ddt * _sigmoid(dtr + dtb)
    return dyc, ddtr, _sum0(dyc), _sum0(ddtr)


def _fn_conv_bwd_b(ctx, d, dn, u, up, w8):
    first, last = ctx.i == 0, ctx.i == ctx.n - 1
    du = d * w8[3:4]
    row = lax.broadcasted_iota(jnp.int32, w8.shape, 0)
    dw = jnp.where(row == 3, _sum0(d * u), 0.0)
    for s in (1, 2, 3):
        du = du + _shift_up(d, dn, s, last) * w8[3 - s:4 - s]
        dw = dw + jnp.where(row == 3 - s, _sum0(d * _shift_down(u, up, s, first)), 0.0)
    return du, dw


def _fn_adam(ctx, w, g, m, v):
    m = ADAM_B1 * m + (1.0 - ADAM_B1) * g
    v = ADAM_B2 * v + (1.0 - ADAM_B2) * (g * g)
    m_hat = m / (1.0 - ADAM_B1 ** ADAM_STEP)
    v_hat = v / (1.0 - ADAM_B2 ** ADAM_STEP)
    return -ADAM_LR * (m_hat / (jnp.sqrt(v_hat) + ADAM_EPS) + ADAM_WD * w), m, v


def _fn_add2(ctx, a, b):
    s = a + b
    return s, s


def _fn_add4(ctx, a, r0, r1, r2):
    return (((a + r0.astype(F32)) + r1.astype(F32)) + r2.astype(F32),)


def _z(r, c, dt):
    return jnp.zeros((r, c), dt)


def _pad_w_in(w):
    r, dt = w.shape[0], w.dtype
    return jnp.concatenate([w[:, 512:1536], w[:, 0:512], w[:, 1544:1928], w[:, 1536:1544], _z(r, 120, dt),
                            w[:, 1928:2184], _z(r, 64, dt), w[:, 2184:2216], _z(r, 32, dt), _z(r, 128, dt)], axis=1)


def _unpad_w_in(d):
    return jnp.concatenate([d[:, 1024:1536], d[:, 0:1024], d[:, 1920:1928], d[:, 1536:1920], d[:, 2048:2304],
                            d[:, 2368:2400]], axis=1)


def _pad_heads(w, width):
    r = w.shape[0]
    w3 = w.reshape(r, MLA_HEADS, width)
    return jnp.pad(w3, ((0, 0), (0, 0), (0, 128 - width))).reshape(r, MLA_HEADS * 128)


def _pad_w_kv(w):
    r = w.shape[0]
    w4 = w.reshape(r, MLA_HEADS, 2, 64)
    return jnp.pad(w4, ((0, 0), (0, 0), (0, 0), (0, 64))).reshape(r, MLA_HEADS * 256)


def _unpad_w_kv(d):
    r = d.shape[0]
    return d.reshape(r, MLA_HEADS, 2, 128)[:, :, :, :64].reshape(r, MLA_HEADS * 128)


def _pad_w_mix(w):
    wo = jnp.pad(w[512:1024].reshape(MLA_HEADS, 64, D_MODEL), ((0, 0), (0, 64), (0, 0))).reshape(1024, D_MODEL)
    return jnp.concatenate([wo, w[0:512]], axis=0)


def _unpad_w_mix(d):
    do = d[:1024].reshape(MLA_HEADS, 128, D_MODEL)[:, :64].reshape(512, D_MODEL)
    return jnp.concatenate([d[1024:1536], do], axis=0)


def _row(v, width=None):
    v = v.reshape(1, -1).astype(F32)
    if width is not None and v.shape[1] < width:
        v = jnp.pad(v, ((0, 0), (0, width - v.shape[1])))
    return v


def _old_local_step(x, mem, positions, target, W, P):
    S = x.shape[0]
    tr = ROW_TILE
    w_in_p = _pad_w_in(W["w_in"])
    w_q_p = _pad_heads(W["w_q_up"], MLA_QK)
    w_kv3 = W["w_kv_up"].reshape(MLA_KV_RANK, MLA_HEADS, 128)
    w_k_p = _pad_heads(w_kv3[:, :, :64].reshape(MLA_KV_RANK, 512), 64)
    w_v_p = _pad_heads(w_kv3[:, :, 64:].reshape(MLA_KV_RANK, 512), 64)
    w_v_pt = w_v_p.T
    w_mix_y = W["w_mix_out"][0:512]
    w_mix_o = jnp.pad(W["w_mix_out"][512:1024].reshape(MLA_HEADS, 64, D_MODEL),
                      ((0, 0), (0, 64), (0, 0))).reshape(MLA_HEADS * 128, D_MODEL)
    conv_w8 = jnp.pad(P["conv_w"].astype(F32), ((0, 4), (0, 0)))
    conv_b = _row(P["conv_b"])
    dt_b = _row(P["dt_bias"], 128)
    a_head = -jnp.exp(P["a_log"].reshape(-1).astype(F32))
    a_row = _row(a_head, 128)
    dexp = jnp.repeat(P["d_skip"].reshape(-1).astype(F32), 64).reshape(1, 512)
    g_ssd, g_q, g_kv = _row(P["ssd_norm_g"]), _row(P["q_norm_g"]), _row(P["kv_norm_g"])
    g_in, b_in = _row(P["ln_in_g"]), _row(P["ln_in_b"])
    g1, b1, g2, b2, g3, b3 = (_row(P[k]) for k in ("ln1_g", "ln1_b", "ln2_g", "ln2_b", "ln3_g", "ln3_b"))

    half = MLA_ROPE // 2
    inv_freq = jnp.power(ROPE_THETA, -jnp.arange(half, dtype=F32) / half)
    ang = positions.reshape(S, 1).astype(F32) * inv_freq
    cos, sin = jnp.cos(ang), jnp.sin(ang)
    zc = lambda n: jnp.zeros((S, n), F32)
    rope_a = jnp.concatenate([jnp.ones((S, 64), F32), cos, cos, zc(32)], axis=1)
    rope_b = jnp.concatenate([zc(80), sin, zc(32)], axis=1)
    rope_c = jnp.concatenate([zc(64), -sin, zc(48)], axis=1)

    (h0,) = _rowwise(_fn_ln, [x], [g_in, b_in], [D_MODEL], tr=tr, name="ln_in")
    proj = _mm(h0, w_in_p, form="nn", name="mm_in")
    conv_y, xbc, dt = _rowwise(
        _fn_conv_fwd, [(proj,) + SEG_XBC, ("prev", proj) + SEG_XBC, (proj,) + SEG_DT], [conv_w8, conv_b, dt_b],
        [1024, 1024, 128], tr=tr, name="conv_fwd")
    y_ssd, hs = _ssd_fwd(xbc, dt, a_row, name="ssd_fwd")
    (y_n,) = _rowwise(_fn_ssd_post, [y_ssd, (xbc, 0, 512), (proj,) + SEG_Z], [dexp, g_ssd], [512], tr=tr,
                      name="ssd_post")
    q_n, kv_n = _rowwise(_fn_mla_pre, [(proj,) + SEG_QLAT, (proj,) + SEG_KVLAT], [g_q, g_kv], [384, 256], tr=tr,
                         name="mla_pre")
    qp = _mm(q_n, w_q_p, form="nn", name="mm_q_up")
    kn = _mm(kv_n, w_k_p, form="nn", name="mm_k_up")
    v_nat = _mm(kv_n, w_v_p, form="nn", out_dtype=BF16, name="mm_v_up")
    v_t = _mm(w_v_pt, kv_n, form="nt", out_dtype=BF16, name="mm_v_up_t")
    q_rot, k_full = _rowwise(_fn_rope, [qp, kn, (proj,) + SEG_KR, rope_a, rope_b, rope_c], [],
                             [(1024, BF16), (1024, BF16)], tr=tr, name="rope")
    o_t, lse = _attn_fwd(q_rot, k_full, v_t, name="attn_fwd")
    mix_o = _mm(o_t, w_mix_o, form="tn", name="mm_mix_o")
    mix_y = _mm(y_n, w_mix_y, form="nn", name="mm_mix_y")
    (h1,) = _rowwise(_fn_res2_ln, [h0, mix_o, mix_y], [g1, b1], [D_MODEL], tr=tr, name="ln1")
    qm = _mm(h1, W["w_mem_q"], form="nn", name="mm_mem_q")
    km = _mm(mem, W["w_mem_k"], form="nn", name="mm_mem_k")
    vm = _mm(mem, W["w_mem_v"], form="nn", name="mm_mem_v")
    (om,) = _rowwise(_fn_mem_fwd, [qm], [km, vm], [D_MODEL], tr=tr, name="mem_fwd")
    xa = _mm(om, W["w_mem_o"], form="nn", name="mm_mem_o")
    (h2,) = _rowwise(_fn_res_ln, [h1, xa], [g2, b2], [D_MODEL], tr=tr, name="ln2")
    u = _mm(h2, W["w_up"], form="nn", name="mm_up")
    ff = _mm(u, W["w_down"], form="nn", a_pro=_relu2, name="mm_down")

    dt3, dg3, db3, loss = _rowwise(_fn_final, [h2, ff, target], [g3, b3], [D_MODEL],
                                   [(1, D_MODEL), (1, D_MODEL), (1, 128)], tr=tr, name="ln3_loss")
    da = _mm(dt3, W["w_down"], form="nt", name="mm_down_dx")
    dw_down = _mm(u, dt3, form="tn", a_pro=_relu2, name="mm_down_dw")
    (du,) = _rowwise(_fn_du, [u, da], [], [4 * D_MODEL], tr=128, name="mlp_du")
    dw_up = _mm(h2, du, form="tn", name="mm_up_dw")
    dh2 = _mm(du, W["w_up"], form="nt", name="mm_up_dx")
    dt2, dg2, db2 = _rowwise(_fn_res_ln_bwd, [h1, xa, dt3, dh2], [g2], [D_MODEL], [(1, D_MODEL)] * 2, tr=tr,
                             name="ln2_bwd")
    dom = _mm(dt2, W["w_mem_o"], form="nt", name="mm_mem_o_dx")
    dw_mem_o = _mm(om, dt2, form="tn", name="mm_mem_o_dw")
    dqm, dkm, dvm = _rowwise(_fn_mem_bwd, [qm, dom], [km, vm], [D_MODEL], [(256, D_MODEL)] * 2, tr=tr,
                             name="mem_bwd")
    dw_mem_q = _mm(h1, dqm, form="tn", name="mm_mem_q_dw")
    dw_mem_k = _mm(mem, dkm, form="tn", name="mm_mem_k_dw")
    dw_mem_v = _mm(mem, dvm, form="tn", name="mm_mem_v_dw")
    dh1 = _mm(dqm, W["w_mem_q"], form="nt", name="mm_mem_q_dx")
    dt1, dg1, db1 = _rowwise(_fn_res2_ln_bwd, [h0, mix_o, mix_y, dt2, dh1], [g1], [D_MODEL], [(1, D_MODEL)] * 2,
                             tr=tr, name="ln1_bwd")
    do_t = _mm(w_mix_o, dt1, form="nt", name="mm_mix_o_dx")
    dy_n = _mm(dt1, w_mix_y, form="nt", name="mm_mix_y_dx")
    dw_mix_o = _mm(o_t, dt1, form="nn", name="mm_mix_o_dw")
    dw_mix_y = _mm(y_n, dt1, form="tn", name="mm_mix_y_dw")
    dq_t, delta = _attn_bwd_dq(q_rot, k_full, k_full.T, v_nat, o_t, do_t, lse, name="attn_bwd_dq")
    dk_t, dv_t = _attn_bwd_dkv(q_rot, q_rot.T, k_full, v_nat, do_t, lse, delta, name="attn_bwd_dkv")
    dk = dk_t.T
    dqp, dkr = _rowwise(_fn_rope_bwd, [dq_t.T, dk, rope_a, rope_b, rope_c], [], [1024, 128], tr=tr,
                        name="rope_bwd")
    dw_q_p = _mm(q_n, dqp, form="tn", name="mm_q_up_dw")
    dq_n = _mm(dqp, w_q_p, form="nt", name="mm_q_up_dx")
    dw_k_p = _mm(kv_n, dk, form="tn", name="mm_k_up_dw")
    dkv_n1 = _mm(dk, w_k_p, form="nt", name="mm_k_up_dx")
    dw_v_pt = _mm(dv_t, kv_n, form="nn", name="mm_v_up_dw")
    dkv_n2 = _mm(dv_t, w_v_pt, form="tn", name="mm_v_up_dx")
    dq_lat, dkv_lat, dg_q, dg_kv = _rowwise(
        _fn_mla_pre_bwd, [(proj,) + SEG_QLAT, (proj,) + SEG_KVLAT, dq_n, dkv_n1, dkv_n2], [g_q, g_kv], [384, 256],
        [(1, 384), (1, 256)], tr=tr, name="mla_pre_bwd")
    dy_ssd, dz, dxs_skip, ddexp, dg_ssd = _rowwise(
        _fn_ssd_post_bwd, [dy_n, y_ssd, (xbc, 0, 512), (proj,) + SEG_Z], [dexp, g_ssd],
        [512, 512, 512], [(1, 512)] * 2, tr=tr, name="ssd_post_bwd")
    dxs, dbc, ddt, da_head = _ssd_bwd(xbc, dt, a_row, hs, dy_ssd, name="ssd_bwd")
    dyc, ddtr, dconv_b, ddt_b = _rowwise(
        _fn_conv_bwd_a, [conv_y, dxs, dxs_skip, dbc, (proj,) + SEG_DT, ddt], [dt_b], [1024, 128],
        [(1, 1024), (1, 128)], tr=tr, name="conv_bwd_a")
    dxbc, dconv_w8 = _rowwise(
        _fn_conv_bwd_b, [dyc, ("next", dyc, 0, 1024), (proj,) + SEG_XBC, ("prev", proj) + SEG_XBC], [conv_w8], [1024],
        [(8, 1024)], tr=tr, name="conv_bwd_b")
    dproj = jnp.concatenate([dxbc, dz, dq_lat, ddtr, dkv_lat, dkr, jnp.zeros((S, 128), F32)], axis=1)
    dw_in_p = _mm(h0, dproj, form="tn", name="mm_in_dw")
    dh0 = _mm(dproj, w_in_p, form="nt", name="mm_in_dx")
    grad_x, dg_in, db_in = _rowwise(_fn_in_ln_bwd, [x, dt1, dh0], [g_in], [D_MODEL], [(1, D_MODEL)] * 2, tr=tr,
                                    name="ln_in_bwd")

    big = {
        "w_in": _unpad_w_in(dw_in_p),
        "w_q_up": dw_q_p.reshape(384, MLA_HEADS, 128)[:, :, :MLA_QK].reshape(384, MLA_HEADS * MLA_QK),
        "w_kv_up": jnp.concatenate([dw_k_p.reshape(MLA_KV_RANK, MLA_HEADS, 128)[:, :, :64],
                                    dw_v_pt.T.reshape(MLA_KV_RANK, MLA_HEADS, 128)[:, :, :64]], axis=2).reshape(
                                        MLA_KV_RANK, MLA_HEADS * 128),
        "w_mix_out": jnp.concatenate([dw_mix_y, dw_mix_o.reshape(MLA_HEADS, 128, D_MODEL)[:, :64].reshape(
            512, D_MODEL)], axis=0),
        "w_mem_q": dw_mem_q, "w_mem_k": dw_mem_k, "w_mem_v": dw_mem_v, "w_mem_o": dw_mem_o,
        "w_up": dw_up, "w_down": dw_down,
        "conv_w": dconv_w8[0:4],
    }
    small = {
        "ln_in_g": dg_in, "ln_in_b": db_in, "conv_b": dconv_b, "dt_bias": ddt_b[:, :8],
        "a_log": da_head[:, :8] * a_head.reshape(1, 8),
        "d_skip": ddexp.reshape(8, 64).sum(axis=1).reshape(1, 8),
        "ssd_norm_g": dg_ssd, "q_norm_g": dg_q, "kv_norm_g": dg_kv,
        "ln1_g": dg1, "ln1_b": db1, "ln2_g": dg2, "ln2_b": db2, "ln3_g": dg3, "ln3_b": db3,
    }
    return loss[0, 0], grad_x, big, small


BIG = {
    "w_in": (1024, 2216, 1), "w_q_up": (384, 768, 1), "w_kv_up": (256, 1024, 1), "w_mix_out": (1024, 1024, 0),
    "w_mem_q": (1024, 1024, 0), "w_mem_k": (1024, 1024, 0), "w_mem_v": (1024, 1024, 0), "w_mem_o": (1024, 1024, 0),
    "w_up": (1024, 4096, 1), "w_down": (4096, 1024, 0), "conv_w": (4, 1024, 1),
}
BIG_ORDER = list(BIG)
SMALL_ORDER = ["ln_in_g", "ln_in_b", "conv_b", "dt_bias", "a_log", "d_skip", "ssd_norm_g", "q_norm_g", "kv_norm_g",
               "ln1_g", "ln1_b", "ln2_g", "ln2_b", "ln3_g", "ln3_b"]
N_SHARD = 4
PACK_COLS = 1024
PACK_ROWS = 4032
HALF_ROWS = PACK_ROWS // 2
GATHER_CHUNKS = 3
CHIP_CHUNKS = 3
PAIR_CHUNKS = 4


def _shard_shape(name):
    r, c, ax = BIG[name]
    return (r // N_SHARD, c) if ax == 0 else (r, c // N_SHARD)


def _split_shards(name, full):
    r, c, ax = BIG[name]
    if ax == 0:
        return full.reshape(N_SHARD, -1)
    return full.reshape(r, N_SHARD, c // N_SHARD).transpose(1, 0, 2).reshape(N_SHARD, -1)


def _join_shards(name, parts):
    r, c, ax = BIG[name]
    if ax == 0:
        return parts.reshape(r, c)
    return parts.reshape(N_SHARD, r, c // N_SHARD).transpose(1, 0, 2).reshape(r, c)


HBM = pl.BlockSpec(memory_space=pl.ANY)


def _place():
    x, y, c = lax.axis_index("x"), lax.axis_index("y"), lax.axis_index("c")
    chips = [(1 - x, y), (x, 1 - y), (1 - x, 1 - y)]
    return x, y, c, chips


def _gather_weights(wp):
    R, C = wp.shape
    H = R // 2
    nq = GATHER_CHUNKS
    CH = H // nq

    def body(w_ref, out_ref, send_sems, recv_sems):
        x, y, c, chips = _place()
        sib = (x, y, 1 - c)

        def piece(k, hc, q):
            return out_ref.at[k, pl.ds(hc * H + q * CH, CH), :]

        def copy(j, src, dst, to):
            return pltpu.make_async_remote_copy(src_ref=src, dst_ref=dst, send_sem=send_sems.at[j],
                                                recv_sem=recv_sems.at[j], device_id=to, device_id_type=MESH)

        me = 2 * x + y
        sends = []
        for q in range(nq):
            for j, (px, py) in enumerate(chips):
                cp = copy(j * nq + q, w_ref.at[pl.ds(c * H + q * CH, CH), :], piece(me, c, q), (px, py, c))
                cp.start()
                sends.append(cp)
        fwds = []
        for q in range(nq):
            for j, (px, py) in enumerate(chips):
                k = 2 * px + py
                copy(j * nq + q, piece(k, c, q), piece(k, c, q), (px, py, c)).wait_recv()
                f = copy((3 + j) * nq + q, piece(k, c, q), piece(k, c, q), sib)
                f.start()
                fwds.append(f)
        for q in range(nq):
            for j, (px, py) in enumerate(chips):
                k = 2 * px + py
                copy((3 + j) * nq + q, piece(k, 1 - c, q), piece(k, 1 - c, q), sib).wait_recv()
        for cp in sends + fwds:
            cp.wait_send()

    out = pl.pallas_call(
        body, name="gather_weights", in_specs=[HBM], out_specs=HBM,
        out_shape=jax.ShapeDtypeStruct((N_SHARD, R, C), wp.dtype),
        scratch_shapes=[pltpu.SemaphoreType.DMA((6 * nq,)), pltpu.SemaphoreType.DMA((6 * nq,))],
    )(wp)
    me = 2 * lax.axis_index("x") + lax.axis_index("y")
    return lax.dynamic_update_slice(out, wp[None], (me, 0, 0))


def _pair_exchange(gp):
    n, R, C = gp.shape
    H = R // 2
    nq = PAIR_CHUNKS
    CH = H // nq

    def body(g_ref, theirs_ref, send_sems, recv_sems):
        x, y, c, _ = _place()
        swaps = []
        for k in range(n):
            for q in range(nq):
                cp = pltpu.make_async_remote_copy(
                    src_ref=g_ref.at[k, pl.ds((1 - c) * H + q * CH, CH), :], dst_ref=theirs_ref.at[k, pl.ds(q * CH, CH), :],
                    send_sem=send_sems.at[k * nq + q], recv_sem=recv_sems.at[k * nq + q], device_id=(x, y, 1 - c),
                    device_id_type=MESH)
                cp.start()
                swaps.append(cp)
        for cp in swaps:
            cp.wait()

    theirs = pl.pallas_call(
        body, name="pair_exchange", in_specs=[HBM], out_specs=HBM,
        out_shape=jax.ShapeDtypeStruct((n, H, C), gp.dtype),
        scratch_shapes=[pltpu.SemaphoreType.DMA((n * nq,)), pltpu.SemaphoreType.DMA((n * nq,))],
    )(gp)
    mine = lax.dynamic_slice(gp, (0, lax.axis_index("c") * H, 0), (n, H, C))
    return mine, theirs


def _chip_exchange(pb):
    n, H, C = pb.shape
    nq = CHIP_CHUNKS
    CH = H // nq

    def body(pb_ref, got_ref, send_sems, recv_sems):
        x, y, c, chips = _place()
        sends = []
        for q in range(nq):
            for j, (px, py) in enumerate(chips):
                cp = pltpu.make_async_remote_copy(
                    src_ref=pb_ref.at[2 * px + py, pl.ds(q * CH, CH), :], dst_ref=got_ref.at[j, pl.ds(q * CH, CH), :],
                    send_sem=send_sems.at[j * nq + q], recv_sem=recv_sems.at[j * nq + q],
                    device_id=(px, py, c), device_id_type=MESH)
                cp.start()
                sends.append(cp)
        for cp in sends:
            cp.wait()

    return pl.pallas_call(
        body, name="chip_exchange", in_specs=[HBM], out_specs=HBM,
        out_shape=jax.ShapeDtypeStruct((3, H, C), BF16),
        scratch_shapes=[pltpu.SemaphoreType.DMA((3 * nq,)), pltpu.SemaphoreType.DMA((3 * nq,))],
    )(pb)


def _pair_join(q):
    H, C = q.shape
    nq = PAIR_CHUNKS
    CH = H // nq

    def body(q_ref, theirs_ref, send_sems, recv_sems):
        x, y, c, _ = _place()
        pushes = []
        for j in range(nq):
            cp = pltpu.make_async_remote_copy(
                src_ref=q_ref.at[pl.ds(j * CH, CH), :], dst_ref=theirs_ref.at[pl.ds(j * CH, CH), :],
                send_sem=send_sems.at[j], recv_sem=recv_sems.at[j], device_id=(x, y, 1 - c), device_id_type=MESH)
            cp.start()
            pushes.append(cp)
        for cp in pushes:
            cp.wait()

    theirs = pl.pallas_call(
        body, name="pair_join", in_specs=[HBM], out_specs=HBM,
        out_shape=jax.ShapeDtypeStruct((H, C), F32),
        scratch_shapes=[pltpu.SemaphoreType.DMA((nq,)), pltpu.SemaphoreType.DMA((nq,))],
    )(q)
    c = lax.axis_index("c")
    out = jnp.zeros((2 * H, C), F32)
    out = lax.dynamic_update_slice(out, q, (c * H, 0))
    return lax.dynamic_update_slice(out, theirs, ((1 - c) * H, 0))


N_DEV = 8


def _small_all_reduce(g):
    r, cdim = g.shape

    def body(g_ref, out_ref, buf, send_sems, recv_sems):
        x, y, c, _ = _place()
        me = 4 * x + 2 * y + c
        buf[me] = g_ref[...]
        copies = []
        for d in range(1, N_DEV):
            to = me ^ d
            cp = pltpu.make_async_remote_copy(src_ref=g_ref, dst_ref=buf.at[me], send_sem=send_sems.at[d - 1],
                                              recv_sem=recv_sems.at[d - 1],
                                              device_id=(to // 4, (to // 2) % 2, to % 2), device_id_type=MESH)
            cp.start()
            copies.append(cp)
        for cp in copies:
            cp.wait()
        acc = buf[0]
        for d in range(1, N_DEV):
            acc = acc + buf[d]
        out_ref[...] = acc

    return pl.pallas_call(
        body, name="small_all_reduce",
        in_specs=[pl.BlockSpec(memory_space=pltpu.VMEM)], out_specs=pl.BlockSpec(memory_space=pltpu.VMEM),
        out_shape=jax.ShapeDtypeStruct((r, cdim), F32),
        scratch_shapes=[pltpu.VMEM((N_DEV, r, cdim), F32), pltpu.SemaphoreType.DMA((N_DEV - 1,)),
                        pltpu.SemaphoreType.DMA((N_DEV - 1,))],
    )(g)


def _adam(w, g, m, v, name):
    shape = w.shape
    w2, g2, m2, v2 = (t.reshape(-1, shape[-1]) for t in (w, g, m, v))
    d, mn, vn = _rowwise(_fn_adam, [w2, g2, m2, v2], [], [shape[-1]] * 3, tr=256, name=name)
    return d.reshape(shape), mn.reshape(shape), vn.reshape(shape)


def _old_kernel(x, mem, positions, ln_in_g, ln_in_b, w_in, conv_w, conv_b, dt_bias, a_log, d_skip, ssd_norm_g, q_norm_g, w_q_up, kv_norm_g, w_kv_up, w_mix_out, ln1_g, ln1_b, w_mem_q, w_mem_k, w_mem_v, w_mem_o, ln2_g, ln2_b, w_up, w_down, ln3_g, ln3_b, loss_target, m_ln_in_g, m_ln_in_b, m_w_in, m_conv_w, m_conv_b, m_dt_bias, m_a_log, m_d_skip, m_ssd_norm_g, m_q_norm_g, m_w_q_up, m_kv_norm_g, m_w_kv_up, m_w_mix_out, m_ln1_g, m_ln1_b, m_w_mem_q, m_w_mem_k, m_w_mem_v, m_w_mem_o, m_ln2_g, m_ln2_b, m_w_up, m_w_down, m_ln3_g, m_ln3_b, v_ln_in_g, v_ln_in_b, v_w_in, v_conv_w, v_conv_b, v_dt_bias, v_a_log, v_d_skip, v_ssd_norm_g, v_q_norm_g, v_w_q_up, v_kv_norm_g, v_w_kv_up, v_w_mix_out, v_ln1_g, v_ln1_b, v_w_mem_q, v_w_mem_k, v_w_mem_v, v_w_mem_o, v_ln2_g, v_ln2_b, v_w_up, v_w_down, v_ln3_g, v_ln3_b):
    args = dict(locals())
    weights = BIG_ORDER + SMALL_ORDER

    flat = []
    for n in BIG_ORDER:
        s = args[n].reshape(-1)
        if n == "conv_w":
            flat.append(lax.bitcast_convert_type(s.astype(F32), BF16).reshape(-1))
        else:
            flat.append(s.astype(BF16))
    flat = jnp.concatenate(flat)
    wp = jnp.pad(flat, (0, PACK_ROWS * PACK_COLS - flat.shape[0])).reshape(PACK_ROWS, PACK_COLS)
    gathered = _gather_weights(wp).reshape(N_SHARD, -1)
    W, off = {}, 0
    for n in BIG_ORDER:
        sr, sc = _shard_shape(n)
        cnt = sr * sc
        if n == "conv_w":
            part = lax.bitcast_convert_type(gathered[:, off:off + 2 * cnt].reshape(N_SHARD, cnt, 2), F32)
            off += 2 * cnt
        else:
            part = gathered[:, off:off + cnt]
            off += cnt
        W[n] = _join_shards(n, part)
    P = {n: args[n] for n in SMALL_ORDER}
    P["conv_w"] = W.pop("conv_w")

    loss, grad_x, gbig, gsmall = _local_step(x[0], mem[0], positions[0], loss_target[0], W, P)
    loss = lax.psum(loss, ("x", "y", "c"))

    gflat = jnp.concatenate([_split_shards(n, gbig[n]) for n in BIG_ORDER], axis=1)
    gp = jnp.pad(gflat, ((0, 0), (0, PACK_ROWS * PACK_COLS - gflat.shape[1]))).reshape(N_SHARD, PACK_ROWS, PACK_COLS)
    mine, theirs = _pair_exchange(gp)
    pf, pb = _rowwise(_fn_add2, [mine.reshape(-1, PACK_COLS), theirs.reshape(-1, PACK_COLS)], [],
                      [PACK_COLS, (PACK_COLS, BF16)], tr=288, name="pair_sum")
    pf = pf.reshape(N_SHARD, HALF_ROWS, PACK_COLS)
    pb = pb.reshape(N_SHARD, HALF_ROWS, PACK_COLS)
    got = _chip_exchange(pb).reshape(3 * HALF_ROWS, PACK_COLS)
    own = lax.dynamic_index_in_dim(pf, 2 * lax.axis_index("x") + lax.axis_index("y"), axis=0, keepdims=False)
    (q,) = _rowwise(_fn_add4, [own] + [(got, 0, PACK_COLS, j * HALF_ROWS) for j in range(3)], [], [PACK_COLS],
                    tr=288, name="chip_sum", n_rows=HALF_ROWS)
    red = _pair_join(q).reshape(-1)

    gs = jnp.concatenate([_row(gsmall[n], PACK_COLS) for n in SMALL_ORDER] + [jnp.zeros((1, PACK_COLS), F32)], axis=0)
    gs = _small_all_reduce(gs)

    grads, deltas, new_m, new_v = {}, {}, {}, {}
    off = 0
    for n in BIG_ORDER:
        sr, sc = _shard_shape(n)
        g = red[off:off + sr * sc].reshape(args[n].shape)
        off += sr * sc
        grads[n] = g
        deltas[n], new_m[n], new_v[n] = _adam(args[n], g, args["m_" + n], args["v_" + n], "adam_" + n)
    pack = lambda pre: jnp.concatenate([_row(args[pre + n], PACK_COLS) for n in SMALL_ORDER]
                                       + [jnp.zeros((1, PACK_COLS), F32)], axis=0)
    ds, ms, vs = _rowwise(_fn_adam, [pack(""), gs, pack("m_"), pack("v_")], [], [PACK_COLS] * 3, tr=16,
                          name="adam_small")
    for i, n in enumerate(SMALL_ORDER):
        cnt = args[n].size
        take = lambda t: t[i, :cnt].reshape(args[n].shape)
        grads[n], deltas[n], new_m[n], new_v[n] = take(gs), take(ds), take(ms), take(vs)

    order = ["ln_in_g", "ln_in_b", "w_in", "conv_w", "conv_b", "dt_bias", "a_log", "d_skip", "ssd_norm_g",
             "q_norm_g", "w_q_up", "kv_norm_g", "w_kv_up", "w_mix_out", "ln1_g", "ln1_b", "w_mem_q", "w_mem_k",
             "w_mem_v", "w_mem_o", "ln2_g", "ln2_b", "w_up", "w_down", "ln3_g", "ln3_b"]
    assert sorted(order) == sorted(weights)
    return (loss, grad_x[None], *[grads[n] for n in order], *[deltas[n] for n in order],
            *[new_m[n] for n in order], *[new_v[n] for n in order])


PACK_A_ROW = {"w_down": 0, "w_up": 1024, "w_mem_q": 2048, "w_mem_k": 2304, "w_mem_v": 2560, "w_mem_o": 2816,
              "w_mix_out": 3072}
PACK_A_ORDER = list(PACK_A_ROW)
PACK_A_ROWS = 3328
PACK_B_ORDER = ["w_in", "w_q_up", "w_kv_up", "conv_w"]
PACK_B_ROWS = 704


def _mesh_pos():
    return 2 * lax.axis_index("x") + lax.axis_index("y"), lax.axis_index("c")


def _local_step(x, mem, positions, target, WB, P, *, wp_a=None, g_a=None):
    S = x.shape[0]
    tr = ROW_TILE
    dist = g_a is None
    w_in_p = _pad_w_in(WB["w_in"])
    w_q_p = _pad_heads(WB["w_q_up"], MLA_QK)
    w_kv3 = WB["w_kv_up"].reshape(MLA_KV_RANK, MLA_HEADS, 128)
    w_k_p = _pad_heads(w_kv3[:, :, :64].reshape(MLA_KV_RANK, 512), 64)
    w_v_p = _pad_heads(w_kv3[:, :, 64:].reshape(MLA_KV_RANK, 512), 64)
    w_v_pt = w_v_p.T
    conv_w8 = jnp.pad(P["conv_w"].astype(F32), ((0, 4), (0, 0)))
    conv_b = _row(P["conv_b"])
    dt_b = _row(P["dt_bias"], 128)
    a_head = -jnp.exp(P["a_log"].reshape(-1).astype(F32))
    a_row = _row(a_head, 128)
    dexp = jnp.repeat(P["d_skip"].reshape(-1).astype(F32), 64).reshape(1, 512)
    g_ssd, g_q, g_kv = _row(P["ssd_norm_g"]), _row(P["q_norm_g"]), _row(P["kv_norm_g"])
    g_in, b_in = _row(P["ln_in_g"]), _row(P["ln_in_b"])
    g1, b1, g2, b2, g3, b3 = (_row(P[k]) for k in ("ln1_g", "ln1_b", "ln2_g", "ln2_b", "ln3_g", "ln3_b"))

    half = MLA_ROPE // 2
    inv_freq = jnp.power(ROPE_THETA, -jnp.arange(half, dtype=F32) / half)
    ang = positions.reshape(S, 1).astype(F32) * inv_freq
    cos, sin = jnp.cos(ang), jnp.sin(ang)
    zc = lambda n: jnp.zeros((S, n), F32)
    rope_a = jnp.concatenate([jnp.ones((S, 64), F32), cos, cos, zc(32)], axis=1)
    rope_b = jnp.concatenate([zc(80), sin, zc(32)], axis=1)
    rope_c = jnp.concatenate([zc(64), -sin, zc(48)], axis=1)

    (h0,) = _rowwise(_fn_ln, [x], [g_in, b_in], [D_MODEL], tr=tr, name="ln_in")
    proj = _mm(h0, w_in_p, form="nn", name="mm_in")
    conv_y, xbc, dt = _rowwise(
        _fn_conv_fwd, [(proj,) + SEG_XBC, ("prev", proj) + SEG_XBC, (proj,) + SEG_DT], [conv_w8, conv_b, dt_b],
        [1024, 1024, 128], tr=tr, name="conv_fwd")
    y_ssd, hs = _ssd_fwd(xbc, dt, a_row, name="ssd_fwd")
    (y_n,) = _rowwise(_fn_ssd_post, [y_ssd, (xbc, 0, 512), (proj,) + SEG_Z], [dexp, g_ssd], [512], tr=tr,
                      name="ssd_post")
    q_n, kv_n = _rowwise(_fn_mla_pre, [(proj,) + SEG_QLAT, (proj,) + SEG_KVLAT], [g_q, g_kv], [384, 256], tr=tr,
                         name="mla_pre")
    qp = _mm(q_n, w_q_p, form="nn", name="mm_q_up")
    kn = _mm(kv_n, w_k_p, form="nn", name="mm_k_up")
    v_nat = _mm(kv_n, w_v_p, form="nn", out_dtype=BF16, name="mm_v_up")
    v_t = _mm(w_v_pt, kv_n, form="nt", out_dtype=BF16, name="mm_v_up_t")
    q_rot, k_full = _rowwise(_fn_rope, [qp, kn, (proj,) + SEG_KR, rope_a, rope_b, rope_c], [],
                             [(1024, BF16), (1024, BF16)], tr=tr, name="rope")
    res = _attn_fwd(q_rot, k_full, v_t, name="attn_fwd", hosted=_gather_step(wp_a) if dist else None)
    o_t, lse = res[0], res[1]
    if dist:
        g_a = lax.dynamic_update_slice(res[2], wp_a[None], (_mesh_pos()[0], 0, 0))
    r_mix = PACK_A_ROW["w_mix_out"]
    w_mix_o = jnp.pad(g_a[2:4, r_mix:r_mix + 256].reshape(MLA_HEADS, 64, D_MODEL),
                      ((0, 0), (0, 64), (0, 0))).reshape(MLA_HEADS * 128, D_MODEL)
    mix_o = _mm(o_t, w_mix_o, form="tn", name="mm_mix_o")
    mix_y = _mm(y_n, g_a, form="nn", b_pack="w_mix_out", name="mm_mix_y")
    (h1,) = _rowwise(_fn_res2_ln, [h0, mix_o, mix_y], [g1, b1], [D_MODEL], tr=tr, name="ln1")
    qm = _mm(h1, g_a, form="nn", b_pack="w_mem_q", name="mm_mem_q")
    km = _mm(mem, g_a, form="nn", b_pack="w_mem_k", name="mm_mem_k")
    vm = _mm(mem, g_a, form="nn", b_pack="w_mem_v", name="mm_mem_v")
    (om,) = _rowwise(_fn_mem_fwd, [qm], [km, vm], [D_MODEL], tr=tr, name="mem_fwd")
    xa = _mm(om, g_a, form="nn", b_pack="w_mem_o", name="mm_mem_o")
    (h2,) = _rowwise(_fn_res_ln, [h1, xa], [g2, b2], [D_MODEL], tr=tr, name="ln2")
    u = _mm(h2, g_a, form="nn", b_pack="w_up", name="mm_up")
    ff = _mm(u, g_a, form="nn", a_pro=_relu2, b_pack="w_down", name="mm_down")

    gp = lax.empty((N_SHARD, PACK_A_ROWS, PACK_COLS), F32)
    dt3, dg3, db3, loss = _rowwise(_fn_final, [h2, ff, target], [g3, b3], [D_MODEL],
                                   [(1, D_MODEL), (1, D_MODEL), (1, 128)], tr=tr, name="ln3_loss")
    da = _mm(dt3, g_a, form="nt", b_pack="w_down", name="mm_down_dx")
    gp = _mm(u, dt3, form="tn", a_pro=_relu2, out_pack=("w_down", gp), name="mm_down_dw")
    (du,) = _rowwise(_fn_du, [u, da], [], [4 * D_MODEL], tr=128, name="mlp_du")
    gp = _mm(h2, du, form="tn", out_pack=("w_up", gp), name="mm_up_dw")
    dh2 = _mm(du, g_a, form="nt", b_pack="w_up", name="mm_up_dx")
    dt2, dg2, db2 = _rowwise(_fn_res_ln_bwd, [h1, xa, dt3, dh2], [g2], [D_MODEL], [(1, D_MODEL)] * 2, tr=tr,
                             name="ln2_bwd")
    dom = _mm(dt2, g_a, form="nt", b_pack="w_mem_o", name="mm_mem_o_dx")
    gp = _mm(om, dt2, form="tn", out_pack=("w_mem_o", gp), name="mm_mem_o_dw")
    dqm, dkm, dvm = _rowwise(_fn_mem_bwd, [qm, dom], [km, vm], [D_MODEL], [(256, D_MODEL)] * 2, tr=tr,
                             name="mem_bwd")
    gp = _mm(h1, dqm, form="tn", out_pack=("w_mem_q", gp), name="mm_mem_q_dw")
    gp = _mm(mem, dkm, form="tn", out_pack=("w_mem_k", gp), name="mm_mem_k_dw")
    gp = _mm(mem, dvm, form="tn", out_pack=("w_mem_v", gp), name="mm_mem_v_dw")
    dh1 = _mm(dqm, g_a, form="nt", b_pack="w_mem_q", name="mm_mem_q_dx")
    dt1, dg1, db1 = _rowwise(_fn_res2_ln_bwd, [h0, mix_o, mix_y, dt2, dh1], [g1], [D_MODEL], [(1, D_MODEL)] * 2,
                             tr=tr, name="ln1_bwd")
    do_t = _mm(w_mix_o, dt1, form="nt", name="mm_mix_o_dx")
    dy_n = _mm(dt1, g_a, form="nt", b_pack="w_mix_out", b_rows=512, name="mm_mix_y_dx")
    dw_mix_o = _mm(o_t, dt1, form="nn", name="mm_mix_o_dw")
    gp = _mm(y_n, dt1, form="tn", out_pack=("w_mix_out", gp), name="mm_mix_y_dw")
    gp = lax.dynamic_update_slice(
        gp, dw_mix_o.reshape(MLA_HEADS, 128, D_MODEL)[:, :64].reshape(2, 256, D_MODEL), (2, r_mix, 0))
    me, c = _mesh_pos() if dist else (0, 0)
    ha = PACK_A_ROWS // 2
    res = _attn_bwd_dq(q_rot, k_full, k_full.T, v_nat, o_t, do_t, lse, name="attn_bwd_dq",
                       hosted=_pair_exchange_step(gp) if dist else None)
    dq_t, delta = res[0], res[1]
    chip_step = None
    if dist:
        mine = lax.dynamic_slice(gp, (0, c * ha, 0), (N_SHARD, ha, PACK_COLS))
        pf, pb = _rowwise(_fn_add2, [mine.reshape(-1, PACK_COLS), res[2].reshape(-1, PACK_COLS)], [],
                          [PACK_COLS, (PACK_COLS, BF16)], tr=512, name="pair_sum_a")
        chip_step = _chip_exchange_step(pb.reshape(N_SHARD, ha, PACK_COLS))
    res = _attn_bwd_dkv(q_rot, q_rot.T, k_full, v_nat, do_t, lse, delta, name="attn_bwd_dkv", hosted=chip_step)
    dk_t, dv_t = res[0], res[1]
    if dist:
        own = lax.dynamic_index_in_dim(pf.reshape(N_SHARD, ha, PACK_COLS), me, axis=0, keepdims=False)
        got = res[2].reshape(3 * ha, PACK_COLS)
        (gp,) = _rowwise(_fn_add4, [own] + [(got, 0, PACK_COLS, j * ha) for j in range(3)], [], [PACK_COLS],
                         tr=512, name="chip_sum_a", n_rows=ha)
    dk = dk_t.T
    dqp, dkr = _rowwise(_fn_rope_bwd, [dq_t.T, dk, rope_a, rope_b, rope_c], [], [1024, 128], tr=tr,
                        name="rope_bwd")
    dw_q_p = _mm(q_n, dqp, form="tn", name="mm_q_up_dw")
    dq_n = _mm(dqp, w_q_p, form="nt", name="mm_q_up_dx")
    dw_k_p = _mm(kv_n, dk, form="tn", name="mm_k_up_dw")
    dkv_n1 = _mm(dk, w_k_p, form="nt", name="mm_k_up_dx")
    dw_v_pt = _mm(dv_t, kv_n, form="nn", name="mm_v_up_dw")
    dkv_n2 = _mm(dv_t, w_v_pt, form="tn", name="mm_v_up_dx")
    dq_lat, dkv_lat, dg_q, dg_kv = _rowwise(
        _fn_mla_pre_bwd, [(proj,) + SEG_QLAT, (proj,) + SEG_KVLAT, dq_n, dkv_n1, dkv_n2], [g_q, g_kv], [384, 256],
        [(1, 384), (1, 256)], tr=tr, name="mla_pre_bwd")
    dy_ssd, dz, dxs_skip, ddexp, dg_ssd = _rowwise(
        _fn_ssd_post_bwd, [dy_n, y_ssd, (xbc, 0, 512), (proj,) + SEG_Z], [dexp, g_ssd],
        [512, 512, 512], [(1, 512)] * 2, tr=tr, name="ssd_post_bwd")
    dxs, dbc, ddt, da_head = _ssd_bwd(xbc, dt, a_row, hs, dy_ssd, name="ssd_bwd")
    dyc, ddtr, dconv_b, ddt_b = _rowwise(
        _fn_conv_bwd_a, [conv_y, dxs, dxs_skip, dbc, (proj,) + SEG_DT, ddt], [dt_b], [1024, 128],
        [(1, 1024), (1, 128)], tr=tr, name="conv_bwd_a")
    dxbc, dconv_w8 = _rowwise(
        _fn_conv_bwd_b, [dyc, ("next", dyc, 0, 1024), (proj,) + SEG_XBC, ("prev", proj) + SEG_XBC], [conv_w8], [1024],
        [(8, 1024)], tr=tr, name="conv_bwd_b")
    dproj = jnp.concatenate([dxbc, dz, dq_lat, ddtr, dkv_lat, dkr, jnp.zeros((S, 128), F32)], axis=1)
    dw_in_p = _mm(h0, dproj, form="tn", name="mm_in_dw")
    dh0 = _mm(dproj, w_in_p, form="nt", name="mm_in_dx")
    grad_x, dg_in, db_in = _rowwise(_fn_in_ln_bwd, [x, dt1, dh0], [g_in], [D_MODEL], [(1, D_MODEL)] * 2, tr=tr,
                                    name="ln_in_bwd")

    big_b = {
        "w_in": _unpad_w_in(dw_in_p),
        "w_q_up": dw_q_p.reshape(384, MLA_HEADS, 128)[:, :, :MLA_QK].reshape(384, MLA_HEADS * MLA_QK),
        "w_kv_up": jnp.concatenate([dw_k_p.reshape(MLA_KV_RANK, MLA_HEADS, 128)[:, :, :64],
                                    dw_v_pt.T.reshape(MLA_KV_RANK, MLA_HEADS, 128)[:, :, :64]], axis=2).reshape(
                                        MLA_KV_RANK, MLA_HEADS * 128),
        "conv_w": dconv_w8[0:4],
    }
    small = {
        "ln_in_g": dg_in, "ln_in_b": db_in, "conv_b": dconv_b, "dt_bias": ddt_b[:, :8],
        "a_log": da_head[:, :8] * a_head.reshape(1, 8),
        "d_skip": ddexp.reshape(8, 64).sum(axis=1).reshape(1, 8),
        "ssd_norm_g": dg_ssd, "q_norm_g": dg_q, "kv_norm_g": dg_kv,
        "ln1_g": dg1, "ln1_b": db1, "ln2_g": dg2, "ln2_b": db2, "ln3_g": dg3, "ln3_b": db3,
    }
    return loss[0, 0], grad_x, gp, big_b, small


def _reduce_scatter(gp, tag):
    n, R, C = gp.shape
    H = R // 2
    me, c = _mesh_pos()
    (theirs,) = _run_step(_pair_exchange_step(gp), "pair_exchange_" + tag)
    mine = lax.dynamic_slice(gp, (0, c * H, 0), (n, H, C))
    pf, pb = _rowwise(_fn_add2, [mine.reshape(-1, C), theirs.reshape(-1, C)], [], [C, (C, BF16)], tr=512,
                      name="pair_sum_" + tag)
    (got,) = _run_step(_chip_exchange_step(pb.reshape(n, H, C)), "chip_exchange_" + tag)
    own = lax.dynamic_index_in_dim(pf.reshape(n, H, C), me, axis=0, keepdims=False)
    got = got.reshape(3 * H, C)
    (q,) = _rowwise(_fn_add4, [own] + [(got, 0, C, j * H) for j in range(3)], [], [C], tr=512,
                    name="chip_sum_" + tag, n_rows=H)
    return q


def _adam(w, g, m, v, name):
    shape = w.shape
    w2, m2, v2 = (t.reshape(-1, shape[-1]) for t in (w, m, v))
    g2 = (g[0], 0, shape[-1], g[1]) if isinstance(g, tuple) else g.reshape(-1, shape[-1])
    d, mn, vn = _rowwise(_fn_adam, [w2, g2, m2, v2], [], [shape[-1]] * 3, tr=256, name=name)
    return d.reshape(shape), mn.reshape(shape), vn.reshape(shape)


def kernel(x, mem, positions, ln_in_g, ln_in_b, w_in, conv_w, conv_b, dt_bias, a_log, d_skip, ssd_norm_g, q_norm_g, w_q_up, kv_norm_g, w_kv_up, w_mix_out, ln1_g, ln1_b, w_mem_q, w_mem_k, w_mem_v, w_mem_o, ln2_g, ln2_b, w_up, w_down, ln3_g, ln3_b, loss_target, m_ln_in_g, m_ln_in_b, m_w_in, m_conv_w, m_conv_b, m_dt_bias, m_a_log, m_d_skip, m_ssd_norm_g, m_q_norm_g, m_w_q_up, m_kv_norm_g, m_w_kv_up, m_w_mix_out, m_ln1_g, m_ln1_b, m_w_mem_q, m_w_mem_k, m_w_mem_v, m_w_mem_o, m_ln2_g, m_ln2_b, m_w_up, m_w_down, m_ln3_g, m_ln3_b, v_ln_in_g, v_ln_in_b, v_w_in, v_conv_w, v_conv_b, v_dt_bias, v_a_log, v_d_skip, v_ssd_norm_g, v_q_norm_g, v_w_q_up, v_kv_norm_g, v_w_kv_up, v_w_mix_out, v_ln1_g, v_ln1_b, v_w_mem_q, v_w_mem_k, v_w_mem_v, v_w_mem_o, v_ln2_g, v_ln2_b, v_w_up, v_w_down, v_ln3_g, v_ln3_b):
    args = dict(locals())
    me, c = _mesh_pos()

    wp_a = jnp.concatenate([args[n].reshape(-1, PACK_COLS).astype(BF16) for n in PACK_A_ORDER], axis=0)
    flat = [args[n].reshape(-1).astype(BF16) for n in PACK_B_ORDER[:-1]]
    flat.append(lax.bitcast_convert_type(conv_w.reshape(-1), BF16).reshape(-1))
    used = sum(f.shape[0] for f in flat)
    flat.append(jnp.zeros((PACK_B_ROWS * PACK_COLS - used,), BF16))
    wp_b = jnp.concatenate(flat).reshape(PACK_B_ROWS, PACK_COLS)

    (g_b,) = _run_step(_gather_step(wp_b), "gather_b")
    g_b = lax.dynamic_update_slice(g_b, wp_b[None], (me, 0, 0)).reshape(N_SHARD, -1)
    WB, off = {}, 0
    for n in PACK_B_ORDER:
        sr, sc = _shard_shape(n)
        cnt = sr * sc
        if n == "conv_w":
            part = lax.bitcast_convert_type(g_b[:, off:off + 2 * cnt].reshape(N_SHARD, cnt, 2), F32)
            off += 2 * cnt
        else:
            part = g_b[:, off:off + cnt]
            off += cnt
        WB[n] = _join_shards(n, part)
    P = {n: args[n] for n in SMALL_ORDER}
    P["conv_w"] = WB.pop("conv_w")

    loss, grad_x, q_a, gbig_b, gsmall = _local_step(x[0], mem[0], positions[0], loss_target[0], WB, P, wp_a=wp_a)
    loss = lax.psum(loss, ("x", "y", "c"))

    gflat = [_split_shards(n, gbig_b[n]) for n in PACK_B_ORDER]
    used = sum(f.shape[1] for f in gflat)
    gflat.append(jnp.zeros((N_SHARD, PACK_B_ROWS * PACK_COLS - used), F32))
    gp_b = jnp.concatenate(gflat, axis=1).reshape(N_SHARD, PACK_B_ROWS, PACK_COLS)
    q_b = _reduce_scatter(gp_b, "b")
    (t_a,) = _run_step(_pair_join_step(q_a), "pair_join_a")
    (t_b,) = _run_step(_pair_join_step(q_b), "pair_join_b")
    red = jnp.where(c == 0, jnp.concatenate([q_a, t_a, q_b, t_b], axis=0), jnp.concatenate([t_a, q_a, t_b, q_b], axis=0))
    gs = jnp.concatenate([_row(gsmall[n], PACK_COLS) for n in SMALL_ORDER] + [jnp.zeros((1, PACK_COLS), F32)], axis=0)
    gs = _small_all_reduce(gs)

    grads, deltas, new_m, new_v = {}, {}, {}, {}
    for n in PACK_A_ORDER:
        r0, (sr, _) = PACK_A_ROW[n], _shard_shape(n)
        grads[n] = red[r0:r0 + sr].reshape(args[n].shape)
        deltas[n], new_m[n], new_v[n] = _adam(args[n], (red, r0), args["m_" + n], args["v_" + n], "adam_" + n)
    red_b = red[PACK_A_ROWS:].reshape(-1)
    off = 0
    for n in PACK_B_ORDER:
        sr, sc = _shard_shape(n)
        grads[n] = red_b[off:off + sr * sc].reshape(args[n].shape)
        off += sr * sc
        deltas[n], new_m[n], new_v[n] = _adam(args[n], grads[n], args["m_" + n], args["v_" + n], "adam_" + n)
    pack = lambda pre: jnp.concatenate([_row(args[pre + n], PACK_COLS) for n in SMALL_ORDER]
                                       + [jnp.zeros((1, PACK_COLS), F32)], axis=0)
    ds, ms, vs = _rowwise(_fn_adam, [pack(""), gs, pack("m_"), pack("v_")], [], [PACK_COLS] * 3, tr=16,
                          name="adam_small")
    for i, n in enumerate(SMALL_ORDER):
        cnt = args[n].size
        take = lambda t: t[i, :cnt].reshape(args[n].shape)
        grads[n], deltas[n], new_m[n], new_v[n] = take(gs), take(ds), take(ms), take(vs)

    order = ["ln_in_g", "ln_in_b", "w_in", "conv_w", "conv_b", "dt_bias", "a_log", "d_skip", "ssd_norm_g",
             "q_norm_g", "w_q_up", "kv_norm_g", "w_kv_up", "w_mix_out", "ln1_g", "ln1_b", "w_mem_q", "w_mem_k",
             "w_mem_v", "w_mem_o", "ln2_g", "ln2_b", "w_up", "w_down", "ln3_g", "ln3_b"]
    return (loss, grad_x[None], *[grads[n] for n in order], *[deltas[n] for n in order],
            *[new_m[n] for n in order], *[new_v[n] for n in order])
```

```python
import functools
import math

import jax
import jax.numpy as jnp
import numpy as np
from jax import lax
from jax.experimental import pallas as pl
from jax.experimental.pallas import tpu as pltpu

F32 = jnp.float32
BF16 = jnp.bfloat16
MESH = pl.DeviceIdType.MESH

D_MODEL = 1024
SSD_HEADS = 8
SSD_INNER = 512
SSD_CHUNK = 128
SSD_STATE = 128
MLA_HEADS = 8
MLA_NOPE = 64
MLA_ROPE = 32
MLA_QK = 96
MLA_Q_RANK = 384
MLA_KV_RANK = 256
ROPE_THETA = 10000.0
MEM_HEADS = 4
MEM_HEAD_DIM = 256
LN_EPS = 1e-5
RMS_EPS = 1e-6
ALPHA = 2.0 ** 0.25
ADAM_LR = 0.001
ADAM_B1 = 0.9
ADAM_B2 = 0.999
ADAM_EPS = 1e-08
ADAM_WD = 0.01
ADAM_STEP = 10

LANES = 128
IN_W = 2560
SEG_XBC = (0, 1024)
SEG_Z = (1024, 512)
SEG_QLAT = (1536, 384)
SEG_DT = (1920, 128)
SEG_KVLAT = (2048, 256)
SEG_KR = (2304, 128)
VMEM_LIMIT = 56 * 1024 * 1024
ATTN_TILE = 512
ROW_TILE = 256
NEG = -1e30

NN = (((1,), (0,)), ((), ()))
NT = (((1,), (1,)), ((), ()))
TN = (((0,), (0,)), ((), ()))


def _dot(a, b, dims=NN):
    return lax.dot_general(a.astype(BF16), b.astype(BF16), dims, preferred_element_type=F32)


def _dot_exact(a, b):
    return lax.dot_general(a, b, NN, precision=lax.Precision.HIGHEST, preferred_element_type=F32)


def _pick(dim, pref):
    t = min(pref, dim)
    t -= t % LANES
    while t >= LANES:
        if dim % t == 0:
            return t
        t -= LANES
    return dim


def _params(sem):
    return pltpu.CompilerParams(dimension_semantics=sem, vmem_limit_bytes=VMEM_LIMIT)


def _pack_caps(wname):
    r, c, ax = BIG[wname]
    if ax == 0:
        return (r if r <= 1024 else r // N_SHARD), c
    return r, c // N_SHARD


def _pack_block(wname, br, bc):
    r, c, ax = BIG[wname]
    r0 = PACK_A_ROW[wname]
    sr = r // N_SHARD if ax == 0 else r
    if ax == 0 and br > sr:
        assert br % sr == 0 and r0 % sr == 0
        return (br // sr, sr, bc), lambda rb, cb: (rb, r0 // sr, cb)
    assert r0 % br == 0
    if ax == 0:
        per = sr // br
        return (1, br, bc), lambda rb, cb: (rb // per, r0 // br + rb % per, cb)
    per = (c // N_SHARD) // bc
    return (1, br, bc), lambda rb, cb: (cb // per, r0 // br + rb, cb % per)


def _mm(a, b, *, form, name, a_pro=None, epi=None, out_dtype=F32, tm=1024, tn=1024, tk=1024, b_pack=None,
        b_rows=None, out_pack=None):
    b_shape = BIG[b_pack][:2] if b_pack else b.shape
    if b_pack and form == "nt":
        b_shape = (b_rows or b_shape[0], b_shape[1])
    if form == "nn":
        (m, k), (_, n) = a.shape, b_shape
    elif form == "nt":
        (m, k), (n, _) = a.shape, b_shape
    else:
        (k, m), (_, n) = a.shape, b_shape
    if b_pack:
        rcap, ccap = _pack_caps(b_pack)
        tk, tn = (min(tk, rcap), min(tn, ccap)) if form == "nn" else (min(tk, ccap), min(tn, rcap))
    if out_pack:
        rcap, ccap = _pack_caps(out_pack[0])
        tm, tn = min(tm, rcap), min(tn, ccap)
    tm, tn, tk = _pick(m, tm), _pick(n, tn), _pick(k, tk)
    dims = {"nn": NN, "nt": NT, "tn": TN}[form]
    nk = k // tk
    direct = out_dtype == F32 and epi is None
    n_extra = (1 if epi else 0) + (1 if out_pack else 0)

    def body(a_ref, b_ref, *rest):
        o_ref = rest[n_extra]
        acc_ref = o_ref if direct else rest[-1]

        @pl.when(pl.program_id(2) == 0)
        def _():
            acc_ref[...] = jnp.zeros_like(acc_ref)

        av = a_ref[...]
        if a_pro is not None:
            av = a_pro(av)
        bv = b_ref[...]
        acc_ref[...] += _dot(av, bv.reshape(-1, bv.shape[-1]), dims).reshape(acc_ref.shape)
        if not direct:
            @pl.when(pl.program_id(2) == nk - 1)
            def _():
                val = acc_ref[...]
                if epi is not None:
                    val = epi[0](val, rest[0][...])
                o_ref[...] = val.reshape(o_ref.shape).astype(o_ref.dtype)

    if form == "tn":
        a_spec = pl.BlockSpec((tk, tm), lambda i, j, kk: (kk, i))
    else:
        a_spec = pl.BlockSpec((tm, tk), lambda i, j, kk: (i, kk))
    if b_pack:
        shape, idx = _pack_block(b_pack, *((tk, tn) if form == "nn" else (tn, tk)))
        b_spec = pl.BlockSpec(shape, (lambda i, j, kk: idx(kk, j)) if form == "nn" else (lambda i, j, kk: idx(j, kk)))
    elif form == "nt":
        b_spec = pl.BlockSpec((tn, tk), lambda i, j, kk: (j, kk))
    else:
        b_spec = pl.BlockSpec((tk, tn), lambda i, j, kk: (kk, j))
    in_specs, args = [a_spec, b_spec], [a, b]
    out_spec = pl.BlockSpec((tm, tn), lambda i, j, kk: (i, j))
    out_sds, aliases = jax.ShapeDtypeStruct((m, n), out_dtype), {}
    if epi is not None:
        in_specs.append(out_spec)
        args.append(epi[1])
    if out_pack:
        wname, buf = out_pack
        shape, idx = _pack_block(wname, tm, tn)
        out_spec = pl.BlockSpec(shape, lambda i, j, kk: idx(i, j))
        out_sds, aliases = jax.ShapeDtypeStruct(buf.shape, buf.dtype), {len(args): 0}
        in_specs.append(HBM)
        args.append(buf)
    acc_shape = out_spec.block_shape if out_pack else (tm, tn)
    return pl.pallas_call(
        body, name=name, grid=(m // tm, n // tn, nk), in_specs=in_specs, out_specs=out_spec, out_shape=out_sds,
        input_output_aliases=aliases, scratch_shapes=[] if direct else [pltpu.VMEM(acc_shape, F32)],
        compiler_params=_params(("parallel", "parallel", "arbitrary")),
    )(*args)


class _Ctx:
    def __init__(self, i, n):
        self.i, self.n = i, n


def _rowwise(fn, rows, consts, row_outs, acc_outs=(), *, tr, name, n_rows=None):
    norm = []
    for r in rows:
        kind = "tile"
        if isinstance(r, tuple) and isinstance(r[0], str):
            kind, r = r[0], r[1:]
        row0 = 0
        if isinstance(r, tuple) and len(r) == 4:
            r, row0 = r[:3], r[3]
        arr, col0, width = r if isinstance(r, tuple) else (r, 0, r.shape[1])
        assert col0 % width == 0
        norm.append((kind, arr, col0 // width, width, row0))
    n_rows = n_rows or next(a.shape[0] for k, a, _, _, _ in norm if k == "tile")
    tr = min(tr, n_rows)
    while n_rows % tr:
        tr -= 8
    n = n_rows // tr
    arrs, specs = [], []
    for kind, arr, cb, width, row0 in norm:
        if kind == "tile":
            assert row0 % tr == 0
            specs.append(pl.BlockSpec((tr, width), lambda i, cb=cb, rb=row0 // tr: (i + rb, cb)))
        elif kind == "prev":
            specs.append(pl.BlockSpec((8, width), lambda i, cb=cb: (jnp.maximum(i * (tr // 8) - 1, 0), cb)))
        else:
            specs.append(pl.BlockSpec((8, width), lambda i, cb=cb: (jnp.minimum((i + 1) * (tr // 8), n_rows // 8 - 1), cb)))
        arrs.append(arr)
    for c in consts:
        specs.append(pl.BlockSpec(c.shape, lambda i, nd=c.ndim: (0,) * nd))
        arrs.append(c)
    n_in, n_ro = len(arrs), len(row_outs)
    row_outs = [w if isinstance(w, tuple) else (w, F32) for w in row_outs]
    out_shape = [jax.ShapeDtypeStruct((n_rows, w), dt) for w, dt in row_outs]
    out_specs = [pl.BlockSpec((tr, w), lambda i: (i, 0)) for w, _ in row_outs]
    out_shape += [jax.ShapeDtypeStruct(s, F32) for s in acc_outs]
    out_specs += [pl.BlockSpec(s, lambda i: (0, 0)) for s in acc_outs]

    def body(*refs):
        i = pl.program_id(0)
        vals = [r[...] for r in refs[:n_in]]
        outs = fn(_Ctx(i, n), *vals)
        if not isinstance(outs, (tuple, list)):
            outs = (outs,)
        o_refs = refs[n_in:]
        for o_ref, o in zip(o_refs[:n_ro], outs[:n_ro]):
            o_ref[...] = o.astype(o_ref.dtype)
        if acc_outs:
            @pl.when(i == 0)
            def _():
                for o_ref in o_refs[n_ro:]:
                    o_ref[...] = jnp.zeros_like(o_ref)

            for o_ref, o in zip(o_refs[n_ro:], outs[n_ro:]):
                o_ref[...] += jnp.broadcast_to(o, o_ref.shape)

    res = pl.pallas_call(
        body, name=name, grid=(n,), in_specs=specs, out_specs=out_specs, out_shape=out_shape,
        compiler_params=_params(("arbitrary",)),
    )(*arrs)
    return res


def _sum0(v):
    return jnp.sum(v, axis=0, keepdims=True)


def _mean1(v):
    return jnp.mean(v, axis=-1, keepdims=True)


def _sigmoid(v):
    return 1.0 / (1.0 + jnp.exp(-v))


def _ln_stats(t):
    xc = t - _mean1(t)
    rstd = lax.rsqrt(_mean1(xc * xc) + LN_EPS)
    return xc * rstd, rstd


def _ln_bwd(xhat, rstd, dy, g):
    dxh = dy * g
    dx = rstd * (dxh - _mean1(dxh) - xhat * _mean1(dxh * xhat))
    return dx, _sum0(dy * xhat), _sum0(dy)


def _rms_fwd(v, g):
    return v * lax.rsqrt(_mean1(v * v) + RMS_EPS) * g


def _rms_bwd(v, dy, g):
    rs = lax.rsqrt(_mean1(v * v) + RMS_EPS)
    vh = v * rs
    dyg = dy * g
    return rs * (dyg - vh * _mean1(dyg * vh)), _sum0(dy * vh)


def _lane(shape):
    return lax.broadcasted_iota(jnp.int32, shape, len(shape) - 1)


def _shift_down(u, halo, s, is_first):
    tr = u.shape[0]
    rolled = pltpu.roll(u, s, 0)
    hr = jnp.where(is_first, 0.0, pltpu.roll(halo, s, 0))
    row = lax.broadcasted_iota(jnp.int32, hr.shape, 0)
    top = jnp.where(row < s, hr, rolled[0:8])
    if tr == 8:
        return top
    return jnp.concatenate([top, rolled[8:]], axis=0)


def _shift_up(d, halo, s, is_last):
    tr = d.shape[0]
    rolled = pltpu.roll(d, tr - s, 0)
    hr = jnp.where(is_last, 0.0, pltpu.roll(halo, 8 - s, 0))
    row = lax.broadcasted_iota(jnp.int32, hr.shape, 0)
    bot = jnp.where(row >= 8 - s, hr, rolled[tr - 8:])
    if tr == 8:
        return bot
    return jnp.concatenate([rolled[:tr - 8], bot], axis=0)


def _rope(v, ta, tb, tc):
    return v * ta + pltpu.roll(v, 16, 1) * tb + pltpu.roll(v, LANES - 16, 1) * tc


def _rope_bwd(d, ta, tb, tc):
    return d * ta + pltpu.roll(d * tb, LANES - 16, 1) + pltpu.roll(d * tc, 16, 1)


def _ssd_common(dtv, a_row):
    L = SSD_CHUNK
    a = dtv * a_row
    r = lax.broadcasted_iota(jnp.int32, (L, L), 0)
    c = lax.broadcasted_iota(jnp.int32, (L, L), 1)
    tril = r >= c
    cs = _dot_exact(tril.astype(F32), a)
    cs_t = cs.T
    cs_last = cs[L - 1:L, :]
    return dict(a=a, tril=tril, cs=cs, cs_t=cs_t, ecs=jnp.exp(cs), dte=jnp.exp(cs_last - cs),
                elast=jnp.exp(cs_last))


def _pair_sel(v, h0, lo):
    return jnp.where(lo, v[:, h0:h0 + 1], v[:, h0 + 1:h0 + 2])


def _ssd_pair(cm, h0, cb, xp, dtv, bmat, cmat, hp, lo):
    L = SSD_CHUNK
    x = xp * _pair_sel(dtv, h0, lo)
    lam0 = jnp.exp(jnp.where(cm["tril"], cm["cs"][:, h0:h0 + 1] - cm["cs_t"][h0:h0 + 1, :], NEG))
    lam1 = jnp.exp(jnp.where(cm["tril"], cm["cs"][:, h0 + 1:h0 + 2] - cm["cs_t"][h0 + 1:h0 + 2, :], NEG))
    m0, m1 = cb * lam0, cb * lam1
    ydiag = jnp.where(lo, _dot(m0, x), _dot(m1, x))
    ecs_p = _pair_sel(cm["ecs"], h0, lo)
    dte_p = _pair_sel(cm["dte"], h0, lo)
    yoff = _dot(cmat, hp, NT) * ecs_p
    xd = x * dte_p
    st = _dot(xd, bmat, TN)
    rlo = lax.broadcasted_iota(jnp.int32, (LANES, SSD_STATE), 0) < 64
    decay = jnp.where(rlo, cm["elast"][:, h0:h0 + 1], cm["elast"][:, h0 + 1:h0 + 2])
    h_next = hp * decay + st
    return dict(x=x, lam0=lam0, lam1=lam1, m0=m0, m1=m1, y=ydiag + yoff, yoff=yoff, ecs_p=ecs_p, dte_p=dte_p,
                xd=xd, decay=decay, h_next=h_next)


def _ssd_fwd(xbc, dt, a_row, *, name):
    S = xbc.shape[0]
    L = SSD_CHUNK
    nc = S // L

    def body(xs_ref, bm_ref, cm_ref, dt_ref, a_ref, y_ref, hs_ref, h_scr):
        @pl.when(pl.program_id(0) == 0)
        def _():
            h_scr[...] = jnp.zeros_like(h_scr)

        dtv = dt_ref[...]
        cm = _ssd_common(dtv, a_ref[...])
        lo = _lane((L, LANES)) < 64
        ys = []
        for g in range(2):
            bmat = bm_ref[:, g * 128:(g + 1) * 128]
            cmat = cm_ref[:, g * 128:(g + 1) * 128]
            cb = _dot(cmat, bmat, NT)
            for pr in range(2):
                p4 = 2 * g + pr
                hp = h_scr[p4]
                hs_ref[0, p4 * 128:(p4 + 1) * 128, :] = hp
                t = _ssd_pair(cm, 2 * p4, cb, xs_ref[:, p4 * 128:(p4 + 1) * 128], dtv, bmat, cmat, hp, lo)
                ys.append(t["y"])
                h_scr[p4] = t["h_next"]
        y_ref[...] = jnp.concatenate(ys, axis=1)

    return pl.pallas_call(
        body, name=name, grid=(nc,),
        in_specs=[pl.BlockSpec((L, 512), lambda c: (c, 0)), pl.BlockSpec((L, 256), lambda c: (c, 2)),
                  pl.BlockSpec((L, 256), lambda c: (c, 3)), pl.BlockSpec((L, 128), lambda c: (c, 0)),
                  pl.BlockSpec((1, 128), lambda c: (0, 0))],
        out_specs=[pl.BlockSpec((L, 512), lambda c: (c, 0)), pl.BlockSpec((1, 512, 128), lambda c: (c, 0, 0))],
        out_shape=[jax.ShapeDtypeStruct((S, 512), F32), jax.ShapeDtypeStruct((nc, 512, 128), F32)],
        scratch_shapes=[pltpu.VMEM((4, 128, 128), F32)],
        compiler_params=_params(("arbitrary",)),
    )(xbc, xbc, xbc, dt, a_row)


def _ssd_bwd(xbc, dt, a_row, hs, dy, *, name):
    S = xbc.shape[0]
    L = SSD_CHUNK
    nc = S // L

    def body(xs_ref, bm_ref, cm_ref, dt_ref, a_ref, hs_ref, dy_ref, dxs_ref, dbc_ref, ddt_ref, da_ref, g_scr):
        @pl.when(pl.program_id(0) == 0)
        def _():
            g_scr[...] = jnp.zeros_like(g_scr)
            da_ref[...] = jnp.zeros_like(da_ref)

        dtv = dt_ref[...]
        a_row_v = a_ref[...]
        cm = _ssd_common(dtv, a_row_v)
        lo = _lane((L, LANES)) < 64
        lane_row = _lane((1, LANES))
        ri = lax.broadcasted_iota(jnp.int32, (L, L), 0)
        ci = lax.broadcasted_iota(jnp.int32, (L, L), 1)
        triu = (ri <= ci).astype(F32)
        stril = ri > ci

        def halves(v, mask):
            return (jnp.sum(jnp.where(mask, v, 0.0), axis=1, keepdims=True),
                    jnp.sum(jnp.where(mask, 0.0, v), axis=1, keepdims=True))

        i_all = jnp.zeros((L, LANES), F32)
        yo_all = jnp.zeros((L, LANES), F32)
        w_all = jnp.zeros((L, LANES), F32)
        ddt_x = jnp.zeros((L, LANES), F32)
        e_row = jnp.zeros((1, LANES), F32)
        rlo = lax.broadcasted_iota(jnp.int32, (LANES, SSD_STATE), 0) < 64
        dxs, dbs, dcs = [], [], []
        for g in range(2):
            bmat = bm_ref[:, g * 128:(g + 1) * 128]
            cmat = cm_ref[:, g * 128:(g + 1) * 128]
            cb = _dot(cmat, bmat, NT)
            dcb = jnp.zeros((L, L), F32)
            db = jnp.zeros((L, SSD_STATE), F32)
            dc = jnp.zeros((L, SSD_STATE), F32)
            for pr in range(2):
                p4 = 2 * g + pr
                h0 = 2 * p4
                hp = hs_ref[0, p4 * 128:(p4 + 1) * 128, :]
                xp = xs_ref[:, p4 * 128:(p4 + 1) * 128]
                t = _ssd_pair(cm, h0, cb, xp, dtv, bmat, cmat, hp, lo)
                gst = g_scr[p4]
                dyp = dy_ref[:, p4 * 128:(p4 + 1) * 128]
                dy0 = jnp.where(lo, dyp, 0.0)
                dy1 = dyp - dy0
                bg = _dot(bmat, gst, NT)
                dx = _dot(t["m0"], dy0, TN) + _dot(t["m1"], dy1, TN) + bg * t["dte_p"]
                dm0, dm1 = _dot(dy0, t["x"], NT), _dot(dy1, t["x"], NT)
                dcb = dcb + dm0 * t["lam0"] + dm1 * t["lam1"]
                dye = dyp * t["ecs_p"]
                dc = dc + _dot(dye, hp)
                db = db + _dot(t["xd"], gst)
                i0 = jnp.sum(jnp.where(stril, _dot(triu, dm0 * t["m0"]), 0.0), axis=1, keepdims=True)
                i1 = jnp.sum(jnp.where(stril, _dot(triu, dm1 * t["m1"]), 0.0), axis=1, keepdims=True)
                yo0, yo1 = halves(dyp * t["yoff"], lo)
                w0, w1 = halves(t["xd"] * bg, lo)
                gh = gst * (hp * t["decay"])
                e0 = _sum0(jnp.sum(jnp.where(rlo, gh, 0.0), axis=1, keepdims=True))
                e1 = _sum0(jnp.sum(jnp.where(rlo, 0.0, gh), axis=1, keepdims=True))
                x0, x1 = halves(dx * xp, lo)
                oh0 = (lane_row == h0).astype(F32)
                oh1 = (lane_row == h0 + 1).astype(F32)
                i_all = i_all + i0 * oh0 + i1 * oh1
                yo_all = yo_all + yo0 * oh0 + yo1 * oh1
                w_all = w_all + w0 * oh0 + w1 * oh1
                e_row = e_row + e0 * oh0 + e1 * oh1
                ddt_x = ddt_x + x0 * oh0 + x1 * oh1
                dxs.append(dx * _pair_sel(dtv, h0, lo))
                g_scr[p4] = gst * t["decay"] + _dot(dye, cmat, TN)
            dbs.append(db + _dot(dcb, cmat, TN))
            dcs.append(dc + _dot(dcb, bmat))
        da = i_all + _dot_exact(triu, yo_all) + _dot_exact(stril.astype(F32), w_all) + e_row
        ddt_ref[...] = da * a_row_v + ddt_x
        da_ref[...] += _sum0(da * dtv)
        dxs_ref[...] = jnp.concatenate(dxs, axis=1)
        dbc_ref[...] = jnp.concatenate(dbs + dcs, axis=1)

    rev = lambda c: nc - 1 - c
    return pl.pallas_call(
        body, name=name, grid=(nc,),
        in_specs=[pl.BlockSpec((L, 512), lambda c: (rev(c), 0)), pl.BlockSpec((L, 256), lambda c: (rev(c), 2)),
                  pl.BlockSpec((L, 256), lambda c: (rev(c), 3)), pl.BlockSpec((L, 128), lambda c: (rev(c), 0)),
                  pl.BlockSpec((1, 128), lambda c: (0, 0)), pl.BlockSpec((1, 512, 128), lambda c: (rev(c), 0, 0)),
                  pl.BlockSpec((L, 512), lambda c: (rev(c), 0))],
        out_specs=[pl.BlockSpec((L, 512), lambda c: (rev(c), 0)), pl.BlockSpec((L, 512), lambda c: (rev(c), 0)),
                   pl.BlockSpec((L, 128), lambda c: (rev(c), 0)), pl.BlockSpec((1, 128), lambda c: (0, 0))],
        out_shape=[jax.ShapeDtypeStruct((S, 512), F32), jax.ShapeDtypeStruct((S, 512), F32),
                   jax.ShapeDtypeStruct((S, 128), F32), jax.ShapeDtypeStruct((1, 128), F32)],
        scratch_shapes=[pltpu.VMEM((4, 128, 128), F32)],
        compiler_params=_params(("arbitrary",)),
    )(xbc, xbc, xbc, dt, a_row, hs, dy)


MLA_SCALE = MLA_QK ** -0.5


def _causal_scores(q, k, qi, ki, t):
    s = _dot(q, k, NT) * MLA_SCALE
    row = qi * t + lax.broadcasted_iota(jnp.int32, (t, t), 0)
    col = ki * t + lax.broadcasted_iota(jnp.int32, (t, t), 1)
    return jnp.where(col <= row, s, NEG)


def _mla_fwd(q, k, kv, *, name):
    S = q.shape[0]
    t = min(ATTN_TILE, S)
    nq = S // t

    def body(q_ref, k_ref, v_ref, o_ref, lse_ref, m_scr, l_scr, acc_scr):
        qi, ki = pl.program_id(1), pl.program_id(2)

        @pl.when(ki == 0)
        def _():
            m_scr[...] = jnp.full_like(m_scr, NEG)
            l_scr[...] = jnp.zeros_like(l_scr)
            acc_scr[...] = jnp.zeros_like(acc_scr)

        @pl.when(ki <= qi)
        def _():
            s = _causal_scores(q_ref[...], k_ref[...], qi, ki, t)
            m_old = m_scr[:, 0:1]
            m_new = jnp.maximum(m_old, jnp.max(s, axis=1, keepdims=True))
            p = jnp.exp(s - m_new)
            corr = jnp.exp(m_old - m_new)
            l_scr[...] = jnp.broadcast_to(corr * l_scr[:, 0:1] + jnp.sum(p, axis=1, keepdims=True), l_scr.shape)
            acc_scr[...] = corr * acc_scr[...] + _dot(p, v_ref[...])
            m_scr[...] = jnp.broadcast_to(m_new, m_scr.shape)

        @pl.when(ki == nq - 1)
        def _():
            l = l_scr[:, 0:1]
            o_ref[...] = acc_scr[...] / l
            lse_ref[0] = jnp.broadcast_to(m_scr[:, 0:1] + jnp.log(l), (t, LANES))

    return pl.pallas_call(
        body, name=name, grid=(MLA_HEADS, nq, nq),
        in_specs=[pl.BlockSpec((t, 128), lambda h, qi, ki: (qi, h)),
                  pl.BlockSpec((t, 128), lambda h, qi, ki: (jnp.minimum(ki, qi), h)),
                  pl.BlockSpec((t, 128), lambda h, qi, ki: (jnp.minimum(ki, qi), 2 * h + 1))],
        out_specs=[pl.BlockSpec((t, 128), lambda h, qi, ki: (qi, h)),
                   pl.BlockSpec((1, t, 128), lambda h, qi, ki: (h, qi, 0))],
        out_shape=[jax.ShapeDtypeStruct((S, MLA_HEADS * 128), F32), jax.ShapeDtypeStruct((MLA_HEADS, S, 128), F32)],
        scratch_shapes=[pltpu.VMEM((t, 128), F32), pltpu.VMEM((t, 128), F32), pltpu.VMEM((t, 128), F32)],
        compiler_params=_params(("parallel", "parallel", "arbitrary")),
    )(q, k, kv)


def _mla_bwd_dkv(q, k, kv, o, do, lse, *, name):
    S = q.shape[0]
    t = min(ATTN_TILE, S)
    nq = S // t

    def body(q_ref, k_ref, v_ref, o_ref, do_ref, lse_ref, dkv_ref):
        ki, qi = pl.program_id(1), pl.program_id(2)

        @pl.when(qi == 0)
        def _():
            dkv_ref[...] = jnp.zeros_like(dkv_ref)

        @pl.when(qi >= ki)
        def _():
            qv, dov = q_ref[...], do_ref[...]
            s = _causal_scores(qv, k_ref[...], qi, ki, t)
            p = jnp.exp(s - lse_ref[0][:, 0:1])
            dv = _dot(p, dov, TN)
            dp = _dot(dov, v_ref[...], NT)
            delta = jnp.sum(dov * o_ref[...], axis=1, keepdims=True)
            ds = p * (dp - delta) * MLA_SCALE
            dkv_ref[...] += jnp.concatenate([_dot(ds, qv, TN), dv], axis=1)

    qmap = lambda h, ki, qi: (jnp.maximum(qi, ki), h)
    return pl.pallas_call(
        body, name=name, grid=(MLA_HEADS, nq, nq),
        in_specs=[pl.BlockSpec((t, 128), qmap),
                  pl.BlockSpec((t, 128), lambda h, ki, qi: (ki, h)),
                  pl.BlockSpec((t, 128), lambda h, ki, qi: (ki, 2 * h + 1)),
                  pl.BlockSpec((t, 128), qmap), pl.BlockSpec((t, 128), qmap),
                  pl.BlockSpec((1, t, 128), lambda h, ki, qi: (h, jnp.maximum(qi, ki), 0))],
        out_specs=pl.BlockSpec((t, 256), lambda h, ki, qi: (ki, h)),
        out_shape=jax.ShapeDtypeStruct((S, MLA_HEADS * 256), F32),
        compiler_params=_params(("parallel", "parallel", "arbitrary")),
    )(q, k, kv, o, do, lse)


def _mla_bwd_dq(q, k, kv, o, do, lse, *, name):
    S = q.shape[0]
    t = min(ATTN_TILE, S)
    nq = S // t

    def body(q_ref, k_ref, v_ref, o_ref, do_ref, lse_ref, dq_ref):
        qi, ki = pl.program_id(1), pl.program_id(2)

        @pl.when(ki == 0)
        def _():
            dq_ref[...] = jnp.zeros_like(dq_ref)

        @pl.when(ki <= qi)
        def _():
            dov, kv = do_ref[...], k_ref[...]
            s = _causal_scores(q_ref[...], kv, qi, ki, t)
            p = jnp.exp(s - lse_ref[0][:, 0:1])
            dp = _dot(dov, v_ref[...], NT)
            delta = jnp.sum(dov * o_ref[...], axis=1, keepdims=True)
            ds = p * (dp - delta) * MLA_SCALE
            dq_ref[...] += _dot(ds, kv)

    qmap = lambda h, qi, ki: (qi, h)
    return pl.pallas_call(
        body, name=name, grid=(MLA_HEADS, nq, nq),
        in_specs=[pl.BlockSpec((t, 128), qmap),
                  pl.BlockSpec((t, 128), lambda h, qi, ki: (jnp.minimum(ki, qi), h)),
                  pl.BlockSpec((t, 128), lambda h, qi, ki: (jnp.minimum(ki, qi), 2 * h + 1)),
                  pl.BlockSpec((t, 128), qmap), pl.BlockSpec((t, 128), qmap),
                  pl.BlockSpec((1, t, 128), lambda h, qi, ki: (h, qi, 0))],
        out_specs=pl.BlockSpec((t, 128), qmap),
        out_shape=jax.ShapeDtypeStruct((S, MLA_HEADS * 128), F32),
        compiler_params=_params(("parallel", "parallel", "arbitrary")),
    )(q, k, kv, o, do, lse)


HBM = pl.BlockSpec(memory_space=pl.ANY)


class _Step:
    def __init__(self, inputs, out_shapes, n_sems, start, finish, mid=None):
        self.inputs, self.out_shapes, self.n_sems = inputs, out_shapes, n_sems
        self.start, self.finish, self.mid = start, finish, mid


def _place():
    x, y, c = lax.axis_index("x"), lax.axis_index("y"), lax.axis_index("c")
    chips = [(1 - x, y), (x, 1 - y), (1 - x, 1 - y)]
    return x, y, c, chips


def _chunks(rows, tile):
    return next(n for n in (4, 3, 2, 1) if rows % (n * tile) == 0)


def _remote(src, dst, sems, j, to):
    return pltpu.make_async_remote_copy(src_ref=src, dst_ref=dst, send_sem=sems[0].at[j], recv_sem=sems[1].at[j],
                                        device_id=to, device_id_type=MESH)


def _gather_step(wp):
    R, C = wp.shape
    H = R // 2
    nq = _chunks(H, 16)
    CH = H // nq

    def copies(ins, outs, sems):
        x, y, c, chips = _place()
        sib, me = (x, y, 1 - c), 2 * x + y
        w_ref, out_ref = ins[0], outs[0]

        def piece(k, hc, q):
            return out_ref.at[k, pl.ds(hc * H + q * CH, CH), :]

        sends, landed, fwds, fwd_landed = [], [], [], []
        for q in range(nq):
            for j, (px, py) in enumerate(chips):
                k = 2 * px + py
                sends.append(_remote(w_ref.at[pl.ds(c * H + q * CH, CH), :], piece(me, c, q), sems, j * nq + q,
                                     (px, py, c)))
                landed.append(_remote(piece(k, c, q), piece(k, c, q), sems, j * nq + q, (px, py, c)))
                fwds.append(_remote(piece(k, c, q), piece(k, c, q), sems, (3 + j) * nq + q, sib))
                fwd_landed.append(_remote(piece(k, 1 - c, q), piece(k, 1 - c, q), sems, (3 + j) * nq + q, sib))
        return sends, landed, fwds, fwd_landed

    def start(ins, outs, sems):
        for cp in copies(ins, outs, sems)[0]:
            cp.start()

    def mid(ins, outs, sems):
        _, landed, fwds, _ = copies(ins, outs, sems)
        for arrived, onward in zip(landed, fwds):
            arrived.wait_recv()
            onward.start()

    def finish(ins, outs, sems):
        sends, _, fwds, fwd_landed = copies(ins, outs, sems)
        for cp in fwd_landed:
            cp.wait_recv()
        for cp in sends + fwds:
            cp.wait_send()

    return _Step([wp], [jax.ShapeDtypeStruct((N_SHARD, R, C), wp.dtype)], 6 * nq, start, finish, mid)


def _pair_exchange_step(gp):
    n, R, C = gp.shape
    H = R // 2
    nq = _chunks(H, 8)
    CH = H // nq

    def copies(ins, outs, sems):
        x, y, c, _ = _place()
        return [_remote(ins[0].at[k, pl.ds((1 - c) * H + q * CH, CH), :], outs[0].at[k, pl.ds(q * CH, CH), :], sems,
                        k * nq + q, (x, y, 1 - c)) for k in range(n) for q in range(nq)]

    def start(ins, outs, sems):
        for cp in copies(ins, outs, sems):
            cp.start()

    def finish(ins, outs, sems):
        for cp in copies(ins, outs, sems):
            cp.wait()

    return _Step([gp], [jax.ShapeDtypeStruct((n, H, C), gp.dtype)], n * nq, start, finish)


def _chip_exchange_step(pb):
    n, H, C = pb.shape
    nq = _chunks(H, 16)
    CH = H // nq

    def copies(ins, outs, sems):
        x, y, c, chips = _place()
        return [_remote(ins[0].at[2 * px + py, pl.ds(q * CH, CH), :], outs[0].at[j, pl.ds(q * CH, CH), :], sems,
                        j * nq + q, (px, py, c)) for q in range(nq) for j, (px, py) in enumerate(chips)]

    def start(ins, outs, sems):
        for cp in copies(ins, outs, sems):
            cp.start()

    def finish(ins, outs, sems):
        for cp in copies(ins, outs, sems):
            cp.wait()

    return _Step([pb], [jax.ShapeDtypeStruct((3, H, C), pb.dtype)], 3 * nq, start, finish)


def _pair_join_step(q):
    H, C = q.shape
    nq = _chunks(H, 8)
    CH = H // nq

    def copies(ins, outs, sems):
        x, y, c, _ = _place()
        return [_remote(ins[0].at[pl.ds(j * CH, CH), :], outs[0].at[pl.ds(j * CH, CH), :], sems, j, (x, y, 1 - c))
                for j in range(nq)]

    def start(ins, outs, sems):
        for cp in copies(ins, outs, sems):
            cp.start()

    def finish(ins, outs, sems):
        for cp in copies(ins, outs, sems):
            cp.wait()

    return _Step([q], [jax.ShapeDtypeStruct((H, C), q.dtype)], nq, start, finish)


def _sem_scratch(step):
    return [pltpu.SemaphoreType.DMA((step.n_sems,)), pltpu.SemaphoreType.DMA((step.n_sems,))]


def _run_step(step, name):
    ni, no = len(step.inputs), len(step.out_shapes)

    def body(*refs):
        ins, outs, sems = refs[:ni], refs[ni:ni + no], refs[ni + no:]
        step.start(ins, outs, sems)
        if step.mid is not None:
            step.mid(ins, outs, sems)
        step.finish(ins, outs, sems)

    return pl.pallas_call(body, name=name, in_specs=[HBM] * ni, out_specs=[HBM] * no, out_shape=step.out_shapes,
                          scratch_shapes=_sem_scratch(step))(*step.inputs)


def _call_with_step(core, step, flags, args, *, name, grid, in_specs, out_specs, out_shape, sem):
    if step is None:
        return pl.pallas_call(core, name=name, grid=grid, in_specs=in_specs, out_specs=out_specs,
                              out_shape=out_shape, compiler_params=_params(sem))(*args)
    n_in, n_out = len(in_specs), len(out_specs)
    si, so = len(step.inputs), len(step.out_shapes)

    def body(*refs):
        ins, s_ins = refs[:n_in], refs[n_in:n_in + si]
        outs = refs[n_in + si:n_in + si + n_out]
        s_outs = refs[n_in + si + n_out:n_in + si + n_out + so]
        sems = refs[n_in + si + n_out + so:]
        first, middle, last = flags()

        @pl.when(first)
        def _():
            step.start(s_ins, s_outs, sems)

        if step.mid is not None:
            @pl.when(middle)
            def _():
                step.mid(s_ins, s_outs, sems)

        core(*ins, *outs)

        @pl.when(last)
        def _():
            step.finish(s_ins, s_outs, sems)

    return pl.pallas_call(
        body, name=name, grid=grid, in_specs=list(in_specs) + [HBM] * si, out_specs=list(out_specs) + [HBM] * so,
        out_shape=list(out_shape) + list(step.out_shapes), scratch_shapes=_sem_scratch(step),
        compiler_params=_params(("arbitrary",) * len(grid)))(*args, *step.inputs)


def _attn_flags(nq):
    h, qi = pl.program_id(0), pl.program_id(1)
    return ((h == 0) & (qi == 0), (h == MLA_HEADS // 2) & (qi == 0), (h == MLA_HEADS - 1) & (qi == nq - 1))


ATTN_SPLIT = 1
ATTN_KEY_SPLIT = 1


def _att_mask(s_t, q0, k0):
    krow = k0 + lax.broadcasted_iota(jnp.int32, s_t.shape, 0)
    qcol = q0 + lax.broadcasted_iota(jnp.int32, s_t.shape, 1)
    return jnp.where(krow <= qcol, s_t, NEG)


def _loop2(lo, hi, step, carry):
    def two(i, c):
        kb = lo + 2 * i
        return step(kb + 1, step(kb, c))

    carry = lax.fori_loop(0, (hi - lo) // 2, two, carry)
    return lax.cond((hi - lo) % 2 == 1, lambda c: step(hi - 1, c), lambda c: c, carry)


def _rows(ref, blk, t):
    return ref[pl.ds(pl.multiple_of(blk * t, t), t), :]


def _cols(ref, blk, t):
    return ref[:, pl.ds(pl.multiple_of(blk * t, t), t)]


def _attn_fwd(q, k, v_t, *, name, hosted=None):
    S = q.shape[0]
    t = min(ATTN_TILE, S)
    nq = S // t

    def body(q_ref, k_ref, vt_ref, o_ref, lse_ref):
        qi = pl.program_id(1)
        w = t // ATTN_SPLIT
        qs = [q_ref[s * w:(s + 1) * w, :] for s in range(ATTN_SPLIT)]

        tk = t // ATTN_KEY_SPLIT

        def step(kb, carry, masked):
            kt, vt = _rows(k_ref, kb, tk), _cols(vt_ref, kb, tk)
            out = []
            for s, (m, l, acc) in enumerate(carry):
                s_t = lax.dot_general(kt, qs[s], NT, preferred_element_type=F32)
                if masked:
                    s_t = _att_mask(s_t, qi * t + s * w, kb * tk)
                m_new = jnp.maximum(m, jnp.max(s_t, axis=0, keepdims=True))
                p_t = jnp.exp(s_t - m_new)
                corr = jnp.exp(m - m_new)
                l = corr * l + jnp.sum(p_t, axis=0, keepdims=True)
                acc = corr * acc + lax.dot_general(vt, p_t.astype(BF16), NN, preferred_element_type=F32)
                out.append((m_new, l, acc))
            return tuple(out)

        init = tuple((jnp.full((1, w), NEG, F32), jnp.zeros((1, w), F32), jnp.zeros((LANES, w), F32))
                     for _ in range(ATTN_SPLIT))
        carry = _loop2(0, qi * ATTN_KEY_SPLIT, lambda kb, c: step(kb, c, False), init)
        for j in range(ATTN_KEY_SPLIT):
            carry = step(qi * ATTN_KEY_SPLIT + j, carry, True)
        for s, (m, l, acc) in enumerate(carry):
            o_ref[:, s * w:(s + 1) * w] = acc / l
            lse_ref[0, :, s * w:(s + 1) * w] = m + jnp.log(l)

    return _call_with_step(
        body, hosted, lambda: _attn_flags(nq), (q, k, v_t), name=name, grid=(MLA_HEADS, nq),
        in_specs=[pl.BlockSpec((t, LANES), lambda h, qi: (qi, h)),
                  pl.BlockSpec((S, LANES), lambda h, qi: (0, h)),
                  pl.BlockSpec((LANES, S), lambda h, qi: (h, 0))],
        out_specs=[pl.BlockSpec((LANES, t), lambda h, qi: (h, qi)),
                   pl.BlockSpec((1, 1, t), lambda h, qi: (h, 0, qi))],
        out_shape=[jax.ShapeDtypeStruct((MLA_HEADS * LANES, S), F32), jax.ShapeDtypeStruct((MLA_HEADS, 1, S), F32)],
        sem=("parallel", "arbitrary"))


def _attn_bwd_dq(q, k, k_t, v, o_t, do_t, lse, *, name, hosted=None):
    S = q.shape[0]
    t = min(ATTN_TILE, S)
    nq = S // t

    def body(q_ref, k_ref, kt_ref, v_ref, o_ref, do_ref, lse_ref, dq_ref, delta_ref):
        qi = pl.program_id(1)
        qv = q_ref[...]
        dov = do_ref[...]
        delta = jnp.sum(dov * o_ref[...], axis=0, keepdims=True)
        delta_ref[0] = delta
        dob = dov.astype(BF16)
        lse_v = lse_ref[0]

        def step(kb, acc, masked):
            s_t = lax.dot_general(_rows(k_ref, kb, t), qv, NT, preferred_element_type=F32)
            if masked:
                s_t = _att_mask(s_t, qi * t, kb * t)
            p_t = jnp.exp(s_t - lse_v)
            dp_t = lax.dot_general(_rows(v_ref, kb, t), dob, NN, preferred_element_type=F32)
            ds_t = (p_t * (dp_t - delta)).astype(BF16)
            return acc + lax.dot_general(_cols(kt_ref, kb, t), ds_t, NN, preferred_element_type=F32)

        acc = _loop2(0, qi, lambda kb, c: step(kb, c, False), jnp.zeros((LANES, t), F32))
        dq_ref[...] = step(qi, acc, True)

    tile_t = pl.BlockSpec((LANES, t), lambda h, qi: (h, qi))
    stat = pl.BlockSpec((1, 1, t), lambda h, qi: (h, 0, qi))
    seq = pl.BlockSpec((S, LANES), lambda h, qi: (0, h))
    return _call_with_step(
        body, hosted, lambda: _attn_flags(nq), (q, k, k_t, v, o_t, do_t, lse), name=name, grid=(MLA_HEADS, nq),
        in_specs=[pl.BlockSpec((t, LANES), lambda h, qi: (qi, h)), seq,
                  pl.BlockSpec((LANES, S), lambda h, qi: (h, 0)), seq, tile_t, tile_t, stat],
        out_specs=[tile_t, stat],
        out_shape=[jax.ShapeDtypeStruct((MLA_HEADS * LANES, S), F32), jax.ShapeDtypeStruct((MLA_HEADS, 1, S), F32)],
        sem=("parallel", "arbitrary"))


def _attn_bwd_dkv(q, q_t, k, v, do_t, lse, delta, *, name, hosted=None):
    S = q.shape[0]
    t = min(ATTN_TILE, S)
    nq = S // t

    def body(q_ref, qt_ref, k_ref, v_ref, do_ref, lse_ref, delta_ref, dk_ref, dv_ref):
        ki = pl.program_id(1)
        kv, vv = k_ref[...], v_ref[...]

        def step(qb, carry, masked):
            dk, dv = carry
            s_t = lax.dot_general(kv, _rows(q_ref, qb, t), NT, preferred_element_type=F32)
            if masked:
                s_t = _att_mask(s_t, qb * t, ki * t)
            p_t = jnp.exp(s_t - _cols(lse_ref.at[0], qb, t))
            dob = _cols(do_ref, qb, t).astype(BF16)
            dv = dv + lax.dot_general(dob, p_t.astype(BF16), NT, preferred_element_type=F32)
            dp_t = lax.dot_general(vv, dob, NN, preferred_element_type=F32)
            ds_t = (p_t * (dp_t - _cols(delta_ref.at[0], qb, t))).astype(BF16)
            dk = dk + lax.dot_general(_cols(qt_ref, qb, t), ds_t, NT, preferred_element_type=F32)
            return dk, dv

        zero = jnp.zeros((LANES, t), F32)
        carry = step(ki, (zero, zero), True)
        dk, dv = _loop2(ki + 1, nq, lambda qb, c: step(qb, c, False), carry)
        dk_ref[...] = dk
        dv_ref[...] = dv

    tile = pl.BlockSpec((t, LANES), lambda h, ki: (ki, h))
    tile_t = pl.BlockSpec((LANES, t), lambda h, ki: (h, ki))
    seq = pl.BlockSpec((S, LANES), lambda h, ki: (0, h))
    seq_t = pl.BlockSpec((LANES, S), lambda h, ki: (h, 0))
    stat = pl.BlockSpec((1, 1, S), lambda h, ki: (h, 0, 0))
    return _call_with_step(
        body, hosted, lambda: _attn_flags(nq), (q, q_t, k, v, do_t, lse, delta), name=name, grid=(MLA_HEADS, nq),
        in_specs=[seq, seq_t, tile, tile, seq_t, stat, stat],
        out_specs=[tile_t, tile_t],
        out_shape=[jax.ShapeDtypeStruct((MLA_HEADS * LANES, S), F32)] * 2,
        sem=("parallel", "arbitrary"))


def _attn_fwd_p(q, k, v_t, *, name, hosted=None):
    S = q.shape[0]
    t = min(ATTN_TILE, S)
    nq = S // t

    def body(q_ref, k_ref, vt_ref, o_ref, lse_ref):
        qi = pl.program_id(1)
        qv = q_ref[...]

        def scores(kb):
            return lax.dot_general(_rows(k_ref, kb, t), qv, NT, preferred_element_type=F32)

        def weigh(kb, p_t):
            return lax.dot_general(_cols(vt_ref, kb, t), p_t, NN, preferred_element_type=F32)

        def soft(s_t, m, l):
            m_new = jnp.maximum(m, jnp.max(s_t, axis=0, keepdims=True))
            p_t = jnp.exp(s_t - m_new)
            corr = jnp.exp(m - m_new)
            return m_new, corr * l + jnp.sum(p_t, axis=0, keepdims=True), corr, p_t.astype(BF16)

        def step(kb, carry):
            s_cur, m, l, acc, p_prev, corr_prev = carry
            s_next = scores(kb + 1)
            acc = corr_prev * acc + weigh(jnp.maximum(kb - 1, 0), p_prev)
            m, l, corr, p_t = soft(s_cur, m, l)
            return s_next, m, l, acc, p_t, corr

        init = (scores(0), jnp.full((1, t), NEG, F32), jnp.zeros((1, t), F32), jnp.zeros((LANES, t), F32),
                jnp.zeros((t, t), BF16), jnp.ones((1, t), F32))
        s_cur, m, l, acc, p_prev, corr_prev = lax.fori_loop(0, qi, step, init)
        acc = corr_prev * acc + weigh(jnp.maximum(qi - 1, 0), p_prev)
        m, l, corr, p_t = soft(_att_mask(s_cur, qi * t, qi * t), m, l)
        acc = corr * acc + weigh(qi, p_t)
        o_ref[...] = acc / l
        lse_ref[0] = m + jnp.log(l)

    return _call_with_step(
        body, hosted, lambda: _attn_flags(nq), (q, k, v_t), name=name, grid=(MLA_HEADS, nq),
        in_specs=[pl.BlockSpec((t, LANES), lambda h, qi: (qi, h)),
                  pl.BlockSpec((S, LANES), lambda h, qi: (0, h)),
                  pl.BlockSpec((LANES, S), lambda h, qi: (h, 0))],
        out_specs=[pl.BlockSpec((LANES, t), lambda h, qi: (h, qi)),
                   pl.BlockSpec((1, 1, t), lambda h, qi: (h, 0, qi))],
        out_shape=[jax.ShapeDtypeStruct((MLA_HEADS * LANES, S), F32), jax.ShapeDtypeStruct((MLA_HEADS, 1, S), F32)],
        sem=("parallel", "arbitrary"))


def _attn_bwd_dq_p(q, k, k_t, v, o_t, do_t, lse, *, name, hosted=None):
    S = q.shape[0]
    t = min(ATTN_TILE, S)
    nq = S // t

    def body(q_ref, k_ref, kt_ref, v_ref, o_ref, do_ref, lse_ref, dq_ref, delta_ref):
        qi = pl.program_id(1)
        qv = q_ref[...]
        dov = do_ref[...]
        delta = jnp.sum(dov * o_ref[...], axis=0, keepdims=True)
        delta_ref[0] = delta
        dob = dov.astype(BF16)
        lse_v = lse_ref[0]

        def front(kb):
            return (lax.dot_general(_rows(k_ref, kb, t), qv, NT, preferred_element_type=F32),
                    lax.dot_general(_rows(v_ref, kb, t), dob, NN, preferred_element_type=F32))

        def back(kb, ds_t):
            return lax.dot_general(_cols(kt_ref, kb, t), ds_t, NN, preferred_element_type=F32)

        def mid(s_t, dp_t):
            return (jnp.exp(s_t - lse_v) * (dp_t - delta)).astype(BF16)

        def step(kb, carry):
            s_cur, dp_cur, acc, ds_prev = carry
            s_next, dp_next = front(kb + 1)
            acc = acc + back(jnp.maximum(kb - 1, 0), ds_prev)
            return s_next, dp_next, acc, mid(s_cur, dp_cur)

        init = (*front(0), jnp.zeros((LANES, t), F32), jnp.zeros((t, t), BF16))
        s_cur, dp_cur, acc, ds_prev = lax.fori_loop(0, qi, step, init)
        acc = acc + back(jnp.maximum(qi - 1, 0), ds_prev)
        dq_ref[...] = acc + back(qi, mid(_att_mask(s_cur, qi * t, qi * t), dp_cur))

    tile_t = pl.BlockSpec((LANES, t), lambda h, qi: (h, qi))
    stat = pl.BlockSpec((1, 1, t), lambda h, qi: (h, 0, qi))
    seq = pl.BlockSpec((S, LANES), lambda h, qi: (0, h))
    return _call_with_step(
        body, hosted, lambda: _attn_flags(nq), (q, k, k_t, v, o_t, do_t, lse), name=name, grid=(MLA_HEADS, nq),
        in_specs=[pl.BlockSpec((t, LANES), lambda h, qi: (qi, h)), seq,
                  pl.BlockSpec((LANES, S), lambda h, qi: (h, 0)), seq, tile_t, tile_t, stat],
        out_specs=[tile_t, stat],
        out_shape=[jax.ShapeDtypeStruct((MLA_HEADS * LANES, S), F32), jax.ShapeDtypeStruct((MLA_HEADS, 1, S), F32)],
        sem=("parallel", "arbitrary"))


def _attn_bwd_dkv_p(q, q_t, k, v, do_t, lse, delta, *, name, hosted=None):
    S = q.shape[0]
    t = min(ATTN_TILE, S)
    nq = S // t

    def body(q_ref, qt_ref, k_ref, v_ref, do_ref, lse_ref, delta_ref, dk_ref, dv_ref):
        ki = pl.program_id(1)
        kv, vv = k_ref[...], v_ref[...]

        def grad_out(qb):
            return _cols(do_ref, qb, t).astype(BF16)

        def front(qb):
            return (lax.dot_general(kv, _rows(q_ref, qb, t), NT, preferred_element_type=F32),
                    lax.dot_general(vv, grad_out(qb), NN, preferred_element_type=F32))

        def mid(s_t, dp_t, qb):
            p_t = jnp.exp(s_t - _cols(lse_ref.at[0], qb, t))
            return p_t.astype(BF16), (p_t * (dp_t - _cols(delta_ref.at[0], qb, t))).astype(BF16)

        def back(qb, dk, dv, p_t, ds_t):
            return (dk + lax.dot_general(_cols(qt_ref, qb, t), ds_t, NT, preferred_element_type=F32),
                    dv + lax.dot_general(grad_out(qb), p_t, NT, preferred_element_type=F32))

        def step(qb, carry):
            s_cur, dp_cur, dk, dv, p_prev, ds_prev = carry
            s_next, dp_next = front(jnp.minimum(qb + 1, nq - 1))
            dk, dv = back(qb - 1, dk, dv, p_prev, ds_prev)
            p_t, ds_t = mid(s_cur, dp_cur, qb)
            return s_next, dp_next, dk, dv, p_t, ds_t

        s0, dp0 = front(ki)
        p0, ds0 = mid(_att_mask(s0, ki * t, ki * t), dp0, ki)
        zero = jnp.zeros((LANES, t), F32)
        init = (*front(jnp.minimum(ki + 1, nq - 1)), zero, zero, p0, ds0)
        _, _, dk, dv, p_prev, ds_prev = lax.fori_loop(ki + 1, nq, step, init)
        dk, dv = back(nq - 1, dk, dv, p_prev, ds_prev)
        dk_ref[...] = dk
        dv_ref[...] = dv

    tile = pl.BlockSpec((t, LANES), lambda h, ki: (ki, h))
    tile_t = pl.BlockSpec((LANES, t), lambda h, ki: (h, ki))
    seq = pl.BlockSpec((S, LANES), lambda h, ki: (0, h))
    seq_t = pl.BlockSpec((LANES, S), lambda h, ki: (h, 0))
    stat = pl.BlockSpec((1, 1, S), lambda h, ki: (h, 0, 0))
    return _call_with_step(
        body, hosted, lambda: _attn_flags(nq), (q, q_t, k, v, do_t, lse, delta), name=name, grid=(MLA_HEADS, nq),
        in_specs=[seq, seq_t, tile, tile, seq_t, stat, stat],
        out_specs=[tile_t, tile_t],
        out_shape=[jax.ShapeDtypeStruct((MLA_HEADS * LANES, S), F32)] * 2,
        sem=("parallel", "arbitrary"))


def _fn_ln(ctx, x, g, b):
    xhat, _ = _ln_stats(x)
    return (xhat * g + b,)


def _fn_conv_fwd(ctx, u, up, dtr, w8, cb, dtb):
    first = ctx.i == 0
    y = u * w8[3:4] + cb
    for s in (1, 2, 3):
        y = y + _shift_down(u, up, s, first) * w8[3 - s:4 - s]
    act = y * _sigmoid(y)
    v = dtr + dtb
    e = jnp.exp(-jnp.abs(v))
    one_p = 1.0 + e
    log1p = jnp.where(one_p == 1.0, e, jnp.log(one_p) * e / (one_p - 1.0))
    return y, act, jnp.maximum(v, 0.0) + log1p


def _fn_ssd_post(ctx, y, xs, z, dexp, g):
    yg = (y + xs * dexp) * (z * _sigmoid(z))
    outs = []
    for k in range(2):
        v = yg[:, 256 * k:256 * (k + 1)]
        outs.append(v * lax.rsqrt(_mean1(v * v) + RMS_EPS))
    return (jnp.concatenate(outs, axis=1) * g,)


def _fn_ssd_post_bwd(ctx, dyn, y, xs, z, dexp, g):
    yt = y + xs * dexp
    sig = _sigmoid(z)
    sz = z * sig
    yg = yt * sz
    dyh = dyn * g
    yh, dyg = [], []
    for k in range(2):
        sl = slice(256 * k, 256 * (k + 1))
        v = yg[:, sl]
        rs = lax.rsqrt(_mean1(v * v) + RMS_EPS)
        vh = v * rs
        yh.append(vh)
        dyg.append(rs * (dyh[:, sl] - vh * _mean1(dyh[:, sl] * vh)))
    yh = jnp.concatenate(yh, axis=1)
    dyg = jnp.concatenate(dyg, axis=1)
    dyt = dyg * sz
    dz = dyg * yt * (sig * (1.0 + z * (1.0 - sig)))
    return dyt, dz, dyt * dexp, _sum0(dyt * xs), _sum0(dyn * yh)


def _fn_mla_pre(ctx, ql, kvl, gq, gkv):
    return _rms_fwd(ql, gq), _rms_fwd(kvl, gkv)


def _fn_mla_pre_bwd(ctx, ql, kvl, dqn, dkvn_k, dkvn_v, gq, gkv):
    dql, dgq = _rms_bwd(ql, dqn, gq)
    dkvl, dgkv = _rms_bwd(kvl, dkvn_k + dkvn_v, gkv)
    return dql, dkvl, dgq, dgkv


def _fn_rope(ctx, qp, kn, kr, ta, tb, tc):
    kpe = _rope(kr, ta, tb, tc)
    qs, ks = [], []
    for h in range(MLA_HEADS):
        sl = slice(128 * h, 128 * (h + 1))
        qs.append(_rope(qp[:, sl], ta, tb, tc) * MLA_SCALE)
        ks.append(kn[:, sl] + kpe)
    return jnp.concatenate(qs, axis=1), jnp.concatenate(ks, axis=1)


def _fn_rope_bwd(ctx, dq, dk, ta, tb, tc):
    qs = []
    ksum = jnp.zeros_like(ta)
    for h in range(MLA_HEADS):
        sl = slice(128 * h, 128 * (h + 1))
        qs.append(_rope_bwd(dq[:, sl] * MLA_SCALE, ta, tb, tc))
        ksum = ksum + dk[:, sl]
    lane = _lane(ksum.shape)
    dkr = jnp.where((lane >= 64) & (lane < 96), _rope_bwd(ksum, ta, tb, tc), 0.0)
    return jnp.concatenate(qs, axis=1), dkr


MEM_SCALE = MEM_HEAD_DIM ** -0.5


def _mem_probs(qh, kh):
    s = _dot(qh, kh, NT) * MEM_SCALE
    p = jnp.exp(s - jnp.max(s, axis=1, keepdims=True))
    return p / jnp.sum(p, axis=1, keepdims=True)


def _fn_mem_fwd(ctx, q, km, vm):
    outs = []
    for h in range(MEM_HEADS):
        sl = slice(256 * h, 256 * (h + 1))
        outs.append(_dot(_mem_probs(q[:, sl], km[:, sl]), vm[:, sl]))
    return (jnp.concatenate(outs, axis=1),)


def _fn_mem_bwd(ctx, q, do, km, vm):
    dqs, dks, dvs = [], [], []
    for h in range(MEM_HEADS):
        sl = slice(256 * h, 256 * (h + 1))
        p = _mem_probs(q[:, sl], km[:, sl])
        dvs.append(_dot(p, do[:, sl], TN))
        dp = _dot(do[:, sl], vm[:, sl], NT)
        ds = p * (dp - jnp.sum(dp * p, axis=1, keepdims=True)) * MEM_SCALE
        dqs.append(_dot(ds, km[:, sl]))
        dks.append(_dot(ds, q[:, sl], TN))
    return jnp.concatenate(dqs, axis=1), jnp.concatenate(dks, axis=1), jnp.concatenate(dvs, axis=1)


def _fn_res_ln(ctx, h, r, g, b):
    xhat, _ = _ln_stats(ALPHA * h + r)
    return (xhat * g + b,)


def _fn_res_ln_bwd(ctx, h, r, d1, d2, g):
    xhat, rstd = _ln_stats(ALPHA * h + r)
    return _ln_bwd(xhat, rstd, ALPHA * d1 + d2, g)


def _fn_res2_ln(ctx, h, r1, r2, g, b):
    xhat, _ = _ln_stats(ALPHA * h + (r1 + r2))
    return (xhat * g + b,)


def _fn_res2_ln_bwd(ctx, h, r1, r2, d1, d2, g):
    xhat, rstd = _ln_stats(ALPHA * h + (r1 + r2))
    return _ln_bwd(xhat, rstd, ALPHA * d1 + d2, g)


def _fn_in_ln_bwd(ctx, x, d1, d2, g):
    xhat, rstd = _ln_stats(x)
    return _ln_bwd(xhat, rstd, ALPHA * d1 + d2, g)


def _fn_final(ctx, h2, ff, tgt, g, b):
    xhat, rstd = _ln_stats(ALPHA * h2 + ff)
    e = xhat * g + b - tgt
    loss = 0.5 * _sum0(jnp.sum(e * e, axis=1, keepdims=True)) / D_MODEL
    dx, dg, db = _ln_bwd(xhat, rstd, e / D_MODEL, g)
    return dx, dg, db, loss


def _epi_du(da, u):
    return da * 2.0 * jnp.maximum(u, 0.0)


def _relu2(u):
    r = jnp.maximum(u, 0.0)
    return r * r


def _fn_conv_bwd_a(ctx, y, dxs1, dxs2, dbc, dtr, ddt, dtb):
    sig = _sigmoid(y)
    dact = jnp.concatenate([dxs1 + dxs2, dbc], axis=1)
    dyc = dact * (sig * (1.0 + y * (1.0 - sig)))
    ddtr = ddt * _sigmoid(dtr + dtb)
    return dyc, ddtr, _sum0(dyc), _sum0(ddtr)


def _fn_conv_bwd_b(ctx, d, dn, u, up, w8):
    first, last = ctx.i == 0, ctx.i == ctx.n - 1
    du = d * w8[3:4]
    row = lax.broadcasted_iota(jnp.int32, w8.shape, 0)
    dw = jnp.where(row == 3, _sum0(d * u), 0.0)
    for s in (1, 2, 3):
        du = du + _shift_up(d, dn, s, last) * w8[3 - s:4 - s]
        dw = dw + jnp.where(row == 3 - s, _sum0(d * _shift_down(u, up, s, first)), 0.0)
    return du, dw


def _fn_adam(ctx, w, g, m, v):
    m = ADAM_B1 * m + (1.0 - ADAM_B1) * g
    v = ADAM_B2 * v + (1.0 - ADAM_B2) * (g * g)
    m_hat = m / (1.0 - ADAM_B1 ** ADAM_STEP)
    v_hat = v / (1.0 - ADAM_B2 ** ADAM_STEP)
    return -ADAM_LR * (m_hat / (jnp.sqrt(v_hat) + ADAM_EPS) + ADAM_WD * w), m, v


def _fn_add2(ctx, a, b):
    s = a + b
    return s, s


def _fn_add4(ctx, a, r0, r1, r2):
    return (((a + r0.astype(F32)) + r1.astype(F32)) + r2.astype(F32),)


def _z(r, c, dt):
    return jnp.zeros((r, c), dt)


def _pad_w_in(w):
    r, dt = w.shape[0], w.dtype
    return jnp.concatenate([w[:, 512:1536], w[:, 0:512], w[:, 1544:1928], w[:, 1536:1544], _z(r, 120, dt),
                            w[:, 1928:2184], _z(r, 64, dt), w[:, 2184:2216], _z(r, 32, dt), _z(r, 128, dt)], axis=1)


def _unpad_w_in(d):
    return jnp.concatenate([d[:, 1024:1536], d[:, 0:1024], d[:, 1920:1928], d[:, 1536:1920], d[:, 2048:2304],
                            d[:, 2368:2400]], axis=1)


def _pad_heads(w, width):
    r = w.shape[0]
    w3 = w.reshape(r, MLA_HEADS, width)
    return jnp.pad(w3, ((0, 0), (0, 0), (0, 128 - width))).reshape(r, MLA_HEADS * 128)


def _pad_w_kv(w):
    r = w.shape[0]
    w4 = w.reshape(r, MLA_HEADS, 2, 64)
    return jnp.pad(w4, ((0, 0), (0, 0), (0, 0), (0, 64))).reshape(r, MLA_HEADS * 256)


def _unpad_w_kv(d):
    r = d.shape[0]
    return d.reshape(r, MLA_HEADS, 2, 128)[:, :, :, :64].reshape(r, MLA_HEADS * 128)


def _pad_w_mix(w):
    wo = jnp.pad(w[512:1024].reshape(MLA_HEADS, 64, D_MODEL), ((0, 0), (0, 64), (0, 0))).reshape(1024, D_MODEL)
    return jnp.concatenate([wo, w[0:512]], axis=0)


def _unpad_w_mix(d):
    do = d[:1024].reshape(MLA_HEADS, 128, D_MODEL)[:, :64].reshape(512, D_MODEL)
    return jnp.concatenate([d[1024:1536], do], axis=0)


def _row(v, width=None):
    v = v.reshape(1, -1).astype(F32)
    if width is not None and v.shape[1] < width:
        v = jnp.pad(v, ((0, 0), (0, width - v.shape[1])))
    return v


def _old_local_step(x, mem, positions, target, W, P):
    S = x.shape[0]
    tr = ROW_TILE
    w_in_p = _pad_w_in(W["w_in"])
    w_q_p = _pad_heads(W["w_q_up"], MLA_QK)
    w_kv3 = W["w_kv_up"].reshape(MLA_KV_RANK, MLA_HEADS, 128)
    w_k_p = _pad_heads(w_kv3[:, :, :64].reshape(MLA_KV_RANK, 512), 64)
    w_v_p = _pad_heads(w_kv3[:, :, 64:].reshape(MLA_KV_RANK, 512), 64)
    w_v_pt = w_v_p.T
    w_mix_y = W["w_mix_out"][0:512]
    w_mix_o = jnp.pad(W["w_mix_out"][512:1024].reshape(MLA_HEADS, 64, D_MODEL),
                      ((0, 0), (0, 64), (0, 0))).reshape(MLA_HEADS * 128, D_MODEL)
    conv_w8 = jnp.pad(P["conv_w"].astype(F32), ((0, 4), (0, 0)))
    conv_b = _row(P["conv_b"])
    dt_b = _row(P["dt_bias"], 128)
    a_head = -jnp.exp(P["a_log"].reshape(-1).astype(F32))
    a_row = _row(a_head, 128)
    dexp = jnp.repeat(P["d_skip"].reshape(-1).astype(F32), 64).reshape(1, 512)
    g_ssd, g_q, g_kv = _row(P["ssd_norm_g"]), _row(P["q_norm_g"]), _row(P["kv_norm_g"])
    g_in, b_in = _row(P["ln_in_g"]), _row(P["ln_in_b"])
    g1, b1, g2, b2, g3, b3 = (_row(P[k]) for k in ("ln1_g", "ln1_b", "ln2_g", "ln2_b", "ln3_g", "ln3_b"))

    half = MLA_ROPE // 2
    inv_freq = jnp.power(ROPE_THETA, -jnp.arange(half, dtype=F32) / half)
    ang = positions.reshape(S, 1).astype(F32) * inv_freq
    cos, sin = jnp.cos(ang), jnp.sin(ang)
    zc = lambda n: jnp.zeros((S, n), F32)
    rope_a = jnp.concatenate([jnp.ones((S, 64), F32), cos, cos, zc(32)], axis=1)
    rope_b = jnp.concatenate([zc(80), sin, zc(32)], axis=1)
    rope_c = jnp.concatenate([zc(64), -sin, zc(48)], axis=1)

    (h0,) = _rowwise(_fn_ln, [x], [g_in, b_in], [D_MODEL], tr=tr, name="ln_in")
    proj = _mm(h0, w_in_p, form="nn", name="mm_in")
    conv_y, xbc, dt = _rowwise(
        _fn_conv_fwd, [(proj,) + SEG_XBC, ("prev", proj) + SEG_XBC, (proj,) + SEG_DT], [conv_w8, conv_b, dt_b],
        [1024, 1024, 128], tr=tr, name="conv_fwd")
    y_ssd, hs = _ssd_fwd(xbc, dt, a_row, name="ssd_fwd")
    (y_n,) = _rowwise(_fn_ssd_post, [y_ssd, (xbc, 0, 512), (proj,) + SEG_Z], [dexp, g_ssd], [(512, BF16)], tr=tr,
                      name="ssd_post")
    q_n, kv_n = _rowwise(_fn_mla_pre, [(proj,) + SEG_QLAT, (proj,) + SEG_KVLAT], [g_q, g_kv], [384, 256], tr=tr,
                         name="mla_pre")
    qp = _mm(q_n, w_q_p, form="nn", name="mm_q_up")
    kn = _mm(kv_n, w_k_p, form="nn", name="mm_k_up")
    v_nat = _mm(kv_n, w_v_p, form="nn", out_dtype=BF16, name="mm_v_up")
    v_t = _mm(w_v_pt, kv_n, form="nt", out_dtype=BF16, name="mm_v_up_t")
    q_rot, k_full = _rowwise(_fn_rope, [qp, kn, (proj,) + SEG_KR, rope_a, rope_b, rope_c], [],
                             [(1024, BF16), (1024, BF16)], tr=tr, name="rope")
    o_t, lse = _attn_fwd(q_rot, k_full, v_t, name="attn_fwd")
    mix_o = _mm(o_t, w_mix_o, form="tn", name="mm_mix_o")
    mix_y = _mm(y_n, w_mix_y, form="nn", name="mm_mix_y")
    (h1,) = _rowwise(_fn_res2_ln, [h0, mix_o, mix_y], [g1, b1], [D_MODEL], tr=tr, name="ln1")
    qm = _mm(h1, W["w_mem_q"], form="nn", name="mm_mem_q")
    km = _mm(mem, W["w_mem_k"], form="nn", name="mm_mem_k")
    vm = _mm(mem, W["w_mem_v"], form="nn", name="mm_mem_v")
    (om,) = _rowwise(_fn_mem_fwd, [qm], [km, vm], [(D_MODEL, BF16)], tr=tr, name="mem_fwd")
    xa = _mm(om, W["w_mem_o"], form="nn", name="mm_mem_o")
    (h2,) = _rowwise(_fn_res_ln, [h1, xa], [g2, b2], [D_MODEL], tr=tr, name="ln2")
    u = _mm(h2, W["w_up"], form="nn", name="mm_up")
    ff = _mm(u, W["w_down"], form="nn", a_pro=_relu2, name="mm_down")

    dt3, dg3, db3, loss = _rowwise(_fn_final, [h2, ff, target], [g3, b3], [D_MODEL],
                                   [(1, D_MODEL), (1, D_MODEL), (1, 128)], tr=tr, name="ln3_loss")
    da = _mm(dt3, W["w_down"], form="nt", name="mm_down_dx")
    dw_down = _mm(u, dt3, form="tn", a_pro=_relu2, name="mm_down_dw")
    dw_up = _mm(h2, du, form="tn", name="mm_up_dw")
    dh2 = _mm(du, W["w_up"], form="nt", name="mm_up_dx")
    dt2, dg2, db2 = _rowwise(_fn_res_ln_bwd, [h1, xa, dt3, dh2], [g2], [D_MODEL], [(1, D_MODEL)] * 2, tr=tr,
                             name="ln2_bwd")
    dom = _mm(dt2, W["w_mem_o"], form="nt", name="mm_mem_o_dx")
    dw_mem_o = _mm(om, dt2, form="tn", name="mm_mem_o_dw")
    dqm, dkm, dvm = _rowwise(_fn_mem_bwd, [qm, dom], [km, vm], [(D_MODEL, BF16)], [(256, D_MODEL)] * 2, tr=tr,
                             name="mem_bwd")
    dw_mem_q = _mm(h1, dqm, form="tn", name="mm_mem_q_dw")
    dw_mem_k = _mm(mem, dkm, form="tn", name="mm_mem_k_dw")
    dw_mem_v = _mm(mem, dvm, form="tn", name="mm_mem_v_dw")
    dh1 = _mm(dqm, W["w_mem_q"], form="nt", name="mm_mem_q_dx")
    dt1, dg1, db1 = _rowwise(_fn_res2_ln_bwd, [h0, mix_o, mix_y, dt2, dh1], [g1], [D_MODEL], [(1, D_MODEL)] * 2,
                             tr=tr, name="ln1_bwd")
    do_t = _mm(w_mix_o, dt1, form="nt", name="mm_mix_o_dx")
    dy_n = _mm(dt1, w_mix_y, form="nt", name="mm_mix_y_dx")
    dw_mix_o = _mm(o_t, dt1, form="nn", name="mm_mix_o_dw")
    dw_mix_y = _mm(y_n, dt1, form="tn", name="mm_mix_y_dw")
    dq_t, delta = _attn_bwd_dq(q_rot, k_full, k_full.T, v_nat, o_t, do_t, lse, name="attn_bwd_dq")
    dk_t, dv_t = _attn_bwd_dkv(q_rot, q_rot.T, k_full, v_nat, do_t, lse, delta, name="attn_bwd_dkv")
    dk = dk_t.T
    dqp, dkr = _rowwise(_fn_rope_bwd, [dq_t.T, dk, rope_a, rope_b, rope_c], [], [(1024, BF16), (128, BF16)], tr=tr,
                        name="rope_bwd")
    dw_q_p = _mm(q_n, dqp, form="tn", name="mm_q_up_dw")
    dq_n = _mm(dqp, w_q_p, form="nt", name="mm_q_up_dx")
    dw_k_p = _mm(kv_n, dk, form="tn", name="mm_k_up_dw")
    dkv_n1 = _mm(dk, w_k_p, form="nt", name="mm_k_up_dx")
    dw_v_pt = _mm(dv_t, kv_n, form="nn", name="mm_v_up_dw")
    dkv_n2 = _mm(dv_t, w_v_pt, form="tn", name="mm_v_up_dx")
    dq_lat, dkv_lat, dg_q, dg_kv = _rowwise(
        _fn_mla_pre_bwd, [(proj,) + SEG_QLAT, (proj,) + SEG_KVLAT, dq_n, dkv_n1, dkv_n2], [g_q, g_kv], [(384, BF16), (256, BF16)],
        [(1, 384), (1, 256)], tr=tr, name="mla_pre_bwd")
    dy_ssd, dz, dxs_skip, ddexp, dg_ssd = _rowwise(
        _fn_ssd_post_bwd, [dy_n, y_ssd, (xbc, 0, 512), (proj,) + SEG_Z], [dexp, g_ssd],
        [512, (512, BF16), 512], [(1, 512)] * 2, tr=tr, name="ssd_post_bwd")
    dxs, dbc, ddt, da_head = _ssd_bwd(xbc, dt, a_row, hs, dy_ssd, name="ssd_bwd")
    dyc, ddtr, dconv_b, ddt_b = _rowwise(
        _fn_conv_bwd_a, [conv_y, dxs, dxs_skip, dbc, (proj,) + SEG_DT, ddt], [dt_b], [1024, (128, BF16)],
        [(1, 1024), (1, 128)], tr=tr, name="conv_bwd_a")
    dxbc, dconv_w8 = _rowwise(
        _fn_conv_bwd_b, [dyc, ("next", dyc, 0, 1024), (proj,) + SEG_XBC, ("prev", proj) + SEG_XBC], [conv_w8], [(1024, BF16)],
        [(8, 1024)], tr=tr, name="conv_bwd_b")
    dproj = jnp.concatenate([dxbc, dz, dq_lat, ddtr, dkv_lat, dkr, jnp.zeros((S, 128), BF16)], axis=1)
    dw_in_p = _mm(h0, dproj, form="tn", name="mm_in_dw")
    dh0 = _mm(dproj, w_in_p, form="nt", name="mm_in_dx")
    grad_x, dg_in, db_in = _rowwise(_fn_in_ln_bwd, [x, dt1, dh0], [g_in], [D_MODEL], [(1, D_MODEL)] * 2, tr=tr,
                                    name="ln_in_bwd")

    big = {
        "w_in": _unpad_w_in(dw_in_p),
        "w_q_up": dw_q_p.reshape(384, MLA_HEADS, 128)[:, :, :MLA_QK].reshape(384, MLA_HEADS * MLA_QK),
        "w_kv_up": jnp.concatenate([dw_k_p.reshape(MLA_KV_RANK, MLA_HEADS, 128)[:, :, :64],
                                    dw_v_pt.T.reshape(MLA_KV_RANK, MLA_HEADS, 128)[:, :, :64]], axis=2).reshape(
                                        MLA_KV_RANK, MLA_HEADS * 128),
        "w_mix_out": jnp.concatenate([dw_mix_y, dw_mix_o.reshape(MLA_HEADS, 128, D_MODEL)[:, :64].reshape(
            512, D_MODEL)], axis=0),
        "w_mem_q": dw_mem_q, "w_mem_k": dw_mem_k, "w_mem_v": dw_mem_v, "w_mem_o": dw_mem_o,
        "w_up": dw_up, "w_down": dw_down,
        "conv_w": dconv_w8[0:4],
    }
    small = {
        "ln_in_g": dg_in, "ln_in_b": db_in, "conv_b": dconv_b, "dt_bias": ddt_b[:, :8],
        "a_log": da_head[:, :8] * a_head.reshape(1, 8),
        "d_skip": ddexp.reshape(8, 64).sum(axis=1).reshape(1, 8),
        "ssd_norm_g": dg_ssd, "q_norm_g": dg_q, "kv_norm_g": dg_kv,
        "ln1_g": dg1, "ln1_b": db1, "ln2_g": dg2, "ln2_b": db2, "ln3_g": dg3, "ln3_b": db3,
    }
    return loss[0, 0], grad_x, big, small


BIG = {
    "w_in": (1024, 2216, 1), "w_q_up": (384, 768, 1), "w_kv_up": (256, 1024, 1), "w_mix_out": (1024, 1024, 0),
    "w_mem_q": (1024, 1024, 0), "w_mem_k": (1024, 1024, 0), "w_mem_v": (1024, 1024, 0), "w_mem_o": (1024, 1024, 0),
    "w_up": (1024, 4096, 1), "w_down": (4096, 1024, 0), "conv_w": (4, 1024, 1),
}
BIG_ORDER = list(BIG)
SMALL_ORDER = ["ln_in_g", "ln_in_b", "conv_b", "dt_bias", "a_log", "d_skip", "ssd_norm_g", "q_norm_g", "kv_norm_g",
               "ln1_g", "ln1_b", "ln2_g", "ln2_b", "ln3_g", "ln3_b"]
N_SHARD = 4
PACK_COLS = 1024
PACK_ROWS = 4032
HALF_ROWS = PACK_ROWS // 2
GATHER_CHUNKS = 3
CHIP_CHUNKS = 3
PAIR_CHUNKS = 4


def _shard_shape(name):
    r, c, ax = BIG[name]
    return (r // N_SHARD, c) if ax == 0 else (r, c // N_SHARD)


def _split_shards(name, full):
    r, c, ax = BIG[name]
    if ax == 0:
        return full.reshape(N_SHARD, -1)
    return full.reshape(r, N_SHARD, c // N_SHARD).transpose(1, 0, 2).reshape(N_SHARD, -1)


def _join_shards(name, parts):
    r, c, ax = BIG[name]
    if ax == 0:
        return parts.reshape(r, c)
    return parts.reshape(N_SHARD, r, c // N_SHARD).transpose(1, 0, 2).reshape(r, c)


HBM = pl.BlockSpec(memory_space=pl.ANY)


def _place():
    x, y, c = lax.axis_index("x"), lax.axis_index("y"), lax.axis_index("c")
    chips = [(1 - x, y), (x, 1 - y), (1 - x, 1 - y)]
    return x, y, c, chips


def _gather_weights(wp):
    R, C = wp.shape
    H = R // 2
    nq = GATHER_CHUNKS
    CH = H // nq

    def body(w_ref, out_ref, send_sems, recv_sems):
        x, y, c, chips = _place()
        sib = (x, y, 1 - c)

        def piece(k, hc, q):
            return out_ref.at[k, pl.ds(hc * H + q * CH, CH), :]

        def copy(j, src, dst, to):
            return pltpu.make_async_remote_copy(src_ref=src, dst_ref=dst, send_sem=send_sems.at[j],
                                                recv_sem=recv_sems.at[j], device_id=to, device_id_type=MESH)

        me = 2 * x + y
        sends = []
        for q in range(nq):
            for j, (px, py) in enumerate(chips):
                cp = copy(j * nq + q, w_ref.at[pl.ds(c * H + q * CH, CH), :], piece(me, c, q), (px, py, c))
                cp.start()
                sends.append(cp)
        fwds = []
        for q in range(nq):
            for j, (px, py) in enumerate(chips):
                k = 2 * px + py
                copy(j * nq + q, piece(k, c, q), piece(k, c, q), (px, py, c)).wait_recv()
                f = copy((3 + j) * nq + q, piece(k, c, q), piece(k, c, q), sib)
                f.start()
                fwds.append(f)
        for q in range(nq):
            for j, (px, py) in enumerate(chips):
                k = 2 * px + py
                copy((3 + j) * nq + q, piece(k, 1 - c, q), piece(k, 1 - c, q), sib).wait_recv()
        for cp in sends + fwds:
            cp.wait_send()

    out = pl.pallas_call(
        body, name="gather_weights", in_specs=[HBM], out_specs=HBM,
        out_shape=jax.ShapeDtypeStruct((N_SHARD, R, C), wp.dtype),
        scratch_shapes=[pltpu.SemaphoreType.DMA((6 * nq,)), pltpu.SemaphoreType.DMA((6 * nq,))],
    )(wp)
    me = 2 * lax.axis_index("x") + lax.axis_index("y")
    return lax.dynamic_update_slice(out, wp[None], (me, 0, 0))


def _pair_exchange(gp):
    n, R, C = gp.shape
    H = R // 2
    nq = PAIR_CHUNKS
    CH = H // nq

    def body(g_ref, theirs_ref, send_sems, recv_sems):
        x, y, c, _ = _place()
        swaps = []
        for k in range(n):
            for q in range(nq):
                cp = pltpu.make_async_remote_copy(
                    src_ref=g_ref.at[k, pl.ds((1 - c) * H + q * CH, CH), :], dst_ref=theirs_ref.at[k, pl.ds(q * CH, CH), :],
                    send_sem=send_sems.at[k * nq + q], recv_sem=recv_sems.at[k * nq + q], device_id=(x, y, 1 - c),
                    device_id_type=MESH)
                cp.start()
                swaps.append(cp)
        for cp in swaps:
            cp.wait()

    theirs = pl.pallas_call(
        body, name="pair_exchange", in_specs=[HBM], out_specs=HBM,
        out_shape=jax.ShapeDtypeStruct((n, H, C), gp.dtype),
        scratch_shapes=[pltpu.SemaphoreType.DMA((n * nq,)), pltpu.SemaphoreType.DMA((n * nq,))],
    )(gp)
    mine = lax.dynamic_slice(gp, (0, lax.axis_index("c") * H, 0), (n, H, C))
    return mine, theirs


def _chip_exchange(pb):
    n, H, C = pb.shape
    nq = CHIP_CHUNKS
    CH = H // nq

    def body(pb_ref, got_ref, send_sems, recv_sems):
        x, y, c, chips = _place()
        sends = []
        for q in range(nq):
            for j, (px, py) in enumerate(chips):
                cp = pltpu.make_async_remote_copy(
                    src_ref=pb_ref.at[2 * px + py, pl.ds(q * CH, CH), :], dst_ref=got_ref.at[j, pl.ds(q * CH, CH), :],
                    send_sem=send_sems.at[j * nq + q], recv_sem=recv_sems.at[j * nq + q],
                    device_id=(px, py, c), device_id_type=MESH)
                cp.start()
                sends.append(cp)
        for cp in sends:
            cp.wait()

    return pl.pallas_call(
        body, name="chip_exchange", in_specs=[HBM], out_specs=HBM,
        out_shape=jax.ShapeDtypeStruct((3, H, C), BF16),
        scratch_shapes=[pltpu.SemaphoreType.DMA((3 * nq,)), pltpu.SemaphoreType.DMA((3 * nq,))],
    )(pb)


def _pair_join(q):
    H, C = q.shape
    nq = PAIR_CHUNKS
    CH = H // nq

    def body(q_ref, theirs_ref, send_sems, recv_sems):
        x, y, c, _ = _place()
        pushes = []
        for j in range(nq):
            cp = pltpu.make_async_remote_copy(
                src_ref=q_ref.at[pl.ds(j * CH, CH), :], dst_ref=theirs_ref.at[pl.ds(j * CH, CH), :],
                send_sem=send_sems.at[j], recv_sem=recv_sems.at[j], device_id=(x, y, 1 - c), device_id_type=MESH)
            cp.start()
            pushes.append(cp)
        for cp in pushes:
            cp.wait()

    theirs = pl.pallas_call(
        body, name="pair_join", in_specs=[HBM], out_specs=HBM,
        out_shape=jax.ShapeDtypeStruct((H, C), F32),
        scratch_shapes=[pltpu.SemaphoreType.DMA((nq,)), pltpu.SemaphoreType.DMA((nq,))],
    )(q)
    c = lax.axis_index("c")
    out = jnp.zeros((2 * H, C), F32)
    out = lax.dynamic_update_slice(out, q, (c * H, 0))
    return lax.dynamic_update_slice(out, theirs, ((1 - c) * H, 0))


N_DEV = 8


def _small_all_reduce(g):
    r, cdim = g.shape

    def body(g_ref, out_ref, buf, send_sems, recv_sems):
        x, y, c, _ = _place()
        me = 4 * x + 2 * y + c
        buf[me] = g_ref[...]
        copies = []
        for d in range(1, N_DEV):
            to = me ^ d
            cp = pltpu.make_async_remote_copy(src_ref=g_ref, dst_ref=buf.at[me], send_sem=send_sems.at[d - 1],
                                              recv_sem=recv_sems.at[d - 1],
                                              device_id=(to // 4, (to // 2) % 2, to % 2), device_id_type=MESH)
            cp.start()
            copies.append(cp)
        for cp in copies:
            cp.wait()
        acc = buf[0]
        for d in range(1, N_DEV):
            acc = acc + buf[d]
        out_ref[...] = acc

    return pl.pallas_call(
        body, name="small_all_reduce",
        in_specs=[pl.BlockSpec(memory_space=pltpu.VMEM)], out_specs=pl.BlockSpec(memory_space=pltpu.VMEM),
        out_shape=jax.ShapeDtypeStruct((r, cdim), F32),
        scratch_shapes=[pltpu.VMEM((N_DEV, r, cdim), F32), pltpu.SemaphoreType.DMA((N_DEV - 1,)),
                        pltpu.SemaphoreType.DMA((N_DEV - 1,))],
    )(g)


def _adam(w, g, m, v, name):
    shape = w.shape
    w2, g2, m2, v2 = (t.reshape(-1, shape[-1]) for t in (w, g, m, v))
    d, mn, vn = _rowwise(_fn_adam, [w2, g2, m2, v2], [], [shape[-1]] * 3, tr=256, name=name)
    return d.reshape(shape), mn.reshape(shape), vn.reshape(shape)


def _old_kernel(x, mem, positions, ln_in_g, ln_in_b, w_in, conv_w, conv_b, dt_bias, a_log, d_skip, ssd_norm_g, q_norm_g, w_q_up, kv_norm_g, w_kv_up, w_mix_out, ln1_g, ln1_b, w_mem_q, w_mem_k, w_mem_v, w_mem_o, ln2_g, ln2_b, w_up, w_down, ln3_g, ln3_b, loss_target, m_ln_in_g, m_ln_in_b, m_w_in, m_conv_w, m_conv_b, m_dt_bias, m_a_log, m_d_skip, m_ssd_norm_g, m_q_norm_g, m_w_q_up, m_kv_norm_g, m_w_kv_up, m_w_mix_out, m_ln1_g, m_ln1_b, m_w_mem_q, m_w_mem_k, m_w_mem_v, m_w_mem_o, m_ln2_g, m_ln2_b, m_w_up, m_w_down, m_ln3_g, m_ln3_b, v_ln_in_g, v_ln_in_b, v_w_in, v_conv_w, v_conv_b, v_dt_bias, v_a_log, v_d_skip, v_ssd_norm_g, v_q_norm_g, v_w_q_up, v_kv_norm_g, v_w_kv_up, v_w_mix_out, v_ln1_g, v_ln1_b, v_w_mem_q, v_w_mem_k, v_w_mem_v, v_w_mem_o, v_ln2_g, v_ln2_b, v_w_up, v_w_down, v_ln3_g, v_ln3_b):
    args = dict(locals())
    weights = BIG_ORDER + SMALL_ORDER

    flat = []
    for n in BIG_ORDER:
        s = args[n].reshape(-1)
        if n == "conv_w":
            flat.append(lax.bitcast_convert_type(s.astype(F32), BF16).reshape(-1))
        else:
            flat.append(s.astype(BF16))
    flat = jnp.concatenate(flat)
    wp = jnp.pad(flat, (0, PACK_ROWS * PACK_COLS - flat.shape[0])).reshape(PACK_ROWS, PACK_COLS)
    gathered = _gather_weights(wp).reshape(N_SHARD, -1)
    W, off = {}, 0
    for n in BIG_ORDER:
        sr, sc = _shard_shape(n)
        cnt = sr * sc
        if n == "conv_w":
            part = lax.bitcast_convert_type(gathered[:, off:off + 2 * cnt].reshape(N_SHARD, cnt, 2), F32)
            off += 2 * cnt
        else:
            part = gathered[:, off:off + cnt]
            off += cnt
        W[n] = _join_shards(n, part)
    P = {n: args[n] for n in SMALL_ORDER}
    P["conv_w"] = W.pop("conv_w")

    loss, grad_x, gbig, gsmall = _local_step(x[0], mem[0], positions[0], loss_target[0], W, P)
    loss = lax.psum(loss, ("x", "y", "c"))

    gflat = jnp.concatenate([_split_shards(n, gbig[n]) for n in BIG_ORDER], axis=1)
    gp = jnp.pad(gflat, ((0, 0), (0, PACK_ROWS * PACK_COLS - gflat.shape[1]))).reshape(N_SHARD, PACK_ROWS, PACK_COLS)
    mine, theirs = _pair_exchange(gp)
    pf, pb = _rowwise(_fn_add2, [mine.reshape(-1, PACK_COLS), theirs.reshape(-1, PACK_COLS)], [],
                      [PACK_COLS, (PACK_COLS, BF16)], tr=288, name="pair_sum")
    pf = pf.reshape(N_SHARD, HALF_ROWS, PACK_COLS)
    pb = pb.reshape(N_SHARD, HALF_ROWS, PACK_COLS)
    got = _chip_exchange(pb).reshape(3 * HALF_ROWS, PACK_COLS)
    own = lax.dynamic_index_in_dim(pf, 2 * lax.axis_index("x") + lax.axis_index("y"), axis=0, keepdims=False)
    (q,) = _rowwise(_fn_add4, [own] + [(got, 0, PACK_COLS, j * HALF_ROWS) for j in range(3)], [], [PACK_COLS],
                    tr=288, name="chip_sum", n_rows=HALF_ROWS)
    red = _pair_join(q).reshape(-1)

    gs = jnp.concatenate([_row(gsmall[n], PACK_COLS) for n in SMALL_ORDER] + [jnp.zeros((1, PACK_COLS), F32)], axis=0)
    gs = _small_all_reduce(gs)

    grads, deltas, new_m, new_v = {}, {}, {}, {}
    off = 0
    for n in BIG_ORDER:
        sr, sc = _shard_shape(n)
        g = red[off:off + sr * sc].reshape(args[n].shape)
        off += sr * sc
        grads[n] = g
        deltas[n], new_m[n], new_v[n] = _adam(args[n], g, args["m_" + n], args["v_" + n], "adam_" + n)
    pack = lambda pre: jnp.concatenate([_row(args[pre + n], PACK_COLS) for n in SMALL_ORDER]
                                       + [jnp.zeros((1, PACK_COLS), F32)], axis=0)
    ds, ms, vs = _rowwise(_fn_adam, [pack(""), gs, pack("m_"), pack("v_")], [], [PACK_COLS] * 3, tr=16,
                          name="adam_small")
    for i, n in enumerate(SMALL_ORDER):
        cnt = args[n].size
        take = lambda t: t[i, :cnt].reshape(args[n].shape)
        grads[n], deltas[n], new_m[n], new_v[n] = take(gs), take(ds), take(ms), take(vs)

    order = ["ln_in_g", "ln_in_b", "w_in", "conv_w", "conv_b", "dt_bias", "a_log", "d_skip", "ssd_norm_g",
             "q_norm_g", "w_q_up", "kv_norm_g", "w_kv_up", "w_mix_out", "ln1_g", "ln1_b", "w_mem_q", "w_mem_k",
             "w_mem_v", "w_mem_o", "ln2_g", "ln2_b", "w_up", "w_down", "ln3_g", "ln3_b"]
    assert sorted(order) == sorted(weights)
    return (loss, grad_x[None], *[grads[n] for n in order], *[deltas[n] for n in order],
            *[new_m[n] for n in order], *[new_v[n] for n in order])


PACK_A_ROW = {"w_down": 0, "w_up": 1024, "w_mem_q": 2048, "w_mem_k": 2304, "w_mem_v": 2560, "w_mem_o": 2816,
              "w_mix_out": 3072}
PACK_A_ORDER = list(PACK_A_ROW)
PACK_A_ROWS = 3328
PACK_B_ORDER = ["w_in", "w_q_up", "w_kv_up", "conv_w"]
PACK_B_ROWS = 704


def _mesh_pos():
    return 2 * lax.axis_index("x") + lax.axis_index("y"), lax.axis_index("c")


def _local_step(x, mem, positions, target, WB, P, *, wp_a=None, g_a=None):
    S = x.shape[0]
    tr = ROW_TILE
    dist = g_a is None
    w_in_p = _pad_w_in(WB["w_in"])
    w_q_p = _pad_heads(WB["w_q_up"], MLA_QK)
    w_kv3 = WB["w_kv_up"].reshape(MLA_KV_RANK, MLA_HEADS, 128)
    w_k_p = _pad_heads(w_kv3[:, :, :64].reshape(MLA_KV_RANK, 512), 64)
    w_v_p = _pad_heads(w_kv3[:, :, 64:].reshape(MLA_KV_RANK, 512), 64)
    w_v_pt = w_v_p.T
    conv_w8 = jnp.pad(P["conv_w"].astype(F32), ((0, 4), (0, 0)))
    conv_b = _row(P["conv_b"])
    dt_b = _row(P["dt_bias"], 128)
    a_head = -jnp.exp(P["a_log"].reshape(-1).astype(F32))
    a_row = _row(a_head, 128)
    dexp = jnp.repeat(P["d_skip"].reshape(-1).astype(F32), 64).reshape(1, 512)
    g_ssd, g_q, g_kv = _row(P["ssd_norm_g"]), _row(P["q_norm_g"]), _row(P["kv_norm_g"])
    g_in, b_in = _row(P["ln_in_g"]), _row(P["ln_in_b"])
    g1, b1, g2, b2, g3, b3 = (_row(P[k]) for k in ("ln1_g", "ln1_b", "ln2_g", "ln2_b", "ln3_g", "ln3_b"))

    half = MLA_ROPE // 2
    inv_freq = jnp.power(ROPE_THETA, -jnp.arange(half, dtype=F32) / half)
    ang = positions.reshape(S, 1).astype(F32) * inv_freq
    cos, sin = jnp.cos(ang), jnp.sin(ang)
    zc = lambda n: jnp.zeros((S, n), F32)
    rope_a = jnp.concatenate([jnp.ones((S, 64), F32), cos, cos, zc(32)], axis=1)
    rope_b = jnp.concatenate([zc(80), sin, zc(32)], axis=1)
    rope_c = jnp.concatenate([zc(64), -sin, zc(48)], axis=1)

    (h0,) = _rowwise(_fn_ln, [x], [g_in, b_in], [D_MODEL], tr=tr, name="ln_in")
    proj = _mm(h0, w_in_p, form="nn", name="mm_in")
    conv_y, xbc, dt = _rowwise(
        _fn_conv_fwd, [(proj,) + SEG_XBC, ("prev", proj) + SEG_XBC, (proj,) + SEG_DT], [conv_w8, conv_b, dt_b],
        [1024, 1024, 128], tr=tr, name="conv_fwd")
    y_ssd, hs = _ssd_fwd(xbc, dt, a_row, name="ssd_fwd")
    (y_n,) = _rowwise(_fn_ssd_post, [y_ssd, (xbc, 0, 512), (proj,) + SEG_Z], [dexp, g_ssd], [(512, BF16)], tr=tr,
                      name="ssd_post")
    q_n, kv_n = _rowwise(_fn_mla_pre, [(proj,) + SEG_QLAT, (proj,) + SEG_KVLAT], [g_q, g_kv], [384, 256], tr=tr,
                         name="mla_pre")
    qp = _mm(q_n, w_q_p, form="nn", name="mm_q_up")
    kn = _mm(kv_n, w_k_p, form="nn", name="mm_k_up")
    v_nat = _mm(kv_n, w_v_p, form="nn", out_dtype=BF16, name="mm_v_up")
    v_t = _mm(w_v_pt, kv_n, form="nt", out_dtype=BF16, name="mm_v_up_t")
    q_rot, k_full = _rowwise(_fn_rope, [qp, kn, (proj,) + SEG_KR, rope_a, rope_b, rope_c], [],
                             [(1024, BF16), (1024, BF16)], tr=tr, name="rope")
    res = _attn_fwd(q_rot, k_full, v_t, name="attn_fwd", hosted=_gather_step(wp_a) if dist else None)
    o_t, lse = res[0], res[1]
    if dist:
        g_a = lax.dynamic_update_slice(res[2], wp_a[None], (_mesh_pos()[0], 0, 0))
    r_mix = PACK_A_ROW["w_mix_out"]
    w_mix_o = jnp.pad(g_a[2:4, r_mix:r_mix + 256].reshape(MLA_HEADS, 64, D_MODEL),
                      ((0, 0), (0, 64), (0, 0))).reshape(MLA_HEADS * 128, D_MODEL)
    mix_o = _mm(o_t, w_mix_o, form="tn", name="mm_mix_o")
    mix_y = _mm(y_n, g_a, form="nn", b_pack="w_mix_out", name="mm_mix_y")
    (h1,) = _rowwise(_fn_res2_ln, [h0, mix_o, mix_y], [g1, b1], [D_MODEL], tr=tr, name="ln1")
    qm = _mm(h1, g_a, form="nn", b_pack="w_mem_q", out_dtype=BF16, name="mm_mem_q")
    km = _mm(mem, g_a, form="nn", b_pack="w_mem_k", out_dtype=BF16, name="mm_mem_k")
    vm = _mm(mem, g_a, form="nn", b_pack="w_mem_v", out_dtype=BF16, name="mm_mem_v")
    (om,) = _rowwise(_fn_mem_fwd, [qm], [km, vm], [(D_MODEL, BF16)], tr=tr, name="mem_fwd")
    xa = _mm(om, g_a, form="nn", b_pack="w_mem_o", name="mm_mem_o")
    (h2,) = _rowwise(_fn_res_ln, [h1, xa], [g2, b2], [D_MODEL], tr=tr, name="ln2")
    u = _mm(h2, g_a, form="nn", b_pack="w_up", name="mm_up")
    ff = _mm(u, g_a, form="nn", a_pro=_relu2, b_pack="w_down", name="mm_down")

    gp = lax.empty((N_SHARD, PACK_A_ROWS, PACK_COLS), F32)
    dt3, dg3, db3, loss = _rowwise(_fn_final, [h2, ff, target], [g3, b3], [D_MODEL],
                                   [(1, D_MODEL), (1, D_MODEL), (1, 128)], tr=tr, name="ln3_loss")
    du = _mm(dt3, g_a, form="nt", b_pack="w_down", epi=(_epi_du, u), out_dtype=BF16, name="mm_down_dx")
    gp = _mm(u, dt3, form="tn", a_pro=_relu2, out_pack=("w_down", gp), name="mm_down_dw")
    gp = _mm(h2, du, form="tn", out_pack=("w_up", gp), name="mm_up_dw")
    dh2 = _mm(du, g_a, form="nt", b_pack="w_up", name="mm_up_dx")
    dt2, dg2, db2 = _rowwise(_fn_res_ln_bwd, [h1, xa, dt3, dh2], [g2], [D_MODEL], [(1, D_MODEL)] * 2, tr=tr,
                             name="ln2_bwd")
    dom = _mm(dt2, g_a, form="nt", b_pack="w_mem_o", out_dtype=BF16, name="mm_mem_o_dx")
    gp = _mm(om, dt2, form="tn", out_pack=("w_mem_o", gp), name="mm_mem_o_dw")
    dqm, dkm, dvm = _rowwise(_fn_mem_bwd, [qm, dom], [km, vm], [(D_MODEL, BF16)], [(256, D_MODEL)] * 2, tr=tr,
                             name="mem_bwd")
    gp = _mm(h1, dqm, form="tn", out_pack=("w_mem_q", gp), name="mm_mem_q_dw")
    gp = _mm(mem, dkm, form="tn", out_pack=("w_mem_k", gp), name="mm_mem_k_dw")
    gp = _mm(mem, dvm, form="tn", out_pack=("w_mem_v", gp), name="mm_mem_v_dw")
    dh1 = _mm(dqm, g_a, form="nt", b_pack="w_mem_q", name="mm_mem_q_dx")
    dt1, dg1, db1 = _rowwise(_fn_res2_ln_bwd, [h0, mix_o, mix_y, dt2, dh1], [g1], [D_MODEL], [(1, D_MODEL)] * 2,
                             tr=tr, name="ln1_bwd")
    do_t = _mm(w_mix_o, dt1, form="nt", name="mm_mix_o_dx")
    dy_n = _mm(dt1, g_a, form="nt", b_pack="w_mix_out", b_rows=512, name="mm_mix_y_dx")
    dw_mix_o = _mm(o_t, dt1, form="nn", name="mm_mix_o_dw")
    gp = _mm(y_n, dt1, form="tn", out_pack=("w_mix_out", gp), name="mm_mix_y_dw")
    gp = lax.dynamic_update_slice(
        gp, dw_mix_o.reshape(MLA_HEADS, 128, D_MODEL)[:, :64].reshape(2, 256, D_MODEL), (2, r_mix, 0))
    me, c = _mesh_pos() if dist else (0, 0)
    ha = PACK_A_ROWS // 2
    res = _attn_bwd_dq(q_rot, k_full, k_full.T, v_nat, o_t, do_t, lse, name="attn_bwd_dq",
                       hosted=_pair_exchange_step(gp) if dist else None)
    dq_t, delta = res[0], res[1]
    chip_step = None
    if dist:
        mine = lax.dynamic_slice(gp, (0, c * ha, 0), (N_SHARD, ha, PACK_COLS))
        pf, pb = _rowwise(_fn_add2, [mine.reshape(-1, PACK_COLS), res[2].reshape(-1, PACK_COLS)], [],
                          [PACK_COLS, (PACK_COLS, BF16)], tr=512, name="pair_sum_a")
        chip_step = _chip_exchange_step(pb.reshape(N_SHARD, ha, PACK_COLS))
    res = _attn_bwd_dkv(q_rot, q_rot.T, k_full, v_nat, do_t, lse, delta, name="attn_bwd_dkv", hosted=chip_step)
    dk_t, dv_t = res[0], res[1]
    if dist:
        own = lax.dynamic_index_in_dim(pf.reshape(N_SHARD, ha, PACK_COLS), me, axis=0, keepdims=False)
        got = res[2].reshape(3 * ha, PACK_COLS)
        (gp,) = _rowwise(_fn_add4, [own] + [(got, 0, PACK_COLS, j * ha) for j in range(3)], [], [PACK_COLS],
                         tr=512, name="chip_sum_a", n_rows=ha)
    dk = dk_t.T
    dqp, dkr = _rowwise(_fn_rope_bwd, [dq_t.T, dk, rope_a, rope_b, rope_c], [], [(1024, BF16), (128, BF16)], tr=tr,
                        name="rope_bwd")
    dw_q_p = _mm(q_n, dqp, form="tn", name="mm_q_up_dw")
    dq_n = _mm(dqp, w_q_p, form="nt", name="mm_q_up_dx")
    dw_k_p = _mm(kv_n, dk, form="tn", name="mm_k_up_dw")
    dkv_n1 = _mm(dk, w_k_p, form="nt", name="mm_k_up_dx")
    dw_v_pt = _mm(dv_t, kv_n, form="nn", name="mm_v_up_dw")
    dkv_n2 = _mm(dv_t, w_v_pt, form="tn", name="mm_v_up_dx")
    dq_lat, dkv_lat, dg_q, dg_kv = _rowwise(
        _fn_mla_pre_bwd, [(proj,) + SEG_QLAT, (proj,) + SEG_KVLAT, dq_n, dkv_n1, dkv_n2], [g_q, g_kv], [(384, BF16), (256, BF16)],
        [(1, 384), (1, 256)], tr=tr, name="mla_pre_bwd")
    dy_ssd, dz, dxs_skip, ddexp, dg_ssd = _rowwise(
        _fn_ssd_post_bwd, [dy_n, y_ssd, (xbc, 0, 512), (proj,) + SEG_Z], [dexp, g_ssd],
        [512, (512, BF16), 512], [(1, 512)] * 2, tr=tr, name="ssd_post_bwd")
    dxs, dbc, ddt, da_head = _ssd_bwd(xbc, dt, a_row, hs, dy_ssd, name="ssd_bwd")
    dyc, ddtr, dconv_b, ddt_b = _rowwise(
        _fn_conv_bwd_a, [conv_y, dxs, dxs_skip, dbc, (proj,) + SEG_DT, ddt], [dt_b], [1024, (128, BF16)],
        [(1, 1024), (1, 128)], tr=tr, name="conv_bwd_a")
    dxbc, dconv_w8 = _rowwise(
        _fn_conv_bwd_b, [dyc, ("next", dyc, 0, 1024), (proj,) + SEG_XBC, ("prev", proj) + SEG_XBC], [conv_w8], [(1024, BF16)],
        [(8, 1024)], tr=tr, name="conv_bwd_b")
    dproj = jnp.concatenate([dxbc, dz, dq_lat, ddtr, dkv_lat, dkr, jnp.zeros((S, 128), BF16)], axis=1)
    dw_in_p = _mm(h0, dproj, form="tn", name="mm_in_dw")
    dh0 = _mm(dproj, w_in_p, form="nt", name="mm_in_dx")
    grad_x, dg_in, db_in = _rowwise(_fn_in_ln_bwd, [x, dt1, dh0], [g_in], [D_MODEL], [(1, D_MODEL)] * 2, tr=tr,
                                    name="ln_in_bwd")

    big_b = {
        "w_in": _unpad_w_in(dw_in_p),
        "w_q_up": dw_q_p.reshape(384, MLA_HEADS, 128)[:, :, :MLA_QK].reshape(384, MLA_HEADS * MLA_QK),
        "w_kv_up": jnp.concatenate([dw_k_p.reshape(MLA_KV_RANK, MLA_HEADS, 128)[:, :, :64],
                                    dw_v_pt.T.reshape(MLA_KV_RANK, MLA_HEADS, 128)[:, :, :64]], axis=2).reshape(
                                        MLA_KV_RANK, MLA_HEADS * 128),
        "conv_w": dconv_w8[0:4],
    }
    small = {
        "ln_in_g": dg_in, "ln_in_b": db_in, "conv_b": dconv_b, "dt_bias": ddt_b[:, :8],
        "a_log": da_head[:, :8] * a_head.reshape(1, 8),
        "d_skip": ddexp.reshape(8, 64).sum(axis=1).reshape(1, 8),
        "ssd_norm_g": dg_ssd, "q_norm_g": dg_q, "kv_norm_g": dg_kv,
        "ln1_g": dg1, "ln1_b": db1, "ln2_g": dg2, "ln2_b": db2, "ln3_g": dg3, "ln3_b": db3,
    }
    return loss[0, 0], grad_x, gp, big_b, small


def _reduce_scatter(gp, tag):
    n, R, C = gp.shape
    H = R // 2
    me, c = _mesh_pos()
    (theirs,) = _run_step(_pair_exchange_step(gp), "pair_exchange_" + tag)
    mine = lax.dynamic_slice(gp, (0, c * H, 0), (n, H, C))
    pf, pb = _rowwise(_fn_add2, [mine.reshape(-1, C), theirs.reshape(-1, C)], [], [C, (C, BF16)], tr=512,
                      name="pair_sum_" + tag)
    (got,) = _run_step(_chip_exchange_step(pb.reshape(n, H, C)), "chip_exchange_" + tag)
    own = lax.dynamic_index_in_dim(pf.reshape(n, H, C), me, axis=0, keepdims=False)
    got = got.reshape(3 * H, C)
    (q,) = _rowwise(_fn_add4, [own] + [(got, 0, C, j * H) for j in range(3)], [], [C], tr=512,
                    name="chip_sum_" + tag, n_rows=H)
    return q


def _adam(w, g, m, v, name):
    shape = w.shape
    w2, m2, v2 = (t.reshape(-1, shape[-1]) for t in (w, m, v))
    g2 = (g[0], 0, shape[-1], g[1]) if isinstance(g, tuple) else g.reshape(-1, shape[-1])
    d, mn, vn = _rowwise(_fn_adam, [w2, g2, m2, v2], [], [shape[-1]] * 3, tr=256, name=name)
    return d.reshape(shape), mn.reshape(shape), vn.reshape(shape)


def kernel(x, mem, positions, ln_in_g, ln_in_b, w_in, conv_w, conv_b, dt_bias, a_log, d_skip, ssd_norm_g, q_norm_g, w_q_up, kv_norm_g, w_kv_up, w_mix_out, ln1_g, ln1_b, w_mem_q, w_mem_k, w_mem_v, w_mem_o, ln2_g, ln2_b, w_up, w_down, ln3_g, ln3_b, loss_target, m_ln_in_g, m_ln_in_b, m_w_in, m_conv_w, m_conv_b, m_dt_bias, m_a_log, m_d_skip, m_ssd_norm_g, m_q_norm_g, m_w_q_up, m_kv_norm_g, m_w_kv_up, m_w_mix_out, m_ln1_g, m_ln1_b, m_w_mem_q, m_w_mem_k, m_w_mem_v, m_w_mem_o, m_ln2_g, m_ln2_b, m_w_up, m_w_down, m_ln3_g, m_ln3_b, v_ln_in_g, v_ln_in_b, v_w_in, v_conv_w, v_conv_b, v_dt_bias, v_a_log, v_d_skip, v_ssd_norm_g, v_q_norm_g, v_w_q_up, v_kv_norm_g, v_w_kv_up, v_w_mix_out, v_ln1_g, v_ln1_b, v_w_mem_q, v_w_mem_k, v_w_mem_v, v_w_mem_o, v_ln2_g, v_ln2_b, v_w_up, v_w_down, v_ln3_g, v_ln3_b):
    args = dict(locals())
    me, c = _mesh_pos()

    wp_a = jnp.concatenate([args[n].reshape(-1, PACK_COLS).astype(BF16) for n in PACK_A_ORDER], axis=0)
    flat = [args[n].reshape(-1).astype(BF16) for n in PACK_B_ORDER[:-1]]
    flat.append(lax.bitcast_convert_type(conv_w.reshape(-1), BF16).reshape(-1))
    used = sum(f.shape[0] for f in flat)
    flat.append(jnp.zeros((PACK_B_ROWS * PACK_COLS - used,), BF16))
    wp_b = jnp.concatenate(flat).reshape(PACK_B_ROWS, PACK_COLS)

    (g_b,) = _run_step(_gather_step(wp_b), "gather_b")
    g_b = lax.dynamic_update_slice(g_b, wp_b[None], (me, 0, 0)).reshape(N_SHARD, -1)
    WB, off = {}, 0
    for n in PACK_B_ORDER:
        sr, sc = _shard_shape(n)
        cnt = sr * sc
        if n == "conv_w":
            part = lax.bitcast_convert_type(g_b[:, off:off + 2 * cnt].reshape(N_SHARD, cnt, 2), F32)
            off += 2 * cnt
        else:
            part = g_b[:, off:off + cnt]
            off += cnt
        WB[n] = _join_shards(n, part)
    P = {n: args[n] for n in SMALL_ORDER}
    P["conv_w"] = WB.pop("conv_w")

    loss, grad_x, q_a, gbig_b, gsmall = _local_step(x[0], mem[0], positions[0], loss_target[0], WB, P, wp_a=wp_a)
    loss = lax.psum(loss, ("x", "y", "c"))

    gflat = [_split_shards(n, gbig_b[n]) for n in PACK_B_ORDER]
    used = sum(f.shape[1] for f in gflat)
    gflat.append(jnp.zeros((N_SHARD, PACK_B_ROWS * PACK_COLS - used), F32))
    gp_b = jnp.concatenate(gflat, axis=1).reshape(N_SHARD, PACK_B_ROWS, PACK_COLS)
    q_b = _reduce_scatter(gp_b, "b")
    (t_a,) = _run_step(_pair_join_step(q_a), "pair_join_a")
    (t_b,) = _run_step(_pair_join_step(q_b), "pair_join_b")
    red = jnp.where(c == 0, jnp.concatenate([q_a, t_a, q_b, t_b], axis=0), jnp.concatenate([t_a, q_a, t_b, q_b], axis=0))
    gs = jnp.concatenate([_row(gsmall[n], PACK_COLS) for n in SMALL_ORDER] + [jnp.zeros((1, PACK_COLS), F32)], axis=0)
    gs = _small_all_reduce(gs)

    grads, deltas, new_m, new_v = {}, {}, {}, {}
    for n in PACK_A_ORDER:
        r0, (sr, _) = PACK_A_ROW[n], _shard_shape(n)
        grads[n] = red[r0:r0 + sr].reshape(args[n].shape)
        deltas[n], new_m[n], new_v[n] = _adam(args[n], (red, r0), args["m_" + n], args["v_" + n], "adam_" + n)
    red_b = red[PACK_A_ROWS:].reshape(-1)
    off = 0
    for n in PACK_B_ORDER:
        sr, sc = _shard_shape(n)
        grads[n] = red_b[off:off + sr * sc].reshape(args[n].shape)
        off += sr * sc
        deltas[n], new_m[n], new_v[n] = _adam(args[n], grads[n], args["m_" + n], args["v_" + n], "adam_" + n)
    pack = lambda pre: jnp.concatenate([_row(args[pre + n], PACK_COLS) for n in SMALL_ORDER]
                                       + [jnp.zeros((1, PACK_COLS), F32)], axis=0)
    ds, ms, vs = _rowwise(_fn_adam, [pack(""), gs, pack("m_"), pack("v_")], [], [PACK_COLS] * 3, tr=16,
                          name="adam_small")
    for i, n in enumerate(SMALL_ORDER):
        cnt = args[n].size
        take = lambda t: t[i, :cnt].reshape(args[n].shape)
        grads[n], deltas[n], new_m[n], new_v[n] = take(gs), take(ds), take(ms), take(vs)

    order = ["ln_in_g", "ln_in_b", "w_in", "conv_w", "conv_b", "dt_bias", "a_log", "d_skip", "ssd_norm_g",
             "q_norm_g", "w_q_up", "kv_norm_g", "w_kv_up", "w_mix_out", "ln1_g", "ln1_b", "w_mem_q", "w_mem_k",
             "w_mem_v", "w_mem_o", "ln2_g", "ln2_b", "w_up", "w_down", "ln3_g", "ln3_b"]
    return (loss, grad_x[None], *[grads[n] for n in order], *[deltas[n] for n in order],
            *[new_m[n] for n in order], *[new_v[n] for n in order])
```

```python
import functools
import math

import jax
import jax.numpy as jnp
import numpy as np
from jax import lax
from jax.experimental import pallas as pl
from jax.experimental.pallas import tpu as pltpu

F32 = jnp.float32
BF16 = jnp.bfloat16
MESH = pl.DeviceIdType.MESH

D_MODEL = 1024
SSD_HEADS = 8
SSD_INNER = 512
SSD_CHUNK = 128
SSD_STATE = 128
MLA_HEADS = 8
MLA_NOPE = 64
MLA_ROPE = 32
MLA_QK = 96
MLA_Q_RANK = 384
MLA_KV_RANK = 256
ROPE_THETA = 10000.0
MEM_HEADS = 4
MEM_HEAD_DIM = 256
LN_EPS = 1e-5
RMS_EPS = 1e-6
ALPHA = 2.0 ** 0.25
ADAM_LR = 0.001
ADAM_B1 = 0.9
ADAM_B2 = 0.999
ADAM_EPS = 1e-08
ADAM_WD = 0.01
ADAM_STEP = 10

LANES = 128
IN_W = 2560
SEG_XBC = (0, 1024)
SEG_Z = (1024, 512)
SEG_QLAT = (1536, 384)
SEG_DT = (1920, 128)
SEG_KVLAT = (2048, 256)
SEG_KR = (2304, 128)
VMEM_LIMIT = 56 * 1024 * 1024
ATTN_TILE = 512
ROW_TILE = 256
NEG = -1e30

NN = (((1,), (0,)), ((), ()))
NT = (((1,), (1,)), ((), ()))
TN = (((0,), (0,)), ((), ()))


def _dot(a, b, dims=NN):
    return lax.dot_general(a.astype(BF16), b.astype(BF16), dims, preferred_element_type=F32)


def _dot_exact(a, b):
    return lax.dot_general(a, b, NN, precision=lax.Precision.HIGHEST, preferred_element_type=F32)


def _pick(dim, pref):
    t = min(pref, dim)
    t -= t % LANES
    while t >= LANES:
        if dim % t == 0:
            return t
        t -= LANES
    return dim


def _params(sem):
    return pltpu.CompilerParams(dimension_semantics=sem, vmem_limit_bytes=VMEM_LIMIT)


def _pack_caps(wname):
    r, c, ax = BIG[wname]
    if ax == 0:
        return (r if r <= 1024 else r // N_SHARD), c
    return r, c // N_SHARD


def _pack_block(wname, br, bc):
    r, c, ax = BIG[wname]
    r0 = PACK_A_ROW[wname]
    sr = r // N_SHARD if ax == 0 else r
    if ax == 0 and br > sr:
        assert br % sr == 0 and r0 % sr == 0
        return (br // sr, sr, bc), lambda rb, cb: (rb, r0 // sr, cb)
    assert r0 % br == 0
    if ax == 0:
        per = sr // br
        return (1, br, bc), lambda rb, cb: (rb // per, r0 // br + rb % per, cb)
    per = (c // N_SHARD) // bc
    return (1, br, bc), lambda rb, cb: (cb // per, r0 // br + rb, cb % per)


def _mm(a, b, *, form, name, a_pro=None, epi=None, out_dtype=F32, tm=1024, tn=1024, tk=1024, b_pack=None,
        b_rows=None, out_pack=None, hosted=None):
    b_shape = BIG[b_pack][:2] if b_pack else b.shape
    if b_pack and form == "nt":
        b_shape = (b_rows or b_shape[0], b_shape[1])
    if form == "nn":
        (m, k), (_, n) = a.shape, b_shape
    elif form == "nt":
        (m, k), (n, _) = a.shape, b_shape
    else:
        (k, m), (_, n) = a.shape, b_shape
    if b_pack:
        rcap, ccap = _pack_caps(b_pack)
        tk, tn = (min(tk, rcap), min(tn, ccap)) if form == "nn" else (min(tk, ccap), min(tn, rcap))
    if out_pack:
        rcap, ccap = _pack_caps(out_pack[0])
        tm, tn = min(tm, rcap), min(tn, ccap)
    tm, tn, tk = _pick(m, tm), _pick(n, tn), _pick(k, tk)
    dims = {"nn": NN, "nt": NT, "tn": TN}[form]
    nk = k // tk
    direct = out_dtype == F32 and epi is None
    n_extra = (1 if epi else 0) + (1 if out_pack else 0)

    def body(a_ref, b_ref, *rest):
        o_ref = rest[n_extra]
        acc_ref = o_ref if direct else rest[-1]

        @pl.when(pl.program_id(2) == 0)
        def _():
            acc_ref[...] = jnp.zeros_like(acc_ref)

        av = a_ref[...]
        if a_pro is not None:
            av = a_pro(av)
        bv = b_ref[...]
        acc_ref[...] += _dot(av, bv.reshape(-1, bv.shape[-1]), dims).reshape(acc_ref.shape)
        if not direct:
            @pl.when(pl.program_id(2) == nk - 1)
            def _():
                val = acc_ref[...]
                if epi is not None:
                    val = epi[0](val, rest[0][...])
                o_ref[...] = val.reshape(o_ref.shape).astype(o_ref.dtype)

    if form == "tn":
        a_spec = pl.BlockSpec((tk, tm), lambda i, j, kk: (kk, i))
    else:
        a_spec = pl.BlockSpec((tm, tk), lambda i, j, kk: (i, kk))
    if b_pack:
        shape, idx = _pack_block(b_pack, *((tk, tn) if form == "nn" else (tn, tk)))
        b_spec = pl.BlockSpec(shape, (lambda i, j, kk: idx(kk, j)) if form == "nn" else (lambda i, j, kk: idx(j, kk)))
    elif form == "nt":
        b_spec = pl.BlockSpec((tn, tk), lambda i, j, kk: (j, kk))
    else:
        b_spec = pl.BlockSpec((tk, tn), lambda i, j, kk: (kk, j))
    in_specs, args = [a_spec, b_spec], [a, b]
    out_spec = pl.BlockSpec((tm, tn), lambda i, j, kk: (i, j))
    out_sds, aliases = jax.ShapeDtypeStruct((m, n), out_dtype), {}
    if epi is not None:
        in_specs.append(out_spec)
        args.append(epi[1])
    if out_pack:
        wname, buf = out_pack
        shape, idx = _pack_block(wname, tm, tn)
        out_spec = pl.BlockSpec(shape, lambda i, j, kk: idx(i, j))
        out_sds, aliases = jax.ShapeDtypeStruct(buf.shape, buf.dtype), {len(args): 0}
        in_specs.append(HBM)
        args.append(buf)
    acc_shape = out_spec.block_shape if out_pack else (tm, tn)
    res = _call_with_step(
        body, hosted, None, args, name=name, grid=(m // tm, n // tn, nk), in_specs=in_specs, out_specs=[out_spec],
        out_shape=[out_sds], sem=("parallel", "parallel", "arbitrary"), aliases=aliases,
        scratch_shapes=[] if direct else [pltpu.VMEM(acc_shape, F32)])
    return res[0] if hosted is None else res


class _Ctx:
    def __init__(self, i, n):
        self.i, self.n = i, n


def _rowwise(fn, rows, consts, row_outs, acc_outs=(), *, tr, name, n_rows=None, hosted=None):
    norm = []
    for r in rows:
        kind = "tile"
        if isinstance(r, tuple) and isinstance(r[0], str):
            kind, r = r[0], r[1:]
        row0 = 0
        if isinstance(r, tuple) and len(r) == 4:
            r, row0 = r[:3], r[3]
        arr, col0, width = r if isinstance(r, tuple) else (r, 0, r.shape[1])
        assert col0 % width == 0
        norm.append((kind, arr, col0 // width, width, row0))
    n_rows = n_rows or next(a.shape[0] for k, a, _, _, _ in norm if k == "tile")
    tr = min(tr, n_rows)
    while n_rows % tr:
        tr -= 8
    n = n_rows // tr
    arrs, specs = [], []
    for kind, arr, cb, width, row0 in norm:
        if kind == "tile":
            assert row0 % tr == 0
            specs.append(pl.BlockSpec((tr, width), lambda i, cb=cb, rb=row0 // tr: (i + rb, cb)))
        elif kind == "prev":
            specs.append(pl.BlockSpec((8, width), lambda i, cb=cb: (jnp.maximum(i * (tr // 8) - 1, 0), cb)))
        else:
            specs.append(pl.BlockSpec((8, width), lambda i, cb=cb: (jnp.minimum((i + 1) * (tr // 8), n_rows // 8 - 1), cb)))
        arrs.append(arr)
    for c in consts:
        specs.append(pl.BlockSpec(c.shape, lambda i, nd=c.ndim: (0,) * nd))
        arrs.append(c)
    n_in, n_ro = len(arrs), len(row_outs)
    row_outs = [w if isinstance(w, tuple) else (w, F32) for w in row_outs]
    out_shape = [jax.ShapeDtypeStruct((n_rows, w), dt) for w, dt in row_outs]
    out_specs = [pl.BlockSpec((tr, w), lambda i: (i, 0)) for w, _ in row_outs]
    out_shape += [jax.ShapeDtypeStruct(s, F32) for s in acc_outs]
    out_specs += [pl.BlockSpec(s, lambda i: (0, 0)) for s in acc_outs]

    def body(*refs):
        i = pl.program_id(0)
        vals = [r[...] for r in refs[:n_in]]
        outs = fn(_Ctx(i, n), *vals)
        if not isinstance(outs, (tuple, list)):
            outs = (outs,)
        o_refs = refs[n_in:]
        for o_ref, o in zip(o_refs[:n_ro], outs[:n_ro]):
            o_ref[...] = o.astype(o_ref.dtype)
        if acc_outs:
            @pl.when(i == 0)
            def _():
                for o_ref in o_refs[n_ro:]:
                    o_ref[...] = jnp.zeros_like(o_ref)

            for o_ref, o in zip(o_refs[n_ro:], outs[n_ro:]):
                o_ref[...] += jnp.broadcast_to(o, o_ref.shape)

    return _call_with_step(body, hosted, None, arrs, name=name, grid=(n,), in_specs=specs, out_specs=out_specs,
                           out_shape=out_shape, sem=("arbitrary",))


def _sum0(v):
    return jnp.sum(v, axis=0, keepdims=True)


def _mean1(v):
    return jnp.mean(v, axis=-1, keepdims=True)


def _sigmoid(v):
    return 1.0 / (1.0 + jnp.exp(-v))


def _ln_stats(t):
    xc = t - _mean1(t)
    rstd = lax.rsqrt(_mean1(xc * xc) + LN_EPS)
    return xc * rstd, rstd


def _ln_bwd(xhat, rstd, dy, g):
    dxh = dy * g
    dx = rstd * (dxh - _mean1(dxh) - xhat * _mean1(dxh * xhat))
    return dx, _sum0(dy * xhat), _sum0(dy)


def _rms_fwd(v, g):
    return v * lax.rsqrt(_mean1(v * v) + RMS_EPS) * g


def _rms_bwd(v, dy, g):
    rs = lax.rsqrt(_mean1(v * v) + RMS_EPS)
    vh = v * rs
    dyg = dy * g
    return rs * (dyg - vh * _mean1(dyg * vh)), _sum0(dy * vh)


def _lane(shape):
    return lax.broadcasted_iota(jnp.int32, shape, len(shape) - 1)


def _shift_down(u, halo, s, is_first):
    tr = u.shape[0]
    rolled = pltpu.roll(u, s, 0)
    hr = jnp.where(is_first, 0.0, pltpu.roll(halo, s, 0))
    row = lax.broadcasted_iota(jnp.int32, hr.shape, 0)
    top = jnp.where(row < s, hr, rolled[0:8])
    if tr == 8:
        return top
    return jnp.concatenate([top, rolled[8:]], axis=0)


def _shift_up(d, halo, s, is_last):
    tr = d.shape[0]
    rolled = pltpu.roll(d, tr - s, 0)
    hr = jnp.where(is_last, 0.0, pltpu.roll(halo, 8 - s, 0))
    row = lax.broadcasted_iota(jnp.int32, hr.shape, 0)
    bot = jnp.where(row >= 8 - s, hr, rolled[tr - 8:])
    if tr == 8:
        return bot
    return jnp.concatenate([rolled[:tr - 8], bot], axis=0)


def _rope(v, ta, tb, tc):
    return v * ta + pltpu.roll(v, 16, 1) * tb + pltpu.roll(v, LANES - 16, 1) * tc


def _rope_bwd(d, ta, tb, tc):
    return d * ta + pltpu.roll(d * tb, LANES - 16, 1) + pltpu.roll(d * tc, 16, 1)


def _ssd_common(dtv, a_row):
    L = SSD_CHUNK
    a = dtv * a_row
    r = lax.broadcasted_iota(jnp.int32, (L, L), 0)
    c = lax.broadcasted_iota(jnp.int32, (L, L), 1)
    tril = r >= c
    cs = _dot_exact(tril.astype(F32), a)
    cs_t = cs.T
    cs_last = cs[L - 1:L, :]
    return dict(a=a, tril=tril, cs=cs, cs_t=cs_t, ecs=jnp.exp(cs), dte=jnp.exp(cs_last - cs),
                elast=jnp.exp(cs_last))


def _pair_sel(v, h0, lo):
    return jnp.where(lo, v[:, h0:h0 + 1], v[:, h0 + 1:h0 + 2])


def _ssd_pair(cm, h0, cb, xp, dtv, bmat, cmat, hp, lo):
    L = SSD_CHUNK
    x = xp * _pair_sel(dtv, h0, lo)
    lam0 = jnp.exp(jnp.where(cm["tril"], cm["cs"][:, h0:h0 + 1] - cm["cs_t"][h0:h0 + 1, :], NEG))
    lam1 = jnp.exp(jnp.where(cm["tril"], cm["cs"][:, h0 + 1:h0 + 2] - cm["cs_t"][h0 + 1:h0 + 2, :], NEG))
    m0, m1 = cb * lam0, cb * lam1
    ydiag = jnp.where(lo, _dot(m0, x), _dot(m1, x))
    ecs_p = _pair_sel(cm["ecs"], h0, lo)
    dte_p = _pair_sel(cm["dte"], h0, lo)
    yoff = _dot(cmat, hp, NT) * ecs_p
    xd = x * dte_p
    st = _dot(xd, bmat, TN)
    rlo = lax.broadcasted_iota(jnp.int32, (LANES, SSD_STATE), 0) < 64
    decay = jnp.where(rlo, cm["elast"][:, h0:h0 + 1], cm["elast"][:, h0 + 1:h0 + 2])
    h_next = hp * decay + st
    return dict(x=x, lam0=lam0, lam1=lam1, m0=m0, m1=m1, y=ydiag + yoff, yoff=yoff, ecs_p=ecs_p, dte_p=dte_p,
                xd=xd, decay=decay, h_next=h_next)


def _ssd_fwd(xbc, dt, a_row, *, name):
    S = xbc.shape[0]
    L = SSD_CHUNK
    nc = S // L

    def body(xs_ref, bm_ref, cm_ref, dt_ref, a_ref, y_ref, hs_ref, h_scr):
        @pl.when(pl.program_id(0) == 0)
        def _():
            h_scr[...] = jnp.zeros_like(h_scr)

        dtv = dt_ref[...]
        cm = _ssd_common(dtv, a_ref[...])
        lo = _lane((L, LANES)) < 64
        ys = []
        for g in range(2):
            bmat = bm_ref[:, g * 128:(g + 1) * 128]
            cmat = cm_ref[:, g * 128:(g + 1) * 128]
            cb = _dot(cmat, bmat, NT)
            for pr in range(2):
                p4 = 2 * g + pr
                hp = h_scr[p4]
                hs_ref[0, p4 * 128:(p4 + 1) * 128, :] = hp
                t = _ssd_pair(cm, 2 * p4, cb, xs_ref[:, p4 * 128:(p4 + 1) * 128], dtv, bmat, cmat, hp, lo)
                ys.append(t["y"])
                h_scr[p4] = t["h_next"]
        y_ref[...] = jnp.concatenate(ys, axis=1)

    return pl.pallas_call(
        body, name=name, grid=(nc,),
        in_specs=[pl.BlockSpec((L, 512), lambda c: (c, 0)), pl.BlockSpec((L, 256), lambda c: (c, 2)),
                  pl.BlockSpec((L, 256), lambda c: (c, 3)), pl.BlockSpec((L, 128), lambda c: (c, 0)),
                  pl.BlockSpec((1, 128), lambda c: (0, 0))],
        out_specs=[pl.BlockSpec((L, 512), lambda c: (c, 0)), pl.BlockSpec((1, 512, 128), lambda c: (c, 0, 0))],
        out_shape=[jax.ShapeDtypeStruct((S, 512), F32), jax.ShapeDtypeStruct((nc, 512, 128), F32)],
        scratch_shapes=[pltpu.VMEM((4, 128, 128), F32)],
        compiler_params=_params(("arbitrary",)),
    )(xbc, xbc, xbc, dt, a_row)


def _ssd_bwd(xbc, dt, a_row, hs, dy, *, name):
    S = xbc.shape[0]
    L = SSD_CHUNK
    nc = S // L

    def body(xs_ref, bm_ref, cm_ref, dt_ref, a_ref, hs_ref, dy_ref, dxs_ref, dbc_ref, ddt_ref, da_ref, g_scr):
        @pl.when(pl.program_id(0) == 0)
        def _():
            g_scr[...] = jnp.zeros_like(g_scr)
            da_ref[...] = jnp.zeros_like(da_ref)

        dtv = dt_ref[...]
        a_row_v = a_ref[...]
        cm = _ssd_common(dtv, a_row_v)
        lo = _lane((L, LANES)) < 64
        lane_row = _lane((1, LANES))
        ri = lax.broadcasted_iota(jnp.int32, (L, L), 0)
        ci = lax.broadcasted_iota(jnp.int32, (L, L), 1)
        triu = (ri <= ci).astype(F32)
        stril = ri > ci

        def halves(v, mask):
            return (jnp.sum(jnp.where(mask, v, 0.0), axis=1, keepdims=True),
                    jnp.sum(jnp.where(mask, 0.0, v), axis=1, keepdims=True))

        i_all = jnp.zeros((L, LANES), F32)
        yo_all = jnp.zeros((L, LANES), F32)
        w_all = jnp.zeros((L, LANES), F32)
        ddt_x = jnp.zeros((L, LANES), F32)
        e_row = jnp.zeros((1, LANES), F32)
        rlo = lax.broadcasted_iota(jnp.int32, (LANES, SSD_STATE), 0) < 64
        dxs, dbs, dcs = [], [], []
        for g in range(2):
            bmat = bm_ref[:, g * 128:(g + 1) * 128]
            cmat = cm_ref[:, g * 128:(g + 1) * 128]
            cb = _dot(cmat, bmat, NT)
            dcb = jnp.zeros((L, L), F32)
            db = jnp.zeros((L, SSD_STATE), F32)
            dc = jnp.zeros((L, SSD_STATE), F32)
            for pr in range(2):
                p4 = 2 * g + pr
                h0 = 2 * p4
                hp = hs_ref[0, p4 * 128:(p4 + 1) * 128, :]
                xp = xs_ref[:, p4 * 128:(p4 + 1) * 128]
                t = _ssd_pair(cm, h0, cb, xp, dtv, bmat, cmat, hp, lo)
                gst = g_scr[p4]
                dyp = dy_ref[:, p4 * 128:(p4 + 1) * 128]
                dy0 = jnp.where(lo, dyp, 0.0)
                dy1 = dyp - dy0
                bg = _dot(bmat, gst, NT)
                dx = _dot(t["m0"], dy0, TN) + _dot(t["m1"], dy1, TN) + bg * t["dte_p"]
                dm0, dm1 = _dot(dy0, t["x"], NT), _dot(dy1, t["x"], NT)
                dcb = dcb + dm0 * t["lam0"] + dm1 * t["lam1"]
                dye = dyp * t["ecs_p"]
                dc = dc + _dot(dye, hp)
                db = db + _dot(t["xd"], gst)
                i0 = jnp.sum(jnp.where(stril, _dot(triu, dm0 * t["m0"]), 0.0), axis=1, keepdims=True)
                i1 = jnp.sum(jnp.where(stril, _dot(triu, dm1 * t["m1"]), 0.0), axis=1, keepdims=True)
                yo0, yo1 = halves(dyp * t["yoff"], lo)
                w0, w1 = halves(t["xd"] * bg, lo)
                gh = gst * (hp * t["decay"])
                e0 = _sum0(jnp.sum(jnp.where(rlo, gh, 0.0), axis=1, keepdims=True))
                e1 = _sum0(jnp.sum(jnp.where(rlo, 0.0, gh), axis=1, keepdims=True))
                x0, x1 = halves(dx * xp, lo)
                oh0 = (lane_row == h0).astype(F32)
                oh1 = (lane_row == h0 + 1).astype(F32)
                i_all = i_all + i0 * oh0 + i1 * oh1
                yo_all = yo_all + yo0 * oh0 + yo1 * oh1
                w_all = w_all + w0 * oh0 + w1 * oh1
                e_row = e_row + e0 * oh0 + e1 * oh1
                ddt_x = ddt_x + x0 * oh0 + x1 * oh1
                dxs.append(dx * _pair_sel(dtv, h0, lo))
                g_scr[p4] = gst * t["decay"] + _dot(dye, cmat, TN)
            dbs.append(db + _dot(dcb, cmat, TN))
            dcs.append(dc + _dot(dcb, bmat))
        da = i_all + _dot_exact(triu, yo_all) + _dot_exact(stril.astype(F32), w_all) + e_row
        ddt_ref[...] = da * a_row_v + ddt_x
        da_ref[...] += _sum0(da * dtv)
        dxs_ref[...] = jnp.concatenate(dxs, axis=1)
        dbc_ref[...] = jnp.concatenate(dbs + dcs, axis=1)

    rev = lambda c: nc - 1 - c
    return pl.pallas_call(
        body, name=name, grid=(nc,),
        in_specs=[pl.BlockSpec((L, 512), lambda c: (rev(c), 0)), pl.BlockSpec((L, 256), lambda c: (rev(c), 2)),
                  pl.BlockSpec((L, 256), lambda c: (rev(c), 3)), pl.BlockSpec((L, 128), lambda c: (rev(c), 0)),
                  pl.BlockSpec((1, 128), lambda c: (0, 0)), pl.BlockSpec((1, 512, 128), lambda c: (rev(c), 0, 0)),
                  pl.BlockSpec((L, 512), lambda c: (rev(c), 0))],
        out_specs=[pl.BlockSpec((L, 512), lambda c: (rev(c), 0)), pl.BlockSpec((L, 512), lambda c: (rev(c), 0)),
                   pl.BlockSpec((L, 128), lambda c: (rev(c), 0)), pl.BlockSpec((1, 128), lambda c: (0, 0))],
        out_shape=[jax.ShapeDtypeStruct((S, 512), F32), jax.ShapeDtypeStruct((S, 512), F32),
                   jax.ShapeDtypeStruct((S, 128), F32), jax.ShapeDtypeStruct((1, 128), F32)],
        scratch_shapes=[pltpu.VMEM((4, 128, 128), F32)],
        compiler_params=_params(("arbitrary",)),
    )(xbc, xbc, xbc, dt, a_row, hs, dy)


MLA_SCALE = MLA_QK ** -0.5


def _causal_scores(q, k, qi, ki, t):
    s = _dot(q, k, NT) * MLA_SCALE
    row = qi * t + lax.broadcasted_iota(jnp.int32, (t, t), 0)
    col = ki * t + lax.broadcasted_iota(jnp.int32, (t, t), 1)
    return jnp.where(col <= row, s, NEG)


def _mla_fwd(q, k, kv, *, name):
    S = q.shape[0]
    t = min(ATTN_TILE, S)
    nq = S // t

    def body(q_ref, k_ref, v_ref, o_ref, lse_ref, m_scr, l_scr, acc_scr):
        qi, ki = pl.program_id(1), pl.program_id(2)

        @pl.when(ki == 0)
        def _():
            m_scr[...] = jnp.full_like(m_scr, NEG)
            l_scr[...] = jnp.zeros_like(l_scr)
            acc_scr[...] = jnp.zeros_like(acc_scr)

        @pl.when(ki <= qi)
        def _():
            s = _causal_scores(q_ref[...], k_ref[...], qi, ki, t)
            m_old = m_scr[:, 0:1]
            m_new = jnp.maximum(m_old, jnp.max(s, axis=1, keepdims=True))
            p = jnp.exp(s - m_new)
            corr = jnp.exp(m_old - m_new)
            l_scr[...] = jnp.broadcast_to(corr * l_scr[:, 0:1] + jnp.sum(p, axis=1, keepdims=True), l_scr.shape)
            acc_scr[...] = corr * acc_scr[...] + _dot(p, v_ref[...])
            m_scr[...] = jnp.broadcast_to(m_new, m_scr.shape)

        @pl.when(ki == nq - 1)
        def _():
            l = l_scr[:, 0:1]
            o_ref[...] = acc_scr[...] / l
            lse_ref[0] = jnp.broadcast_to(m_scr[:, 0:1] + jnp.log(l), (t, LANES))

    return pl.pallas_call(
        body, name=name, grid=(MLA_HEADS, nq, nq),
        in_specs=[pl.BlockSpec((t, 128), lambda h, qi, ki: (qi, h)),
                  pl.BlockSpec((t, 128), lambda h, qi, ki: (jnp.minimum(ki, qi), h)),
                  pl.BlockSpec((t, 128), lambda h, qi, ki: (jnp.minimum(ki, qi), 2 * h + 1))],
        out_specs=[pl.BlockSpec((t, 128), lambda h, qi, ki: (qi, h)),
                   pl.BlockSpec((1, t, 128), lambda h, qi, ki: (h, qi, 0))],
        out_shape=[jax.ShapeDtypeStruct((S, MLA_HEADS * 128), F32), jax.ShapeDtypeStruct((MLA_HEADS, S, 128), F32)],
        scratch_shapes=[pltpu.VMEM((t, 128), F32), pltpu.VMEM((t, 128), F32), pltpu.VMEM((t, 128), F32)],
        compiler_params=_params(("parallel", "parallel", "arbitrary")),
    )(q, k, kv)


def _mla_bwd_dkv(q, k, kv, o, do, lse, *, name):
    S = q.shape[0]
    t = min(ATTN_TILE, S)
    nq = S // t

    def body(q_ref, k_ref, v_ref, o_ref, do_ref, lse_ref, dkv_ref):
        ki, qi = pl.program_id(1), pl.program_id(2)

        @pl.when(qi == 0)
        def _():
            dkv_ref[...] = jnp.zeros_like(dkv_ref)

        @pl.when(qi >= ki)
        def _():
            qv, dov = q_ref[...], do_ref[...]
            s = _causal_scores(qv, k_ref[...], qi, ki, t)
            p = jnp.exp(s - lse_ref[0][:, 0:1])
            dv = _dot(p, dov, TN)
            dp = _dot(dov, v_ref[...], NT)
            delta = jnp.sum(dov * o_ref[...], axis=1, keepdims=True)
            ds = p * (dp - delta) * MLA_SCALE
            dkv_ref[...] += jnp.concatenate([_dot(ds, qv, TN), dv], axis=1)

    qmap = lambda h, ki, qi: (jnp.maximum(qi, ki), h)
    return pl.pallas_call(
        body, name=name, grid=(MLA_HEADS, nq, nq),
        in_specs=[pl.BlockSpec((t, 128), qmap),
                  pl.BlockSpec((t, 128), lambda h, ki, qi: (ki, h)),
                  pl.BlockSpec((t, 128), lambda h, ki, qi: (ki, 2 * h + 1)),
                  pl.BlockSpec((t, 128), qmap), pl.BlockSpec((t, 128), qmap),
                  pl.BlockSpec((1, t, 128), lambda h, ki, qi: (h, jnp.maximum(qi, ki), 0))],
        out_specs=pl.BlockSpec((t, 256), lambda h, ki, qi: (ki, h)),
        out_shape=jax.ShapeDtypeStruct((S, MLA_HEADS * 256), F32),
        compiler_params=_params(("parallel", "parallel", "arbitrary")),
    )(q, k, kv, o, do, lse)


def _mla_bwd_dq(q, k, kv, o, do, lse, *, name):
    S = q.shape[0]
    t = min(ATTN_TILE, S)
    nq = S // t

    def body(q_ref, k_ref, v_ref, o_ref, do_ref, lse_ref, dq_ref):
        qi, ki = pl.program_id(1), pl.program_id(2)

        @pl.when(ki == 0)
        def _():
            dq_ref[...] = jnp.zeros_like(dq_ref)

        @pl.when(ki <= qi)
        def _():
            dov, kv = do_ref[...], k_ref[...]
            s = _causal_scores(q_ref[...], kv, qi, ki, t)
            p = jnp.exp(s - lse_ref[0][:, 0:1])
            dp = _dot(dov, v_ref[...], NT)
            delta = jnp.sum(dov * o_ref[...], axis=1, keepdims=True)
            ds = p * (dp - delta) * MLA_SCALE
            dq_ref[...] += _dot(ds, kv)

    qmap = lambda h, qi, ki: (qi, h)
    return pl.pallas_call(
        body, name=name, grid=(MLA_HEADS, nq, nq),
        in_specs=[pl.BlockSpec((t, 128), qmap),
                  pl.BlockSpec((t, 128), lambda h, qi, ki: (jnp.minimum(ki, qi), h)),
                  pl.BlockSpec((t, 128), lambda h, qi, ki: (jnp.minimum(ki, qi), 2 * h + 1)),
                  pl.BlockSpec((t, 128), qmap), pl.BlockSpec((t, 128), qmap),
                  pl.BlockSpec((1, t, 128), lambda h, qi, ki: (h, qi, 0))],
        out_specs=pl.BlockSpec((t, 128), qmap),
        out_shape=jax.ShapeDtypeStruct((S, MLA_HEADS * 128), F32),
        compiler_params=_params(("parallel", "parallel", "arbitrary")),
    )(q, k, kv, o, do, lse)


HBM = pl.BlockSpec(memory_space=pl.ANY)


class _Step:
    def __init__(self, inputs, out_shapes, n_sems, start, finish, mid=None):
        self.inputs, self.out_shapes, self.n_sems = inputs, out_shapes, n_sems
        self.start, self.finish, self.mid = start, finish, mid


def _place():
    x, y, c = lax.axis_index("x"), lax.axis_index("y"), lax.axis_index("c")
    chips = [(1 - x, y), (x, 1 - y), (1 - x, 1 - y)]
    return x, y, c, chips


def _chunks(rows, tile):
    return next(n for n in (4, 3, 2, 1) if rows % (n * tile) == 0)


def _remote(src, dst, sems, j, to):
    return pltpu.make_async_remote_copy(src_ref=src, dst_ref=dst, send_sem=sems[0].at[j], recv_sem=sems[1].at[j],
                                        device_id=to, device_id_type=MESH)


def _gather_step(wp):
    R, C = wp.shape
    H = R // 2
    nq = _chunks(H, 16)
    CH = H // nq

    def copies(ins, outs, sems):
        x, y, c, chips = _place()
        sib, me = (x, y, 1 - c), 2 * x + y
        w_ref, out_ref = ins[0], outs[0]

        def piece(k, hc, q):
            return out_ref.at[k, pl.ds(hc * H + q * CH, CH), :]

        sends, landed, fwds, fwd_landed = [], [], [], []
        for q in range(nq):
            for j, (px, py) in enumerate(chips):
                k = 2 * px + py
                sends.append(_remote(w_ref.at[pl.ds(c * H + q * CH, CH), :], piece(me, c, q), sems, j * nq + q,
                                     (px, py, c)))
                landed.append(_remote(piece(k, c, q), piece(k, c, q), sems, j * nq + q, (px, py, c)))
                fwds.append(_remote(piece(k, c, q), piece(k, c, q), sems, (3 + j) * nq + q, sib))
                fwd_landed.append(_remote(piece(k, 1 - c, q), piece(k, 1 - c, q), sems, (3 + j) * nq + q, sib))
        return sends, landed, fwds, fwd_landed

    def start(ins, outs, sems):
        for cp in copies(ins, outs, sems)[0]:
            cp.start()

    def mid(ins, outs, sems):
        _, landed, fwds, _ = copies(ins, outs, sems)
        for arrived, onward in zip(landed, fwds):
            arrived.wait_recv()
            onward.start()

    def finish(ins, outs, sems):
        sends, _, fwds, fwd_landed = copies(ins, outs, sems)
        for cp in fwd_landed:
            cp.wait_recv()
        for cp in sends + fwds:
            cp.wait_send()

    return _Step([wp], [jax.ShapeDtypeStruct((N_SHARD, R, C), wp.dtype)], 6 * nq, start, finish, mid)


def _pair_exchange_step(gp):
    n, R, C = gp.shape
    H = R // 2
    nq = _chunks(H, 8)
    CH = H // nq

    def copies(ins, outs, sems):
        x, y, c, _ = _place()
        return [_remote(ins[0].at[k, pl.ds((1 - c) * H + q * CH, CH), :], outs[0].at[k, pl.ds(q * CH, CH), :], sems,
                        k * nq + q, (x, y, 1 - c)) for k in range(n) for q in range(nq)]

    def start(ins, outs, sems):
        for cp in copies(ins, outs, sems):
            cp.start()

    def finish(ins, outs, sems):
        for cp in copies(ins, outs, sems):
            cp.wait()

    return _Step([gp], [jax.ShapeDtypeStruct((n, H, C), gp.dtype)], n * nq, start, finish)


def _chip_exchange_step(pb):
    n, H, C = pb.shape
    nq = _chunks(H, 16)
    CH = H // nq

    def copies(ins, outs, sems):
        x, y, c, chips = _place()
        return [_remote(ins[0].at[2 * px + py, pl.ds(q * CH, CH), :], outs[0].at[j, pl.ds(q * CH, CH), :], sems,
                        j * nq + q, (px, py, c)) for q in range(nq) for j, (px, py) in enumerate(chips)]

    def start(ins, outs, sems):
        for cp in copies(ins, outs, sems):
            cp.start()

    def finish(ins, outs, sems):
        for cp in copies(ins, outs, sems):
            cp.wait()

    return _Step([pb], [jax.ShapeDtypeStruct((3, H, C), pb.dtype)], 3 * nq, start, finish)


def _pair_join_step(q):
    H, C = q.shape
    nq = _chunks(H, 8)
    CH = H // nq

    def copies(ins, outs, sems):
        x, y, c, _ = _place()
        return [_remote(ins[0].at[pl.ds(j * CH, CH), :], outs[0].at[pl.ds(j * CH, CH), :], sems, j, (x, y, 1 - c))
                for j in range(nq)]

    def start(ins, outs, sems):
        for cp in copies(ins, outs, sems):
            cp.start()

    def finish(ins, outs, sems):
        for cp in copies(ins, outs, sems):
            cp.wait()

    return _Step([q], [jax.ShapeDtypeStruct((H, C), q.dtype)], nq, start, finish)


def _sem_scratch(step):
    return [pltpu.SemaphoreType.DMA((step.n_sems,)), pltpu.SemaphoreType.DMA((step.n_sems,))]


def _run_step(step, name):
    ni, no = len(step.inputs), len(step.out_shapes)

    def body(*refs):
        ins, outs, sems = refs[:ni], refs[ni:ni + no], refs[ni + no:]
        step.start(ins, outs, sems)
        if step.mid is not None:
            step.mid(ins, outs, sems)
        step.finish(ins, outs, sems)

    return pl.pallas_call(body, name=name, in_specs=[HBM] * ni, out_specs=[HBM] * no, out_shape=step.out_shapes,
                          scratch_shapes=_sem_scratch(step))(*step.inputs)


def _grid_flags(grid):
    ids = [pl.program_id(d) for d in range(len(grid))]
    first = functools.reduce(lambda a, b: a & b, [i == 0 for i in ids])
    last = functools.reduce(lambda a, b: a & b, [i == n - 1 for i, n in zip(ids, grid)])
    return first, last, last


def _call_with_step(core, step, flags, args, *, name, grid, in_specs, out_specs, out_shape, sem, scratch_shapes=(),
                    aliases=None):
    aliases = aliases or {}
    if step is None:
        return pl.pallas_call(core, name=name, grid=grid, in_specs=in_specs, out_specs=out_specs,
                              out_shape=out_shape, scratch_shapes=list(scratch_shapes),
                              input_output_aliases=aliases, compiler_params=_params(sem))(*args)
    n_in, n_out, n_scr = len(in_specs), len(out_specs), len(scratch_shapes)
    si, so = len(step.inputs), len(step.out_shapes)
    flags = flags or (lambda: _grid_flags(grid))

    def body(*refs):
        ins, s_ins = refs[:n_in], refs[n_in:n_in + si]
        outs = refs[n_in + si:n_in + si + n_out]
        s_outs = refs[n_in + si + n_out:n_in + si + n_out + so]
        scr = refs[n_in + si + n_out + so:n_in + si + n_out + so + n_scr]
        sems = refs[n_in + si + n_out + so + n_scr:]
        first, middle, last = flags()

        @pl.when(first)
        def _():
            step.start(s_ins, s_outs, sems)

        if step.mid is not None:
            @pl.when(middle)
            def _():
                step.mid(s_ins, s_outs, sems)

        core(*ins, *outs, *scr)

        @pl.when(last)
        def _():
            step.finish(s_ins, s_outs, sems)

    return pl.pallas_call(
        body, name=name, grid=grid, in_specs=list(in_specs) + [HBM] * si, out_specs=list(out_specs) + [HBM] * so,
        out_shape=list(out_shape) + list(step.out_shapes), scratch_shapes=list(scratch_shapes) + _sem_scratch(step),
        input_output_aliases=aliases, compiler_params=_params(("arbitrary",) * len(grid)))(*args, *step.inputs)


def _attn_flags(nq):
    h, qi = pl.program_id(0), pl.program_id(1)
    return ((h == 0) & (qi == 0), (h == MLA_HEADS - 1) & (qi == 0), (h == MLA_HEADS - 1) & (qi == nq - 1))


ATTN_SPLIT = 1
ATTN_KEY_SPLIT = 1


def _att_mask(s_t, q0, k0):
    krow = k0 + lax.broadcasted_iota(jnp.int32, s_t.shape, 0)
    qcol = q0 + lax.broadcasted_iota(jnp.int32, s_t.shape, 1)
    return jnp.where(krow <= qcol, s_t, NEG)


def _loop2(lo, hi, step, carry):
    def two(i, c):
        kb = lo + 2 * i
        return step(kb + 1, step(kb, c))

    carry = lax.fori_loop(0, (hi - lo) // 2, two, carry)
    return lax.cond((hi - lo) % 2 == 1, lambda c: step(hi - 1, c), lambda c: c, carry)


def _rows(ref, blk, t):
    return ref[pl.ds(pl.multiple_of(blk * t, t), t), :]


def _cols(ref, blk, t):
    return ref[:, pl.ds(pl.multiple_of(blk * t, t), t)]


def _attn_fwd(q, k, v_t, *, name, hosted=None):
    S = q.shape[0]
    t = min(ATTN_TILE, S)
    nq = S // t

    def body(q_ref, k_ref, vt_ref, o_ref, lse_ref):
        qi = pl.program_id(1)
        w = t // ATTN_SPLIT
        qs = [q_ref[s * w:(s + 1) * w, :] for s in range(ATTN_SPLIT)]

        tk = t // ATTN_KEY_SPLIT

        def step(kb, carry, masked):
            kt, vt = _rows(k_ref, kb, tk), _cols(vt_ref, kb, tk)
            out = []
            for s, (m, l, acc) in enumerate(carry):
                s_t = lax.dot_general(kt, qs[s], NT, preferred_element_type=F32)
                if masked:
                    s_t = _att_mask(s_t, qi * t + s * w, kb * tk)
                m_new = jnp.maximum(m, jnp.max(s_t, axis=0, keepdims=True))
                p_t = jnp.exp(s_t - m_new)
                corr = jnp.exp(m - m_new)
                l = corr * l + jnp.sum(p_t, axis=0, keepdims=True)
                acc = corr * acc + lax.dot_general(vt, p_t.astype(BF16), NN, preferred_element_type=F32)
                out.append((m_new, l, acc))
            return tuple(out)

        init = tuple((jnp.full((1, w), NEG, F32), jnp.zeros((1, w), F32), jnp.zeros((LANES, w), F32))
                     for _ in range(ATTN_SPLIT))
        carry = _loop2(0, qi * ATTN_KEY_SPLIT, lambda kb, c: step(kb, c, False), init)
        for j in range(ATTN_KEY_SPLIT):
            carry = step(qi * ATTN_KEY_SPLIT + j, carry, True)
        for s, (m, l, acc) in enumerate(carry):
            o_ref[:, s * w:(s + 1) * w] = acc / l
            lse_ref[0, :, s * w:(s + 1) * w] = m + jnp.log(l)

    return _call_with_step(
        body, hosted, lambda: _attn_flags(nq), (q, k, v_t), name=name, grid=(MLA_HEADS, nq),
        in_specs=[pl.BlockSpec((t, LANES), lambda h, qi: (qi, h)),
                  pl.BlockSpec((S, LANES), lambda h, qi: (0, h)),
                  pl.BlockSpec((LANES, S), lambda h, qi: (h, 0))],
        out_specs=[pl.BlockSpec((LANES, t), lambda h, qi: (h, qi)),
                   pl.BlockSpec((1, 1, t), lambda h, qi: (h, 0, qi))],
        out_shape=[jax.ShapeDtypeStruct((MLA_HEADS * LANES, S), F32), jax.ShapeDtypeStruct((MLA_HEADS, 1, S), F32)],
        sem=("parallel", "arbitrary"))


def _attn_bwd_dq(q, k, v, o_t, do_t, lse, *, name, hosted=None):
    S = q.shape[0]
    t = min(ATTN_TILE, S)
    nq = S // t

    def body(q_ref, k_ref, v_ref, o_ref, do_ref, lse_ref, dq_ref, delta_ref):
        qi = pl.program_id(1)
        qv = q_ref[...]
        dov = do_ref[...]
        delta = jnp.sum(dov * o_ref[...], axis=0, keepdims=True)
        delta_ref[0] = delta
        dob = dov.astype(BF16)
        lse_v = lse_ref[0]

        def step(kb, acc, masked):
            s_t = lax.dot_general(_rows(k_ref, kb, t), qv, NT, preferred_element_type=F32)
            if masked:
                s_t = _att_mask(s_t, qi * t, kb * t)
            p_t = jnp.exp(s_t - lse_v)
            dp_t = lax.dot_general(_rows(v_ref, kb, t), dob, NN, preferred_element_type=F32)
            ds_t = (p_t * (dp_t - delta)).astype(BF16)
            return acc + lax.dot_general(_rows(k_ref, kb, t), ds_t, TN, preferred_element_type=F32)

        acc = _loop2(0, qi, lambda kb, c: step(kb, c, False), jnp.zeros((LANES, t), F32))
        dq_ref[...] = step(qi, acc, True).T

    tile = pl.BlockSpec((t, LANES), lambda h, qi: (qi, h))
    tile_t = pl.BlockSpec((LANES, t), lambda h, qi: (h, qi))
    stat = pl.BlockSpec((1, 1, t), lambda h, qi: (h, 0, qi))
    seq = pl.BlockSpec((S, LANES), lambda h, qi: (0, h))
    return _call_with_step(
        body, hosted, lambda: _attn_flags(nq), (q, k, v, o_t, do_t, lse), name=name, grid=(MLA_HEADS, nq),
        in_specs=[tile, seq, seq, tile_t, tile_t, stat],
        out_specs=[tile, stat],
        out_shape=[jax.ShapeDtypeStruct((S, MLA_HEADS * LANES), F32), jax.ShapeDtypeStruct((MLA_HEADS, 1, S), F32)],
        sem=("parallel", "arbitrary"))


def _attn_bwd_dkv(q, k, v, do_t, lse, delta, *, name, hosted=None):
    S = q.shape[0]
    t = min(ATTN_TILE, S)
    nq = S // t

    def body(q_ref, k_ref, v_ref, do_ref, lse_ref, delta_ref, dk_ref, dv_ref):
        ki = pl.program_id(1)
        kv, vv = k_ref[...], v_ref[...]

        def step(qb, carry, masked):
            dk, dv = carry
            qt = _rows(q_ref, qb, t)
            s_t = lax.dot_general(kv, qt, NT, preferred_element_type=F32)
            if masked:
                s_t = _att_mask(s_t, qb * t, ki * t)
            p_t = jnp.exp(s_t - _cols(lse_ref.at[0], qb, t))
            dob = _cols(do_ref, qb, t).astype(BF16)
            dv = dv + lax.dot_general(dob, p_t.astype(BF16), NT, preferred_element_type=F32)
            dp_t = lax.dot_general(vv, dob, NN, preferred_element_type=F32)
            ds_t = (p_t * (dp_t - _cols(delta_ref.at[0], qb, t))).astype(BF16)
            dk = dk + lax.dot_general(qt.T, ds_t, NT, preferred_element_type=F32)
            return dk, dv

        zero = jnp.zeros((LANES, t), F32)
        carry = step(ki, (zero, zero), True)
        dk, dv = _loop2(ki + 1, nq, lambda qb, c: step(qb, c, False), carry)
        dk_ref[...] = dk.T
        dv_ref[...] = dv

    tile = pl.BlockSpec((t, LANES), lambda h, ki: (ki, h))
    tile_t = pl.BlockSpec((LANES, t), lambda h, ki: (h, ki))
    seq = pl.BlockSpec((S, LANES), lambda h, ki: (0, h))
    seq_t = pl.BlockSpec((LANES, S), lambda h, ki: (h, 0))
    stat = pl.BlockSpec((1, 1, S), lambda h, ki: (h, 0, 0))
    return _call_with_step(
        body, hosted, lambda: _attn_flags(nq), (q, k, v, do_t, lse, delta), name=name, grid=(MLA_HEADS, nq),
        in_specs=[seq, tile, tile, seq_t, stat, stat],
        out_specs=[tile, tile_t],
        out_shape=[jax.ShapeDtypeStruct((S, MLA_HEADS * LANES), F32), jax.ShapeDtypeStruct((MLA_HEADS * LANES, S), F32)],
        sem=("parallel", "arbitrary"))


def _attn_fwd_p(q, k, v_t, *, name, hosted=None):
    S = q.shape[0]
    t = min(ATTN_TILE, S)
    nq = S // t

    def body(q_ref, k_ref, vt_ref, o_ref, lse_ref):
        qi = pl.program_id(1)
        qv = q_ref[...]

        def scores(kb):
            return lax.dot_general(_rows(k_ref, kb, t), qv, NT, preferred_element_type=F32)

        def weigh(kb, p_t):
            return lax.dot_general(_cols(vt_ref, kb, t), p_t, NN, preferred_element_type=F32)

        def soft(s_t, m, l):
            m_new = jnp.maximum(m, jnp.max(s_t, axis=0, keepdims=True))
            p_t = jnp.exp(s_t - m_new)
            corr = jnp.exp(m - m_new)
            return m_new, corr * l + jnp.sum(p_t, axis=0, keepdims=True), corr, p_t.astype(BF16)

        def step(kb, carry):
            s_cur, m, l, acc, p_prev, corr_prev = carry
            s_next = scores(kb + 1)
            acc = corr_prev * acc + weigh(jnp.maximum(kb - 1, 0), p_prev)
            m, l, corr, p_t = soft(s_cur, m, l)
            return s_next, m, l, acc, p_t, corr

        init = (scores(0), jnp.full((1, t), NEG, F32), jnp.zeros((1, t), F32), jnp.zeros((LANES, t), F32),
                jnp.zeros((t, t), BF16), jnp.ones((1, t), F32))
        s_cur, m, l, acc, p_prev, corr_prev = lax.fori_loop(0, qi, step, init)
        acc = corr_prev * acc + weigh(jnp.maximum(qi - 1, 0), p_prev)
        m, l, corr, p_t = soft(_att_mask(s_cur, qi * t, qi * t), m, l)
        acc = corr * acc + weigh(qi, p_t)
        o_ref[...] = acc / l
        lse_ref[0] = m + jnp.log(l)

    return _call_with_step(
        body, hosted, lambda: _attn_flags(nq), (q, k, v_t), name=name, grid=(MLA_HEADS, nq),
        in_specs=[pl.BlockSpec((t, LANES), lambda h, qi: (qi, h)),
                  pl.BlockSpec((S, LANES), lambda h, qi: (0, h)),
                  pl.BlockSpec((LANES, S), lambda h, qi: (h, 0))],
        out_specs=[pl.BlockSpec((LANES, t), lambda h, qi: (h, qi)),
                   pl.BlockSpec((1, 1, t), lambda h, qi: (h, 0, qi))],
        out_shape=[jax.ShapeDtypeStruct((MLA_HEADS * LANES, S), F32), jax.ShapeDtypeStruct((MLA_HEADS, 1, S), F32)],
        sem=("parallel", "arbitrary"))


def _attn_bwd_dq_p(q, k, k_t, v, o_t, do_t, lse, *, name, hosted=None):
    S = q.shape[0]
    t = min(ATTN_TILE, S)
    nq = S // t

    def body(q_ref, k_ref, kt_ref, v_ref, o_ref, do_ref, lse_ref, dq_ref, delta_ref):
        qi = pl.program_id(1)
        qv = q_ref[...]
        dov = do_ref[...]
        delta = jnp.sum(dov * o_ref[...], axis=0, keepdims=True)
        delta_ref[0] = delta
        dob = dov.astype(BF16)
        lse_v = lse_ref[0]

        def front(kb):
            return (lax.dot_general(_rows(k_ref, kb, t), qv, NT, preferred_element_type=F32),
                    lax.dot_general(_rows(v_ref, kb, t), dob, NN, preferred_element_type=F32))

        def back(kb, ds_t):
            return lax.dot_general(_cols(kt_ref, kb, t), ds_t, NN, preferred_element_type=F32)

        def mid(s_t, dp_t):
            return (jnp.exp(s_t - lse_v) * (dp_t - delta)).astype(BF16)

        def step(kb, carry):
            s_cur, dp_cur, acc, ds_prev = carry
            s_next, dp_next = front(kb + 1)
            acc = acc + back(jnp.maximum(kb - 1, 0), ds_prev)
            return s_next, dp_next, acc, mid(s_cur, dp_cur)

        init = (*front(0), jnp.zeros((LANES, t), F32), jnp.zeros((t, t), BF16))
        s_cur, dp_cur, acc, ds_prev = lax.fori_loop(0, qi, step, init)
        acc = acc + back(jnp.maximum(qi - 1, 0), ds_prev)
        dq_ref[...] = acc + back(qi, mid(_att_mask(s_cur, qi * t, qi * t), dp_cur))

    tile_t = pl.BlockSpec((LANES, t), lambda h, qi: (h, qi))
    stat = pl.BlockSpec((1, 1, t), lambda h, qi: (h, 0, qi))
    seq = pl.BlockSpec((S, LANES), lambda h, qi: (0, h))
    return _call_with_step(
        body, hosted, lambda: _attn_flags(nq), (q, k, k_t, v, o_t, do_t, lse), name=name, grid=(MLA_HEADS, nq),
        in_specs=[pl.BlockSpec((t, LANES), lambda h, qi: (qi, h)), seq,
                  pl.BlockSpec((LANES, S), lambda h, qi: (h, 0)), seq, tile_t, tile_t, stat],
        out_specs=[tile_t, stat],
        out_shape=[jax.ShapeDtypeStruct((MLA_HEADS * LANES, S), F32), jax.ShapeDtypeStruct((MLA_HEADS, 1, S), F32)],
        sem=("parallel", "arbitrary"))


def _attn_bwd_dkv_p(q, q_t, k, v, do_t, lse, delta, *, name, hosted=None):
    S = q.shape[0]
    t = min(ATTN_TILE, S)
    nq = S // t

    def body(q_ref, qt_ref, k_ref, v_ref, do_ref, lse_ref, delta_ref, dk_ref, dv_ref):
        ki = pl.program_id(1)
        kv, vv = k_ref[...], v_ref[...]

        def grad_out(qb):
            return _cols(do_ref, qb, t).astype(BF16)

        def front(qb):
            return (lax.dot_general(kv, _rows(q_ref, qb, t), NT, preferred_element_type=F32),
                    lax.dot_general(vv, grad_out(qb), NN, preferred_element_type=F32))

        def mid(s_t, dp_t, qb):
            p_t = jnp.exp(s_t - _cols(lse_ref.at[0], qb, t))
            return p_t.astype(BF16), (p_t * (dp_t - _cols(delta_ref.at[0], qb, t))).astype(BF16)

        def back(qb, dk, dv, p_t, ds_t):
            return (dk + lax.dot_general(_cols(qt_ref, qb, t), ds_t, NT, preferred_element_type=F32),
                    dv + lax.dot_general(grad_out(qb), p_t, NT, preferred_element_type=F32))

        def step(qb, carry):
            s_cur, dp_cur, dk, dv, p_prev, ds_prev = carry
            s_next, dp_next = front(jnp.minimum(qb + 1, nq - 1))
            dk, dv = back(qb - 1, dk, dv, p_prev, ds_prev)
            p_t, ds_t = mid(s_cur, dp_cur, qb)
            return s_next, dp_next, dk, dv, p_t, ds_t

        s0, dp0 = front(ki)
        p0, ds0 = mid(_att_mask(s0, ki * t, ki * t), dp0, ki)
        zero = jnp.zeros((LANES, t), F32)
        init = (*front(jnp.minimum(ki + 1, nq - 1)), zero, zero, p0, ds0)
        _, _, dk, dv, p_prev, ds_prev = lax.fori_loop(ki + 1, nq, step, init)
        dk, dv = back(nq - 1, dk, dv, p_prev, ds_prev)
        dk_ref[...] = dk
        dv_ref[...] = dv

    tile = pl.BlockSpec((t, LANES), lambda h, ki: (ki, h))
    tile_t = pl.BlockSpec((LANES, t), lambda h, ki: (h, ki))
    seq = pl.BlockSpec((S, LANES), lambda h, ki: (0, h))
    seq_t = pl.BlockSpec((LANES, S), lambda h, ki: (h, 0))
    stat = pl.BlockSpec((1, 1, S), lambda h, ki: (h, 0, 0))
    return _call_with_step(
        body, hosted, lambda: _attn_flags(nq), (q, q_t, k, v, do_t, lse, delta), name=name, grid=(MLA_HEADS, nq),
        in_specs=[seq, seq_t, tile, tile, seq_t, stat, stat],
        out_specs=[tile_t, tile_t],
        out_shape=[jax.ShapeDtypeStruct((MLA_HEADS * LANES, S), F32)] * 2,
        sem=("parallel", "arbitrary"))


def _fn_ln(ctx, x, g, b):
    xhat, _ = _ln_stats(x)
    y = xhat * g + b
    return y, y


def _fn_conv_fwd(ctx, u, up, dtr, w8, cb, dtb):
    first = ctx.i == 0
    y = u * w8[3:4] + cb
    for s in (1, 2, 3):
        y = y + _shift_down(u, up, s, first) * w8[3 - s:4 - s]
    act = y * _sigmoid(y)
    v = dtr + dtb
    e = jnp.exp(-jnp.abs(v))
    one_p = 1.0 + e
    log1p = jnp.where(one_p == 1.0, e, jnp.log(one_p) * e / (one_p - 1.0))
    return y, act, jnp.maximum(v, 0.0) + log1p


def _fn_ssd_post(ctx, y, xs, z, dexp, g):
    yg = (y + xs * dexp) * (z * _sigmoid(z))
    outs = []
    for k in range(2):
        v = yg[:, 256 * k:256 * (k + 1)]
        outs.append(v * lax.rsqrt(_mean1(v * v) + RMS_EPS))
    return (jnp.concatenate(outs, axis=1) * g,)


def _fn_ssd_post_bwd(ctx, dyn, y, xs, z, dexp, g):
    yt = y + xs * dexp
    sig = _sigmoid(z)
    sz = z * sig
    yg = yt * sz
    dyh = dyn * g
    yh, dyg = [], []
    for k in range(2):
        sl = slice(256 * k, 256 * (k + 1))
        v = yg[:, sl]
        rs = lax.rsqrt(_mean1(v * v) + RMS_EPS)
        vh = v * rs
        yh.append(vh)
        dyg.append(rs * (dyh[:, sl] - vh * _mean1(dyh[:, sl] * vh)))
    yh = jnp.concatenate(yh, axis=1)
    dyg = jnp.concatenate(dyg, axis=1)
    dyt = dyg * sz
    dz = dyg * yt * (sig * (1.0 + z * (1.0 - sig)))
    return dyt, dz, dyt * dexp, _sum0(dyt * xs), _sum0(dyn * yh)


def _fn_mla_pre(ctx, ql, kvl, gq, gkv):
    return _rms_fwd(ql, gq), _rms_fwd(kvl, gkv)


def _fn_mla_pre_bwd(ctx, ql, kvl, dqn, dkvn_k, dkvn_v, gq, gkv):
    dql, dgq = _rms_bwd(ql, dqn, gq)
    dkvl, dgkv = _rms_bwd(kvl, dkvn_k + dkvn_v, gkv)
    return dql, dkvl, dgq, dgkv


def _fn_rope(ctx, qp, kn, kr, ta, tb, tc):
    kpe = _rope(kr, ta, tb, tc)
    qs, ks = [], []
    for h in range(MLA_HEADS):
        sl = slice(128 * h, 128 * (h + 1))
        qs.append(_rope(qp[:, sl], ta, tb, tc) * MLA_SCALE)
        ks.append(kn[:, sl] + kpe)
    return jnp.concatenate(qs, axis=1), jnp.concatenate(ks, axis=1)


def _fn_rope_bwd(ctx, dq, dk, ta, tb, tc):
    qs = []
    ksum = jnp.zeros_like(ta)
    for h in range(MLA_HEADS):
        sl = slice(128 * h, 128 * (h + 1))
        qs.append(_rope_bwd(dq[:, sl] * MLA_SCALE, ta, tb, tc))
        ksum = ksum + dk[:, sl]
    lane = _lane(ksum.shape)
    dkr = jnp.where((lane >= 64) & (lane < 96), _rope_bwd(ksum, ta, tb, tc), 0.0)
    return jnp.concatenate(qs, axis=1), dkr


MEM_SCALE = MEM_HEAD_DIM ** -0.5


def _mem_probs(qh, kh):
    s = _dot(qh, kh, NT) * MEM_SCALE
    p = jnp.exp(s - jnp.max(s, axis=1, keepdims=True))
    return p / jnp.sum(p, axis=1, keepdims=True)


def _fn_mem_fwd(ctx, q, km, vm):
    outs = []
    for h in range(MEM_HEADS):
        sl = slice(256 * h, 256 * (h + 1))
        outs.append(_dot(_mem_probs(q[:, sl], km[:, sl]), vm[:, sl]))
    return (jnp.concatenate(outs, axis=1),)


def _fn_mem_bwd(ctx, q, do, km, vm):
    dqs, dks, dvs = [], [], []
    for h in range(MEM_HEADS):
        sl = slice(256 * h, 256 * (h + 1))
        p = _mem_probs(q[:, sl], km[:, sl])
        dvs.append(_dot(p, do[:, sl], TN))
        dp = _dot(do[:, sl], vm[:, sl], NT)
        ds = p * (dp - jnp.sum(dp * p, axis=1, keepdims=True)) * MEM_SCALE
        dqs.append(_dot(ds, km[:, sl]))
        dks.append(_dot(ds, q[:, sl], TN))
    return jnp.concatenate(dqs, axis=1), jnp.concatenate(dks, axis=1), jnp.concatenate(dvs, axis=1)


def _fn_res_ln(ctx, h, r, g, b):
    xhat, _ = _ln_stats(ALPHA * h + r)
    y = xhat * g + b
    return y, y


def _fn_res_ln_bwd(ctx, h, r, d1, d2, g):
    xhat, rstd = _ln_stats(ALPHA * h + r)
    return _ln_bwd(xhat, rstd, ALPHA * d1 + d2, g)


def _fn_res2_ln(ctx, h, r1, r2, g, b):
    xhat, _ = _ln_stats(ALPHA * h + (r1 + r2))
    return (xhat * g + b,)


def _fn_res2_ln_bwd(ctx, h, r1, r2, d1, d2, g):
    xhat, rstd = _ln_stats(ALPHA * h + (r1 + r2))
    return _ln_bwd(xhat, rstd, ALPHA * d1 + d2, g)


def _fn_in_ln_bwd(ctx, x, d1, d2, g):
    xhat, rstd = _ln_stats(x)
    return _ln_bwd(xhat, rstd, ALPHA * d1 + d2, g)


def _fn_final(ctx, h2, ff, tgt, g, b):
    xhat, rstd = _ln_stats(ALPHA * h2 + ff)
    e = xhat * g + b - tgt
    loss = 0.5 * _sum0(jnp.sum(e * e, axis=1, keepdims=True)) / D_MODEL
    dx, dg, db = _ln_bwd(xhat, rstd, e / D_MODEL, g)
    return dx, dx, dg, db, loss


def _epi_du(da, u):
    return da * 2.0 * jnp.maximum(u, 0.0)


def _relu2(u):
    r = jnp.maximum(u, 0.0)
    return r * r


def _fn_conv_bwd_a(ctx, y, dxs1, dxs2, dbc, dtr, ddt, dtb):
    sig = _sigmoid(y)
    dact = jnp.concatenate([dxs1 + dxs2, dbc], axis=1)
    dyc = dact * (sig * (1.0 + y * (1.0 - sig)))
    ddtr = ddt * _sigmoid(dtr + dtb)
    return dyc, ddtr, _sum0(dyc), _sum0(ddtr)


def _fn_conv_bwd_b(ctx, d, dn, u, up, w8):
    first, last = ctx.i == 0, ctx.i == ctx.n - 1
    du = d * w8[3:4]
    row = lax.broadcasted_iota(jnp.int32, w8.shape, 0)
    dw = jnp.where(row == 3, _sum0(d * u), 0.0)
    for s in (1, 2, 3):
        du = du + _shift_up(d, dn, s, last) * w8[3 - s:4 - s]
        dw = dw + jnp.where(row == 3 - s, _sum0(d * _shift_down(u, up, s, first)), 0.0)
    return du, dw


def _fn_adam(ctx, w, g, m, v):
    m = ADAM_B1 * m + (1.0 - ADAM_B1) * g
    v = ADAM_B2 * v + (1.0 - ADAM_B2) * (g * g)
    m_hat = m / (1.0 - ADAM_B1 ** ADAM_STEP)
    v_hat = v / (1.0 - ADAM_B2 ** ADAM_STEP)
    return -ADAM_LR * (m_hat / (jnp.sqrt(v_hat) + ADAM_EPS) + ADAM_WD * w), m, v


def _fn_add2(ctx, a, b):
    s = a + b
    return s, s


def _fn_add4(ctx, a, r0, r1, r2):
    return (((a + r0.astype(F32)) + r1.astype(F32)) + r2.astype(F32),)


def _z(r, c, dt):
    return jnp.zeros((r, c), dt)


def _pad_w_in(w):
    r, dt = w.shape[0], w.dtype
    return jnp.concatenate([w[:, 512:1536], w[:, 0:512], w[:, 1544:1928], w[:, 1536:1544], _z(r, 120, dt),
                            w[:, 1928:2184], _z(r, 64, dt), w[:, 2184:2216], _z(r, 32, dt), _z(r, 128, dt)], axis=1)


def _unpad_w_in(d):
    return jnp.concatenate([d[:, 1024:1536], d[:, 0:1024], d[:, 1920:1928], d[:, 1536:1920], d[:, 2048:2304],
                            d[:, 2368:2400]], axis=1)


def _pad_heads(w, width):
    r = w.shape[0]
    w3 = w.reshape(r, MLA_HEADS, width)
    return jnp.pad(w3, ((0, 0), (0, 0), (0, 128 - width))).reshape(r, MLA_HEADS * 128)


def _pad_w_kv(w):
    r = w.shape[0]
    w4 = w.reshape(r, MLA_HEADS, 2, 64)
    return jnp.pad(w4, ((0, 0), (0, 0), (0, 0), (0, 64))).reshape(r, MLA_HEADS * 256)


def _unpad_w_kv(d):
    r = d.shape[0]
    return d.reshape(r, MLA_HEADS, 2, 128)[:, :, :, :64].reshape(r, MLA_HEADS * 128)


def _pad_w_mix(w):
    wo = jnp.pad(w[512:1024].reshape(MLA_HEADS, 64, D_MODEL), ((0, 0), (0, 64), (0, 0))).reshape(1024, D_MODEL)
    return jnp.concatenate([wo, w[0:512]], axis=0)


def _unpad_w_mix(d):
    do = d[:1024].reshape(MLA_HEADS, 128, D_MODEL)[:, :64].reshape(512, D_MODEL)
    return jnp.concatenate([d[1024:1536], do], axis=0)


def _row(v, width=None):
    v = v.reshape(1, -1).astype(F32)
    if width is not None and v.shape[1] < width:
        v = jnp.pad(v, ((0, 0), (0, width - v.shape[1])))
    return v


def _old_local_step(x, mem, positions, target, W, P):
    S = x.shape[0]
    tr = ROW_TILE
    w_in_p = _pad_w_in(W["w_in"])
    w_q_p = _pad_heads(W["w_q_up"], MLA_QK)
    w_kv3 = W["w_kv_up"].reshape(MLA_KV_RANK, MLA_HEADS, 128)
    w_k_p = _pad_heads(w_kv3[:, :, :64].reshape(MLA_KV_RANK, 512), 64)
    w_v_p = _pad_heads(w_kv3[:, :, 64:].reshape(MLA_KV_RANK, 512), 64)
    w_v_pt = w_v_p.T
    w_mix_y = W["w_mix_out"][0:512]
    w_mix_o = jnp.pad(W["w_mix_out"][512:1024].reshape(MLA_HEADS, 64, D_MODEL),
                      ((0, 0), (0, 64), (0, 0))).reshape(MLA_HEADS * 128, D_MODEL)
    conv_w8 = jnp.pad(P["conv_w"].astype(F32), ((0, 4), (0, 0)))
    conv_b = _row(P["conv_b"])
    dt_b = _row(P["dt_bias"], 128)
    a_head = -jnp.exp(P["a_log"].reshape(-1).astype(F32))
    a_row = _row(a_head, 128)
    dexp = jnp.repeat(P["d_skip"].reshape(-1).astype(F32), 64).reshape(1, 512)
    g_ssd, g_q, g_kv = _row(P["ssd_norm_g"]), _row(P["q_norm_g"]), _row(P["kv_norm_g"])
    g_in, b_in = _row(P["ln_in_g"]), _row(P["ln_in_b"])
    g1, b1, g2, b2, g3, b3 = (_row(P[k]) for k in ("ln1_g", "ln1_b", "ln2_g", "ln2_b", "ln3_g", "ln3_b"))

    half = MLA_ROPE // 2
    inv_freq = jnp.power(ROPE_THETA, -jnp.arange(half, dtype=F32) / half)
    ang = positions.reshape(S, 1).astype(F32) * inv_freq
    cos, sin = jnp.cos(ang), jnp.sin(ang)
    zc = lambda n: jnp.zeros((S, n), F32)
    rope_a = jnp.concatenate([jnp.ones((S, 64), F32), cos, cos, zc(32)], axis=1)
    rope_b = jnp.concatenate([zc(80), sin, zc(32)], axis=1)
    rope_c = jnp.concatenate([zc(64), -sin, zc(48)], axis=1)

    h0, h0_b = _rowwise(_fn_ln, [x], [g_in, b_in], [D_MODEL, (D_MODEL, BF16)], tr=tr, name="ln_in")
    proj = _mm(h0_b, w_in_p, form="nn", name="mm_in")
    conv_y, xbc, dt = _rowwise(
        _fn_conv_fwd, [(proj,) + SEG_XBC, ("prev", proj) + SEG_XBC, (proj,) + SEG_DT], [conv_w8, conv_b, dt_b],
        [1024, 1024, 128], tr=tr, name="conv_fwd")
    y_ssd, hs = _ssd_fwd(xbc, dt, a_row, name="ssd_fwd")
    (y_n,) = _rowwise(_fn_ssd_post, [y_ssd, (xbc, 0, 512), (proj,) + SEG_Z], [dexp, g_ssd], [(512, BF16)], tr=tr,
                      name="ssd_post")
    q_n, kv_n = _rowwise(_fn_mla_pre, [(proj,) + SEG_QLAT, (proj,) + SEG_KVLAT], [g_q, g_kv], [384, 256], tr=tr,
                         name="mla_pre")
    qp = _mm(q_n, w_q_p, form="nn", name="mm_q_up")
    kn = _mm(kv_n, w_k_p, form="nn", name="mm_k_up")
    v_nat = _mm(kv_n, w_v_p, form="nn", out_dtype=BF16, name="mm_v_up")
    v_t = _mm(w_v_pt, kv_n, form="nt", out_dtype=BF16, name="mm_v_up_t")
    q_rot, k_full = _rowwise(_fn_rope, [qp, kn, (proj,) + SEG_KR, rope_a, rope_b, rope_c], [],
                             [(1024, BF16), (1024, BF16)], tr=tr, name="rope")
    o_t, lse = _attn_fwd(q_rot, k_full, v_t, name="attn_fwd")
    mix_o = _mm(o_t, w_mix_o, form="tn", name="mm_mix_o")
    mix_y = _mm(y_n, w_mix_y, form="nn", name="mm_mix_y")
    (h1,) = _rowwise(_fn_res2_ln, [h0, mix_o, mix_y], [g1, b1], [D_MODEL], tr=tr, name="ln1")
    qm = _mm(h1, W["w_mem_q"], form="nn", name="mm_mem_q")
    km = _mm(mem, W["w_mem_k"], form="nn", name="mm_mem_k")
    vm = _mm(mem, W["w_mem_v"], form="nn", name="mm_mem_v")
    (om,) = _rowwise(_fn_mem_fwd, [qm], [km, vm], [(D_MODEL, BF16)], tr=tr, name="mem_fwd")
    xa = _mm(om, W["w_mem_o"], form="nn", name="mm_mem_o")
    h2, h2_b = _rowwise(_fn_res_ln, [h1, xa], [g2, b2], [D_MODEL, (D_MODEL, BF16)], tr=tr, name="ln2")
    u = _mm(h2, W["w_up"], form="nn", name="mm_up")
    ff = _mm(u, W["w_down"], form="nn", a_pro=_relu2, name="mm_down")

    dt3, dt3_b, dg3, db3, loss = _rowwise(_fn_final, [h2, ff, target], [g3, b3], [D_MODEL, (D_MODEL, BF16)],
                                   [(1, D_MODEL), (1, D_MODEL), (1, 128)], tr=tr, name="ln3_loss")
    da = _mm(dt3, W["w_down"], form="nt", name="mm_down_dx")
    dw_down = _mm(u, dt3, form="tn", a_pro=_relu2, name="mm_down_dw")
    dw_up = _mm(h2, du, form="tn", name="mm_up_dw")
    dh2 = _mm(du, W["w_up"], form="nt", name="mm_up_dx")
    dt2, dg2, db2 = _rowwise(_fn_res_ln_bwd, [h1, xa, dt3, dh2], [g2], [D_MODEL], [(1, D_MODEL)] * 2, tr=tr,
                             name="ln2_bwd")
    dom = _mm(dt2, W["w_mem_o"], form="nt", name="mm_mem_o_dx")
    dw_mem_o = _mm(om, dt2, form="tn", name="mm_mem_o_dw")
    dqm, dkm, dvm = _rowwise(_fn_mem_bwd, [qm, dom], [km, vm], [(D_MODEL, BF16)], [(256, D_MODEL)] * 2, tr=tr,
                             name="mem_bwd")
    dw_mem_q = _mm(h1, dqm, form="tn", name="mm_mem_q_dw")
    dw_mem_k = _mm(mem, dkm, form="tn", name="mm_mem_k_dw")
    dw_mem_v = _mm(mem, dvm, form="tn", name="mm_mem_v_dw")
    dh1 = _mm(dqm, W["w_mem_q"], form="nt", name="mm_mem_q_dx")
    dt1, dg1, db1 = _rowwise(_fn_res2_ln_bwd, [h0, mix_o, mix_y, dt2, dh1], [g1], [D_MODEL], [(1, D_MODEL)] * 2,
                             tr=tr, name="ln1_bwd")
    do_t = _mm(w_mix_o, dt1, form="nt", name="mm_mix_o_dx")
    dy_n = _mm(dt1, w_mix_y, form="nt", name="mm_mix_y_dx")
    dw_mix_o = _mm(o_t, dt1, form="nn", name="mm_mix_o_dw")
    dw_mix_y = _mm(y_n, dt1, form="tn", name="mm_mix_y_dw")
    dq_t, delta = _attn_bwd_dq(q_rot, k_full, k_full.T, v_nat, o_t, do_t, lse, name="attn_bwd_dq")
    dk_t, dv_t = _attn_bwd_dkv(q_rot, q_rot.T, k_full, v_nat, do_t, lse, delta, name="attn_bwd_dkv")
    dk = dk_t.T
    dqp, dkr = _rowwise(_fn_rope_bwd, [dq_t.T, dk, rope_a, rope_b, rope_c], [], [(1024, BF16), (128, BF16)], tr=tr,
                        name="rope_bwd")
    dw_q_p = _mm(q_n, dqp, form="tn", name="mm_q_up_dw")
    dq_n = _mm(dqp, w_q_p, form="nt", name="mm_q_up_dx")
    dw_k_p = _mm(kv_n, dk, form="tn", name="mm_k_up_dw")
    dkv_n1 = _mm(dk, w_k_p, form="nt", name="mm_k_up_dx")
    dw_v_pt = _mm(dv_t, kv_n, form="nn", name="mm_v_up_dw")
    dkv_n2 = _mm(dv_t, w_v_pt, form="tn", name="mm_v_up_dx")
    dq_lat, dkv_lat, dg_q, dg_kv = _rowwise(
        _fn_mla_pre_bwd, [(proj,) + SEG_QLAT, (proj,) + SEG_KVLAT, dq_n, dkv_n1, dkv_n2], [g_q, g_kv], [(384, BF16), (256, BF16)],
        [(1, 384), (1, 256)], tr=tr, name="mla_pre_bwd")
    dy_ssd, dz, dxs_skip, ddexp, dg_ssd = _rowwise(
        _fn_ssd_post_bwd, [dy_n, y_ssd, (xbc, 0, 512), (proj,) + SEG_Z], [dexp, g_ssd],
        [512, (512, BF16), 512], [(1, 512)] * 2, tr=tr, name="ssd_post_bwd")
    dxs, dbc, ddt, da_head = _ssd_bwd(xbc, dt, a_row, hs, dy_ssd, name="ssd_bwd")
    dyc, ddtr, dconv_b, ddt_b = _rowwise(
        _fn_conv_bwd_a, [conv_y, dxs, dxs_skip, dbc, (proj,) + SEG_DT, ddt], [dt_b], [1024, (128, BF16)],
        [(1, 1024), (1, 128)], tr=tr, name="conv_bwd_a")
    dxbc, dconv_w8 = _rowwise(
        _fn_conv_bwd_b, [dyc, ("next", dyc, 0, 1024), (proj,) + SEG_XBC, ("prev", proj) + SEG_XBC], [conv_w8], [(1024, BF16)],
        [(8, 1024)], tr=tr, name="conv_bwd_b")
    dproj = jnp.concatenate([dxbc, dz, dq_lat, ddtr, dkv_lat, dkr, jnp.zeros((S, 128), BF16)], axis=1)
    res = _mm(h0_b, dproj, form="tn", name="mm_in_dw", hosted=_pair_join_step(gp) if dist else None)
    dw_in_p, t_a = (res[0], res[1]) if dist else (res, None)
    big_b = _group_b_grads(dw_in_p, dw_q_p, dw_k_p, dw_v_pt, dconv_w8)
    q_b = None
    if dist:
        hb = PACK_B_ROWS // 2
        gp_b = _pack_group_b(big_b)
        dh0, theirs_b = _mm(dproj, w_in_p, form="nt", name="mm_in_dx", hosted=_pair_exchange_step(gp_b))
        mine_b = lax.dynamic_slice(gp_b, (0, c * hb, 0), (N_SHARD, hb, PACK_COLS))
        pf_b, pb_b = _rowwise(_fn_add2, [mine_b.reshape(-1, PACK_COLS), theirs_b.reshape(-1, PACK_COLS)], [],
                              [PACK_COLS, (PACK_COLS, BF16)], tr=512, name="pair_sum_b")
        q_b = (pf_b.reshape(N_SHARD, hb, PACK_COLS), pb_b.reshape(N_SHARD, hb, PACK_COLS))
    else:
        dh0 = _mm(dproj, w_in_p, form="nt", name="mm_in_dx")
    grad_x, dg_in, db_in = _rowwise(_fn_in_ln_bwd, [x, dt1, dh0], [g_in], [D_MODEL], [(1, D_MODEL)] * 2, tr=tr,
                                    name="ln_in_bwd")

    big = {
        "w_in": _unpad_w_in(dw_in_p),
        "w_q_up": dw_q_p.reshape(384, MLA_HEADS, 128)[:, :, :MLA_QK].reshape(384, MLA_HEADS * MLA_QK),
        "w_kv_up": jnp.concatenate([dw_k_p.reshape(MLA_KV_RANK, MLA_HEADS, 128)[:, :, :64],
                                    dw_v_pt.T.reshape(MLA_KV_RANK, MLA_HEADS, 128)[:, :, :64]], axis=2).reshape(
                                        MLA_KV_RANK, MLA_HEADS * 128),
        "w_mix_out": jnp.concatenate([dw_mix_y, dw_mix_o.reshape(MLA_HEADS, 128, D_MODEL)[:, :64].reshape(
            512, D_MODEL)], axis=0),
        "w_mem_q": dw_mem_q, "w_mem_k": dw_mem_k, "w_mem_v": dw_mem_v, "w_mem_o": dw_mem_o,
        "w_up": dw_up, "w_down": dw_down,
        "conv_w": dconv_w8[0:4],
    }
    small = {
        "ln_in_g": dg_in, "ln_in_b": db_in, "conv_b": dconv_b, "dt_bias": ddt_b[:, :8],
        "a_log": da_head[:, :8] * a_head.reshape(1, 8),
        "d_skip": ddexp.reshape(8, 64).sum(axis=1).reshape(1, 8),
        "ssd_norm_g": dg_ssd, "q_norm_g": dg_q, "kv_norm_g": dg_kv,
        "ln1_g": dg1, "ln1_b": db1, "ln2_g": dg2, "ln2_b": db2, "ln3_g": dg3, "ln3_b": db3,
    }
    return loss[0, 0], grad_x, big, small


BIG = {
    "w_in": (1024, 2216, 1), "w_q_up": (384, 768, 1), "w_kv_up": (256, 1024, 1), "w_mix_out": (1024, 1024, 0),
    "w_mem_q": (1024, 1024, 0), "w_mem_k": (1024, 1024, 0), "w_mem_v": (1024, 1024, 0), "w_mem_o": (1024, 1024, 0),
    "w_up": (1024, 4096, 1), "w_down": (4096, 1024, 0), "conv_w": (4, 1024, 1),
}
BIG_ORDER = list(BIG)
SMALL_ORDER = ["ln_in_g", "ln_in_b", "conv_b", "dt_bias", "a_log", "d_skip", "ssd_norm_g", "q_norm_g", "kv_norm_g",
               "ln1_g", "ln1_b", "ln2_g", "ln2_b", "ln3_g", "ln3_b"]
N_SHARD = 4
PACK_COLS = 1024
PACK_ROWS = 4032
HALF_ROWS = PACK_ROWS // 2
GATHER_CHUNKS = 3
CHIP_CHUNKS = 3
PAIR_CHUNKS = 4


def _shard_shape(name):
    r, c, ax = BIG[name]
    return (r // N_SHARD, c) if ax == 0 else (r, c // N_SHARD)


def _split_shards(name, full):
    r, c, ax = BIG[name]
    if ax == 0:
        return full.reshape(N_SHARD, -1)
    return full.reshape(r, N_SHARD, c // N_SHARD).transpose(1, 0, 2).reshape(N_SHARD, -1)


def _join_shards(name, parts):
    r, c, ax = BIG[name]
    if ax == 0:
        return parts.reshape(r, c)
    return parts.reshape(N_SHARD, r, c // N_SHARD).transpose(1, 0, 2).reshape(r, c)


HBM = pl.BlockSpec(memory_space=pl.ANY)


def _place():
    x, y, c = lax.axis_index("x"), lax.axis_index("y"), lax.axis_index("c")
    chips = [(1 - x, y), (x, 1 - y), (1 - x, 1 - y)]
    return x, y, c, chips


def _gather_weights(wp):
    R, C = wp.shape
    H = R // 2
    nq = GATHER_CHUNKS
    CH = H // nq

    def body(w_ref, out_ref, send_sems, recv_sems):
        x, y, c, chips = _place()
        sib = (x, y, 1 - c)

        def piece(k, hc, q):
            return out_ref.at[k, pl.ds(hc * H + q * CH, CH), :]

        def copy(j, src, dst, to):
            return pltpu.make_async_remote_copy(src_ref=src, dst_ref=dst, send_sem=send_sems.at[j],
                                                recv_sem=recv_sems.at[j], device_id=to, device_id_type=MESH)

        me = 2 * x + y
        sends = []
        for q in range(nq):
            for j, (px, py) in enumerate(chips):
                cp = copy(j * nq + q, w_ref.at[pl.ds(c * H + q * CH, CH), :], piece(me, c, q), (px, py, c))
                cp.start()
                sends.append(cp)
        fwds = []
        for q in range(nq):
            for j, (px, py) in enumerate(chips):
                k = 2 * px + py
                copy(j * nq + q, piece(k, c, q), piece(k, c, q), (px, py, c)).wait_recv()
                f = copy((3 + j) * nq + q, piece(k, c, q), piece(k, c, q), sib)
                f.start()
                fwds.append(f)
        for q in range(nq):
            for j, (px, py) in enumerate(chips):
                k = 2 * px + py
                copy((3 + j) * nq + q, piece(k, 1 - c, q), piece(k, 1 - c, q), sib).wait_recv()
        for cp in sends + fwds:
            cp.wait_send()

    out = pl.pallas_call(
        body, name="gather_weights", in_specs=[HBM], out_specs=HBM,
        out_shape=jax.ShapeDtypeStruct((N_SHARD, R, C), wp.dtype),
        scratch_shapes=[pltpu.SemaphoreType.DMA((6 * nq,)), pltpu.SemaphoreType.DMA((6 * nq,))],
    )(wp)
    me = 2 * lax.axis_index("x") + lax.axis_index("y")
    return lax.dynamic_update_slice(out, wp[None], (me, 0, 0))


def _pair_exchange(gp):
    n, R, C = gp.shape
    H = R // 2
    nq = PAIR_CHUNKS
    CH = H // nq

    def body(g_ref, theirs_ref, send_sems, recv_sems):
        x, y, c, _ = _place()
        swaps = []
        for k in range(n):
            for q in range(nq):
                cp = pltpu.make_async_remote_copy(
                    src_ref=g_ref.at[k, pl.ds((1 - c) * H + q * CH, CH), :], dst_ref=theirs_ref.at[k, pl.ds(q * CH, CH), :],
                    send_sem=send_sems.at[k * nq + q], recv_sem=recv_sems.at[k * nq + q], device_id=(x, y, 1 - c),
                    device_id_type=MESH)
                cp.start()
                swaps.append(cp)
        for cp in swaps:
            cp.wait()

    theirs = pl.pallas_call(
        body, name="pair_exchange", in_specs=[HBM], out_specs=HBM,
        out_shape=jax.ShapeDtypeStruct((n, H, C), gp.dtype),
        scratch_shapes=[pltpu.SemaphoreType.DMA((n * nq,)), pltpu.SemaphoreType.DMA((n * nq,))],
    )(gp)
    mine = lax.dynamic_slice(gp, (0, lax.axis_index("c") * H, 0), (n, H, C))
    return mine, theirs


def _chip_exchange(pb):
    n, H, C = pb.shape
    nq = CHIP_CHUNKS
    CH = H // nq

    def body(pb_ref, got_ref, send_sems, recv_sems):
        x, y, c, chips = _place()
        sends = []
        for q in range(nq):
            for j, (px, py) in enumerate(chips):
                cp = pltpu.make_async_remote_copy(
                    src_ref=pb_ref.at[2 * px + py, pl.ds(q * CH, CH), :], dst_ref=got_ref.at[j, pl.ds(q * CH, CH), :],
                    send_sem=send_sems.at[j * nq + q], recv_sem=recv_sems.at[j * nq + q],
                    device_id=(px, py, c), device_id_type=MESH)
                cp.start()
                sends.append(cp)
        for cp in sends:
            cp.wait()

    return pl.pallas_call(
        body, name="chip_exchange", in_specs=[HBM], out_specs=HBM,
        out_shape=jax.ShapeDtypeStruct((3, H, C), BF16),
        scratch_shapes=[pltpu.SemaphoreType.DMA((3 * nq,)), pltpu.SemaphoreType.DMA((3 * nq,))],
    )(pb)


def _pair_join(q):
    H, C = q.shape
    nq = PAIR_CHUNKS
    CH = H // nq

    def body(q_ref, theirs_ref, send_sems, recv_sems):
        x, y, c, _ = _place()
        pushes = []
        for j in range(nq):
            cp = pltpu.make_async_remote_copy(
                src_ref=q_ref.at[pl.ds(j * CH, CH), :], dst_ref=theirs_ref.at[pl.ds(j * CH, CH), :],
                send_sem=send_sems.at[j], recv_sem=recv_sems.at[j], device_id=(x, y, 1 - c), device_id_type=MESH)
            cp.start()
            pushes.append(cp)
        for cp in pushes:
            cp.wait()

    theirs = pl.pallas_call(
        body, name="pair_join", in_specs=[HBM], out_specs=HBM,
        out_shape=jax.ShapeDtypeStruct((H, C), F32),
        scratch_shapes=[pltpu.SemaphoreType.DMA((nq,)), pltpu.SemaphoreType.DMA((nq,))],
    )(q)
    c = lax.axis_index("c")
    out = jnp.zeros((2 * H, C), F32)
    out = lax.dynamic_update_slice(out, q, (c * H, 0))
    return lax.dynamic_update_slice(out, theirs, ((1 - c) * H, 0))


N_DEV = 8


def _small_all_reduce(g, step=None):
    r, cdim = g.shape
    si, so = (len(step.inputs), len(step.out_shapes)) if step else (0, 0)

    def body(g_ref, *refs):
        s_ins, out_ref, s_outs = refs[:si], refs[si], refs[si + 1:si + 1 + so]
        buf, send_sems, recv_sems = refs[si + 1 + so:si + 4 + so]
        s_sems = refs[si + 4 + so:]
        if step:
            step.start(s_ins, s_outs, s_sems)
        x, y, c, _ = _place()
        me = 4 * x + 2 * y + c
        buf[me] = g_ref[...]
        copies = []
        for d in range(1, N_DEV):
            to = me ^ d
            cp = pltpu.make_async_remote_copy(src_ref=g_ref, dst_ref=buf.at[me], send_sem=send_sems.at[d - 1],
                                              recv_sem=recv_sems.at[d - 1],
                                              device_id=(to // 4, (to // 2) % 2, to % 2), device_id_type=MESH)
            cp.start()
            copies.append(cp)
        for cp in copies:
            cp.wait()
        acc = buf[0]
        for d in range(1, N_DEV):
            acc = acc + buf[d]
        out_ref[...] = acc
        if step:
            step.finish(s_ins, s_outs, s_sems)

    res = pl.pallas_call(
        body, name="small_all_reduce",
        in_specs=[pl.BlockSpec(memory_space=pltpu.VMEM)] + [HBM] * si,
        out_specs=[pl.BlockSpec(memory_space=pltpu.VMEM)] + [HBM] * so,
        out_shape=[jax.ShapeDtypeStruct((r, cdim), F32)] + (list(step.out_shapes) if step else []),
        scratch_shapes=[pltpu.VMEM((N_DEV, r, cdim), F32), pltpu.SemaphoreType.DMA((N_DEV - 1,)),
                        pltpu.SemaphoreType.DMA((N_DEV - 1,))] + (_sem_scratch(step) if step else []),
    )(g, *(step.inputs if step else []))
    return res if step else res[0]


def _adam(w, g, m, v, name):
    shape = w.shape
    w2, g2, m2, v2 = (t.reshape(-1, shape[-1]) for t in (w, g, m, v))
    d, mn, vn = _rowwise(_fn_adam, [w2, g2, m2, v2], [], [shape[-1]] * 3, tr=256, name=name)
    return d.reshape(shape), mn.reshape(shape), vn.reshape(shape)


def _old_kernel(x, mem, positions, ln_in_g, ln_in_b, w_in, conv_w, conv_b, dt_bias, a_log, d_skip, ssd_norm_g, q_norm_g, w_q_up, kv_norm_g, w_kv_up, w_mix_out, ln1_g, ln1_b, w_mem_q, w_mem_k, w_mem_v, w_mem_o, ln2_g, ln2_b, w_up, w_down, ln3_g, ln3_b, loss_target, m_ln_in_g, m_ln_in_b, m_w_in, m_conv_w, m_conv_b, m_dt_bias, m_a_log, m_d_skip, m_ssd_norm_g, m_q_norm_g, m_w_q_up, m_kv_norm_g, m_w_kv_up, m_w_mix_out, m_ln1_g, m_ln1_b, m_w_mem_q, m_w_mem_k, m_w_mem_v, m_w_mem_o, m_ln2_g, m_ln2_b, m_w_up, m_w_down, m_ln3_g, m_ln3_b, v_ln_in_g, v_ln_in_b, v_w_in, v_conv_w, v_conv_b, v_dt_bias, v_a_log, v_d_skip, v_ssd_norm_g, v_q_norm_g, v_w_q_up, v_kv_norm_g, v_w_kv_up, v_w_mix_out, v_ln1_g, v_ln1_b, v_w_mem_q, v_w_mem_k, v_w_mem_v, v_w_mem_o, v_ln2_g, v_ln2_b, v_w_up, v_w_down, v_ln3_g, v_ln3_b):
    args = dict(locals())
    weights = BIG_ORDER + SMALL_ORDER

    flat = []
    for n in BIG_ORDER:
        s = args[n].reshape(-1)
        if n == "conv_w":
            flat.append(lax.bitcast_convert_type(s.astype(F32), BF16).reshape(-1))
        else:
            flat.append(s.astype(BF16))
    flat = jnp.concatenate(flat)
    wp = jnp.pad(flat, (0, PACK_ROWS * PACK_COLS - flat.shape[0])).reshape(PACK_ROWS, PACK_COLS)
    gathered = _gather_weights(wp).reshape(N_SHARD, -1)
    W, off = {}, 0
    for n in BIG_ORDER:
        sr, sc = _shard_shape(n)
        cnt = sr * sc
        if n == "conv_w":
            part = lax.bitcast_convert_type(gathered[:, off:off + 2 * cnt].reshape(N_SHARD, cnt, 2), F32)
            off += 2 * cnt
        else:
            part = gathered[:, off:off + cnt]
            off += cnt
        W[n] = _join_shards(n, part)
    P = {n: args[n] for n in SMALL_ORDER}
    P["conv_w"] = W.pop("conv_w")

    loss, grad_x, gbig, gsmall = _local_step(x[0], mem[0], positions[0], loss_target[0], W, P)
    loss = lax.psum(loss, ("x", "y", "c"))

    gflat = jnp.concatenate([_split_shards(n, gbig[n]) for n in BIG_ORDER], axis=1)
    gp = jnp.pad(gflat, ((0, 0), (0, PACK_ROWS * PACK_COLS - gflat.shape[1]))).reshape(N_SHARD, PACK_ROWS, PACK_COLS)
    mine, theirs = _pair_exchange(gp)
    pf, pb = _rowwise(_fn_add2, [mine.reshape(-1, PACK_COLS), theirs.reshape(-1, PACK_COLS)], [],
                      [PACK_COLS, (PACK_COLS, BF16)], tr=288, name="pair_sum")
    pf = pf.reshape(N_SHARD, HALF_ROWS, PACK_COLS)
    pb = pb.reshape(N_SHARD, HALF_ROWS, PACK_COLS)
    got = _chip_exchange(pb).reshape(3 * HALF_ROWS, PACK_COLS)
    own = lax.dynamic_index_in_dim(pf, 2 * lax.axis_index("x") + lax.axis_index("y"), axis=0, keepdims=False)
    (q,) = _rowwise(_fn_add4, [own] + [(got, 0, PACK_COLS, j * HALF_ROWS) for j in range(3)], [], [PACK_COLS],
                    tr=288, name="chip_sum", n_rows=HALF_ROWS)
    red = _pair_join(q).reshape(-1)

    gs = jnp.concatenate([_row(gsmall[n], PACK_COLS) for n in SMALL_ORDER] + [jnp.zeros((1, PACK_COLS), F32)], axis=0)
    gs = _small_all_reduce(gs)

    grads, deltas, new_m, new_v = {}, {}, {}, {}
    off = 0
    for n in BIG_ORDER:
        sr, sc = _shard_shape(n)
        g = red[off:off + sr * sc].reshape(args[n].shape)
        off += sr * sc
        grads[n] = g
        deltas[n], new_m[n], new_v[n] = _adam(args[n], g, args["m_" + n], args["v_" + n], "adam_" + n)
    pack = lambda pre: jnp.concatenate([_row(args[pre + n], PACK_COLS) for n in SMALL_ORDER]
                                       + [jnp.zeros((1, PACK_COLS), F32)], axis=0)
    ds, ms, vs = _rowwise(_fn_adam, [pack(""), gs, pack("m_"), pack("v_")], [], [PACK_COLS] * 3, tr=16,
                          name="adam_small")
    for i, n in enumerate(SMALL_ORDER):
        cnt = args[n].size
        take = lambda t: t[i, :cnt].reshape(args[n].shape)
        grads[n], deltas[n], new_m[n], new_v[n] = take(gs), take(ds), take(ms), take(vs)

    order = ["ln_in_g", "ln_in_b", "w_in", "conv_w", "conv_b", "dt_bias", "a_log", "d_skip", "ssd_norm_g",
             "q_norm_g", "w_q_up", "kv_norm_g", "w_kv_up", "w_mix_out", "ln1_g", "ln1_b", "w_mem_q", "w_mem_k",
             "w_mem_v", "w_mem_o", "ln2_g", "ln2_b", "w_up", "w_down", "ln3_g", "ln3_b"]
    assert sorted(order) == sorted(weights)
    return (loss, grad_x[None], *[grads[n] for n in order], *[deltas[n] for n in order],
            *[new_m[n] for n in order], *[new_v[n] for n in order])


PACK_A_ROW = {"w_down": 0, "w_up": 1024, "w_mem_q": 2048, "w_mem_k": 2304, "w_mem_v": 2560, "w_mem_o": 2816,
              "w_mix_out": 3072}
PACK_A_ORDER = list(PACK_A_ROW)
PACK_A_ROWS = 3328
PACK_B_ORDER = ["w_in", "w_q_up", "w_kv_up", "conv_w"]
PACK_B_ROWS = 704


def _mesh_pos():
    return 2 * lax.axis_index("x") + lax.axis_index("y"), lax.axis_index("c")


def _local_step(x, mem, positions, target, WB, P, *, wp_a=None, g_a=None):
    S = x.shape[0]
    tr = ROW_TILE
    dist = g_a is None
    w_in_p = _pad_w_in(WB["w_in"])
    w_q_p = _pad_heads(WB["w_q_up"], MLA_QK)
    w_kv3 = WB["w_kv_up"].reshape(MLA_KV_RANK, MLA_HEADS, 128)
    w_k_p = _pad_heads(w_kv3[:, :, :64].reshape(MLA_KV_RANK, 512), 64)
    w_v_p = _pad_heads(w_kv3[:, :, 64:].reshape(MLA_KV_RANK, 512), 64)
    w_v_pt = w_v_p.T
    conv_w8 = jnp.pad(P["conv_w"].astype(F32), ((0, 4), (0, 0)))
    conv_b = _row(P["conv_b"])
    dt_b = _row(P["dt_bias"], 128)
    a_head = -jnp.exp(P["a_log"].reshape(-1).astype(F32))
    a_row = _row(a_head, 128)
    dexp = jnp.repeat(P["d_skip"].reshape(-1).astype(F32), 64).reshape(1, 512)
    g_ssd, g_q, g_kv = _row(P["ssd_norm_g"]), _row(P["q_norm_g"]), _row(P["kv_norm_g"])
    g_in, b_in = _row(P["ln_in_g"]), _row(P["ln_in_b"])
    g1, b1, g2, b2, g3, b3 = (_row(P[k]) for k in ("ln1_g", "ln1_b", "ln2_g", "ln2_b", "ln3_g", "ln3_b"))

    half = MLA_ROPE // 2
    inv_freq = jnp.power(ROPE_THETA, -jnp.arange(half, dtype=F32) / half)
    ang = positions.reshape(S, 1).astype(F32) * inv_freq
    cos, sin = jnp.cos(ang), jnp.sin(ang)
    zc = lambda n: jnp.zeros((S, n), F32)
    rope_a = jnp.concatenate([jnp.ones((S, 64), F32), cos, cos, zc(32)], axis=1)
    rope_b = jnp.concatenate([zc(80), sin, zc(32)], axis=1)
    rope_c = jnp.concatenate([zc(64), -sin, zc(48)], axis=1)

    h0, h0_b = _rowwise(_fn_ln, [x], [g_in, b_in], [D_MODEL, (D_MODEL, BF16)], tr=tr, name="ln_in")
    proj = _mm(h0_b, w_in_p, form="nn", name="mm_in")
    conv_y, xbc, dt = _rowwise(
        _fn_conv_fwd, [(proj,) + SEG_XBC, ("prev", proj) + SEG_XBC, (proj,) + SEG_DT], [conv_w8, conv_b, dt_b],
        [1024, 1024, 128], tr=tr, name="conv_fwd")
    y_ssd, hs = _ssd_fwd(xbc, dt, a_row, name="ssd_fwd")
    (y_n,) = _rowwise(_fn_ssd_post, [y_ssd, (xbc, 0, 512), (proj,) + SEG_Z], [dexp, g_ssd], [(512, BF16)], tr=tr,
                      name="ssd_post")
    q_n, kv_n = _rowwise(_fn_mla_pre, [(proj,) + SEG_QLAT, (proj,) + SEG_KVLAT], [g_q, g_kv], [384, 256], tr=tr,
                         name="mla_pre")
    qp = _mm(q_n, w_q_p, form="nn", name="mm_q_up")
    kn = _mm(kv_n, w_k_p, form="nn", name="mm_k_up")
    v_nat = _mm(kv_n, w_v_p, form="nn", out_dtype=BF16, name="mm_v_up")
    v_t = _mm(w_v_pt, kv_n, form="nt", out_dtype=BF16, name="mm_v_up_t")
    q_rot, k_full = _rowwise(_fn_rope, [qp, kn, (proj,) + SEG_KR, rope_a, rope_b, rope_c], [],
                             [(1024, BF16), (1024, BF16)], tr=tr, name="rope")
    res = _attn_fwd(q_rot, k_full, v_t, name="attn_fwd", hosted=_gather_step(wp_a) if dist else None)
    o_t, lse = res[0], res[1]
    if dist:
        g_a = lax.dynamic_update_slice(res[2], wp_a[None], (_mesh_pos()[0], 0, 0))
    r_mix = PACK_A_ROW["w_mix_out"]
    w_mix_o = jnp.pad(g_a[2:4, r_mix:r_mix + 256].reshape(MLA_HEADS, 64, D_MODEL),
                      ((0, 0), (0, 64), (0, 0))).reshape(MLA_HEADS * 128, D_MODEL)
    mix_o = _mm(o_t, w_mix_o, form="tn", name="mm_mix_o")
    mix_y = _mm(y_n, g_a, form="nn", b_pack="w_mix_out", name="mm_mix_y")
    (h1,) = _rowwise(_fn_res2_ln, [h0, mix_o, mix_y], [g1, b1], [D_MODEL], tr=tr, name="ln1")
    qm = _mm(h1, g_a, form="nn", b_pack="w_mem_q", out_dtype=BF16, name="mm_mem_q")
    km = _mm(mem, g_a, form="nn", b_pack="w_mem_k", out_dtype=BF16, name="mm_mem_k")
    vm = _mm(mem, g_a, form="nn", b_pack="w_mem_v", out_dtype=BF16, name="mm_mem_v")
    (om,) = _rowwise(_fn_mem_fwd, [qm], [km, vm], [(D_MODEL, BF16)], tr=tr, name="mem_fwd")
    xa = _mm(om, g_a, form="nn", b_pack="w_mem_o", name="mm_mem_o")
    h2, h2_b = _rowwise(_fn_res_ln, [h1, xa], [g2, b2], [D_MODEL, (D_MODEL, BF16)], tr=tr, name="ln2")
    u = _mm(h2_b, g_a, form="nn", b_pack="w_up", name="mm_up")
    ff = _mm(u, g_a, form="nn", a_pro=_relu2, b_pack="w_down", name="mm_down")

    gp = lax.empty((N_SHARD, PACK_A_ROWS, PACK_COLS), F32)
    dt3, dt3_b, dg3, db3, loss = _rowwise(_fn_final, [h2, ff, target], [g3, b3], [D_MODEL, (D_MODEL, BF16)],
                                   [(1, D_MODEL), (1, D_MODEL), (1, 128)], tr=tr, name="ln3_loss")
    du = _mm(dt3_b, g_a, form="nt", b_pack="w_down", epi=(_epi_du, u), out_dtype=BF16, name="mm_down_dx")
    gp = _mm(u, dt3_b, form="tn", a_pro=_relu2, out_pack=("w_down", gp), name="mm_down_dw")
    gp = _mm(h2_b, du, form="tn", out_pack=("w_up", gp), name="mm_up_dw")
    dh2 = _mm(du, g_a, form="nt", b_pack="w_up", name="mm_up_dx")
    dt2, dg2, db2 = _rowwise(_fn_res_ln_bwd, [h1, xa, dt3, dh2], [g2], [D_MODEL], [(1, D_MODEL)] * 2, tr=tr,
                             name="ln2_bwd")
    dom = _mm(dt2, g_a, form="nt", b_pack="w_mem_o", out_dtype=BF16, name="mm_mem_o_dx")
    gp = _mm(om, dt2, form="tn", out_pack=("w_mem_o", gp), name="mm_mem_o_dw")
    dqm, dkm, dvm = _rowwise(_fn_mem_bwd, [qm, dom], [km, vm], [(D_MODEL, BF16)], [(256, D_MODEL)] * 2, tr=tr,
                             name="mem_bwd")
    gp = _mm(h1, dqm, form="tn", out_pack=("w_mem_q", gp), name="mm_mem_q_dw")
    gp = _mm(mem, dkm, form="tn", out_pack=("w_mem_k", gp), name="mm_mem_k_dw")
    gp = _mm(mem, dvm, form="tn", out_pack=("w_mem_v", gp), name="mm_mem_v_dw")
    dh1 = _mm(dqm, g_a, form="nt", b_pack="w_mem_q", name="mm_mem_q_dx")
    dt1, dg1, db1 = _rowwise(_fn_res2_ln_bwd, [h0, mix_o, mix_y, dt2, dh1], [g1], [D_MODEL], [(1, D_MODEL)] * 2,
                             tr=tr, name="ln1_bwd")
    do_t = _mm(w_mix_o, dt1, form="nt", name="mm_mix_o_dx")
    dy_n = _mm(dt1, g_a, form="nt", b_pack="w_mix_out", b_rows=512, name="mm_mix_y_dx")
    dw_mix_o = _mm(o_t, dt1, form="nn", name="mm_mix_o_dw")
    gp = _mm(y_n, dt1, form="tn", out_pack=("w_mix_out", gp), name="mm_mix_y_dw")
    gp = lax.dynamic_update_slice(
        gp, dw_mix_o.reshape(MLA_HEADS, 128, D_MODEL)[:, :64].reshape(2, 256, D_MODEL), (2, r_mix, 0))
    me, c = _mesh_pos() if dist else (0, 0)
    ha = PACK_A_ROWS // 2
    res = _attn_bwd_dq(q_rot, k_full, v_nat, o_t, do_t, lse, name="attn_bwd_dq",
                       hosted=_pair_exchange_step(gp) if dist else None)
    dq_rot, delta = res[0], res[1]
    chip_step = None
    if dist:
        mine = lax.dynamic_slice(gp, (0, c * ha, 0), (N_SHARD, ha, PACK_COLS))
        pf, pb = _rowwise(_fn_add2, [mine.reshape(-1, PACK_COLS), res[2].reshape(-1, PACK_COLS)], [],
                          [PACK_COLS, (PACK_COLS, BF16)], tr=512, name="pair_sum_a")
        chip_step = _chip_exchange_step(pb.reshape(N_SHARD, ha, PACK_COLS))
    res = _attn_bwd_dkv(q_rot, k_full, v_nat, do_t, lse, delta, name="attn_bwd_dkv", hosted=chip_step)
    dk, dv_t = res[0], res[1]
    if dist:
        own = lax.dynamic_index_in_dim(pf.reshape(N_SHARD, ha, PACK_COLS), me, axis=0, keepdims=False)
        got = res[2].reshape(3 * ha, PACK_COLS)
        (gp,) = _rowwise(_fn_add4, [own] + [(got, 0, PACK_COLS, j * ha) for j in range(3)], [], [PACK_COLS],
                         tr=512, name="chip_sum_a", n_rows=ha)
    dqp, dkr = _rowwise(_fn_rope_bwd, [dq_rot, dk, rope_a, rope_b, rope_c], [], [(1024, BF16), (128, BF16)], tr=tr,
                        name="rope_bwd")
    dw_q_p = _mm(q_n, dqp, form="tn", name="mm_q_up_dw")
    dq_n = _mm(dqp, w_q_p, form="nt", name="mm_q_up_dx")
    dw_k_p = _mm(kv_n, dk, form="tn", name="mm_k_up_dw")
    dkv_n1 = _mm(dk, w_k_p, form="nt", name="mm_k_up_dx")
    dw_v_pt = _mm(dv_t, kv_n, form="nn", name="mm_v_up_dw")
    dkv_n2 = _mm(dv_t, w_v_pt, form="tn", name="mm_v_up_dx")
    dq_lat, dkv_lat, dg_q, dg_kv = _rowwise(
        _fn_mla_pre_bwd, [(proj,) + SEG_QLAT, (proj,) + SEG_KVLAT, dq_n, dkv_n1, dkv_n2], [g_q, g_kv], [(384, BF16), (256, BF16)],
        [(1, 384), (1, 256)], tr=tr, name="mla_pre_bwd")
    dy_ssd, dz, dxs_skip, ddexp, dg_ssd = _rowwise(
        _fn_ssd_post_bwd, [dy_n, y_ssd, (xbc, 0, 512), (proj,) + SEG_Z], [dexp, g_ssd],
        [512, (512, BF16), 512], [(1, 512)] * 2, tr=tr, name="ssd_post_bwd")
    dxs, dbc, ddt, da_head = _ssd_bwd(xbc, dt, a_row, hs, dy_ssd, name="ssd_bwd")
    dyc, ddtr, dconv_b, ddt_b = _rowwise(
        _fn_conv_bwd_a, [conv_y, dxs, dxs_skip, dbc, (proj,) + SEG_DT, ddt], [dt_b], [1024, (128, BF16)],
        [(1, 1024), (1, 128)], tr=tr, name="conv_bwd_a")
    dxbc, dconv_w8 = _rowwise(
        _fn_conv_bwd_b, [dyc, ("next", dyc, 0, 1024), (proj,) + SEG_XBC, ("prev", proj) + SEG_XBC], [conv_w8], [(1024, BF16)],
        [(8, 1024)], tr=tr, name="conv_bwd_b")
    dproj = jnp.concatenate([dxbc, dz, dq_lat, ddtr, dkv_lat, dkr, jnp.zeros((S, 128), BF16)], axis=1)
    res = _mm(h0_b, dproj, form="tn", name="mm_in_dw", hosted=_pair_join_step(gp) if dist else None)
    dw_in_p, t_a = (res[0], res[1]) if dist else (res, None)
    big_b = _group_b_grads(dw_in_p, dw_q_p, dw_k_p, dw_v_pt, dconv_w8)
    q_b = None
    if dist:
        hb = PACK_B_ROWS // 2
        gp_b = _pack_group_b(big_b)
        dh0, theirs_b = _mm(dproj, w_in_p, form="nt", name="mm_in_dx", hosted=_pair_exchange_step(gp_b))
        mine_b = lax.dynamic_slice(gp_b, (0, c * hb, 0), (N_SHARD, hb, PACK_COLS))
        pf_b, pb_b = _rowwise(_fn_add2, [mine_b.reshape(-1, PACK_COLS), theirs_b.reshape(-1, PACK_COLS)], [],
                              [PACK_COLS, (PACK_COLS, BF16)], tr=512, name="pair_sum_b")
        q_b = (pf_b.reshape(N_SHARD, hb, PACK_COLS), pb_b.reshape(N_SHARD, hb, PACK_COLS))
    else:
        dh0 = _mm(dproj, w_in_p, form="nt", name="mm_in_dx")
    grad_x, dg_in, db_in = _rowwise(_fn_in_ln_bwd, [x, dt1, dh0], [g_in], [D_MODEL], [(1, D_MODEL)] * 2, tr=tr,
                                    name="ln_in_bwd")

    small = {
        "ln_in_g": dg_in, "ln_in_b": db_in, "conv_b": dconv_b, "dt_bias": ddt_b[:, :8],
        "a_log": da_head[:, :8] * a_head.reshape(1, 8),
        "d_skip": ddexp.reshape(8, 64).sum(axis=1).reshape(1, 8),
        "ssd_norm_g": dg_ssd, "q_norm_g": dg_q, "kv_norm_g": dg_kv,
        "ln1_g": dg1, "ln1_b": db1, "ln2_g": dg2, "ln2_b": db2, "ln3_g": dg3, "ln3_b": db3,
    }
    return loss[0, 0], grad_x, (gp, t_a, q_b), big_b, small


def _group_b_grads(dw_in_p, dw_q_p, dw_k_p, dw_v_pt, dconv_w8):
    return {
        "w_in": _unpad_w_in(dw_in_p),
        "w_q_up": dw_q_p.reshape(384, MLA_HEADS, 128)[:, :, :MLA_QK].reshape(384, MLA_HEADS * MLA_QK),
        "w_kv_up": jnp.concatenate([dw_k_p.reshape(MLA_KV_RANK, MLA_HEADS, 128)[:, :, :64],
                                    dw_v_pt.T.reshape(MLA_KV_RANK, MLA_HEADS, 128)[:, :, :64]], axis=2).reshape(
                                        MLA_KV_RANK, MLA_HEADS * 128),
        "conv_w": dconv_w8[0:4],
    }


def _pack_group_b(big_b):
    gflat = [_split_shards(n, big_b[n]) for n in PACK_B_ORDER]
    used = sum(f.shape[1] for f in gflat)
    gflat.append(jnp.zeros((N_SHARD, PACK_B_ROWS * PACK_COLS - used), F32))
    return jnp.concatenate(gflat, axis=1).reshape(N_SHARD, PACK_B_ROWS, PACK_COLS)


def _reduce_scatter(gp, tag):
    n, R, C = gp.shape
    H = R // 2
    me, c = _mesh_pos()
    (theirs,) = _run_step(_pair_exchange_step(gp), "pair_exchange_" + tag)
    mine = lax.dynamic_slice(gp, (0, c * H, 0), (n, H, C))
    pf, pb = _rowwise(_fn_add2, [mine.reshape(-1, C), theirs.reshape(-1, C)], [], [C, (C, BF16)], tr=512,
                      name="pair_sum_" + tag)
    (got,) = _run_step(_chip_exchange_step(pb.reshape(n, H, C)), "chip_exchange_" + tag)
    own = lax.dynamic_index_in_dim(pf.reshape(n, H, C), me, axis=0, keepdims=False)
    got = got.reshape(3 * H, C)
    (q,) = _rowwise(_fn_add4, [own] + [(got, 0, C, j * H) for j in range(3)], [], [C], tr=512,
                    name="chip_sum_" + tag, n_rows=H)
    return q


def _adam(w, g, m, v, name):
    shape = w.shape
    w2, m2, v2 = (t.reshape(-1, shape[-1]) for t in (w, m, v))
    g2 = (g[0], 0, shape[-1], g[1]) if isinstance(g, tuple) else g.reshape(-1, shape[-1])
    d, mn, vn = _rowwise(_fn_adam, [w2, g2, m2, v2], [], [shape[-1]] * 3, tr=256, name=name)
    return d.reshape(shape), mn.reshape(shape), vn.reshape(shape)


def kernel(x, mem, positions, ln_in_g, ln_in_b, w_in, conv_w, conv_b, dt_bias, a_log, d_skip, ssd_norm_g, q_norm_g, w_q_up, kv_norm_g, w_kv_up, w_mix_out, ln1_g, ln1_b, w_mem_q, w_mem_k, w_mem_v, w_mem_o, ln2_g, ln2_b, w_up, w_down, ln3_g, ln3_b, loss_target, m_ln_in_g, m_ln_in_b, m_w_in, m_conv_w, m_conv_b, m_dt_bias, m_a_log, m_d_skip, m_ssd_norm_g, m_q_norm_g, m_w_q_up, m_kv_norm_g, m_w_kv_up, m_w_mix_out, m_ln1_g, m_ln1_b, m_w_mem_q, m_w_mem_k, m_w_mem_v, m_w_mem_o, m_ln2_g, m_ln2_b, m_w_up, m_w_down, m_ln3_g, m_ln3_b, v_ln_in_g, v_ln_in_b, v_w_in, v_conv_w, v_conv_b, v_dt_bias, v_a_log, v_d_skip, v_ssd_norm_g, v_q_norm_g, v_w_q_up, v_kv_norm_g, v_w_kv_up, v_w_mix_out, v_ln1_g, v_ln1_b, v_w_mem_q, v_w_mem_k, v_w_mem_v, v_w_mem_o, v_ln2_g, v_ln2_b, v_w_up, v_w_down, v_ln3_g, v_ln3_b):
    args = dict(locals())
    me, c = _mesh_pos()

    wp_a = jnp.concatenate([args[n].reshape(-1, PACK_COLS).astype(BF16) for n in PACK_A_ORDER], axis=0)
    flat = [args[n].reshape(-1).astype(BF16) for n in PACK_B_ORDER[:-1]]
    flat.append(lax.bitcast_convert_type(conv_w.reshape(-1), BF16).reshape(-1))
    used = sum(f.shape[0] for f in flat)
    flat.append(jnp.zeros((PACK_B_ROWS * PACK_COLS - used,), BF16))
    wp_b = jnp.concatenate(flat).reshape(PACK_B_ROWS, PACK_COLS)

    (g_b,) = _run_step(_gather_step(wp_b), "gather_b")
    g_b = lax.dynamic_update_slice(g_b, wp_b[None], (me, 0, 0)).reshape(N_SHARD, -1)
    WB, off = {}, 0
    for n in PACK_B_ORDER:
        sr, sc = _shard_shape(n)
        cnt = sr * sc
        if n == "conv_w":
            part = lax.bitcast_convert_type(g_b[:, off:off + 2 * cnt].reshape(N_SHARD, cnt, 2), F32)
            off += 2 * cnt
        else:
            part = g_b[:, off:off + cnt]
            off += cnt
        WB[n] = _join_shards(n, part)
    P = {n: args[n] for n in SMALL_ORDER}
    P["conv_w"] = WB.pop("conv_w")

    loss, grad_x, (q_a, t_a, (pf_b, pb_b)), _, gsmall = _local_step(x[0], mem[0], positions[0], loss_target[0], WB,
                                                                     P, wp_a=wp_a)
    loss = lax.psum(loss, ("x", "y", "c"))

    gs = jnp.concatenate([_row(gsmall[n], PACK_COLS) for n in SMALL_ORDER] + [jnp.zeros((1, PACK_COLS), F32)], axis=0)
    gs, got_b = _small_all_reduce(gs, _chip_exchange_step(pb_b))
    hb = PACK_B_ROWS // 2
    own_b = lax.dynamic_index_in_dim(pf_b, me, axis=0, keepdims=False)
    got_b = got_b.reshape(3 * hb, PACK_COLS)
    (q_b,) = _rowwise(_fn_add4, [own_b] + [(got_b, 0, PACK_COLS, j * hb) for j in range(3)], [], [PACK_COLS],
                      tr=512, name="chip_sum_b", n_rows=hb)
    (t_b,) = _run_step(_pair_join_step(q_b), "pair_join_b")
    red = jnp.where(c == 0, jnp.concatenate([q_a, t_a, q_b, t_b], axis=0), jnp.concatenate([t_a, q_a, t_b, q_b], axis=0))

    grads, deltas, new_m, new_v = {}, {}, {}, {}
    for n in PACK_A_ORDER:
        r0, (sr, _) = PACK_A_ROW[n], _shard_shape(n)
        grads[n] = red[r0:r0 + sr].reshape(args[n].shape)
        deltas[n], new_m[n], new_v[n] = _adam(args[n], (red, r0), args["m_" + n], args["v_" + n], "adam_" + n)
    red_b = red[PACK_A_ROWS:].reshape(-1)
    off = 0
    for n in PACK_B_ORDER:
        sr, sc = _shard_shape(n)
        grads[n] = red_b[off:off + sr * sc].reshape(args[n].shape)
        off += sr * sc
        deltas[n], new_m[n], new_v[n] = _adam(args[n], grads[n], args["m_" + n], args["v_" + n], "adam_" + n)
    pack = lambda pre: jnp.concatenate([_row(args[pre + n], PACK_COLS) for n in SMALL_ORDER]
                                       + [jnp.zeros((1, PACK_COLS), F32)], axis=0)
    ds, ms, vs = _rowwise(_fn_adam, [pack(""), gs, pack("m_"), pack("v_")], [], [PACK_COLS] * 3, tr=16,
                          name="adam_small")
    for i, n in enumerate(SMALL_ORDER):
        cnt = args[n].size
        take = lambda t: t[i, :cnt].reshape(args[n].shape)
        grads[n], deltas[n], new_m[n], new_v[n] = take(gs), take(ds), take(ms), take(vs)

    order = ["ln_in_g", "ln_in_b", "w_in", "conv_w", "conv_b", "dt_bias", "a_log", "d_skip", "ssd_norm_g",
             "q_norm_g", "w_q_up", "kv_norm_g", "w_kv_up", "w_mix_out", "ln1_g", "ln1_b", "w_mem_q", "w_mem_k",
             "w_mem_v", "w_mem_o", "ln2_g", "ln2_b", "w_up", "w_down", "ln3_g", "ln3_b"]
    return (loss, grad_x[None], *[grads[n] for n in order], *[deltas[n] for n in order],
            *[new_m[n] for n in order], *[new_v[n] for n in order])
```

```python
import functools
import math

import jax
import jax.numpy as jnp
import numpy as np
from jax import lax
from jax.experimental import pallas as pl
from jax.experimental.pallas import tpu as pltpu

F32 = jnp.float32
BF16 = jnp.bfloat16
MESH = pl.DeviceIdType.MESH

D_MODEL = 1024
SSD_HEADS = 8
SSD_INNER = 512
SSD_CHUNK = 128
SSD_STATE = 128
MLA_HEADS = 8
MLA_NOPE = 64
MLA_ROPE = 32
MLA_QK = 96
MLA_Q_RANK = 384
MLA_KV_RANK = 256
ROPE_THETA = 10000.0
MEM_HEADS = 4
MEM_HEAD_DIM = 256
LN_EPS = 1e-5
RMS_EPS = 1e-6
ALPHA = 2.0 ** 0.25
ADAM_LR = 0.001
ADAM_B1 = 0.9
ADAM_B2 = 0.999
ADAM_EPS = 1e-08
ADAM_WD = 0.01
ADAM_STEP = 10

LANES = 128
IN_W = 2560
SEG_XBC = (0, 1024)
SEG_Z = (1024, 512)
SEG_QLAT = (1536, 384)
SEG_DT = (1920, 128)
SEG_KVLAT = (2048, 256)
SEG_KR = (2304, 128)
VMEM_LIMIT = 56 * 1024 * 1024
ATTN_TILE = 512
ROW_TILE = 256
NEG = -1e30

NN = (((1,), (0,)), ((), ()))
NT = (((1,), (1,)), ((), ()))
TN = (((0,), (0,)), ((), ()))


def _dot(a, b, dims=NN):
    return lax.dot_general(a.astype(BF16), b.astype(BF16), dims, preferred_element_type=F32)


def _dot_exact(a, b):
    return lax.dot_general(a, b, NN, precision=lax.Precision.HIGHEST, preferred_element_type=F32)


def _pick(dim, pref):
    t = min(pref, dim)
    t -= t % LANES
    while t >= LANES:
        if dim % t == 0:
            return t
        t -= LANES
    return dim


def _params(sem):
    return pltpu.CompilerParams(dimension_semantics=sem, vmem_limit_bytes=VMEM_LIMIT)


def _pack_caps(wname):
    r, c, ax = BIG[wname]
    if ax == 0:
        return (r if r <= 1024 else r // N_SHARD), c
    return r, c // N_SHARD


def _pack_block(wname, br, bc):
    r, c, ax = BIG[wname]
    r0 = PACK_A_ROW[wname]
    sr = r // N_SHARD if ax == 0 else r
    if ax == 0 and br > sr:
        assert br % sr == 0 and r0 % sr == 0
        return (br // sr, sr, bc), lambda rb, cb: (rb, r0 // sr, cb)
    assert r0 % br == 0
    if ax == 0:
        per = sr // br
        return (1, br, bc), lambda rb, cb: (rb // per, r0 // br + rb % per, cb)
    per = (c // N_SHARD) // bc
    return (1, br, bc), lambda rb, cb: (cb // per, r0 // br + rb, cb % per)


def _mm(a, b, *, form, name, a_pro=None, epi=None, out_dtype=F32, tm=1024, tn=1024, tk=1024, b_pack=None,
        b_rows=None, out_pack=None, hosted=None):
    b_shape = BIG[b_pack][:2] if b_pack else b.shape
    if b_pack and form == "nt":
        b_shape = (b_rows or b_shape[0], b_shape[1])
    if form == "nn":
        (m, k), (_, n) = a.shape, b_shape
    elif form == "nt":
        (m, k), (n, _) = a.shape, b_shape
    else:
        (k, m), (_, n) = a.shape, b_shape
    if b_pack:
        rcap, ccap = _pack_caps(b_pack)
        tk, tn = (min(tk, rcap), min(tn, ccap)) if form == "nn" else (min(tk, ccap), min(tn, rcap))
    if out_pack:
        rcap, ccap = _pack_caps(out_pack[0])
        tm, tn = min(tm, rcap), min(tn, ccap)
    tm, tn, tk = _pick(m, tm), _pick(n, tn), _pick(k, tk)
    dims = {"nn": NN, "nt": NT, "tn": TN}[form]
    nk = k // tk
    direct = out_dtype == F32 and epi is None
    n_extra = (1 if epi else 0) + (1 if out_pack else 0)

    def body(a_ref, b_ref, *rest):
        o_ref = rest[n_extra]
        acc_ref = o_ref if direct else rest[-1]

        @pl.when(pl.program_id(2) == 0)
        def _():
            acc_ref[...] = jnp.zeros_like(acc_ref)

        av = a_ref[...]
        if a_pro is not None:
            av = a_pro(av)
        bv = b_ref[...]
        acc_ref[...] += _dot(av, bv.reshape(-1, bv.shape[-1]), dims).reshape(acc_ref.shape)
        if not direct:
            @pl.when(pl.program_id(2) == nk - 1)
            def _():
                val = acc_ref[...]
                if epi is not None:
                    val = epi[0](val, rest[0][...])
                o_ref[...] = val.reshape(o_ref.shape).astype(o_ref.dtype)

    if form == "tn":
        a_spec = pl.BlockSpec((tk, tm), lambda i, j, kk: (kk, i))
    else:
        a_spec = pl.BlockSpec((tm, tk), lambda i, j, kk: (i, kk))
    if b_pack:
        shape, idx = _pack_block(b_pack, *((tk, tn) if form == "nn" else (tn, tk)))
        b_spec = pl.BlockSpec(shape, (lambda i, j, kk: idx(kk, j)) if form == "nn" else (lambda i, j, kk: idx(j, kk)))
    elif form == "nt":
        b_spec = pl.BlockSpec((tn, tk), lambda i, j, kk: (j, kk))
    else:
        b_spec = pl.BlockSpec((tk, tn), lambda i, j, kk: (kk, j))
    in_specs, args = [a_spec, b_spec], [a, b]
    out_spec = pl.BlockSpec((tm, tn), lambda i, j, kk: (i, j))
    out_sds, aliases = jax.ShapeDtypeStruct((m, n), out_dtype), {}
    if epi is not None:
        in_specs.append(out_spec)
        args.append(epi[1])
    if out_pack:
        wname, buf = out_pack
        shape, idx = _pack_block(wname, tm, tn)
        out_spec = pl.BlockSpec(shape, lambda i, j, kk: idx(i, j))
        out_sds, aliases = jax.ShapeDtypeStruct(buf.shape, buf.dtype), {len(args): 0}
        in_specs.append(HBM)
        args.append(buf)
    acc_shape = out_spec.block_shape if out_pack else (tm, tn)
    res = _call_with_step(
        body, hosted, None, args, name=name, grid=(m // tm, n // tn, nk), in_specs=in_specs, out_specs=[out_spec],
        out_shape=[out_sds], sem=("parallel", "parallel", "arbitrary"), aliases=aliases,
        scratch_shapes=[] if direct else [pltpu.VMEM(acc_shape, F32)])
    return res[0] if hosted is None else res


class _Ctx:
    def __init__(self, i, n):
        self.i, self.n = i, n


def _rowwise(fn, rows, consts, row_outs, acc_outs=(), *, tr, name, n_rows=None, hosted=None):
    norm = []
    for r in rows:
        kind = "tile"
        if isinstance(r, tuple) and isinstance(r[0], str):
            kind, r = r[0], r[1:]
        row0 = 0
        if isinstance(r, tuple) and len(r) == 4:
            r, row0 = r[:3], r[3]
        arr, col0, width = r if isinstance(r, tuple) else (r, 0, r.shape[1])
        assert col0 % width == 0
        norm.append((kind, arr, col0 // width, width, row0))
    n_rows = n_rows or next(a.shape[0] for k, a, _, _, _ in norm if k == "tile")
    tr = min(tr, n_rows)
    while n_rows % tr:
        tr -= 8
    n = n_rows // tr
    arrs, specs = [], []
    for kind, arr, cb, width, row0 in norm:
        if kind == "tile":
            assert row0 % tr == 0
            specs.append(pl.BlockSpec((tr, width), lambda i, cb=cb, rb=row0 // tr: (i + rb, cb)))
        elif kind == "prev":
            specs.append(pl.BlockSpec((8, width), lambda i, cb=cb: (jnp.maximum(i * (tr // 8) - 1, 0), cb)))
        else:
            specs.append(pl.BlockSpec((8, width), lambda i, cb=cb: (jnp.minimum((i + 1) * (tr // 8), n_rows // 8 - 1), cb)))
        arrs.append(arr)
    for c in consts:
        specs.append(pl.BlockSpec(c.shape, lambda i, nd=c.ndim: (0,) * nd))
        arrs.append(c)
    n_in, n_ro = len(arrs), len(row_outs)
    row_outs = [w if isinstance(w, tuple) else (w, F32) for w in row_outs]
    out_shape = [jax.ShapeDtypeStruct((n_rows, w), dt) for w, dt in row_outs]
    out_specs = [pl.BlockSpec((tr, w), lambda i: (i, 0)) for w, _ in row_outs]
    out_shape += [jax.ShapeDtypeStruct(s, F32) for s in acc_outs]
    out_specs += [pl.BlockSpec(s, lambda i: (0, 0)) for s in acc_outs]

    def body(*refs):
        i = pl.program_id(0)
        vals = [r[...] for r in refs[:n_in]]
        outs = fn(_Ctx(i, n), *vals)
        if not isinstance(outs, (tuple, list)):
            outs = (outs,)
        o_refs = refs[n_in:]
        for o_ref, o in zip(o_refs[:n_ro], outs[:n_ro]):
            o_ref[...] = o.astype(o_ref.dtype)
        if acc_outs:
            @pl.when(i == 0)
            def _():
                for o_ref in o_refs[n_ro:]:
                    o_ref[...] = jnp.zeros_like(o_ref)

            for o_ref, o in zip(o_refs[n_ro:], outs[n_ro:]):
                o_ref[...] += jnp.broadcast_to(o, o_ref.shape)

    return _call_with_step(body, hosted, None, arrs, name=name, grid=(n,), in_specs=specs, out_specs=out_specs,
                           out_shape=out_shape, sem=("arbitrary",))


def _sum0(v):
    return jnp.sum(v, axis=0, keepdims=True)


def _mean1(v):
    return jnp.mean(v, axis=-1, keepdims=True)


def _sigmoid(v):
    return 1.0 / (1.0 + jnp.exp(-v))


def _ln_stats(t):
    xc = t - _mean1(t)
    rstd = lax.rsqrt(_mean1(xc * xc) + LN_EPS)
    return xc * rstd, rstd


def _ln_bwd(xhat, rstd, dy, g):
    dxh = dy * g
    dx = rstd * (dxh - _mean1(dxh) - xhat * _mean1(dxh * xhat))
    return dx, _sum0(dy * xhat), _sum0(dy)


def _rms_fwd(v, g):
    return v * lax.rsqrt(_mean1(v * v) + RMS_EPS) * g


def _rms_bwd(v, dy, g):
    rs = lax.rsqrt(_mean1(v * v) + RMS_EPS)
    vh = v * rs
    dyg = dy * g
    return rs * (dyg - vh * _mean1(dyg * vh)), _sum0(dy * vh)


def _lane(shape):
    return lax.broadcasted_iota(jnp.int32, shape, len(shape) - 1)


def _shift_down(u, halo, s, is_first):
    tr = u.shape[0]
    rolled = pltpu.roll(u, s, 0)
    hr = jnp.where(is_first, 0.0, pltpu.roll(halo, s, 0))
    row = lax.broadcasted_iota(jnp.int32, hr.shape, 0)
    top = jnp.where(row < s, hr, rolled[0:8])
    if tr == 8:
        return top
    return jnp.concatenate([top, rolled[8:]], axis=0)


def _shift_up(d, halo, s, is_last):
    tr = d.shape[0]
    rolled = pltpu.roll(d, tr - s, 0)
    hr = jnp.where(is_last, 0.0, pltpu.roll(halo, 8 - s, 0))
    row = lax.broadcasted_iota(jnp.int32, hr.shape, 0)
    bot = jnp.where(row >= 8 - s, hr, rolled[tr - 8:])
    if tr == 8:
        return bot
    return jnp.concatenate([rolled[:tr - 8], bot], axis=0)


def _rope(v, ta, tb, tc):
    return v * ta + pltpu.roll(v, 16, 1) * tb + pltpu.roll(v, LANES - 16, 1) * tc


def _rope_bwd(d, ta, tb, tc):
    return d * ta + pltpu.roll(d * tb, LANES - 16, 1) + pltpu.roll(d * tc, 16, 1)


def _ssd_common(dtv, a_row):
    L = SSD_CHUNK
    a = dtv * a_row
    r = lax.broadcasted_iota(jnp.int32, (L, L), 0)
    c = lax.broadcasted_iota(jnp.int32, (L, L), 1)
    tril = r >= c
    cs = _dot_exact(tril.astype(F32), a)
    cs_t = cs.T
    cs_last = cs[L - 1:L, :]
    return dict(a=a, tril=tril, cs=cs, cs_t=cs_t, ecs=jnp.exp(cs), dte=jnp.exp(cs_last - cs),
                elast=jnp.exp(cs_last))


def _pair_sel(v, h0, lo):
    return jnp.where(lo, v[:, h0:h0 + 1], v[:, h0 + 1:h0 + 2])


def _ssd_pair(cm, h0, cb, xp, dtv, bmat, cmat, hp, lo):
    L = SSD_CHUNK
    x = xp * _pair_sel(dtv, h0, lo)
    lam0 = jnp.exp(jnp.where(cm["tril"], cm["cs"][:, h0:h0 + 1] - cm["cs_t"][h0:h0 + 1, :], NEG))
    lam1 = jnp.exp(jnp.where(cm["tril"], cm["cs"][:, h0 + 1:h0 + 2] - cm["cs_t"][h0 + 1:h0 + 2, :], NEG))
    m0, m1 = cb * lam0, cb * lam1
    ydiag = jnp.where(lo, _dot(m0, x), _dot(m1, x))
    ecs_p = _pair_sel(cm["ecs"], h0, lo)
    dte_p = _pair_sel(cm["dte"], h0, lo)
    yoff = _dot(cmat, hp, NT) * ecs_p
    xd = x * dte_p
    st = _dot(xd, bmat, TN)
    rlo = lax.broadcasted_iota(jnp.int32, (LANES, SSD_STATE), 0) < 64
    decay = jnp.where(rlo, cm["elast"][:, h0:h0 + 1], cm["elast"][:, h0 + 1:h0 + 2])
    h_next = hp * decay + st
    return dict(x=x, lam0=lam0, lam1=lam1, m0=m0, m1=m1, y=ydiag + yoff, yoff=yoff, ecs_p=ecs_p, dte_p=dte_p,
                xd=xd, decay=decay, h_next=h_next)


def _ssd_fwd(xbc, dt, a_row, *, name):
    S = xbc.shape[0]
    L = SSD_CHUNK
    nc = S // L

    def body(xs_ref, bm_ref, cm_ref, dt_ref, a_ref, y_ref, hs_ref, h_scr):
        @pl.when(pl.program_id(0) == 0)
        def _():
            h_scr[...] = jnp.zeros_like(h_scr)

        dtv = dt_ref[...]
        cm = _ssd_common(dtv, a_ref[...])
        lo = _lane((L, LANES)) < 64
        ys = []
        for g in range(2):
            bmat = bm_ref[:, g * 128:(g + 1) * 128]
            cmat = cm_ref[:, g * 128:(g + 1) * 128]
            cb = _dot(cmat, bmat, NT)
            for pr in range(2):
                p4 = 2 * g + pr
                hp = h_scr[p4]
                hs_ref[0, p4 * 128:(p4 + 1) * 128, :] = hp
                t = _ssd_pair(cm, 2 * p4, cb, xs_ref[:, p4 * 128:(p4 + 1) * 128], dtv, bmat, cmat, hp, lo)
                ys.append(t["y"])
                h_scr[p4] = t["h_next"]
        y_ref[...] = jnp.concatenate(ys, axis=1)

    return pl.pallas_call(
        body, name=name, grid=(nc,),
        in_specs=[pl.BlockSpec((L, 512), lambda c: (c, 0)), pl.BlockSpec((L, 256), lambda c: (c, 2)),
                  pl.BlockSpec((L, 256), lambda c: (c, 3)), pl.BlockSpec((L, 128), lambda c: (c, 0)),
                  pl.BlockSpec((1, 128), lambda c: (0, 0))],
        out_specs=[pl.BlockSpec((L, 512), lambda c: (c, 0)), pl.BlockSpec((1, 512, 128), lambda c: (c, 0, 0))],
        out_shape=[jax.ShapeDtypeStruct((S, 512), F32), jax.ShapeDtypeStruct((nc, 512, 128), F32)],
        scratch_shapes=[pltpu.VMEM((4, 128, 128), F32)],
        compiler_params=_params(("arbitrary",)),
    )(xbc, xbc, xbc, dt, a_row)


def _ssd_bwd(xbc, dt, a_row, hs, dy, *, name):
    S = xbc.shape[0]
    L = SSD_CHUNK
    nc = S // L

    def body(xs_ref, bm_ref, cm_ref, dt_ref, a_ref, hs_ref, dy_ref, dxs_ref, dbc_ref, ddt_ref, da_ref, g_scr):
        @pl.when(pl.program_id(0) == 0)
        def _():
            g_scr[...] = jnp.zeros_like(g_scr)
            da_ref[...] = jnp.zeros_like(da_ref)

        dtv = dt_ref[...]
        a_row_v = a_ref[...]
        cm = _ssd_common(dtv, a_row_v)
        lo = _lane((L, LANES)) < 64
        lane_row = _lane((1, LANES))
        ri = lax.broadcasted_iota(jnp.int32, (L, L), 0)
        ci = lax.broadcasted_iota(jnp.int32, (L, L), 1)
        triu = (ri <= ci).astype(F32)
        stril = ri > ci

        def halves(v, mask):
            return (jnp.sum(jnp.where(mask, v, 0.0), axis=1, keepdims=True),
                    jnp.sum(jnp.where(mask, 0.0, v), axis=1, keepdims=True))

        i_all = jnp.zeros((L, LANES), F32)
        yo_all = jnp.zeros((L, LANES), F32)
        w_all = jnp.zeros((L, LANES), F32)
        ddt_x = jnp.zeros((L, LANES), F32)
        e_row = jnp.zeros((1, LANES), F32)
        rlo = lax.broadcasted_iota(jnp.int32, (LANES, SSD_STATE), 0) < 64
        dxs, dbs, dcs = [], [], []
        for g in range(2):
            bmat = bm_ref[:, g * 128:(g + 1) * 128]
            cmat = cm_ref[:, g * 128:(g + 1) * 128]
            cb = _dot(cmat, bmat, NT)
            dcb = jnp.zeros((L, L), F32)
            db = jnp.zeros((L, SSD_STATE), F32)
            dc = jnp.zeros((L, SSD_STATE), F32)
            for pr in range(2):
                p4 = 2 * g + pr
                h0 = 2 * p4
                hp = hs_ref[0, p4 * 128:(p4 + 1) * 128, :]
                xp = xs_ref[:, p4 * 128:(p4 + 1) * 128]
                t = _ssd_pair(cm, h0, cb, xp, dtv, bmat, cmat, hp, lo)
                gst = g_scr[p4]
                dyp = dy_ref[:, p4 * 128:(p4 + 1) * 128]
                dy0 = jnp.where(lo, dyp, 0.0)
                dy1 = dyp - dy0
                bg = _dot(bmat, gst, NT)
                dx = _dot(t["m0"], dy0, TN) + _dot(t["m1"], dy1, TN) + bg * t["dte_p"]
                dm0, dm1 = _dot(dy0, t["x"], NT), _dot(dy1, t["x"], NT)
                dcb = dcb + dm0 * t["lam0"] + dm1 * t["lam1"]
                dye = dyp * t["ecs_p"]
                dc = dc + _dot(dye, hp)
                db = db + _dot(t["xd"], gst)
                i0 = jnp.sum(jnp.where(stril, _dot(triu, dm0 * t["m0"]), 0.0), axis=1, keepdims=True)
                i1 = jnp.sum(jnp.where(stril, _dot(triu, dm1 * t["m1"]), 0.0), axis=1, keepdims=True)
                yo0, yo1 = halves(dyp * t["yoff"], lo)
                w0, w1 = halves(t["xd"] * bg, lo)
                gh = gst * (hp * t["decay"])
                e0 = _sum0(jnp.sum(jnp.where(rlo, gh, 0.0), axis=1, keepdims=True))
                e1 = _sum0(jnp.sum(jnp.where(rlo, 0.0, gh), axis=1, keepdims=True))
                x0, x1 = halves(dx * xp, lo)
                oh0 = (lane_row == h0).astype(F32)
                oh1 = (lane_row == h0 + 1).astype(F32)
                i_all = i_all + i0 * oh0 + i1 * oh1
                yo_all = yo_all + yo0 * oh0 + yo1 * oh1
                w_all = w_all + w0 * oh0 + w1 * oh1
                e_row = e_row + e0 * oh0 + e1 * oh1
                ddt_x = ddt_x + x0 * oh0 + x1 * oh1
                dxs.append(dx * _pair_sel(dtv, h0, lo))
                g_scr[p4] = gst * t["decay"] + _dot(dye, cmat, TN)
            dbs.append(db + _dot(dcb, cmat, TN))
            dcs.append(dc + _dot(dcb, bmat))
        da = i_all + _dot_exact(triu, yo_all) + _dot_exact(stril.astype(F32), w_all) + e_row
        ddt_ref[...] = da * a_row_v + ddt_x
        da_ref[...] += _sum0(da * dtv)
        dxs_ref[...] = jnp.concatenate(dxs, axis=1)
        dbc_ref[...] = jnp.concatenate(dbs + dcs, axis=1)

    rev = lambda c: nc - 1 - c
    return pl.pallas_call(
        body, name=name, grid=(nc,),
        in_specs=[pl.BlockSpec((L, 512), lambda c: (rev(c), 0)), pl.BlockSpec((L, 256), lambda c: (rev(c), 2)),
                  pl.BlockSpec((L, 256), lambda c: (rev(c), 3)), pl.BlockSpec((L, 128), lambda c: (rev(c), 0)),
                  pl.BlockSpec((1, 128), lambda c: (0, 0)), pl.BlockSpec((1, 512, 128), lambda c: (rev(c), 0, 0)),
                  pl.BlockSpec((L, 512), lambda c: (rev(c), 0))],
        out_specs=[pl.BlockSpec((L, 512), lambda c: (rev(c), 0)), pl.BlockSpec((L, 512), lambda c: (rev(c), 0)),
                   pl.BlockSpec((L, 128), lambda c: (rev(c), 0)), pl.BlockSpec((1, 128), lambda c: (0, 0))],
        out_shape=[jax.ShapeDtypeStruct((S, 512), F32), jax.ShapeDtypeStruct((S, 512), F32),
                   jax.ShapeDtypeStruct((S, 128), F32), jax.ShapeDtypeStruct((1, 128), F32)],
        scratch_shapes=[pltpu.VMEM((4, 128, 128), F32)],
        compiler_params=_params(("arbitrary",)),
    )(xbc, xbc, xbc, dt, a_row, hs, dy)


MLA_SCALE = MLA_QK ** -0.5


def _causal_scores(q, k, qi, ki, t):
    s = _dot(q, k, NT) * MLA_SCALE
    row = qi * t + lax.broadcasted_iota(jnp.int32, (t, t), 0)
    col = ki * t + lax.broadcasted_iota(jnp.int32, (t, t), 1)
    return jnp.where(col <= row, s, NEG)


def _mla_fwd(q, k, kv, *, name):
    S = q.shape[0]
    t = min(ATTN_TILE, S)
    nq = S // t

    def body(q_ref, k_ref, v_ref, o_ref, lse_ref, m_scr, l_scr, acc_scr):
        qi, ki = pl.program_id(1), pl.program_id(2)

        @pl.when(ki == 0)
        def _():
            m_scr[...] = jnp.full_like(m_scr, NEG)
            l_scr[...] = jnp.zeros_like(l_scr)
            acc_scr[...] = jnp.zeros_like(acc_scr)

        @pl.when(ki <= qi)
        def _():
            s = _causal_scores(q_ref[...], k_ref[...], qi, ki, t)
            m_old = m_scr[:, 0:1]
            m_new = jnp.maximum(m_old, jnp.max(s, axis=1, keepdims=True))
            p = jnp.exp(s - m_new)
            corr = jnp.exp(m_old - m_new)
            l_scr[...] = jnp.broadcast_to(corr * l_scr[:, 0:1] + jnp.sum(p, axis=1, keepdims=True), l_scr.shape)
            acc_scr[...] = corr * acc_scr[...] + _dot(p, v_ref[...])
            m_scr[...] = jnp.broadcast_to(m_new, m_scr.shape)

        @pl.when(ki == nq - 1)
        def _():
            l = l_scr[:, 0:1]
            o_ref[...] = acc_scr[...] / l
            lse_ref[0] = jnp.broadcast_to(m_scr[:, 0:1] + jnp.log(l), (t, LANES))

    return pl.pallas_call(
        body, name=name, grid=(MLA_HEADS, nq, nq),
        in_specs=[pl.BlockSpec((t, 128), lambda h, qi, ki: (qi, h)),
                  pl.BlockSpec((t, 128), lambda h, qi, ki: (jnp.minimum(ki, qi), h)),
                  pl.BlockSpec((t, 128), lambda h, qi, ki: (jnp.minimum(ki, qi), 2 * h + 1))],
        out_specs=[pl.BlockSpec((t, 128), lambda h, qi, ki: (qi, h)),
                   pl.BlockSpec((1, t, 128), lambda h, qi, ki: (h, qi, 0))],
        out_shape=[jax.ShapeDtypeStruct((S, MLA_HEADS * 128), F32), jax.ShapeDtypeStruct((MLA_HEADS, S, 128), F32)],
        scratch_shapes=[pltpu.VMEM((t, 128), F32), pltpu.VMEM((t, 128), F32), pltpu.VMEM((t, 128), F32)],
        compiler_params=_params(("parallel", "parallel", "arbitrary")),
    )(q, k, kv)


def _mla_bwd_dkv(q, k, kv, o, do, lse, *, name):
    S = q.shape[0]
    t = min(ATTN_TILE, S)
    nq = S // t

    def body(q_ref, k_ref, v_ref, o_ref, do_ref, lse_ref, dkv_ref):
        ki, qi = pl.program_id(1), pl.program_id(2)

        @pl.when(qi == 0)
        def _():
            dkv_ref[...] = jnp.zeros_like(dkv_ref)

        @pl.when(qi >= ki)
        def _():
            qv, dov = q_ref[...], do_ref[...]
            s = _causal_scores(qv, k_ref[...], qi, ki, t)
            p = jnp.exp(s - lse_ref[0][:, 0:1])
            dv = _dot(p, dov, TN)
            dp = _dot(dov, v_ref[...], NT)
            delta = jnp.sum(dov * o_ref[...], axis=1, keepdims=True)
            ds = p * (dp - delta) * MLA_SCALE
            dkv_ref[...] += jnp.concatenate([_dot(ds, qv, TN), dv], axis=1)

    qmap = lambda h, ki, qi: (jnp.maximum(qi, ki), h)
    return pl.pallas_call(
        body, name=name, grid=(MLA_HEADS, nq, nq),
        in_specs=[pl.BlockSpec((t, 128), qmap),
                  pl.BlockSpec((t, 128), lambda h, ki, qi: (ki, h)),
                  pl.BlockSpec((t, 128), lambda h, ki, qi: (ki, 2 * h + 1)),
                  pl.BlockSpec((t, 128), qmap), pl.BlockSpec((t, 128), qmap),
                  pl.BlockSpec((1, t, 128), lambda h, ki, qi: (h, jnp.maximum(qi, ki), 0))],
        out_specs=pl.BlockSpec((t, 256), lambda h, ki, qi: (ki, h)),
        out_shape=jax.ShapeDtypeStruct((S, MLA_HEADS * 256), F32),
        compiler_params=_params(("parallel", "parallel", "arbitrary")),
    )(q, k, kv, o, do, lse)


def _mla_bwd_dq(q, k, kv, o, do, lse, *, name):
    S = q.shape[0]
    t = min(ATTN_TILE, S)
    nq = S // t

    def body(q_ref, k_ref, v_ref, o_ref, do_ref, lse_ref, dq_ref):
        qi, ki = pl.program_id(1), pl.program_id(2)

        @pl.when(ki == 0)
        def _():
            dq_ref[...] = jnp.zeros_like(dq_ref)

        @pl.when(ki <= qi)
        def _():
            dov, kv = do_ref[...], k_ref[...]
            s = _causal_scores(q_ref[...], kv, qi, ki, t)
            p = jnp.exp(s - lse_ref[0][:, 0:1])
            dp = _dot(dov, v_ref[...], NT)
            delta = jnp.sum(dov * o_ref[...], axis=1, keepdims=True)
            ds = p * (dp - delta) * MLA_SCALE
            dq_ref[...] += _dot(ds, kv)

    qmap = lambda h, qi, ki: (qi, h)
    return pl.pallas_call(
        body, name=name, grid=(MLA_HEADS, nq, nq),
        in_specs=[pl.BlockSpec((t, 128), qmap),
                  pl.BlockSpec((t, 128), lambda h, qi, ki: (jnp.minimum(ki, qi), h)),
                  pl.BlockSpec((t, 128), lambda h, qi, ki: (jnp.minimum(ki, qi), 2 * h + 1)),
                  pl.BlockSpec((t, 128), qmap), pl.BlockSpec((t, 128), qmap),
                  pl.BlockSpec((1, t, 128), lambda h, qi, ki: (h, qi, 0))],
        out_specs=pl.BlockSpec((t, 128), qmap),
        out_shape=jax.ShapeDtypeStruct((S, MLA_HEADS * 128), F32),
        compiler_params=_params(("parallel", "parallel", "arbitrary")),
    )(q, k, kv, o, do, lse)


HBM = pl.BlockSpec(memory_space=pl.ANY)


class _Step:
    def __init__(self, inputs, out_shapes, n_sems, start, finish, mid=None):
        self.inputs, self.out_shapes, self.n_sems = inputs, out_shapes, n_sems
        self.start, self.finish, self.mid = start, finish, mid
        self.alias = []


class _Shifted:
    def __init__(self, ref, off):
        self.ref, self.off = ref, off

    @property
    def at(self):
        return self

    def __getitem__(self, j):
        return self.ref.at[self.off + j]


def _merge_steps(steps):
    offs = [sum(s.n_sems for s in steps[:i]) for i in range(len(steps) + 1)]
    i_offs = [sum(len(s.inputs) for s in steps[:i]) for i in range(len(steps))]
    o_offs = [sum(len(s.out_shapes) for s in steps[:i]) for i in range(len(steps))]

    def phase(which):
        def run(ins, outs, sems):
            for s, off, i0, o0 in zip(steps, offs, i_offs, o_offs):
                fn = getattr(s, which)
                if fn is not None:
                    fn(ins[i0:i0 + len(s.inputs)], outs[o0:o0 + len(s.out_shapes)],
                       [_Shifted(sems[0], off), _Shifted(sems[1], off)])
        return run

    merged = _Step([a for s in steps for a in s.inputs], [o for s in steps for o in s.out_shapes], offs[-1],
                   phase("start"), phase("finish"), phase("mid") if any(s.mid for s in steps) else None)
    merged.alias = [(i0 + a, o0 + b) for s, i0, o0 in zip(steps, i_offs, o_offs) for a, b in s.alias]
    return merged


def _place():
    x, y, c = lax.axis_index("x"), lax.axis_index("y"), lax.axis_index("c")
    chips = [(1 - x, y), (x, 1 - y), (1 - x, 1 - y)]
    return x, y, c, chips


def _chunks(rows, tile):
    return next(n for n in (4, 3, 2, 1) if rows % (n * tile) == 0)


def _remote(src, dst, sems, j, to):
    return pltpu.make_async_remote_copy(src_ref=src, dst_ref=dst, send_sem=sems[0].at[j], recv_sem=sems[1].at[j],
                                        device_id=to, device_id_type=MESH)


def _gather_step(wp):
    R, C = wp.shape
    H = R // 2
    nq = _chunks(H, 16)
    CH = H // nq

    def copies(ins, outs, sems):
        x, y, c, chips = _place()
        sib, me = (x, y, 1 - c), 2 * x + y
        w_ref, out_ref = ins[0], outs[0]

        def piece(k, hc, q):
            return out_ref.at[k, pl.ds(hc * H + q * CH, CH), :]

        sends, landed, fwds, fwd_landed = [], [], [], []
        for q in range(nq):
            for j, (px, py) in enumerate(chips):
                k = 2 * px + py
                sends.append(_remote(w_ref.at[pl.ds(c * H + q * CH, CH), :], piece(me, c, q), sems, j * nq + q,
                                     (px, py, c)))
                landed.append(_remote(piece(k, c, q), piece(k, c, q), sems, j * nq + q, (px, py, c)))
                fwds.append(_remote(piece(k, c, q), piece(k, c, q), sems, (3 + j) * nq + q, sib))
                fwd_landed.append(_remote(piece(k, 1 - c, q), piece(k, 1 - c, q), sems, (3 + j) * nq + q, sib))
        return sends, landed, fwds, fwd_landed

    def start(ins, outs, sems):
        for cp in copies(ins, outs, sems)[0]:
            cp.start()

    def mid(ins, outs, sems):
        _, landed, fwds, _ = copies(ins, outs, sems)
        for arrived, onward in zip(landed, fwds):
            arrived.wait_recv()
            onward.start()

    def finish(ins, outs, sems):
        sends, _, fwds, fwd_landed = copies(ins, outs, sems)
        for cp in fwd_landed:
            cp.wait_recv()
        for cp in sends + fwds:
            cp.wait_send()

    return _Step([wp], [jax.ShapeDtypeStruct((N_SHARD, R, C), wp.dtype)], 6 * nq, start, finish, mid)


def _pair_exchange_step(gp):
    n, R, C = gp.shape
    H = R // 2
    nq = _chunks(H, 8)
    CH = H // nq

    def copies(ins, outs, sems):
        x, y, c, _ = _place()
        return [_remote(ins[0].at[k, pl.ds((1 - c) * H + q * CH, CH), :], outs[0].at[k, pl.ds(q * CH, CH), :], sems,
                        k * nq + q, (x, y, 1 - c)) for k in range(n) for q in range(nq)]

    def start(ins, outs, sems):
        for cp in copies(ins, outs, sems):
            cp.start()

    def finish(ins, outs, sems):
        for cp in copies(ins, outs, sems):
            cp.wait()

    return _Step([gp], [jax.ShapeDtypeStruct((n, H, C), gp.dtype)], n * nq, start, finish)


def _chip_exchange_step(pb):
    n, H, C = pb.shape
    nq = _chunks(H, 16)
    CH = H // nq

    def copies(ins, outs, sems):
        x, y, c, chips = _place()
        return [_remote(ins[0].at[2 * px + py, pl.ds(q * CH, CH), :], outs[0].at[j, pl.ds(q * CH, CH), :], sems,
                        j * nq + q, (px, py, c)) for q in range(nq) for j, (px, py) in enumerate(chips)]

    def start(ins, outs, sems):
        for cp in copies(ins, outs, sems):
            cp.start()

    def finish(ins, outs, sems):
        for cp in copies(ins, outs, sems):
            cp.wait()

    return _Step([pb], [jax.ShapeDtypeStruct((3, H, C), pb.dtype)], 3 * nq, start, finish)


def _pair_join_step(q):
    H, C = q.shape
    nq = _chunks(H, 8)
    CH = H // nq

    def copies(ins, outs, sems):
        x, y, c, _ = _place()
        return [_remote(ins[0].at[pl.ds(j * CH, CH), :], outs[0].at[pl.ds(j * CH, CH), :], sems, j, (x, y, 1 - c))
                for j in range(nq)]

    def start(ins, outs, sems):
        for cp in copies(ins, outs, sems):
            cp.start()

    def finish(ins, outs, sems):
        for cp in copies(ins, outs, sems):
            cp.wait()

    return _Step([q], [jax.ShapeDtypeStruct((H, C), q.dtype)], nq, start, finish)


def _sem_scratch(step):
    return [pltpu.SemaphoreType.DMA((step.n_sems,)), pltpu.SemaphoreType.DMA((step.n_sems,))]


def _run_step(step, name):
    ni, no = len(step.inputs), len(step.out_shapes)

    def body(*refs):
        ins, outs, sems = refs[:ni], refs[ni:ni + no], refs[ni + no:]
        step.start(ins, outs, sems)
        if step.mid is not None:
            step.mid(ins, outs, sems)
        step.finish(ins, outs, sems)

    return pl.pallas_call(body, name=name, in_specs=[HBM] * ni, out_specs=[HBM] * no, out_shape=step.out_shapes,
                          input_output_aliases=dict(step.alias),
                          scratch_shapes=_sem_scratch(step))(*step.inputs)


def _grid_flags(grid):
    ids = [pl.program_id(d) for d in range(len(grid))]
    first = functools.reduce(lambda a, b: a & b, [i == 0 for i in ids])
    last = functools.reduce(lambda a, b: a & b, [i == n - 1 for i, n in zip(ids, grid)])
    return first, last, last


def _call_with_step(core, step, flags, args, *, name, grid, in_specs, out_specs, out_shape, sem, scratch_shapes=(),
                    aliases=None):
    aliases = aliases or {}
    if step is None:
        return pl.pallas_call(core, name=name, grid=grid, in_specs=in_specs, out_specs=out_specs,
                              out_shape=out_shape, scratch_shapes=list(scratch_shapes),
                              input_output_aliases=aliases, compiler_params=_params(sem))(*args)
    n_in, n_out, n_scr = len(in_specs), len(out_specs), len(scratch_shapes)
    si, so = len(step.inputs), len(step.out_shapes)
    flags = flags or (lambda: _grid_flags(grid))
    aliases = {**aliases, **{n_in + a: n_out + b for a, b in step.alias}}

    def body(*refs):
        ins, s_ins = refs[:n_in], refs[n_in:n_in + si]
        outs = refs[n_in + si:n_in + si + n_out]
        s_outs = refs[n_in + si + n_out:n_in + si + n_out + so]
        scr = refs[n_in + si + n_out + so:n_in + si + n_out + so + n_scr]
        sems = refs[n_in + si + n_out + so + n_scr:]
        first, middle, last = flags()

        @pl.when(first)
        def _():
            step.start(s_ins, s_outs, sems)

        if step.mid is not None:
            @pl.when(middle)
            def _():
                step.mid(s_ins, s_outs, sems)

        core(*ins, *outs, *scr)

        @pl.when(last)
        def _():
            step.finish(s_ins, s_outs, sems)

    return pl.pallas_call(
        body, name=name, grid=grid, in_specs=list(in_specs) + [HBM] * si, out_specs=list(out_specs) + [HBM] * so,
        out_shape=list(out_shape) + list(step.out_shapes), scratch_shapes=list(scratch_shapes) + _sem_scratch(step),
        input_output_aliases=aliases, compiler_params=_params(("arbitrary",) * len(grid)))(*args, *step.inputs)


def _attn_flags(nq):
    h, qi = pl.program_id(0), pl.program_id(1)
    return ((h == 0) & (qi == 0), (h == MLA_HEADS - 1) & (qi == 0), (h == MLA_HEADS - 1) & (qi == nq - 1))


ATTN_SPLIT = 1
ATTN_KEY_SPLIT = 1


def _att_mask(s_t, q0, k0):
    krow = k0 + lax.broadcasted_iota(jnp.int32, s_t.shape, 0)
    qcol = q0 + lax.broadcasted_iota(jnp.int32, s_t.shape, 1)
    return jnp.where(krow <= qcol, s_t, NEG)


def _loop2(lo, hi, step, carry):
    def two(i, c):
        kb = lo + 2 * i
        return step(kb + 1, step(kb, c))

    carry = lax.fori_loop(0, (hi - lo) // 2, two, carry)
    return lax.cond((hi - lo) % 2 == 1, lambda c: step(hi - 1, c), lambda c: c, carry)


def _rows(ref, blk, t):
    return ref[pl.ds(pl.multiple_of(blk * t, t), t), :]


def _cols(ref, blk, t):
    return ref[:, pl.ds(pl.multiple_of(blk * t, t), t)]


def _attn_fwd(q, k, v_t, *, name, hosted=None):
    S = q.shape[0]
    t = min(ATTN_TILE, S)
    nq = S // t

    def body(q_ref, k_ref, vt_ref, o_ref, lse_ref):
        qi = pl.program_id(1)
        w = t // ATTN_SPLIT
        qs = [q_ref[s * w:(s + 1) * w, :] for s in range(ATTN_SPLIT)]

        tk = t // ATTN_KEY_SPLIT

        def step(kb, carry, masked):
            kt, vt = _rows(k_ref, kb, tk), _cols(vt_ref, kb, tk)
            out = []
            for s, (m, l, acc) in enumerate(carry):
                s_t = lax.dot_general(kt, qs[s], NT, preferred_element_type=F32)
                if masked:
                    s_t = _att_mask(s_t, qi * t + s * w, kb * tk)
                m_new = jnp.maximum(m, jnp.max(s_t, axis=0, keepdims=True))
                p_t = jnp.exp(s_t - m_new)
                corr = jnp.exp(m - m_new)
                l = corr * l + jnp.sum(p_t, axis=0, keepdims=True)
                acc = corr * acc + lax.dot_general(vt, p_t.astype(BF16), NN, preferred_element_type=F32)
                out.append((m_new, l, acc))
            return tuple(out)

        init = tuple((jnp.full((1, w), NEG, F32), jnp.zeros((1, w), F32), jnp.zeros((LANES, w), F32))
                     for _ in range(ATTN_SPLIT))
        carry = _loop2(0, qi * ATTN_KEY_SPLIT, lambda kb, c: step(kb, c, False), init)
        for j in range(ATTN_KEY_SPLIT):
            carry = step(qi * ATTN_KEY_SPLIT + j, carry, True)
        for s, (m, l, acc) in enumerate(carry):
            o_ref[:, s * w:(s + 1) * w] = acc / l
            lse_ref[0, :, s * w:(s + 1) * w] = m + jnp.log(l)

    return _call_with_step(
        body, hosted, lambda: _attn_flags(nq), (q, k, v_t), name=name, grid=(MLA_HEADS, nq),
        in_specs=[pl.BlockSpec((t, LANES), lambda h, qi: (qi, h)),
                  pl.BlockSpec((S, LANES), lambda h, qi: (0, h)),
                  pl.BlockSpec((LANES, S), lambda h, qi: (h, 0))],
        out_specs=[pl.BlockSpec((LANES, t), lambda h, qi: (h, qi)),
                   pl.BlockSpec((1, 1, t), lambda h, qi: (h, 0, qi))],
        out_shape=[jax.ShapeDtypeStruct((MLA_HEADS * LANES, S), F32), jax.ShapeDtypeStruct((MLA_HEADS, 1, S), F32)],
        sem=("parallel", "arbitrary"))


def _attn_bwd_dq(q, k, v, o_t, do_t, lse, *, name, hosted=None):
    S = q.shape[0]
    t = min(ATTN_TILE, S)
    nq = S // t

    def body(q_ref, k_ref, v_ref, o_ref, do_ref, lse_ref, dq_ref, delta_ref):
        qi = pl.program_id(1)
        qv = q_ref[...]
        dov = do_ref[...]
        delta = jnp.sum(dov * o_ref[...], axis=0, keepdims=True)
        delta_ref[0] = delta
        dob = dov.astype(BF16)
        lse_v = lse_ref[0]

        def step(kb, acc, masked):
            s_t = lax.dot_general(_rows(k_ref, kb, t), qv, NT, preferred_element_type=F32)
            if masked:
                s_t = _att_mask(s_t, qi * t, kb * t)
            p_t = jnp.exp(s_t - lse_v)
            dp_t = lax.dot_general(_rows(v_ref, kb, t), dob, NN, preferred_element_type=F32)
            ds_t = (p_t * (dp_t - delta)).astype(BF16)
            return acc + lax.dot_general(_rows(k_ref, kb, t), ds_t, TN, preferred_element_type=F32)

        acc = _loop2(0, qi, lambda kb, c: step(kb, c, False), jnp.zeros((LANES, t), F32))
        dq_ref[...] = step(qi, acc, True).T

    tile = pl.BlockSpec((t, LANES), lambda h, qi: (qi, h))
    tile_t = pl.BlockSpec((LANES, t), lambda h, qi: (h, qi))
    stat = pl.BlockSpec((1, 1, t), lambda h, qi: (h, 0, qi))
    seq = pl.BlockSpec((S, LANES), lambda h, qi: (0, h))
    return _call_with_step(
        body, hosted, lambda: _attn_flags(nq), (q, k, v, o_t, do_t, lse), name=name, grid=(MLA_HEADS, nq),
        in_specs=[tile, seq, seq, tile_t, tile_t, stat],
        out_specs=[tile, stat],
        out_shape=[jax.ShapeDtypeStruct((S, MLA_HEADS * LANES), F32), jax.ShapeDtypeStruct((MLA_HEADS, 1, S), F32)],
        sem=("parallel", "arbitrary"))


def _attn_bwd_dkv(q, k, v, do_t, lse, delta, *, name, hosted=None):
    S = q.shape[0]
    t = min(ATTN_TILE, S)
    nq = S // t

    def body(q_ref, k_ref, v_ref, do_ref, lse_ref, delta_ref, dk_ref, dv_ref):
        ki = pl.program_id(1)
        kv, vv = k_ref[...], v_ref[...]

        def step(qb, carry, masked):
            dk, dv = carry
            qt = _rows(q_ref, qb, t)
            s_t = lax.dot_general(kv, qt, NT, preferred_element_type=F32)
            if masked:
                s_t = _att_mask(s_t, qb * t, ki * t)
            p_t = jnp.exp(s_t - _cols(lse_ref.at[0], qb, t))
            dob = _cols(do_ref, qb, t).astype(BF16)
            dv = dv + lax.dot_general(dob, p_t.astype(BF16), NT, preferred_element_type=F32)
            dp_t = lax.dot_general(vv, dob, NN, preferred_element_type=F32)
            ds_t = (p_t * (dp_t - _cols(delta_ref.at[0], qb, t))).astype(BF16)
            dk = dk + lax.dot_general(qt.T, ds_t, NT, preferred_element_type=F32)
            return dk, dv

        zero = jnp.zeros((LANES, t), F32)
        carry = step(ki, (zero, zero), True)
        dk, dv = _loop2(ki + 1, nq, lambda qb, c: step(qb, c, False), carry)
        dk_ref[...] = dk.T
        dv_ref[...] = dv

    tile = pl.BlockSpec((t, LANES), lambda h, ki: (ki, h))
    tile_t = pl.BlockSpec((LANES, t), lambda h, ki: (h, ki))
    seq = pl.BlockSpec((S, LANES), lambda h, ki: (0, h))
    seq_t = pl.BlockSpec((LANES, S), lambda h, ki: (h, 0))
    stat = pl.BlockSpec((1, 1, S), lambda h, ki: (h, 0, 0))
    return _call_with_step(
        body, hosted, lambda: _attn_flags(nq), (q, k, v, do_t, lse, delta), name=name, grid=(MLA_HEADS, nq),
        in_specs=[seq, tile, tile, seq_t, stat, stat],
        out_specs=[tile, tile_t],
        out_shape=[jax.ShapeDtypeStruct((S, MLA_HEADS * LANES), F32), jax.ShapeDtypeStruct((MLA_HEADS * LANES, S), F32)],
        sem=("parallel", "arbitrary"))


def _attn_fwd_p(q, k, v_t, *, name, hosted=None):
    S = q.shape[0]
    t = min(ATTN_TILE, S)
    nq = S // t

    def body(q_ref, k_ref, vt_ref, o_ref, lse_ref):
        qi = pl.program_id(1)
        qv = q_ref[...]

        def scores(kb):
            return lax.dot_general(_rows(k_ref, kb, t), qv, NT, preferred_element_type=F32)

        def weigh(kb, p_t):
            return lax.dot_general(_cols(vt_ref, kb, t), p_t, NN, preferred_element_type=F32)

        def soft(s_t, m, l):
            m_new = jnp.maximum(m, jnp.max(s_t, axis=0, keepdims=True))
            p_t = jnp.exp(s_t - m_new)
            corr = jnp.exp(m - m_new)
            return m_new, corr * l + jnp.sum(p_t, axis=0, keepdims=True), corr, p_t.astype(BF16)

        def step(kb, carry):
            s_cur, m, l, acc, p_prev, corr_prev = carry
            s_next = scores(kb + 1)
            acc = corr_prev * acc + weigh(jnp.maximum(kb - 1, 0), p_prev)
            m, l, corr, p_t = soft(s_cur, m, l)
            return s_next, m, l, acc, p_t, corr

        init = (scores(0), jnp.full((1, t), NEG, F32), jnp.zeros((1, t), F32), jnp.zeros((LANES, t), F32),
                jnp.zeros((t, t), BF16), jnp.ones((1, t), F32))
        s_cur, m, l, acc, p_prev, corr_prev = lax.fori_loop(0, qi, step, init)
        acc = corr_prev * acc + weigh(jnp.maximum(qi - 1, 0), p_prev)
        m, l, corr, p_t = soft(_att_mask(s_cur, qi * t, qi * t), m, l)
        acc = corr * acc + weigh(qi, p_t)
        o_ref[...] = acc / l
        lse_ref[0] = m + jnp.log(l)

    return _call_with_step(
        body, hosted, lambda: _attn_flags(nq), (q, k, v_t), name=name, grid=(MLA_HEADS, nq),
        in_specs=[pl.BlockSpec((t, LANES), lambda h, qi: (qi, h)),
                  pl.BlockSpec((S, LANES), lambda h, qi: (0, h)),
                  pl.BlockSpec((LANES, S), lambda h, qi: (h, 0))],
        out_specs=[pl.BlockSpec((LANES, t), lambda h, qi: (h, qi)),
                   pl.BlockSpec((1, 1, t), lambda h, qi: (h, 0, qi))],
        out_shape=[jax.ShapeDtypeStruct((MLA_HEADS * LANES, S), F32), jax.ShapeDtypeStruct((MLA_HEADS, 1, S), F32)],
        sem=("parallel", "arbitrary"))


def _attn_bwd_dq_p(q, k, k_t, v, o_t, do_t, lse, *, name, hosted=None):
    S = q.shape[0]
    t = min(ATTN_TILE, S)
    nq = S // t

    def body(q_ref, k_ref, kt_ref, v_ref, o_ref, do_ref, lse_ref, dq_ref, delta_ref):
        qi = pl.program_id(1)
        qv = q_ref[...]
        dov = do_ref[...]
        delta = jnp.sum(dov * o_ref[...], axis=0, keepdims=True)
        delta_ref[0] = delta
        dob = dov.astype(BF16)
        lse_v = lse_ref[0]

        def front(kb):
            return (lax.dot_general(_rows(k_ref, kb, t), qv, NT, preferred_element_type=F32),
                    lax.dot_general(_rows(v_ref, kb, t), dob, NN, preferred_element_type=F32))

        def back(kb, ds_t):
            return lax.dot_general(_cols(kt_ref, kb, t), ds_t, NN, preferred_element_type=F32)

        def mid(s_t, dp_t):
            return (jnp.exp(s_t - lse_v) * (dp_t - delta)).astype(BF16)

        def step(kb, carry):
            s_cur, dp_cur, acc, ds_prev = carry
            s_next, dp_next = front(kb + 1)
            acc = acc + back(jnp.maximum(kb - 1, 0), ds_prev)
            return s_next, dp_next, acc, mid(s_cur, dp_cur)

        init = (*front(0), jnp.zeros((LANES, t), F32), jnp.zeros((t, t), BF16))
        s_cur, dp_cur, acc, ds_prev = lax.fori_loop(0, qi, step, init)
        acc = acc + back(jnp.maximum(qi - 1, 0), ds_prev)
        dq_ref[...] = acc + back(qi, mid(_att_mask(s_cur, qi * t, qi * t), dp_cur))

    tile_t = pl.BlockSpec((LANES, t), lambda h, qi: (h, qi))
    stat = pl.BlockSpec((1, 1, t), lambda h, qi: (h, 0, qi))
    seq = pl.BlockSpec((S, LANES), lambda h, qi: (0, h))
    return _call_with_step(
        body, hosted, lambda: _attn_flags(nq), (q, k, k_t, v, o_t, do_t, lse), name=name, grid=(MLA_HEADS, nq),
        in_specs=[pl.BlockSpec((t, LANES), lambda h, qi: (qi, h)), seq,
                  pl.BlockSpec((LANES, S), lambda h, qi: (h, 0)), seq, tile_t, tile_t, stat],
        out_specs=[tile_t, stat],
        out_shape=[jax.ShapeDtypeStruct((MLA_HEADS * LANES, S), F32), jax.ShapeDtypeStruct((MLA_HEADS, 1, S), F32)],
        sem=("parallel", "arbitrary"))


def _attn_bwd_dkv_p(q, q_t, k, v, do_t, lse, delta, *, name, hosted=None):
    S = q.shape[0]
    t = min(ATTN_TILE, S)
    nq = S // t

    def body(q_ref, qt_ref, k_ref, v_ref, do_ref, lse_ref, delta_ref, dk_ref, dv_ref):
        ki = pl.program_id(1)
        kv, vv = k_ref[...], v_ref[...]

        def grad_out(qb):
            return _cols(do_ref, qb, t).astype(BF16)

        def front(qb):
            return (lax.dot_general(kv, _rows(q_ref, qb, t), NT, preferred_element_type=F32),
                    lax.dot_general(vv, grad_out(qb), NN, preferred_element_type=F32))

        def mid(s_t, dp_t, qb):
            p_t = jnp.exp(s_t - _cols(lse_ref.at[0], qb, t))
            return p_t.astype(BF16), (p_t * (dp_t - _cols(delta_ref.at[0], qb, t))).astype(BF16)

        def back(qb, dk, dv, p_t, ds_t):
            return (dk + lax.dot_general(_cols(qt_ref, qb, t), ds_t, NT, preferred_element_type=F32),
                    dv + lax.dot_general(grad_out(qb), p_t, NT, preferred_element_type=F32))

        def step(qb, carry):
            s_cur, dp_cur, dk, dv, p_prev, ds_prev = carry
            s_next, dp_next = front(jnp.minimum(qb + 1, nq - 1))
            dk, dv = back(qb - 1, dk, dv, p_prev, ds_prev)
            p_t, ds_t = mid(s_cur, dp_cur, qb)
            return s_next, dp_next, dk, dv, p_t, ds_t

        s0, dp0 = front(ki)
        p0, ds0 = mid(_att_mask(s0, ki * t, ki * t), dp0, ki)
        zero = jnp.zeros((LANES, t), F32)
        init = (*front(jnp.minimum(ki + 1, nq - 1)), zero, zero, p0, ds0)
        _, _, dk, dv, p_prev, ds_prev = lax.fori_loop(ki + 1, nq, step, init)
        dk, dv = back(nq - 1, dk, dv, p_prev, ds_prev)
        dk_ref[...] = dk
        dv_ref[...] = dv

    tile = pl.BlockSpec((t, LANES), lambda h, ki: (ki, h))
    tile_t = pl.BlockSpec((LANES, t), lambda h, ki: (h, ki))
    seq = pl.BlockSpec((S, LANES), lambda h, ki: (0, h))
    seq_t = pl.BlockSpec((LANES, S), lambda h, ki: (h, 0))
    stat = pl.BlockSpec((1, 1, S), lambda h, ki: (h, 0, 0))
    return _call_with_step(
        body, hosted, lambda: _attn_flags(nq), (q, q_t, k, v, do_t, lse, delta), name=name, grid=(MLA_HEADS, nq),
        in_specs=[seq, seq_t, tile, tile, seq_t, stat, stat],
        out_specs=[tile_t, tile_t],
        out_shape=[jax.ShapeDtypeStruct((MLA_HEADS * LANES, S), F32)] * 2,
        sem=("parallel", "arbitrary"))


def _fn_ln(ctx, x, g, b):
    xhat, _ = _ln_stats(x)
    y = xhat * g + b
    return y, y


def _fn_conv_fwd(ctx, u, up, dtr, w8, cb, dtb):
    first = ctx.i == 0
    y = u * w8[3:4] + cb
    for s in (1, 2, 3):
        y = y + _shift_down(u, up, s, first) * w8[3 - s:4 - s]
    act = y * _sigmoid(y)
    v = dtr + dtb
    e = jnp.exp(-jnp.abs(v))
    one_p = 1.0 + e
    log1p = jnp.where(one_p == 1.0, e, jnp.log(one_p) * e / (one_p - 1.0))
    return y, act, jnp.maximum(v, 0.0) + log1p


def _fn_ssd_post(ctx, y, xs, z, dexp, g):
    yg = (y + xs * dexp) * (z * _sigmoid(z))
    outs = []
    for k in range(2):
        v = yg[:, 256 * k:256 * (k + 1)]
        outs.append(v * lax.rsqrt(_mean1(v * v) + RMS_EPS))
    return (jnp.concatenate(outs, axis=1) * g,)


def _fn_ssd_post_bwd(ctx, dyn, y, xs, z, dexp, g):
    yt = y + xs * dexp
    sig = _sigmoid(z)
    sz = z * sig
    yg = yt * sz
    dyh = dyn * g
    yh, dyg = [], []
    for k in range(2):
        sl = slice(256 * k, 256 * (k + 1))
        v = yg[:, sl]
        rs = lax.rsqrt(_mean1(v * v) + RMS_EPS)
        vh = v * rs
        yh.append(vh)
        dyg.append(rs * (dyh[:, sl] - vh * _mean1(dyh[:, sl] * vh)))
    yh = jnp.concatenate(yh, axis=1)
    dyg = jnp.concatenate(dyg, axis=1)
    dyt = dyg * sz
    dz = dyg * yt * (sig * (1.0 + z * (1.0 - sig)))
    return dyt, dz, dyt * dexp, _sum0(dyt * xs), _sum0(dyn * yh)


def _fn_mla_pre(ctx, ql, kvl, gq, gkv):
    return _rms_fwd(ql, gq), _rms_fwd(kvl, gkv)


def _fn_mla_pre_bwd(ctx, ql, kvl, dqn, dkvn_k, dkvn_v, gq, gkv):
    dql, dgq = _rms_bwd(ql, dqn, gq)
    dkvl, dgkv = _rms_bwd(kvl, dkvn_k + dkvn_v, gkv)
    return dql, dkvl, dgq, dgkv


def _fn_rope(ctx, qp, kn, kr, ta, tb, tc):
    kpe = _rope(kr, ta, tb, tc)
    qs, ks = [], []
    for h in range(MLA_HEADS):
        sl = slice(128 * h, 128 * (h + 1))
        qs.append(_rope(qp[:, sl], ta, tb, tc) * MLA_SCALE)
        ks.append(kn[:, sl] + kpe)
    return jnp.concatenate(qs, axis=1), jnp.concatenate(ks, axis=1)


def _fn_rope_bwd(ctx, dq, dk, ta, tb, tc):
    qs = []
    ksum = jnp.zeros_like(ta)
    for h in range(MLA_HEADS):
        sl = slice(128 * h, 128 * (h + 1))
        qs.append(_rope_bwd(dq[:, sl] * MLA_SCALE, ta, tb, tc))
        ksum = ksum + dk[:, sl]
    lane = _lane(ksum.shape)
    dkr = jnp.where((lane >= 64) & (lane < 96), _rope_bwd(ksum, ta, tb, tc), 0.0)
    return jnp.concatenate(qs, axis=1), dkr


MEM_SCALE = MEM_HEAD_DIM ** -0.5


def _mem_probs(qh, kh):
    s = _dot(qh, kh, NT) * MEM_SCALE
    p = jnp.exp(s - jnp.max(s, axis=1, keepdims=True))
    return p / jnp.sum(p, axis=1, keepdims=True)


def _fn_mem_fwd(ctx, q, km, vm):
    outs = []
    for h in range(MEM_HEADS):
        sl = slice(256 * h, 256 * (h + 1))
        outs.append(_dot(_mem_probs(q[:, sl], km[:, sl]), vm[:, sl]))
    return (jnp.concatenate(outs, axis=1),)


def _fn_mem_bwd(ctx, q, do, km, vm):
    dqs, dks, dvs = [], [], []
    for h in range(MEM_HEADS):
        sl = slice(256 * h, 256 * (h + 1))
        p = _mem_probs(q[:, sl], km[:, sl])
        dvs.append(_dot(p, do[:, sl], TN))
        dp = _dot(do[:, sl], vm[:, sl], NT)
        ds = p * (dp - jnp.sum(dp * p, axis=1, keepdims=True)) * MEM_SCALE
        dqs.append(_dot(ds, km[:, sl]))
        dks.append(_dot(ds, q[:, sl], TN))
    return jnp.concatenate(dqs, axis=1), jnp.concatenate(dks, axis=1), jnp.concatenate(dvs, axis=1)


def _fn_res_ln(ctx, h, r, g, b):
    xhat, _ = _ln_stats(ALPHA * h + r)
    y = xhat * g + b
    return y, y


def _fn_res_ln_bwd(ctx, h, r, d1, d2, g):
    xhat, rstd = _ln_stats(ALPHA * h + r)
    return _ln_bwd(xhat, rstd, ALPHA * d1 + d2, g)


def _fn_res2_ln(ctx, h, r1, r2, g, b):
    xhat, _ = _ln_stats(ALPHA * h + (r1 + r2))
    return (xhat * g + b,)


def _fn_res2_ln_bwd(ctx, h, r1, r2, d1, d2, g):
    xhat, rstd = _ln_stats(ALPHA * h + (r1 + r2))
    return _ln_bwd(xhat, rstd, ALPHA * d1 + d2, g)


def _fn_in_ln_bwd(ctx, x, d1, d2, g):
    xhat, rstd = _ln_stats(x)
    return _ln_bwd(xhat, rstd, ALPHA * d1 + d2, g)


def _fn_final(ctx, h2, ff, tgt, g, b):
    xhat, rstd = _ln_stats(ALPHA * h2 + ff)
    e = xhat * g + b - tgt
    loss = 0.5 * _sum0(jnp.sum(e * e, axis=1, keepdims=True)) / D_MODEL
    dx, dg, db = _ln_bwd(xhat, rstd, e / D_MODEL, g)
    return dx, dx, dg, db, loss


def _epi_du(da, u):
    return da * 2.0 * jnp.maximum(u, 0.0)


def _relu2(u):
    r = jnp.maximum(u, 0.0)
    return r * r


def _fn_conv_bwd_a(ctx, y, dxs1, dxs2, dbc, dtr, ddt, dtb):
    sig = _sigmoid(y)
    dact = jnp.concatenate([dxs1 + dxs2, dbc], axis=1)
    dyc = dact * (sig * (1.0 + y * (1.0 - sig)))
    ddtr = ddt * _sigmoid(dtr + dtb)
    return dyc, ddtr, _sum0(dyc), _sum0(ddtr)


def _fn_conv_bwd_b(ctx, d, dn, u, up, w8):
    first, last = ctx.i == 0, ctx.i == ctx.n - 1
    du = d * w8[3:4]
    row = lax.broadcasted_iota(jnp.int32, w8.shape, 0)
    dw = jnp.where(row == 3, _sum0(d * u), 0.0)
    for s in (1, 2, 3):
        du = du + _shift_up(d, dn, s, last) * w8[3 - s:4 - s]
        dw = dw + jnp.where(row == 3 - s, _sum0(d * _shift_down(u, up, s, first)), 0.0)
    return du, dw


def _fn_adam(ctx, w, g, m, v):
    m = ADAM_B1 * m + (1.0 - ADAM_B1) * g
    v = ADAM_B2 * v + (1.0 - ADAM_B2) * (g * g)
    m_hat = m / (1.0 - ADAM_B1 ** ADAM_STEP)
    v_hat = v / (1.0 - ADAM_B2 ** ADAM_STEP)
    return -ADAM_LR * (m_hat / (jnp.sqrt(v_hat) + ADAM_EPS) + ADAM_WD * w), m, v


def _fn_add2(ctx, a, b):
    s = a + b
    return s, s


def _fn_add4(ctx, a, r0, r1, r2):
    return (((a + r0.astype(F32)) + r1.astype(F32)) + r2.astype(F32),)


def _z(r, c, dt):
    return jnp.zeros((r, c), dt)


W_IN_SHARD = 554
W_IN_GROUPS = [(0, 512, 1024), (512, 1536, 0), (1536, 1544, 1920), (1544, 1928, 1536), (1928, 2184, 2048),
               (2184, 2216, 2368)]


def _pad_w_in(ws):
    r, dt = ws.shape[1], ws.dtype

    def cols(a, b):
        out = []
        for k in range(N_SHARD):
            lo, hi = max(a, k * W_IN_SHARD), min(b, (k + 1) * W_IN_SHARD)
            if lo < hi:
                out.append(ws[k][:, lo - k * W_IN_SHARD:hi - k * W_IN_SHARD])
        return out

    return jnp.concatenate(cols(512, 1536) + cols(0, 512) + cols(1544, 1928) + cols(1536, 1544) + [_z(r, 120, dt)]
                           + cols(1928, 2184) + [_z(r, 64, dt)] + cols(2184, 2216) + [_z(r, 32, dt), _z(r, 128, dt)],
                           axis=1)


def _unpad_w_in(d):
    shards = []
    for k in range(N_SHARD):
        a, b = k * W_IN_SHARD, (k + 1) * W_IN_SHARD
        parts = []
        for o0, o1, p0 in W_IN_GROUPS:
            lo, hi = max(a, o0), min(b, o1)
            if lo < hi:
                parts.append(d[:, p0 + lo - o0:p0 + hi - o0])
        shards.append(jnp.concatenate(parts, axis=1))
    return jnp.stack(shards)


def _pad_heads(w, width):
    r = w.shape[0]
    w3 = w.reshape(r, MLA_HEADS, width)
    return jnp.pad(w3, ((0, 0), (0, 0), (0, 128 - width))).reshape(r, MLA_HEADS * 128)


def _pad_w_kv(w):
    r = w.shape[0]
    w4 = w.reshape(r, MLA_HEADS, 2, 64)
    return jnp.pad(w4, ((0, 0), (0, 0), (0, 0), (0, 64))).reshape(r, MLA_HEADS * 256)


def _unpad_w_kv(d):
    r = d.shape[0]
    return d.reshape(r, MLA_HEADS, 2, 128)[:, :, :, :64].reshape(r, MLA_HEADS * 128)


def _pad_w_mix(w):
    wo = jnp.pad(w[512:1024].reshape(MLA_HEADS, 64, D_MODEL), ((0, 0), (0, 64), (0, 0))).reshape(1024, D_MODEL)
    return jnp.concatenate([wo, w[0:512]], axis=0)


def _unpad_w_mix(d):
    do = d[:1024].reshape(MLA_HEADS, 128, D_MODEL)[:, :64].reshape(512, D_MODEL)
    return jnp.concatenate([d[1024:1536], do], axis=0)


def _row(v, width=None):
    v = v.reshape(1, -1).astype(F32)
    if width is not None and v.shape[1] < width:
        v = jnp.pad(v, ((0, 0), (0, width - v.shape[1])))
    return v


def _old_local_step(x, mem, positions, target, W, P):
    S = x.shape[0]
    tr = ROW_TILE
    w_in_p = _pad_w_in(W["w_in"])
    w_q_p = _pad_heads(W["w_q_up"], MLA_QK)
    w_kv3 = W["w_kv_up"].reshape(MLA_KV_RANK, MLA_HEADS, 128)
    w_k_p = _pad_heads(w_kv3[:, :, :64].reshape(MLA_KV_RANK, 512), 64)
    w_v_p = _pad_heads(w_kv3[:, :, 64:].reshape(MLA_KV_RANK, 512), 64)
    w_v_pt = w_v_p.T
    w_mix_y = W["w_mix_out"][0:512]
    w_mix_o = jnp.pad(W["w_mix_out"][512:1024].reshape(MLA_HEADS, 64, D_MODEL),
                      ((0, 0), (0, 64), (0, 0))).reshape(MLA_HEADS * 128, D_MODEL)
    conv_w8 = jnp.pad(P["conv_w"].astype(F32), ((0, 4), (0, 0)))
    conv_b = _row(P["conv_b"])
    dt_b = _row(P["dt_bias"], 128)
    a_head = -jnp.exp(P["a_log"].reshape(-1).astype(F32))
    a_row = _row(a_head, 128)
    dexp = jnp.repeat(P["d_skip"].reshape(-1).astype(F32), 64).reshape(1, 512)
    g_ssd, g_q, g_kv = _row(P["ssd_norm_g"]), _row(P["q_norm_g"]), _row(P["kv_norm_g"])
    g_in, b_in = _row(P["ln_in_g"]), _row(P["ln_in_b"])
    g1, b1, g2, b2, g3, b3 = (_row(P[k]) for k in ("ln1_g", "ln1_b", "ln2_g", "ln2_b", "ln3_g", "ln3_b"))

    half = MLA_ROPE // 2
    inv_freq = jnp.power(ROPE_THETA, -jnp.arange(half, dtype=F32) / half)
    ang = positions.reshape(S, 1).astype(F32) * inv_freq
    cos, sin = jnp.cos(ang), jnp.sin(ang)
    zc = lambda n: jnp.zeros((S, n), F32)
    rope_a = jnp.concatenate([jnp.ones((S, 64), F32), cos, cos, zc(32)], axis=1)
    rope_b = jnp.concatenate([zc(80), sin, zc(32)], axis=1)
    rope_c = jnp.concatenate([zc(64), -sin, zc(48)], axis=1)

    h0, h0_b = _rowwise(_fn_ln, [x], [g_in, b_in], [D_MODEL, (D_MODEL, BF16)], tr=tr, name="ln_in")
    proj = _mm(h0_b, w_in_p, form="nn", name="mm_in")
    conv_y, xbc, dt = _rowwise(
        _fn_conv_fwd, [(proj,) + SEG_XBC, ("prev", proj) + SEG_XBC, (proj,) + SEG_DT], [conv_w8, conv_b, dt_b],
        [1024, 1024, 128], tr=tr, name="conv_fwd")
    y_ssd, hs = _ssd_fwd(xbc, dt, a_row, name="ssd_fwd")
    (y_n,) = _rowwise(_fn_ssd_post, [y_ssd, (xbc, 0, 512), (proj,) + SEG_Z], [dexp, g_ssd], [(512, BF16)], tr=tr,
                      name="ssd_post")
    q_n, kv_n = _rowwise(_fn_mla_pre, [(proj,) + SEG_QLAT, (proj,) + SEG_KVLAT], [g_q, g_kv], [384, 256], tr=tr,
                         name="mla_pre")
    qp = _mm(q_n, w_q_p, form="nn", name="mm_q_up")
    kn = _mm(kv_n, w_k_p, form="nn", name="mm_k_up")
    v_nat = _mm(kv_n, w_v_p, form="nn", out_dtype=BF16, name="mm_v_up")
    v_t = _mm(w_v_pt, kv_n, form="nt", out_dtype=BF16, name="mm_v_up_t")
    q_rot, k_full = _rowwise(_fn_rope, [qp, kn, (proj,) + SEG_KR, rope_a, rope_b, rope_c], [],
                             [(1024, BF16), (1024, BF16)], tr=tr, name="rope")
    o_t, lse = _attn_fwd(q_rot, k_full, v_t, name="attn_fwd")
    mix_o = _mm(o_t, w_mix_o, form="tn", name="mm_mix_o")
    mix_y = _mm(y_n, w_mix_y, form="nn", name="mm_mix_y")
    (h1,) = _rowwise(_fn_res2_ln, [h0, mix_o, mix_y], [g1, b1], [D_MODEL], tr=tr, name="ln1")
    qm = _mm(h1, W["w_mem_q"], form="nn", name="mm_mem_q")
    km = _mm(mem, W["w_mem_k"], form="nn", name="mm_mem_k")
    vm = _mm(mem, W["w_mem_v"], form="nn", name="mm_mem_v")
    (om,) = _rowwise(_fn_mem_fwd, [qm], [km, vm], [(D_MODEL, BF16)], tr=tr, name="mem_fwd")
    xa = _mm(om, W["w_mem_o"], form="nn", name="mm_mem_o")
    h2, h2_b = _rowwise(_fn_res_ln, [h1, xa], [g2, b2], [D_MODEL, (D_MODEL, BF16)], tr=tr, name="ln2")
    u = _mm(h2, W["w_up"], form="nn", name="mm_up")
    ff = _mm(u, W["w_down"], form="nn", a_pro=_relu2, name="mm_down")

    dt3, dt3_b, dg3, db3, loss = _rowwise(_fn_final, [h2, ff, target], [g3, b3], [D_MODEL, (D_MODEL, BF16)],
                                   [(1, D_MODEL), (1, D_MODEL), (1, 128)], tr=tr, name="ln3_loss")
    da = _mm(dt3, W["w_down"], form="nt", name="mm_down_dx")
    dw_down = _mm(u, dt3, form="tn", a_pro=_relu2, name="mm_down_dw")
    dw_up = _mm(h2, du, form="tn", name="mm_up_dw")
    dh2 = _mm(du, W["w_up"], form="nt", name="mm_up_dx")
    dt2, dg2, db2 = _rowwise(_fn_res_ln_bwd, [h1, xa, dt3, dh2], [g2], [D_MODEL], [(1, D_MODEL)] * 2, tr=tr,
                             name="ln2_bwd")
    dom = _mm(dt2, W["w_mem_o"], form="nt", name="mm_mem_o_dx")
    dw_mem_o = _mm(om, dt2, form="tn", name="mm_mem_o_dw")
    dqm, dkm, dvm = _rowwise(_fn_mem_bwd, [qm, dom], [km, vm], [(D_MODEL, BF16)], [(256, D_MODEL)] * 2, tr=tr,
                             name="mem_bwd")
    dw_mem_q = _mm(h1, dqm, form="tn", name="mm_mem_q_dw")
    dw_mem_k = _mm(mem, dkm, form="tn", name="mm_mem_k_dw")
    dw_mem_v = _mm(mem, dvm, form="tn", name="mm_mem_v_dw")
    dh1 = _mm(dqm, W["w_mem_q"], form="nt", name="mm_mem_q_dx")
    dt1, dg1, db1 = _rowwise(_fn_res2_ln_bwd, [h0, mix_o, mix_y, dt2, dh1], [g1], [D_MODEL], [(1, D_MODEL)] * 2,
                             tr=tr, name="ln1_bwd")
    do_t = _mm(w_mix_o, dt1, form="nt", name="mm_mix_o_dx")
    dy_n = _mm(dt1, w_mix_y, form="nt", name="mm_mix_y_dx")
    dw_mix_o = _mm(o_t, dt1, form="nn", name="mm_mix_o_dw")
    dw_mix_y = _mm(y_n, dt1, form="tn", name="mm_mix_y_dw")
    dq_t, delta = _attn_bwd_dq(q_rot, k_full, k_full.T, v_nat, o_t, do_t, lse, name="attn_bwd_dq")
    dk_t, dv_t = _attn_bwd_dkv(q_rot, q_rot.T, k_full, v_nat, do_t, lse, delta, name="attn_bwd_dkv")
    dk = dk_t.T
    dqp, dkr = _rowwise(_fn_rope_bwd, [dq_t.T, dk, rope_a, rope_b, rope_c], [], [(1024, BF16), (128, BF16)], tr=tr,
                        name="rope_bwd")
    dw_q_p = _mm(q_n, dqp, form="tn", name="mm_q_up_dw")
    dq_n = _mm(dqp, w_q_p, form="nt", name="mm_q_up_dx")
    dw_k_p = _mm(kv_n, dk, form="tn", name="mm_k_up_dw")
    dkv_n1 = _mm(dk, w_k_p, form="nt", name="mm_k_up_dx")
    dw_v_pt = _mm(dv_t, kv_n, form="nn", name="mm_v_up_dw")
    dkv_n2 = _mm(dv_t, w_v_pt, form="tn", name="mm_v_up_dx")
    dq_lat, dkv_lat, dg_q, dg_kv = _rowwise(
        _fn_mla_pre_bwd, [(proj,) + SEG_QLAT, (proj,) + SEG_KVLAT, dq_n, dkv_n1, dkv_n2], [g_q, g_kv], [(384, BF16), (256, BF16)],
        [(1, 384), (1, 256)], tr=tr, name="mla_pre_bwd")
    dy_ssd, dz, dxs_skip, ddexp, dg_ssd = _rowwise(
        _fn_ssd_post_bwd, [dy_n, y_ssd, (xbc, 0, 512), (proj,) + SEG_Z], [dexp, g_ssd],
        [512, (512, BF16), 512], [(1, 512)] * 2, tr=tr, name="ssd_post_bwd")
    dxs, dbc, ddt, da_head = _ssd_bwd(xbc, dt, a_row, hs, dy_ssd, name="ssd_bwd")
    dyc, ddtr, dconv_b, ddt_b = _rowwise(
        _fn_conv_bwd_a, [conv_y, dxs, dxs_skip, dbc, (proj,) + SEG_DT, ddt], [dt_b], [1024, (128, BF16)],
        [(1, 1024), (1, 128)], tr=tr, name="conv_bwd_a")
    dxbc, dconv_w8 = _rowwise(
        _fn_conv_bwd_b, [dyc, ("next", dyc, 0, 1024), (proj,) + SEG_XBC, ("prev", proj) + SEG_XBC], [conv_w8], [(1024, BF16)],
        [(8, 1024)], tr=tr, name="conv_bwd_b")
    dproj = jnp.concatenate([dxbc, dz, dq_lat, ddtr, dkv_lat, dkr, jnp.zeros((S, 128), BF16)], axis=1)
    res = _mm(h0_b, dproj, form="tn", name="mm_in_dw", hosted=_pair_fill_step(gp) if dist else None)
    dw_in_p, red_a = (res[0], res[1]) if dist else (res, None)
    big_b = _group_b_grads(dw_in_p, dw_q_p, dw_k_p, dw_v_pt, dconv_w8)
    q_b = None
    if dist:
        gp_c, gp_b = _pack_group_b(big_b)
        dh0, theirs_c, theirs_b = _mm(dproj, w_in_p, form="nt", name="mm_in_dx", hosted=_merge_steps(
            [_pair_exchange_step(gp_c), _pair_exchange_step(gp_b)]))
        q_b = ((gp_c, theirs_c, _pair_sum(gp_c, theirs_c, "pair_sum_w_in")),
               (gp_b, theirs_b, _pair_sum(gp_b, theirs_b, "pair_sum_b")))
        gp = red_a
    else:
        dh0 = _mm(dproj, w_in_p, form="nt", name="mm_in_dx")
    grad_x, dg_in, db_in = _rowwise(_fn_in_ln_bwd, [x, dt1, dh0], [g_in], [D_MODEL], [(1, D_MODEL)] * 2, tr=tr,
                                    name="ln_in_bwd")

    big = {
        "w_in": _unpad_w_in(dw_in_p),
        "w_q_up": dw_q_p.reshape(384, MLA_HEADS, 128)[:, :, :MLA_QK].reshape(384, MLA_HEADS * MLA_QK),
        "w_kv_up": jnp.concatenate([dw_k_p.reshape(MLA_KV_RANK, MLA_HEADS, 128)[:, :, :64],
                                    dw_v_pt.T.reshape(MLA_KV_RANK, MLA_HEADS, 128)[:, :, :64]], axis=2).reshape(
                                        MLA_KV_RANK, MLA_HEADS * 128),
        "w_mix_out": jnp.concatenate([dw_mix_y, dw_mix_o.reshape(MLA_HEADS, 128, D_MODEL)[:, :64].reshape(
            512, D_MODEL)], axis=0),
        "w_mem_q": dw_mem_q, "w_mem_k": dw_mem_k, "w_mem_v": dw_mem_v, "w_mem_o": dw_mem_o,
        "w_up": dw_up, "w_down": dw_down,
        "conv_w": dconv_w8[0:4],
    }
    small = {
        "ln_in_g": dg_in, "ln_in_b": db_in, "conv_b": dconv_b, "dt_bias": ddt_b[:, :8],
        "a_log": da_head[:, :8] * a_head.reshape(1, 8),
        "d_skip": ddexp.reshape(8, 64).sum(axis=1).reshape(1, 8),
        "ssd_norm_g": dg_ssd, "q_norm_g": dg_q, "kv_norm_g": dg_kv,
        "ln1_g": dg1, "ln1_b": db1, "ln2_g": dg2, "ln2_b": db2, "ln3_g": dg3, "ln3_b": db3,
    }
    return loss[0, 0], grad_x, big, small


BIG = {
    "w_in": (1024, 2216, 1), "w_q_up": (384, 768, 1), "w_kv_up": (256, 1024, 1), "w_mix_out": (1024, 1024, 0),
    "w_mem_q": (1024, 1024, 0), "w_mem_k": (1024, 1024, 0), "w_mem_v": (1024, 1024, 0), "w_mem_o": (1024, 1024, 0),
    "w_up": (1024, 4096, 1), "w_down": (4096, 1024, 0), "conv_w": (4, 1024, 1),
}
BIG_ORDER = list(BIG)
SMALL_ORDER = ["ln_in_g", "ln_in_b", "conv_b", "dt_bias", "a_log", "d_skip", "ssd_norm_g", "q_norm_g", "kv_norm_g",
               "ln1_g", "ln1_b", "ln2_g", "ln2_b", "ln3_g", "ln3_b"]
N_SHARD = 4
PACK_COLS = 1024
PACK_ROWS = 4032
HALF_ROWS = PACK_ROWS // 2
GATHER_CHUNKS = 3
CHIP_CHUNKS = 3
PAIR_CHUNKS = 4


def _shard_shape(name):
    r, c, ax = BIG[name]
    return (r // N_SHARD, c) if ax == 0 else (r, c // N_SHARD)


def _split_shards(name, full):
    r, c, ax = BIG[name]
    if ax == 0:
        return full.reshape(N_SHARD, -1)
    return full.reshape(r, N_SHARD, c // N_SHARD).transpose(1, 0, 2).reshape(N_SHARD, -1)


def _join_shards(name, parts):
    r, c, ax = BIG[name]
    if ax == 0:
        return parts.reshape(r, c)
    return parts.reshape(N_SHARD, r, c // N_SHARD).transpose(1, 0, 2).reshape(r, c)


HBM = pl.BlockSpec(memory_space=pl.ANY)


def _place():
    x, y, c = lax.axis_index("x"), lax.axis_index("y"), lax.axis_index("c")
    chips = [(1 - x, y), (x, 1 - y), (1 - x, 1 - y)]
    return x, y, c, chips


def _gather_weights(wp):
    R, C = wp.shape
    H = R // 2
    nq = GATHER_CHUNKS
    CH = H // nq

    def body(w_ref, out_ref, send_sems, recv_sems):
        x, y, c, chips = _place()
        sib = (x, y, 1 - c)

        def piece(k, hc, q):
            return out_ref.at[k, pl.ds(hc * H + q * CH, CH), :]

        def copy(j, src, dst, to):
            return pltpu.make_async_remote_copy(src_ref=src, dst_ref=dst, send_sem=send_sems.at[j],
                                                recv_sem=recv_sems.at[j], device_id=to, device_id_type=MESH)

        me = 2 * x + y
        sends = []
        for q in range(nq):
            for j, (px, py) in enumerate(chips):
                cp = copy(j * nq + q, w_ref.at[pl.ds(c * H + q * CH, CH), :], piece(me, c, q), (px, py, c))
                cp.start()
                sends.append(cp)
        fwds = []
        for q in range(nq):
            for j, (px, py) in enumerate(chips):
                k = 2 * px + py
                copy(j * nq + q, piece(k, c, q), piece(k, c, q), (px, py, c)).wait_recv()
                f = copy((3 + j) * nq + q, piece(k, c, q), piece(k, c, q), sib)
                f.start()
                fwds.append(f)
        for q in range(nq):
            for j, (px, py) in enumerate(chips):
                k = 2 * px + py
                copy((3 + j) * nq + q, piece(k, 1 - c, q), piece(k, 1 - c, q), sib).wait_recv()
        for cp in sends + fwds:
            cp.wait_send()

    out = pl.pallas_call(
        body, name="gather_weights", in_specs=[HBM], out_specs=HBM,
        out_shape=jax.ShapeDtypeStruct((N_SHARD, R, C), wp.dtype),
        scratch_shapes=[pltpu.SemaphoreType.DMA((6 * nq,)), pltpu.SemaphoreType.DMA((6 * nq,))],
    )(wp)
    me = 2 * lax.axis_index("x") + lax.axis_index("y")
    return lax.dynamic_update_slice(out, wp[None], (me, 0, 0))


def _pair_exchange(gp):
    n, R, C = gp.shape
    H = R // 2
    nq = PAIR_CHUNKS
    CH = H // nq

    def body(g_ref, theirs_ref, send_sems, recv_sems):
        x, y, c, _ = _place()
        swaps = []
        for k in range(n):
            for q in range(nq):
                cp = pltpu.make_async_remote_copy(
                    src_ref=g_ref.at[k, pl.ds((1 - c) * H + q * CH, CH), :], dst_ref=theirs_ref.at[k, pl.ds(q * CH, CH), :],
                    send_sem=send_sems.at[k * nq + q], recv_sem=recv_sems.at[k * nq + q], device_id=(x, y, 1 - c),
                    device_id_type=MESH)
                cp.start()
                swaps.append(cp)
        for cp in swaps:
            cp.wait()

    theirs = pl.pallas_call(
        body, name="pair_exchange", in_specs=[HBM], out_specs=HBM,
        out_shape=jax.ShapeDtypeStruct((n, H, C), gp.dtype),
        scratch_shapes=[pltpu.SemaphoreType.DMA((n * nq,)), pltpu.SemaphoreType.DMA((n * nq,))],
    )(gp)
    mine = lax.dynamic_slice(gp, (0, lax.axis_index("c") * H, 0), (n, H, C))
    return mine, theirs


def _chip_exchange(pb):
    n, H, C = pb.shape
    nq = CHIP_CHUNKS
    CH = H // nq

    def body(pb_ref, got_ref, send_sems, recv_sems):
        x, y, c, chips = _place()
        sends = []
        for q in range(nq):
            for j, (px, py) in enumerate(chips):
                cp = pltpu.make_async_remote_copy(
                    src_ref=pb_ref.at[2 * px + py, pl.ds(q * CH, CH), :], dst_ref=got_ref.at[j, pl.ds(q * CH, CH), :],
                    send_sem=send_sems.at[j * nq + q], recv_sem=recv_sems.at[j * nq + q],
                    device_id=(px, py, c), device_id_type=MESH)
                cp.start()
                sends.append(cp)
        for cp in sends:
            cp.wait()

    return pl.pallas_call(
        body, name="chip_exchange", in_specs=[HBM], out_specs=HBM,
        out_shape=jax.ShapeDtypeStruct((3, H, C), BF16),
        scratch_shapes=[pltpu.SemaphoreType.DMA((3 * nq,)), pltpu.SemaphoreType.DMA((3 * nq,))],
    )(pb)


def _pair_join(q):
    H, C = q.shape
    nq = PAIR_CHUNKS
    CH = H // nq

    def body(q_ref, theirs_ref, send_sems, recv_sems):
        x, y, c, _ = _place()
        pushes = []
        for j in range(nq):
            cp = pltpu.make_async_remote_copy(
                src_ref=q_ref.at[pl.ds(j * CH, CH), :], dst_ref=theirs_ref.at[pl.ds(j * CH, CH), :],
                send_sem=send_sems.at[j], recv_sem=recv_sems.at[j], device_id=(x, y, 1 - c), device_id_type=MESH)
            cp.start()
            pushes.append(cp)
        for cp in pushes:
            cp.wait()

    theirs = pl.pallas_call(
        body, name="pair_join", in_specs=[HBM], out_specs=HBM,
        out_shape=jax.ShapeDtypeStruct((H, C), F32),
        scratch_shapes=[pltpu.SemaphoreType.DMA((nq,)), pltpu.SemaphoreType.DMA((nq,))],
    )(q)
    c = lax.axis_index("c")
    out = jnp.zeros((2 * H, C), F32)
    out = lax.dynamic_update_slice(out, q, (c * H, 0))
    return lax.dynamic_update_slice(out, theirs, ((1 - c) * H, 0))


N_DEV = 8


def _small_all_reduce(g, step=None):
    r, cdim = g.shape
    si, so = (len(step.inputs), len(step.out_shapes)) if step else (0, 0)

    def body(g_ref, *refs):
        s_ins, out_ref, s_outs = refs[:si], refs[si], refs[si + 1:si + 1 + so]
        buf, send_sems, recv_sems = refs[si + 1 + so:si + 4 + so]
        s_sems = refs[si + 4 + so:]
        if step:
            step.start(s_ins, s_outs, s_sems)
        x, y, c, _ = _place()
        me = 4 * x + 2 * y + c
        buf[me] = g_ref[...]
        copies = []
        for d in range(1, N_DEV):
            to = me ^ d
            cp = pltpu.make_async_remote_copy(src_ref=g_ref, dst_ref=buf.at[me], send_sem=send_sems.at[d - 1],
                                              recv_sem=recv_sems.at[d - 1],
                                              device_id=(to // 4, (to // 2) % 2, to % 2), device_id_type=MESH)
            cp.start()
            copies.append(cp)
        for cp in copies:
            cp.wait()
        acc = buf[0]
        for d in range(1, N_DEV):
            acc = acc + buf[d]
        out_ref[...] = acc
        if step:
            step.finish(s_ins, s_outs, s_sems)

    res = pl.pallas_call(
        body, name="small_all_reduce",
        in_specs=[pl.BlockSpec(memory_space=pltpu.VMEM)] + [HBM] * si,
        out_specs=[pl.BlockSpec(memory_space=pltpu.VMEM)] + [HBM] * so,
        out_shape=[jax.ShapeDtypeStruct((r, cdim), F32)] + (list(step.out_shapes) if step else []),
        scratch_shapes=[pltpu.VMEM((N_DEV, r, cdim), F32), pltpu.SemaphoreType.DMA((N_DEV - 1,)),
                        pltpu.SemaphoreType.DMA((N_DEV - 1,))] + (_sem_scratch(step) if step else []),
    )(g, *(step.inputs if step else []))
    return res if step else res[0]


def _adam(w, g, m, v, name):
    shape = w.shape
    w2, g2, m2, v2 = (t.reshape(-1, shape[-1]) for t in (w, g, m, v))
    d, mn, vn = _rowwise(_fn_adam, [w2, g2, m2, v2], [], [shape[-1]] * 3, tr=256, name=name)
    return d.reshape(shape), mn.reshape(shape), vn.reshape(shape)


def _old_kernel(x, mem, positions, ln_in_g, ln_in_b, w_in, conv_w, conv_b, dt_bias, a_log, d_skip, ssd_norm_g, q_norm_g, w_q_up, kv_norm_g, w_kv_up, w_mix_out, ln1_g, ln1_b, w_mem_q, w_mem_k, w_mem_v, w_mem_o, ln2_g, ln2_b, w_up, w_down, ln3_g, ln3_b, loss_target, m_ln_in_g, m_ln_in_b, m_w_in, m_conv_w, m_conv_b, m_dt_bias, m_a_log, m_d_skip, m_ssd_norm_g, m_q_norm_g, m_w_q_up, m_kv_norm_g, m_w_kv_up, m_w_mix_out, m_ln1_g, m_ln1_b, m_w_mem_q, m_w_mem_k, m_w_mem_v, m_w_mem_o, m_ln2_g, m_ln2_b, m_w_up, m_w_down, m_ln3_g, m_ln3_b, v_ln_in_g, v_ln_in_b, v_w_in, v_conv_w, v_conv_b, v_dt_bias, v_a_log, v_d_skip, v_ssd_norm_g, v_q_norm_g, v_w_q_up, v_kv_norm_g, v_w_kv_up, v_w_mix_out, v_ln1_g, v_ln1_b, v_w_mem_q, v_w_mem_k, v_w_mem_v, v_w_mem_o, v_ln2_g, v_ln2_b, v_w_up, v_w_down, v_ln3_g, v_ln3_b):
    args = dict(locals())
    weights = BIG_ORDER + SMALL_ORDER

    flat = []
    for n in BIG_ORDER:
        s = args[n].reshape(-1)
        if n == "conv_w":
            flat.append(lax.bitcast_convert_type(s.astype(F32), BF16).reshape(-1))
        else:
            flat.append(s.astype(BF16))
    flat = jnp.concatenate(flat)
    wp = jnp.pad(flat, (0, PACK_ROWS * PACK_COLS - flat.shape[0])).reshape(PACK_ROWS, PACK_COLS)
    gathered = _gather_weights(wp).reshape(N_SHARD, -1)
    W, off = {}, 0
    for n in BIG_ORDER:
        sr, sc = _shard_shape(n)
        cnt = sr * sc
        if n == "conv_w":
            part = lax.bitcast_convert_type(gathered[:, off:off + 2 * cnt].reshape(N_SHARD, cnt, 2), F32)
            off += 2 * cnt
        else:
            part = gathered[:, off:off + cnt]
            off += cnt
        W[n] = _join_shards(n, part)
    P = {n: args[n] for n in SMALL_ORDER}
    P["conv_w"] = W.pop("conv_w")

    loss, grad_x, gbig, gsmall = _local_step(x[0], mem[0], positions[0], loss_target[0], W, P)
    loss = lax.psum(loss, ("x", "y", "c"))

    gflat = jnp.concatenate([_split_shards(n, gbig[n]) for n in BIG_ORDER], axis=1)
    gp = jnp.pad(gflat, ((0, 0), (0, PACK_ROWS * PACK_COLS - gflat.shape[1]))).reshape(N_SHARD, PACK_ROWS, PACK_COLS)
    mine, theirs = _pair_exchange(gp)
    pf, pb = _rowwise(_fn_add2, [mine.reshape(-1, PACK_COLS), theirs.reshape(-1, PACK_COLS)], [],
                      [PACK_COLS, (PACK_COLS, BF16)], tr=288, name="pair_sum")
    pf = pf.reshape(N_SHARD, HALF_ROWS, PACK_COLS)
    pb = pb.reshape(N_SHARD, HALF_ROWS, PACK_COLS)
    got = _chip_exchange(pb).reshape(3 * HALF_ROWS, PACK_COLS)
    own = lax.dynamic_index_in_dim(pf, 2 * lax.axis_index("x") + lax.axis_index("y"), axis=0, keepdims=False)
    (q,) = _rowwise(_fn_add4, [own] + [(got, 0, PACK_COLS, j * HALF_ROWS) for j in range(3)], [], [PACK_COLS],
                    tr=288, name="chip_sum", n_rows=HALF_ROWS)
    red = _pair_join(q).reshape(-1)

    gs = jnp.concatenate([_row(gsmall[n], PACK_COLS) for n in SMALL_ORDER] + [jnp.zeros((1, PACK_COLS), F32)], axis=0)
    gs = _small_all_reduce(gs)

    grads, deltas, new_m, new_v = {}, {}, {}, {}
    off = 0
    for n in BIG_ORDER:
        sr, sc = _shard_shape(n)
        g = red[off:off + sr * sc].reshape(args[n].shape)
        off += sr * sc
        grads[n] = g
        deltas[n], new_m[n], new_v[n] = _adam(args[n], g, args["m_" + n], args["v_" + n], "adam_" + n)
    pack = lambda pre: jnp.concatenate([_row(args[pre + n], PACK_COLS) for n in SMALL_ORDER]
                                       + [jnp.zeros((1, PACK_COLS), F32)], axis=0)
    ds, ms, vs = _rowwise(_fn_adam, [pack(""), gs, pack("m_"), pack("v_")], [], [PACK_COLS] * 3, tr=16,
                          name="adam_small")
    for i, n in enumerate(SMALL_ORDER):
        cnt = args[n].size
        take = lambda t: t[i, :cnt].reshape(args[n].shape)
        grads[n], deltas[n], new_m[n], new_v[n] = take(gs), take(ds), take(ms), take(vs)

    order = ["ln_in_g", "ln_in_b", "w_in", "conv_w", "conv_b", "dt_bias", "a_log", "d_skip", "ssd_norm_g",
             "q_norm_g", "w_q_up", "kv_norm_g", "w_kv_up", "w_mix_out", "ln1_g", "ln1_b", "w_mem_q", "w_mem_k",
             "w_mem_v", "w_mem_o", "ln2_g", "ln2_b", "w_up", "w_down", "ln3_g", "ln3_b"]
    assert sorted(order) == sorted(weights)
    return (loss, grad_x[None], *[grads[n] for n in order], *[deltas[n] for n in order],
            *[new_m[n] for n in order], *[new_v[n] for n in order])


PACK_A_ROW = {"w_down": 0, "w_up": 1024, "w_mem_q": 2048, "w_mem_k": 2304, "w_mem_v": 2560, "w_mem_o": 2816,
              "w_mix_out": 3072}
PACK_A_ORDER = list(PACK_A_ROW)
PACK_A_ROWS = 3328
PACK_B_ORDER = ["w_q_up", "w_kv_up", "conv_w"]
PACK_B_ROWS = 160


def _mesh_pos():
    return 2 * lax.axis_index("x") + lax.axis_index("y"), lax.axis_index("c")


def _local_step(x, mem, positions, target, WB, P, *, wp_a=None, g_a=None):
    S = x.shape[0]
    tr = ROW_TILE
    dist = g_a is None
    w_in_p = _pad_w_in(WB["w_in"])
    w_q_p = _pad_heads(WB["w_q_up"], MLA_QK)
    w_kv3 = WB["w_kv_up"].reshape(MLA_KV_RANK, MLA_HEADS, 128)
    w_k_p = _pad_heads(w_kv3[:, :, :64].reshape(MLA_KV_RANK, 512), 64)
    w_v_p = _pad_heads(w_kv3[:, :, 64:].reshape(MLA_KV_RANK, 512), 64)
    w_v_pt = w_v_p.T
    conv_w8 = jnp.pad(P["conv_w"].astype(F32), ((0, 4), (0, 0)))
    conv_b = _row(P["conv_b"])
    dt_b = _row(P["dt_bias"], 128)
    a_head = -jnp.exp(P["a_log"].reshape(-1).astype(F32))
    a_row = _row(a_head, 128)
    dexp = jnp.repeat(P["d_skip"].reshape(-1).astype(F32), 64).reshape(1, 512)
    g_ssd, g_q, g_kv = _row(P["ssd_norm_g"]), _row(P["q_norm_g"]), _row(P["kv_norm_g"])
    g_in, b_in = _row(P["ln_in_g"]), _row(P["ln_in_b"])
    g1, b1, g2, b2, g3, b3 = (_row(P[k]) for k in ("ln1_g", "ln1_b", "ln2_g", "ln2_b", "ln3_g", "ln3_b"))

    half = MLA_ROPE // 2
    inv_freq = jnp.power(ROPE_THETA, -jnp.arange(half, dtype=F32) / half)
    ang = positions.reshape(S, 1).astype(F32) * inv_freq
    cos, sin = jnp.cos(ang), jnp.sin(ang)
    zc = lambda n: jnp.zeros((S, n), F32)
    rope_a = jnp.concatenate([jnp.ones((S, 64), F32), cos, cos, zc(32)], axis=1)
    rope_b = jnp.concatenate([zc(80), sin, zc(32)], axis=1)
    rope_c = jnp.concatenate([zc(64), -sin, zc(48)], axis=1)

    h0, h0_b = _rowwise(_fn_ln, [x], [g_in, b_in], [D_MODEL, (D_MODEL, BF16)], tr=tr, name="ln_in")
    proj = _mm(h0_b, w_in_p, form="nn", name="mm_in")
    conv_y, xbc, dt = _rowwise(
        _fn_conv_fwd, [(proj,) + SEG_XBC, ("prev", proj) + SEG_XBC, (proj,) + SEG_DT], [conv_w8, conv_b, dt_b],
        [1024, 1024, 128], tr=tr, name="conv_fwd")
    y_ssd, hs = _ssd_fwd(xbc, dt, a_row, name="ssd_fwd")
    (y_n,) = _rowwise(_fn_ssd_post, [y_ssd, (xbc, 0, 512), (proj,) + SEG_Z], [dexp, g_ssd], [(512, BF16)], tr=tr,
                      name="ssd_post")
    q_n, kv_n = _rowwise(_fn_mla_pre, [(proj,) + SEG_QLAT, (proj,) + SEG_KVLAT], [g_q, g_kv], [384, 256], tr=tr,
                         name="mla_pre")
    qp = _mm(q_n, w_q_p, form="nn", name="mm_q_up")
    kn = _mm(kv_n, w_k_p, form="nn", name="mm_k_up")
    v_nat = _mm(kv_n, w_v_p, form="nn", out_dtype=BF16, name="mm_v_up")
    v_t = _mm(w_v_pt, kv_n, form="nt", out_dtype=BF16, name="mm_v_up_t")
    q_rot, k_full = _rowwise(_fn_rope, [qp, kn, (proj,) + SEG_KR, rope_a, rope_b, rope_c], [],
                             [(1024, BF16), (1024, BF16)], tr=tr, name="rope")
    res = _attn_fwd(q_rot, k_full, v_t, name="attn_fwd", hosted=_gather_step(wp_a) if dist else None)
    o_t, lse = res[0], res[1]
    if dist:
        g_a = lax.dynamic_update_slice(res[2], wp_a[None], (_mesh_pos()[0], 0, 0))
    r_mix = PACK_A_ROW["w_mix_out"]
    w_mix_o = jnp.pad(g_a[2:4, r_mix:r_mix + 256].reshape(MLA_HEADS, 64, D_MODEL),
                      ((0, 0), (0, 64), (0, 0))).reshape(MLA_HEADS * 128, D_MODEL)
    mix_o = _mm(o_t, w_mix_o, form="tn", name="mm_mix_o")
    mix_y = _mm(y_n, g_a, form="nn", b_pack="w_mix_out", name="mm_mix_y")
    (h1,) = _rowwise(_fn_res2_ln, [h0, mix_o, mix_y], [g1, b1], [D_MODEL], tr=tr, name="ln1")
    qm = _mm(h1, g_a, form="nn", b_pack="w_mem_q", out_dtype=BF16, name="mm_mem_q")
    km = _mm(mem, g_a, form="nn", b_pack="w_mem_k", out_dtype=BF16, name="mm_mem_k")
    vm = _mm(mem, g_a, form="nn", b_pack="w_mem_v", out_dtype=BF16, name="mm_mem_v")
    (om,) = _rowwise(_fn_mem_fwd, [qm], [km, vm], [(D_MODEL, BF16)], tr=tr, name="mem_fwd")
    xa = _mm(om, g_a, form="nn", b_pack="w_mem_o", name="mm_mem_o")
    h2, h2_b = _rowwise(_fn_res_ln, [h1, xa], [g2, b2], [D_MODEL, (D_MODEL, BF16)], tr=tr, name="ln2")
    u = _mm(h2_b, g_a, form="nn", b_pack="w_up", name="mm_up")
    ff = _mm(u, g_a, form="nn", a_pro=_relu2, b_pack="w_down", name="mm_down")

    gp = lax.empty((N_SHARD, PACK_A_ROWS, PACK_COLS), F32)
    dt3, dt3_b, dg3, db3, loss = _rowwise(_fn_final, [h2, ff, target], [g3, b3], [D_MODEL, (D_MODEL, BF16)],
                                   [(1, D_MODEL), (1, D_MODEL), (1, 128)], tr=tr, name="ln3_loss")
    du = _mm(dt3_b, g_a, form="nt", b_pack="w_down", epi=(_epi_du, u), out_dtype=BF16, name="mm_down_dx")
    gp = _mm(u, dt3_b, form="tn", a_pro=_relu2, out_pack=("w_down", gp), name="mm_down_dw")
    gp = _mm(h2_b, du, form="tn", out_pack=("w_up", gp), name="mm_up_dw")
    dh2 = _mm(du, g_a, form="nt", b_pack="w_up", name="mm_up_dx")
    dt2, dg2, db2 = _rowwise(_fn_res_ln_bwd, [h1, xa, dt3, dh2], [g2], [D_MODEL], [(1, D_MODEL)] * 2, tr=tr,
                             name="ln2_bwd")
    dom = _mm(dt2, g_a, form="nt", b_pack="w_mem_o", out_dtype=BF16, name="mm_mem_o_dx")
    gp = _mm(om, dt2, form="tn", out_pack=("w_mem_o", gp), name="mm_mem_o_dw")
    dqm, dkm, dvm = _rowwise(_fn_mem_bwd, [qm, dom], [km, vm], [(D_MODEL, BF16)], [(256, D_MODEL)] * 2, tr=tr,
                             name="mem_bwd")
    gp = _mm(h1, dqm, form="tn", out_pack=("w_mem_q", gp), name="mm_mem_q_dw")
    gp = _mm(mem, dkm, form="tn", out_pack=("w_mem_k", gp), name="mm_mem_k_dw")
    gp = _mm(mem, dvm, form="tn", out_pack=("w_mem_v", gp), name="mm_mem_v_dw")
    dh1 = _mm(dqm, g_a, form="nt", b_pack="w_mem_q", name="mm_mem_q_dx")
    dt1, dg1, db1 = _rowwise(_fn_res2_ln_bwd, [h0, mix_o, mix_y, dt2, dh1], [g1], [D_MODEL], [(1, D_MODEL)] * 2,
                             tr=tr, name="ln1_bwd")
    do_t = _mm(w_mix_o, dt1, form="nt", name="mm_mix_o_dx")
    dy_n = _mm(dt1, g_a, form="nt", b_pack="w_mix_out", b_rows=512, name="mm_mix_y_dx")
    dw_mix_o = _mm(o_t, dt1, form="nn", name="mm_mix_o_dw")
    gp = _mm(y_n, dt1, form="tn", out_pack=("w_mix_out", gp), name="mm_mix_y_dw")
    gp = lax.dynamic_update_slice(
        gp, dw_mix_o.reshape(MLA_HEADS, 128, D_MODEL)[:, :64].reshape(2, 256, D_MODEL), (2, r_mix, 0))
    me, c = _mesh_pos() if dist else (0, 0)
    ha = PACK_A_ROWS // 2
    res = _attn_bwd_dq(q_rot, k_full, v_nat, o_t, do_t, lse, name="attn_bwd_dq",
                       hosted=_pair_exchange_step(gp) if dist else None)
    dq_rot, delta = res[0], res[1]
    chip_step = None
    if dist:
        theirs_a = res[2]
        chip_step = _chip_exchange_step(_pair_sum(gp, theirs_a, "pair_sum_a"))
    res = _attn_bwd_dkv(q_rot, k_full, v_nat, do_t, lse, delta, name="attn_bwd_dkv", hosted=chip_step)
    dk, dv_t = res[0], res[1]
    if dist:
        gp = _chip_sum(gp, theirs_a, res[2], "chip_sum_a")
    dqp, dkr = _rowwise(_fn_rope_bwd, [dq_rot, dk, rope_a, rope_b, rope_c], [], [(1024, BF16), (128, BF16)], tr=tr,
                        name="rope_bwd")
    dw_q_p = _mm(q_n, dqp, form="tn", name="mm_q_up_dw")
    dq_n = _mm(dqp, w_q_p, form="nt", name="mm_q_up_dx")
    dw_k_p = _mm(kv_n, dk, form="tn", name="mm_k_up_dw")
    dkv_n1 = _mm(dk, w_k_p, form="nt", name="mm_k_up_dx")
    dw_v_pt = _mm(dv_t, kv_n, form="nn", name="mm_v_up_dw")
    dkv_n2 = _mm(dv_t, w_v_pt, form="tn", name="mm_v_up_dx")
    dq_lat, dkv_lat, dg_q, dg_kv = _rowwise(
        _fn_mla_pre_bwd, [(proj,) + SEG_QLAT, (proj,) + SEG_KVLAT, dq_n, dkv_n1, dkv_n2], [g_q, g_kv], [(384, BF16), (256, BF16)],
        [(1, 384), (1, 256)], tr=tr, name="mla_pre_bwd")
    dy_ssd, dz, dxs_skip, ddexp, dg_ssd = _rowwise(
        _fn_ssd_post_bwd, [dy_n, y_ssd, (xbc, 0, 512), (proj,) + SEG_Z], [dexp, g_ssd],
        [512, (512, BF16), 512], [(1, 512)] * 2, tr=tr, name="ssd_post_bwd")
    dxs, dbc, ddt, da_head = _ssd_bwd(xbc, dt, a_row, hs, dy_ssd, name="ssd_bwd")
    dyc, ddtr, dconv_b, ddt_b = _rowwise(
        _fn_conv_bwd_a, [conv_y, dxs, dxs_skip, dbc, (proj,) + SEG_DT, ddt], [dt_b], [1024, (128, BF16)],
        [(1, 1024), (1, 128)], tr=tr, name="conv_bwd_a")
    dxbc, dconv_w8 = _rowwise(
        _fn_conv_bwd_b, [dyc, ("next", dyc, 0, 1024), (proj,) + SEG_XBC, ("prev", proj) + SEG_XBC], [conv_w8], [(1024, BF16)],
        [(8, 1024)], tr=tr, name="conv_bwd_b")
    dproj = jnp.concatenate([dxbc, dz, dq_lat, ddtr, dkv_lat, dkr, jnp.zeros((S, 128), BF16)], axis=1)
    res = _mm(h0_b, dproj, form="tn", name="mm_in_dw", hosted=_pair_fill_step(gp) if dist else None)
    dw_in_p, red_a = (res[0], res[1]) if dist else (res, None)
    big_b = _group_b_grads(dw_in_p, dw_q_p, dw_k_p, dw_v_pt, dconv_w8)
    q_b = None
    if dist:
        gp_c, gp_b = _pack_group_b(big_b)
        dh0, theirs_c, theirs_b = _mm(dproj, w_in_p, form="nt", name="mm_in_dx", hosted=_merge_steps(
            [_pair_exchange_step(gp_c), _pair_exchange_step(gp_b)]))
        q_b = ((gp_c, theirs_c, _pair_sum(gp_c, theirs_c, "pair_sum_w_in")),
               (gp_b, theirs_b, _pair_sum(gp_b, theirs_b, "pair_sum_b")))
        gp = red_a
    else:
        dh0 = _mm(dproj, w_in_p, form="nt", name="mm_in_dx")
    grad_x, dg_in, db_in = _rowwise(_fn_in_ln_bwd, [x, dt1, dh0], [g_in], [D_MODEL], [(1, D_MODEL)] * 2, tr=tr,
                                    name="ln_in_bwd")

    small = {
        "ln_in_g": dg_in, "ln_in_b": db_in, "conv_b": dconv_b, "dt_bias": ddt_b[:, :8],
        "a_log": da_head[:, :8] * a_head.reshape(1, 8),
        "d_skip": ddexp.reshape(8, 64).sum(axis=1).reshape(1, 8),
        "ssd_norm_g": dg_ssd, "q_norm_g": dg_q, "kv_norm_g": dg_kv,
        "ln1_g": dg1, "ln1_b": db1, "ln2_g": dg2, "ln2_b": db2, "ln3_g": dg3, "ln3_b": db3,
    }
    return loss[0, 0], grad_x, (gp, q_b), big_b, small


def _group_b_grads(dw_in_p, dw_q_p, dw_k_p, dw_v_pt, dconv_w8):
    return {
        "w_in": _unpad_w_in(dw_in_p),
        "w_q_up": dw_q_p.reshape(384, MLA_HEADS, 128)[:, :, :MLA_QK].reshape(384, MLA_HEADS * MLA_QK),
        "w_kv_up": jnp.concatenate([dw_k_p.reshape(MLA_KV_RANK, MLA_HEADS, 128)[:, :, :64],
                                    dw_v_pt.T.reshape(MLA_KV_RANK, MLA_HEADS, 128)[:, :, :64]], axis=2).reshape(
                                        MLA_KV_RANK, MLA_HEADS * 128),
        "conv_w": dconv_w8[0:4],
    }


def _pack_group_b(big_b):
    gflat = [_split_shards(n, big_b[n]) for n in PACK_B_ORDER]
    used = sum(f.shape[1] for f in gflat)
    gflat.append(jnp.zeros((N_SHARD, PACK_B_ROWS * PACK_COLS - used), F32))
    return big_b["w_in"], jnp.concatenate(gflat, axis=1).reshape(N_SHARD, PACK_B_ROWS, PACK_COLS)


def _half_tile(h):
    return next(t for t in range(512, 0, -16) if h % t == 0)


def _pair_sum(gp, theirs, name):
    n, R, C = gp.shape
    H = R // 2
    tr = _half_tile(H)
    nb = H // tr

    def body(s_ref, g_ref, t_ref, o_ref):
        o_ref[...] = (g_ref[...] + t_ref[...]).astype(o_ref.dtype)

    return pl.pallas_call(
        body, name=name,
        grid_spec=pltpu.PrefetchScalarGridSpec(
            num_scalar_prefetch=1, grid=(n, nb),
            in_specs=[pl.BlockSpec((1, tr, C), lambda k, i, s: (k, s[0] * nb + i, 0)),
                      pl.BlockSpec((1, tr, C), lambda k, i, s: (k, i, 0))],
            out_specs=pl.BlockSpec((1, tr, C), lambda k, i, s: (k, i, 0))),
        out_shape=jax.ShapeDtypeStruct((n, H, C), BF16), compiler_params=_params(("arbitrary", "arbitrary")),
    )(jnp.stack([lax.axis_index("c")]).astype(jnp.int32), gp, theirs)


def _chip_sum(gp, theirs, got, name):
    n, R, C = gp.shape
    H = R // 2
    tr = _half_tile(H)
    nb = H // tr

    def body(s_ref, g_ref, t_ref, r_ref, o_ref):
        acc = g_ref[0] + t_ref[0]
        for j in range(3):
            acc = acc + r_ref[j].astype(F32)
        o_ref[...] = acc

    me, c = _mesh_pos()
    return pl.pallas_call(
        body, name=name,
        grid_spec=pltpu.PrefetchScalarGridSpec(
            num_scalar_prefetch=1, grid=(nb,),
            in_specs=[pl.BlockSpec((1, tr, C), lambda i, s: (s[0], s[1] * nb + i, 0)),
                      pl.BlockSpec((1, tr, C), lambda i, s: (s[0], i, 0)),
                      pl.BlockSpec((3, tr, C), lambda i, s: (0, i, 0))],
            out_specs=pl.BlockSpec((tr, C), lambda i, s: (s[1] * nb + i, 0))),
        out_shape=jax.ShapeDtypeStruct((R, C), F32), compiler_params=_params(("arbitrary",)),
    )(jnp.stack([me, c]).astype(jnp.int32), gp, theirs, got)


def _pair_fill_step(red):
    R, C = red.shape
    H = R // 2
    nq = _chunks(H, 8)
    CH = H // nq

    def copies(ins, outs, sems):
        x, y, c, _ = _place()
        return [_remote(ins[0].at[pl.ds(c * H + j * CH, CH), :], outs[0].at[pl.ds(c * H + j * CH, CH), :], sems, j,
                        (x, y, 1 - c)) for j in range(nq)]

    def start(ins, outs, sems):
        for cp in copies(ins, outs, sems):
            cp.start()

    def finish(ins, outs, sems):
        for cp in copies(ins, outs, sems):
            cp.wait()

    step = _Step([red], [jax.ShapeDtypeStruct((R, C), red.dtype)], nq, start, finish)
    step.alias = [(0, 0)]
    return step


def _reduce_scatter(gp, tag):
    n, R, C = gp.shape
    H = R // 2
    me, c = _mesh_pos()
    (theirs,) = _run_step(_pair_exchange_step(gp), "pair_exchange_" + tag)
    mine = lax.dynamic_slice(gp, (0, c * H, 0), (n, H, C))
    pf, pb = _rowwise(_fn_add2, [mine.reshape(-1, C), theirs.reshape(-1, C)], [], [C, (C, BF16)], tr=512,
                      name="pair_sum_" + tag)
    (got,) = _run_step(_chip_exchange_step(pb.reshape(n, H, C)), "chip_exchange_" + tag)
    own = lax.dynamic_index_in_dim(pf.reshape(n, H, C), me, axis=0, keepdims=False)
    got = got.reshape(3 * H, C)
    (q,) = _rowwise(_fn_add4, [own] + [(got, 0, C, j * H) for j in range(3)], [], [C], tr=512,
                    name="chip_sum_" + tag, n_rows=H)
    return q


def _adam(w, g, m, v, name):
    shape = w.shape
    w2, m2, v2 = (t.reshape(-1, shape[-1]) for t in (w, m, v))
    g2 = (g[0], 0, shape[-1], g[1]) if isinstance(g, tuple) else g.reshape(-1, shape[-1])
    d, mn, vn = _rowwise(_fn_adam, [w2, g2, m2, v2], [], [shape[-1]] * 3, tr=256, name=name)
    return d.reshape(shape), mn.reshape(shape), vn.reshape(shape)


def kernel(x, mem, positions, ln_in_g, ln_in_b, w_in, conv_w, conv_b, dt_bias, a_log, d_skip, ssd_norm_g, q_norm_g, w_q_up, kv_norm_g, w_kv_up, w_mix_out, ln1_g, ln1_b, w_mem_q, w_mem_k, w_mem_v, w_mem_o, ln2_g, ln2_b, w_up, w_down, ln3_g, ln3_b, loss_target, m_ln_in_g, m_ln_in_b, m_w_in, m_conv_w, m_conv_b, m_dt_bias, m_a_log, m_d_skip, m_ssd_norm_g, m_q_norm_g, m_w_q_up, m_kv_norm_g, m_w_kv_up, m_w_mix_out, m_ln1_g, m_ln1_b, m_w_mem_q, m_w_mem_k, m_w_mem_v, m_w_mem_o, m_ln2_g, m_ln2_b, m_w_up, m_w_down, m_ln3_g, m_ln3_b, v_ln_in_g, v_ln_in_b, v_w_in, v_conv_w, v_conv_b, v_dt_bias, v_a_log, v_d_skip, v_ssd_norm_g, v_q_norm_g, v_w_q_up, v_kv_norm_g, v_w_kv_up, v_w_mix_out, v_ln1_g, v_ln1_b, v_w_mem_q, v_w_mem_k, v_w_mem_v, v_w_mem_o, v_ln2_g, v_ln2_b, v_w_up, v_w_down, v_ln3_g, v_ln3_b):
    args = dict(locals())
    me, c = _mesh_pos()

    wp_a = jnp.concatenate([args[n].reshape(-1, PACK_COLS).astype(BF16) for n in PACK_A_ORDER], axis=0)
    flat = [args[n].reshape(-1).astype(BF16) for n in PACK_B_ORDER[:-1]]
    flat.append(lax.bitcast_convert_type(conv_w.reshape(-1), BF16).reshape(-1))
    used = sum(f.shape[0] for f in flat)
    flat.append(jnp.zeros((PACK_B_ROWS * PACK_COLS - used,), BF16))
    wp_b = jnp.concatenate(flat).reshape(PACK_B_ROWS, PACK_COLS)

    wp_c = w_in[0].astype(BF16)

    g_c, g_b = _run_step(_merge_steps([_gather_step(wp_c), _gather_step(wp_b)]), "gather_b")
    g_c = lax.dynamic_update_slice(g_c, wp_c[None], (me, 0, 0))
    g_b = lax.dynamic_update_slice(g_b, wp_b[None], (me, 0, 0)).reshape(N_SHARD, -1)
    WB, off = {"w_in": g_c}, 0
    for n in PACK_B_ORDER:
        sr, sc = _shard_shape(n)
        cnt = sr * sc
        if n == "conv_w":
            part = lax.bitcast_convert_type(g_b[:, off:off + 2 * cnt].reshape(N_SHARD, cnt, 2), F32)
            off += 2 * cnt
        else:
            part = g_b[:, off:off + cnt]
            off += cnt
        WB[n] = _join_shards(n, part)
    P = {n: args[n] for n in SMALL_ORDER}
    P["conv_w"] = WB.pop("conv_w")

    loss, grad_x, (red_a, ((gp_c, theirs_c, pb_c), (gp_b, theirs_b, pb_b))), _, gsmall = _local_step(
        x[0], mem[0], positions[0], loss_target[0], WB, P, wp_a=wp_a)
    loss = lax.psum(loss, ("x", "y", "c"))

    gs = jnp.concatenate([_row(gsmall[n], PACK_COLS) for n in SMALL_ORDER] + [jnp.zeros((1, PACK_COLS), F32)], axis=0)
    gs, got_c, got_b = _small_all_reduce(gs, _merge_steps([_chip_exchange_step(pb_c), _chip_exchange_step(pb_b)]))
    red_c, red_b = _run_step(_merge_steps([_pair_fill_step(_chip_sum(gp_c, theirs_c, got_c, "chip_sum_w_in")),
                                           _pair_fill_step(_chip_sum(gp_b, theirs_b, got_b, "chip_sum_b"))]),
                             "pair_fill_b")

    grads, deltas, new_m, new_v = {}, {}, {}, {}
    for n in PACK_A_ORDER:
        r0, (sr, _) = PACK_A_ROW[n], _shard_shape(n)
        grads[n] = red_a[r0:r0 + sr].reshape(args[n].shape)
        deltas[n], new_m[n], new_v[n] = _adam(args[n], (red_a, r0), args["m_" + n], args["v_" + n], "adam_" + n)
    grads["w_in"] = red_c.reshape(w_in.shape)
    deltas["w_in"], new_m["w_in"], new_v["w_in"] = _adam(w_in, grads["w_in"], m_w_in, v_w_in, "adam_w_in")
    red_b = red_b.reshape(-1)
    off = 0
    for n in PACK_B_ORDER:
        sr, sc = _shard_shape(n)
        grads[n] = red_b[off:off + sr * sc].reshape(args[n].shape)
        off += sr * sc
        deltas[n], new_m[n], new_v[n] = _adam(args[n], grads[n], args["m_" + n], args["v_" + n], "adam_" + n)
    pack = lambda pre: jnp.concatenate([_row(args[pre + n], PACK_COLS) for n in SMALL_ORDER]
                                       + [jnp.zeros((1, PACK_COLS), F32)], axis=0)
    ds, ms, vs = _rowwise(_fn_adam, [pack(""), gs, pack("m_"), pack("v_")], [], [PACK_COLS] * 3, tr=16,
                          name="adam_small")
    for i, n in enumerate(SMALL_ORDER):
        cnt = args[n].size
        take = lambda t: t[i, :cnt].reshape(args[n].shape)
        grads[n], deltas[n], new_m[n], new_v[n] = take(gs), take(ds), take(ms), take(vs)

    order = ["ln_in_g", "ln_in_b", "w_in", "conv_w", "conv_b", "dt_bias", "a_log", "d_skip", "ssd_norm_g",
             "q_norm_g", "w_q_up", "kv_norm_g", "w_kv_up", "w_mix_out", "ln1_g", "ln1_b", "w_mem_q", "w_mem_k",
             "w_mem_v", "w_mem_o", "ln2_g", "ln2_b", "w_up", "w_down", "ln3_g", "ln3_b"]
    return (loss, grad_x[None], *[grads[n] for n in order], *[deltas[n] for n in order],
            *[new_m[n] for n in order], *[new_v[n] for n in order])
```

```python
import functools
import math

import jax
import jax.numpy as jnp
import numpy as np
from jax import lax
from jax.experimental import pallas as pl
from jax.experimental.pallas import tpu as pltpu

F32 = jnp.float32
BF16 = jnp.bfloat16
MESH = pl.DeviceIdType.MESH

D_MODEL = 1024
SSD_HEADS = 8
SSD_INNER = 512
SSD_CHUNK = 128
SSD_STATE = 128
MLA_HEADS = 8
MLA_NOPE = 64
MLA_ROPE = 32
MLA_QK = 96
MLA_Q_RANK = 384
MLA_KV_RANK = 256
ROPE_THETA = 10000.0
MEM_HEADS = 4
MEM_HEAD_DIM = 256
LN_EPS = 1e-5
RMS_EPS = 1e-6
ALPHA = 2.0 ** 0.25
ADAM_LR = 0.001
ADAM_B1 = 0.9
ADAM_B2 = 0.999
ADAM_EPS = 1e-08
ADAM_WD = 0.01
ADAM_STEP = 10

LANES = 128
IN_W = 2560
SEG_XBC = (0, 1024)
SEG_Z = (1024, 512)
SEG_QLAT = (1536, 384)
SEG_DT = (1920, 128)
SEG_KVLAT = (2048, 256)
SEG_KR = (2304, 128)
VMEM_LIMIT = 56 * 1024 * 1024
ATTN_TILE = 512
ROW_TILE = 256
NEG = -1e30

NN = (((1,), (0,)), ((), ()))
NT = (((1,), (1,)), ((), ()))
TN = (((0,), (0,)), ((), ()))


def _dot(a, b, dims=NN):
    return lax.dot_general(a.astype(BF16), b.astype(BF16), dims, preferred_element_type=F32)


def _dot_exact(a, b):
    return lax.dot_general(a, b, NN, precision=lax.Precision.HIGHEST, preferred_element_type=F32)


def _pick(dim, pref):
    t = min(pref, dim)
    t -= t % LANES
    while t >= LANES:
        if dim % t == 0:
            return t
        t -= LANES
    return dim


def _params(sem):
    return pltpu.CompilerParams(dimension_semantics=sem, vmem_limit_bytes=VMEM_LIMIT)


def _pack_caps(wname):
    r, c, ax = BIG[wname]
    if ax == 0:
        return (r if r <= 1024 else r // N_SHARD), c
    return r, c // N_SHARD


def _pack_block(wname, br, bc):
    r, c, ax = BIG[wname]
    r0 = PACK_A_ROW[wname]
    sr = r // N_SHARD if ax == 0 else r
    if ax == 0 and br > sr:
        assert br % sr == 0 and r0 % sr == 0
        return (br // sr, sr, bc), lambda rb, cb: (rb, r0 // sr, cb)
    assert r0 % br == 0
    if ax == 0:
        per = sr // br
        return (1, br, bc), lambda rb, cb: (rb // per, r0 // br + rb % per, cb)
    per = (c // N_SHARD) // bc
    return (1, br, bc), lambda rb, cb: (cb // per, r0 // br + rb, cb % per)


def _mm(a, b, *, form, name, a_pro=None, epi=None, out_dtype=F32, tm=1024, tn=1024, tk=1024, b_pack=None,
        b_rows=None, out_pack=None, hosted=None):
    b_shape = BIG[b_pack][:2] if b_pack else b.shape
    if b_pack and form == "nt":
        b_shape = (b_rows or b_shape[0], b_shape[1])
    if form == "nn":
        (m, k), (_, n) = a.shape, b_shape
    elif form == "nt":
        (m, k), (n, _) = a.shape, b_shape
    else:
        (k, m), (_, n) = a.shape, b_shape
    if b_pack:
        rcap, ccap = _pack_caps(b_pack)
        tk, tn = (min(tk, rcap), min(tn, ccap)) if form == "nn" else (min(tk, ccap), min(tn, rcap))
    if out_pack:
        rcap, ccap = _pack_caps(out_pack[0])
        tm, tn = min(tm, rcap), min(tn, ccap)
    tm, tn, tk = _pick(m, tm), _pick(n, tn), _pick(k, tk)
    dims = {"nn": NN, "nt": NT, "tn": TN}[form]
    nk = k // tk
    direct = out_dtype == F32 and epi is None
    n_extra = (1 if epi else 0) + (1 if out_pack else 0)

    def body(a_ref, b_ref, *rest):
        o_ref = rest[n_extra]
        acc_ref = o_ref if direct else rest[-1]

        @pl.when(pl.program_id(2) == 0)
        def _():
            acc_ref[...] = jnp.zeros_like(acc_ref)

        av = a_ref[...]
        if a_pro is not None:
            av = a_pro(av)
        bv = b_ref[...]
        acc_ref[...] += _dot(av, bv.reshape(-1, bv.shape[-1]), dims).reshape(acc_ref.shape)
        if not direct:
            @pl.when(pl.program_id(2) == nk - 1)
            def _():
                val = acc_ref[...]
                if epi is not None:
                    val = epi[0](val, rest[0][...])
                o_ref[...] = val.reshape(o_ref.shape).astype(o_ref.dtype)

    if form == "tn":
        a_spec = pl.BlockSpec((tk, tm), lambda i, j, kk: (kk, i))
    else:
        a_spec = pl.BlockSpec((tm, tk), lambda i, j, kk: (i, kk))
    if b_pack:
        shape, idx = _pack_block(b_pack, *((tk, tn) if form == "nn" else (tn, tk)))
        b_spec = pl.BlockSpec(shape, (lambda i, j, kk: idx(kk, j)) if form == "nn" else (lambda i, j, kk: idx(j, kk)))
    elif form == "nt":
        b_spec = pl.BlockSpec((tn, tk), lambda i, j, kk: (j, kk))
    else:
        b_spec = pl.BlockSpec((tk, tn), lambda i, j, kk: (kk, j))
    in_specs, args = [a_spec, b_spec], [a, b]
    out_spec = pl.BlockSpec((tm, tn), lambda i, j, kk: (i, j))
    out_sds, aliases = jax.ShapeDtypeStruct((m, n), out_dtype), {}
    if epi is not None:
        in_specs.append(out_spec)
        args.append(epi[1])
    if out_pack:
        wname, buf = out_pack
        shape, idx = _pack_block(wname, tm, tn)
        out_spec = pl.BlockSpec(shape, lambda i, j, kk: idx(i, j))
        out_sds, aliases = jax.ShapeDtypeStruct(buf.shape, buf.dtype), {len(args): 0}
        in_specs.append(HBM)
        args.append(buf)
    acc_shape = out_spec.block_shape if out_pack else (tm, tn)
    res = _call_with_step(
        body, hosted, None, args, name=name, grid=(m // tm, n // tn, nk), in_specs=in_specs, out_specs=[out_spec],
        out_shape=[out_sds], sem=("parallel", "parallel", "arbitrary"), aliases=aliases,
        scratch_shapes=[] if direct else [pltpu.VMEM(acc_shape, F32)])
    return res[0] if hosted is None else res


class _Ctx:
    def __init__(self, i, n):
        self.i, self.n = i, n


def _rowwise(fn, rows, consts, row_outs, acc_outs=(), *, tr, name, n_rows=None, hosted=None):
    norm = []
    for r in rows:
        kind = "tile"
        if isinstance(r, tuple) and isinstance(r[0], str):
            kind, r = r[0], r[1:]
        row0 = 0
        if isinstance(r, tuple) and len(r) == 4:
            r, row0 = r[:3], r[3]
        arr, col0, width = r if isinstance(r, tuple) else (r, 0, r.shape[1])
        assert col0 % width == 0
        norm.append((kind, arr, col0 // width, width, row0))
    n_rows = n_rows or next(a.shape[0] for k, a, _, _, _ in norm if k == "tile")
    tr = min(tr, n_rows)
    while n_rows % tr:
        tr -= 8
    n = n_rows // tr
    arrs, specs = [], []
    for kind, arr, cb, width, row0 in norm:
        if kind == "tile":
            assert row0 % tr == 0
            specs.append(pl.BlockSpec((tr, width), lambda i, cb=cb, rb=row0 // tr: (i + rb, cb)))
        elif kind == "prev":
            specs.append(pl.BlockSpec((8, width), lambda i, cb=cb: (jnp.maximum(i * (tr // 8) - 1, 0), cb)))
        else:
            specs.append(pl.BlockSpec((8, width), lambda i, cb=cb: (jnp.minimum((i + 1) * (tr // 8), n_rows // 8 - 1), cb)))
        arrs.append(arr)
    for c in consts:
        specs.append(pl.BlockSpec(c.shape, lambda i, nd=c.ndim: (0,) * nd))
        arrs.append(c)
    n_in, n_ro = len(arrs), len(row_outs)
    row_outs = [w if isinstance(w, tuple) else (w, F32) for w in row_outs]
    out_shape = [jax.ShapeDtypeStruct((n_rows, w), dt) for w, dt in row_outs]
    out_specs = [pl.BlockSpec((tr, w), lambda i: (i, 0)) for w, _ in row_outs]
    out_shape += [jax.ShapeDtypeStruct(s, F32) for s in acc_outs]
    out_specs += [pl.BlockSpec(s, lambda i: (0, 0)) for s in acc_outs]

    def body(*refs):
        i = pl.program_id(0)
        vals = [r[...] for r in refs[:n_in]]
        outs = fn(_Ctx(i, n), *vals)
        if not isinstance(outs, (tuple, list)):
            outs = (outs,)
        o_refs = refs[n_in:]
        for o_ref, o in zip(o_refs[:n_ro], outs[:n_ro]):
            o_ref[...] = o.astype(o_ref.dtype)
        if acc_outs:
            @pl.when(i == 0)
            def _():
                for o_ref in o_refs[n_ro:]:
                    o_ref[...] = jnp.zeros_like(o_ref)

            for o_ref, o in zip(o_refs[n_ro:], outs[n_ro:]):
                o_ref[...] += jnp.broadcast_to(o, o_ref.shape)

    return _call_with_step(body, hosted, None, arrs, name=name, grid=(n,), in_specs=specs, out_specs=out_specs,
                           out_shape=out_shape, sem=("arbitrary",))


def _sum0(v):
    return jnp.sum(v, axis=0, keepdims=True)


def _mean1(v):
    return jnp.mean(v, axis=-1, keepdims=True)


def _sigmoid(v):
    return 1.0 / (1.0 + jnp.exp(-v))


def _ln_stats(t):
    xc = t - _mean1(t)
    rstd = lax.rsqrt(_mean1(xc * xc) + LN_EPS)
    return xc * rstd, rstd


def _ln_bwd(xhat, rstd, dy, g):
    dxh = dy * g
    dx = rstd * (dxh - _mean1(dxh) - xhat * _mean1(dxh * xhat))
    return dx, _sum0(dy * xhat), _sum0(dy)


def _rms_fwd(v, g):
    return v * lax.rsqrt(_mean1(v * v) + RMS_EPS) * g


def _rms_bwd(v, dy, g):
    rs = lax.rsqrt(_mean1(v * v) + RMS_EPS)
    vh = v * rs
    dyg = dy * g
    return rs * (dyg - vh * _mean1(dyg * vh)), _sum0(dy * vh)


def _lane(shape):
    return lax.broadcasted_iota(jnp.int32, shape, len(shape) - 1)


def _shift_down(u, halo, s, is_first):
    tr = u.shape[0]
    rolled = pltpu.roll(u, s, 0)
    hr = jnp.where(is_first, 0.0, pltpu.roll(halo, s, 0))
    row = lax.broadcasted_iota(jnp.int32, hr.shape, 0)
    top = jnp.where(row < s, hr, rolled[0:8])
    if tr == 8:
        return top
    return jnp.concatenate([top, rolled[8:]], axis=0)


def _shift_up(d, halo, s, is_last):
    tr = d.shape[0]
    rolled = pltpu.roll(d, tr - s, 0)
    hr = jnp.where(is_last, 0.0, pltpu.roll(halo, 8 - s, 0))
    row = lax.broadcasted_iota(jnp.int32, hr.shape, 0)
    bot = jnp.where(row >= 8 - s, hr, rolled[tr - 8:])
    if tr == 8:
        return bot
    return jnp.concatenate([rolled[:tr - 8], bot], axis=0)


def _rope(v, ta, tb, tc):
    return v * ta + pltpu.roll(v, 16, 1) * tb + pltpu.roll(v, LANES - 16, 1) * tc


def _rope_bwd(d, ta, tb, tc):
    return d * ta + pltpu.roll(d * tb, LANES - 16, 1) + pltpu.roll(d * tc, 16, 1)


def _ssd_common(dtv, a_row):
    L = SSD_CHUNK
    a = dtv * a_row
    r = lax.broadcasted_iota(jnp.int32, (L, L), 0)
    c = lax.broadcasted_iota(jnp.int32, (L, L), 1)
    tril = r >= c
    cs = _dot_exact(tril.astype(F32), a)
    cs_t = cs.T
    cs_last = cs[L - 1:L, :]
    return dict(a=a, tril=tril, cs=cs, cs_t=cs_t, ecs=jnp.exp(cs), dte=jnp.exp(cs_last - cs),
                elast=jnp.exp(cs_last))


def _pair_sel(v, h0, lo):
    return jnp.where(lo, v[:, h0:h0 + 1], v[:, h0 + 1:h0 + 2])


def _ssd_pair(cm, h0, cb, xp, dtv, bmat, cmat, hp, lo):
    L = SSD_CHUNK
    x = xp * _pair_sel(dtv, h0, lo)
    lam0 = jnp.exp(jnp.where(cm["tril"], cm["cs"][:, h0:h0 + 1] - cm["cs_t"][h0:h0 + 1, :], NEG))
    lam1 = jnp.exp(jnp.where(cm["tril"], cm["cs"][:, h0 + 1:h0 + 2] - cm["cs_t"][h0 + 1:h0 + 2, :], NEG))
    m0, m1 = cb * lam0, cb * lam1
    ydiag = jnp.where(lo, _dot(m0, x), _dot(m1, x))
    ecs_p = _pair_sel(cm["ecs"], h0, lo)
    dte_p = _pair_sel(cm["dte"], h0, lo)
    yoff = _dot(cmat, hp, NT) * ecs_p
    xd = x * dte_p
    st = _dot(xd, bmat, TN)
    rlo = lax.broadcasted_iota(jnp.int32, (LANES, SSD_STATE), 0) < 64
    decay = jnp.where(rlo, cm["elast"][:, h0:h0 + 1], cm["elast"][:, h0 + 1:h0 + 2])
    h_next = hp * decay + st
    return dict(x=x, lam0=lam0, lam1=lam1, m0=m0, m1=m1, y=ydiag + yoff, yoff=yoff, ecs_p=ecs_p, dte_p=dte_p,
                xd=xd, decay=decay, h_next=h_next)


def _ssd_fwd(xbc, dt, a_row, *, name):
    S = xbc.shape[0]
    L = SSD_CHUNK
    nc = S // L

    def body(xs_ref, bm_ref, cm_ref, dt_ref, a_ref, y_ref, hs_ref, h_scr):
        @pl.when(pl.program_id(0) == 0)
        def _():
            h_scr[...] = jnp.zeros_like(h_scr)

        dtv = dt_ref[...]
        cm = _ssd_common(dtv, a_ref[...])
        lo = _lane((L, LANES)) < 64
        ys = []
        for g in range(2):
            bmat = bm_ref[:, g * 128:(g + 1) * 128]
            cmat = cm_ref[:, g * 128:(g + 1) * 128]
            cb = _dot(cmat, bmat, NT)
            for pr in range(2):
                p4 = 2 * g + pr
                hp = h_scr[p4]
                hs_ref[0, p4 * 128:(p4 + 1) * 128, :] = hp
                t = _ssd_pair(cm, 2 * p4, cb, xs_ref[:, p4 * 128:(p4 + 1) * 128], dtv, bmat, cmat, hp, lo)
                ys.append(t["y"])
                h_scr[p4] = t["h_next"]
        y_ref[...] = jnp.concatenate(ys, axis=1)

    return pl.pallas_call(
        body, name=name, grid=(nc,),
        in_specs=[pl.BlockSpec((L, 512), lambda c: (c, 0)), pl.BlockSpec((L, 256), lambda c: (c, 2)),
                  pl.BlockSpec((L, 256), lambda c: (c, 3)), pl.BlockSpec((L, 128), lambda c: (c, 0)),
                  pl.BlockSpec((1, 128), lambda c: (0, 0))],
        out_specs=[pl.BlockSpec((L, 512), lambda c: (c, 0)), pl.BlockSpec((1, 512, 128), lambda c: (c, 0, 0))],
        out_shape=[jax.ShapeDtypeStruct((S, 512), F32), jax.ShapeDtypeStruct((nc, 512, 128), F32)],
        scratch_shapes=[pltpu.VMEM((4, 128, 128), F32)],
        compiler_params=_params(("arbitrary",)),
    )(xbc, xbc, xbc, dt, a_row)


def _ssd_bwd(xbc, dt, a_row, hs, dy, *, name):
    S = xbc.shape[0]
    L = SSD_CHUNK
    nc = S // L

    def body(xs_ref, bm_ref, cm_ref, dt_ref, a_ref, hs_ref, dy_ref, dxs_ref, dbc_ref, ddt_ref, da_ref, g_scr):
        @pl.when(pl.program_id(0) == 0)
        def _():
            g_scr[...] = jnp.zeros_like(g_scr)
            da_ref[...] = jnp.zeros_like(da_ref)

        dtv = dt_ref[...]
        a_row_v = a_ref[...]
        cm = _ssd_common(dtv, a_row_v)
        lo = _lane((L, LANES)) < 64
        lane_row = _lane((1, LANES))
        ri = lax.broadcasted_iota(jnp.int32, (L, L), 0)
        ci = lax.broadcasted_iota(jnp.int32, (L, L), 1)
        triu = (ri <= ci).astype(F32)
        stril = ri > ci

        def halves(v, mask):
            return (jnp.sum(jnp.where(mask, v, 0.0), axis=1, keepdims=True),
                    jnp.sum(jnp.where(mask, 0.0, v), axis=1, keepdims=True))

        i_all = jnp.zeros((L, LANES), F32)
        yo_all = jnp.zeros((L, LANES), F32)
        w_all = jnp.zeros((L, LANES), F32)
        ddt_x = jnp.zeros((L, LANES), F32)
        e_row = jnp.zeros((1, LANES), F32)
        rlo = lax.broadcasted_iota(jnp.int32, (LANES, SSD_STATE), 0) < 64
        dxs, dbs, dcs = [], [], []
        for g in range(2):
            bmat = bm_ref[:, g * 128:(g + 1) * 128]
            cmat = cm_ref[:, g * 128:(g + 1) * 128]
            cb = _dot(cmat, bmat, NT)
            dcb = jnp.zeros((L, L), F32)
            db = jnp.zeros((L, SSD_STATE), F32)
            dc = jnp.zeros((L, SSD_STATE), F32)
            for pr in range(2):
                p4 = 2 * g + pr
                h0 = 2 * p4
                hp = hs_ref[0, p4 * 128:(p4 + 1) * 128, :]
                xp = xs_ref[:, p4 * 128:(p4 + 1) * 128]
                t = _ssd_pair(cm, h0, cb, xp, dtv, bmat, cmat, hp, lo)
                gst = g_scr[p4]
                dyp = dy_ref[:, p4 * 128:(p4 + 1) * 128]
                dy0 = jnp.where(lo, dyp, 0.0)
                dy1 = dyp - dy0
                bg = _dot(bmat, gst, NT)
                dx = _dot(t["m0"], dy0, TN) + _dot(t["m1"], dy1, TN) + bg * t["dte_p"]
                dm0, dm1 = _dot(dy0, t["x"], NT), _dot(dy1, t["x"], NT)
                dcb = dcb + dm0 * t["lam0"] + dm1 * t["lam1"]
                dye = dyp * t["ecs_p"]
                dc = dc + _dot(dye, hp)
                db = db + _dot(t["xd"], gst)
                i0 = jnp.sum(jnp.where(stril, _dot(triu, dm0 * t["m0"]), 0.0), axis=1, keepdims=True)
                i1 = jnp.sum(jnp.where(stril, _dot(triu, dm1 * t["m1"]), 0.0), axis=1, keepdims=True)
                yo0, yo1 = halves(dyp * t["yoff"], lo)
                w0, w1 = halves(t["xd"] * bg, lo)
                gh = gst * (hp * t["decay"])
                e0 = _sum0(jnp.sum(jnp.where(rlo, gh, 0.0), axis=1, keepdims=True))
                e1 = _sum0(jnp.sum(jnp.where(rlo, 0.0, gh), axis=1, keepdims=True))
                x0, x1 = halves(dx * xp, lo)
                oh0 = (lane_row == h0).astype(F32)
                oh1 = (lane_row == h0 + 1).astype(F32)
                i_all = i_all + i0 * oh0 + i1 * oh1
                yo_all = yo_all + yo0 * oh0 + yo1 * oh1
                w_all = w_all + w0 * oh0 + w1 * oh1
                e_row = e_row + e0 * oh0 + e1 * oh1
                ddt_x = ddt_x + x0 * oh0 + x1 * oh1
                dxs.append(dx * _pair_sel(dtv, h0, lo))
                g_scr[p4] = gst * t["decay"] + _dot(dye, cmat, TN)
            dbs.append(db + _dot(dcb, cmat, TN))
            dcs.append(dc + _dot(dcb, bmat))
        da = i_all + _dot_exact(triu, yo_all) + _dot_exact(stril.astype(F32), w_all) + e_row
        ddt_ref[...] = da * a_row_v + ddt_x
        da_ref[...] += _sum0(da * dtv)
        dxs_ref[...] = jnp.concatenate(dxs, axis=1)
        dbc_ref[...] = jnp.concatenate(dbs + dcs, axis=1)

    rev = lambda c: nc - 1 - c
    return pl.pallas_call(
        body, name=name, grid=(nc,),
        in_specs=[pl.BlockSpec((L, 512), lambda c: (rev(c), 0)), pl.BlockSpec((L, 256), lambda c: (rev(c), 2)),
                  pl.BlockSpec((L, 256), lambda c: (rev(c), 3)), pl.BlockSpec((L, 128), lambda c: (rev(c), 0)),
                  pl.BlockSpec((1, 128), lambda c: (0, 0)), pl.BlockSpec((1, 512, 128), lambda c: (rev(c), 0, 0)),
                  pl.BlockSpec((L, 512), lambda c: (rev(c), 0))],
        out_specs=[pl.BlockSpec((L, 512), lambda c: (rev(c), 0)), pl.BlockSpec((L, 512), lambda c: (rev(c), 0)),
                   pl.BlockSpec((L, 128), lambda c: (rev(c), 0)), pl.BlockSpec((1, 128), lambda c: (0, 0))],
        out_shape=[jax.ShapeDtypeStruct((S, 512), F32), jax.ShapeDtypeStruct((S, 512), F32),
                   jax.ShapeDtypeStruct((S, 128), F32), jax.ShapeDtypeStruct((1, 128), F32)],
        scratch_shapes=[pltpu.VMEM((4, 128, 128), F32)],
        compiler_params=_params(("arbitrary",)),
    )(xbc, xbc, xbc, dt, a_row, hs, dy)


MLA_SCALE = MLA_QK ** -0.5


def _causal_scores(q, k, qi, ki, t):
    s = _dot(q, k, NT) * MLA_SCALE
    row = qi * t + lax.broadcasted_iota(jnp.int32, (t, t), 0)
    col = ki * t + lax.broadcasted_iota(jnp.int32, (t, t), 1)
    return jnp.where(col <= row, s, NEG)


def _mla_fwd(q, k, kv, *, name):
    S = q.shape[0]
    t = min(ATTN_TILE, S)
    nq = S // t

    def body(q_ref, k_ref, v_ref, o_ref, lse_ref, m_scr, l_scr, acc_scr):
        qi, ki = pl.program_id(1), pl.program_id(2)

        @pl.when(ki == 0)
        def _():
            m_scr[...] = jnp.full_like(m_scr, NEG)
            l_scr[...] = jnp.zeros_like(l_scr)
            acc_scr[...] = jnp.zeros_like(acc_scr)

        @pl.when(ki <= qi)
        def _():
            s = _causal_scores(q_ref[...], k_ref[...], qi, ki, t)
            m_old = m_scr[:, 0:1]
            m_new = jnp.maximum(m_old, jnp.max(s, axis=1, keepdims=True))
            p = jnp.exp(s - m_new)
            corr = jnp.exp(m_old - m_new)
            l_scr[...] = jnp.broadcast_to(corr * l_scr[:, 0:1] + jnp.sum(p, axis=1, keepdims=True), l_scr.shape)
            acc_scr[...] = corr * acc_scr[...] + _dot(p, v_ref[...])
            m_scr[...] = jnp.broadcast_to(m_new, m_scr.shape)

        @pl.when(ki == nq - 1)
        def _():
            l = l_scr[:, 0:1]
            o_ref[...] = acc_scr[...] / l
            lse_ref[0] = jnp.broadcast_to(m_scr[:, 0:1] + jnp.log(l), (t, LANES))

    return pl.pallas_call(
        body, name=name, grid=(MLA_HEADS, nq, nq),
        in_specs=[pl.BlockSpec((t, 128), lambda h, qi, ki: (qi, h)),
                  pl.BlockSpec((t, 128), lambda h, qi, ki: (jnp.minimum(ki, qi), h)),
                  pl.BlockSpec((t, 128), lambda h, qi, ki: (jnp.minimum(ki, qi), 2 * h + 1))],
        out_specs=[pl.BlockSpec((t, 128), lambda h, qi, ki: (qi, h)),
                   pl.BlockSpec((1, t, 128), lambda h, qi, ki: (h, qi, 0))],
        out_shape=[jax.ShapeDtypeStruct((S, MLA_HEADS * 128), F32), jax.ShapeDtypeStruct((MLA_HEADS, S, 128), F32)],
        scratch_shapes=[pltpu.VMEM((t, 128), F32), pltpu.VMEM((t, 128), F32), pltpu.VMEM((t, 128), F32)],
        compiler_params=_params(("parallel", "parallel", "arbitrary")),
    )(q, k, kv)


def _mla_bwd_dkv(q, k, kv, o, do, lse, *, name):
    S = q.shape[0]
    t = min(ATTN_TILE, S)
    nq = S // t

    def body(q_ref, k_ref, v_ref, o_ref, do_ref, lse_ref, dkv_ref):
        ki, qi = pl.program_id(1), pl.program_id(2)

        @pl.when(qi == 0)
        def _():
            dkv_ref[...] = jnp.zeros_like(dkv_ref)

        @pl.when(qi >= ki)
        def _():
            qv, dov = q_ref[...], do_ref[...]
            s = _causal_scores(qv, k_ref[...], qi, ki, t)
            p = jnp.exp(s - lse_ref[0][:, 0:1])
            dv = _dot(p, dov, TN)
            dp = _dot(dov, v_ref[...], NT)
            delta = jnp.sum(dov * o_ref[...], axis=1, keepdims=True)
            ds = p * (dp - delta) * MLA_SCALE
            dkv_ref[...] += jnp.concatenate([_dot(ds, qv, TN), dv], axis=1)

    qmap = lambda h, ki, qi: (jnp.maximum(qi, ki), h)
    return pl.pallas_call(
        body, name=name, grid=(MLA_HEADS, nq, nq),
        in_specs=[pl.BlockSpec((t, 128), qmap),
                  pl.BlockSpec((t, 128), lambda h, ki, qi: (ki, h)),
                  pl.BlockSpec((t, 128), lambda h, ki, qi: (ki, 2 * h + 1)),
                  pl.BlockSpec((t, 128), qmap), pl.BlockSpec((t, 128), qmap),
                  pl.BlockSpec((1, t, 128), lambda h, ki, qi: (h, jnp.maximum(qi, ki), 0))],
        out_specs=pl.BlockSpec((t, 256), lambda h, ki, qi: (ki, h)),
        out_shape=jax.ShapeDtypeStruct((S, MLA_HEADS * 256), F32),
        compiler_params=_params(("parallel", "parallel", "arbitrary")),
    )(q, k, kv, o, do, lse)


def _mla_bwd_dq(q, k, kv, o, do, lse, *, name):
    S = q.shape[0]
    t = min(ATTN_TILE, S)
    nq = S // t

    def body(q_ref, k_ref, v_ref, o_ref, do_ref, lse_ref, dq_ref):
        qi, ki = pl.program_id(1), pl.program_id(2)

        @pl.when(ki == 0)
        def _():
            dq_ref[...] = jnp.zeros_like(dq_ref)

        @pl.when(ki <= qi)
        def _():
            dov, kv = do_ref[...], k_ref[...]
            s = _causal_scores(q_ref[...], kv, qi, ki, t)
            p = jnp.exp(s - lse_ref[0][:, 0:1])
            dp = _dot(dov, v_ref[...], NT)
            delta = jnp.sum(dov * o_ref[...], axis=1, keepdims=True)
            ds = p * (dp - delta) * MLA_SCALE
            dq_ref[...] += _dot(ds, kv)

    qmap = lambda h, qi, ki: (qi, h)
    return pl.pallas_call(
        body, name=name, grid=(MLA_HEADS, nq, nq),
        in_specs=[pl.BlockSpec((t, 128), qmap),
                  pl.BlockSpec((t, 128), lambda h, qi, ki: (jnp.minimum(ki, qi), h)),
                  pl.BlockSpec((t, 128), lambda h, qi, ki: (jnp.minimum(ki, qi), 2 * h + 1)),
                  pl.BlockSpec((t, 128), qmap), pl.BlockSpec((t, 128), qmap),
                  pl.BlockSpec((1, t, 128), lambda h, qi, ki: (h, qi, 0))],
        out_specs=pl.BlockSpec((t, 128), qmap),
        out_shape=jax.ShapeDtypeStruct((S, MLA_HEADS * 128), F32),
        compiler_params=_params(("parallel", "parallel", "arbitrary")),
    )(q, k, kv, o, do, lse)


HBM = pl.BlockSpec(memory_space=pl.ANY)


class _Step:
    def __init__(self, inputs, out_shapes, n_sems, start, finish, mid=None):
        self.inputs, self.out_shapes, self.n_sems = inputs, out_shapes, n_sems
        self.start, self.finish, self.mid = start, finish, mid
        self.alias = []


class _Shifted:
    def __init__(self, ref, off):
        self.ref, self.off = ref, off

    @property
    def at(self):
        return self

    def __getitem__(self, j):
        return self.ref.at[self.off + j]


def _merge_steps(steps):
    offs = [sum(s.n_sems for s in steps[:i]) for i in range(len(steps) + 1)]
    i_offs = [sum(len(s.inputs) for s in steps[:i]) for i in range(len(steps))]
    o_offs = [sum(len(s.out_shapes) for s in steps[:i]) for i in range(len(steps))]

    def phase(which):
        def run(ins, outs, sems):
            for s, off, i0, o0 in zip(steps, offs, i_offs, o_offs):
                fn = getattr(s, which)
                if fn is not None:
                    fn(ins[i0:i0 + len(s.inputs)], outs[o0:o0 + len(s.out_shapes)],
                       [_Shifted(sems[0], off), _Shifted(sems[1], off)])
        return run

    merged = _Step([a for s in steps for a in s.inputs], [o for s in steps for o in s.out_shapes], offs[-1],
                   phase("start"), phase("finish"), phase("mid") if any(s.mid for s in steps) else None)
    merged.alias = [(i0 + a, o0 + b) for s, i0, o0 in zip(steps, i_offs, o_offs) for a, b in s.alias]
    return merged


def _place():
    x, y, c = lax.axis_index("x"), lax.axis_index("y"), lax.axis_index("c")
    chips = [(1 - x, y), (x, 1 - y), (1 - x, 1 - y)]
    return x, y, c, chips


def _chunks(rows, tile):
    return next(n for n in (4, 3, 2, 1) if rows % (n * tile) == 0)


def _remote(src, dst, sems, j, to):
    return pltpu.make_async_remote_copy(src_ref=src, dst_ref=dst, send_sem=sems[0].at[j], recv_sem=sems[1].at[j],
                                        device_id=to, device_id_type=MESH)


def _gather_step(wp):
    R, C = wp.shape
    H = R // 2
    nq = _chunks(H, 16)
    CH = H // nq

    def copies(ins, outs, sems):
        x, y, c, chips = _place()
        sib, me = (x, y, 1 - c), 2 * x + y
        w_ref, out_ref = ins[0], outs[0]

        def piece(k, hc, q):
            return out_ref.at[k, pl.ds(hc * H + q * CH, CH), :]

        sends, landed, fwds, fwd_landed = [], [], [], []
        for q in range(nq):
            for j, (px, py) in enumerate(chips):
                k = 2 * px + py
                sends.append(_remote(w_ref.at[pl.ds(c * H + q * CH, CH), :], piece(me, c, q), sems, j * nq + q,
                                     (px, py, c)))
                landed.append(_remote(piece(k, c, q), piece(k, c, q), sems, j * nq + q, (px, py, c)))
                fwds.append(_remote(piece(k, c, q), piece(k, c, q), sems, (3 + j) * nq + q, sib))
                fwd_landed.append(_remote(piece(k, 1 - c, q), piece(k, 1 - c, q), sems, (3 + j) * nq + q, sib))
        return sends, landed, fwds, fwd_landed

    def start(ins, outs, sems):
        for cp in copies(ins, outs, sems)[0]:
            cp.start()

    def mid(ins, outs, sems):
        _, landed, fwds, _ = copies(ins, outs, sems)
        for arrived, onward in zip(landed, fwds):
            arrived.wait_recv()
            onward.start()

    def finish(ins, outs, sems):
        sends, _, fwds, fwd_landed = copies(ins, outs, sems)
        for cp in fwd_landed:
            cp.wait_recv()
        for cp in sends + fwds:
            cp.wait_send()

    return _Step([wp], [jax.ShapeDtypeStruct((N_SHARD, R, C), wp.dtype)], 6 * nq, start, finish, mid)


def _pair_exchange_step(gp):
    n, R, C = gp.shape
    H = R // 2
    nq = _chunks(H, 8)
    CH = H // nq

    def copies(ins, outs, sems):
        x, y, c, _ = _place()
        return [_remote(ins[0].at[k, pl.ds((1 - c) * H + q * CH, CH), :], outs[0].at[k, pl.ds(q * CH, CH), :], sems,
                        k * nq + q, (x, y, 1 - c)) for k in range(n) for q in range(nq)]

    def start(ins, outs, sems):
        for cp in copies(ins, outs, sems):
            cp.start()

    def finish(ins, outs, sems):
        for cp in copies(ins, outs, sems):
            cp.wait()

    return _Step([gp], [jax.ShapeDtypeStruct((n, H, C), gp.dtype)], n * nq, start, finish)


def _chip_exchange_step(pb):
    n, H, C = pb.shape
    nq = _chunks(H, 16)
    CH = H // nq

    def copies(ins, outs, sems):
        x, y, c, chips = _place()
        return [_remote(ins[0].at[2 * px + py, pl.ds(q * CH, CH), :], outs[0].at[j, pl.ds(q * CH, CH), :], sems,
                        j * nq + q, (px, py, c)) for q in range(nq) for j, (px, py) in enumerate(chips)]

    def start(ins, outs, sems):
        for cp in copies(ins, outs, sems):
            cp.start()

    def finish(ins, outs, sems):
        for cp in copies(ins, outs, sems):
            cp.wait()

    return _Step([pb], [jax.ShapeDtypeStruct((3, H, C), pb.dtype)], 3 * nq, start, finish)


def _pair_join_step(q):
    H, C = q.shape
    nq = _chunks(H, 8)
    CH = H // nq

    def copies(ins, outs, sems):
        x, y, c, _ = _place()
        return [_remote(ins[0].at[pl.ds(j * CH, CH), :], outs[0].at[pl.ds(j * CH, CH), :], sems, j, (x, y, 1 - c))
                for j in range(nq)]

    def start(ins, outs, sems):
        for cp in copies(ins, outs, sems):
            cp.start()

    def finish(ins, outs, sems):
        for cp in copies(ins, outs, sems):
            cp.wait()

    return _Step([q], [jax.ShapeDtypeStruct((H, C), q.dtype)], nq, start, finish)


def _sem_scratch(step):
    return [pltpu.SemaphoreType.DMA((step.n_sems,)), pltpu.SemaphoreType.DMA((step.n_sems,))]


def _run_step(step, name):
    ni, no = len(step.inputs), len(step.out_shapes)

    def body(*refs):
        ins, outs, sems = refs[:ni], refs[ni:ni + no], refs[ni + no:]
        step.start(ins, outs, sems)
        if step.mid is not None:
            step.mid(ins, outs, sems)
        step.finish(ins, outs, sems)

    return pl.pallas_call(body, name=name, in_specs=[HBM] * ni, out_specs=[HBM] * no, out_shape=step.out_shapes,
                          input_output_aliases=dict(step.alias),
                          scratch_shapes=_sem_scratch(step))(*step.inputs)


def _grid_flags(grid):
    ids = [pl.program_id(d) for d in range(len(grid))]
    first = functools.reduce(lambda a, b: a & b, [i == 0 for i in ids])
    last = functools.reduce(lambda a, b: a & b, [i == n - 1 for i, n in zip(ids, grid)])
    return first, last, last


def _call_with_step(core, step, flags, args, *, name, grid, in_specs, out_specs, out_shape, sem, scratch_shapes=(),
                    aliases=None):
    aliases = aliases or {}
    if step is None:
        return pl.pallas_call(core, name=name, grid=grid, in_specs=in_specs, out_specs=out_specs,
                              out_shape=out_shape, scratch_shapes=list(scratch_shapes),
                              input_output_aliases=aliases, compiler_params=_params(sem))(*args)
    n_in, n_out, n_scr = len(in_specs), len(out_specs), len(scratch_shapes)
    si, so = len(step.inputs), len(step.out_shapes)
    flags = flags or (lambda: _grid_flags(grid))
    aliases = {**aliases, **{n_in + a: n_out + b for a, b in step.alias}}

    def body(*refs):
        ins, s_ins = refs[:n_in], refs[n_in:n_in + si]
        outs = refs[n_in + si:n_in + si + n_out]
        s_outs = refs[n_in + si + n_out:n_in + si + n_out + so]
        scr = refs[n_in + si + n_out + so:n_in + si + n_out + so + n_scr]
        sems = refs[n_in + si + n_out + so + n_scr:]
        first, middle, last = flags()

        @pl.when(first)
        def _():
            step.start(s_ins, s_outs, sems)

        if step.mid is not None:
            @pl.when(middle)
            def _():
                step.mid(s_ins, s_outs, sems)

        core(*ins, *outs, *scr)

        @pl.when(last)
        def _():
            step.finish(s_ins, s_outs, sems)

    return pl.pallas_call(
        body, name=name, grid=grid, in_specs=list(in_specs) + [HBM] * si, out_specs=list(out_specs) + [HBM] * so,
        out_shape=list(out_shape) + list(step.out_shapes), scratch_shapes=list(scratch_shapes) + _sem_scratch(step),
        input_output_aliases=aliases, compiler_params=_params(("arbitrary",) * len(grid)))(*args, *step.inputs)


def _attn_flags(nq):
    h, qi = pl.program_id(0), pl.program_id(1)
    return ((h == 0) & (qi == 0), (h == MLA_HEADS - 1) & (qi == 0), (h == MLA_HEADS - 1) & (qi == nq - 1))


ATTN_SPLIT = 1
ATTN_KEY_SPLIT = 1


def _att_mask(s_t, q0, k0):
    krow = k0 + lax.broadcasted_iota(jnp.int32, s_t.shape, 0)
    qcol = q0 + lax.broadcasted_iota(jnp.int32, s_t.shape, 1)
    return jnp.where(krow <= qcol, s_t, NEG)


def _loop2(lo, hi, step, carry):
    n = hi - lo

    def four(i, c):
        kb = lo + 4 * i
        return step(kb + 3, step(kb + 2, step(kb + 1, step(kb, c))))

    carry = lax.fori_loop(0, n // 4, four, carry)
    base = lo + 4 * (n // 4)
    carry = lax.cond(n % 4 >= 2, lambda c: step(base + 1, step(base, c)), lambda c: c, carry)
    return lax.cond(n % 2 == 1, lambda c: step(hi - 1, c), lambda c: c, carry)


def _rows(ref, blk, t):
    return ref[pl.ds(pl.multiple_of(blk * t, t), t), :]


def _cols(ref, blk, t):
    return ref[:, pl.ds(pl.multiple_of(blk * t, t), t)]


def _attn_fwd(q, k, v_t, *, name, hosted=None):
    S = q.shape[0]
    t = min(ATTN_TILE, S)
    nq = S // t

    def body(q_ref, k_ref, vt_ref, o_ref, lse_ref):
        qi = pl.program_id(1)
        w = t // ATTN_SPLIT
        qs = [q_ref[s * w:(s + 1) * w, :] for s in range(ATTN_SPLIT)]

        tk = t // ATTN_KEY_SPLIT

        def step(kb, carry, masked):
            kt, vt = _rows(k_ref, kb, tk), _cols(vt_ref, kb, tk)
            out = []
            for s, (m, l, acc) in enumerate(carry):
                s_t = lax.dot_general(kt, qs[s], NT, preferred_element_type=F32)
                if masked:
                    s_t = _att_mask(s_t, qi * t + s * w, kb * tk)
                m_new = jnp.maximum(m, jnp.max(s_t, axis=0, keepdims=True))
                p_t = jnp.exp(s_t - m_new)
                corr = jnp.exp(m - m_new)
                l = corr * l + jnp.sum(p_t, axis=0, keepdims=True)
                acc = corr * acc + lax.dot_general(vt, p_t.astype(BF16), NN, preferred_element_type=F32)
                out.append((m_new, l, acc))
            return tuple(out)

        init = tuple((jnp.full((1, w), NEG, F32), jnp.zeros((1, w), F32), jnp.zeros((LANES, w), F32))
                     for _ in range(ATTN_SPLIT))
        carry = _loop2(0, qi * ATTN_KEY_SPLIT, lambda kb, c: step(kb, c, False), init)
        for j in range(ATTN_KEY_SPLIT):
            carry = step(qi * ATTN_KEY_SPLIT + j, carry, True)
        for s, (m, l, acc) in enumerate(carry):
            o_ref[:, s * w:(s + 1) * w] = acc / l
            lse_ref[0, :, s * w:(s + 1) * w] = m + jnp.log(l)

    return _call_with_step(
        body, hosted, lambda: _attn_flags(nq), (q, k, v_t), name=name, grid=(MLA_HEADS, nq),
        in_specs=[pl.BlockSpec((t, LANES), lambda h, qi: (qi, h)),
                  pl.BlockSpec((S, LANES), lambda h, qi: (0, h)),
                  pl.BlockSpec((LANES, S), lambda h, qi: (h, 0))],
        out_specs=[pl.BlockSpec((LANES, t), lambda h, qi: (h, qi)),
                   pl.BlockSpec((1, 1, t), lambda h, qi: (h, 0, qi))],
        out_shape=[jax.ShapeDtypeStruct((MLA_HEADS * LANES, S), F32), jax.ShapeDtypeStruct((MLA_HEADS, 1, S), F32)],
        sem=("parallel", "arbitrary"))


def _attn_bwd_dq(q, k, v, o_t, do_t, lse, *, name, hosted=None):
    S = q.shape[0]
    t = min(ATTN_TILE, S)
    nq = S // t

    def body(q_ref, k_ref, v_ref, o_ref, do_ref, lse_ref, dq_ref, delta_ref):
        qi = pl.program_id(1)
        qv = q_ref[...]
        dov = do_ref[...]
        delta = jnp.sum(dov * o_ref[...], axis=0, keepdims=True)
        delta_ref[0] = delta
        dob = dov.astype(BF16)
        lse_v = lse_ref[0]

        def step(kb, acc, masked):
            s_t = lax.dot_general(_rows(k_ref, kb, t), qv, NT, preferred_element_type=F32)
            if masked:
                s_t = _att_mask(s_t, qi * t, kb * t)
            p_t = jnp.exp(s_t - lse_v)
            dp_t = lax.dot_general(_rows(v_ref, kb, t), dob, NN, preferred_element_type=F32)
            ds_t = (p_t * (dp_t - delta)).astype(BF16)
            return acc + lax.dot_general(_rows(k_ref, kb, t), ds_t, TN, preferred_element_type=F32)

        acc = _loop2(0, qi, lambda kb, c: step(kb, c, False), jnp.zeros((LANES, t), F32))
        dq_ref[...] = step(qi, acc, True).T

    tile = pl.BlockSpec((t, LANES), lambda h, qi: (qi, h))
    tile_t = pl.BlockSpec((LANES, t), lambda h, qi: (h, qi))
    stat = pl.BlockSpec((1, 1, t), lambda h, qi: (h, 0, qi))
    seq = pl.BlockSpec((S, LANES), lambda h, qi: (0, h))
    return _call_with_step(
        body, hosted, lambda: _attn_flags(nq), (q, k, v, o_t, do_t, lse), name=name, grid=(MLA_HEADS, nq),
        in_specs=[tile, seq, seq, tile_t, tile_t, stat],
        out_specs=[tile, stat],
        out_shape=[jax.ShapeDtypeStruct((S, MLA_HEADS * LANES), F32), jax.ShapeDtypeStruct((MLA_HEADS, 1, S), F32)],
        sem=("parallel", "arbitrary"))


def _attn_bwd_dkv(q, k, v, do_t, lse, delta, *, name, hosted=None):
    S = q.shape[0]
    t = min(ATTN_TILE, S)
    nq = S // t

    def body(q_ref, k_ref, v_ref, do_ref, lse_ref, delta_ref, dk_ref, dv_ref):
        ki = pl.program_id(1)
        kv, vv = k_ref[...], v_ref[...]

        def step(qb, carry, masked):
            dk, dv = carry
            qt = _rows(q_ref, qb, t)
            s_t = lax.dot_general(kv, qt, NT, preferred_element_type=F32)
            if masked:
                s_t = _att_mask(s_t, qb * t, ki * t)
            p_t = jnp.exp(s_t - _cols(lse_ref.at[0], qb, t))
            dob = _cols(do_ref, qb, t).astype(BF16)
            dv = dv + lax.dot_general(dob, p_t.astype(BF16), NT, preferred_element_type=F32)
            dp_t = lax.dot_general(vv, dob, NN, preferred_element_type=F32)
            ds_t = (p_t * (dp_t - _cols(delta_ref.at[0], qb, t))).astype(BF16)
            dk = dk + lax.dot_general(qt.T, ds_t, NT, preferred_element_type=F32)
            return dk, dv

        zero = jnp.zeros((LANES, t), F32)
        carry = step(ki, (zero, zero), True)
        dk, dv = _loop2(ki + 1, nq, lambda qb, c: step(qb, c, False), carry)
        dk_ref[...] = dk.T
        dv_ref[...] = dv

    tile = pl.BlockSpec((t, LANES), lambda h, ki: (ki, h))
    tile_t = pl.BlockSpec((LANES, t), lambda h, ki: (h, ki))
    seq = pl.BlockSpec((S, LANES), lambda h, ki: (0, h))
    seq_t = pl.BlockSpec((LANES, S), lambda h, ki: (h, 0))
    stat = pl.BlockSpec((1, 1, S), lambda h, ki: (h, 0, 0))
    return _call_with_step(
        body, hosted, lambda: _attn_flags(nq), (q, k, v, do_t, lse, delta), name=name, grid=(MLA_HEADS, nq),
        in_specs=[seq, tile, tile, seq_t, stat, stat],
        out_specs=[tile, tile_t],
        out_shape=[jax.ShapeDtypeStruct((S, MLA_HEADS * LANES), F32), jax.ShapeDtypeStruct((MLA_HEADS * LANES, S), F32)],
        sem=("parallel", "arbitrary"))


def _attn_fwd_p(q, k, v_t, *, name, hosted=None):
    S = q.shape[0]
    t = min(ATTN_TILE, S)
    nq = S // t

    def body(q_ref, k_ref, vt_ref, o_ref, lse_ref):
        qi = pl.program_id(1)
        qv = q_ref[...]

        def scores(kb):
            return lax.dot_general(_rows(k_ref, kb, t), qv, NT, preferred_element_type=F32)

        def weigh(kb, p_t):
            return lax.dot_general(_cols(vt_ref, kb, t), p_t, NN, preferred_element_type=F32)

        def soft(s_t, m, l):
            m_new = jnp.maximum(m, jnp.max(s_t, axis=0, keepdims=True))
            p_t = jnp.exp(s_t - m_new)
            corr = jnp.exp(m - m_new)
            return m_new, corr * l + jnp.sum(p_t, axis=0, keepdims=True), corr, p_t.astype(BF16)

        def step(kb, carry):
            s_cur, m, l, acc, p_prev, corr_prev = carry
            s_next = scores(kb + 1)
            acc = corr_prev * acc + weigh(jnp.maximum(kb - 1, 0), p_prev)
            m, l, corr, p_t = soft(s_cur, m, l)
            return s_next, m, l, acc, p_t, corr

        init = (scores(0), jnp.full((1, t), NEG, F32), jnp.zeros((1, t), F32), jnp.zeros((LANES, t), F32),
                jnp.zeros((t, t), BF16), jnp.ones((1, t), F32))
        s_cur, m, l, acc, p_prev, corr_prev = lax.fori_loop(0, qi, step, init)
        acc = corr_prev * acc + weigh(jnp.maximum(qi - 1, 0), p_prev)
        m, l, corr, p_t = soft(_att_mask(s_cur, qi * t, qi * t), m, l)
        acc = corr * acc + weigh(qi, p_t)
        o_ref[...] = acc / l
        lse_ref[0] = m + jnp.log(l)

    return _call_with_step(
        body, hosted, lambda: _attn_flags(nq), (q, k, v_t), name=name, grid=(MLA_HEADS, nq),
        in_specs=[pl.BlockSpec((t, LANES), lambda h, qi: (qi, h)),
                  pl.BlockSpec((S, LANES), lambda h, qi: (0, h)),
                  pl.BlockSpec((LANES, S), lambda h, qi: (h, 0))],
        out_specs=[pl.BlockSpec((LANES, t), lambda h, qi: (h, qi)),
                   pl.BlockSpec((1, 1, t), lambda h, qi: (h, 0, qi))],
        out_shape=[jax.ShapeDtypeStruct((MLA_HEADS * LANES, S), F32), jax.ShapeDtypeStruct((MLA_HEADS, 1, S), F32)],
        sem=("parallel", "arbitrary"))


def _attn_bwd_dq_p(q, k, k_t, v, o_t, do_t, lse, *, name, hosted=None):
    S = q.shape[0]
    t = min(ATTN_TILE, S)
    nq = S // t

    def body(q_ref, k_ref, kt_ref, v_ref, o_ref, do_ref, lse_ref, dq_ref, delta_ref):
        qi = pl.program_id(1)
        qv = q_ref[...]
        dov = do_ref[...]
        delta = jnp.sum(dov * o_ref[...], axis=0, keepdims=True)
        delta_ref[0] = delta
        dob = dov.astype(BF16)
        lse_v = lse_ref[0]

        def front(kb):
            return (lax.dot_general(_rows(k_ref, kb, t), qv, NT, preferred_element_type=F32),
                    lax.dot_general(_rows(v_ref, kb, t), dob, NN, preferred_element_type=F32))

        def back(kb, ds_t):
            return lax.dot_general(_cols(kt_ref, kb, t), ds_t, NN, preferred_element_type=F32)

        def mid(s_t, dp_t):
            return (jnp.exp(s_t - lse_v) * (dp_t - delta)).astype(BF16)

        def step(kb, carry):
            s_cur, dp_cur, acc, ds_prev = carry
            s_next, dp_next = front(kb + 1)
            acc = acc + back(jnp.maximum(kb - 1, 0), ds_prev)
            return s_next, dp_next, acc, mid(s_cur, dp_cur)

        init = (*front(0), jnp.zeros((LANES, t), F32), jnp.zeros((t, t), BF16))
        s_cur, dp_cur, acc, ds_prev = lax.fori_loop(0, qi, step, init)
        acc = acc + back(jnp.maximum(qi - 1, 0), ds_prev)
        dq_ref[...] = acc + back(qi, mid(_att_mask(s_cur, qi * t, qi * t), dp_cur))

    tile_t = pl.BlockSpec((LANES, t), lambda h, qi: (h, qi))
    stat = pl.BlockSpec((1, 1, t), lambda h, qi: (h, 0, qi))
    seq = pl.BlockSpec((S, LANES), lambda h, qi: (0, h))
    return _call_with_step(
        body, hosted, lambda: _attn_flags(nq), (q, k, k_t, v, o_t, do_t, lse), name=name, grid=(MLA_HEADS, nq),
        in_specs=[pl.BlockSpec((t, LANES), lambda h, qi: (qi, h)), seq,
                  pl.BlockSpec((LANES, S), lambda h, qi: (h, 0)), seq, tile_t, tile_t, stat],
        out_specs=[tile_t, stat],
        out_shape=[jax.ShapeDtypeStruct((MLA_HEADS * LANES, S), F32), jax.ShapeDtypeStruct((MLA_HEADS, 1, S), F32)],
        sem=("parallel", "arbitrary"))


def _attn_bwd_dkv_p(q, q_t, k, v, do_t, lse, delta, *, name, hosted=None):
    S = q.shape[0]
    t = min(ATTN_TILE, S)
    nq = S // t

    def body(q_ref, qt_ref, k_ref, v_ref, do_ref, lse_ref, delta_ref, dk_ref, dv_ref):
        ki = pl.program_id(1)
        kv, vv = k_ref[...], v_ref[...]

        def grad_out(qb):
            return _cols(do_ref, qb, t).astype(BF16)

        def front(qb):
            return (lax.dot_general(kv, _rows(q_ref, qb, t), NT, preferred_element_type=F32),
                    lax.dot_general(vv, grad_out(qb), NN, preferred_element_type=F32))

        def mid(s_t, dp_t, qb):
            p_t = jnp.exp(s_t - _cols(lse_ref.at[0], qb, t))
            return p_t.astype(BF16), (p_t * (dp_t - _cols(delta_ref.at[0], qb, t))).astype(BF16)

        def back(qb, dk, dv, p_t, ds_t):
            return (dk + lax.dot_general(_cols(qt_ref, qb, t), ds_t, NT, preferred_element_type=F32),
                    dv + lax.dot_general(grad_out(qb), p_t, NT, preferred_element_type=F32))

        def step(qb, carry):
            s_cur, dp_cur, dk, dv, p_prev, ds_prev = carry
            s_next, dp_next = front(jnp.minimum(qb + 1, nq - 1))
            dk, dv = back(qb - 1, dk, dv, p_prev, ds_prev)
            p_t, ds_t = mid(s_cur, dp_cur, qb)
            return s_next, dp_next, dk, dv, p_t, ds_t

        s0, dp0 = front(ki)
        p0, ds0 = mid(_att_mask(s0, ki * t, ki * t), dp0, ki)
        zero = jnp.zeros((LANES, t), F32)
        init = (*front(jnp.minimum(ki + 1, nq - 1)), zero, zero, p0, ds0)
        _, _, dk, dv, p_prev, ds_prev = lax.fori_loop(ki + 1, nq, step, init)
        dk, dv = back(nq - 1, dk, dv, p_prev, ds_prev)
        dk_ref[...] = dk
        dv_ref[...] = dv

    tile = pl.BlockSpec((t, LANES), lambda h, ki: (ki, h))
    tile_t = pl.BlockSpec((LANES, t), lambda h, ki: (h, ki))
    seq = pl.BlockSpec((S, LANES), lambda h, ki: (0, h))
    seq_t = pl.BlockSpec((LANES, S), lambda h, ki: (h, 0))
    stat = pl.BlockSpec((1, 1, S), lambda h, ki: (h, 0, 0))
    return _call_with_step(
        body, hosted, lambda: _attn_flags(nq), (q, q_t, k, v, do_t, lse, delta), name=name, grid=(MLA_HEADS, nq),
        in_specs=[seq, seq_t, tile, tile, seq_t, stat, stat],
        out_specs=[tile_t, tile_t],
        out_shape=[jax.ShapeDtypeStruct((MLA_HEADS * LANES, S), F32)] * 2,
        sem=("parallel", "arbitrary"))


def _fn_ln(ctx, x, g, b):
    xhat, _ = _ln_stats(x)
    y = xhat * g + b
    return y, y


def _fn_conv_fwd(ctx, u, up, dtr, w8, cb, dtb):
    first = ctx.i == 0
    y = u * w8[3:4] + cb
    for s in (1, 2, 3):
        y = y + _shift_down(u, up, s, first) * w8[3 - s:4 - s]
    act = y * _sigmoid(y)
    v = dtr + dtb
    e = jnp.exp(-jnp.abs(v))
    one_p = 1.0 + e
    log1p = jnp.where(one_p == 1.0, e, jnp.log(one_p) * e / (one_p - 1.0))
    return y, act, jnp.maximum(v, 0.0) + log1p


def _fn_ssd_post(ctx, y, xs, z, dexp, g):
    yg = (y + xs * dexp) * (z * _sigmoid(z))
    outs = []
    for k in range(2):
        v = yg[:, 256 * k:256 * (k + 1)]
        outs.append(v * lax.rsqrt(_mean1(v * v) + RMS_EPS))
    return (jnp.concatenate(outs, axis=1) * g,)


def _fn_ssd_post_bwd(ctx, dyn, y, xs, z, dexp, g):
    yt = y + xs * dexp
    sig = _sigmoid(z)
    sz = z * sig
    yg = yt * sz
    dyh = dyn * g
    yh, dyg = [], []
    for k in range(2):
        sl = slice(256 * k, 256 * (k + 1))
        v = yg[:, sl]
        rs = lax.rsqrt(_mean1(v * v) + RMS_EPS)
        vh = v * rs
        yh.append(vh)
        dyg.append(rs * (dyh[:, sl] - vh * _mean1(dyh[:, sl] * vh)))
    yh = jnp.concatenate(yh, axis=1)
    dyg = jnp.concatenate(dyg, axis=1)
    dyt = dyg * sz
    dz = dyg * yt * (sig * (1.0 + z * (1.0 - sig)))
    return dyt, dz, dyt * dexp, _sum0(dyt * xs), _sum0(dyn * yh)


def _fn_mla_pre(ctx, ql, kvl, gq, gkv):
    return _rms_fwd(ql, gq), _rms_fwd(kvl, gkv)


def _fn_mla_pre_bwd(ctx, ql, kvl, dqn, dkvn_k, dkvn_v, gq, gkv):
    dql, dgq = _rms_bwd(ql, dqn, gq)
    dkvl, dgkv = _rms_bwd(kvl, dkvn_k + dkvn_v, gkv)
    return dql, dkvl, dgq, dgkv


def _fn_rope(ctx, qp, kn, kr, ta, tb, tc):
    kpe = _rope(kr, ta, tb, tc)
    qs, ks = [], []
    for h in range(MLA_HEADS):
        sl = slice(128 * h, 128 * (h + 1))
        qs.append(_rope(qp[:, sl], ta, tb, tc) * MLA_SCALE)
        ks.append(kn[:, sl] + kpe)
    return jnp.concatenate(qs, axis=1), jnp.concatenate(ks, axis=1)


def _fn_rope_bwd(ctx, dq, dk, ta, tb, tc):
    qs = []
    ksum = jnp.zeros_like(ta)
    for h in range(MLA_HEADS):
        sl = slice(128 * h, 128 * (h + 1))
        qs.append(_rope_bwd(dq[:, sl] * MLA_SCALE, ta, tb, tc))
        ksum = ksum + dk[:, sl]
    lane = _lane(ksum.shape)
    dkr = jnp.where((lane >= 64) & (lane < 96), _rope_bwd(ksum, ta, tb, tc), 0.0)
    return jnp.concatenate(qs, axis=1), dkr


MEM_SCALE = MEM_HEAD_DIM ** -0.5


def _mem_probs(qh, kh):
    s = _dot(qh, kh, NT) * MEM_SCALE
    p = jnp.exp(s - jnp.max(s, axis=1, keepdims=True))
    return p / jnp.sum(p, axis=1, keepdims=True)


def _fn_mem_fwd(ctx, q, km, vm):
    outs = []
    for h in range(MEM_HEADS):
        sl = slice(256 * h, 256 * (h + 1))
        outs.append(_dot(_mem_probs(q[:, sl], km[:, sl]), vm[:, sl]))
    return (jnp.concatenate(outs, axis=1),)


def _fn_mem_bwd(ctx, q, do, km, vm):
    dqs, dks, dvs = [], [], []
    for h in range(MEM_HEADS):
        sl = slice(256 * h, 256 * (h + 1))
        p = _mem_probs(q[:, sl], km[:, sl])
        dvs.append(_dot(p, do[:, sl], TN))
        dp = _dot(do[:, sl], vm[:, sl], NT)
        ds = p * (dp - jnp.sum(dp * p, axis=1, keepdims=True)) * MEM_SCALE
        dqs.append(_dot(ds, km[:, sl]))
        dks.append(_dot(ds, q[:, sl], TN))
    return jnp.concatenate(dqs, axis=1), jnp.concatenate(dks, axis=1), jnp.concatenate(dvs, axis=1)


def _fn_res_ln(ctx, h, r, g, b):
    xhat, _ = _ln_stats(ALPHA * h + r)
    y = xhat * g + b
    return y, y


def _fn_res_ln_bwd(ctx, h, r, d1, d2, g):
    xhat, rstd = _ln_stats(ALPHA * h + r)
    return _ln_bwd(xhat, rstd, ALPHA * d1 + d2, g)


def _fn_res2_ln(ctx, h, r1, r2, g, b):
    xhat, _ = _ln_stats(ALPHA * h + (r1 + r2))
    return (xhat * g + b,)


def _fn_res2_ln_bwd(ctx, h, r1, r2, d1, d2, g):
    xhat, rstd = _ln_stats(ALPHA * h + (r1 + r2))
    return _ln_bwd(xhat, rstd, ALPHA * d1 + d2, g)


def _fn_in_ln_bwd(ctx, x, d1, d2, g):
    xhat, rstd = _ln_stats(x)
    return _ln_bwd(xhat, rstd, ALPHA * d1 + d2, g)


def _fn_final(ctx, h2, ff, tgt, g, b):
    xhat, rstd = _ln_stats(ALPHA * h2 + ff)
    e = xhat * g + b - tgt
    loss = 0.5 * _sum0(jnp.sum(e * e, axis=1, keepdims=True)) / D_MODEL
    dx, dg, db = _ln_bwd(xhat, rstd, e / D_MODEL, g)
    return dx, dx, dg, db, loss


def _epi_du(da, u):
    return da * 2.0 * jnp.maximum(u.astype(F32), 0.0)


def _relu2(u):
    r = jnp.maximum(u.astype(F32), 0.0)
    return r * r


def _fn_conv_bwd_a(ctx, y, dxs1, dxs2, dbc, dtr, ddt, dtb):
    sig = _sigmoid(y)
    dact = jnp.concatenate([dxs1 + dxs2, dbc], axis=1)
    dyc = dact * (sig * (1.0 + y * (1.0 - sig)))
    ddtr = ddt * _sigmoid(dtr + dtb)
    return dyc, ddtr, _sum0(dyc), _sum0(ddtr)


def _fn_conv_bwd_b(ctx, d, dn, u, up, w8):
    first, last = ctx.i == 0, ctx.i == ctx.n - 1
    du = d * w8[3:4]
    row = lax.broadcasted_iota(jnp.int32, w8.shape, 0)
    dw = jnp.where(row == 3, _sum0(d * u), 0.0)
    for s in (1, 2, 3):
        du = du + _shift_up(d, dn, s, last) * w8[3 - s:4 - s]
        dw = dw + jnp.where(row == 3 - s, _sum0(d * _shift_down(u, up, s, first)), 0.0)
    return du, dw


def _fn_adam(ctx, w, g, m, v):
    m = ADAM_B1 * m + (1.0 - ADAM_B1) * g
    v = ADAM_B2 * v + (1.0 - ADAM_B2) * (g * g)
    m_hat = m / (1.0 - ADAM_B1 ** ADAM_STEP)
    v_hat = v / (1.0 - ADAM_B2 ** ADAM_STEP)
    return -ADAM_LR * (m_hat / (jnp.sqrt(v_hat) + ADAM_EPS) + ADAM_WD * w), m, v


def _fn_add2(ctx, a, b):
    s = a + b
    return s, s


def _fn_add4(ctx, a, r0, r1, r2):
    return (((a + r0.astype(F32)) + r1.astype(F32)) + r2.astype(F32),)


def _z(r, c, dt):
    return jnp.zeros((r, c), dt)


W_IN_SHARD = 554
W_IN_GROUPS = [(0, 512, 1024), (512, 1536, 0), (1536, 1544, 1920), (1544, 1928, 1536), (1928, 2184, 2048),
               (2184, 2216, 2368)]


def _pad_w_in(ws):
    r, dt = ws.shape[1], ws.dtype

    def cols(a, b):
        out = []
        for k in range(N_SHARD):
            lo, hi = max(a, k * W_IN_SHARD), min(b, (k + 1) * W_IN_SHARD)
            if lo < hi:
                out.append(ws[k][:, lo - k * W_IN_SHARD:hi - k * W_IN_SHARD])
        return out

    return jnp.concatenate(cols(512, 1536) + cols(0, 512) + cols(1544, 1928) + cols(1536, 1544) + [_z(r, 120, dt)]
                           + cols(1928, 2184) + [_z(r, 64, dt)] + cols(2184, 2216) + [_z(r, 32, dt), _z(r, 128, dt)],
                           axis=1)


def _unpad_w_in(d):
    shards = []
    for k in range(N_SHARD):
        a, b = k * W_IN_SHARD, (k + 1) * W_IN_SHARD
        parts = []
        for o0, o1, p0 in W_IN_GROUPS:
            lo, hi = max(a, o0), min(b, o1)
            if lo < hi:
                parts.append(d[:, p0 + lo - o0:p0 + hi - o0])
        shards.append(jnp.concatenate(parts, axis=1))
    return jnp.stack(shards)


def _pad_heads(w, width):
    r = w.shape[0]
    w3 = w.reshape(r, MLA_HEADS, width)
    return jnp.pad(w3, ((0, 0), (0, 0), (0, 128 - width))).reshape(r, MLA_HEADS * 128)


def _pad_w_kv(w):
    r = w.shape[0]
    w4 = w.reshape(r, MLA_HEADS, 2, 64)
    return jnp.pad(w4, ((0, 0), (0, 0), (0, 0), (0, 64))).reshape(r, MLA_HEADS * 256)


def _unpad_w_kv(d):
    r = d.shape[0]
    return d.reshape(r, MLA_HEADS, 2, 128)[:, :, :, :64].reshape(r, MLA_HEADS * 128)


def _pad_w_mix(w):
    wo = jnp.pad(w[512:1024].reshape(MLA_HEADS, 64, D_MODEL), ((0, 0), (0, 64), (0, 0))).reshape(1024, D_MODEL)
    return jnp.concatenate([wo, w[0:512]], axis=0)


def _unpad_w_mix(d):
    do = d[:1024].reshape(MLA_HEADS, 128, D_MODEL)[:, :64].reshape(512, D_MODEL)
    return jnp.concatenate([d[1024:1536], do], axis=0)


def _row(v, width=None):
    v = v.reshape(1, -1).astype(F32)
    if width is not None and v.shape[1] < width:
        v = jnp.pad(v, ((0, 0), (0, width - v.shape[1])))
    return v


def _old_local_step(x, mem, positions, target, W, P):
    S = x.shape[0]
    tr = ROW_TILE
    w_in_p = _pad_w_in(W["w_in"])
    w_q_p = _pad_heads(W["w_q_up"], MLA_QK)
    w_kv3 = W["w_kv_up"].reshape(MLA_KV_RANK, MLA_HEADS, 128)
    w_k_p = _pad_heads(w_kv3[:, :, :64].reshape(MLA_KV_RANK, 512), 64)
    w_v_p = _pad_heads(w_kv3[:, :, 64:].reshape(MLA_KV_RANK, 512), 64)
    w_v_pt = w_v_p.T
    w_mix_y = W["w_mix_out"][0:512]
    w_mix_o = jnp.pad(W["w_mix_out"][512:1024].reshape(MLA_HEADS, 64, D_MODEL),
                      ((0, 0), (0, 64), (0, 0))).reshape(MLA_HEADS * 128, D_MODEL)
    conv_w8 = jnp.pad(P["conv_w"].astype(F32), ((0, 4), (0, 0)))
    conv_b = _row(P["conv_b"])
    dt_b = _row(P["dt_bias"], 128)
    a_head = -jnp.exp(P["a_log"].reshape(-1).astype(F32))
    a_row = _row(a_head, 128)
    dexp = jnp.repeat(P["d_skip"].reshape(-1).astype(F32), 64).reshape(1, 512)
    g_ssd, g_q, g_kv = _row(P["ssd_norm_g"]), _row(P["q_norm_g"]), _row(P["kv_norm_g"])
    g_in, b_in = _row(P["ln_in_g"]), _row(P["ln_in_b"])
    g1, b1, g2, b2, g3, b3 = (_row(P[k]) for k in ("ln1_g", "ln1_b", "ln2_g", "ln2_b", "ln3_g", "ln3_b"))

    half = MLA_ROPE // 2
    inv_freq = jnp.power(ROPE_THETA, -jnp.arange(half, dtype=F32) / half)
    ang = positions.reshape(S, 1).astype(F32) * inv_freq
    cos, sin = jnp.cos(ang), jnp.sin(ang)
    zc = lambda n: jnp.zeros((S, n), F32)
    rope_a = jnp.concatenate([jnp.ones((S, 64), F32), cos, cos, zc(32)], axis=1)
    rope_b = jnp.concatenate([zc(80), sin, zc(32)], axis=1)
    rope_c = jnp.concatenate([zc(64), -sin, zc(48)], axis=1)

    proj = _mm(h0_b, w_in_p, form="nn", name="mm_in")
    conv_y, xbc, dt = _rowwise(
        _fn_conv_fwd, [(proj,) + SEG_XBC, ("prev", proj) + SEG_XBC, (proj,) + SEG_DT], [conv_w8, conv_b, dt_b],
        [1024, 1024, 128], tr=tr, name="conv_fwd")
    y_ssd, hs = _ssd_fwd(xbc, dt, a_row, name="ssd_fwd")
    (y_n,) = _rowwise(_fn_ssd_post, [y_ssd, (xbc, 0, 512), (proj,) + SEG_Z], [dexp, g_ssd], [(512, BF16)], tr=tr,
                      name="ssd_post")
    q_n, kv_n = _rowwise(_fn_mla_pre, [(proj,) + SEG_QLAT, (proj,) + SEG_KVLAT], [g_q, g_kv], [384, 256], tr=tr,
                         name="mla_pre")
    qp = _mm(q_n, w_q_p, form="nn", name="mm_q_up")
    kn = _mm(kv_n, w_k_p, form="nn", name="mm_k_up")
    v_nat = _mm(kv_n, w_v_p, form="nn", out_dtype=BF16, name="mm_v_up")
    v_t = _mm(w_v_pt, kv_n, form="nt", out_dtype=BF16, name="mm_v_up_t")
    q_rot, k_full = _rowwise(_fn_rope, [qp, kn, (proj,) + SEG_KR, rope_a, rope_b, rope_c], [],
                             [(1024, BF16), (1024, BF16)], tr=tr, name="rope")
    o_t, lse = _attn_fwd(q_rot, k_full, v_t, name="attn_fwd")
    mix_o = _mm(o_t, w_mix_o, form="tn", name="mm_mix_o")
    mix_y = _mm(y_n, w_mix_y, form="nn", name="mm_mix_y")
    (h1,) = _rowwise(_fn_res2_ln, [h0, mix_o, mix_y], [g1, b1], [D_MODEL], tr=tr, name="ln1")
    qm = _mm(h1, W["w_mem_q"], form="nn", name="mm_mem_q")
    km = _mm(mem, W["w_mem_k"], form="nn", name="mm_mem_k")
    vm = _mm(mem, W["w_mem_v"], form="nn", name="mm_mem_v")
    (om,) = _rowwise(_fn_mem_fwd, [qm], [km, vm], [(D_MODEL, BF16)], tr=tr, name="mem_fwd")
    xa = _mm(om, W["w_mem_o"], form="nn", name="mm_mem_o")
    h2, h2_b = _rowwise(_fn_res_ln, [h1, xa], [g2, b2], [D_MODEL, (D_MODEL, BF16)], tr=tr, name="ln2")
    u = _mm(h2, W["w_up"], form="nn", name="mm_up")
    ff = _mm(u, W["w_down"], form="nn", a_pro=_relu2, name="mm_down")

    dt3, dt3_b, dg3, db3, loss = _rowwise(_fn_final, [h2, ff, target], [g3, b3], [D_MODEL, (D_MODEL, BF16)],
                                   [(1, D_MODEL), (1, D_MODEL), (1, 128)], tr=tr, name="ln3_loss")
    da = _mm(dt3, W["w_down"], form="nt", name="mm_down_dx")
    dw_down = _mm(u, dt3, form="tn", a_pro=_relu2, name="mm_down_dw")
    dw_up = _mm(h2, du, form="tn", name="mm_up_dw")
    dh2 = _mm(du, W["w_up"], form="nt", name="mm_up_dx")
    dt2, dg2, db2 = _rowwise(_fn_res_ln_bwd, [h1, xa, dt3, dh2], [g2], [D_MODEL], [(1, D_MODEL)] * 2, tr=tr,
                             name="ln2_bwd")
    dom = _mm(dt2, W["w_mem_o"], form="nt", name="mm_mem_o_dx")
    dw_mem_o = _mm(om, dt2, form="tn", name="mm_mem_o_dw")
    dqm, dkm, dvm = _rowwise(_fn_mem_bwd, [qm, dom], [km, vm], [(D_MODEL, BF16)], [(256, D_MODEL)] * 2, tr=tr,
                             name="mem_bwd")
    dw_mem_q = _mm(h1, dqm, form="tn", name="mm_mem_q_dw")
    dw_mem_k = _mm(mem, dkm, form="tn", name="mm_mem_k_dw")
    dw_mem_v = _mm(mem, dvm, form="tn", name="mm_mem_v_dw")
    dh1 = _mm(dqm, W["w_mem_q"], form="nt", name="mm_mem_q_dx")
    dt1, dg1, db1 = _rowwise(_fn_res2_ln_bwd, [h0, mix_o, mix_y, dt2, dh1], [g1], [D_MODEL], [(1, D_MODEL)] * 2,
                             tr=tr, name="ln1_bwd")
    do_t = _mm(w_mix_o, dt1, form="nt", name="mm_mix_o_dx")
    dy_n = _mm(dt1, w_mix_y, form="nt", name="mm_mix_y_dx")
    dw_mix_o = _mm(o_t, dt1, form="nn", name="mm_mix_o_dw")
    dw_mix_y = _mm(y_n, dt1, form="tn", name="mm_mix_y_dw")
    dq_t, delta = _attn_bwd_dq(q_rot, k_full, k_full.T, v_nat, o_t, do_t, lse, name="attn_bwd_dq")
    dk_t, dv_t = _attn_bwd_dkv(q_rot, q_rot.T, k_full, v_nat, do_t, lse, delta, name="attn_bwd_dkv")
    dk = dk_t.T
    dqp, dkr = _rowwise(_fn_rope_bwd, [dq_t.T, dk, rope_a, rope_b, rope_c], [], [(1024, BF16), (128, BF16)], tr=tr,
                        name="rope_bwd")
    dw_q_p = _mm(q_n, dqp, form="tn", name="mm_q_up_dw")
    dq_n = _mm(dqp, w_q_p, form="nt", name="mm_q_up_dx")
    dw_k_p = _mm(kv_n, dk, form="tn", name="mm_k_up_dw")
    dkv_n1 = _mm(dk, w_k_p, form="nt", name="mm_k_up_dx")
    dw_v_pt = _mm(dv_t, kv_n, form="nn", name="mm_v_up_dw")
    dkv_n2 = _mm(dv_t, w_v_pt, form="tn", name="mm_v_up_dx")
    dq_lat, dkv_lat, dg_q, dg_kv = _rowwise(
        _fn_mla_pre_bwd, [(proj,) + SEG_QLAT, (proj,) + SEG_KVLAT, dq_n, dkv_n1, dkv_n2], [g_q, g_kv], [(384, BF16), (256, BF16)],
        [(1, 384), (1, 256)], tr=tr, name="mla_pre_bwd")
    dy_ssd, dz, dxs_skip, ddexp, dg_ssd = _rowwise(
        _fn_ssd_post_bwd, [dy_n, y_ssd, (xbc, 0, 512), (proj,) + SEG_Z], [dexp, g_ssd],
        [512, (512, BF16), 512], [(1, 512)] * 2, tr=tr, name="ssd_post_bwd")
    dxs, dbc, ddt, da_head = _ssd_bwd(xbc, dt, a_row, hs, dy_ssd, name="ssd_bwd")
    dyc, ddtr, dconv_b, ddt_b = _rowwise(
        _fn_conv_bwd_a, [conv_y, dxs, dxs_skip, dbc, (proj,) + SEG_DT, ddt], [dt_b], [1024, (128, BF16)],
        [(1, 1024), (1, 128)], tr=tr, name="conv_bwd_a")
    dxbc, dconv_w8 = _rowwise(
        _fn_conv_bwd_b, [dyc, ("next", dyc, 0, 1024), (proj,) + SEG_XBC, ("prev", proj) + SEG_XBC], [conv_w8], [(1024, BF16)],
        [(8, 1024)], tr=tr, name="conv_bwd_b")
    dproj = jnp.concatenate([dxbc, dz, dq_lat, ddtr, dkv_lat, dkr, jnp.zeros((S, 128), BF16)], axis=1)
    res = _mm(h0_b, dproj, form="tn", name="mm_in_dw", hosted=_pair_fill_step(gp) if dist else None)
    dw_in_p, red_a = (res[0], res[1]) if dist else (res, None)
    big_b = _group_b_grads(dw_in_p, dw_q_p, dw_k_p, dw_v_pt, dconv_w8)
    q_b = None
    if dist:
        gp_c, gp_b = _pack_group_b(big_b)
        dh0, theirs_c, theirs_b = _mm(dproj, w_in_p, form="nt", name="mm_in_dx", hosted=_merge_steps(
            [_pair_exchange_step(gp_c), _pair_exchange_step(gp_b)]))
        q_b = ((gp_c, theirs_c, _pair_sum(gp_c, theirs_c, "pair_sum_w_in")),
               (gp_b, theirs_b, _pair_sum(gp_b, theirs_b, "pair_sum_b")))
        gp = red_a
    else:
        dh0 = _mm(dproj, w_in_p, form="nt", name="mm_in_dx")
    grad_x, dg_in, db_in = _rowwise(_fn_in_ln_bwd, [x, dt1, dh0], [g_in], [D_MODEL], [(1, D_MODEL)] * 2, tr=tr,
                                    name="ln_in_bwd")

    big = {
        "w_in": _unpad_w_in(dw_in_p),
        "w_q_up": dw_q_p.reshape(384, MLA_HEADS, 128)[:, :, :MLA_QK].reshape(384, MLA_HEADS * MLA_QK),
        "w_kv_up": jnp.concatenate([dw_k_p.reshape(MLA_KV_RANK, MLA_HEADS, 128)[:, :, :64],
                                    dw_v_pt.T.reshape(MLA_KV_RANK, MLA_HEADS, 128)[:, :, :64]], axis=2).reshape(
                                        MLA_KV_RANK, MLA_HEADS * 128),
        "w_mix_out": jnp.concatenate([dw_mix_y, dw_mix_o.reshape(MLA_HEADS, 128, D_MODEL)[:, :64].reshape(
            512, D_MODEL)], axis=0),
        "w_mem_q": dw_mem_q, "w_mem_k": dw_mem_k, "w_mem_v": dw_mem_v, "w_mem_o": dw_mem_o,
        "w_up": dw_up, "w_down": dw_down,
        "conv_w": dconv_w8[0:4],
    }
    small = {
        "ln_in_g": dg_in, "ln_in_b": db_in, "conv_b": dconv_b, "dt_bias": ddt_b[:, :8],
        "a_log": da_head[:, :8] * a_head.reshape(1, 8),
        "d_skip": ddexp.reshape(8, 64).sum(axis=1).reshape(1, 8),
        "ssd_norm_g": dg_ssd, "q_norm_g": dg_q, "kv_norm_g": dg_kv,
        "ln1_g": dg1, "ln1_b": db1, "ln2_g": dg2, "ln2_b": db2, "ln3_g": dg3, "ln3_b": db3,
    }
    return loss[0, 0], grad_x, big, small


BIG = {
    "w_in": (1024, 2216, 1), "w_q_up": (384, 768, 1), "w_kv_up": (256, 1024, 1), "w_mix_out": (1024, 1024, 0),
    "w_mem_q": (1024, 1024, 0), "w_mem_k": (1024, 1024, 0), "w_mem_v": (1024, 1024, 0), "w_mem_o": (1024, 1024, 0),
    "w_up": (1024, 4096, 1), "w_down": (4096, 1024, 0), "conv_w": (4, 1024, 1),
}
BIG_ORDER = list(BIG)
SMALL_ORDER = ["ln_in_g", "ln_in_b", "conv_b", "dt_bias", "a_log", "d_skip", "ssd_norm_g", "q_norm_g", "kv_norm_g",
               "ln1_g", "ln1_b", "ln2_g", "ln2_b", "ln3_g", "ln3_b"]
N_SHARD = 4
PACK_COLS = 1024
PACK_ROWS = 4032
HALF_ROWS = PACK_ROWS // 2
GATHER_CHUNKS = 3
CHIP_CHUNKS = 3
PAIR_CHUNKS = 4


def _shard_shape(name):
    r, c, ax = BIG[name]
    return (r // N_SHARD, c) if ax == 0 else (r, c // N_SHARD)


def _split_shards(name, full):
    r, c, ax = BIG[name]
    if ax == 0:
        return full.reshape(N_SHARD, -1)
    return full.reshape(r, N_SHARD, c // N_SHARD).transpose(1, 0, 2).reshape(N_SHARD, -1)


def _join_shards(name, parts):
    r, c, ax = BIG[name]
    if ax == 0:
        return parts.reshape(r, c)
    return parts.reshape(N_SHARD, r, c // N_SHARD).transpose(1, 0, 2).reshape(r, c)


HBM = pl.BlockSpec(memory_space=pl.ANY)


def _place():
    x, y, c = lax.axis_index("x"), lax.axis_index("y"), lax.axis_index("c")
    chips = [(1 - x, y), (x, 1 - y), (1 - x, 1 - y)]
    return x, y, c, chips


def _gather_weights(wp):
    R, C = wp.shape
    H = R // 2
    nq = GATHER_CHUNKS
    CH = H // nq

    def body(w_ref, out_ref, send_sems, recv_sems):
        x, y, c, chips = _place()
        sib = (x, y, 1 - c)

        def piece(k, hc, q):
            return out_ref.at[k, pl.ds(hc * H + q * CH, CH), :]

        def copy(j, src, dst, to):
            return pltpu.make_async_remote_copy(src_ref=src, dst_ref=dst, send_sem=send_sems.at[j],
                                                recv_sem=recv_sems.at[j], device_id=to, device_id_type=MESH)

        me = 2 * x + y
        sends = []
        for q in range(nq):
            for j, (px, py) in enumerate(chips):
                cp = copy(j * nq + q, w_ref.at[pl.ds(c * H + q * CH, CH), :], piece(me, c, q), (px, py, c))
                cp.start()
                sends.append(cp)
        fwds = []
        for q in range(nq):
            for j, (px, py) in enumerate(chips):
                k = 2 * px + py
                copy(j * nq + q, piece(k, c, q), piece(k, c, q), (px, py, c)).wait_recv()
                f = copy((3 + j) * nq + q, piece(k, c, q), piece(k, c, q), sib)
                f.start()
                fwds.append(f)
        for q in range(nq):
            for j, (px, py) in enumerate(chips):
                k = 2 * px + py
                copy((3 + j) * nq + q, piece(k, 1 - c, q), piece(k, 1 - c, q), sib).wait_recv()
        for cp in sends + fwds:
            cp.wait_send()

    out = pl.pallas_call(
        body, name="gather_weights", in_specs=[HBM], out_specs=HBM,
        out_shape=jax.ShapeDtypeStruct((N_SHARD, R, C), wp.dtype),
        scratch_shapes=[pltpu.SemaphoreType.DMA((6 * nq,)), pltpu.SemaphoreType.DMA((6 * nq,))],
    )(wp)
    me = 2 * lax.axis_index("x") + lax.axis_index("y")
    return lax.dynamic_update_slice(out, wp[None], (me, 0, 0))


def _pair_exchange(gp):
    n, R, C = gp.shape
    H = R // 2
    nq = PAIR_CHUNKS
    CH = H // nq

    def body(g_ref, theirs_ref, send_sems, recv_sems):
        x, y, c, _ = _place()
        swaps = []
        for k in range(n):
            for q in range(nq):
                cp = pltpu.make_async_remote_copy(
                    src_ref=g_ref.at[k, pl.ds((1 - c) * H + q * CH, CH), :], dst_ref=theirs_ref.at[k, pl.ds(q * CH, CH), :],
                    send_sem=send_sems.at[k * nq + q], recv_sem=recv_sems.at[k * nq + q], device_id=(x, y, 1 - c),
                    device_id_type=MESH)
                cp.start()
                swaps.append(cp)
        for cp in swaps:
            cp.wait()

    theirs = pl.pallas_call(
        body, name="pair_exchange", in_specs=[HBM], out_specs=HBM,
        out_shape=jax.ShapeDtypeStruct((n, H, C), gp.dtype),
        scratch_shapes=[pltpu.SemaphoreType.DMA((n * nq,)), pltpu.SemaphoreType.DMA((n * nq,))],
    )(gp)
    mine = lax.dynamic_slice(gp, (0, lax.axis_index("c") * H, 0), (n, H, C))
    return mine, theirs


def _chip_exchange(pb):
    n, H, C = pb.shape
    nq = CHIP_CHUNKS
    CH = H // nq

    def body(pb_ref, got_ref, send_sems, recv_sems):
        x, y, c, chips = _place()
        sends = []
        for q in range(nq):
            for j, (px, py) in enumerate(chips):
                cp = pltpu.make_async_remote_copy(
                    src_ref=pb_ref.at[2 * px + py, pl.ds(q * CH, CH), :], dst_ref=got_ref.at[j, pl.ds(q * CH, CH), :],
                    send_sem=send_sems.at[j * nq + q], recv_sem=recv_sems.at[j * nq + q],
                    device_id=(px, py, c), device_id_type=MESH)
                cp.start()
                sends.append(cp)
        for cp in sends:
            cp.wait()

    return pl.pallas_call(
        body, name="chip_exchange", in_specs=[HBM], out_specs=HBM,
        out_shape=jax.ShapeDtypeStruct((3, H, C), BF16),
        scratch_shapes=[pltpu.SemaphoreType.DMA((3 * nq,)), pltpu.SemaphoreType.DMA((3 * nq,))],
    )(pb)


def _pair_join(q):
    H, C = q.shape
    nq = PAIR_CHUNKS
    CH = H // nq

    def body(q_ref, theirs_ref, send_sems, recv_sems):
        x, y, c, _ = _place()
        pushes = []
        for j in range(nq):
            cp = pltpu.make_async_remote_copy(
                src_ref=q_ref.at[pl.ds(j * CH, CH), :], dst_ref=theirs_ref.at[pl.ds(j * CH, CH), :],
                send_sem=send_sems.at[j], recv_sem=recv_sems.at[j], device_id=(x, y, 1 - c), device_id_type=MESH)
            cp.start()
            pushes.append(cp)
        for cp in pushes:
            cp.wait()

    theirs = pl.pallas_call(
        body, name="pair_join", in_specs=[HBM], out_specs=HBM,
        out_shape=jax.ShapeDtypeStruct((H, C), F32),
        scratch_shapes=[pltpu.SemaphoreType.DMA((nq,)), pltpu.SemaphoreType.DMA((nq,))],
    )(q)
    c = lax.axis_index("c")
    out = jnp.zeros((2 * H, C), F32)
    out = lax.dynamic_update_slice(out, q, (c * H, 0))
    return lax.dynamic_update_slice(out, theirs, ((1 - c) * H, 0))


N_DEV = 8


def _small_all_reduce(g, step=None):
    r, cdim = g.shape
    si, so = (len(step.inputs), len(step.out_shapes)) if step else (0, 0)

    def body(g_ref, *refs):
        s_ins, out_ref, s_outs = refs[:si], refs[si], refs[si + 1:si + 1 + so]
        buf, send_sems, recv_sems = refs[si + 1 + so:si + 4 + so]
        s_sems = refs[si + 4 + so:]
        if step:
            step.start(s_ins, s_outs, s_sems)
        x, y, c, _ = _place()
        me = 4 * x + 2 * y + c
        buf[me] = g_ref[...]
        copies = []
        for d in range(1, N_DEV):
            to = me ^ d
            cp = pltpu.make_async_remote_copy(src_ref=g_ref, dst_ref=buf.at[me], send_sem=send_sems.at[d - 1],
                                              recv_sem=recv_sems.at[d - 1],
                                              device_id=(to // 4, (to // 2) % 2, to % 2), device_id_type=MESH)
            cp.start()
            copies.append(cp)
        for cp in copies:
            cp.wait()
        acc = buf[0]
        for d in range(1, N_DEV):
            acc = acc + buf[d]
        out_ref[...] = acc
        if step:
            step.finish(s_ins, s_outs, s_sems)

    res = pl.pallas_call(
        body, name="small_all_reduce",
        in_specs=[pl.BlockSpec(memory_space=pltpu.VMEM)] + [HBM] * si,
        out_specs=[pl.BlockSpec(memory_space=pltpu.VMEM)] + [HBM] * so,
        out_shape=[jax.ShapeDtypeStruct((r, cdim), F32)] + (list(step.out_shapes) if step else []),
        scratch_shapes=[pltpu.VMEM((N_DEV, r, cdim), F32), pltpu.SemaphoreType.DMA((N_DEV - 1,)),
                        pltpu.SemaphoreType.DMA((N_DEV - 1,))] + (_sem_scratch(step) if step else []),
    )(g, *(step.inputs if step else []))
    return res if step else res[0]


def _adam(w, g, m, v, name):
    shape = w.shape
    w2, g2, m2, v2 = (t.reshape(-1, shape[-1]) for t in (w, g, m, v))
    d, mn, vn = _rowwise(_fn_adam, [w2, g2, m2, v2], [], [shape[-1]] * 3, tr=256, name=name)
    return d.reshape(shape), mn.reshape(shape), vn.reshape(shape)


def _old_kernel(x, mem, positions, ln_in_g, ln_in_b, w_in, conv_w, conv_b, dt_bias, a_log, d_skip, ssd_norm_g, q_norm_g, w_q_up, kv_norm_g, w_kv_up, w_mix_out, ln1_g, ln1_b, w_mem_q, w_mem_k, w_mem_v, w_mem_o, ln2_g, ln2_b, w_up, w_down, ln3_g, ln3_b, loss_target, m_ln_in_g, m_ln_in_b, m_w_in, m_conv_w, m_conv_b, m_dt_bias, m_a_log, m_d_skip, m_ssd_norm_g, m_q_norm_g, m_w_q_up, m_kv_norm_g, m_w_kv_up, m_w_mix_out, m_ln1_g, m_ln1_b, m_w_mem_q, m_w_mem_k, m_w_mem_v, m_w_mem_o, m_ln2_g, m_ln2_b, m_w_up, m_w_down, m_ln3_g, m_ln3_b, v_ln_in_g, v_ln_in_b, v_w_in, v_conv_w, v_conv_b, v_dt_bias, v_a_log, v_d_skip, v_ssd_norm_g, v_q_norm_g, v_w_q_up, v_kv_norm_g, v_w_kv_up, v_w_mix_out, v_ln1_g, v_ln1_b, v_w_mem_q, v_w_mem_k, v_w_mem_v, v_w_mem_o, v_ln2_g, v_ln2_b, v_w_up, v_w_down, v_ln3_g, v_ln3_b):
    args = dict(locals())
    weights = BIG_ORDER + SMALL_ORDER

    flat = []
    for n in BIG_ORDER:
        s = args[n].reshape(-1)
        if n == "conv_w":
            flat.append(lax.bitcast_convert_type(s.astype(F32), BF16).reshape(-1))
        else:
            flat.append(s.astype(BF16))
    flat = jnp.concatenate(flat)
    wp = jnp.pad(flat, (0, PACK_ROWS * PACK_COLS - flat.shape[0])).reshape(PACK_ROWS, PACK_COLS)
    gathered = _gather_weights(wp).reshape(N_SHARD, -1)
    W, off = {}, 0
    for n in BIG_ORDER:
        sr, sc = _shard_shape(n)
        cnt = sr * sc
        if n == "conv_w":
            part = lax.bitcast_convert_type(gathered[:, off:off + 2 * cnt].reshape(N_SHARD, cnt, 2), F32)
            off += 2 * cnt
        else:
            part = gathered[:, off:off + cnt]
            off += cnt
        W[n] = _join_shards(n, part)
    P = {n: args[n] for n in SMALL_ORDER}
    P["conv_w"] = W.pop("conv_w")

    loss, grad_x, gbig, gsmall = _local_step(x[0], mem[0], positions[0], loss_target[0], W, P)
    loss = lax.psum(loss, ("x", "y", "c"))

    gflat = jnp.concatenate([_split_shards(n, gbig[n]) for n in BIG_ORDER], axis=1)
    gp = jnp.pad(gflat, ((0, 0), (0, PACK_ROWS * PACK_COLS - gflat.shape[1]))).reshape(N_SHARD, PACK_ROWS, PACK_COLS)
    mine, theirs = _pair_exchange(gp)
    pf, pb = _rowwise(_fn_add2, [mine.reshape(-1, PACK_COLS), theirs.reshape(-1, PACK_COLS)], [],
                      [PACK_COLS, (PACK_COLS, BF16)], tr=288, name="pair_sum")
    pf = pf.reshape(N_SHARD, HALF_ROWS, PACK_COLS)
    pb = pb.reshape(N_SHARD, HALF_ROWS, PACK_COLS)
    got = _chip_exchange(pb).reshape(3 * HALF_ROWS, PACK_COLS)
    own = lax.dynamic_index_in_dim(pf, 2 * lax.axis_index("x") + lax.axis_index("y"), axis=0, keepdims=False)
    (q,) = _rowwise(_fn_add4, [own] + [(got, 0, PACK_COLS, j * HALF_ROWS) for j in range(3)], [], [PACK_COLS],
                    tr=288, name="chip_sum", n_rows=HALF_ROWS)
    red = _pair_join(q).reshape(-1)

    gs = jnp.concatenate([_row(gsmall[n], PACK_COLS) for n in SMALL_ORDER] + [jnp.zeros((1, PACK_COLS), F32)], axis=0)
    gs = _small_all_reduce(gs)

    grads, deltas, new_m, new_v = {}, {}, {}, {}
    off = 0
    for n in BIG_ORDER:
        sr, sc = _shard_shape(n)
        g = red[off:off + sr * sc].reshape(args[n].shape)
        off += sr * sc
        grads[n] = g
        deltas[n], new_m[n], new_v[n] = _adam(args[n], g, args["m_" + n], args["v_" + n], "adam_" + n)
    pack = lambda pre: jnp.concatenate([_row(args[pre + n], PACK_COLS) for n in SMALL_ORDER]
                                       + [jnp.zeros((1, PACK_COLS), F32)], axis=0)
    ds, ms, vs = _rowwise(_fn_adam, [pack(""), gs, pack("m_"), pack("v_")], [], [PACK_COLS] * 3, tr=16,
                          name="adam_small")
    for i, n in enumerate(SMALL_ORDER):
        cnt = args[n].size
        take = lambda t: t[i, :cnt].reshape(args[n].shape)
        grads[n], deltas[n], new_m[n], new_v[n] = take(gs), take(ds), take(ms), take(vs)

    order = ["ln_in_g", "ln_in_b", "w_in", "conv_w", "conv_b", "dt_bias", "a_log", "d_skip", "ssd_norm_g",
             "q_norm_g", "w_q_up", "kv_norm_g", "w_kv_up", "w_mix_out", "ln1_g", "ln1_b", "w_mem_q", "w_mem_k",
             "w_mem_v", "w_mem_o", "ln2_g", "ln2_b", "w_up", "w_down", "ln3_g", "ln3_b"]
    assert sorted(order) == sorted(weights)
    return (loss, grad_x[None], *[grads[n] for n in order], *[deltas[n] for n in order],
            *[new_m[n] for n in order], *[new_v[n] for n in order])


PACK_A_ROW = {"w_down": 0, "w_up": 1024, "w_mem_q": 2048, "w_mem_k": 2304, "w_mem_v": 2560, "w_mem_o": 2816,
              "w_mix_out": 3072}
PACK_A_ORDER = list(PACK_A_ROW)
PACK_A_ROWS = 3328
PACK_B_ORDER = ["w_q_up", "w_kv_up", "conv_w"]
PACK_B_ROWS = 160


def _mesh_pos():
    return 2 * lax.axis_index("x") + lax.axis_index("y"), lax.axis_index("c")


def _unpack_group_b(g_c, g_b, own):
    me, _ = _mesh_pos()
    g_c = lax.dynamic_update_slice(g_c, own[0][None], (me, 0, 0))
    g_b = lax.dynamic_update_slice(g_b, own[1][None], (me, 0, 0)).reshape(N_SHARD, -1)
    WB, off = {"w_in": g_c}, 0
    for n in PACK_B_ORDER:
        sr, sc = _shard_shape(n)
        cnt = sr * sc
        if n == "conv_w":
            part = lax.bitcast_convert_type(g_b[:, off:off + 2 * cnt].reshape(N_SHARD, cnt, 2), F32)
            off += 2 * cnt
        else:
            part = g_b[:, off:off + cnt]
            off += cnt
        WB[n] = _join_shards(n, part)
    return WB


def _local_step(x, mem, positions, target, WB, P, *, wp_a=None, g_a=None, wp_b=None):
    S = x.shape[0]
    tr = ROW_TILE
    dist = g_a is None
    g_in, b_in = _row(P["ln_in_g"]), _row(P["ln_in_b"])
    res = _rowwise(_fn_ln, [x], [g_in, b_in], [D_MODEL, (D_MODEL, BF16)], tr=tr, name="ln_in",
                   hosted=_merge_steps([_gather_step(w) for w in wp_b]) if dist else None)
    h0, h0_b = res[0], res[1]
    if dist:
        WB = _unpack_group_b(res[2], res[3], wp_b)
    P = {**P, "conv_w": WB["conv_w"]}
    w_in_p = _pad_w_in(WB["w_in"])
    w_q_p = _pad_heads(WB["w_q_up"], MLA_QK)
    w_kv3 = WB["w_kv_up"].reshape(MLA_KV_RANK, MLA_HEADS, 128)
    w_k_p = _pad_heads(w_kv3[:, :, :64].reshape(MLA_KV_RANK, 512), 64)
    w_v_p = _pad_heads(w_kv3[:, :, 64:].reshape(MLA_KV_RANK, 512), 64)
    w_v_pt = w_v_p.T
    conv_w8 = jnp.pad(P["conv_w"].astype(F32), ((0, 4), (0, 0)))
    conv_b = _row(P["conv_b"])
    dt_b = _row(P["dt_bias"], 128)
    a_head = -jnp.exp(P["a_log"].reshape(-1).astype(F32))
    a_row = _row(a_head, 128)
    dexp = jnp.repeat(P["d_skip"].reshape(-1).astype(F32), 64).reshape(1, 512)
    g_ssd, g_q, g_kv = _row(P["ssd_norm_g"]), _row(P["q_norm_g"]), _row(P["kv_norm_g"])
    g_in, b_in = _row(P["ln_in_g"]), _row(P["ln_in_b"])
    g1, b1, g2, b2, g3, b3 = (_row(P[k]) for k in ("ln1_g", "ln1_b", "ln2_g", "ln2_b", "ln3_g", "ln3_b"))

    half = MLA_ROPE // 2
    inv_freq = jnp.power(ROPE_THETA, -jnp.arange(half, dtype=F32) / half)
    ang = positions.reshape(S, 1).astype(F32) * inv_freq
    cos, sin = jnp.cos(ang), jnp.sin(ang)
    zc = lambda n: jnp.zeros((S, n), F32)
    rope_a = jnp.concatenate([jnp.ones((S, 64), F32), cos, cos, zc(32)], axis=1)
    rope_b = jnp.concatenate([zc(80), sin, zc(32)], axis=1)
    rope_c = jnp.concatenate([zc(64), -sin, zc(48)], axis=1)

    proj = _mm(h0_b, w_in_p, form="nn", name="mm_in")
    conv_y, xbc, dt = _rowwise(
        _fn_conv_fwd, [(proj,) + SEG_XBC, ("prev", proj) + SEG_XBC, (proj,) + SEG_DT], [conv_w8, conv_b, dt_b],
        [1024, 1024, 128], tr=tr, name="conv_fwd")
    y_ssd, hs = _ssd_fwd(xbc, dt, a_row, name="ssd_fwd")
    (y_n,) = _rowwise(_fn_ssd_post, [y_ssd, (xbc, 0, 512), (proj,) + SEG_Z], [dexp, g_ssd], [(512, BF16)], tr=tr,
                      name="ssd_post")
    q_n, kv_n = _rowwise(_fn_mla_pre, [(proj,) + SEG_QLAT, (proj,) + SEG_KVLAT], [g_q, g_kv], [384, 256], tr=tr,
                         name="mla_pre")
    qp = _mm(q_n, w_q_p, form="nn", name="mm_q_up")
    kn = _mm(kv_n, w_k_p, form="nn", name="mm_k_up")
    v_nat = _mm(kv_n, w_v_p, form="nn", out_dtype=BF16, name="mm_v_up")
    v_t = _mm(w_v_pt, kv_n, form="nt", out_dtype=BF16, name="mm_v_up_t")
    q_rot, k_full = _rowwise(_fn_rope, [qp, kn, (proj,) + SEG_KR, rope_a, rope_b, rope_c], [],
                             [(1024, BF16), (1024, BF16)], tr=tr, name="rope")
    res = _attn_fwd(q_rot, k_full, v_t, name="attn_fwd", hosted=_gather_step(wp_a) if dist else None)
    o_t, lse = res[0], res[1]
    if dist:
        g_a = lax.dynamic_update_slice(res[2], wp_a[None], (_mesh_pos()[0], 0, 0))
    r_mix = PACK_A_ROW["w_mix_out"]
    w_mix_o = jnp.pad(g_a[2:4, r_mix:r_mix + 256].reshape(MLA_HEADS, 64, D_MODEL),
                      ((0, 0), (0, 64), (0, 0))).reshape(MLA_HEADS * 128, D_MODEL)
    mix_o = _mm(o_t, w_mix_o, form="tn", name="mm_mix_o")
    mix_y = _mm(y_n, g_a, form="nn", b_pack="w_mix_out", name="mm_mix_y")
    (h1,) = _rowwise(_fn_res2_ln, [h0, mix_o, mix_y], [g1, b1], [D_MODEL], tr=tr, name="ln1")
    qm = _mm(h1, g_a, form="nn", b_pack="w_mem_q", out_dtype=BF16, name="mm_mem_q")
    km = _mm(mem, g_a, form="nn", b_pack="w_mem_k", out_dtype=BF16, name="mm_mem_k")
    vm = _mm(mem, g_a, form="nn", b_pack="w_mem_v", out_dtype=BF16, name="mm_mem_v")
    (om,) = _rowwise(_fn_mem_fwd, [qm], [km, vm], [(D_MODEL, BF16)], tr=tr, name="mem_fwd")
    xa = _mm(om, g_a, form="nn", b_pack="w_mem_o", name="mm_mem_o")
    h2, h2_b = _rowwise(_fn_res_ln, [h1, xa], [g2, b2], [D_MODEL, (D_MODEL, BF16)], tr=tr, name="ln2")
    u = _mm(h2_b, g_a, form="nn", b_pack="w_up", out_dtype=BF16, name="mm_up")
    ff = _mm(u, g_a, form="nn", a_pro=_relu2, b_pack="w_down", name="mm_down")

    gp = lax.empty((N_SHARD, PACK_A_ROWS, PACK_COLS), F32)
    dt3, dt3_b, dg3, db3, loss = _rowwise(_fn_final, [h2, ff, target], [g3, b3], [D_MODEL, (D_MODEL, BF16)],
                                   [(1, D_MODEL), (1, D_MODEL), (1, 128)], tr=tr, name="ln3_loss")
    du = _mm(dt3_b, g_a, form="nt", b_pack="w_down", epi=(_epi_du, u), out_dtype=BF16, name="mm_down_dx")
    gp = _mm(u, dt3_b, form="tn", a_pro=_relu2, out_pack=("w_down", gp), name="mm_down_dw")
    gp = _mm(h2_b, du, form="tn", out_pack=("w_up", gp), name="mm_up_dw")
    dh2 = _mm(du, g_a, form="nt", b_pack="w_up", name="mm_up_dx")
    dt2, dg2, db2 = _rowwise(_fn_res_ln_bwd, [h1, xa, dt3, dh2], [g2], [D_MODEL], [(1, D_MODEL)] * 2, tr=tr,
                             name="ln2_bwd")
    dom = _mm(dt2, g_a, form="nt", b_pack="w_mem_o", out_dtype=BF16, name="mm_mem_o_dx")
    gp = _mm(om, dt2, form="tn", out_pack=("w_mem_o", gp), name="mm_mem_o_dw")
    dqm, dkm, dvm = _rowwise(_fn_mem_bwd, [qm, dom], [km, vm], [(D_MODEL, BF16)], [(256, D_MODEL)] * 2, tr=tr,
                             name="mem_bwd")
    gp = _mm(h1, dqm, form="tn", out_pack=("w_mem_q", gp), name="mm_mem_q_dw")
    gp = _mm(mem, dkm, form="tn", out_pack=("w_mem_k", gp), name="mm_mem_k_dw")
    gp = _mm(mem, dvm, form="tn", out_pack=("w_mem_v", gp), name="mm_mem_v_dw")
    dh1 = _mm(dqm, g_a, form="nt", b_pack="w_mem_q", name="mm_mem_q_dx")
    dt1, dg1, db1 = _rowwise(_fn_res2_ln_bwd, [h0, mix_o, mix_y, dt2, dh1], [g1], [D_MODEL], [(1, D_MODEL)] * 2,
                             tr=tr, name="ln1_bwd")
    do_t = _mm(w_mix_o, dt1, form="nt", name="mm_mix_o_dx")
    dy_n = _mm(dt1, g_a, form="nt", b_pack="w_mix_out", b_rows=512, name="mm_mix_y_dx")
    dw_mix_o = _mm(o_t, dt1, form="nn", name="mm_mix_o_dw")
    gp = _mm(y_n, dt1, form="tn", out_pack=("w_mix_out", gp), name="mm_mix_y_dw")
    gp = lax.dynamic_update_slice(
        gp, dw_mix_o.reshape(MLA_HEADS, 128, D_MODEL)[:, :64].reshape(2, 256, D_MODEL), (2, r_mix, 0))
    me, c = _mesh_pos() if dist else (0, 0)
    ha = PACK_A_ROWS // 2
    res = _attn_bwd_dq(q_rot, k_full, v_nat, o_t, do_t, lse, name="attn_bwd_dq",
                       hosted=_pair_exchange_step(gp) if dist else None)
    dq_rot, delta = res[0], res[1]
    chip_step = None
    if dist:
        theirs_a = res[2]
        chip_step = _chip_exchange_step(_pair_sum(gp, theirs_a, "pair_sum_a"))
    res = _attn_bwd_dkv(q_rot, k_full, v_nat, do_t, lse, delta, name="attn_bwd_dkv", hosted=chip_step)
    dk, dv_t = res[0], res[1]
    if dist:
        gp = _chip_sum(gp, theirs_a, res[2], "chip_sum_a")
    dqp, dkr = _rowwise(_fn_rope_bwd, [dq_rot, dk, rope_a, rope_b, rope_c], [], [(1024, BF16), (128, BF16)], tr=tr,
                        name="rope_bwd")
    dw_q_p = _mm(q_n, dqp, form="tn", name="mm_q_up_dw")
    dq_n = _mm(dqp, w_q_p, form="nt", name="mm_q_up_dx")
    dw_k_p = _mm(kv_n, dk, form="tn", name="mm_k_up_dw")
    dkv_n1 = _mm(dk, w_k_p, form="nt", name="mm_k_up_dx")
    dw_v_pt = _mm(dv_t, kv_n, form="nn", name="mm_v_up_dw")
    dkv_n2 = _mm(dv_t, w_v_pt, form="tn", name="mm_v_up_dx")
    dq_lat, dkv_lat, dg_q, dg_kv = _rowwise(
        _fn_mla_pre_bwd, [(proj,) + SEG_QLAT, (proj,) + SEG_KVLAT, dq_n, dkv_n1, dkv_n2], [g_q, g_kv], [(384, BF16), (256, BF16)],
        [(1, 384), (1, 256)], tr=tr, name="mla_pre_bwd")
    dy_ssd, dz, dxs_skip, ddexp, dg_ssd = _rowwise(
        _fn_ssd_post_bwd, [dy_n, y_ssd, (xbc, 0, 512), (proj,) + SEG_Z], [dexp, g_ssd],
        [512, (512, BF16), 512], [(1, 512)] * 2, tr=tr, name="ssd_post_bwd")
    dxs, dbc, ddt, da_head = _ssd_bwd(xbc, dt, a_row, hs, dy_ssd, name="ssd_bwd")
    dyc, ddtr, dconv_b, ddt_b = _rowwise(
        _fn_conv_bwd_a, [conv_y, dxs, dxs_skip, dbc, (proj,) + SEG_DT, ddt], [dt_b], [1024, (128, BF16)],
        [(1, 1024), (1, 128)], tr=tr, name="conv_bwd_a")
    dxbc, dconv_w8 = _rowwise(
        _fn_conv_bwd_b, [dyc, ("next", dyc, 0, 1024), (proj,) + SEG_XBC, ("prev", proj) + SEG_XBC], [conv_w8], [(1024, BF16)],
        [(8, 1024)], tr=tr, name="conv_bwd_b")
    dproj = jnp.concatenate([dxbc, dz, dq_lat, ddtr, dkv_lat, dkr, jnp.zeros((S, 128), BF16)], axis=1)
    res = _mm(h0_b, dproj, form="tn", name="mm_in_dw", hosted=_pair_fill_step(gp) if dist else None)
    dw_in_p, red_a = (res[0], res[1]) if dist else (res, None)
    big_b = _group_b_grads(dw_in_p, dw_q_p, dw_k_p, dw_v_pt, dconv_w8)
    q_b = None
    if dist:
        gp_c, gp_b = _pack_group_b(big_b)
        dh0, theirs_c, theirs_b = _mm(dproj, w_in_p, form="nt", name="mm_in_dx", hosted=_merge_steps(
            [_pair_exchange_step(gp_c), _pair_exchange_step(gp_b)]))
        q_b = ((gp_c, theirs_c, _pair_sum(gp_c, theirs_c, "pair_sum_w_in")),
               (gp_b, theirs_b, _pair_sum(gp_b, theirs_b, "pair_sum_b")))
        gp = red_a
    else:
        dh0 = _mm(dproj, w_in_p, form="nt", name="mm_in_dx")
    grad_x, dg_in, db_in = _rowwise(_fn_in_ln_bwd, [x, dt1, dh0], [g_in], [D_MODEL], [(1, D_MODEL)] * 2, tr=tr,
                                    name="ln_in_bwd")

    small = {
        "ln_in_g": dg_in, "ln_in_b": db_in, "conv_b": dconv_b, "dt_bias": ddt_b[:, :8],
        "a_log": da_head[:, :8] * a_head.reshape(1, 8),
        "d_skip": ddexp.reshape(8, 64).sum(axis=1).reshape(1, 8),
        "ssd_norm_g": dg_ssd, "q_norm_g": dg_q, "kv_norm_g": dg_kv,
        "ln1_g": dg1, "ln1_b": db1, "ln2_g": dg2, "ln2_b": db2, "ln3_g": dg3, "ln3_b": db3,
    }
    return loss[0, 0], grad_x, (gp, q_b), big_b, small


def _group_b_grads(dw_in_p, dw_q_p, dw_k_p, dw_v_pt, dconv_w8):
    return {
        "w_in": _unpad_w_in(dw_in_p),
        "w_q_up": dw_q_p.reshape(384, MLA_HEADS, 128)[:, :, :MLA_QK].reshape(384, MLA_HEADS * MLA_QK),
        "w_kv_up": jnp.concatenate([dw_k_p.reshape(MLA_KV_RANK, MLA_HEADS, 128)[:, :, :64],
                                    dw_v_pt.T.reshape(MLA_KV_RANK, MLA_HEADS, 128)[:, :, :64]], axis=2).reshape(
                                        MLA_KV_RANK, MLA_HEADS * 128),
        "conv_w": dconv_w8[0:4],
    }


def _pack_group_b(big_b):
    gflat = [_split_shards(n, big_b[n]) for n in PACK_B_ORDER]
    used = sum(f.shape[1] for f in gflat)
    gflat.append(jnp.zeros((N_SHARD, PACK_B_ROWS * PACK_COLS - used), F32))
    return big_b["w_in"], jnp.concatenate(gflat, axis=1).reshape(N_SHARD, PACK_B_ROWS, PACK_COLS)


def _half_tile(h):
    return next(t for t in range(512, 0, -16) if h % t == 0)


def _pair_sum(gp, theirs, name):
    n, R, C = gp.shape
    H = R // 2
    tr = _half_tile(H)
    nb = H // tr

    def body(s_ref, g_ref, t_ref, o_ref):
        o_ref[...] = (g_ref[...] + t_ref[...]).astype(o_ref.dtype)

    def shard(k, s):
        return k + (k >= s[1]).astype(jnp.int32)

    me, c = _mesh_pos()
    return pl.pallas_call(
        body, name=name,
        grid_spec=pltpu.PrefetchScalarGridSpec(
            num_scalar_prefetch=1, grid=(n - 1, nb),
            in_specs=[pl.BlockSpec((1, tr, C), lambda k, i, s: (shard(k, s), s[0] * nb + i, 0)),
                      pl.BlockSpec((1, tr, C), lambda k, i, s: (shard(k, s), i, 0))],
            out_specs=pl.BlockSpec((1, tr, C), lambda k, i, s: (shard(k, s), i, 0))),
        out_shape=jax.ShapeDtypeStruct((n, H, C), BF16), compiler_params=_params(("arbitrary", "arbitrary")),
    )(jnp.stack([c, me]).astype(jnp.int32), gp, theirs)


def _chip_sum(gp, theirs, got, name):
    n, R, C = gp.shape
    H = R // 2
    tr = _half_tile(H)
    nb = H // tr

    def body(s_ref, g_ref, t_ref, r_ref, o_ref):
        acc = g_ref[0] + t_ref[0]
        for j in range(3):
            acc = acc + r_ref[j].astype(F32)
        o_ref[...] = acc

    me, c = _mesh_pos()
    return pl.pallas_call(
        body, name=name,
        grid_spec=pltpu.PrefetchScalarGridSpec(
            num_scalar_prefetch=1, grid=(nb,),
            in_specs=[pl.BlockSpec((1, tr, C), lambda i, s: (s[0], s[1] * nb + i, 0)),
                      pl.BlockSpec((1, tr, C), lambda i, s: (s[0], i, 0)),
                      pl.BlockSpec((3, tr, C), lambda i, s: (0, i, 0))],
            out_specs=pl.BlockSpec((tr, C), lambda i, s: (s[1] * nb + i, 0))),
        out_shape=jax.ShapeDtypeStruct((R, C), F32), compiler_params=_params(("arbitrary",)),
    )(jnp.stack([me, c]).astype(jnp.int32), gp, theirs, got)


def _pair_fill_step(red):
    R, C = red.shape
    H = R // 2
    nq = _chunks(H, 8)
    CH = H // nq

    def copies(ins, outs, sems):
        x, y, c, _ = _place()
        return [_remote(ins[0].at[pl.ds(c * H + j * CH, CH), :], outs[0].at[pl.ds(c * H + j * CH, CH), :], sems, j,
                        (x, y, 1 - c)) for j in range(nq)]

    def start(ins, outs, sems):
        for cp in copies(ins, outs, sems):
            cp.start()

    def finish(ins, outs, sems):
        for cp in copies(ins, outs, sems):
            cp.wait()

    step = _Step([red], [jax.ShapeDtypeStruct((R, C), red.dtype)], nq, start, finish)
    step.alias = [(0, 0)]
    return step


def _reduce_scatter(gp, tag):
    n, R, C = gp.shape
    H = R // 2
    me, c = _mesh_pos()
    (theirs,) = _run_step(_pair_exchange_step(gp), "pair_exchange_" + tag)
    mine = lax.dynamic_slice(gp, (0, c * H, 0), (n, H, C))
    pf, pb = _rowwise(_fn_add2, [mine.reshape(-1, C), theirs.reshape(-1, C)], [], [C, (C, BF16)], tr=512,
                      name="pair_sum_" + tag)
    (got,) = _run_step(_chip_exchange_step(pb.reshape(n, H, C)), "chip_exchange_" + tag)
    own = lax.dynamic_index_in_dim(pf.reshape(n, H, C), me, axis=0, keepdims=False)
    got = got.reshape(3 * H, C)
    (q,) = _rowwise(_fn_add4, [own] + [(got, 0, C, j * H) for j in range(3)], [], [C], tr=512,
                    name="chip_sum_" + tag, n_rows=H)
    return q


def _adam(w, g, m, v, name):
    shape = w.shape
    w2, m2, v2 = (t.reshape(-1, shape[-1]) for t in (w, m, v))
    g2 = (g[0], 0, shape[-1], g[1]) if isinstance(g, tuple) else g.reshape(-1, shape[-1])
    d, mn, vn = _rowwise(_fn_adam, [w2, g2, m2, v2], [], [shape[-1]] * 3, tr=256, name=name)
    return d.reshape(shape), mn.reshape(shape), vn.reshape(shape)


def kernel(x, mem, positions, ln_in_g, ln_in_b, w_in, conv_w, conv_b, dt_bias, a_log, d_skip, ssd_norm_g, q_norm_g, w_q_up, kv_norm_g, w_kv_up, w_mix_out, ln1_g, ln1_b, w_mem_q, w_mem_k, w_mem_v, w_mem_o, ln2_g, ln2_b, w_up, w_down, ln3_g, ln3_b, loss_target, m_ln_in_g, m_ln_in_b, m_w_in, m_conv_w, m_conv_b, m_dt_bias, m_a_log, m_d_skip, m_ssd_norm_g, m_q_norm_g, m_w_q_up, m_kv_norm_g, m_w_kv_up, m_w_mix_out, m_ln1_g, m_ln1_b, m_w_mem_q, m_w_mem_k, m_w_mem_v, m_w_mem_o, m_ln2_g, m_ln2_b, m_w_up, m_w_down, m_ln3_g, m_ln3_b, v_ln_in_g, v_ln_in_b, v_w_in, v_conv_w, v_conv_b, v_dt_bias, v_a_log, v_d_skip, v_ssd_norm_g, v_q_norm_g, v_w_q_up, v_kv_norm_g, v_w_kv_up, v_w_mix_out, v_ln1_g, v_ln1_b, v_w_mem_q, v_w_mem_k, v_w_mem_v, v_w_mem_o, v_ln2_g, v_ln2_b, v_w_up, v_w_down, v_ln3_g, v_ln3_b):
    args = dict(locals())
    me, c = _mesh_pos()

    wp_a = jnp.concatenate([args[n].reshape(-1, PACK_COLS).astype(BF16) for n in PACK_A_ORDER], axis=0)
    flat = [args[n].reshape(-1).astype(BF16) for n in PACK_B_ORDER[:-1]]
    flat.append(lax.bitcast_convert_type(conv_w.reshape(-1), BF16).reshape(-1))
    used = sum(f.shape[0] for f in flat)
    flat.append(jnp.zeros((PACK_B_ROWS * PACK_COLS - used,), BF16))
    wp_b = jnp.concatenate(flat).reshape(PACK_B_ROWS, PACK_COLS)

    wp_c = w_in[0].astype(BF16)

    P = {n: args[n] for n in SMALL_ORDER}
    loss, grad_x, (red_a, ((gp_c, theirs_c, pb_c), (gp_b, theirs_b, pb_b))), _, gsmall = _local_step(
        x[0], mem[0], positions[0], loss_target[0], None, P, wp_a=wp_a, wp_b=(wp_c, wp_b))

    gs = jnp.concatenate([_row(gsmall[n], PACK_COLS) for n in SMALL_ORDER] + [_row(loss, PACK_COLS)], axis=0)
    gs, got_c, got_b = _small_all_reduce(gs, _merge_steps([_chip_exchange_step(pb_c), _chip_exchange_step(pb_b)]))
    loss = gs[len(SMALL_ORDER), 0]
    red_c, red_b = _run_step(_merge_steps([_pair_fill_step(_chip_sum(gp_c, theirs_c, got_c, "chip_sum_w_in")),
                                           _pair_fill_step(_chip_sum(gp_b, theirs_b, got_b, "chip_sum_b"))]),
                             "pair_fill_b")

    grads, deltas, new_m, new_v = {}, {}, {}, {}
    for n in PACK_A_ORDER:
        r0, (sr, _) = PACK_A_ROW[n], _shard_shape(n)
        grads[n] = red_a[r0:r0 + sr].reshape(args[n].shape)
        deltas[n], new_m[n], new_v[n] = _adam(args[n], (red_a, r0), args["m_" + n], args["v_" + n], "adam_" + n)
    grads["w_in"] = red_c.reshape(w_in.shape)
    deltas["w_in"], new_m["w_in"], new_v["w_in"] = _adam(w_in, grads["w_in"], m_w_in, v_w_in, "adam_w_in")
    red_b = red_b.reshape(-1)
    off = 0
    for n in PACK_B_ORDER:
        sr, sc = _shard_shape(n)
        grads[n] = red_b[off:off + sr * sc].reshape(args[n].shape)
        off += sr * sc
        deltas[n], new_m[n], new_v[n] = _adam(args[n], grads[n], args["m_" + n], args["v_" + n], "adam_" + n)
    pack = lambda pre: jnp.concatenate([_row(args[pre + n], PACK_COLS) for n in SMALL_ORDER]
                                       + [jnp.zeros((1, PACK_COLS), F32)], axis=0)
    ds, ms, vs = _rowwise(_fn_adam, [pack(""), gs, pack("m_"), pack("v_")], [], [PACK_COLS] * 3, tr=16,
                          name="adam_small")
    for i, n in enumerate(SMALL_ORDER):
        cnt = args[n].size
        take = lambda t: t[i, :cnt].reshape(args[n].shape)
        grads[n], deltas[n], new_m[n], new_v[n] = take(gs), take(ds), take(ms), take(vs)

    order = ["ln_in_g", "ln_in_b", "w_in", "conv_w", "conv_b", "dt_bias", "a_log", "d_skip", "ssd_norm_g",
             "q_norm_g", "w_q_up", "kv_norm_g", "w_kv_up", "w_mix_out", "ln1_g", "ln1_b", "w_mem_q", "w_mem_k",
             "w_mem_v", "w_mem_o", "ln2_g", "ln2_b", "w_up", "w_down", "ln3_g", "ln3_b"]
    return (loss, grad_x[None], *[grads[n] for n in order], *[deltas[n] for n in order],
            *[new_m[n] for n in order], *[new_v[n] for n in order])
```

```python
import functools
import math

import jax
import jax.numpy as jnp
import numpy as np
from jax import lax
from jax.experimental import pallas as pl
from jax.experimental.pallas import tpu as pltpu

F32 = jnp.float32
BF16 = jnp.bfloat16
MESH = pl.DeviceIdType.MESH

D_MODEL = 1024
SSD_HEADS = 8
SSD_INNER = 512
SSD_CHUNK = 128
SSD_STATE = 128
MLA_HEADS = 8
MLA_NOPE = 64
MLA_ROPE = 32
MLA_QK = 96
MLA_Q_RANK = 384
MLA_KV_RANK = 256
ROPE_THETA = 10000.0
MEM_HEADS = 4
MEM_HEAD_DIM = 256
LN_EPS = 1e-5
RMS_EPS = 1e-6
ALPHA = 2.0 ** 0.25
ADAM_LR = 0.001
ADAM_B1 = 0.9
ADAM_B2 = 0.999
ADAM_EPS = 1e-08
ADAM_WD = 0.01
ADAM_STEP = 10

LANES = 128
IN_W = 2560
SEG_XBC = (0, 1024)
SEG_Z = (1024, 512)
SEG_QLAT = (1536, 384)
SEG_DT = (1920, 128)
SEG_KVLAT = (2048, 256)
SEG_KR = (2304, 128)
VMEM_LIMIT = 56 * 1024 * 1024
ATTN_TILE = 512
ROW_TILE = 512
SSD_PER_STEP = 2
NEG = -1e30

NN = (((1,), (0,)), ((), ()))
NT = (((1,), (1,)), ((), ()))
TN = (((0,), (0,)), ((), ()))


def _dot(a, b, dims=NN):
    return lax.dot_general(a.astype(BF16), b.astype(BF16), dims, preferred_element_type=F32)


def _dot_exact(a, b):
    return lax.dot_general(a, b, NN, precision=lax.Precision.HIGHEST, preferred_element_type=F32)


def _pick(dim, pref):
    t = min(pref, dim)
    t -= t % LANES
    while t >= LANES:
        if dim % t == 0:
            return t
        t -= LANES
    return dim


def _params(sem):
    return pltpu.CompilerParams(dimension_semantics=sem, vmem_limit_bytes=VMEM_LIMIT)


def _pack_caps(wname):
    r, c, ax = BIG[wname]
    if ax == 0:
        return (r if r <= 1024 else r // N_SHARD), c
    return r, c // N_SHARD


def _pack_block(wname, br, bc):
    r, c, ax = BIG[wname]
    r0 = PACK_A_ROW[wname]
    sr = r // N_SHARD if ax == 0 else r
    if ax == 0 and br > sr:
        assert br % sr == 0 and r0 % sr == 0
        return (br // sr, sr, bc), lambda rb, cb: (rb, r0 // sr, cb)
    assert r0 % br == 0
    if ax == 0:
        per = sr // br
        return (1, br, bc), lambda rb, cb: (rb // per, r0 // br + rb % per, cb)
    per = (c // N_SHARD) // bc
    return (1, br, bc), lambda rb, cb: (cb // per, r0 // br + rb, cb % per)


def _mm(a, b, *, form, name, a_pro=None, epi=None, out_dtype=F32, tm=1024, tn=1024, tk=1024, b_pack=None,
        b_rows=None, out_pack=None, hosted=None):
    b_shape = BIG[b_pack][:2] if b_pack else b.shape
    if b_pack and form == "nt":
        b_shape = (b_rows or b_shape[0], b_shape[1])
    if form == "nn":
        (m, k), (_, n) = a.shape, b_shape
    elif form == "nt":
        (m, k), (n, _) = a.shape, b_shape
    else:
        (k, m), (_, n) = a.shape, b_shape
    if b_pack:
        rcap, ccap = _pack_caps(b_pack)
        tk, tn = (min(tk, rcap), min(tn, ccap)) if form == "nn" else (min(tk, ccap), min(tn, rcap))
    if out_pack:
        rcap, ccap = _pack_caps(out_pack[0])
        tm, tn = min(tm, rcap), min(tn, ccap)
    tm, tn, tk = _pick(m, tm), _pick(n, tn), _pick(k, tk)
    dims = {"nn": NN, "nt": NT, "tn": TN}[form]
    nk = k // tk
    direct = out_dtype == F32 and epi is None
    n_extra = (1 if epi else 0) + (1 if out_pack else 0)

    def body(a_ref, b_ref, *rest):
        o_ref = rest[n_extra]
        acc_ref = o_ref if direct else rest[-1]

        @pl.when(pl.program_id(2) == 0)
        def _():
            acc_ref[...] = jnp.zeros_like(acc_ref)

        av = a_ref[...]
        if a_pro is not None:
            av = a_pro(av)
        bv = b_ref[...]
        acc_ref[...] += _dot(av, bv.reshape(-1, bv.shape[-1]), dims).reshape(acc_ref.shape)
        if not direct:
            @pl.when(pl.program_id(2) == nk - 1)
            def _():
                val = acc_ref[...]
                if epi is not None:
                    val = epi[0](val, rest[0][...])
                o_ref[...] = val.reshape(o_ref.shape).astype(o_ref.dtype)

    if form == "tn":
        a_spec = pl.BlockSpec((tk, tm), lambda i, j, kk: (kk, i))
    else:
        a_spec = pl.BlockSpec((tm, tk), lambda i, j, kk: (i, kk))
    if b_pack:
        shape, idx = _pack_block(b_pack, *((tk, tn) if form == "nn" else (tn, tk)))
        b_spec = pl.BlockSpec(shape, (lambda i, j, kk: idx(kk, j)) if form == "nn" else (lambda i, j, kk: idx(j, kk)))
    elif form == "nt":
        b_spec = pl.BlockSpec((tn, tk), lambda i, j, kk: (j, kk))
    else:
        b_spec = pl.BlockSpec((tk, tn), lambda i, j, kk: (kk, j))
    in_specs, args = [a_spec, b_spec], [a, b]
    out_spec = pl.BlockSpec((tm, tn), lambda i, j, kk: (i, j))
    out_sds, aliases = jax.ShapeDtypeStruct((m, n), out_dtype), {}
    if epi is not None:
        in_specs.append(out_spec)
        args.append(epi[1])
    if out_pack:
        wname, buf = out_pack
        shape, idx = _pack_block(wname, tm, tn)
        out_spec = pl.BlockSpec(shape, lambda i, j, kk: idx(i, j))
        out_sds, aliases = jax.ShapeDtypeStruct(buf.shape, buf.dtype), {len(args): 0}
        in_specs.append(HBM)
        args.append(buf)
    acc_shape = out_spec.block_shape if out_pack else (tm, tn)
    res = _call_with_step(
        body, hosted, None, args, name=name, grid=(m // tm, n // tn, nk), in_specs=in_specs, out_specs=[out_spec],
        out_shape=[out_sds], sem=("parallel", "parallel", "arbitrary"), aliases=aliases,
        scratch_shapes=[] if direct else [pltpu.VMEM(acc_shape, F32)])
    return res[0] if hosted is None else res


class _Ctx:
    def __init__(self, i, n):
        self.i, self.n = i, n


def _rowwise(fn, rows, consts, row_outs, acc_outs=(), *, tr, name, n_rows=None, hosted=None):
    norm = []
    for r in rows:
        kind = "tile"
        if isinstance(r, tuple) and isinstance(r[0], str):
            kind, r = r[0], r[1:]
        row0 = 0
        if isinstance(r, tuple) and len(r) == 4:
            r, row0 = r[:3], r[3]
        arr, col0, width = r if isinstance(r, tuple) else (r, 0, r.shape[1])
        assert col0 % width == 0
        norm.append((kind, arr, col0 // width, width, row0))
    n_rows = n_rows or next(a.shape[0] for k, a, _, _, _ in norm if k == "tile")
    tr = min(tr, n_rows)
    while n_rows % tr:
        tr -= 8
    n = n_rows // tr
    arrs, specs = [], []
    for kind, arr, cb, width, row0 in norm:
        if kind == "tile":
            assert row0 % tr == 0
            specs.append(pl.BlockSpec((tr, width), lambda i, cb=cb, rb=row0 // tr: (i + rb, cb)))
        elif kind == "prev":
            specs.append(pl.BlockSpec((8, width), lambda i, cb=cb: (jnp.maximum(i * (tr // 8) - 1, 0), cb)))
        else:
            specs.append(pl.BlockSpec((8, width), lambda i, cb=cb: (jnp.minimum((i + 1) * (tr // 8), n_rows // 8 - 1), cb)))
        arrs.append(arr)
    for c in consts:
        specs.append(pl.BlockSpec(c.shape, lambda i, nd=c.ndim: (0,) * nd))
        arrs.append(c)
    n_in, n_ro = len(arrs), len(row_outs)
    row_outs = [w if isinstance(w, tuple) else (w, F32) for w in row_outs]
    out_shape = [jax.ShapeDtypeStruct((n_rows, w), dt) for w, dt in row_outs]
    out_specs = [pl.BlockSpec((tr, w), lambda i: (i, 0)) for w, _ in row_outs]
    out_shape += [jax.ShapeDtypeStruct(s, F32) for s in acc_outs]
    out_specs += [pl.BlockSpec(s, lambda i: (0, 0)) for s in acc_outs]

    def body(*refs):
        i = pl.program_id(0)
        vals = [r[...] for r in refs[:n_in]]
        outs = fn(_Ctx(i, n), *vals)
        if not isinstance(outs, (tuple, list)):
            outs = (outs,)
        o_refs = refs[n_in:]
        for o_ref, o in zip(o_refs[:n_ro], outs[:n_ro]):
            o_ref[...] = o.astype(o_ref.dtype)
        if acc_outs:
            @pl.when(i == 0)
            def _():
                for o_ref in o_refs[n_ro:]:
                    o_ref[...] = jnp.zeros_like(o_ref)

            for o_ref, o in zip(o_refs[n_ro:], outs[n_ro:]):
                o_ref[...] += jnp.broadcast_to(o, o_ref.shape)

    return _call_with_step(body, hosted, None, arrs, name=name, grid=(n,), in_specs=specs, out_specs=out_specs,
                           out_shape=out_shape, sem=("arbitrary",))


def _sum0(v):
    return jnp.sum(v, axis=0, keepdims=True)


def _mean1(v):
    return jnp.mean(v, axis=-1, keepdims=True)


def _sigmoid(v):
    return 1.0 / (1.0 + jnp.exp(-v))


def _ln_stats(t):
    xc = t - _mean1(t)
    rstd = lax.rsqrt(_mean1(xc * xc) + LN_EPS)
    return xc * rstd, rstd


def _ln_bwd(xhat, rstd, dy, g):
    dxh = dy * g
    dx = rstd * (dxh - _mean1(dxh) - xhat * _mean1(dxh * xhat))
    return dx, _sum0(dy * xhat), _sum0(dy)


def _rms_fwd(v, g):
    return v * lax.rsqrt(_mean1(v * v) + RMS_EPS) * g


def _rms_bwd(v, dy, g):
    rs = lax.rsqrt(_mean1(v * v) + RMS_EPS)
    vh = v * rs
    dyg = dy * g
    return rs * (dyg - vh * _mean1(dyg * vh)), _sum0(dy * vh)


def _lane(shape):
    return lax.broadcasted_iota(jnp.int32, shape, len(shape) - 1)


def _shift_down(u, halo, s, is_first):
    tr = u.shape[0]
    rolled = pltpu.roll(u, s, 0)
    hr = jnp.where(is_first, 0.0, pltpu.roll(halo, s, 0))
    row = lax.broadcasted_iota(jnp.int32, hr.shape, 0)
    top = jnp.where(row < s, hr, rolled[0:8])
    if tr == 8:
        return top
    return jnp.concatenate([top, rolled[8:]], axis=0)


def _shift_up(d, halo, s, is_last):
    tr = d.shape[0]
    rolled = pltpu.roll(d, tr - s, 0)
    hr = jnp.where(is_last, 0.0, pltpu.roll(halo, 8 - s, 0))
    row = lax.broadcasted_iota(jnp.int32, hr.shape, 0)
    bot = jnp.where(row >= 8 - s, hr, rolled[tr - 8:])
    if tr == 8:
        return bot
    return jnp.concatenate([rolled[:tr - 8], bot], axis=0)


def _rope(v, ta, tb, tc):
    return v * ta + pltpu.roll(v, 16, 1) * tb + pltpu.roll(v, LANES - 16, 1) * tc


def _rope_bwd(d, ta, tb, tc):
    return d * ta + pltpu.roll(d * tb, LANES - 16, 1) + pltpu.roll(d * tc, 16, 1)


def _ssd_common(dtv, a_row):
    L = SSD_CHUNK
    a = dtv * a_row
    r = lax.broadcasted_iota(jnp.int32, (L, L), 0)
    c = lax.broadcasted_iota(jnp.int32, (L, L), 1)
    tril = r >= c
    cs = _dot_exact(tril.astype(F32), a)
    cs_t = cs.T
    cs_last = cs[L - 1:L, :]
    return dict(a=a, tril=tril, cs=cs, cs_t=cs_t, ecs=jnp.exp(cs), dte=jnp.exp(cs_last - cs),
                elast=jnp.exp(cs_last))


def _pair_sel(v, h0, lo):
    return jnp.where(lo, v[:, h0:h0 + 1], v[:, h0 + 1:h0 + 2])


def _ssd_pair(cm, h0, cb, xp, dtv, bmat, cmat, hp, lo):
    L = SSD_CHUNK
    x = xp * _pair_sel(dtv, h0, lo)
    lam0 = jnp.exp(jnp.where(cm["tril"], cm["cs"][:, h0:h0 + 1] - cm["cs_t"][h0:h0 + 1, :], NEG))
    lam1 = jnp.exp(jnp.where(cm["tril"], cm["cs"][:, h0 + 1:h0 + 2] - cm["cs_t"][h0 + 1:h0 + 2, :], NEG))
    m0, m1 = cb * lam0, cb * lam1
    ydiag = jnp.where(lo, _dot(m0, x), _dot(m1, x))
    ecs_p = _pair_sel(cm["ecs"], h0, lo)
    dte_p = _pair_sel(cm["dte"], h0, lo)
    yoff = _dot(cmat, hp, NT) * ecs_p
    xd = x * dte_p
    st = _dot(xd, bmat, TN)
    rlo = lax.broadcasted_iota(jnp.int32, (LANES, SSD_STATE), 0) < 64
    decay = jnp.where(rlo, cm["elast"][:, h0:h0 + 1], cm["elast"][:, h0 + 1:h0 + 2])
    h_next = hp * decay + st
    return dict(x=x, lam0=lam0, lam1=lam1, m0=m0, m1=m1, y=ydiag + yoff, yoff=yoff, ecs_p=ecs_p, dte_p=dte_p,
                xd=xd, decay=decay, h_next=h_next)


def _ssd_fwd(xbc, dt, a_row, *, name):
    S = xbc.shape[0]
    L = SSD_CHUNK
    nc = S // L
    per = SSD_PER_STEP if nc % SSD_PER_STEP == 0 else 1
    G = per * L

    def body(xs_ref, bm_ref, cm_ref, dt_ref, a_ref, y_ref, hs_ref, h_scr):
        @pl.when(pl.program_id(0) == 0)
        def _():
            h_scr[...] = jnp.zeros_like(h_scr)

        lo = _lane((L, LANES)) < 64
        for sub in range(per):
            rows = slice(sub * L, (sub + 1) * L)
            dtv = dt_ref[rows, :]
            cm = _ssd_common(dtv, a_ref[...])
            ys = []
            for g in range(2):
                bmat = bm_ref[rows, g * 128:(g + 1) * 128]
                cmat = cm_ref[rows, g * 128:(g + 1) * 128]
                cb = _dot(cmat, bmat, NT)
                for pr in range(2):
                    p4 = 2 * g + pr
                    hp = h_scr[p4]
                    hs_ref[sub, p4 * 128:(p4 + 1) * 128, :] = hp
                    t = _ssd_pair(cm, 2 * p4, cb, xs_ref[rows, p4 * 128:(p4 + 1) * 128], dtv, bmat, cmat, hp, lo)
                    ys.append(t["y"])
                    h_scr[p4] = t["h_next"]
            y_ref[rows, :] = jnp.concatenate(ys, axis=1)

    return pl.pallas_call(
        body, name=name, grid=(nc // per,),
        in_specs=[pl.BlockSpec((G, 512), lambda c: (c, 0)), pl.BlockSpec((G, 256), lambda c: (c, 2)),
                  pl.BlockSpec((G, 256), lambda c: (c, 3)), pl.BlockSpec((G, 128), lambda c: (c, 0)),
                  pl.BlockSpec((1, 128), lambda c: (0, 0))],
        out_specs=[pl.BlockSpec((G, 512), lambda c: (c, 0)), pl.BlockSpec((per, 512, 128), lambda c: (c, 0, 0))],
        out_shape=[jax.ShapeDtypeStruct((S, 512), F32), jax.ShapeDtypeStruct((nc, 512, 128), F32)],
        scratch_shapes=[pltpu.VMEM((4, 128, 128), F32)],
        compiler_params=_params(("arbitrary",)),
    )(xbc, xbc, xbc, dt, a_row)


def _ssd_bwd(xbc, dt, a_row, hs, dy, *, name):
    S = xbc.shape[0]
    L = SSD_CHUNK
    nc = S // L

    per = SSD_PER_STEP if nc % SSD_PER_STEP == 0 else 1
    G = per * L

    def body(xs_ref, bm_ref, cm_ref, dt_ref, a_ref, hs_ref, dy_ref, dxs_ref, dbc_ref, ddt_ref, da_ref, g_scr):
        @pl.when(pl.program_id(0) == 0)
        def _():
            g_scr[...] = jnp.zeros_like(g_scr)
            da_ref[...] = jnp.zeros_like(da_ref)

        for sub in reversed(range(per)):
            rows = pl.ds(sub * L, L)
            chunk(xs_ref.at[rows, :], bm_ref.at[rows, :], cm_ref.at[rows, :], dt_ref.at[rows, :], a_ref,
                  hs_ref.at[pl.ds(sub, 1)], dy_ref.at[rows, :], dxs_ref.at[rows, :], dbc_ref.at[rows, :],
                  ddt_ref.at[rows, :], da_ref, g_scr)

    def chunk(xs_ref, bm_ref, cm_ref, dt_ref, a_ref, hs_ref, dy_ref, dxs_ref, dbc_ref, ddt_ref, da_ref, g_scr):
        dtv = dt_ref[...]
        a_row_v = a_ref[...]
        cm = _ssd_common(dtv, a_row_v)
        lo = _lane((L, LANES)) < 64
        lane_row = _lane((1, LANES))
        ri = lax.broadcasted_iota(jnp.int32, (L, L), 0)
        ci = lax.broadcasted_iota(jnp.int32, (L, L), 1)
        triu = (ri <= ci).astype(F32)
        stril = ri > ci

        def halves(v, mask):
            return (jnp.sum(jnp.where(mask, v, 0.0), axis=1, keepdims=True),
                    jnp.sum(jnp.where(mask, 0.0, v), axis=1, keepdims=True))

        i_all = jnp.zeros((L, LANES), F32)
        yo_all = jnp.zeros((L, LANES), F32)
        w_all = jnp.zeros((L, LANES), F32)
        ddt_x = jnp.zeros((L, LANES), F32)
        e_row = jnp.zeros((1, LANES), F32)
        rlo = lax.broadcasted_iota(jnp.int32, (LANES, SSD_STATE), 0) < 64
        dxs, dbs, dcs = [], [], []
        for g in range(2):
            bmat = bm_ref[:, g * 128:(g + 1) * 128]
            cmat = cm_ref[:, g * 128:(g + 1) * 128]
            cb = _dot(cmat, bmat, NT)
            dcb = jnp.zeros((L, L), F32)
            db = jnp.zeros((L, SSD_STATE), F32)
            dc = jnp.zeros((L, SSD_STATE), F32)
            for pr in range(2):
                p4 = 2 * g + pr
                h0 = 2 * p4
                hp = hs_ref[0, p4 * 128:(p4 + 1) * 128, :]
                xp = xs_ref[:, p4 * 128:(p4 + 1) * 128]
                t = _ssd_pair(cm, h0, cb, xp, dtv, bmat, cmat, hp, lo)
                gst = g_scr[p4]
                dyp = dy_ref[:, p4 * 128:(p4 + 1) * 128]
                dy0 = jnp.where(lo, dyp, 0.0)
                dy1 = dyp - dy0
                bg = _dot(bmat, gst, NT)
                dx = _dot(t["m0"], dy0, TN) + _dot(t["m1"], dy1, TN) + bg * t["dte_p"]
                dm0, dm1 = _dot(dy0, t["x"], NT), _dot(dy1, t["x"], NT)
                dcb = dcb + dm0 * t["lam0"] + dm1 * t["lam1"]
                dye = dyp * t["ecs_p"]
                dc = dc + _dot(dye, hp)
                db = db + _dot(t["xd"], gst)
                i0 = jnp.sum(jnp.where(stril, _dot(triu, dm0 * t["m0"]), 0.0), axis=1, keepdims=True)
                i1 = jnp.sum(jnp.where(stril, _dot(triu, dm1 * t["m1"]), 0.0), axis=1, keepdims=True)
                yo0, yo1 = halves(dyp * t["yoff"], lo)
                w0, w1 = halves(t["xd"] * bg, lo)
                gh = gst * (hp * t["decay"])
                e0 = _sum0(jnp.sum(jnp.where(rlo, gh, 0.0), axis=1, keepdims=True))
                e1 = _sum0(jnp.sum(jnp.where(rlo, 0.0, gh), axis=1, keepdims=True))
                x0, x1 = halves(dx * xp, lo)
                oh0 = (lane_row == h0).astype(F32)
                oh1 = (lane_row == h0 + 1).astype(F32)
                i_all = i_all + i0 * oh0 + i1 * oh1
                yo_all = yo_all + yo0 * oh0 + yo1 * oh1
                w_all = w_all + w0 * oh0 + w1 * oh1
                e_row = e_row + e0 * oh0 + e1 * oh1
                ddt_x = ddt_x + x0 * oh0 + x1 * oh1
                dxs.append(dx * _pair_sel(dtv, h0, lo))
                g_scr[p4] = gst * t["decay"] + _dot(dye, cmat, TN)
            dbs.append(db + _dot(dcb, cmat, TN))
            dcs.append(dc + _dot(dcb, bmat))
        da = i_all + _dot_exact(triu, yo_all) + _dot_exact(stril.astype(F32), w_all) + e_row
        ddt_ref[...] = da * a_row_v + ddt_x
        da_ref[...] += _sum0(da * dtv)
        dxs_ref[...] = jnp.concatenate(dxs, axis=1)
        dbc_ref[...] = jnp.concatenate(dbs + dcs, axis=1)

    rev = lambda c: nc // per - 1 - c
    return pl.pallas_call(
        body, name=name, grid=(nc // per,),
        in_specs=[pl.BlockSpec((G, 512), lambda c: (rev(c), 0)), pl.BlockSpec((G, 256), lambda c: (rev(c), 2)),
                  pl.BlockSpec((G, 256), lambda c: (rev(c), 3)), pl.BlockSpec((G, 128), lambda c: (rev(c), 0)),
                  pl.BlockSpec((1, 128), lambda c: (0, 0)), pl.BlockSpec((per, 512, 128), lambda c: (rev(c), 0, 0)),
                  pl.BlockSpec((G, 512), lambda c: (rev(c), 0))],
        out_specs=[pl.BlockSpec((G, 512), lambda c: (rev(c), 0)), pl.BlockSpec((G, 512), lambda c: (rev(c), 0)),
                   pl.BlockSpec((G, 128), lambda c: (rev(c), 0)), pl.BlockSpec((1, 128), lambda c: (0, 0))],
        out_shape=[jax.ShapeDtypeStruct((S, 512), F32), jax.ShapeDtypeStruct((S, 512), F32),
                   jax.ShapeDtypeStruct((S, 128), F32), jax.ShapeDtypeStruct((1, 128), F32)],
        scratch_shapes=[pltpu.VMEM((4, 128, 128), F32)],
        compiler_params=_params(("arbitrary",)),
    )(xbc, xbc, xbc, dt, a_row, hs, dy)


MLA_SCALE = MLA_QK ** -0.5


def _causal_scores(q, k, qi, ki, t):
    s = _dot(q, k, NT) * MLA_SCALE
    row = qi * t + lax.broadcasted_iota(jnp.int32, (t, t), 0)
    col = ki * t + lax.broadcasted_iota(jnp.int32, (t, t), 1)
    return jnp.where(col <= row, s, NEG)


def _mla_fwd(q, k, kv, *, name):
    S = q.shape[0]
    t = min(ATTN_TILE, S)
    nq = S // t

    def body(q_ref, k_ref, v_ref, o_ref, lse_ref, m_scr, l_scr, acc_scr):
        qi, ki = pl.program_id(1), pl.program_id(2)

        @pl.when(ki == 0)
        def _():
            m_scr[...] = jnp.full_like(m_scr, NEG)
            l_scr[...] = jnp.zeros_like(l_scr)
            acc_scr[...] = jnp.zeros_like(acc_scr)

        @pl.when(ki <= qi)
        def _():
            s = _causal_scores(q_ref[...], k_ref[...], qi, ki, t)
            m_old = m_scr[:, 0:1]
            m_new = jnp.maximum(m_old, jnp.max(s, axis=1, keepdims=True))
            p = jnp.exp(s - m_new)
            corr = jnp.exp(m_old - m_new)
            l_scr[...] = jnp.broadcast_to(corr * l_scr[:, 0:1] + jnp.sum(p, axis=1, keepdims=True), l_scr.shape)
            acc_scr[...] = corr * acc_scr[...] + _dot(p, v_ref[...])
            m_scr[...] = jnp.broadcast_to(m_new, m_scr.shape)

        @pl.when(ki == nq - 1)
        def _():
            l = l_scr[:, 0:1]
            o_ref[...] = acc_scr[...] / l
            lse_ref[0] = jnp.broadcast_to(m_scr[:, 0:1] + jnp.log(l), (t, LANES))

    return pl.pallas_call(
        body, name=name, grid=(MLA_HEADS, nq, nq),
        in_specs=[pl.BlockSpec((t, 128), lambda h, qi, ki: (qi, h)),
                  pl.BlockSpec((t, 128), lambda h, qi, ki: (jnp.minimum(ki, qi), h)),
                  pl.BlockSpec((t, 128), lambda h, qi, ki: (jnp.minimum(ki, qi), 2 * h + 1))],
        out_specs=[pl.BlockSpec((t, 128), lambda h, qi, ki: (qi, h)),
                   pl.BlockSpec((1, t, 128), lambda h, qi, ki: (h, qi, 0))],
        out_shape=[jax.ShapeDtypeStruct((S, MLA_HEADS * 128), F32), jax.ShapeDtypeStruct((MLA_HEADS, S, 128), F32)],
        scratch_shapes=[pltpu.VMEM((t, 128), F32), pltpu.VMEM((t, 128), F32), pltpu.VMEM((t, 128), F32)],
        compiler_params=_params(("parallel", "parallel", "arbitrary")),
    )(q, k, kv)


def _mla_bwd_dkv(q, k, kv, o, do, lse, *, name):
    S = q.shape[0]
    t = min(ATTN_TILE, S)
    nq = S // t

    def body(q_ref, k_ref, v_ref, o_ref, do_ref, lse_ref, dkv_ref):
        ki, qi = pl.program_id(1), pl.program_id(2)

        @pl.when(qi == 0)
        def _():
            dkv_ref[...] = jnp.zeros_like(dkv_ref)

        @pl.when(qi >= ki)
        def _():
            qv, dov = q_ref[...], do_ref[...]
            s = _causal_scores(qv, k_ref[...], qi, ki, t)
            p = jnp.exp(s - lse_ref[0][:, 0:1])
            dv = _dot(p, dov, TN)
            dp = _dot(dov, v_ref[...], NT)
            delta = jnp.sum(dov * o_ref[...], axis=1, keepdims=True)
            ds = p * (dp - delta) * MLA_SCALE
            dkv_ref[...] += jnp.concatenate([_dot(ds, qv, TN), dv], axis=1)

    qmap = lambda h, ki, qi: (jnp.maximum(qi, ki), h)
    return pl.pallas_call(
        body, name=name, grid=(MLA_HEADS, nq, nq),
        in_specs=[pl.BlockSpec((t, 128), qmap),
                  pl.BlockSpec((t, 128), lambda h, ki, qi: (ki, h)),
                  pl.BlockSpec((t, 128), lambda h, ki, qi: (ki, 2 * h + 1)),
                  pl.BlockSpec((t, 128), qmap), pl.BlockSpec((t, 128), qmap),
                  pl.BlockSpec((1, t, 128), lambda h, ki, qi: (h, jnp.maximum(qi, ki), 0))],
        out_specs=pl.BlockSpec((t, 256), lambda h, ki, qi: (ki, h)),
        out_shape=jax.ShapeDtypeStruct((S, MLA_HEADS * 256), F32),
        compiler_params=_params(("parallel", "parallel", "arbitrary")),
    )(q, k, kv, o, do, lse)


def _mla_bwd_dq(q, k, kv, o, do, lse, *, name):
    S = q.shape[0]
    t = min(ATTN_TILE, S)
    nq = S // t

    def body(q_ref, k_ref, v_ref, o_ref, do_ref, lse_ref, dq_ref):
        qi, ki = pl.program_id(1), pl.program_id(2)

        @pl.when(ki == 0)
        def _():
            dq_ref[...] = jnp.zeros_like(dq_ref)

        @pl.when(ki <= qi)
        def _():
            dov, kv = do_ref[...], k_ref[...]
            s = _causal_scores(q_ref[...], kv, qi, ki, t)
            p = jnp.exp(s - lse_ref[0][:, 0:1])
            dp = _dot(dov, v_ref[...], NT)
            delta = jnp.sum(dov * o_ref[...], axis=1, keepdims=True)
            ds = p * (dp - delta) * MLA_SCALE
            dq_ref[...] += _dot(ds, kv)

    qmap = lambda h, qi, ki: (qi, h)
    return pl.pallas_call(
        body, name=name, grid=(MLA_HEADS, nq, nq),
        in_specs=[pl.BlockSpec((t, 128), qmap),
                  pl.BlockSpec((t, 128), lambda h, qi, ki: (jnp.minimum(ki, qi), h)),
                  pl.BlockSpec((t, 128), lambda h, qi, ki: (jnp.minimum(ki, qi), 2 * h + 1)),
                  pl.BlockSpec((t, 128), qmap), pl.BlockSpec((t, 128), qmap),
                  pl.BlockSpec((1, t, 128), lambda h, qi, ki: (h, qi, 0))],
        out_specs=pl.BlockSpec((t, 128), qmap),
        out_shape=jax.ShapeDtypeStruct((S, MLA_HEADS * 128), F32),
        compiler_params=_params(("parallel", "parallel", "arbitrary")),
    )(q, k, kv, o, do, lse)


HBM = pl.BlockSpec(memory_space=pl.ANY)


class _Step:
    def __init__(self, inputs, out_shapes, n_sems, start, finish, mid=None):
        self.inputs, self.out_shapes, self.n_sems = inputs, out_shapes, n_sems
        self.start, self.finish, self.mid = start, finish, mid
        self.alias = []


class _Shifted:
    def __init__(self, ref, off):
        self.ref, self.off = ref, off

    @property
    def at(self):
        return self

    def __getitem__(self, j):
        return self.ref.at[self.off + j]


def _merge_steps(steps):
    offs = [sum(s.n_sems for s in steps[:i]) for i in range(len(steps) + 1)]
    i_offs = [sum(len(s.inputs) for s in steps[:i]) for i in range(len(steps))]
    o_offs = [sum(len(s.out_shapes) for s in steps[:i]) for i in range(len(steps))]

    def phase(which):
        def run(ins, outs, sems):
            for s, off, i0, o0 in zip(steps, offs, i_offs, o_offs):
                fn = getattr(s, which)
                if fn is not None:
                    fn(ins[i0:i0 + len(s.inputs)], outs[o0:o0 + len(s.out_shapes)],
                       [_Shifted(sems[0], off), _Shifted(sems[1], off)])
        return run

    merged = _Step([a for s in steps for a in s.inputs], [o for s in steps for o in s.out_shapes], offs[-1],
                   phase("start"), phase("finish"), phase("mid") if any(s.mid for s in steps) else None)
    merged.alias = [(i0 + a, o0 + b) for s, i0, o0 in zip(steps, i_offs, o_offs) for a, b in s.alias]
    return merged


def _place():
    x, y, c = lax.axis_index("x"), lax.axis_index("y"), lax.axis_index("c")
    chips = [(1 - x, y), (x, 1 - y), (1 - x, 1 - y)]
    return x, y, c, chips


def _chunks(rows, tile):
    return next(n for n in (4, 3, 2, 1) if rows % (n * tile) == 0)


def _remote(src, dst, sems, j, to):
    return pltpu.make_async_remote_copy(src_ref=src, dst_ref=dst, send_sem=sems[0].at[j], recv_sem=sems[1].at[j],
                                        device_id=to, device_id_type=MESH)


def _gather_step(wp):
    R, C = wp.shape
    H = R // 2
    nq = _chunks(H, 16)
    CH = H // nq

    def copies(ins, outs, sems):
        x, y, c, chips = _place()
        sib, me = (x, y, 1 - c), 2 * x + y
        w_ref, out_ref = ins[0], outs[0]

        def piece(k, hc, q):
            return out_ref.at[k, pl.ds(hc * H + q * CH, CH), :]

        sends, landed, fwds, fwd_landed = [], [], [], []
        for q in range(nq):
            for j, (px, py) in enumerate(chips):
                k = 2 * px + py
                sends.append(_remote(w_ref.at[pl.ds(c * H + q * CH, CH), :], piece(me, c, q), sems, j * nq + q,
                                     (px, py, c)))
                landed.append(_remote(piece(k, c, q), piece(k, c, q), sems, j * nq + q, (px, py, c)))
                fwds.append(_remote(piece(k, c, q), piece(k, c, q), sems, (3 + j) * nq + q, sib))
                fwd_landed.append(_remote(piece(k, 1 - c, q), piece(k, 1 - c, q), sems, (3 + j) * nq + q, sib))
        return sends, landed, fwds, fwd_landed

    def start(ins, outs, sems):
        for cp in copies(ins, outs, sems)[0]:
            cp.start()

    def mid(ins, outs, sems):
        _, landed, fwds, _ = copies(ins, outs, sems)
        for arrived, onward in zip(landed, fwds):
            arrived.wait_recv()
            onward.start()

    def finish(ins, outs, sems):
        sends, _, fwds, fwd_landed = copies(ins, outs, sems)
        for cp in fwd_landed:
            cp.wait_recv()
        for cp in sends + fwds:
            cp.wait_send()

    return _Step([wp], [jax.ShapeDtypeStruct((N_SHARD, R, C), wp.dtype)], 6 * nq, start, finish, mid)


def _pair_exchange_step(gp):
    n, R, C = gp.shape
    H = R // 2
    nq = _chunks(H, 8)
    CH = H // nq

    def copies(ins, outs, sems):
        x, y, c, _ = _place()
        return [_remote(ins[0].at[k, pl.ds((1 - c) * H + q * CH, CH), :], outs[0].at[k, pl.ds(q * CH, CH), :], sems,
                        k * nq + q, (x, y, 1 - c)) for k in range(n) for q in range(nq)]

    def start(ins, outs, sems):
        for cp in copies(ins, outs, sems):
            cp.start()

    def finish(ins, outs, sems):
        for cp in copies(ins, outs, sems):
            cp.wait()

    return _Step([gp], [jax.ShapeDtypeStruct((n, H, C), gp.dtype)], n * nq, start, finish)


def _chip_exchange_step(pb):
    n, H, C = pb.shape
    nq = _chunks(H, 16)
    CH = H // nq

    def copies(ins, outs, sems):
        x, y, c, chips = _place()
        return [_remote(ins[0].at[2 * px + py, pl.ds(q * CH, CH), :], outs[0].at[j, pl.ds(q * CH, CH), :], sems,
                        j * nq + q, (px, py, c)) for q in range(nq) for j, (px, py) in enumerate(chips)]

    def start(ins, outs, sems):
        for cp in copies(ins, outs, sems):
            cp.start()

    def finish(ins, outs, sems):
        for cp in copies(ins, outs, sems):
            cp.wait()

    return _Step([pb], [jax.ShapeDtypeStruct((3, H, C), pb.dtype)], 3 * nq, start, finish)


def _pair_join_step(q):
    H, C = q.shape
    nq = _chunks(H, 8)
    CH = H // nq

    def copies(ins, outs, sems):
        x, y, c, _ = _place()
        return [_remote(ins[0].at[pl.ds(j * CH, CH), :], outs[0].at[pl.ds(j * CH, CH), :], sems, j, (x, y, 1 - c))
                for j in range(nq)]

    def start(ins, outs, sems):
        for cp in copies(ins, outs, sems):
            cp.start()

    def finish(ins, outs, sems):
        for cp in copies(ins, outs, sems):
            cp.wait()

    return _Step([q], [jax.ShapeDtypeStruct((H, C), q.dtype)], nq, start, finish)


def _sem_scratch(step):
    return [pltpu.SemaphoreType.DMA((step.n_sems,)), pltpu.SemaphoreType.DMA((step.n_sems,))]


def _run_step(step, name):
    ni, no = len(step.inputs), len(step.out_shapes)

    def body(*refs):
        ins, outs, sems = refs[:ni], refs[ni:ni + no], refs[ni + no:]
        step.start(ins, outs, sems)
        if step.mid is not None:
            step.mid(ins, outs, sems)
        step.finish(ins, outs, sems)

    return pl.pallas_call(body, name=name, in_specs=[HBM] * ni, out_specs=[HBM] * no, out_shape=step.out_shapes,
                          input_output_aliases=dict(step.alias),
                          scratch_shapes=_sem_scratch(step))(*step.inputs)


def _grid_flags(grid):
    ids = [pl.program_id(d) for d in range(len(grid))]
    first = functools.reduce(lambda a, b: a & b, [i == 0 for i in ids])
    last = functools.reduce(lambda a, b: a & b, [i == n - 1 for i, n in zip(ids, grid)])
    return first, last, last


def _call_with_step(core, step, flags, args, *, name, grid, in_specs, out_specs, out_shape, sem, scratch_shapes=(),
                    aliases=None):
    aliases = aliases or {}
    if step is None:
        return pl.pallas_call(core, name=name, grid=grid, in_specs=in_specs, out_specs=out_specs,
                              out_shape=out_shape, scratch_shapes=list(scratch_shapes),
                              input_output_aliases=aliases, compiler_params=_params(sem))(*args)
    n_in, n_out, n_scr = len(in_specs), len(out_specs), len(scratch_shapes)
    si, so = len(step.inputs), len(step.out_shapes)
    flags = flags or (lambda: _grid_flags(grid))
    aliases = {**aliases, **{n_in + a: n_out + b for a, b in step.alias}}

    def body(*refs):
        ins, s_ins = refs[:n_in], refs[n_in:n_in + si]
        outs = refs[n_in + si:n_in + si + n_out]
        s_outs = refs[n_in + si + n_out:n_in + si + n_out + so]
        scr = refs[n_in + si + n_out + so:n_in + si + n_out + so + n_scr]
        sems = refs[n_in + si + n_out + so + n_scr:]
        first, middle, last = flags()

        @pl.when(first)
        def _():
            step.start(s_ins, s_outs, sems)

        if step.mid is not None:
            @pl.when(middle)
            def _():
                step.mid(s_ins, s_outs, sems)

        core(*ins, *outs, *scr)

        @pl.when(last)
        def _():
            step.finish(s_ins, s_outs, sems)

    return pl.pallas_call(
        body, name=name, grid=grid, in_specs=list(in_specs) + [HBM] * si, out_specs=list(out_specs) + [HBM] * so,
        out_shape=list(out_shape) + list(step.out_shapes), scratch_shapes=list(scratch_shapes) + _sem_scratch(step),
        input_output_aliases=aliases, compiler_params=_params(("arbitrary",) * len(grid)))(*args, *step.inputs)


def _attn_flags(nq):
    h, qi = pl.program_id(0), pl.program_id(1)
    return ((h == 0) & (qi == 0), (h == MLA_HEADS - 1) & (qi == 0), (h == MLA_HEADS - 1) & (qi == nq - 1))


ATTN_SPLIT = 1
ATTN_KEY_SPLIT = 1


def _att_mask(s_t, q0, k0):
    krow = k0 + lax.broadcasted_iota(jnp.int32, s_t.shape, 0)
    qcol = q0 + lax.broadcasted_iota(jnp.int32, s_t.shape, 1)
    return jnp.where(krow <= qcol, s_t, NEG)


def _loop2(lo, hi, step, carry):
    n = hi - lo

    def four(i, c):
        kb = lo + 4 * i
        return step(kb + 3, step(kb + 2, step(kb + 1, step(kb, c))))

    carry = lax.fori_loop(0, n // 4, four, carry)
    base = lo + 4 * (n // 4)
    carry = lax.cond(n % 4 >= 2, lambda c: step(base + 1, step(base, c)), lambda c: c, carry)
    return lax.cond(n % 2 == 1, lambda c: step(hi - 1, c), lambda c: c, carry)


def _rows(ref, blk, t):
    return ref[pl.ds(pl.multiple_of(blk * t, t), t), :]


def _cols(ref, blk, t):
    return ref[:, pl.ds(pl.multiple_of(blk * t, t), t)]


def _attn_fwd(q, k, v_t, *, name, hosted=None):
    S = q.shape[0]
    t = min(ATTN_TILE, S)
    nq = S // t

    def body(q_ref, k_ref, vt_ref, o_ref, lse_ref):
        qi = pl.program_id(1)
        qv = q_ref[...]

        def absorb(kb, carry, masked):
            m, l, acc = carry
            s_t = lax.dot_general(_rows(k_ref, kb, t), qv, NT, preferred_element_type=F32)
            if masked:
                s_t = _att_mask(s_t, qi * t, kb * t)
            m_new = jnp.maximum(m, jnp.max(s_t, axis=0, keepdims=True))
            p_t = jnp.exp(s_t - m_new)
            corr = jnp.exp(m - m_new)
            return (m_new, corr * l + jnp.sum(p_t, axis=0, keepdims=True),
                    corr * acc + lax.dot_general(_cols(vt_ref, kb, t), p_t.astype(BF16), NN,
                                                 preferred_element_type=F32))

        init = (jnp.full((1, t), NEG, F32), jnp.zeros((1, t), F32), jnp.zeros((LANES, t), F32))
        carry = _loop2(0, qi, lambda kb, c: absorb(kb, c, False), init)
        m, l, acc = absorb(qi, carry, True)
        o_ref[...] = acc / l
        lse_ref[0] = m + jnp.log(l)

    return _call_with_step(
        body, hosted, lambda: _attn_flags(nq), (q, k, v_t), name=name, grid=(MLA_HEADS, nq),
        in_specs=[pl.BlockSpec((t, LANES), lambda h, qi: (qi, h)),
                  pl.BlockSpec((S, LANES), lambda h, qi: (0, h)),
                  pl.BlockSpec((LANES, S), lambda h, qi: (h, 0))],
        out_specs=[pl.BlockSpec((LANES, t), lambda h, qi: (h, qi)),
                   pl.BlockSpec((1, 1, t), lambda h, qi: (h, 0, qi))],
        out_shape=[jax.ShapeDtypeStruct((MLA_HEADS * LANES, S), F32), jax.ShapeDtypeStruct((MLA_HEADS, 1, S), F32)],
        sem=("parallel", "arbitrary"))


def _attn_bwd_dq(q, k, v, o_t, do_t, lse, *, name, hosted=None):
    S = q.shape[0]
    t = min(ATTN_TILE, S)
    nq = S // t

    def body(q_ref, k_ref, v_ref, o_ref, do_ref, lse_ref, dq_ref, delta_ref):
        qi = pl.program_id(1)
        qv = q_ref[...]
        dov = do_ref[...]
        delta = jnp.sum(dov * o_ref[...], axis=0, keepdims=True)
        delta_ref[0] = delta
        dob = dov.astype(BF16)
        lse_v = lse_ref[0]


        def step(kb, acc, masked):
            kt = _rows(k_ref, kb, t)
            s_t = lax.dot_general(kt, qv, NT, preferred_element_type=F32)
            if masked:
                s_t = _att_mask(s_t, qi * t, kb * t)
            p_t = jnp.exp(s_t - lse_v)
            dp_t = lax.dot_general(_rows(v_ref, kb, t), dob, NN, preferred_element_type=F32)
            ds_t = (p_t * (dp_t - delta)).astype(BF16)
            return acc + lax.dot_general(kt, ds_t, TN, preferred_element_type=F32)

        acc = _loop2(0, qi, lambda kb, c: step(kb, c, False), jnp.zeros((LANES, t), F32))
        dq_ref[...] = step(qi, acc, True).T

    tile = pl.BlockSpec((t, LANES), lambda h, qi: (qi, h))
    tile_t = pl.BlockSpec((LANES, t), lambda h, qi: (h, qi))
    stat = pl.BlockSpec((1, 1, t), lambda h, qi: (h, 0, qi))
    seq = pl.BlockSpec((S, LANES), lambda h, qi: (0, h))
    return _call_with_step(
        body, hosted, lambda: _attn_flags(nq), (q, k, v, o_t, do_t, lse), name=name, grid=(MLA_HEADS, nq),
        in_specs=[tile, seq, seq, tile_t, tile_t, stat],
        out_specs=[tile, stat],
        out_shape=[jax.ShapeDtypeStruct((S, MLA_HEADS * LANES), F32), jax.ShapeDtypeStruct((MLA_HEADS, 1, S), F32)],
        sem=("parallel", "arbitrary"))


def _attn_bwd_dkv(q, k, v, do_t, lse, delta, *, name, hosted=None):
    S = q.shape[0]
    t = min(ATTN_TILE, S)
    nq = S // t

    def body(q_ref, k_ref, v_ref, do_ref, lse_ref, delta_ref, dk_ref, dv_ref):
        ki = pl.program_id(1)
        kv, vv = k_ref[...], v_ref[...]


        def step(qb, carry, masked):
            dk, dv = carry
            qt = _rows(q_ref, qb, t)
            s_t = lax.dot_general(kv, qt, NT, preferred_element_type=F32)
            if masked:
                s_t = _att_mask(s_t, qb * t, ki * t)
            p_t = jnp.exp(s_t - _cols(lse_ref.at[0], qb, t))
            dob = _cols(do_ref, qb, t).astype(BF16)
            dv = dv + lax.dot_general(dob, p_t.astype(BF16), NT, preferred_element_type=F32)
            dp_t = lax.dot_general(vv, dob, NN, preferred_element_type=F32)
            ds_t = (p_t * (dp_t - _cols(delta_ref.at[0], qb, t))).astype(BF16)
            dk = dk + lax.dot_general(qt.T, ds_t, NT, preferred_element_type=F32)
            return dk, dv

        zero = jnp.zeros((LANES, t), F32)
        carry = step(ki, (zero, zero), True)
        dk, dv = _loop2(ki + 1, nq, lambda qb, c: step(qb, c, False), carry)
        dk_ref[...] = dk.T
        dv_ref[...] = dv

    tile = pl.BlockSpec((t, LANES), lambda h, ki: (ki, h))
    tile_t = pl.BlockSpec((LANES, t), lambda h, ki: (h, ki))
    seq = pl.BlockSpec((S, LANES), lambda h, ki: (0, h))
    seq_t = pl.BlockSpec((LANES, S), lambda h, ki: (h, 0))
    stat = pl.BlockSpec((1, 1, S), lambda h, ki: (h, 0, 0))
    return _call_with_step(
        body, hosted, lambda: _attn_flags(nq), (q, k, v, do_t, lse, delta), name=name, grid=(MLA_HEADS, nq),
        in_specs=[seq, tile, tile, seq_t, stat, stat],
        out_specs=[tile, tile_t],
        out_shape=[jax.ShapeDtypeStruct((S, MLA_HEADS * LANES), F32), jax.ShapeDtypeStruct((MLA_HEADS * LANES, S), F32)],
        sem=("parallel", "arbitrary"))


def _attn_fwd_p(q, k, v_t, *, name, hosted=None):
    S = q.shape[0]
    t = min(ATTN_TILE, S)
    nq = S // t

    def body(q_ref, k_ref, vt_ref, o_ref, lse_ref):
        qi = pl.program_id(1)
        qv = q_ref[...]

        def scores(kb):
            return lax.dot_general(_rows(k_ref, kb, t), qv, NT, preferred_element_type=F32)

        def weigh(kb, p_t):
            return lax.dot_general(_cols(vt_ref, kb, t), p_t, NN, preferred_element_type=F32)

        def soft(s_t, m, l):
            m_new = jnp.maximum(m, jnp.max(s_t, axis=0, keepdims=True))
            p_t = jnp.exp(s_t - m_new)
            corr = jnp.exp(m - m_new)
            return m_new, corr * l + jnp.sum(p_t, axis=0, keepdims=True), corr, p_t.astype(BF16)

        def step(kb, carry):
            s_cur, m, l, acc, p_prev, corr_prev = carry
            s_next = scores(kb + 1)
            acc = corr_prev * acc + weigh(jnp.maximum(kb - 1, 0), p_prev)
            m, l, corr, p_t = soft(s_cur, m, l)
            return s_next, m, l, acc, p_t, corr

        init = (scores(0), jnp.full((1, t), NEG, F32), jnp.zeros((1, t), F32), jnp.zeros((LANES, t), F32),
                jnp.zeros((t, t), BF16), jnp.ones((1, t), F32))
        s_cur, m, l, acc, p_prev, corr_prev = lax.fori_loop(0, qi, step, init)
        acc = corr_prev * acc + weigh(jnp.maximum(qi - 1, 0), p_prev)
        m, l, corr, p_t = soft(_att_mask(s_cur, qi * t, qi * t), m, l)
        acc = corr * acc + weigh(qi, p_t)
        o_ref[...] = acc / l
        lse_ref[0] = m + jnp.log(l)

    return _call_with_step(
        body, hosted, lambda: _attn_flags(nq), (q, k, v_t), name=name, grid=(MLA_HEADS, nq),
        in_specs=[pl.BlockSpec((t, LANES), lambda h, qi: (qi, h)),
                  pl.BlockSpec((S, LANES), lambda h, qi: (0, h)),
                  pl.BlockSpec((LANES, S), lambda h, qi: (h, 0))],
        out_specs=[pl.BlockSpec((LANES, t), lambda h, qi: (h, qi)),
                   pl.BlockSpec((1, 1, t), lambda h, qi: (h, 0, qi))],
        out_shape=[jax.ShapeDtypeStruct((MLA_HEADS * LANES, S), F32), jax.ShapeDtypeStruct((MLA_HEADS, 1, S), F32)],
        sem=("parallel", "arbitrary"))


def _attn_bwd_dq_p(q, k, k_t, v, o_t, do_t, lse, *, name, hosted=None):
    S = q.shape[0]
    t = min(ATTN_TILE, S)
    nq = S // t

    def body(q_ref, k_ref, kt_ref, v_ref, o_ref, do_ref, lse_ref, dq_ref, delta_ref):
        qi = pl.program_id(1)
        qv = q_ref[...]
        dov = do_ref[...]
        delta = jnp.sum(dov * o_ref[...], axis=0, keepdims=True)
        delta_ref[0] = delta
        dob = dov.astype(BF16)
        lse_v = lse_ref[0]

        def front(kb):
            return (lax.dot_general(_rows(k_ref, kb, t), qv, NT, preferred_element_type=F32),
                    lax.dot_general(_rows(v_ref, kb, t), dob, NN, preferred_element_type=F32))

        def back(kb, ds_t):
            return lax.dot_general(_cols(kt_ref, kb, t), ds_t, NN, preferred_element_type=F32)

        def mid(s_t, dp_t):
            return (jnp.exp(s_t - lse_v) * (dp_t - delta)).astype(BF16)

        def step(kb, carry):
            s_cur, dp_cur, acc, ds_prev = carry
            s_next, dp_next = front(kb + 1)
            acc = acc + back(jnp.maximum(kb - 1, 0), ds_prev)
            return s_next, dp_next, acc, mid(s_cur, dp_cur)

        init = (*front(0), jnp.zeros((LANES, t), F32), jnp.zeros((t, t), BF16))
        s_cur, dp_cur, acc, ds_prev = lax.fori_loop(0, qi, step, init)
        acc = acc + back(jnp.maximum(qi - 1, 0), ds_prev)
        dq_ref[...] = acc + back(qi, mid(_att_mask(s_cur, qi * t, qi * t), dp_cur))

    tile_t = pl.BlockSpec((LANES, t), lambda h, qi: (h, qi))
    stat = pl.BlockSpec((1, 1, t), lambda h, qi: (h, 0, qi))
    seq = pl.BlockSpec((S, LANES), lambda h, qi: (0, h))
    return _call_with_step(
        body, hosted, lambda: _attn_flags(nq), (q, k, k_t, v, o_t, do_t, lse), name=name, grid=(MLA_HEADS, nq),
        in_specs=[pl.BlockSpec((t, LANES), lambda h, qi: (qi, h)), seq,
                  pl.BlockSpec((LANES, S), lambda h, qi: (h, 0)), seq, tile_t, tile_t, stat],
        out_specs=[tile_t, stat],
        out_shape=[jax.ShapeDtypeStruct((MLA_HEADS * LANES, S), F32), jax.ShapeDtypeStruct((MLA_HEADS, 1, S), F32)],
        sem=("parallel", "arbitrary"))


def _attn_bwd_dkv_p(q, q_t, k, v, do_t, lse, delta, *, name, hosted=None):
    S = q.shape[0]
    t = min(ATTN_TILE, S)
    nq = S // t

    def body(q_ref, qt_ref, k_ref, v_ref, do_ref, lse_ref, delta_ref, dk_ref, dv_ref):
        ki = pl.program_id(1)
        kv, vv = k_ref[...], v_ref[...]

        def grad_out(qb):
            return _cols(do_ref, qb, t).astype(BF16)

        def front(qb):
            return (lax.dot_general(kv, _rows(q_ref, qb, t), NT, preferred_element_type=F32),
                    lax.dot_general(vv, grad_out(qb), NN, preferred_element_type=F32))

        def mid(s_t, dp_t, qb):
            p_t = jnp.exp(s_t - _cols(lse_ref.at[0], qb, t))
            return p_t.astype(BF16), (p_t * (dp_t - _cols(delta_ref.at[0], qb, t))).astype(BF16)

        def back(qb, dk, dv, p_t, ds_t):
            return (dk + lax.dot_general(_cols(qt_ref, qb, t), ds_t, NT, preferred_element_type=F32),
                    dv + lax.dot_general(grad_out(qb), p_t, NT, preferred_element_type=F32))

        def step(qb, carry):
            s_cur, dp_cur, dk, dv, p_prev, ds_prev = carry
            s_next, dp_next = front(jnp.minimum(qb + 1, nq - 1))
            dk, dv = back(qb - 1, dk, dv, p_prev, ds_prev)
            p_t, ds_t = mid(s_cur, dp_cur, qb)
            return s_next, dp_next, dk, dv, p_t, ds_t

        s0, dp0 = front(ki)
        p0, ds0 = mid(_att_mask(s0, ki * t, ki * t), dp0, ki)
        zero = jnp.zeros((LANES, t), F32)
        init = (*front(jnp.minimum(ki + 1, nq - 1)), zero, zero, p0, ds0)
        _, _, dk, dv, p_prev, ds_prev = lax.fori_loop(ki + 1, nq, step, init)
        dk, dv = back(nq - 1, dk, dv, p_prev, ds_prev)
        dk_ref[...] = dk
        dv_ref[...] = dv

    tile = pl.BlockSpec((t, LANES), lambda h, ki: (ki, h))
    tile_t = pl.BlockSpec((LANES, t), lambda h, ki: (h, ki))
    seq = pl.BlockSpec((S, LANES), lambda h, ki: (0, h))
    seq_t = pl.BlockSpec((LANES, S), lambda h, ki: (h, 0))
    stat = pl.BlockSpec((1, 1, S), lambda h, ki: (h, 0, 0))
    return _call_with_step(
        body, hosted, lambda: _attn_flags(nq), (q, q_t, k, v, do_t, lse, delta), name=name, grid=(MLA_HEADS, nq),
        in_specs=[seq, seq_t, tile, tile, seq_t, stat, stat],
        out_specs=[tile_t, tile_t],
        out_shape=[jax.ShapeDtypeStruct((MLA_HEADS * LANES, S), F32)] * 2,
        sem=("parallel", "arbitrary"))


def _fn_ln(ctx, x, g, b):
    xhat, _ = _ln_stats(x)
    y = xhat * g + b
    return y, y


def _fn_conv_fwd(ctx, u, up, dtr, w8, cb, dtb):
    first = ctx.i == 0
    y = u * w8[3:4] + cb
    for s in (1, 2, 3):
        y = y + _shift_down(u, up, s, first) * w8[3 - s:4 - s]
    act = y * _sigmoid(y)
    v = dtr + dtb
    e = jnp.exp(-jnp.abs(v))
    one_p = 1.0 + e
    log1p = jnp.where(one_p == 1.0, e, jnp.log(one_p) * e / (one_p - 1.0))
    return y, act, jnp.maximum(v, 0.0) + log1p


def _fn_ssd_post(ctx, y, xs, z, dexp, g):
    yg = (y + xs * dexp) * (z * _sigmoid(z))
    outs = []
    for k in range(2):
        v = yg[:, 256 * k:256 * (k + 1)]
        outs.append(v * lax.rsqrt(_mean1(v * v) + RMS_EPS))
    return (jnp.concatenate(outs, axis=1) * g,)


def _fn_ssd_post_bwd(ctx, dyn, y, xs, z, dexp, g):
    yt = y + xs * dexp
    sig = _sigmoid(z)
    sz = z * sig
    yg = yt * sz
    dyh = dyn * g
    yh, dyg = [], []
    for k in range(2):
        sl = slice(256 * k, 256 * (k + 1))
        v = yg[:, sl]
        rs = lax.rsqrt(_mean1(v * v) + RMS_EPS)
        vh = v * rs
        yh.append(vh)
        dyg.append(rs * (dyh[:, sl] - vh * _mean1(dyh[:, sl] * vh)))
    yh = jnp.concatenate(yh, axis=1)
    dyg = jnp.concatenate(dyg, axis=1)
    dyt = dyg * sz
    dz = dyg * yt * (sig * (1.0 + z * (1.0 - sig)))
    return dyt, dz, dyt * dexp, _sum0(dyt * xs), _sum0(dyn * yh)


def _fn_mla_pre(ctx, ql, kvl, gq, gkv):
    return _rms_fwd(ql, gq), _rms_fwd(kvl, gkv)


def _fn_mla_pre_bwd(ctx, ql, kvl, dqn, dkvn_k, dkvn_v, gq, gkv):
    dql, dgq = _rms_bwd(ql, dqn, gq)
    dkvl, dgkv = _rms_bwd(kvl, dkvn_k + dkvn_v, gkv)
    return dql, dkvl, dgq, dgkv


def _fn_rope(ctx, qp, kn, kr, ta, tb, tc):
    kpe = _rope(kr, ta, tb, tc)
    qs, ks = [], []
    for h in range(MLA_HEADS):
        sl = slice(128 * h, 128 * (h + 1))
        qs.append(_rope(qp[:, sl], ta, tb, tc) * MLA_SCALE)
        ks.append(kn[:, sl] + kpe)
    return jnp.concatenate(qs, axis=1), jnp.concatenate(ks, axis=1)


def _fn_rope_bwd(ctx, dq, dk, ta, tb, tc):
    qs = []
    ksum = jnp.zeros_like(ta)
    for h in range(MLA_HEADS):
        sl = slice(128 * h, 128 * (h + 1))
        qs.append(_rope_bwd(dq[:, sl] * MLA_SCALE, ta, tb, tc))
        ksum = ksum + dk[:, sl]
    lane = _lane(ksum.shape)
    dkr = jnp.where((lane >= 64) & (lane < 96), _rope_bwd(ksum, ta, tb, tc), 0.0)
    return jnp.concatenate(qs, axis=1), dkr


MEM_SCALE = MEM_HEAD_DIM ** -0.5


def _mem_probs(qh, kh):
    s = _dot(qh, kh, NT) * MEM_SCALE
    p = jnp.exp(s - jnp.max(s, axis=1, keepdims=True))
    return p / jnp.sum(p, axis=1, keepdims=True)


def _fn_mem_fwd(ctx, q, km, vm):
    outs = []
    for h in range(MEM_HEADS):
        sl = slice(256 * h, 256 * (h + 1))
        outs.append(_dot(_mem_probs(q[:, sl], km[:, sl]), vm[:, sl]))
    return (jnp.concatenate(outs, axis=1),)


def _fn_mem_bwd(ctx, q, do, km, vm):
    dqs, dks, dvs = [], [], []
    for h in range(MEM_HEADS):
        sl = slice(256 * h, 256 * (h + 1))
        p = _mem_probs(q[:, sl], km[:, sl])
        dvs.append(_dot(p, do[:, sl], TN))
        dp = _dot(do[:, sl], vm[:, sl], NT)
        ds = p * (dp - jnp.sum(dp * p, axis=1, keepdims=True)) * MEM_SCALE
        dqs.append(_dot(ds, km[:, sl]))
        dks.append(_dot(ds, q[:, sl], TN))
    return jnp.concatenate(dqs, axis=1), jnp.concatenate(dks, axis=1), jnp.concatenate(dvs, axis=1)


def _fn_res_ln(ctx, h, r, g, b):
    xhat, _ = _ln_stats(ALPHA * h + r)
    y = xhat * g + b
    return y, y


def _fn_res_ln_bwd(ctx, h, r, d1, d2, g):
    xhat, rstd = _ln_stats(ALPHA * h + r)
    return _ln_bwd(xhat, rstd, ALPHA * d1 + d2, g)


def _fn_res2_ln(ctx, h, r1, r2, g, b):
    xhat, _ = _ln_stats(ALPHA * h + (r1 + r2))
    return (xhat * g + b,)


def _fn_res2_ln_bwd(ctx, h, r1, r2, d1, d2, g):
    xhat, rstd = _ln_stats(ALPHA * h + (r1 + r2))
    return _ln_bwd(xhat, rstd, ALPHA * d1 + d2, g)


def _fn_in_ln_bwd(ctx, x, d1, d2, g):
    xhat, rstd = _ln_stats(x)
    return _ln_bwd(xhat, rstd, ALPHA * d1 + d2, g)


def _fn_final(ctx, h2, ff, tgt, g, b):
    xhat, rstd = _ln_stats(ALPHA * h2 + ff)
    e = xhat * g + b - tgt
    loss = 0.5 * _sum0(jnp.sum(e * e, axis=1, keepdims=True)) / D_MODEL
    dx, dg, db = _ln_bwd(xhat, rstd, e / D_MODEL, g)
    return dx, dx, dg, db, loss


def _epi_du(da, u):
    return da * 2.0 * jnp.maximum(u.astype(F32), 0.0)


def _relu2(u):
    r = jnp.maximum(u.astype(F32), 0.0)
    return r * r


def _fn_conv_bwd_a(ctx, y, dxs1, dxs2, dbc, dtr, ddt, dtb):
    sig = _sigmoid(y)
    dact = jnp.concatenate([dxs1 + dxs2, dbc], axis=1)
    dyc = dact * (sig * (1.0 + y * (1.0 - sig)))
    ddtr = ddt * _sigmoid(dtr + dtb)
    return dyc, ddtr, _sum0(dyc), _sum0(ddtr)


def _fn_conv_bwd_b(ctx, d, dn, u, up, w8):
    first, last = ctx.i == 0, ctx.i == ctx.n - 1
    du = d * w8[3:4]
    row = lax.broadcasted_iota(jnp.int32, w8.shape, 0)
    dw = jnp.where(row == 3, _sum0(d * u), 0.0)
    for s in (1, 2, 3):
        du = du + _shift_up(d, dn, s, last) * w8[3 - s:4 - s]
        dw = dw + jnp.where(row == 3 - s, _sum0(d * _shift_down(u, up, s, first)), 0.0)
    return du, dw


def _fn_adam(ctx, w, g, m, v):
    m = ADAM_B1 * m + (1.0 - ADAM_B1) * g
    v = ADAM_B2 * v + (1.0 - ADAM_B2) * (g * g)
    m_hat = m / (1.0 - ADAM_B1 ** ADAM_STEP)
    v_hat = v / (1.0 - ADAM_B2 ** ADAM_STEP)
    return -ADAM_LR * (m_hat / (jnp.sqrt(v_hat) + ADAM_EPS) + ADAM_WD * w), m, v


def _fn_add2(ctx, a, b):
    s = a + b
    return s, s


def _fn_add4(ctx, a, r0, r1, r2):
    return (((a + r0.astype(F32)) + r1.astype(F32)) + r2.astype(F32),)


def _z(r, c, dt):
    return jnp.zeros((r, c), dt)


W_IN_SHARD = 554
W_IN_GROUPS = [(0, 512, 1024), (512, 1536, 0), (1536, 1544, 1920), (1544, 1928, 1536), (1928, 2184, 2048),
               (2184, 2216, 2368)]


def _pad_w_in(ws):
    r, dt = ws.shape[1], ws.dtype

    def cols(a, b):
        out = []
        for k in range(N_SHARD):
            lo, hi = max(a, k * W_IN_SHARD), min(b, (k + 1) * W_IN_SHARD)
            if lo < hi:
                out.append(ws[k][:, lo - k * W_IN_SHARD:hi - k * W_IN_SHARD])
        return out

    return jnp.concatenate(cols(512, 1536) + cols(0, 512) + cols(1544, 1928) + cols(1536, 1544) + [_z(r, 120, dt)]
                           + cols(1928, 2184) + [_z(r, 64, dt)] + cols(2184, 2216) + [_z(r, 32, dt), _z(r, 128, dt)],
                           axis=1)


def _unpad_w_in(d):
    shards = []
    for k in range(N_SHARD):
        a, b = k * W_IN_SHARD, (k + 1) * W_IN_SHARD
        parts = []
        for o0, o1, p0 in W_IN_GROUPS:
            lo, hi = max(a, o0), min(b, o1)
            if lo < hi:
                parts.append(d[:, p0 + lo - o0:p0 + hi - o0])
        shards.append(jnp.concatenate(parts, axis=1))
    return jnp.stack(shards)


def _pad_heads(w, width):
    r = w.shape[0]
    w3 = w.reshape(r, MLA_HEADS, width)
    return jnp.pad(w3, ((0, 0), (0, 0), (0, 128 - width))).reshape(r, MLA_HEADS * 128)


def _pad_w_kv(w):
    r = w.shape[0]
    w4 = w.reshape(r, MLA_HEADS, 2, 64)
    return jnp.pad(w4, ((0, 0), (0, 0), (0, 0), (0, 64))).reshape(r, MLA_HEADS * 256)


def _unpad_w_kv(d):
    r = d.shape[0]
    return d.reshape(r, MLA_HEADS, 2, 128)[:, :, :, :64].reshape(r, MLA_HEADS * 128)


def _pad_w_mix(w):
    wo = jnp.pad(w[512:1024].reshape(MLA_HEADS, 64, D_MODEL), ((0, 0), (0, 64), (0, 0))).reshape(1024, D_MODEL)
    return jnp.concatenate([wo, w[0:512]], axis=0)


def _unpad_w_mix(d):
    do = d[:1024].reshape(MLA_HEADS, 128, D_MODEL)[:, :64].reshape(512, D_MODEL)
    return jnp.concatenate([d[1024:1536], do], axis=0)


def _row(v, width=None):
    v = v.reshape(1, -1).astype(F32)
    if width is not None and v.shape[1] < width:
        v = jnp.pad(v, ((0, 0), (0, width - v.shape[1])))
    return v


def _old_local_step(x, mem, positions, target, W, P):
    S = x.shape[0]
    tr = ROW_TILE
    w_in_p = _pad_w_in(W["w_in"])
    w_q_p = _pad_heads(W["w_q_up"], MLA_QK)
    w_kv3 = W["w_kv_up"].reshape(MLA_KV_RANK, MLA_HEADS, 128)
    w_k_p = _pad_heads(w_kv3[:, :, :64].reshape(MLA_KV_RANK, 512), 64)
    w_v_p = _pad_heads(w_kv3[:, :, 64:].reshape(MLA_KV_RANK, 512), 64)
    w_v_pt = w_v_p.T
    w_mix_y = W["w_mix_out"][0:512]
    w_mix_o = jnp.pad(W["w_mix_out"][512:1024].reshape(MLA_HEADS, 64, D_MODEL),
                      ((0, 0), (0, 64), (0, 0))).reshape(MLA_HEADS * 128, D_MODEL)
    conv_w8 = jnp.pad(P["conv_w"].astype(F32), ((0, 4), (0, 0)))
    conv_b = _row(P["conv_b"])
    dt_b = _row(P["dt_bias"], 128)
    a_head = -jnp.exp(P["a_log"].reshape(-1).astype(F32))
    a_row = _row(a_head, 128)
    dexp = jnp.repeat(P["d_skip"].reshape(-1).astype(F32), 64).reshape(1, 512)
    g_ssd, g_q, g_kv = _row(P["ssd_norm_g"]), _row(P["q_norm_g"]), _row(P["kv_norm_g"])
    g_in, b_in = _row(P["ln_in_g"]), _row(P["ln_in_b"])
    g1, b1, g2, b2, g3, b3 = (_row(P[k]) for k in ("ln1_g", "ln1_b", "ln2_g", "ln2_b", "ln3_g", "ln3_b"))

    half = MLA_ROPE // 2
    inv_freq = jnp.power(ROPE_THETA, -jnp.arange(half, dtype=F32) / half)
    ang = positions.reshape(S, 1).astype(F32) * inv_freq
    cos, sin = jnp.cos(ang), jnp.sin(ang)
    zc = lambda n: jnp.zeros((S, n), F32)
    rope_a = jnp.concatenate([jnp.ones((S, 64), F32), cos, cos, zc(32)], axis=1)
    rope_b = jnp.concatenate([zc(80), sin, zc(32)], axis=1)
    rope_c = jnp.concatenate([zc(64), -sin, zc(48)], axis=1)

    proj = _mm(h0_b, w_in_p, form="nn", tn=IN_W // 2, name="mm_in")
    conv_y, xbc, dt = _rowwise(
        _fn_conv_fwd, [(proj,) + SEG_XBC, ("prev", proj) + SEG_XBC, (proj,) + SEG_DT], [conv_w8, conv_b, dt_b],
        [1024, 1024, 128], tr=tr, name="conv_fwd")
    y_ssd, hs = _ssd_fwd(xbc, dt, a_row, name="ssd_fwd")
    (y_n,) = _rowwise(_fn_ssd_post, [y_ssd, (xbc, 0, 512), (proj,) + SEG_Z], [dexp, g_ssd], [(512, BF16)], tr=tr,
                      name="ssd_post")
    q_n, kv_n = _rowwise(_fn_mla_pre, [(proj,) + SEG_QLAT, (proj,) + SEG_KVLAT], [g_q, g_kv], [384, 256], tr=tr,
                         name="mla_pre")
    qp = _mm(q_n, w_q_p, form="nn", name="mm_q_up")
    kn = _mm(kv_n, w_k_p, form="nn", name="mm_k_up")
    v_nat = _mm(kv_n, w_v_p, form="nn", out_dtype=BF16, name="mm_v_up")
    v_t = _mm(w_v_pt, kv_n, form="nt", out_dtype=BF16, name="mm_v_up_t")
    q_rot, k_full = _rowwise(_fn_rope, [qp, kn, (proj,) + SEG_KR, rope_a, rope_b, rope_c], [],
                             [(1024, BF16), (1024, BF16)], tr=tr, name="rope")
    o_t, lse = _attn_fwd(q_rot, k_full, v_t, name="attn_fwd")
    mix_o = _mm(o_t, w_mix_o, form="tn", name="mm_mix_o")
    mix_y = _mm(y_n, w_mix_y, form="nn", name="mm_mix_y")
    (h1,) = _rowwise(_fn_res2_ln, [h0, mix_o, mix_y], [g1, b1], [D_MODEL], tr=tr, name="ln1")
    qm = _mm(h1, W["w_mem_q"], form="nn", name="mm_mem_q")
    km = _mm(mem, W["w_mem_k"], form="nn", name="mm_mem_k")
    vm = _mm(mem, W["w_mem_v"], form="nn", name="mm_mem_v")
    (om,) = _rowwise(_fn_mem_fwd, [qm], [km, vm], [(D_MODEL, BF16)], tr=tr, name="mem_fwd")
    xa = _mm(om, W["w_mem_o"], form="nn", name="mm_mem_o")
    h2, h2_b = _rowwise(_fn_res_ln, [h1, xa], [g2, b2], [D_MODEL, (D_MODEL, BF16)], tr=tr, name="ln2")
    u = _mm(h2, W["w_up"], form="nn", name="mm_up")
    ff = _mm(u, W["w_down"], form="nn", a_pro=_relu2, name="mm_down")

    dt3, dt3_b, dg3, db3, loss = _rowwise(_fn_final, [h2, ff, target], [g3, b3], [D_MODEL, (D_MODEL, BF16)],
                                   [(1, D_MODEL), (1, D_MODEL), (1, 128)], tr=tr, name="ln3_loss")
    da = _mm(dt3, W["w_down"], form="nt", name="mm_down_dx")
    dw_down = _mm(u, dt3, form="tn", a_pro=_relu2, name="mm_down_dw")
    dw_up = _mm(h2, du, form="tn", name="mm_up_dw")
    dh2 = _mm(du, W["w_up"], form="nt", name="mm_up_dx")
    dt2, dg2, db2 = _rowwise(_fn_res_ln_bwd, [h1, xa, dt3, dh2], [g2], [D_MODEL], [(1, D_MODEL)] * 2, tr=tr,
                             name="ln2_bwd")
    dom = _mm(dt2, W["w_mem_o"], form="nt", name="mm_mem_o_dx")
    dw_mem_o = _mm(om, dt2, form="tn", name="mm_mem_o_dw")
    dqm, dkm, dvm = _rowwise(_fn_mem_bwd, [qm, dom], [km, vm], [(D_MODEL, BF16)], [(256, D_MODEL)] * 2, tr=tr,
                             name="mem_bwd")
    dw_mem_q = _mm(h1, dqm, form="tn", name="mm_mem_q_dw")
    dw_mem_k = _mm(mem, dkm, form="tn", name="mm_mem_k_dw")
    dw_mem_v = _mm(mem, dvm, form="tn", name="mm_mem_v_dw")
    dh1 = _mm(dqm, W["w_mem_q"], form="nt", name="mm_mem_q_dx")
    dt1, dg1, db1 = _rowwise(_fn_res2_ln_bwd, [h0, mix_o, mix_y, dt2, dh1], [g1], [D_MODEL], [(1, D_MODEL)] * 2,
                             tr=tr, name="ln1_bwd")
    do_t = _mm(w_mix_o, dt1, form="nt", name="mm_mix_o_dx")
    dy_n = _mm(dt1, w_mix_y, form="nt", name="mm_mix_y_dx")
    dw_mix_o = _mm(o_t, dt1, form="nn", name="mm_mix_o_dw")
    dw_mix_y = _mm(y_n, dt1, form="tn", name="mm_mix_y_dw")
    dq_t, delta = _attn_bwd_dq(q_rot, k_full, k_full.T, v_nat, o_t, do_t, lse, name="attn_bwd_dq")
    dk_t, dv_t = _attn_bwd_dkv(q_rot, q_rot.T, k_full, v_nat, do_t, lse, delta, name="attn_bwd_dkv")
    dk = dk_t.T
    dqp, dkr = _rowwise(_fn_rope_bwd, [dq_t.T, dk, rope_a, rope_b, rope_c], [], [(1024, BF16), (128, BF16)], tr=tr,
                        name="rope_bwd")
    dw_q_p = _mm(q_n, dqp, form="tn", name="mm_q_up_dw")
    dq_n = _mm(dqp, w_q_p, form="nt", name="mm_q_up_dx")
    dw_k_p = _mm(kv_n, dk, form="tn", name="mm_k_up_dw")
    dkv_n1 = _mm(dk, w_k_p, form="nt", name="mm_k_up_dx")
    dw_v_pt = _mm(dv_t, kv_n, form="nn", name="mm_v_up_dw")
    dkv_n2 = _mm(dv_t, w_v_pt, form="tn", name="mm_v_up_dx")
    dq_lat, dkv_lat, dg_q, dg_kv = _rowwise(
        _fn_mla_pre_bwd, [(proj,) + SEG_QLAT, (proj,) + SEG_KVLAT, dq_n, dkv_n1, dkv_n2], [g_q, g_kv], [(384, BF16), (256, BF16)],
        [(1, 384), (1, 256)], tr=tr, name="mla_pre_bwd")
    dy_ssd, dz, dxs_skip, ddexp, dg_ssd = _rowwise(
        _fn_ssd_post_bwd, [dy_n, y_ssd, (xbc, 0, 512), (proj,) + SEG_Z], [dexp, g_ssd],
        [512, (512, BF16), 512], [(1, 512)] * 2, tr=tr, name="ssd_post_bwd")
    dxs, dbc, ddt, da_head = _ssd_bwd(xbc, dt, a_row, hs, dy_ssd, name="ssd_bwd")
    dyc, ddtr, dconv_b, ddt_b = _rowwise(
        _fn_conv_bwd_a, [conv_y, dxs, dxs_skip, dbc, (proj,) + SEG_DT, ddt], [dt_b], [1024, (128, BF16)],
        [(1, 1024), (1, 128)], tr=tr, name="conv_bwd_a")
    dxbc, dconv_w8 = _rowwise(
        _fn_conv_bwd_b, [dyc, ("next", dyc, 0, 1024), (proj,) + SEG_XBC, ("prev", proj) + SEG_XBC], [conv_w8], [(1024, BF16)],
        [(8, 1024)], tr=tr, name="conv_bwd_b")
    dproj = jnp.concatenate([dxbc, dz, dq_lat, ddtr, dkv_lat, dkr, jnp.zeros((S, 128), BF16)], axis=1)
    res = _mm(h0_b, dproj, form="tn", tn=IN_W // 2, name="mm_in_dw", hosted=_pair_fill_step(gp) if dist else None)
    dw_in_p, red_a = (res[0], res[1]) if dist else (res, None)
    big_b = _group_b_grads(dw_in_p, dw_q_p, dw_k_p, dw_v_pt, dconv_w8)
    q_b = None
    if dist:
        gp_c, gp_b = _pack_group_b(big_b)
        dh0, theirs_c, theirs_b = _mm(dproj, w_in_p, form="nt", tk=IN_W // 2, name="mm_in_dx", hosted=_merge_steps(
            [_pair_exchange_step(gp_c), _pair_exchange_step(gp_b)]))
        q_b = ((gp_c, theirs_c, _pair_sum(gp_c, theirs_c, "pair_sum_w_in")),
               (gp_b, theirs_b, _pair_sum(gp_b, theirs_b, "pair_sum_b")))
        gp = red_a
    else:
        dh0 = _mm(dproj, w_in_p, form="nt", tk=IN_W // 2, name="mm_in_dx")
    grad_x, dg_in, db_in = _rowwise(_fn_in_ln_bwd, [x, dt1, dh0], [g_in], [D_MODEL], [(1, D_MODEL)] * 2, tr=tr,
                                    name="ln_in_bwd")

    big = {
        "w_in": _unpad_w_in(dw_in_p),
        "w_q_up": dw_q_p.reshape(384, MLA_HEADS, 128)[:, :, :MLA_QK].reshape(384, MLA_HEADS * MLA_QK),
        "w_kv_up": jnp.concatenate([dw_k_p.reshape(MLA_KV_RANK, MLA_HEADS, 128)[:, :, :64],
                                    dw_v_pt.T.reshape(MLA_KV_RANK, MLA_HEADS, 128)[:, :, :64]], axis=2).reshape(
                                        MLA_KV_RANK, MLA_HEADS * 128),
        "w_mix_out": jnp.concatenate([dw_mix_y, dw_mix_o.reshape(MLA_HEADS, 128, D_MODEL)[:, :64].reshape(
            512, D_MODEL)], axis=0),
        "w_mem_q": dw_mem_q, "w_mem_k": dw_mem_k, "w_mem_v": dw_mem_v, "w_mem_o": dw_mem_o,
        "w_up": dw_up, "w_down": dw_down,
        "conv_w": dconv_w8[0:4],
    }
    small = {
        "ln_in_g": dg_in, "ln_in_b": db_in, "conv_b": dconv_b, "dt_bias": ddt_b[:, :8],
        "a_log": da_head[:, :8] * a_head.reshape(1, 8),
        "d_skip": ddexp.reshape(8, 64).sum(axis=1).reshape(1, 8),
        "ssd_norm_g": dg_ssd, "q_norm_g": dg_q, "kv_norm_g": dg_kv,
        "ln1_g": dg1, "ln1_b": db1, "ln2_g": dg2, "ln2_b": db2, "ln3_g": dg3, "ln3_b": db3,
    }
    return loss[0, 0], grad_x, big, small


BIG = {
    "w_in": (1024, 2216, 1), "w_q_up": (384, 768, 1), "w_kv_up": (256, 1024, 1), "w_mix_out": (1024, 1024, 0),
    "w_mem_q": (1024, 1024, 0), "w_mem_k": (1024, 1024, 0), "w_mem_v": (1024, 1024, 0), "w_mem_o": (1024, 1024, 0),
    "w_up": (1024, 4096, 1), "w_down": (4096, 1024, 0), "conv_w": (4, 1024, 1),
}
BIG_ORDER = list(BIG)
SMALL_ORDER = ["ln_in_g", "ln_in_b", "conv_b", "dt_bias", "a_log", "d_skip", "ssd_norm_g", "q_norm_g", "kv_norm_g",
               "ln1_g", "ln1_b", "ln2_g", "ln2_b", "ln3_g", "ln3_b"]
N_SHARD = 4
PACK_COLS = 1024
PACK_ROWS = 4032
HALF_ROWS = PACK_ROWS // 2
GATHER_CHUNKS = 3
CHIP_CHUNKS = 3
PAIR_CHUNKS = 4


def _shard_shape(name):
    r, c, ax = BIG[name]
    return (r // N_SHARD, c) if ax == 0 else (r, c // N_SHARD)


def _split_shards(name, full):
    r, c, ax = BIG[name]
    if ax == 0:
        return full.reshape(N_SHARD, -1)
    return full.reshape(r, N_SHARD, c // N_SHARD).transpose(1, 0, 2).reshape(N_SHARD, -1)


def _join_shards(name, parts):
    r, c, ax = BIG[name]
    if ax == 0:
        return parts.reshape(r, c)
    return parts.reshape(N_SHARD, r, c // N_SHARD).transpose(1, 0, 2).reshape(r, c)


HBM = pl.BlockSpec(memory_space=pl.ANY)


def _place():
    x, y, c = lax.axis_index("x"), lax.axis_index("y"), lax.axis_index("c")
    chips = [(1 - x, y), (x, 1 - y), (1 - x, 1 - y)]
    return x, y, c, chips


def _gather_weights(wp):
    R, C = wp.shape
    H = R // 2
    nq = GATHER_CHUNKS
    CH = H // nq

    def body(w_ref, out_ref, send_sems, recv_sems):
        x, y, c, chips = _place()
        sib = (x, y, 1 - c)

        def piece(k, hc, q):
            return out_ref.at[k, pl.ds(hc * H + q * CH, CH), :]

        def copy(j, src, dst, to):
            return pltpu.make_async_remote_copy(src_ref=src, dst_ref=dst, send_sem=send_sems.at[j],
                                                recv_sem=recv_sems.at[j], device_id=to, device_id_type=MESH)

        me = 2 * x + y
        sends = []
        for q in range(nq):
            for j, (px, py) in enumerate(chips):
                cp = copy(j * nq + q, w_ref.at[pl.ds(c * H + q * CH, CH), :], piece(me, c, q), (px, py, c))
                cp.start()
                sends.append(cp)
        fwds = []
        for q in range(nq):
            for j, (px, py) in enumerate(chips):
                k = 2 * px + py
                copy(j * nq + q, piece(k, c, q), piece(k, c, q), (px, py, c)).wait_recv()
                f = copy((3 + j) * nq + q, piece(k, c, q), piece(k, c, q), sib)
                f.start()
                fwds.append(f)
        for q in range(nq):
            for j, (px, py) in enumerate(chips):
                k = 2 * px + py
                copy((3 + j) * nq + q, piece(k, 1 - c, q), piece(k, 1 - c, q), sib).wait_recv()
        for cp in sends + fwds:
            cp.wait_send()

    out = pl.pallas_call(
        body, name="gather_weights", in_specs=[HBM], out_specs=HBM,
        out_shape=jax.ShapeDtypeStruct((N_SHARD, R, C), wp.dtype),
        scratch_shapes=[pltpu.SemaphoreType.DMA((6 * nq,)), pltpu.SemaphoreType.DMA((6 * nq,))],
    )(wp)
    me = 2 * lax.axis_index("x") + lax.axis_index("y")
    return lax.dynamic_update_slice(out, wp[None], (me, 0, 0))


def _pair_exchange(gp):
    n, R, C = gp.shape
    H = R // 2
    nq = PAIR_CHUNKS
    CH = H // nq

    def body(g_ref, theirs_ref, send_sems, recv_sems):
        x, y, c, _ = _place()
        swaps = []
        for k in range(n):
            for q in range(nq):
                cp = pltpu.make_async_remote_copy(
                    src_ref=g_ref.at[k, pl.ds((1 - c) * H + q * CH, CH), :], dst_ref=theirs_ref.at[k, pl.ds(q * CH, CH), :],
                    send_sem=send_sems.at[k * nq + q], recv_sem=recv_sems.at[k * nq + q], device_id=(x, y, 1 - c),
                    device_id_type=MESH)
                cp.start()
                swaps.append(cp)
        for cp in swaps:
            cp.wait()

    theirs = pl.pallas_call(
        body, name="pair_exchange", in_specs=[HBM], out_specs=HBM,
        out_shape=jax.ShapeDtypeStruct((n, H, C), gp.dtype),
        scratch_shapes=[pltpu.SemaphoreType.DMA((n * nq,)), pltpu.SemaphoreType.DMA((n * nq,))],
    )(gp)
    mine = lax.dynamic_slice(gp, (0, lax.axis_index("c") * H, 0), (n, H, C))
    return mine, theirs


def _chip_exchange(pb):
    n, H, C = pb.shape
    nq = CHIP_CHUNKS
    CH = H // nq

    def body(pb_ref, got_ref, send_sems, recv_sems):
        x, y, c, chips = _place()
        sends = []
        for q in range(nq):
            for j, (px, py) in enumerate(chips):
                cp = pltpu.make_async_remote_copy(
                    src_ref=pb_ref.at[2 * px + py, pl.ds(q * CH, CH), :], dst_ref=got_ref.at[j, pl.ds(q * CH, CH), :],
                    send_sem=send_sems.at[j * nq + q], recv_sem=recv_sems.at[j * nq + q],
                    device_id=(px, py, c), device_id_type=MESH)
                cp.start()
                sends.append(cp)
        for cp in sends:
            cp.wait()

    return pl.pallas_call(
        body, name="chip_exchange", in_specs=[HBM], out_specs=HBM,
        out_shape=jax.ShapeDtypeStruct((3, H, C), BF16),
        scratch_shapes=[pltpu.SemaphoreType.DMA((3 * nq,)), pltpu.SemaphoreType.DMA((3 * nq,))],
    )(pb)


def _pair_join(q):
    H, C = q.shape
    nq = PAIR_CHUNKS
    CH = H // nq

    def body(q_ref, theirs_ref, send_sems, recv_sems):
        x, y, c, _ = _place()
        pushes = []
        for j in range(nq):
            cp = pltpu.make_async_remote_copy(
                src_ref=q_ref.at[pl.ds(j * CH, CH), :], dst_ref=theirs_ref.at[pl.ds(j * CH, CH), :],
                send_sem=send_sems.at[j], recv_sem=recv_sems.at[j], device_id=(x, y, 1 - c), device_id_type=MESH)
            cp.start()
            pushes.append(cp)
        for cp in pushes:
            cp.wait()

    theirs = pl.pallas_call(
        body, name="pair_join", in_specs=[HBM], out_specs=HBM,
        out_shape=jax.ShapeDtypeStruct((H, C), F32),
        scratch_shapes=[pltpu.SemaphoreType.DMA((nq,)), pltpu.SemaphoreType.DMA((nq,))],
    )(q)
    c = lax.axis_index("c")
    out = jnp.zeros((2 * H, C), F32)
    out = lax.dynamic_update_slice(out, q, (c * H, 0))
    return lax.dynamic_update_slice(out, theirs, ((1 - c) * H, 0))


N_DEV = 8


def _small_all_reduce(g, step=None):
    r, cdim = g.shape
    si, so = (len(step.inputs), len(step.out_shapes)) if step else (0, 0)

    def body(g_ref, *refs):
        s_ins, out_ref, s_outs = refs[:si], refs[si], refs[si + 1:si + 1 + so]
        buf, send_sems, recv_sems = refs[si + 1 + so:si + 4 + so]
        s_sems = refs[si + 4 + so:]
        if step:
            step.start(s_ins, s_outs, s_sems)
        x, y, c, _ = _place()
        me = 4 * x + 2 * y + c
        buf[me] = g_ref[...]
        copies = []
        for d in range(1, N_DEV):
            to = me ^ d
            cp = pltpu.make_async_remote_copy(src_ref=g_ref, dst_ref=buf.at[me], send_sem=send_sems.at[d - 1],
                                              recv_sem=recv_sems.at[d - 1],
                                              device_id=(to // 4, (to // 2) % 2, to % 2), device_id_type=MESH)
            cp.start()
            copies.append(cp)
        for cp in copies:
            cp.wait()
        acc = buf[0]
        for d in range(1, N_DEV):
            acc = acc + buf[d]
        out_ref[...] = acc
        if step:
            step.finish(s_ins, s_outs, s_sems)

    res = pl.pallas_call(
        body, name="small_all_reduce",
        in_specs=[pl.BlockSpec(memory_space=pltpu.VMEM)] + [HBM] * si,
        out_specs=[pl.BlockSpec(memory_space=pltpu.VMEM)] + [HBM] * so,
        out_shape=[jax.ShapeDtypeStruct((r, cdim), F32)] + (list(step.out_shapes) if step else []),
        scratch_shapes=[pltpu.VMEM((N_DEV, r, cdim), F32), pltpu.SemaphoreType.DMA((N_DEV - 1,)),
                        pltpu.SemaphoreType.DMA((N_DEV - 1,))] + (_sem_scratch(step) if step else []),
    )(g, *(step.inputs if step else []))
    return res if step else res[0]


def _adam(w, g, m, v, name):
    shape = w.shape
    w2, g2, m2, v2 = (t.reshape(-1, shape[-1]) for t in (w, g, m, v))
    d, mn, vn = _rowwise(_fn_adam, [w2, g2, m2, v2], [], [shape[-1]] * 3, tr=256, name=name)
    return d.reshape(shape), mn.reshape(shape), vn.reshape(shape)


def _old_kernel(x, mem, positions, ln_in_g, ln_in_b, w_in, conv_w, conv_b, dt_bias, a_log, d_skip, ssd_norm_g, q_norm_g, w_q_up, kv_norm_g, w_kv_up, w_mix_out, ln1_g, ln1_b, w_mem_q, w_mem_k, w_mem_v, w_mem_o, ln2_g, ln2_b, w_up, w_down, ln3_g, ln3_b, loss_target, m_ln_in_g, m_ln_in_b, m_w_in, m_conv_w, m_conv_b, m_dt_bias, m_a_log, m_d_skip, m_ssd_norm_g, m_q_norm_g, m_w_q_up, m_kv_norm_g, m_w_kv_up, m_w_mix_out, m_ln1_g, m_ln1_b, m_w_mem_q, m_w_mem_k, m_w_mem_v, m_w_mem_o, m_ln2_g, m_ln2_b, m_w_up, m_w_down, m_ln3_g, m_ln3_b, v_ln_in_g, v_ln_in_b, v_w_in, v_conv_w, v_conv_b, v_dt_bias, v_a_log, v_d_skip, v_ssd_norm_g, v_q_norm_g, v_w_q_up, v_kv_norm_g, v_w_kv_up, v_w_mix_out, v_ln1_g, v_ln1_b, v_w_mem_q, v_w_mem_k, v_w_mem_v, v_w_mem_o, v_ln2_g, v_ln2_b, v_w_up, v_w_down, v_ln3_g, v_ln3_b):
    args = dict(locals())
    weights = BIG_ORDER + SMALL_ORDER

    flat = []
    for n in BIG_ORDER:
        s = args[n].reshape(-1)
        if n == "conv_w":
            flat.append(lax.bitcast_convert_type(s.astype(F32), BF16).reshape(-1))
        else:
            flat.append(s.astype(BF16))
    flat = jnp.concatenate(flat)
    wp = jnp.pad(flat, (0, PACK_ROWS * PACK_COLS - flat.shape[0])).reshape(PACK_ROWS, PACK_COLS)
    gathered = _gather_weights(wp).reshape(N_SHARD, -1)
    W, off = {}, 0
    for n in BIG_ORDER:
        sr, sc = _shard_shape(n)
        cnt = sr * sc
        if n == "conv_w":
            part = lax.bitcast_convert_type(gathered[:, off:off + 2 * cnt].reshape(N_SHARD, cnt, 2), F32)
            off += 2 * cnt
        else:
            part = gathered[:, off:off + cnt]
            off += cnt
        W[n] = _join_shards(n, part)
    P = {n: args[n] for n in SMALL_ORDER}
    P["conv_w"] = W.pop("conv_w")

    loss, grad_x, gbig, gsmall = _local_step(x[0], mem[0], positions[0], loss_target[0], W, P)
    loss = lax.psum(loss, ("x", "y", "c"))

    gflat = jnp.concatenate([_split_shards(n, gbig[n]) for n in BIG_ORDER], axis=1)
    gp = jnp.pad(gflat, ((0, 0), (0, PACK_ROWS * PACK_COLS - gflat.shape[1]))).reshape(N_SHARD, PACK_ROWS, PACK_COLS)
    mine, theirs = _pair_exchange(gp)
    pf, pb = _rowwise(_fn_add2, [mine.reshape(-1, PACK_COLS), theirs.reshape(-1, PACK_COLS)], [],
                      [PACK_COLS, (PACK_COLS, BF16)], tr=288, name="pair_sum")
    pf = pf.reshape(N_SHARD, HALF_ROWS, PACK_COLS)
    pb = pb.reshape(N_SHARD, HALF_ROWS, PACK_COLS)
    got = _chip_exchange(pb).reshape(3 * HALF_ROWS, PACK_COLS)
    own = lax.dynamic_index_in_dim(pf, 2 * lax.axis_index("x") + lax.axis_index("y"), axis=0, keepdims=False)
    (q,) = _rowwise(_fn_add4, [own] + [(got, 0, PACK_COLS, j * HALF_ROWS) for j in range(3)], [], [PACK_COLS],
                    tr=288, name="chip_sum", n_rows=HALF_ROWS)
    red = _pair_join(q).reshape(-1)

    gs = jnp.concatenate([_row(gsmall[n], PACK_COLS) for n in SMALL_ORDER] + [jnp.zeros((1, PACK_COLS), F32)], axis=0)
    gs = _small_all_reduce(gs)

    grads, deltas, new_m, new_v = {}, {}, {}, {}
    off = 0
    for n in BIG_ORDER:
        sr, sc = _shard_shape(n)
        g = red[off:off + sr * sc].reshape(args[n].shape)
        off += sr * sc
        grads[n] = g
        deltas[n], new_m[n], new_v[n] = _adam(args[n], g, args["m_" + n], args["v_" + n], "adam_" + n)
    pack = lambda pre: jnp.concatenate([_row(args[pre + n], PACK_COLS) for n in SMALL_ORDER]
                                       + [jnp.zeros((1, PACK_COLS), F32)], axis=0)
    ds, ms, vs = _rowwise(_fn_adam, [pack(""), gs, pack("m_"), pack("v_")], [], [PACK_COLS] * 3, tr=16,
                          name="adam_small")
    for i, n in enumerate(SMALL_ORDER):
        cnt = args[n].size
        take = lambda t: t[i, :cnt].reshape(args[n].shape)
        grads[n], deltas[n], new_m[n], new_v[n] = take(gs), take(ds), take(ms), take(vs)

    order = ["ln_in_g", "ln_in_b", "w_in", "conv_w", "conv_b", "dt_bias", "a_log", "d_skip", "ssd_norm_g",
             "q_norm_g", "w_q_up", "kv_norm_g", "w_kv_up", "w_mix_out", "ln1_g", "ln1_b", "w_mem_q", "w_mem_k",
             "w_mem_v", "w_mem_o", "ln2_g", "ln2_b", "w_up", "w_down", "ln3_g", "ln3_b"]
    assert sorted(order) == sorted(weights)
    return (loss, grad_x[None], *[grads[n] for n in order], *[deltas[n] for n in order],
            *[new_m[n] for n in order], *[new_v[n] for n in order])


PACK_A_ROW = {"w_down": 0, "w_up": 1024, "w_mem_q": 2048, "w_mem_k": 2304, "w_mem_v": 2560, "w_mem_o": 2816,
              "w_mix_out": 3072}
PACK_A_ORDER = list(PACK_A_ROW)
PACK_A_ROWS = 3328
PACK_B_ORDER = ["w_q_up", "w_kv_up", "conv_w"]
PACK_B_ROWS = 160


def _mesh_pos():
    return 2 * lax.axis_index("x") + lax.axis_index("y"), lax.axis_index("c")


def _unpack_group_b(g_c, g_b, own):
    me, _ = _mesh_pos()
    g_c = lax.dynamic_update_slice(g_c, own[0][None], (me, 0, 0))
    g_b = lax.dynamic_update_slice(g_b, own[1][None], (me, 0, 0)).reshape(N_SHARD, -1)
    WB, off = {"w_in": g_c}, 0
    for n in PACK_B_ORDER:
        sr, sc = _shard_shape(n)
        cnt = sr * sc
        if n == "conv_w":
            part = lax.bitcast_convert_type(g_b[:, off:off + 2 * cnt].reshape(N_SHARD, cnt, 2), F32)
            off += 2 * cnt
        else:
            part = g_b[:, off:off + cnt]
            off += cnt
        WB[n] = _join_shards(n, part)
    return WB


def _local_step(x, mem, positions, target, WB, P, *, wp_a=None, g_a=None, wp_b=None):
    S = x.shape[0]
    tr = ROW_TILE
    dist = g_a is None
    g_in, b_in = _row(P["ln_in_g"]), _row(P["ln_in_b"])
    res = _rowwise(_fn_ln, [x], [g_in, b_in], [D_MODEL, (D_MODEL, BF16)], tr=tr, name="ln_in",
                   hosted=_merge_steps([_gather_step(w) for w in wp_b]) if dist else None)
    h0, h0_b = res[0], res[1]
    if dist:
        WB = _unpack_group_b(res[2], res[3], wp_b)
    P = {**P, "conv_w": WB["conv_w"]}
    w_in_p = _pad_w_in(WB["w_in"])
    w_q_p = _pad_heads(WB["w_q_up"], MLA_QK)
    w_kv3 = WB["w_kv_up"].reshape(MLA_KV_RANK, MLA_HEADS, 128)
    w_k_p = _pad_heads(w_kv3[:, :, :64].reshape(MLA_KV_RANK, 512), 64)
    w_v_p = _pad_heads(w_kv3[:, :, 64:].reshape(MLA_KV_RANK, 512), 64)
    w_v_pt = w_v_p.T
    conv_w8 = jnp.pad(P["conv_w"].astype(F32), ((0, 4), (0, 0)))
    conv_b = _row(P["conv_b"])
    dt_b = _row(P["dt_bias"], 128)
    a_head = -jnp.exp(P["a_log"].reshape(-1).astype(F32))
    a_row = _row(a_head, 128)
    dexp = jnp.repeat(P["d_skip"].reshape(-1).astype(F32), 64).reshape(1, 512)
    g_ssd, g_q, g_kv = _row(P["ssd_norm_g"]), _row(P["q_norm_g"]), _row(P["kv_norm_g"])
    g_in, b_in = _row(P["ln_in_g"]), _row(P["ln_in_b"])
    g1, b1, g2, b2, g3, b3 = (_row(P[k]) for k in ("ln1_g", "ln1_b", "ln2_g", "ln2_b", "ln3_g", "ln3_b"))

    half = MLA_ROPE // 2
    inv_freq = jnp.power(ROPE_THETA, -jnp.arange(half, dtype=F32) / half)
    ang = positions.reshape(S, 1).astype(F32) * inv_freq
    cos, sin = jnp.cos(ang), jnp.sin(ang)
    zc = lambda n: jnp.zeros((S, n), F32)
    rope_a = jnp.concatenate([jnp.ones((S, 64), F32), cos, cos, zc(32)], axis=1)
    rope_b = jnp.concatenate([zc(80), sin, zc(32)], axis=1)
    rope_c = jnp.concatenate([zc(64), -sin, zc(48)], axis=1)

    proj = _mm(h0_b, w_in_p, form="nn", tn=IN_W // 2, name="mm_in")
    conv_y, xbc, dt = _rowwise(
        _fn_conv_fwd, [(proj,) + SEG_XBC, ("prev", proj) + SEG_XBC, (proj,) + SEG_DT], [conv_w8, conv_b, dt_b],
        [1024, 1024, 128], tr=tr, name="conv_fwd")
    y_ssd, hs = _ssd_fwd(xbc, dt, a_row, name="ssd_fwd")
    (y_n,) = _rowwise(_fn_ssd_post, [y_ssd, (xbc, 0, 512), (proj,) + SEG_Z], [dexp, g_ssd], [(512, BF16)], tr=tr,
                      name="ssd_post")
    q_n, kv_n = _rowwise(_fn_mla_pre, [(proj,) + SEG_QLAT, (proj,) + SEG_KVLAT], [g_q, g_kv], [384, 256], tr=tr,
                         name="mla_pre")
    qp = _mm(q_n, w_q_p, form="nn", name="mm_q_up")
    kn = _mm(kv_n, w_k_p, form="nn", name="mm_k_up")
    v_nat = _mm(kv_n, w_v_p, form="nn", out_dtype=BF16, name="mm_v_up")
    v_t = _mm(w_v_pt, kv_n, form="nt", out_dtype=BF16, name="mm_v_up_t")
    q_rot, k_full = _rowwise(_fn_rope, [qp, kn, (proj,) + SEG_KR, rope_a, rope_b, rope_c], [],
                             [(1024, BF16), (1024, BF16)], tr=tr, name="rope")
    res = _attn_fwd(q_rot, k_full, v_t, name="attn_fwd", hosted=_gather_step(wp_a) if dist else None)
    o_t, lse = res[0], res[1]
    if dist:
        g_a = lax.dynamic_update_slice(res[2], wp_a[None], (_mesh_pos()[0], 0, 0))
    r_mix = PACK_A_ROW["w_mix_out"]
    w_mix_o = jnp.pad(g_a[2:4, r_mix:r_mix + 256].reshape(MLA_HEADS, 64, D_MODEL),
                      ((0, 0), (0, 64), (0, 0))).reshape(MLA_HEADS * 128, D_MODEL)
    mix_o = _mm(o_t, w_mix_o, form="tn", name="mm_mix_o")
    mix_y = _mm(y_n, g_a, form="nn", b_pack="w_mix_out", name="mm_mix_y")
    (h1,) = _rowwise(_fn_res2_ln, [h0, mix_o, mix_y], [g1, b1], [D_MODEL], tr=tr, name="ln1")
    qm = _mm(h1, g_a, form="nn", b_pack="w_mem_q", out_dtype=BF16, name="mm_mem_q")
    km = _mm(mem, g_a, form="nn", b_pack="w_mem_k", out_dtype=BF16, name="mm_mem_k")
    vm = _mm(mem, g_a, form="nn", b_pack="w_mem_v", out_dtype=BF16, name="mm_mem_v")
    (om,) = _rowwise(_fn_mem_fwd, [qm], [km, vm], [(D_MODEL, BF16)], tr=tr, name="mem_fwd")
    xa = _mm(om, g_a, form="nn", b_pack="w_mem_o", name="mm_mem_o")
    h2, h2_b = _rowwise(_fn_res_ln, [h1, xa], [g2, b2], [D_MODEL, (D_MODEL, BF16)], tr=tr, name="ln2")
    u = _mm(h2_b, g_a, form="nn", b_pack="w_up", out_dtype=BF16, name="mm_up")
    ff = _mm(u, g_a, form="nn", a_pro=_relu2, b_pack="w_down", name="mm_down")

    gp = lax.empty((N_SHARD, PACK_A_ROWS, PACK_COLS), F32)
    dt3, dt3_b, dg3, db3, loss = _rowwise(_fn_final, [h2, ff, target], [g3, b3], [D_MODEL, (D_MODEL, BF16)],
                                   [(1, D_MODEL), (1, D_MODEL), (1, 128)], tr=tr, name="ln3_loss")
    du = _mm(dt3_b, g_a, form="nt", b_pack="w_down", epi=(_epi_du, u), out_dtype=BF16, name="mm_down_dx")
    gp = _mm(u, dt3_b, form="tn", a_pro=_relu2, out_pack=("w_down", gp), name="mm_down_dw")
    gp = _mm(h2_b, du, form="tn", out_pack=("w_up", gp), name="mm_up_dw")
    dh2 = _mm(du, g_a, form="nt", b_pack="w_up", name="mm_up_dx")
    dt2, dg2, db2 = _rowwise(_fn_res_ln_bwd, [h1, xa, dt3, dh2], [g2], [D_MODEL], [(1, D_MODEL)] * 2, tr=tr,
                             name="ln2_bwd")
    dom = _mm(dt2, g_a, form="nt", b_pack="w_mem_o", out_dtype=BF16, name="mm_mem_o_dx")
    gp = _mm(om, dt2, form="tn", out_pack=("w_mem_o", gp), name="mm_mem_o_dw")
    dqm, dkm, dvm = _rowwise(_fn_mem_bwd, [qm, dom], [km, vm], [(D_MODEL, BF16)], [(256, D_MODEL)] * 2, tr=tr,
                             name="mem_bwd")
    gp = _mm(h1, dqm, form="tn", out_pack=("w_mem_q", gp), name="mm_mem_q_dw")
    gp = _mm(mem, dkm, form="tn", out_pack=("w_mem_k", gp), name="mm_mem_k_dw")
    gp = _mm(mem, dvm, form="tn", out_pack=("w_mem_v", gp), name="mm_mem_v_dw")
    dh1 = _mm(dqm, g_a, form="nt", b_pack="w_mem_q", name="mm_mem_q_dx")
    dt1, dg1, db1 = _rowwise(_fn_res2_ln_bwd, [h0, mix_o, mix_y, dt2, dh1], [g1], [D_MODEL], [(1, D_MODEL)] * 2,
                             tr=tr, name="ln1_bwd")
    do_t = _mm(w_mix_o, dt1, form="nt", name="mm_mix_o_dx")
    dy_n = _mm(dt1, g_a, form="nt", b_pack="w_mix_out", b_rows=512, name="mm_mix_y_dx")
    dw_mix_o = _mm(o_t, dt1, form="nn", name="mm_mix_o_dw")
    gp = _mm(y_n, dt1, form="tn", out_pack=("w_mix_out", gp), name="mm_mix_y_dw")
    gp = lax.dynamic_update_slice(
        gp, dw_mix_o.reshape(MLA_HEADS, 128, D_MODEL)[:, :64].reshape(2, 256, D_MODEL), (2, r_mix, 0))
    me, c = _mesh_pos() if dist else (0, 0)
    ha = PACK_A_ROWS // 2
    res = _attn_bwd_dq(q_rot, k_full, v_nat, o_t, do_t, lse, name="attn_bwd_dq",
                       hosted=_pair_exchange_step(gp) if dist else None)
    dq_rot, delta = res[0], res[1]
    chip_step = None
    if dist:
        theirs_a = res[2]
        chip_step = _chip_exchange_step(_pair_sum(gp, theirs_a, "pair_sum_a"))
    res = _attn_bwd_dkv(q_rot, k_full, v_nat, do_t, lse, delta, name="attn_bwd_dkv", hosted=chip_step)
    dk, dv_t = res[0], res[1]
    if dist:
        gp = _chip_sum(gp, theirs_a, res[2], "chip_sum_a")
    dqp, dkr = _rowwise(_fn_rope_bwd, [dq_rot, dk, rope_a, rope_b, rope_c], [], [(1024, BF16), (128, BF16)], tr=tr,
                        name="rope_bwd")
    dw_q_p = _mm(q_n, dqp, form="tn", name="mm_q_up_dw")
    dq_n = _mm(dqp, w_q_p, form="nt", name="mm_q_up_dx")
    dw_k_p = _mm(kv_n, dk, form="tn", name="mm_k_up_dw")
    dkv_n1 = _mm(dk, w_k_p, form="nt", name="mm_k_up_dx")
    dw_v_pt = _mm(dv_t, kv_n, form="nn", name="mm_v_up_dw")
    dkv_n2 = _mm(dv_t, w_v_pt, form="tn", name="mm_v_up_dx")
    dq_lat, dkv_lat, dg_q, dg_kv = _rowwise(
        _fn_mla_pre_bwd, [(proj,) + SEG_QLAT, (proj,) + SEG_KVLAT, dq_n, dkv_n1, dkv_n2], [g_q, g_kv], [(384, BF16), (256, BF16)],
        [(1, 384), (1, 256)], tr=tr, name="mla_pre_bwd")
    dy_ssd, dz, dxs_skip, ddexp, dg_ssd = _rowwise(
        _fn_ssd_post_bwd, [dy_n, y_ssd, (xbc, 0, 512), (proj,) + SEG_Z], [dexp, g_ssd],
        [512, (512, BF16), 512], [(1, 512)] * 2, tr=tr, name="ssd_post_bwd")
    dxs, dbc, ddt, da_head = _ssd_bwd(xbc, dt, a_row, hs, dy_ssd, name="ssd_bwd")
    dyc, ddtr, dconv_b, ddt_b = _rowwise(
        _fn_conv_bwd_a, [conv_y, dxs, dxs_skip, dbc, (proj,) + SEG_DT, ddt], [dt_b], [1024, (128, BF16)],
        [(1, 1024), (1, 128)], tr=tr, name="conv_bwd_a")
    dxbc, dconv_w8 = _rowwise(
        _fn_conv_bwd_b, [dyc, ("next", dyc, 0, 1024), (proj,) + SEG_XBC, ("prev", proj) + SEG_XBC], [conv_w8], [(1024, BF16)],
        [(8, 1024)], tr=tr, name="conv_bwd_b")
    dproj = jnp.concatenate([dxbc, dz, dq_lat, ddtr, dkv_lat, dkr, jnp.zeros((S, 128), BF16)], axis=1)
    res = _mm(h0_b, dproj, form="tn", tn=IN_W // 2, name="mm_in_dw", hosted=_pair_fill_step(gp) if dist else None)
    dw_in_p, red_a = (res[0], res[1]) if dist else (res, None)
    big_b = _group_b_grads(dw_in_p, dw_q_p, dw_k_p, dw_v_pt, dconv_w8)
    q_b = None
    if dist:
        gp_c, gp_b = _pack_group_b(big_b)
        dh0, theirs_c, theirs_b = _mm(dproj, w_in_p, form="nt", tk=IN_W // 2, name="mm_in_dx", hosted=_merge_steps(
            [_pair_exchange_step(gp_c), _pair_exchange_step(gp_b)]))
        q_b = ((gp_c, theirs_c, _pair_sum(gp_c, theirs_c, "pair_sum_w_in")),
               (gp_b, theirs_b, _pair_sum(gp_b, theirs_b, "pair_sum_b")))
        gp = red_a
    else:
        dh0 = _mm(dproj, w_in_p, form="nt", tk=IN_W // 2, name="mm_in_dx")
    grad_x, dg_in, db_in = _rowwise(_fn_in_ln_bwd, [x, dt1, dh0], [g_in], [D_MODEL], [(1, D_MODEL)] * 2, tr=tr,
                                    name="ln_in_bwd")

    small = {
        "ln_in_g": dg_in, "ln_in_b": db_in, "conv_b": dconv_b, "dt_bias": ddt_b[:, :8],
        "a_log": da_head[:, :8] * a_head.reshape(1, 8),
        "d_skip": ddexp.reshape(8, 64).sum(axis=1).reshape(1, 8),
        "ssd_norm_g": dg_ssd, "q_norm_g": dg_q, "kv_norm_g": dg_kv,
        "ln1_g": dg1, "ln1_b": db1, "ln2_g": dg2, "ln2_b": db2, "ln3_g": dg3, "ln3_b": db3,
    }
    return loss[0, 0], grad_x, (gp, q_b), big_b, small


def _group_b_grads(dw_in_p, dw_q_p, dw_k_p, dw_v_pt, dconv_w8):
    return {
        "w_in": _unpad_w_in(dw_in_p),
        "w_q_up": dw_q_p.reshape(384, MLA_HEADS, 128)[:, :, :MLA_QK].reshape(384, MLA_HEADS * MLA_QK),
        "w_kv_up": jnp.concatenate([dw_k_p.reshape(MLA_KV_RANK, MLA_HEADS, 128)[:, :, :64],
                                    dw_v_pt.T.reshape(MLA_KV_RANK, MLA_HEADS, 128)[:, :, :64]], axis=2).reshape(
                                        MLA_KV_RANK, MLA_HEADS * 128),
        "conv_w": dconv_w8[0:4],
    }


def _pack_group_b(big_b):
    gflat = [_split_shards(n, big_b[n]) for n in PACK_B_ORDER]
    used = sum(f.shape[1] for f in gflat)
    gflat.append(jnp.zeros((N_SHARD, PACK_B_ROWS * PACK_COLS - used), F32))
    return big_b["w_in"], jnp.concatenate(gflat, axis=1).reshape(N_SHARD, PACK_B_ROWS, PACK_COLS)


def _half_tile(h):
    return next(t for t in range(512, 0, -16) if h % t == 0)


def _pair_sum(gp, theirs, name):
    n, R, C = gp.shape
    H = R // 2
    tr = _half_tile(H)
    nb = H // tr

    def body(s_ref, g_ref, t_ref, o_ref):
        o_ref[...] = (g_ref[...] + t_ref[...]).astype(o_ref.dtype)

    def shard(k, s):
        return k + (k >= s[1]).astype(jnp.int32)

    me, c = _mesh_pos()
    return pl.pallas_call(
        body, name=name,
        grid_spec=pltpu.PrefetchScalarGridSpec(
            num_scalar_prefetch=1, grid=(n - 1, nb),
            in_specs=[pl.BlockSpec((1, tr, C), lambda k, i, s: (shard(k, s), s[0] * nb + i, 0)),
                      pl.BlockSpec((1, tr, C), lambda k, i, s: (shard(k, s), i, 0))],
            out_specs=pl.BlockSpec((1, tr, C), lambda k, i, s: (shard(k, s), i, 0))),
        out_shape=jax.ShapeDtypeStruct((n, H, C), BF16), compiler_params=_params(("arbitrary", "arbitrary")),
    )(jnp.stack([c, me]).astype(jnp.int32), gp, theirs)


def _chip_sum(gp, theirs, got, name):
    n, R, C = gp.shape
    H = R // 2
    tr = _half_tile(H)
    nb = H // tr

    def body(s_ref, g_ref, t_ref, r_ref, o_ref):
        acc = g_ref[0] + t_ref[0]
        for j in range(3):
            acc = acc + r_ref[j].astype(F32)
        o_ref[...] = acc

    me, c = _mesh_pos()
    return pl.pallas_call(
        body, name=name,
        grid_spec=pltpu.PrefetchScalarGridSpec(
            num_scalar_prefetch=1, grid=(nb,),
            in_specs=[pl.BlockSpec((1, tr, C), lambda i, s: (s[0], s[1] * nb + i, 0)),
                      pl.BlockSpec((1, tr, C), lambda i, s: (s[0], i, 0)),
                      pl.BlockSpec((3, tr, C), lambda i, s: (0, i, 0))],
            out_specs=pl.BlockSpec((tr, C), lambda i, s: (s[1] * nb + i, 0))),
        out_shape=jax.ShapeDtypeStruct((R, C), F32), compiler_params=_params(("arbitrary",)),
    )(jnp.stack([me, c]).astype(jnp.int32), gp, theirs, got)


def _pair_fill_step(red):
    R, C = red.shape
    H = R // 2
    nq = _chunks(H, 8)
    CH = H // nq

    def copies(ins, outs, sems):
        x, y, c, _ = _place()
        return [_remote(ins[0].at[pl.ds(c * H + j * CH, CH), :], outs[0].at[pl.ds(c * H + j * CH, CH), :], sems, j,
                        (x, y, 1 - c)) for j in range(nq)]

    def start(ins, outs, sems):
        for cp in copies(ins, outs, sems):
            cp.start()

    def finish(ins, outs, sems):
        for cp in copies(ins, outs, sems):
            cp.wait()

    step = _Step([red], [jax.ShapeDtypeStruct((R, C), red.dtype)], nq, start, finish)
    step.alias = [(0, 0)]
    return step


def _reduce_scatter(gp, tag):
    n, R, C = gp.shape
    H = R // 2
    me, c = _mesh_pos()
    (theirs,) = _run_step(_pair_exchange_step(gp), "pair_exchange_" + tag)
    mine = lax.dynamic_slice(gp, (0, c * H, 0), (n, H, C))
    pf, pb = _rowwise(_fn_add2, [mine.reshape(-1, C), theirs.reshape(-1, C)], [], [C, (C, BF16)], tr=512,
                      name="pair_sum_" + tag)
    (got,) = _run_step(_chip_exchange_step(pb.reshape(n, H, C)), "chip_exchange_" + tag)
    own = lax.dynamic_index_in_dim(pf.reshape(n, H, C), me, axis=0, keepdims=False)
    got = got.reshape(3 * H, C)
    (q,) = _rowwise(_fn_add4, [own] + [(got, 0, C, j * H) for j in range(3)], [], [C], tr=512,
                    name="chip_sum_" + tag, n_rows=H)
    return q


def _adam(w, g, m, v, name):
    shape = w.shape
    w2, m2, v2 = (t.reshape(-1, shape[-1]) for t in (w, m, v))
    g2 = (g[0], 0, shape[-1], g[1]) if isinstance(g, tuple) else g.reshape(-1, shape[-1])
    d, mn, vn = _rowwise(_fn_adam, [w2, g2, m2, v2], [], [shape[-1]] * 3, tr=256, name=name)
    return d.reshape(shape), mn.reshape(shape), vn.reshape(shape)


def kernel(x, mem, positions, ln_in_g, ln_in_b, w_in, conv_w, conv_b, dt_bias, a_log, d_skip, ssd_norm_g, q_norm_g, w_q_up, kv_norm_g, w_kv_up, w_mix_out, ln1_g, ln1_b, w_mem_q, w_mem_k, w_mem_v, w_mem_o, ln2_g, ln2_b, w_up, w_down, ln3_g, ln3_b, loss_target, m_ln_in_g, m_ln_in_b, m_w_in, m_conv_w, m_conv_b, m_dt_bias, m_a_log, m_d_skip, m_ssd_norm_g, m_q_norm_g, m_w_q_up, m_kv_norm_g, m_w_kv_up, m_w_mix_out, m_ln1_g, m_ln1_b, m_w_mem_q, m_w_mem_k, m_w_mem_v, m_w_mem_o, m_ln2_g, m_ln2_b, m_w_up, m_w_down, m_ln3_g, m_ln3_b, v_ln_in_g, v_ln_in_b, v_w_in, v_conv_w, v_conv_b, v_dt_bias, v_a_log, v_d_skip, v_ssd_norm_g, v_q_norm_g, v_w_q_up, v_kv_norm_g, v_w_kv_up, v_w_mix_out, v_ln1_g, v_ln1_b, v_w_mem_q, v_w_mem_k, v_w_mem_v, v_w_mem_o, v_ln2_g, v_ln2_b, v_w_up, v_w_down, v_ln3_g, v_ln3_b):
    args = dict(locals())
    me, c = _mesh_pos()

    wp_a = jnp.concatenate([args[n].reshape(-1, PACK_COLS).astype(BF16) for n in PACK_A_ORDER], axis=0)
    flat = [args[n].reshape(-1).astype(BF16) for n in PACK_B_ORDER[:-1]]
    flat.append(lax.bitcast_convert_type(conv_w.reshape(-1), BF16).reshape(-1))
    used = sum(f.shape[0] for f in flat)
    flat.append(jnp.zeros((PACK_B_ROWS * PACK_COLS - used,), BF16))
    wp_b = jnp.concatenate(flat).reshape(PACK_B_ROWS, PACK_COLS)

    wp_c = w_in[0].astype(BF16)

    P = {n: args[n] for n in SMALL_ORDER}
    loss, grad_x, (red_a, ((gp_c, theirs_c, pb_c), (gp_b, theirs_b, pb_b))), _, gsmall = _local_step(
        x[0], mem[0], positions[0], loss_target[0], None, P, wp_a=wp_a, wp_b=(wp_c, wp_b))

    gs = jnp.concatenate([_row(gsmall[n], PACK_COLS) for n in SMALL_ORDER] + [_row(loss, PACK_COLS)], axis=0)
    gs, got_c, got_b = _small_all_reduce(gs, _merge_steps([_chip_exchange_step(pb_c), _chip_exchange_step(pb_b)]))
    loss = gs[len(SMALL_ORDER), 0]
    red_c, red_b = _run_step(_merge_steps([_pair_fill_step(_chip_sum(gp_c, theirs_c, got_c, "chip_sum_w_in")),
                                           _pair_fill_step(_chip_sum(gp_b, theirs_b, got_b, "chip_sum_b"))]),
                             "pair_fill_b")

    grads, deltas, new_m, new_v = {}, {}, {}, {}
    for n in PACK_A_ORDER:
        r0, (sr, _) = PACK_A_ROW[n], _shard_shape(n)
        grads[n] = red_a[r0:r0 + sr].reshape(args[n].shape)
        deltas[n], new_m[n], new_v[n] = _adam(args[n], (red_a, r0), args["m_" + n], args["v_" + n], "adam_" + n)
    grads["w_in"] = red_c.reshape(w_in.shape)
    deltas["w_in"], new_m["w_in"], new_v["w_in"] = _adam(w_in, grads["w_in"], m_w_in, v_w_in, "adam_w_in")
    red_b = red_b.reshape(-1)
    off = 0
    for n in PACK_B_ORDER:
        sr, sc = _shard_shape(n)
        grads[n] = red_b[off:off + sr * sc].reshape(args[n].shape)
        off += sr * sc
        deltas[n], new_m[n], new_v[n] = _adam(args[n], grads[n], args["m_" + n], args["v_" + n], "adam_" + n)
    pack = lambda pre: jnp.concatenate([_row(args[pre + n], PACK_COLS) for n in SMALL_ORDER]
                                       + [jnp.zeros((1, PACK_COLS), F32)], axis=0)
    ds, ms, vs = _rowwise(_fn_adam, [pack(""), gs, pack("m_"), pack("v_")], [], [PACK_COLS] * 3, tr=16,
                          name="adam_small")
    for i, n in enumerate(SMALL_ORDER):
        cnt = args[n].size
        take = lambda t: t[i, :cnt].reshape(args[n].shape)
        grads[n], deltas[n], new_m[n], new_v[n] = take(gs), take(ds), take(ms), take(vs)

    order = ["ln_in_g", "ln_in_b", "w_in", "conv_w", "conv_b", "dt_bias", "a_log", "d_skip", "ssd_norm_g",
             "q_norm_g", "w_q_up", "kv_norm_g", "w_kv_up", "w_mix_out", "ln1_g", "ln1_b", "w_mem_q", "w_mem_k",
             "w_mem_v", "w_mem_o", "ln2_g", "ln2_b", "w_up", "w_down", "ln3_g", "ln3_b"]
    return (loss, grad_x[None], *[grads[n] for n in order], *[deltas[n] for n in order],
            *[new_m[n] for n in order], *[new_v[n] for n in order])
```

```python
import functools

import jax
import jax.numpy as jnp
from jax import lax
from jax.experimental import pallas as pl
from jax.experimental.pallas import tpu as pltpu

F32 = jnp.float32
BF16 = jnp.bfloat16
MESH = pl.DeviceIdType.MESH

D_MODEL = 1024
SSD_CHUNK = 128
SSD_STATE = 128
MLA_HEADS = 8
MLA_ROPE = 32
MLA_QK = 96
MLA_KV_RANK = 256
ROPE_THETA = 10000.0
MEM_HEADS = 4
MEM_HEAD_DIM = 256
LN_EPS = 1e-5
RMS_EPS = 1e-6
ALPHA = 2.0 ** 0.25
ADAM_LR = 0.001
ADAM_B1 = 0.9
ADAM_B2 = 0.999
ADAM_EPS = 1e-08
ADAM_WD = 0.01
ADAM_STEP = 10

LANES = 128
IN_W = 2560
SEG_XBC = (0, 1024)
SEG_Z = (1024, 512)
SEG_QLAT = (1536, 384)
SEG_DT = (1920, 128)
SEG_KVLAT = (2048, 256)
SEG_KR = (2304, 128)
VMEM_LIMIT = 56 * 1024 * 1024
ATTN_TILE = 512
ROW_TILE = 512
SSD_PER_STEP = 2
NEG = -1e30
MLA_SCALE = MLA_QK ** -0.5
MEM_SCALE = MEM_HEAD_DIM ** -0.5

NN = (((1,), (0,)), ((), ()))
NT = (((1,), (1,)), ((), ()))
TN = (((0,), (0,)), ((), ()))


def _dot(a, b, dims=NN):
    return lax.dot_general(a.astype(BF16), b.astype(BF16), dims, preferred_element_type=F32)


def _dot_exact(a, b):
    return lax.dot_general(a, b, NN, precision=lax.Precision.HIGHEST, preferred_element_type=F32)


def _pick(dim, pref):
    t = min(pref, dim)
    t -= t % LANES
    while t >= LANES:
        if dim % t == 0:
            return t
        t -= LANES
    return dim


def _params(sem):
    return pltpu.CompilerParams(dimension_semantics=sem, vmem_limit_bytes=VMEM_LIMIT)


def _pack_caps(wname):
    r, c, ax = BIG[wname]
    if ax == 0:
        return (r if r <= 1024 else r // N_SHARD), c
    return r, c // N_SHARD


def _pack_block(wname, br, bc):
    r, c, ax = BIG[wname]
    r0 = PACK_A_ROW[wname]
    sr = r // N_SHARD if ax == 0 else r
    if ax == 0 and br > sr:
        assert br % sr == 0 and r0 % sr == 0
        return (br // sr, sr, bc), lambda rb, cb: (rb, r0 // sr, cb)
    assert r0 % br == 0
    if ax == 0:
        per = sr // br
        return (1, br, bc), lambda rb, cb: (rb // per, r0 // br + rb % per, cb)
    per = (c // N_SHARD) // bc
    return (1, br, bc), lambda rb, cb: (cb // per, r0 // br + rb, cb % per)


def _mm(a, b, *, form, name, a_pro=None, epi=None, out_dtype=F32, tm=1024, tn=1024, tk=1024, b_pack=None,
        b_rows=None, out_pack=None, hosted=None):
    b_shape = BIG[b_pack][:2] if b_pack else b.shape
    if b_pack and form == "nt":
        b_shape = (b_rows or b_shape[0], b_shape[1])
    if form == "nn":
        (m, k), (_, n) = a.shape, b_shape
    elif form == "nt":
        (m, k), (n, _) = a.shape, b_shape
    else:
        (k, m), (_, n) = a.shape, b_shape
    if b_pack:
        rcap, ccap = _pack_caps(b_pack)
        tk, tn = (min(tk, rcap), min(tn, ccap)) if form == "nn" else (min(tk, ccap), min(tn, rcap))
    if out_pack:
        rcap, ccap = _pack_caps(out_pack[0])
        tm, tn = min(tm, rcap), min(tn, ccap)
    tm, tn, tk = _pick(m, tm), _pick(n, tn), _pick(k, tk)
    dims = {"nn": NN, "nt": NT, "tn": TN}[form]
    nk = k // tk
    direct = out_dtype == F32 and epi is None
    n_extra = (1 if epi else 0) + (1 if out_pack else 0)

    def body(a_ref, b_ref, *rest):
        o_ref = rest[n_extra]
        acc_ref = o_ref if direct else rest[-1]

        @pl.when(pl.program_id(2) == 0)
        def _():
            acc_ref[...] = jnp.zeros_like(acc_ref)

        av = a_ref[...]
        if a_pro is not None:
            av = a_pro(av)
        bv = b_ref[...]
        acc_ref[...] += _dot(av, bv.reshape(-1, bv.shape[-1]), dims).reshape(acc_ref.shape)
        if not direct:
            @pl.when(pl.program_id(2) == nk - 1)
            def _():
                val = acc_ref[...]
                if epi is not None:
                    val = epi[0](val, rest[0][...])
                o_ref[...] = val.reshape(o_ref.shape).astype(o_ref.dtype)

    if form == "tn":
        a_spec = pl.BlockSpec((tk, tm), lambda i, j, kk: (kk, i))
    else:
        a_spec = pl.BlockSpec((tm, tk), lambda i, j, kk: (i, kk))
    if b_pack:
        shape, idx = _pack_block(b_pack, *((tk, tn) if form == "nn" else (tn, tk)))
        b_spec = pl.BlockSpec(shape, (lambda i, j, kk: idx(kk, j)) if form == "nn" else (lambda i, j, kk: idx(j, kk)))
    elif form == "nt":
        b_spec = pl.BlockSpec((tn, tk), lambda i, j, kk: (j, kk))
    else:
        b_spec = pl.BlockSpec((tk, tn), lambda i, j, kk: (kk, j))
    in_specs, args = [a_spec, b_spec], [a, b]
    out_spec = pl.BlockSpec((tm, tn), lambda i, j, kk: (i, j))
    out_sds, aliases = jax.ShapeDtypeStruct((m, n), out_dtype), {}
    if epi is not None:
        in_specs.append(out_spec)
        args.append(epi[1])
    if out_pack:
        wname, buf = out_pack
        shape, idx = _pack_block(wname, tm, tn)
        out_spec = pl.BlockSpec(shape, lambda i, j, kk: idx(i, j))
        out_sds, aliases = jax.ShapeDtypeStruct(buf.shape, buf.dtype), {len(args): 0}
        in_specs.append(HBM)
        args.append(buf)
    acc_shape = out_spec.block_shape if out_pack else (tm, tn)
    res = _call_with_step(
        body, hosted, None, args, name=name, grid=(m // tm, n // tn, nk), in_specs=in_specs, out_specs=[out_spec],
        out_shape=[out_sds], sem=("parallel", "parallel", "arbitrary"), aliases=aliases,
        scratch_shapes=[] if direct else [pltpu.VMEM(acc_shape, F32)])
    return res[0] if hosted is None else res


class _Ctx:
    def __init__(self, i, n):
        self.i, self.n = i, n


def _rowwise(fn, rows, consts, row_outs, acc_outs=(), *, tr, name, n_rows=None, hosted=None):
    norm = []
    for r in rows:
        kind = "tile"
        if isinstance(r, tuple) and isinstance(r[0], str):
            kind, r = r[0], r[1:]
        row0 = 0
        if isinstance(r, tuple) and len(r) == 4:
            r, row0 = r[:3], r[3]
        arr, col0, width = r if isinstance(r, tuple) else (r, 0, r.shape[1])
        assert col0 % width == 0
        norm.append((kind, arr, col0 // width, width, row0))
    n_rows = n_rows or next(a.shape[0] for k, a, _, _, _ in norm if k == "tile")
    tr = min(tr, n_rows)
    while n_rows % tr:
        tr -= 8
    n = n_rows // tr
    arrs, specs = [], []
    for kind, arr, cb, width, row0 in norm:
        if kind == "tile":
            assert row0 % tr == 0
            specs.append(pl.BlockSpec((tr, width), lambda i, cb=cb, rb=row0 // tr: (i + rb, cb)))
        elif kind == "prev":
            specs.append(pl.BlockSpec((8, width), lambda i, cb=cb: (jnp.maximum(i * (tr // 8) - 1, 0), cb)))
        else:
            specs.append(pl.BlockSpec((8, width), lambda i, cb=cb: (jnp.minimum((i + 1) * (tr // 8), n_rows // 8 - 1), cb)))
        arrs.append(arr)
    for c in consts:
        specs.append(pl.BlockSpec(c.shape, lambda i, nd=c.ndim: (0,) * nd))
        arrs.append(c)
    n_in, n_ro = len(arrs), len(row_outs)
    row_outs = [w if isinstance(w, tuple) else (w, F32) for w in row_outs]
    out_shape = [jax.ShapeDtypeStruct((n_rows, w), dt) for w, dt in row_outs]
    out_specs = [pl.BlockSpec((tr, w), lambda i: (i, 0)) for w, _ in row_outs]
    out_shape += [jax.ShapeDtypeStruct(s, F32) for s in acc_outs]
    out_specs += [pl.BlockSpec(s, lambda i: (0, 0)) for s in acc_outs]

    def body(*refs):
        i = pl.program_id(0)
        vals = [r[...] for r in refs[:n_in]]
        outs = fn(_Ctx(i, n), *vals)
        if not isinstance(outs, (tuple, list)):
            outs = (outs,)
        o_refs = refs[n_in:]
        for o_ref, o in zip(o_refs[:n_ro], outs[:n_ro]):
            o_ref[...] = o.astype(o_ref.dtype)
        if acc_outs:
            @pl.when(i == 0)
            def _():
                for o_ref in o_refs[n_ro:]:
                    o_ref[...] = jnp.zeros_like(o_ref)

            for o_ref, o in zip(o_refs[n_ro:], outs[n_ro:]):
                o_ref[...] += jnp.broadcast_to(o, o_ref.shape)

    return _call_with_step(body, hosted, None, arrs, name=name, grid=(n,), in_specs=specs, out_specs=out_specs,
                           out_shape=out_shape, sem=("arbitrary",))


def _sum0(v):
    return jnp.sum(v, axis=0, keepdims=True)


def _mean1(v):
    return jnp.mean(v, axis=-1, keepdims=True)


def _sigmoid(v):
    return 1.0 / (1.0 + jnp.exp(-v))


def _ln_stats(t):
    xc = t - _mean1(t)
    rstd = lax.rsqrt(_mean1(xc * xc) + LN_EPS)
    return xc * rstd, rstd


def _ln_bwd(xhat, rstd, dy, g):
    dxh = dy * g
    dx = rstd * (dxh - _mean1(dxh) - xhat * _mean1(dxh * xhat))
    return dx, _sum0(dy * xhat), _sum0(dy)


def _rms_fwd(v, g):
    return v * lax.rsqrt(_mean1(v * v) + RMS_EPS) * g


def _rms_bwd(v, dy, g):
    rs = lax.rsqrt(_mean1(v * v) + RMS_EPS)
    vh = v * rs
    dyg = dy * g
    return rs * (dyg - vh * _mean1(dyg * vh)), _sum0(dy * vh)


def _lane(shape):
    return lax.broadcasted_iota(jnp.int32, shape, len(shape) - 1)


def _shift_down(u, halo, s, is_first):
    tr = u.shape[0]
    rolled = pltpu.roll(u, s, 0)
    hr = jnp.where(is_first, 0.0, pltpu.roll(halo, s, 0))
    row = lax.broadcasted_iota(jnp.int32, hr.shape, 0)
    top = jnp.where(row < s, hr, rolled[0:8])
    if tr == 8:
        return top
    return jnp.concatenate([top, rolled[8:]], axis=0)


def _shift_up(d, halo, s, is_last):
    tr = d.shape[0]
    rolled = pltpu.roll(d, tr - s, 0)
    hr = jnp.where(is_last, 0.0, pltpu.roll(halo, 8 - s, 0))
    row = lax.broadcasted_iota(jnp.int32, hr.shape, 0)
    bot = jnp.where(row >= 8 - s, hr, rolled[tr - 8:])
    if tr == 8:
        return bot
    return jnp.concatenate([rolled[:tr - 8], bot], axis=0)


def _rope(v, ta, tb, tc):
    return v * ta + pltpu.roll(v, 16, 1) * tb + pltpu.roll(v, LANES - 16, 1) * tc


def _rope_bwd(d, ta, tb, tc):
    return d * ta + pltpu.roll(d * tb, LANES - 16, 1) + pltpu.roll(d * tc, 16, 1)


def _ssd_common(dtv, a_row):
    L = SSD_CHUNK
    a = dtv * a_row
    r = lax.broadcasted_iota(jnp.int32, (L, L), 0)
    c = lax.broadcasted_iota(jnp.int32, (L, L), 1)
    tril = r >= c
    cs = _dot_exact(tril.astype(F32), a)
    cs_t = cs.T
    cs_last = cs[L - 1:L, :]
    return dict(a=a, tril=tril, cs=cs, cs_t=cs_t, ecs=jnp.exp(cs), dte=jnp.exp(cs_last - cs),
                elast=jnp.exp(cs_last))


def _pair_sel(v, h0, lo):
    return jnp.where(lo, v[:, h0:h0 + 1], v[:, h0 + 1:h0 + 2])


def _ssd_pair(cm, h0, cb, xp, dtv, bmat, cmat, hp, lo):
    x = xp * _pair_sel(dtv, h0, lo)
    lam0 = jnp.exp(jnp.where(cm["tril"], cm["cs"][:, h0:h0 + 1] - cm["cs_t"][h0:h0 + 1, :], NEG))
    lam1 = jnp.exp(jnp.where(cm["tril"], cm["cs"][:, h0 + 1:h0 + 2] - cm["cs_t"][h0 + 1:h0 + 2, :], NEG))
    m0, m1 = cb * lam0, cb * lam1
    ydiag = jnp.where(lo, _dot(m0, x), _dot(m1, x))
    ecs_p = _pair_sel(cm["ecs"], h0, lo)
    dte_p = _pair_sel(cm["dte"], h0, lo)
    yoff = _dot(cmat, hp, NT) * ecs_p
    xd = x * dte_p
    st = _dot(xd, bmat, TN)
    rlo = lax.broadcasted_iota(jnp.int32, (LANES, SSD_STATE), 0) < 64
    decay = jnp.where(rlo, cm["elast"][:, h0:h0 + 1], cm["elast"][:, h0 + 1:h0 + 2])
    h_next = hp * decay + st
    return dict(x=x, lam0=lam0, lam1=lam1, m0=m0, m1=m1, y=ydiag + yoff, yoff=yoff, ecs_p=ecs_p, dte_p=dte_p,
                xd=xd, decay=decay, h_next=h_next)


def _ssd_fwd(xbc, dt, a_row, *, name):
    S = xbc.shape[0]
    L = SSD_CHUNK
    nc = S // L
    per = SSD_PER_STEP if nc % SSD_PER_STEP == 0 else 1
    G = per * L

    def body(xs_ref, bm_ref, cm_ref, dt_ref, a_ref, y_ref, hs_ref, h_scr):
        @pl.when(pl.program_id(0) == 0)
        def _():
            h_scr[...] = jnp.zeros_like(h_scr)

        lo = _lane((L, LANES)) < 64
        for sub in range(per):
            rows = slice(sub * L, (sub + 1) * L)
            dtv = dt_ref[rows, :]
            cm = _ssd_common(dtv, a_ref[...])
            ys = []
            for g in range(2):
                bmat = bm_ref[rows, g * 128:(g + 1) * 128]
                cmat = cm_ref[rows, g * 128:(g + 1) * 128]
                cb = _dot(cmat, bmat, NT)
                for pr in range(2):
                    p4 = 2 * g + pr
                    hp = h_scr[p4]
                    hs_ref[sub, p4 * 128:(p4 + 1) * 128, :] = hp
                    t = _ssd_pair(cm, 2 * p4, cb, xs_ref[rows, p4 * 128:(p4 + 1) * 128], dtv, bmat, cmat, hp, lo)
                    ys.append(t["y"])
                    h_scr[p4] = t["h_next"]
            y_ref[rows, :] = jnp.concatenate(ys, axis=1)

    return pl.pallas_call(
        body, name=name, grid=(nc // per,),
        in_specs=[pl.BlockSpec((G, 512), lambda c: (c, 0)), pl.BlockSpec((G, 256), lambda c: (c, 2)),
                  pl.BlockSpec((G, 256), lambda c: (c, 3)), pl.BlockSpec((G, 128), lambda c: (c, 0)),
                  pl.BlockSpec((1, 128), lambda c: (0, 0))],
        out_specs=[pl.BlockSpec((G, 512), lambda c: (c, 0)), pl.BlockSpec((per, 512, 128), lambda c: (c, 0, 0))],
        out_shape=[jax.ShapeDtypeStruct((S, 512), F32), jax.ShapeDtypeStruct((nc, 512, 128), F32)],
        scratch_shapes=[pltpu.VMEM((4, 128, 128), F32)],
        compiler_params=_params(("arbitrary",)),
    )(xbc, xbc, xbc, dt, a_row)


def _ssd_bwd(xbc, dt, a_row, hs, dy, *, name):
    S = xbc.shape[0]
    L = SSD_CHUNK
    nc = S // L
    per = SSD_PER_STEP if nc % SSD_PER_STEP == 0 else 1
    G = per * L

    def body(xs_ref, bm_ref, cm_ref, dt_ref, a_ref, hs_ref, dy_ref, dxs_ref, dbc_ref, ddt_ref, da_ref, g_scr):
        @pl.when(pl.program_id(0) == 0)
        def _():
            g_scr[...] = jnp.zeros_like(g_scr)
            da_ref[...] = jnp.zeros_like(da_ref)

        for sub in reversed(range(per)):
            rows = pl.ds(sub * L, L)
            chunk(xs_ref.at[rows, :], bm_ref.at[rows, :], cm_ref.at[rows, :], dt_ref.at[rows, :], a_ref,
                  hs_ref.at[pl.ds(sub, 1)], dy_ref.at[rows, :], dxs_ref.at[rows, :], dbc_ref.at[rows, :],
                  ddt_ref.at[rows, :], da_ref, g_scr)

    def chunk(xs_ref, bm_ref, cm_ref, dt_ref, a_ref, hs_ref, dy_ref, dxs_ref, dbc_ref, ddt_ref, da_ref, g_scr):
        dtv = dt_ref[...]
        a_row_v = a_ref[...]
        cm = _ssd_common(dtv, a_row_v)
        lo = _lane((L, LANES)) < 64
        lane_row = _lane((1, LANES))
        ri = lax.broadcasted_iota(jnp.int32, (L, L), 0)
        ci = lax.broadcasted_iota(jnp.int32, (L, L), 1)
        triu = (ri <= ci).astype(F32)
        stril = ri > ci

        def halves(v, mask):
            return (jnp.sum(jnp.where(mask, v, 0.0), axis=1, keepdims=True),
                    jnp.sum(jnp.where(mask, 0.0, v), axis=1, keepdims=True))

        i_all = jnp.zeros((L, LANES), F32)
        yo_all = jnp.zeros((L, LANES), F32)
        w_all = jnp.zeros((L, LANES), F32)
        ddt_x = jnp.zeros((L, LANES), F32)
        e_row = jnp.zeros((1, LANES), F32)
        rlo = lax.broadcasted_iota(jnp.int32, (LANES, SSD_STATE), 0) < 64
        dxs, dbs, dcs = [], [], []
        for g in range(2):
            bmat = bm_ref[:, g * 128:(g + 1) * 128]
            cmat = cm_ref[:, g * 128:(g + 1) * 128]
            cb = _dot(cmat, bmat, NT)
            dcb = jnp.zeros((L, L), F32)
            db = jnp.zeros((L, SSD_STATE), F32)
            dc = jnp.zeros((L, SSD_STATE), F32)
            for pr in range(2):
                p4 = 2 * g + pr
                h0 = 2 * p4
                hp = hs_ref[0, p4 * 128:(p4 + 1) * 128, :]
                xp = xs_ref[:, p4 * 128:(p4 + 1) * 128]
                t = _ssd_pair(cm, h0, cb, xp, dtv, bmat, cmat, hp, lo)
                gst = g_scr[p4]
                dyp = dy_ref[:, p4 * 128:(p4 + 1) * 128]
                dy0 = jnp.where(lo, dyp, 0.0)
                dy1 = dyp - dy0
                bg = _dot(bmat, gst, NT)
                dx = _dot(t["m0"], dy0, TN) + _dot(t["m1"], dy1, TN) + bg * t["dte_p"]
                dm0, dm1 = _dot(dy0, t["x"], NT), _dot(dy1, t["x"], NT)
                dcb = dcb + dm0 * t["lam0"] + dm1 * t["lam1"]
                dye = dyp * t["ecs_p"]
                dc = dc + _dot(dye, hp)
                db = db + _dot(t["xd"], gst)
                i0 = jnp.sum(jnp.where(stril, _dot(triu, dm0 * t["m0"]), 0.0), axis=1, keepdims=True)
                i1 = jnp.sum(jnp.where(stril, _dot(triu, dm1 * t["m1"]), 0.0), axis=1, keepdims=True)
                yo0, yo1 = halves(dyp * t["yoff"], lo)
                w0, w1 = halves(t["xd"] * bg, lo)
                gh = gst * (hp * t["decay"])
                e0 = _sum0(jnp.sum(jnp.where(rlo, gh, 0.0), axis=1, keepdims=True))
                e1 = _sum0(jnp.sum(jnp.where(rlo, 0.0, gh), axis=1, keepdims=True))
                x0, x1 = halves(dx * xp, lo)
                oh0 = (lane_row == h0).astype(F32)
                oh1 = (lane_row == h0 + 1).astype(F32)
                i_all = i_all + i0 * oh0 + i1 * oh1
                yo_all = yo_all + yo0 * oh0 + yo1 * oh1
                w_all = w_all + w0 * oh0 + w1 * oh1
                e_row = e_row + e0 * oh0 + e1 * oh1
                ddt_x = ddt_x + x0 * oh0 + x1 * oh1
                dxs.append(dx * _pair_sel(dtv, h0, lo))
                g_scr[p4] = gst * t["decay"] + _dot(dye, cmat, TN)
            dbs.append(db + _dot(dcb, cmat, TN))
            dcs.append(dc + _dot(dcb, bmat))
        da = i_all + _dot_exact(triu, yo_all) + _dot_exact(stril.astype(F32), w_all) + e_row
        ddt_ref[...] = da * a_row_v + ddt_x
        da_ref[...] += _sum0(da * dtv)
        dxs_ref[...] = jnp.concatenate(dxs, axis=1)
        dbc_ref[...] = jnp.concatenate(dbs + dcs, axis=1)

    rev = lambda c: nc // per - 1 - c
    return pl.pallas_call(
        body, name=name, grid=(nc // per,),
        in_specs=[pl.BlockSpec((G, 512), lambda c: (rev(c), 0)), pl.BlockSpec((G, 256), lambda c: (rev(c), 2)),
                  pl.BlockSpec((G, 256), lambda c: (rev(c), 3)), pl.BlockSpec((G, 128), lambda c: (rev(c), 0)),
                  pl.BlockSpec((1, 128), lambda c: (0, 0)), pl.BlockSpec((per, 512, 128), lambda c: (rev(c), 0, 0)),
                  pl.BlockSpec((G, 512), lambda c: (rev(c), 0))],
        out_specs=[pl.BlockSpec((G, 512), lambda c: (rev(c), 0)), pl.BlockSpec((G, 512), lambda c: (rev(c), 0)),
                   pl.BlockSpec((G, 128), lambda c: (rev(c), 0)), pl.BlockSpec((1, 128), lambda c: (0, 0))],
        out_shape=[jax.ShapeDtypeStruct((S, 512), F32), jax.ShapeDtypeStruct((S, 512), F32),
                   jax.ShapeDtypeStruct((S, 128), F32), jax.ShapeDtypeStruct((1, 128), F32)],
        scratch_shapes=[pltpu.VMEM((4, 128, 128), F32)],
        compiler_params=_params(("arbitrary",)),
    )(xbc, xbc, xbc, dt, a_row, hs, dy)


HBM = pl.BlockSpec(memory_space=pl.ANY)


class _Step:
    def __init__(self, inputs, out_shapes, n_sems, start, finish, mid=None):
        self.inputs, self.out_shapes, self.n_sems = inputs, out_shapes, n_sems
        self.start, self.finish, self.mid = start, finish, mid
        self.alias = []


class _Shifted:
    def __init__(self, ref, off):
        self.ref, self.off = ref, off

    @property
    def at(self):
        return self

    def __getitem__(self, j):
        return self.ref.at[self.off + j]


def _merge_steps(steps):
    offs = [sum(s.n_sems for s in steps[:i]) for i in range(len(steps) + 1)]
    i_offs = [sum(len(s.inputs) for s in steps[:i]) for i in range(len(steps))]
    o_offs = [sum(len(s.out_shapes) for s in steps[:i]) for i in range(len(steps))]

    def phase(which):
        def run(ins, outs, sems):
            for s, off, i0, o0 in zip(steps, offs, i_offs, o_offs):
                fn = getattr(s, which)
                if fn is not None:
                    fn(ins[i0:i0 + len(s.inputs)], outs[o0:o0 + len(s.out_shapes)],
                       [_Shifted(sems[0], off), _Shifted(sems[1], off)])
        return run

    merged = _Step([a for s in steps for a in s.inputs], [o for s in steps for o in s.out_shapes], offs[-1],
                   phase("start"), phase("finish"), phase("mid") if any(s.mid for s in steps) else None)
    merged.alias = [(i0 + a, o0 + b) for s, i0, o0 in zip(steps, i_offs, o_offs) for a, b in s.alias]
    return merged


def _place():
    x, y, c = lax.axis_index("x"), lax.axis_index("y"), lax.axis_index("c")
    chips = [(1 - x, y), (x, 1 - y), (1 - x, 1 - y)]
    return x, y, c, chips


def _mesh_pos():
    return 2 * lax.axis_index("x") + lax.axis_index("y"), lax.axis_index("c")


def _chunks(rows, tile):
    return next(n for n in (4, 3, 2, 1) if rows % (n * tile) == 0)


def _remote(src, dst, sems, j, to):
    return pltpu.make_async_remote_copy(src_ref=src, dst_ref=dst, send_sem=sems[0].at[j], recv_sem=sems[1].at[j],
                                        device_id=to, device_id_type=MESH)


def _gather_step(wp):
    R, C = wp.shape
    H = R // 2
    nq = _chunks(H, 16)
    CH = H // nq

    def copies(ins, outs, sems):
        x, y, c, chips = _place()
        sib, me = (x, y, 1 - c), 2 * x + y
        w_ref, out_ref = ins[0], outs[0]

        def piece(k, hc, q):
            return out_ref.at[k, pl.ds(hc * H + q * CH, CH), :]

        sends, landed, fwds, fwd_landed = [], [], [], []
        for q in range(nq):
            for j, (px, py) in enumerate(chips):
                k = 2 * px + py
                sends.append(_remote(w_ref.at[pl.ds(c * H + q * CH, CH), :], piece(me, c, q), sems, j * nq + q,
                                     (px, py, c)))
                landed.append(_remote(piece(k, c, q), piece(k, c, q), sems, j * nq + q, (px, py, c)))
                fwds.append(_remote(piece(k, c, q), piece(k, c, q), sems, (3 + j) * nq + q, sib))
                fwd_landed.append(_remote(piece(k, 1 - c, q), piece(k, 1 - c, q), sems, (3 + j) * nq + q, sib))
        return sends, landed, fwds, fwd_landed

    def start(ins, outs, sems):
        for cp in copies(ins, outs, sems)[0]:
            cp.start()

    def mid(ins, outs, sems):
        _, landed, fwds, _ = copies(ins, outs, sems)
        for arrived, onward in zip(landed, fwds):
            arrived.wait_recv()
            onward.start()

    def finish(ins, outs, sems):
        sends, _, fwds, fwd_landed = copies(ins, outs, sems)
        for cp in fwd_landed:
            cp.wait_recv()
        for cp in sends + fwds:
            cp.wait_send()

    return _Step([wp], [jax.ShapeDtypeStruct((N_SHARD, R, C), wp.dtype)], 6 * nq, start, finish, mid)


def _pair_exchange_step(gp):
    n, R, C = gp.shape
    H = R // 2
    nq = _chunks(H, 8)
    CH = H // nq

    def copies(ins, outs, sems):
        x, y, c, _ = _place()
        return [_remote(ins[0].at[k, pl.ds((1 - c) * H + q * CH, CH), :], outs[0].at[k, pl.ds(q * CH, CH), :], sems,
                        k * nq + q, (x, y, 1 - c)) for k in range(n) for q in range(nq)]

    def start(ins, outs, sems):
        for cp in copies(ins, outs, sems):
            cp.start()

    def finish(ins, outs, sems):
        for cp in copies(ins, outs, sems):
            cp.wait()

    return _Step([gp], [jax.ShapeDtypeStruct((n, H, C), gp.dtype)], n * nq, start, finish)


def _chip_exchange_step(pb):
    n, H, C = pb.shape
    nq = _chunks(H, 16)
    CH = H // nq

    def copies(ins, outs, sems):
        x, y, c, chips = _place()
        return [_remote(ins[0].at[2 * px + py, pl.ds(q * CH, CH), :], outs[0].at[j, pl.ds(q * CH, CH), :], sems,
                        j * nq + q, (px, py, c)) for q in range(nq) for j, (px, py) in enumerate(chips)]

    def start(ins, outs, sems):
        for cp in copies(ins, outs, sems):
            cp.start()

    def finish(ins, outs, sems):
        for cp in copies(ins, outs, sems):
            cp.wait()

    return _Step([pb], [jax.ShapeDtypeStruct((3, H, C), pb.dtype)], 3 * nq, start, finish)


def _pair_fill_step(red):
    R, C = red.shape
    H = R // 2
    nq = _chunks(H, 8)
    CH = H // nq

    def copies(ins, outs, sems):
        x, y, c, _ = _place()
        return [_remote(ins[0].at[pl.ds(c * H + j * CH, CH), :], outs[0].at[pl.ds(c * H + j * CH, CH), :], sems, j,
                        (x, y, 1 - c)) for j in range(nq)]

    def start(ins, outs, sems):
        for cp in copies(ins, outs, sems):
            cp.start()

    def finish(ins, outs, sems):
        for cp in copies(ins, outs, sems):
            cp.wait()

    step = _Step([red], [jax.ShapeDtypeStruct((R, C), red.dtype)], nq, start, finish)
    step.alias = [(0, 0)]
    return step


def _sem_scratch(step):
    return [pltpu.SemaphoreType.DMA((step.n_sems,)), pltpu.SemaphoreType.DMA((step.n_sems,))]


def _run_step(step, name):
    ni, no = len(step.inputs), len(step.out_shapes)

    def body(*refs):
        ins, outs, sems = refs[:ni], refs[ni:ni + no], refs[ni + no:]
        step.start(ins, outs, sems)
        if step.mid is not None:
            step.mid(ins, outs, sems)
        step.finish(ins, outs, sems)

    return pl.pallas_call(body, name=name, in_specs=[HBM] * ni, out_specs=[HBM] * no, out_shape=step.out_shapes,
                          input_output_aliases=dict(step.alias),
                          scratch_shapes=_sem_scratch(step))(*step.inputs)


def _grid_flags(grid):
    ids = [pl.program_id(d) for d in range(len(grid))]
    first = functools.reduce(lambda a, b: a & b, [i == 0 for i in ids])
    last = functools.reduce(lambda a, b: a & b, [i == n - 1 for i, n in zip(ids, grid)])
    return first, last, last


def _call_with_step(core, step, flags, args, *, name, grid, in_specs, out_specs, out_shape, sem, scratch_shapes=(),
                    aliases=None):
    aliases = aliases or {}
    if step is None:
        return pl.pallas_call(core, name=name, grid=grid, in_specs=in_specs, out_specs=out_specs,
                              out_shape=out_shape, scratch_shapes=list(scratch_shapes),
                              input_output_aliases=aliases, compiler_params=_params(sem))(*args)
    n_in, n_out, n_scr = len(in_specs), len(out_specs), len(scratch_shapes)
    si, so = len(step.inputs), len(step.out_shapes)
    flags = flags or (lambda: _grid_flags(grid))
    aliases = {**aliases, **{n_in + a: n_out + b for a, b in step.alias}}

    def body(*refs):
        ins, s_ins = refs[:n_in], refs[n_in:n_in + si]
        outs = refs[n_in + si:n_in + si + n_out]
        s_outs = refs[n_in + si + n_out:n_in + si + n_out + so]
        scr = refs[n_in + si + n_out + so:n_in + si + n_out + so + n_scr]
        sems = refs[n_in + si + n_out + so + n_scr:]
        first, middle, last = flags()

        @pl.when(first)
        def _():
            step.start(s_ins, s_outs, sems)

        if step.mid is not None:
            @pl.when(middle)
            def _():
                step.mid(s_ins, s_outs, sems)

        core(*ins, *outs, *scr)

        @pl.when(last)
        def _():
            step.finish(s_ins, s_outs, sems)

    return pl.pallas_call(
        body, name=name, grid=grid, in_specs=list(in_specs) + [HBM] * si, out_specs=list(out_specs) + [HBM] * so,
        out_shape=list(out_shape) + list(step.out_shapes), scratch_shapes=list(scratch_shapes) + _sem_scratch(step),
        input_output_aliases=aliases, compiler_params=_params(("arbitrary",) * len(grid)))(*args, *step.inputs)


def _attn_flags(nq):
    h, qi = pl.program_id(0), pl.program_id(1)
    return ((h == 0) & (qi == 0), (h == MLA_HEADS - 1) & (qi == 0), (h == MLA_HEADS - 1) & (qi == nq - 1))


def _att_mask(s_t, q0, k0):
    krow = k0 + lax.broadcasted_iota(jnp.int32, s_t.shape, 0)
    qcol = q0 + lax.broadcasted_iota(jnp.int32, s_t.shape, 1)
    return jnp.where(krow <= qcol, s_t, NEG)


def _loop_blocks(lo, hi, step, carry):
    n = hi - lo

    def four(i, c):
        kb = lo + 4 * i
        return step(kb + 3, step(kb + 2, step(kb + 1, step(kb, c))))

    carry = lax.fori_loop(0, n // 4, four, carry)
    base = lo + 4 * (n // 4)
    carry = lax.cond(n % 4 >= 2, lambda c: step(base + 1, step(base, c)), lambda c: c, carry)
    return lax.cond(n % 2 == 1, lambda c: step(hi - 1, c), lambda c: c, carry)


def _rows(ref, blk, t):
    return ref[pl.ds(pl.multiple_of(blk * t, t), t), :]


def _cols(ref, blk, t):
    return ref[:, pl.ds(pl.multiple_of(blk * t, t), t)]


def _attn_fwd(q, k, v_t, *, name, hosted=None):
    S = q.shape[0]
    t = min(ATTN_TILE, S)
    nq = S // t

    def body(q_ref, k_ref, vt_ref, o_ref, lse_ref):
        qi = pl.program_id(1)
        qv = q_ref[...]

        def absorb(kb, carry, masked):
            m, l, acc = carry
            s_t = lax.dot_general(_rows(k_ref, kb, t), qv, NT, preferred_element_type=F32)
            if masked:
                s_t = _att_mask(s_t, qi * t, kb * t)
            m_new = jnp.maximum(m, jnp.max(s_t, axis=0, keepdims=True))
            p_t = jnp.exp(s_t - m_new)
            corr = jnp.exp(m - m_new)
            return (m_new, corr * l + jnp.sum(p_t, axis=0, keepdims=True),
                    corr * acc + lax.dot_general(_cols(vt_ref, kb, t), p_t.astype(BF16), NN,
                                                 preferred_element_type=F32))

        init = (jnp.full((1, t), NEG, F32), jnp.zeros((1, t), F32), jnp.zeros((LANES, t), F32))
        carry = _loop_blocks(0, qi, lambda kb, c: absorb(kb, c, False), init)
        m, l, acc = absorb(qi, carry, True)
        o_ref[...] = acc / l
        lse_ref[0] = m + jnp.log(l)

    return _call_with_step(
        body, hosted, lambda: _attn_flags(nq), (q, k, v_t), name=name, grid=(MLA_HEADS, nq),
        in_specs=[pl.BlockSpec((t, LANES), lambda h, qi: (qi, h)),
                  pl.BlockSpec((S, LANES), lambda h, qi: (0, h)),
                  pl.BlockSpec((LANES, S), lambda h, qi: (h, 0))],
        out_specs=[pl.BlockSpec((LANES, t), lambda h, qi: (h, qi)),
                   pl.BlockSpec((1, 1, t), lambda h, qi: (h, 0, qi))],
        out_shape=[jax.ShapeDtypeStruct((MLA_HEADS * LANES, S), F32), jax.ShapeDtypeStruct((MLA_HEADS, 1, S), F32)],
        sem=("parallel", "arbitrary"))


def _attn_bwd_dq(q, k, v, o_t, do_t, lse, *, name, hosted=None):
    S = q.shape[0]
    t = min(ATTN_TILE, S)
    nq = S // t

    def body(q_ref, k_ref, v_ref, o_ref, do_ref, lse_ref, dq_ref, delta_ref):
        qi = pl.program_id(1)
        qv = q_ref[...]
        dov = do_ref[...]
        delta = jnp.sum(dov * o_ref[...], axis=0, keepdims=True)
        delta_ref[0] = delta
        dob = dov.astype(BF16)
        lse_v = lse_ref[0]

        def step(kb, acc, masked):
            kt = _rows(k_ref, kb, t)
            s_t = lax.dot_general(kt, qv, NT, preferred_element_type=F32)
            if masked:
                s_t = _att_mask(s_t, qi * t, kb * t)
            p_t = jnp.exp(s_t - lse_v)
            dp_t = lax.dot_general(_rows(v_ref, kb, t), dob, NN, preferred_element_type=F32)
            ds_t = (p_t * (dp_t - delta)).astype(BF16)
            return acc + lax.dot_general(kt, ds_t, TN, preferred_element_type=F32)

        acc = _loop_blocks(0, qi, lambda kb, c: step(kb, c, False), jnp.zeros((LANES, t), F32))
        dq_ref[...] = step(qi, acc, True).T

    tile = pl.BlockSpec((t, LANES), lambda h, qi: (qi, h))
    tile_t = pl.BlockSpec((LANES, t), lambda h, qi: (h, qi))
    stat = pl.BlockSpec((1, 1, t), lambda h, qi: (h, 0, qi))
    seq = pl.BlockSpec((S, LANES), lambda h, qi: (0, h))
    return _call_with_step(
        body, hosted, lambda: _attn_flags(nq), (q, k, v, o_t, do_t, lse), name=name, grid=(MLA_HEADS, nq),
        in_specs=[tile, seq, seq, tile_t, tile_t, stat],
        out_specs=[tile, stat],
        out_shape=[jax.ShapeDtypeStruct((S, MLA_HEADS * LANES), F32), jax.ShapeDtypeStruct((MLA_HEADS, 1, S), F32)],
        sem=("parallel", "arbitrary"))


def _attn_bwd_dkv(q, k, v, do_t, lse, delta, *, name, hosted=None):
    S = q.shape[0]
    t = min(ATTN_TILE, S)
    nq = S // t

    def body(q_ref, k_ref, v_ref, do_ref, lse_ref, delta_ref, dk_ref, dv_ref):
        ki = pl.program_id(1)
        kv, vv = k_ref[...], v_ref[...]

        def step(qb, carry, masked):
            dk, dv = carry
            qt = _rows(q_ref, qb, t)
            s_t = lax.dot_general(kv, qt, NT, preferred_element_type=F32)
            if masked:
                s_t = _att_mask(s_t, qb * t, ki * t)
            p_t = jnp.exp(s_t - _cols(lse_ref.at[0], qb, t))
            dob = _cols(do_ref, qb, t).astype(BF16)
            dv = dv + lax.dot_general(dob, p_t.astype(BF16), NT, preferred_element_type=F32)
            dp_t = lax.dot_general(vv, dob, NN, preferred_element_type=F32)
            ds_t = (p_t * (dp_t - _cols(delta_ref.at[0], qb, t))).astype(BF16)
            dk = dk + lax.dot_general(qt.T, ds_t, NT, preferred_element_type=F32)
            return dk, dv

        zero = jnp.zeros((LANES, t), F32)
        carry = step(ki, (zero, zero), True)
        dk, dv = _loop_blocks(ki + 1, nq, lambda qb, c: step(qb, c, False), carry)
        dk_ref[...] = dk.T
        dv_ref[...] = dv

    tile = pl.BlockSpec((t, LANES), lambda h, ki: (ki, h))
    tile_t = pl.BlockSpec((LANES, t), lambda h, ki: (h, ki))
    seq = pl.BlockSpec((S, LANES), lambda h, ki: (0, h))
    seq_t = pl.BlockSpec((LANES, S), lambda h, ki: (h, 0))
    stat = pl.BlockSpec((1, 1, S), lambda h, ki: (h, 0, 0))
    return _call_with_step(
        body, hosted, lambda: _attn_flags(nq), (q, k, v, do_t, lse, delta), name=name, grid=(MLA_HEADS, nq),
        in_specs=[seq, tile, tile, seq_t, stat, stat],
        out_specs=[tile, tile_t],
        out_shape=[jax.ShapeDtypeStruct((S, MLA_HEADS * LANES), F32), jax.ShapeDtypeStruct((MLA_HEADS * LANES, S), F32)],
        sem=("parallel", "arbitrary"))


def _fn_ln(ctx, x, g, b):
    xhat, _ = _ln_stats(x)
    y = xhat * g + b
    return y, y


def _fn_conv_fwd(ctx, u, up, dtr, w8, cb, dtb):
    first = ctx.i == 0
    y = u * w8[3:4] + cb
    for s in (1, 2, 3):
        y = y + _shift_down(u, up, s, first) * w8[3 - s:4 - s]
    act = y * _sigmoid(y)
    v = dtr + dtb
    e = jnp.exp(-jnp.abs(v))
    one_p = 1.0 + e
    log1p = jnp.where(one_p == 1.0, e, jnp.log(one_p) * e / (one_p - 1.0))
    return y, act, jnp.maximum(v, 0.0) + log1p


def _fn_ssd_post(ctx, y, xs, z, dexp, g):
    yg = (y + xs * dexp) * (z * _sigmoid(z))
    outs = []
    for k in range(2):
        v = yg[:, 256 * k:256 * (k + 1)]
        outs.append(v * lax.rsqrt(_mean1(v * v) + RMS_EPS))
    return (jnp.concatenate(outs, axis=1) * g,)


def _fn_ssd_post_bwd(ctx, dyn, y, xs, z, dexp, g):
    yt = y + xs * dexp
    sig = _sigmoid(z)
    sz = z * sig
    yg = yt * sz
    dyh = dyn * g
    yh, dyg = [], []
    for k in range(2):
        sl = slice(256 * k, 256 * (k + 1))
        v = yg[:, sl]
        rs = lax.rsqrt(_mean1(v * v) + RMS_EPS)
        vh = v * rs
        yh.append(vh)
        dyg.append(rs * (dyh[:, sl] - vh * _mean1(dyh[:, sl] * vh)))
    yh = jnp.concatenate(yh, axis=1)
    dyg = jnp.concatenate(dyg, axis=1)
    dyt = dyg * sz
    dz = dyg * yt * (sig * (1.0 + z * (1.0 - sig)))
    return dyt, dz, dyt * dexp, _sum0(dyt * xs), _sum0(dyn * yh)


def _fn_mla_pre(ctx, ql, kvl, gq, gkv):
    return _rms_fwd(ql, gq), _rms_fwd(kvl, gkv)


def _fn_mla_pre_bwd(ctx, ql, kvl, dqn, dkvn_k, dkvn_v, gq, gkv):
    dql, dgq = _rms_bwd(ql, dqn, gq)
    dkvl, dgkv = _rms_bwd(kvl, dkvn_k + dkvn_v, gkv)
    return dql, dkvl, dgq, dgkv


def _fn_rope(ctx, qp, kn, kr, ta, tb, tc):
    kpe = _rope(kr, ta, tb, tc)
    qs, ks = [], []
    for h in range(MLA_HEADS):
        sl = slice(128 * h, 128 * (h + 1))
        qs.append(_rope(qp[:, sl], ta, tb, tc) * MLA_SCALE)
        ks.append(kn[:, sl] + kpe)
    return jnp.concatenate(qs, axis=1), jnp.concatenate(ks, axis=1)


def _fn_rope_bwd(ctx, dq, dk, ta, tb, tc):
    qs = []
    ksum = jnp.zeros_like(ta)
    for h in range(MLA_HEADS):
        sl = slice(128 * h, 128 * (h + 1))
        qs.append(_rope_bwd(dq[:, sl] * MLA_SCALE, ta, tb, tc))
        ksum = ksum + dk[:, sl]
    lane = _lane(ksum.shape)
    dkr = jnp.where((lane >= 64) & (lane < 96), _rope_bwd(ksum, ta, tb, tc), 0.0)
    return jnp.concatenate(qs, axis=1), dkr


def _mem_probs(qh, kh):
    s = _dot(qh, kh, NT) * MEM_SCALE
    p = jnp.exp(s - jnp.max(s, axis=1, keepdims=True))
    return p / jnp.sum(p, axis=1, keepdims=True)


def _fn_mem_fwd(ctx, q, km, vm):
    outs = []
    for h in range(MEM_HEADS):
        sl = slice(256 * h, 256 * (h + 1))
        outs.append(_dot(_mem_probs(q[:, sl], km[:, sl]), vm[:, sl]))
    return (jnp.concatenate(outs, axis=1),)


def _fn_mem_bwd(ctx, q, do, km, vm):
    dqs, dks, dvs = [], [], []
    for h in range(MEM_HEADS):
        sl = slice(256 * h, 256 * (h + 1))
        p = _mem_probs(q[:, sl], km[:, sl])
        dvs.append(_dot(p, do[:, sl], TN))
        dp = _dot(do[:, sl], vm[:, sl], NT)
        ds = p * (dp - jnp.sum(dp * p, axis=1, keepdims=True)) * MEM_SCALE
        dqs.append(_dot(ds, km[:, sl]))
        dks.append(_dot(ds, q[:, sl], TN))
    return jnp.concatenate(dqs, axis=1), jnp.concatenate(dks, axis=1), jnp.concatenate(dvs, axis=1)


def _fn_res_ln(ctx, h, r, g, b):
    xhat, _ = _ln_stats(ALPHA * h + r)
    y = xhat * g + b
    return y, y


def _fn_res_ln_bwd(ctx, h, r, d1, d2, g):
    xhat, rstd = _ln_stats(ALPHA * h + r)
    return _ln_bwd(xhat, rstd, ALPHA * d1 + d2, g)


def _fn_res2_ln(ctx, h, r1, r2, g, b):
    xhat, _ = _ln_stats(ALPHA * h + (r1 + r2))
    return (xhat * g + b,)


def _fn_res2_ln_bwd(ctx, h, r1, r2, d1, d2, g):
    xhat, rstd = _ln_stats(ALPHA * h + (r1 + r2))
    return _ln_bwd(xhat, rstd, ALPHA * d1 + d2, g)


def _fn_in_ln_bwd(ctx, x, d1, d2, g):
    xhat, rstd = _ln_stats(x)
    return _ln_bwd(xhat, rstd, ALPHA * d1 + d2, g)


def _fn_final(ctx, h2, ff, tgt, g, b):
    xhat, rstd = _ln_stats(ALPHA * h2 + ff)
    e = xhat * g + b - tgt
    loss = 0.5 * _sum0(jnp.sum(e * e, axis=1, keepdims=True)) / D_MODEL
    dx, dg, db = _ln_bwd(xhat, rstd, e / D_MODEL, g)
    return dx, dx, dg, db, loss


def _epi_du(da, u):
    return da * 2.0 * jnp.maximum(u.astype(F32), 0.0)


def _relu2(u):
    r = jnp.maximum(u.astype(F32), 0.0)
    return r * r


def _fn_conv_bwd_a(ctx, y, dxs1, dxs2, dbc, dtr, ddt, dtb):
    sig = _sigmoid(y)
    dact = jnp.concatenate([dxs1 + dxs2, dbc], axis=1)
    dyc = dact * (sig * (1.0 + y * (1.0 - sig)))
    ddtr = ddt * _sigmoid(dtr + dtb)
    return dyc, ddtr, _sum0(dyc), _sum0(ddtr)


def _fn_conv_bwd_b(ctx, d, dn, u, up, w8):
    first, last = ctx.i == 0, ctx.i == ctx.n - 1
    du = d * w8[3:4]
    row = lax.broadcasted_iota(jnp.int32, w8.shape, 0)
    dw = jnp.where(row == 3, _sum0(d * u), 0.0)
    for s in (1, 2, 3):
        du = du + _shift_up(d, dn, s, last) * w8[3 - s:4 - s]
        dw = dw + jnp.where(row == 3 - s, _sum0(d * _shift_down(u, up, s, first)), 0.0)
    return du, dw


def _fn_adam(ctx, w, g, m, v):
    m = ADAM_B1 * m + (1.0 - ADAM_B1) * g
    v = ADAM_B2 * v + (1.0 - ADAM_B2) * (g * g)
    m_hat = m / (1.0 - ADAM_B1 ** ADAM_STEP)
    v_hat = v / (1.0 - ADAM_B2 ** ADAM_STEP)
    return -ADAM_LR * (m_hat / (jnp.sqrt(v_hat) + ADAM_EPS) + ADAM_WD * w), m, v


def _z(r, c, dt):
    return jnp.zeros((r, c), dt)


W_IN_SHARD = 554
W_IN_GROUPS = [(0, 512, 1024), (512, 1536, 0), (1536, 1544, 1920), (1544, 1928, 1536), (1928, 2184, 2048),
               (2184, 2216, 2368)]


def _pad_w_in(ws):
    r, dt = ws.shape[1], ws.dtype

    def cols(a, b):
        out = []
        for k in range(N_SHARD):
            lo, hi = max(a, k * W_IN_SHARD), min(b, (k + 1) * W_IN_SHARD)
            if lo < hi:
                out.append(ws[k][:, lo - k * W_IN_SHARD:hi - k * W_IN_SHARD])
        return out

    return jnp.concatenate(cols(512, 1536) + cols(0, 512) + cols(1544, 1928) + cols(1536, 1544) + [_z(r, 120, dt)]
                           + cols(1928, 2184) + [_z(r, 64, dt)] + cols(2184, 2216) + [_z(r, 32, dt), _z(r, 128, dt)],
                           axis=1)


def _unpad_w_in(d):
    shards = []
    for k in range(N_SHARD):
        a, b = k * W_IN_SHARD, (k + 1) * W_IN_SHARD
        parts = []
        for o0, o1, p0 in W_IN_GROUPS:
            lo, hi = max(a, o0), min(b, o1)
            if lo < hi:
                parts.append(d[:, p0 + lo - o0:p0 + hi - o0])
        shards.append(jnp.concatenate(parts, axis=1))
    return jnp.stack(shards)


def _pad_heads(w, width):
    r = w.shape[0]
    w3 = w.reshape(r, MLA_HEADS, width)
    return jnp.pad(w3, ((0, 0), (0, 0), (0, 128 - width))).reshape(r, MLA_HEADS * 128)


def _row(v, width=None):
    v = v.reshape(1, -1).astype(F32)
    if width is not None and v.shape[1] < width:
        v = jnp.pad(v, ((0, 0), (0, width - v.shape[1])))
    return v


BIG = {
    "w_in": (1024, 2216, 1), "w_q_up": (384, 768, 1), "w_kv_up": (256, 1024, 1), "w_mix_out": (1024, 1024, 0),
    "w_mem_q": (1024, 1024, 0), "w_mem_k": (1024, 1024, 0), "w_mem_v": (1024, 1024, 0), "w_mem_o": (1024, 1024, 0),
    "w_up": (1024, 4096, 1), "w_down": (4096, 1024, 0), "conv_w": (4, 1024, 1),
}
BIG_ORDER = list(BIG)
SMALL_ORDER = ["ln_in_g", "ln_in_b", "conv_b", "dt_bias", "a_log", "d_skip", "ssd_norm_g", "q_norm_g", "kv_norm_g",
               "ln1_g", "ln1_b", "ln2_g", "ln2_b", "ln3_g", "ln3_b"]
N_SHARD = 4
N_DEV = 8
PACK_COLS = 1024
PACK_A_ROW = {"w_down": 0, "w_up": 1024, "w_mem_q": 2048, "w_mem_k": 2304, "w_mem_v": 2560, "w_mem_o": 2816,
              "w_mix_out": 3072}
PACK_A_ORDER = list(PACK_A_ROW)
PACK_A_ROWS = 3328
PACK_B_ORDER = ["w_q_up", "w_kv_up", "conv_w"]
PACK_B_ROWS = 160


def _shard_shape(name):
    r, c, ax = BIG[name]
    return (r // N_SHARD, c) if ax == 0 else (r, c // N_SHARD)


def _split_shards(name, full):
    r, c, ax = BIG[name]
    if ax == 0:
        return full.reshape(N_SHARD, -1)
    return full.reshape(r, N_SHARD, c // N_SHARD).transpose(1, 0, 2).reshape(N_SHARD, -1)


def _join_shards(name, parts):
    r, c, ax = BIG[name]
    if ax == 0:
        return parts.reshape(r, c)
    return parts.reshape(N_SHARD, r, c // N_SHARD).transpose(1, 0, 2).reshape(r, c)


def _small_all_reduce(g, step=None):
    r, cdim = g.shape
    si, so = (len(step.inputs), len(step.out_shapes)) if step else (0, 0)

    def body(g_ref, *refs):
        s_ins, out_ref, s_outs = refs[:si], refs[si], refs[si + 1:si + 1 + so]
        buf, send_sems, recv_sems = refs[si + 1 + so:si + 4 + so]
        s_sems = refs[si + 4 + so:]
        if step:
            step.start(s_ins, s_outs, s_sems)
        x, y, c, _ = _place()
        me = 4 * x + 2 * y + c
        buf[me] = g_ref[...]
        copies = []
        for d in range(1, N_DEV):
            to = me ^ d
            cp = pltpu.make_async_remote_copy(src_ref=g_ref, dst_ref=buf.at[me], send_sem=send_sems.at[d - 1],
                                              recv_sem=recv_sems.at[d - 1],
                                              device_id=(to // 4, (to // 2) % 2, to % 2), device_id_type=MESH)
            cp.start()
            copies.append(cp)
        for cp in copies:
            cp.wait()
        acc = buf[0]
        for d in range(1, N_DEV):
            acc = acc + buf[d]
        out_ref[...] = acc
        if step:
            step.finish(s_ins, s_outs, s_sems)

    res = pl.pallas_call(
        body, name="small_all_reduce",
        in_specs=[pl.BlockSpec(memory_space=pltpu.VMEM)] + [HBM] * si,
        out_specs=[pl.BlockSpec(memory_space=pltpu.VMEM)] + [HBM] * so,
        out_shape=[jax.ShapeDtypeStruct((r, cdim), F32)] + (list(step.out_shapes) if step else []),
        scratch_shapes=[pltpu.VMEM((N_DEV, r, cdim), F32), pltpu.SemaphoreType.DMA((N_DEV - 1,)),
                        pltpu.SemaphoreType.DMA((N_DEV - 1,))] + (_sem_scratch(step) if step else []),
    )(g, *(step.inputs if step else []))
    return res if step else res[0]


def _half_tile(h):
    return next(t for t in range(512, 0, -16) if h % t == 0)


def _pair_sum(gp, theirs, name):
    n, R, C = gp.shape
    H = R // 2
    tr = _half_tile(H)
    nb = H // tr

    def body(s_ref, g_ref, t_ref, o_ref):
        o_ref[...] = (g_ref[...] + t_ref[...]).astype(o_ref.dtype)

    def shard(k, s):
        return k + (k >= s[1]).astype(jnp.int32)

    me, c = _mesh_pos()
    return pl.pallas_call(
        body, name=name,
        grid_spec=pltpu.PrefetchScalarGridSpec(
            num_scalar_prefetch=1, grid=(n - 1, nb),
            in_specs=[pl.BlockSpec((1, tr, C), lambda k, i, s: (shard(k, s), s[0] * nb + i, 0)),
                      pl.BlockSpec((1, tr, C), lambda k, i, s: (shard(k, s), i, 0))],
            out_specs=pl.BlockSpec((1, tr, C), lambda k, i, s: (shard(k, s), i, 0))),
        out_shape=jax.ShapeDtypeStruct((n, H, C), BF16), compiler_params=_params(("arbitrary", "arbitrary")),
    )(jnp.stack([c, me]).astype(jnp.int32), gp, theirs)


def _chip_sum(gp, theirs, got, name):
    n, R, C = gp.shape
    H = R // 2
    tr = _half_tile(H)
    nb = H // tr

    def body(s_ref, g_ref, t_ref, r_ref, o_ref):
        acc = g_ref[0] + t_ref[0]
        for j in range(3):
            acc = acc + r_ref[j].astype(F32)
        o_ref[...] = acc

    me, c = _mesh_pos()
    return pl.pallas_call(
        body, name=name,
        grid_spec=pltpu.PrefetchScalarGridSpec(
            num_scalar_prefetch=1, grid=(nb,),
            in_specs=[pl.BlockSpec((1, tr, C), lambda i, s: (s[0], s[1] * nb + i, 0)),
                      pl.BlockSpec((1, tr, C), lambda i, s: (s[0], i, 0)),
                      pl.BlockSpec((3, tr, C), lambda i, s: (0, i, 0))],
            out_specs=pl.BlockSpec((tr, C), lambda i, s: (s[1] * nb + i, 0))),
        out_shape=jax.ShapeDtypeStruct((R, C), F32), compiler_params=_params(("arbitrary",)),
    )(jnp.stack([me, c]).astype(jnp.int32), gp, theirs, got)


def _unpack_group_b(g_c, g_b, own):
    me, _ = _mesh_pos()
    g_c = lax.dynamic_update_slice(g_c, own[0][None], (me, 0, 0))
    g_b = lax.dynamic_update_slice(g_b, own[1][None], (me, 0, 0)).reshape(N_SHARD, -1)
    WB, off = {"w_in": g_c}, 0
    for n in PACK_B_ORDER:
        sr, sc = _shard_shape(n)
        cnt = sr * sc
        if n == "conv_w":
            part = lax.bitcast_convert_type(g_b[:, off:off + 2 * cnt].reshape(N_SHARD, cnt, 2), F32)
            off += 2 * cnt
        else:
            part = g_b[:, off:off + cnt]
            off += cnt
        WB[n] = _join_shards(n, part)
    return WB


def _group_b_grads(dw_in_p, dw_q_p, dw_k_p, dw_v_pt, dconv_w8):
    return {
        "w_in": _unpad_w_in(dw_in_p),
        "w_q_up": dw_q_p.reshape(384, MLA_HEADS, 128)[:, :, :MLA_QK].reshape(384, MLA_HEADS * MLA_QK),
        "w_kv_up": jnp.concatenate([dw_k_p.reshape(MLA_KV_RANK, MLA_HEADS, 128)[:, :, :64],
                                    dw_v_pt.T.reshape(MLA_KV_RANK, MLA_HEADS, 128)[:, :, :64]], axis=2).reshape(
                                        MLA_KV_RANK, MLA_HEADS * 128),
        "conv_w": dconv_w8[0:4],
    }


def _pack_group_b(big_b):
    gflat = [_split_shards(n, big_b[n]) for n in PACK_B_ORDER]
    used = sum(f.shape[1] for f in gflat)
    gflat.append(jnp.zeros((N_SHARD, PACK_B_ROWS * PACK_COLS - used), F32))
    return big_b["w_in"], jnp.concatenate(gflat, axis=1).reshape(N_SHARD, PACK_B_ROWS, PACK_COLS)


def _local_step(x, mem, positions, target, WB, P, *, wp_a=None, g_a=None, wp_b=None):
    S = x.shape[0]
    tr = ROW_TILE
    dist = g_a is None
    g_in, b_in = _row(P["ln_in_g"]), _row(P["ln_in_b"])
    res = _rowwise(_fn_ln, [x], [g_in, b_in], [D_MODEL, (D_MODEL, BF16)], tr=tr, name="ln_in",
                   hosted=_merge_steps([_gather_step(w) for w in wp_b]) if dist else None)
    h0, h0_b = res[0], res[1]
    if dist:
        WB = _unpack_group_b(res[2], res[3], wp_b)
    P = {**P, "conv_w": WB["conv_w"]}
    w_in_p = _pad_w_in(WB["w_in"])
    w_q_p = _pad_heads(WB["w_q_up"], MLA_QK)
    w_kv3 = WB["w_kv_up"].reshape(MLA_KV_RANK, MLA_HEADS, 128)
    w_k_p = _pad_heads(w_kv3[:, :, :64].reshape(MLA_KV_RANK, 512), 64)
    w_v_p = _pad_heads(w_kv3[:, :, 64:].reshape(MLA_KV_RANK, 512), 64)
    w_v_pt = w_v_p.T
    conv_w8 = jnp.pad(P["conv_w"].astype(F32), ((0, 4), (0, 0)))
    conv_b = _row(P["conv_b"])
    dt_b = _row(P["dt_bias"], 128)
    a_head = -jnp.exp(P["a_log"].reshape(-1).astype(F32))
    a_row = _row(a_head, 128)
    dexp = jnp.repeat(P["d_skip"].reshape(-1).astype(F32), 64).reshape(1, 512)
    g_ssd, g_q, g_kv = _row(P["ssd_norm_g"]), _row(P["q_norm_g"]), _row(P["kv_norm_g"])
    g1, b1, g2, b2, g3, b3 = (_row(P[k]) for k in ("ln1_g", "ln1_b", "ln2_g", "ln2_b", "ln3_g", "ln3_b"))

    half = MLA_ROPE // 2
    inv_freq = jnp.power(ROPE_THETA, -jnp.arange(half, dtype=F32) / half)
    ang = positions.reshape(S, 1).astype(F32) * inv_freq
    cos, sin = jnp.cos(ang), jnp.sin(ang)
    zc = lambda n: jnp.zeros((S, n), F32)
    rope_a = jnp.concatenate([jnp.ones((S, 64), F32), cos, cos, zc(32)], axis=1)
    rope_b = jnp.concatenate([zc(80), sin, zc(32)], axis=1)
    rope_c = jnp.concatenate([zc(64), -sin, zc(48)], axis=1)

    proj = _mm(h0_b, w_in_p, form="nn", tn=IN_W // 2, name="mm_in")
    conv_y, xbc, dt = _rowwise(
        _fn_conv_fwd, [(proj,) + SEG_XBC, ("prev", proj) + SEG_XBC, (proj,) + SEG_DT], [conv_w8, conv_b, dt_b],
        [1024, 1024, 128], tr=tr, name="conv_fwd")
    y_ssd, hs = _ssd_fwd(xbc, dt, a_row, name="ssd_fwd")
    (y_n,) = _rowwise(_fn_ssd_post, [y_ssd, (xbc, 0, 512), (proj,) + SEG_Z], [dexp, g_ssd], [(512, BF16)], tr=tr,
                      name="ssd_post")
    q_n, kv_n = _rowwise(_fn_mla_pre, [(proj,) + SEG_QLAT, (proj,) + SEG_KVLAT], [g_q, g_kv], [384, 256], tr=tr,
                         name="mla_pre")
    qp = _mm(q_n, w_q_p, form="nn", name="mm_q_up")
    kn = _mm(kv_n, w_k_p, form="nn", name="mm_k_up")
    v_nat = _mm(kv_n, w_v_p, form="nn", out_dtype=BF16, name="mm_v_up")
    v_t = _mm(w_v_pt, kv_n, form="nt", out_dtype=BF16, name="mm_v_up_t")
    q_rot, k_full = _rowwise(_fn_rope, [qp, kn, (proj,) + SEG_KR, rope_a, rope_b, rope_c], [],
                             [(1024, BF16), (1024, BF16)], tr=tr, name="rope")
    res = _attn_fwd(q_rot, k_full, v_t, name="attn_fwd", hosted=_gather_step(wp_a) if dist else None)
    o_t, lse = res[0], res[1]
    if dist:
        g_a = lax.dynamic_update_slice(res[2], wp_a[None], (_mesh_pos()[0], 0, 0))
    r_mix = PACK_A_ROW["w_mix_out"]
    w_mix_o = jnp.pad(g_a[2:4, r_mix:r_mix + 256].reshape(MLA_HEADS, 64, D_MODEL),
                      ((0, 0), (0, 64), (0, 0))).reshape(MLA_HEADS * 128, D_MODEL)
    mix_o = _mm(o_t, w_mix_o, form="tn", name="mm_mix_o")
    mix_y = _mm(y_n, g_a, form="nn", b_pack="w_mix_out", name="mm_mix_y")
    (h1,) = _rowwise(_fn_res2_ln, [h0, mix_o, mix_y], [g1, b1], [D_MODEL], tr=tr, name="ln1")
    qm = _mm(h1, g_a, form="nn", b_pack="w_mem_q", out_dtype=BF16, name="mm_mem_q")
    km = _mm(mem, g_a, form="nn", b_pack="w_mem_k", out_dtype=BF16, name="mm_mem_k")
    vm = _mm(mem, g_a, form="nn", b_pack="w_mem_v", out_dtype=BF16, name="mm_mem_v")
    (om,) = _rowwise(_fn_mem_fwd, [qm], [km, vm], [(D_MODEL, BF16)], tr=tr, name="mem_fwd")
    xa = _mm(om, g_a, form="nn", b_pack="w_mem_o", name="mm_mem_o")
    h2, h2_b = _rowwise(_fn_res_ln, [h1, xa], [g2, b2], [D_MODEL, (D_MODEL, BF16)], tr=tr, name="ln2")
    u = _mm(h2_b, g_a, form="nn", b_pack="w_up", out_dtype=BF16, name="mm_up")
    ff = _mm(u, g_a, form="nn", a_pro=_relu2, b_pack="w_down", name="mm_down")

    gp = lax.empty((N_SHARD, PACK_A_ROWS, PACK_COLS), F32)
    dt3, dt3_b, dg3, db3, loss = _rowwise(_fn_final, [h2, ff, target], [g3, b3], [D_MODEL, (D_MODEL, BF16)],
                                          [(1, D_MODEL), (1, D_MODEL), (1, 128)], tr=tr, name="ln3_loss")
    du = _mm(dt3_b, g_a, form="nt", b_pack="w_down", epi=(_epi_du, u), out_dtype=BF16, name="mm_down_dx")
    gp = _mm(u, dt3_b, form="tn", a_pro=_relu2, out_pack=("w_down", gp), name="mm_down_dw")
    gp = _mm(h2_b, du, form="tn", out_pack=("w_up", gp), name="mm_up_dw")
    dh2 = _mm(du, g_a, form="nt", b_pack="w_up", name="mm_up_dx")
    dt2, dg2, db2 = _rowwise(_fn_res_ln_bwd, [h1, xa, dt3, dh2], [g2], [D_MODEL], [(1, D_MODEL)] * 2, tr=tr,
                             name="ln2_bwd")
    dom = _mm(dt2, g_a, form="nt", b_pack="w_mem_o", out_dtype=BF16, name="mm_mem_o_dx")
    gp = _mm(om, dt2, form="tn", out_pack=("w_mem_o", gp), name="mm_mem_o_dw")
    dqm, dkm, dvm = _rowwise(_fn_mem_bwd, [qm, dom], [km, vm], [(D_MODEL, BF16)], [(256, D_MODEL)] * 2, tr=tr,
                             name="mem_bwd")
    gp = _mm(h1, dqm, form="tn", out_pack=("w_mem_q", gp), name="mm_mem_q_dw")
    gp = _mm(mem, dkm, form="tn", out_pack=("w_mem_k", gp), name="mm_mem_k_dw")
    gp = _mm(mem, dvm, form="tn", out_pack=("w_mem_v", gp), name="mm_mem_v_dw")
    dh1 = _mm(dqm, g_a, form="nt", b_pack="w_mem_q", name="mm_mem_q_dx")
    dt1, dg1, db1 = _rowwise(_fn_res2_ln_bwd, [h0, mix_o, mix_y, dt2, dh1], [g1], [D_MODEL], [(1, D_MODEL)] * 2,
                             tr=tr, name="ln1_bwd")
    do_t = _mm(w_mix_o, dt1, form="nt", name="mm_mix_o_dx")
    dy_n = _mm(dt1, g_a, form="nt", b_pack="w_mix_out", b_rows=512, name="mm_mix_y_dx")
    dw_mix_o = _mm(o_t, dt1, form="nn", name="mm_mix_o_dw")
    gp = _mm(y_n, dt1, form="tn", out_pack=("w_mix_out", gp), name="mm_mix_y_dw")
    gp = lax.dynamic_update_slice(
        gp, dw_mix_o.reshape(MLA_HEADS, 128, D_MODEL)[:, :64].reshape(2, 256, D_MODEL), (2, r_mix, 0))
    res = _attn_bwd_dq(q_rot, k_full, v_nat, o_t, do_t, lse, name="attn_bwd_dq",
                       hosted=_pair_exchange_step(gp) if dist else None)
    dq_rot, delta = res[0], res[1]
    chip_step = None
    if dist:
        theirs_a = res[2]
        chip_step = _chip_exchange_step(_pair_sum(gp, theirs_a, "pair_sum_a"))
    res = _attn_bwd_dkv(q_rot, k_full, v_nat, do_t, lse, delta, name="attn_bwd_dkv", hosted=chip_step)
    dk, dv_t = res[0], res[1]
    if dist:
        gp = _chip_sum(gp, theirs_a, res[2], "chip_sum_a")
    dqp, dkr = _rowwise(_fn_rope_bwd, [dq_rot, dk, rope_a, rope_b, rope_c], [], [(1024, BF16), (128, BF16)], tr=tr,
                        name="rope_bwd")
    dw_q_p = _mm(q_n, dqp, form="tn", name="mm_q_up_dw")
    dq_n = _mm(dqp, w_q_p, form="nt", name="mm_q_up_dx")
    dw_k_p = _mm(kv_n, dk, form="tn", name="mm_k_up_dw")
    dkv_n1 = _mm(dk, w_k_p, form="nt", name="mm_k_up_dx")
    dw_v_pt = _mm(dv_t, kv_n, form="nn", name="mm_v_up_dw")
    dkv_n2 = _mm(dv_t, w_v_pt, form="tn", name="mm_v_up_dx")
    dq_lat, dkv_lat, dg_q, dg_kv = _rowwise(
        _fn_mla_pre_bwd, [(proj,) + SEG_QLAT, (proj,) + SEG_KVLAT, dq_n, dkv_n1, dkv_n2], [g_q, g_kv],
        [(384, BF16), (256, BF16)], [(1, 384), (1, 256)], tr=tr, name="mla_pre_bwd")
    dy_ssd, dz, dxs_skip, ddexp, dg_ssd = _rowwise(
        _fn_ssd_post_bwd, [dy_n, y_ssd, (xbc, 0, 512), (proj,) + SEG_Z], [dexp, g_ssd],
        [512, (512, BF16), 512], [(1, 512)] * 2, tr=tr, name="ssd_post_bwd")
    dxs, dbc, ddt, da_head = _ssd_bwd(xbc, dt, a_row, hs, dy_ssd, name="ssd_bwd")
    dyc, ddtr, dconv_b, ddt_b = _rowwise(
        _fn_conv_bwd_a, [conv_y, dxs, dxs_skip, dbc, (proj,) + SEG_DT, ddt], [dt_b], [1024, (128, BF16)],
        [(1, 1024), (1, 128)], tr=tr, name="conv_bwd_a")
    dxbc, dconv_w8 = _rowwise(
        _fn_conv_bwd_b, [dyc, ("next", dyc, 0, 1024), (proj,) + SEG_XBC, ("prev", proj) + SEG_XBC], [conv_w8],
        [(1024, BF16)], [(8, 1024)], tr=tr, name="conv_bwd_b")
    dproj = jnp.concatenate([dxbc, dz, dq_lat, ddtr, dkv_lat, dkr, jnp.zeros((S, 128), BF16)], axis=1)
    res = _mm(h0_b, dproj, form="tn", tn=IN_W // 2, name="mm_in_dw", hosted=_pair_fill_step(gp) if dist else None)
    dw_in_p, red_a = (res[0], res[1]) if dist else (res, None)
    big_b = _group_b_grads(dw_in_p, dw_q_p, dw_k_p, dw_v_pt, dconv_w8)
    q_b = None
    if dist:
        gp_c, gp_b = _pack_group_b(big_b)
        dh0, theirs_c, theirs_b = _mm(dproj, w_in_p, form="nt", tk=IN_W // 2, name="mm_in_dx", hosted=_merge_steps(
            [_pair_exchange_step(gp_c), _pair_exchange_step(gp_b)]))
        q_b = ((gp_c, theirs_c, _pair_sum(gp_c, theirs_c, "pair_sum_w_in")),
               (gp_b, theirs_b, _pair_sum(gp_b, theirs_b, "pair_sum_b")))
        gp = red_a
    else:
        dh0 = _mm(dproj, w_in_p, form="nt", tk=IN_W // 2, name="mm_in_dx")
    grad_x, dg_in, db_in = _rowwise(_fn_in_ln_bwd, [x, dt1, dh0], [g_in], [D_MODEL], [(1, D_MODEL)] * 2, tr=tr,
                                    name="ln_in_bwd")

    small = {
        "ln_in_g": dg_in, "ln_in_b": db_in, "conv_b": dconv_b, "dt_bias": ddt_b[:, :8],
        "a_log": da_head[:, :8] * a_head.reshape(1, 8),
        "d_skip": ddexp.reshape(8, 64).sum(axis=1).reshape(1, 8),
        "ssd_norm_g": dg_ssd, "q_norm_g": dg_q, "kv_norm_g": dg_kv,
        "ln1_g": dg1, "ln1_b": db1, "ln2_g": dg2, "ln2_b": db2, "ln3_g": dg3, "ln3_b": db3,
    }
    return loss[0, 0], grad_x, (gp, q_b), big_b, small


def _adam(w, g, m, v, name):
    shape = w.shape
    w2, m2, v2 = (t.reshape(-1, shape[-1]) for t in (w, m, v))
    g2 = (g[0], 0, shape[-1], g[1]) if isinstance(g, tuple) else g.reshape(-1, shape[-1])
    d, mn, vn = _rowwise(_fn_adam, [w2, g2, m2, v2], [], [shape[-1]] * 3, tr=ROW_TILE, name=name)
    return d.reshape(shape), mn.reshape(shape), vn.reshape(shape)


def kernel(x, mem, positions, ln_in_g, ln_in_b, w_in, conv_w, conv_b, dt_bias, a_log, d_skip, ssd_norm_g, q_norm_g, w_q_up, kv_norm_g, w_kv_up, w_mix_out, ln1_g, ln1_b, w_mem_q, w_mem_k, w_mem_v, w_mem_o, ln2_g, ln2_b, w_up, w_down, ln3_g, ln3_b, loss_target, m_ln_in_g, m_ln_in_b, m_w_in, m_conv_w, m_conv_b, m_dt_bias, m_a_log, m_d_skip, m_ssd_norm_g, m_q_norm_g, m_w_q_up, m_kv_norm_g, m_w_kv_up, m_w_mix_out, m_ln1_g, m_ln1_b, m_w_mem_q, m_w_mem_k, m_w_mem_v, m_w_mem_o, m_ln2_g, m_ln2_b, m_w_up, m_w_down, m_ln3_g, m_ln3_b, v_ln_in_g, v_ln_in_b, v_w_in, v_conv_w, v_conv_b, v_dt_bias, v_a_log, v_d_skip, v_ssd_norm_g, v_q_norm_g, v_w_q_up, v_kv_norm_g, v_w_kv_up, v_w_mix_out, v_ln1_g, v_ln1_b, v_w_mem_q, v_w_mem_k, v_w_mem_v, v_w_mem_o, v_ln2_g, v_ln2_b, v_w_up, v_w_down, v_ln3_g, v_ln3_b):
    args = dict(locals())

    wp_a = jnp.concatenate([args[n].reshape(-1, PACK_COLS).astype(BF16) for n in PACK_A_ORDER], axis=0)
    flat = [args[n].reshape(-1).astype(BF16) for n in PACK_B_ORDER[:-1]]
    flat.append(lax.bitcast_convert_type(conv_w.reshape(-1), BF16).reshape(-1))
    used = sum(f.shape[0] for f in flat)
    flat.append(jnp.zeros((PACK_B_ROWS * PACK_COLS - used,), BF16))
    wp_b = jnp.concatenate(flat).reshape(PACK_B_ROWS, PACK_COLS)
    wp_c = w_in[0].astype(BF16)

    P = {n: args[n] for n in SMALL_ORDER}
    loss, grad_x, (red_a, ((gp_c, theirs_c, pb_c), (gp_b, theirs_b, pb_b))), _, gsmall = _local_step(
        x[0], mem[0], positions[0], loss_target[0], None, P, wp_a=wp_a, wp_b=(wp_c, wp_b))

    gs = jnp.concatenate([_row(gsmall[n], PACK_COLS) for n in SMALL_ORDER] + [_row(loss, PACK_COLS)], axis=0)
    gs, got_c, got_b = _small_all_reduce(gs, _merge_steps([_chip_exchange_step(pb_c), _chip_exchange_step(pb_b)]))
    loss = gs[len(SMALL_ORDER), 0]
    red_c, red_b = _run_step(_merge_steps([_pair_fill_step(_chip_sum(gp_c, theirs_c, got_c, "chip_sum_w_in")),
                                           _pair_fill_step(_chip_sum(gp_b, theirs_b, got_b, "chip_sum_b"))]),
                             "pair_fill_b")

    grads, deltas, new_m, new_v = {}, {}, {}, {}
    for n in PACK_A_ORDER:
        r0, (sr, _) = PACK_A_ROW[n], _shard_shape(n)
        grads[n] = red_a[r0:r0 + sr].reshape(args[n].shape)
        deltas[n], new_m[n], new_v[n] = _adam(args[n], (red_a, r0), args["m_" + n], args["v_" + n], "adam_" + n)
    grads["w_in"] = red_c.reshape(w_in.shape)
    deltas["w_in"], new_m["w_in"], new_v["w_in"] = _adam(w_in, grads["w_in"], m_w_in, v_w_in, "adam_w_in")
    red_b = red_b.reshape(-1)
    off = 0
    for n in PACK_B_ORDER:
        sr, sc = _shard_shape(n)
        grads[n] = red_b[off:off + sr * sc].reshape(args[n].shape)
        off += sr * sc
        deltas[n], new_m[n], new_v[n] = _adam(args[n], grads[n], args["m_" + n], args["v_" + n], "adam_" + n)
    pack = lambda pre: jnp.concatenate([_row(args[pre + n], PACK_COLS) for n in SMALL_ORDER]
                                       + [jnp.zeros((1, PACK_COLS), F32)], axis=0)
    ds, ms, vs = _rowwise(_fn_adam, [pack(""), gs, pack("m_"), pack("v_")], [], [PACK_COLS] * 3, tr=16,
                          name="adam_small")
    for i, n in enumerate(SMALL_ORDER):
        cnt = args[n].size
        take = lambda t: t[i, :cnt].reshape(args[n].shape)
        grads[n], deltas[n], new_m[n], new_v[n] = take(gs), take(ds), take(ms), take(vs)

    order = ["ln_in_g", "ln_in_b", "w_in", "conv_w", "conv_b", "dt_bias", "a_log", "d_skip", "ssd_norm_g",
             "q_norm_g", "w_q_up", "kv_norm_g", "w_kv_up", "w_mix_out", "ln1_g", "ln1_b", "w_mem_q", "w_mem_k",
             "w_mem_v", "w_mem_o", "ln2_g", "ln2_b", "w_up", "w_down", "ln3_g", "ln3_b"]
    return (loss, grad_x[None], *[grads[n] for n in order], *[deltas[n] for n in order],
            *[new_m[n] for n in order], *[new_v[n] for n in order])
```

```python
import functools

import jax
import jax.numpy as jnp
from jax import lax
from jax.experimental import pallas as pl
from jax.experimental.pallas import tpu as pltpu

F32 = jnp.float32
BF16 = jnp.bfloat16
MESH = pl.DeviceIdType.MESH

D_MODEL = 1024
SSD_CHUNK = 128
SSD_STATE = 128
MLA_HEADS = 8
MLA_ROPE = 32
MLA_QK = 96
MLA_KV_RANK = 256
ROPE_THETA = 10000.0
MEM_HEADS = 4
MEM_HEAD_DIM = 256
LN_EPS = 1e-5
RMS_EPS = 1e-6
ALPHA = 2.0 ** 0.25
ADAM_LR = 0.001
ADAM_B1 = 0.9
ADAM_B2 = 0.999
ADAM_EPS = 1e-08
ADAM_WD = 0.01
ADAM_STEP = 10

LANES = 128
IN_W = 2560
SEG_XBC = (0, 1024)
SEG_Z = (1024, 512)
SEG_QLAT = (1536, 384)
SEG_DT = (1920, 128)
SEG_KVLAT = (2048, 256)
SEG_KR = (2304, 128)
VMEM_LIMIT = 56 * 1024 * 1024
ATTN_TILE = 512
ROW_TILE = 512
SSD_PER_STEP = 2
NEG = -1e30
MLA_SCALE = MLA_QK ** -0.5
MEM_SCALE = MEM_HEAD_DIM ** -0.5

NN = (((1,), (0,)), ((), ()))
NT = (((1,), (1,)), ((), ()))
TN = (((0,), (0,)), ((), ()))


def _dot(a, b, dims=NN):
    return lax.dot_general(a.astype(BF16), b.astype(BF16), dims, preferred_element_type=F32)


def _dot_exact(a, b):
    return lax.dot_general(a, b, NN, precision=lax.Precision.HIGHEST, preferred_element_type=F32)


def _pick(dim, pref):
    t = min(pref, dim)
    t -= t % LANES
    while t >= LANES:
        if dim % t == 0:
            return t
        t -= LANES
    return dim


def _params(sem):
    return pltpu.CompilerParams(dimension_semantics=sem, vmem_limit_bytes=VMEM_LIMIT)


def _pack_caps(wname):
    r, c, ax = BIG[wname]
    if ax == 0:
        return (r if r <= 1024 else r // N_SHARD), c
    return r, c // N_SHARD


def _pack_block(wname, br, bc):
    r, c, ax = BIG[wname]
    r0 = PACK_A_ROW[wname]
    sr = r // N_SHARD if ax == 0 else r
    if ax == 0 and br > sr:
        assert br % sr == 0 and r0 % sr == 0
        return (br // sr, sr, bc), lambda rb, cb: (rb, r0 // sr, cb)
    assert r0 % br == 0
    if ax == 0:
        per = sr // br
        return (1, br, bc), lambda rb, cb: (rb // per, r0 // br + rb % per, cb)
    per = (c // N_SHARD) // bc
    return (1, br, bc), lambda rb, cb: (cb // per, r0 // br + rb, cb % per)


def _mm(a, b, *, form, name, a_pro=None, epi=None, out_dtype=F32, tm=1024, tn=1024, tk=1024, b_pack=None,
        b_rows=None, out_pack=None, hosted=None):
    b_shape = BIG[b_pack][:2] if b_pack else b.shape
    if b_pack and form == "nt":
        b_shape = (b_rows or b_shape[0], b_shape[1])
    if form == "nn":
        (m, k), (_, n) = a.shape, b_shape
    elif form == "nt":
        (m, k), (n, _) = a.shape, b_shape
    else:
        (k, m), (_, n) = a.shape, b_shape
    if b_pack:
        rcap, ccap = _pack_caps(b_pack)
        tk, tn = (min(tk, rcap), min(tn, ccap)) if form == "nn" else (min(tk, ccap), min(tn, rcap))
    if out_pack:
        rcap, ccap = _pack_caps(out_pack[0])
        tm, tn = min(tm, rcap), min(tn, ccap)
    tm, tn, tk = _pick(m, tm), _pick(n, tn), _pick(k, tk)
    dims = {"nn": NN, "nt": NT, "tn": TN}[form]
    nk = k // tk
    direct = out_dtype == F32 and epi is None
    n_extra = (1 if epi else 0) + (1 if out_pack else 0)

    def body(a_ref, b_ref, *rest):
        o_ref = rest[n_extra]
        acc_ref = o_ref if direct else rest[-1]

        @pl.when(pl.program_id(2) == 0)
        def _():
            acc_ref[...] = jnp.zeros_like(acc_ref)

        av = a_ref[...]
        if a_pro is not None:
            av = a_pro(av)
        bv = b_ref[...]
        acc_ref[...] += _dot(av, bv.reshape(-1, bv.shape[-1]), dims).reshape(acc_ref.shape)
        if not direct:
            @pl.when(pl.program_id(2) == nk - 1)
            def _():
                val = acc_ref[...]
                if epi is not None:
                    val = epi[0](val, rest[0][...])
                o_ref[...] = val.reshape(o_ref.shape).astype(o_ref.dtype)

    if form == "tn":
        a_spec = pl.BlockSpec((tk, tm), lambda i, j, kk: (kk, i))
    else:
        a_spec = pl.BlockSpec((tm, tk), lambda i, j, kk: (i, kk))
    if b_pack:
        shape, idx = _pack_block(b_pack, *((tk, tn) if form == "nn" else (tn, tk)))
        b_spec = pl.BlockSpec(shape, (lambda i, j, kk: idx(kk, j)) if form == "nn" else (lambda i, j, kk: idx(j, kk)))
    elif form == "nt":
        b_spec = pl.BlockSpec((tn, tk), lambda i, j, kk: (j, kk))
    else:
        b_spec = pl.BlockSpec((tk, tn), lambda i, j, kk: (kk, j))
    in_specs, args = [a_spec, b_spec], [a, b]
    out_spec = pl.BlockSpec((tm, tn), lambda i, j, kk: (i, j))
    out_sds, aliases = jax.ShapeDtypeStruct((m, n), out_dtype), {}
    if epi is not None:
        in_specs.append(out_spec)
        args.append(epi[1])
    if out_pack:
        wname, buf = out_pack
        shape, idx = _pack_block(wname, tm, tn)
        out_spec = pl.BlockSpec(shape, lambda i, j, kk: idx(i, j))
        out_sds, aliases = jax.ShapeDtypeStruct(buf.shape, buf.dtype), {len(args): 0}
        in_specs.append(HBM)
        args.append(buf)
    acc_shape = out_spec.block_shape if out_pack else (tm, tn)
    res = _call_with_step(
        body, hosted, None, args, name=name, grid=(m // tm, n // tn, nk), in_specs=in_specs, out_specs=[out_spec],
        out_shape=[out_sds], sem=("parallel", "parallel", "arbitrary"), aliases=aliases,
        scratch_shapes=[] if direct else [pltpu.VMEM(acc_shape, F32)])
    return res[0] if hosted is None else res


class _Ctx:
    def __init__(self, i, n):
        self.i, self.n = i, n


def _rowwise(fn, rows, consts, row_outs, acc_outs=(), *, tr, name, n_rows=None, hosted=None):
    norm = []
    for r in rows:
        kind = "tile"
        if isinstance(r, tuple) and isinstance(r[0], str):
            kind, r = r[0], r[1:]
        row0 = 0
        if isinstance(r, tuple) and len(r) == 4:
            r, row0 = r[:3], r[3]
        arr, col0, width = r if isinstance(r, tuple) else (r, 0, r.shape[1])
        assert col0 % width == 0
        norm.append((kind, arr, col0 // width, width, row0))
    n_rows = n_rows or next(a.shape[0] for k, a, _, _, _ in norm if k == "tile")
    tr = min(tr, n_rows)
    while n_rows % tr:
        tr -= 8
    n = n_rows // tr
    arrs, specs = [], []
    for kind, arr, cb, width, row0 in norm:
        if kind == "tile":
            assert row0 % tr == 0
            specs.append(pl.BlockSpec((tr, width), lambda i, cb=cb, rb=row0 // tr: (i + rb, cb)))
        elif kind == "prev":
            specs.append(pl.BlockSpec((8, width), lambda i, cb=cb: (jnp.maximum(i * (tr // 8) - 1, 0), cb)))
        else:
            specs.append(pl.BlockSpec((8, width), lambda i, cb=cb: (jnp.minimum((i + 1) * (tr // 8), n_rows // 8 - 1), cb)))
        arrs.append(arr)
    for c in consts:
        specs.append(pl.BlockSpec(c.shape, lambda i, nd=c.ndim: (0,) * nd))
        arrs.append(c)
    n_in, n_ro = len(arrs), len(row_outs)
    row_outs = [w if isinstance(w, tuple) else (w, F32) for w in row_outs]
    out_shape = [jax.ShapeDtypeStruct((n_rows, w), dt) for w, dt in row_outs]
    out_specs = [pl.BlockSpec((tr, w), lambda i: (i, 0)) for w, _ in row_outs]
    out_shape += [jax.ShapeDtypeStruct(s, F32) for s in acc_outs]
    out_specs += [pl.BlockSpec(s, lambda i: (0, 0)) for s in acc_outs]

    def body(*refs):
        i = pl.program_id(0)
        vals = [r[...] for r in refs[:n_in]]
        outs = fn(_Ctx(i, n), *vals)
        if not isinstance(outs, (tuple, list)):
            outs = (outs,)
        o_refs = refs[n_in:]
        for o_ref, o in zip(o_refs[:n_ro], outs[:n_ro]):
            o_ref[...] = o.astype(o_ref.dtype)
        if acc_outs:
            @pl.when(i == 0)
            def _():
                for o_ref in o_refs[n_ro:]:
                    o_ref[...] = jnp.zeros_like(o_ref)

            for o_ref, o in zip(o_refs[n_ro:], outs[n_ro:]):
                o_ref[...] += jnp.broadcast_to(o, o_ref.shape)

    return _call_with_step(body, hosted, None, arrs, name=name, grid=(n,), in_specs=specs, out_specs=out_specs,
                           out_shape=out_shape, sem=("arbitrary",))


def _sum0(v):
    return jnp.sum(v, axis=0, keepdims=True)


def _mean1(v):
    return jnp.mean(v, axis=-1, keepdims=True)


def _sigmoid(v):
    return 1.0 / (1.0 + jnp.exp(-v))


def _ln_stats(t):
    xc = t - _mean1(t)
    rstd = lax.rsqrt(_mean1(xc * xc) + LN_EPS)
    return xc * rstd, rstd


def _ln_bwd(xhat, rstd, dy, g):
    dxh = dy * g
    dx = rstd * (dxh - _mean1(dxh) - xhat * _mean1(dxh * xhat))
    return dx, _sum0(dy * xhat), _sum0(dy)


def _rms_fwd(v, g):
    return v * lax.rsqrt(_mean1(v * v) + RMS_EPS) * g


def _rms_bwd(v, dy, g):
    rs = lax.rsqrt(_mean1(v * v) + RMS_EPS)
    vh = v * rs
    dyg = dy * g
    return rs * (dyg - vh * _mean1(dyg * vh)), _sum0(dy * vh)


def _lane(shape):
    return lax.broadcasted_iota(jnp.int32, shape, len(shape) - 1)


def _shift_down(u, halo, s, is_first):
    tr = u.shape[0]
    rolled = pltpu.roll(u, s, 0)
    hr = jnp.where(is_first, 0.0, pltpu.roll(halo, s, 0))
    row = lax.broadcasted_iota(jnp.int32, hr.shape, 0)
    top = jnp.where(row < s, hr, rolled[0:8])
    if tr == 8:
        return top
    return jnp.concatenate([top, rolled[8:]], axis=0)


def _shift_up(d, halo, s, is_last):
    tr = d.shape[0]
    rolled = pltpu.roll(d, tr - s, 0)
    hr = jnp.where(is_last, 0.0, pltpu.roll(halo, 8 - s, 0))
    row = lax.broadcasted_iota(jnp.int32, hr.shape, 0)
    bot = jnp.where(row >= 8 - s, hr, rolled[tr - 8:])
    if tr == 8:
        return bot
    return jnp.concatenate([rolled[:tr - 8], bot], axis=0)


def _rope(v, ta, tb, tc):
    return v * ta + pltpu.roll(v, 16, 1) * tb + pltpu.roll(v, LANES - 16, 1) * tc


def _rope_bwd(d, ta, tb, tc):
    return d * ta + pltpu.roll(d * tb, LANES - 16, 1) + pltpu.roll(d * tc, 16, 1)


def _ssd_common(dtv, a_row):
    L = SSD_CHUNK
    a = dtv * a_row
    r = lax.broadcasted_iota(jnp.int32, (L, L), 0)
    c = lax.broadcasted_iota(jnp.int32, (L, L), 1)
    tril = r >= c
    cs = _dot_exact(tril.astype(F32), a)
    cs_t = cs.T
    cs_last = cs[L - 1:L, :]
    return dict(a=a, tril=tril, cs=cs, cs_t=cs_t, ecs=jnp.exp(cs), dte=jnp.exp(cs_last - cs),
                elast=jnp.exp(cs_last))


def _pair_sel(v, h0, lo):
    return jnp.where(lo, v[:, h0:h0 + 1], v[:, h0 + 1:h0 + 2])


def _ssd_pair(cm, h0, cb, xp, dtv, bmat, cmat, hp, lo):
    x = xp * _pair_sel(dtv, h0, lo)
    lam0 = jnp.exp(jnp.where(cm["tril"], cm["cs"][:, h0:h0 + 1] - cm["cs_t"][h0:h0 + 1, :], NEG))
    lam1 = jnp.exp(jnp.where(cm["tril"], cm["cs"][:, h0 + 1:h0 + 2] - cm["cs_t"][h0 + 1:h0 + 2, :], NEG))
    m0, m1 = cb * lam0, cb * lam1
    ydiag = jnp.where(lo, _dot(m0, x), _dot(m1, x))
    ecs_p = _pair_sel(cm["ecs"], h0, lo)
    dte_p = _pair_sel(cm["dte"], h0, lo)
    yoff = _dot(cmat, hp, NT) * ecs_p
    xd = x * dte_p
    st = _dot(xd, bmat, TN)
    rlo = lax.broadcasted_iota(jnp.int32, (LANES, SSD_STATE), 0) < 64
    decay = jnp.where(rlo, cm["elast"][:, h0:h0 + 1], cm["elast"][:, h0 + 1:h0 + 2])
    h_next = hp * decay + st
    return dict(x=x, lam0=lam0, lam1=lam1, m0=m0, m1=m1, y=ydiag + yoff, yoff=yoff, ecs_p=ecs_p, dte_p=dte_p,
                xd=xd, decay=decay, h_next=h_next)


def _ssd_fwd(xbc, dt, a_row, *, name):
    S = xbc.shape[0]
    L = SSD_CHUNK
    nc = S // L
    per = SSD_PER_STEP if nc % SSD_PER_STEP == 0 else 1
    G = per * L

    def body(xs_ref, bm_ref, cm_ref, dt_ref, a_ref, y_ref, hs_ref, h_scr):
        @pl.when(pl.program_id(0) == 0)
        def _():
            h_scr[...] = jnp.zeros_like(h_scr)

        lo = _lane((L, LANES)) < 64
        for sub in range(per):
            rows = slice(sub * L, (sub + 1) * L)
            dtv = dt_ref[rows, :]
            cm = _ssd_common(dtv, a_ref[...])
            ys = []
            for g in range(2):
                bmat = bm_ref[rows, g * 128:(g + 1) * 128]
                cmat = cm_ref[rows, g * 128:(g + 1) * 128]
                cb = _dot(cmat, bmat, NT)
                for pr in range(2):
                    p4 = 2 * g + pr
                    hp = h_scr[p4]
                    hs_ref[sub, p4 * 128:(p4 + 1) * 128, :] = hp
                    t = _ssd_pair(cm, 2 * p4, cb, xs_ref[rows, p4 * 128:(p4 + 1) * 128], dtv, bmat, cmat, hp, lo)
                    ys.append(t["y"])
                    h_scr[p4] = t["h_next"]
            y_ref[rows, :] = jnp.concatenate(ys, axis=1)

    return pl.pallas_call(
        body, name=name, grid=(nc // per,),
        in_specs=[pl.BlockSpec((G, 512), lambda c: (c, 0)), pl.BlockSpec((G, 256), lambda c: (c, 2)),
                  pl.BlockSpec((G, 256), lambda c: (c, 3)), pl.BlockSpec((G, 128), lambda c: (c, 0)),
                  pl.BlockSpec((1, 128), lambda c: (0, 0))],
        out_specs=[pl.BlockSpec((G, 512), lambda c: (c, 0)), pl.BlockSpec((per, 512, 128), lambda c: (c, 0, 0))],
        out_shape=[jax.ShapeDtypeStruct((S, 512), F32), jax.ShapeDtypeStruct((nc, 512, 128), F32)],
        scratch_shapes=[pltpu.VMEM((4, 128, 128), F32)],
        compiler_params=_params(("arbitrary",)),
    )(xbc, xbc, xbc, dt, a_row)


def _ssd_bwd(xbc, dt, a_row, hs, dy, *, name, hosted=None):
    S = xbc.shape[0]
    L = SSD_CHUNK
    nc = S // L
    per = SSD_PER_STEP if nc % SSD_PER_STEP == 0 else 1
    G = per * L

    def body(xs_ref, bm_ref, cm_ref, dt_ref, a_ref, hs_ref, dy_ref, dxs_ref, dbc_ref, ddt_ref, da_ref, g_scr):
        @pl.when(pl.program_id(0) == 0)
        def _():
            g_scr[...] = jnp.zeros_like(g_scr)
            da_ref[...] = jnp.zeros_like(da_ref)

        for sub in reversed(range(per)):
            rows = pl.ds(sub * L, L)
            chunk(xs_ref.at[rows, :], bm_ref.at[rows, :], cm_ref.at[rows, :], dt_ref.at[rows, :], a_ref,
                  hs_ref.at[pl.ds(sub, 1)], dy_ref.at[rows, :], dxs_ref.at[rows, :], dbc_ref.at[rows, :],
                  ddt_ref.at[rows, :], da_ref, g_scr)

    def chunk(xs_ref, bm_ref, cm_ref, dt_ref, a_ref, hs_ref, dy_ref, dxs_ref, dbc_ref, ddt_ref, da_ref, g_scr):
        dtv = dt_ref[...]
        a_row_v = a_ref[...]
        cm = _ssd_common(dtv, a_row_v)
        lo = _lane((L, LANES)) < 64
        lane_row = _lane((1, LANES))
        ri = lax.broadcasted_iota(jnp.int32, (L, L), 0)
        ci = lax.broadcasted_iota(jnp.int32, (L, L), 1)
        triu = (ri <= ci).astype(F32)
        stril = ri > ci

        def halves(v, mask):
            return (jnp.sum(jnp.where(mask, v, 0.0), axis=1, keepdims=True),
                    jnp.sum(jnp.where(mask, 0.0, v), axis=1, keepdims=True))

        i_all = jnp.zeros((L, LANES), F32)
        yo_all = jnp.zeros((L, LANES), F32)
        w_all = jnp.zeros((L, LANES), F32)
        ddt_x = jnp.zeros((L, LANES), F32)
        e_row = jnp.zeros((1, LANES), F32)
        rlo = lax.broadcasted_iota(jnp.int32, (LANES, SSD_STATE), 0) < 64
        dxs, dbs, dcs = [], [], []
        for g in range(2):
            bmat = bm_ref[:, g * 128:(g + 1) * 128]
            cmat = cm_ref[:, g * 128:(g + 1) * 128]
            cb = _dot(cmat, bmat, NT)
            dcb = jnp.zeros((L, L), F32)
            db = jnp.zeros((L, SSD_STATE), F32)
            dc = jnp.zeros((L, SSD_STATE), F32)
            for pr in range(2):
                p4 = 2 * g + pr
                h0 = 2 * p4
                hp = hs_ref[0, p4 * 128:(p4 + 1) * 128, :]
                xp = xs_ref[:, p4 * 128:(p4 + 1) * 128]
                t = _ssd_pair(cm, h0, cb, xp, dtv, bmat, cmat, hp, lo)
                gst = g_scr[p4]
                dyp = dy_ref[:, p4 * 128:(p4 + 1) * 128]
                dy0 = jnp.where(lo, dyp, 0.0)
                dy1 = dyp - dy0
                bg = _dot(bmat, gst, NT)
                dx = _dot(t["m0"], dy0, TN) + _dot(t["m1"], dy1, TN) + bg * t["dte_p"]
                dm0, dm1 = _dot(dy0, t["x"], NT), _dot(dy1, t["x"], NT)
                dcb = dcb + dm0 * t["lam0"] + dm1 * t["lam1"]
                dye = dyp * t["ecs_p"]
                dc = dc + _dot(dye, hp)
                db = db + _dot(t["xd"], gst)
                i0 = jnp.sum(jnp.where(stril, _dot(triu, dm0 * t["m0"]), 0.0), axis=1, keepdims=True)
                i1 = jnp.sum(jnp.where(stril, _dot(triu, dm1 * t["m1"]), 0.0), axis=1, keepdims=True)
                yo0, yo1 = halves(dyp * t["yoff"], lo)
                w0, w1 = halves(t["xd"] * bg, lo)
                gh = gst * (hp * t["decay"])
                e0 = _sum0(jnp.sum(jnp.where(rlo, gh, 0.0), axis=1, keepdims=True))
                e1 = _sum0(jnp.sum(jnp.where(rlo, 0.0, gh), axis=1, keepdims=True))
                x0, x1 = halves(dx * xp, lo)
                oh0 = (lane_row == h0).astype(F32)
                oh1 = (lane_row == h0 + 1).astype(F32)
                i_all = i_all + i0 * oh0 + i1 * oh1
                yo_all = yo_all + yo0 * oh0 + yo1 * oh1
                w_all = w_all + w0 * oh0 + w1 * oh1
                e_row = e_row + e0 * oh0 + e1 * oh1
                ddt_x = ddt_x + x0 * oh0 + x1 * oh1
                dxs.append(dx * _pair_sel(dtv, h0, lo))
                g_scr[p4] = gst * t["decay"] + _dot(dye, cmat, TN)
            dbs.append(db + _dot(dcb, cmat, TN))
            dcs.append(dc + _dot(dcb, bmat))
        da = i_all + _dot_exact(triu, yo_all) + _dot_exact(stril.astype(F32), w_all) + e_row
        ddt_ref[...] = da * a_row_v + ddt_x
        da_ref[...] += _sum0(da * dtv)
        dxs_ref[...] = jnp.concatenate(dxs, axis=1)
        dbc_ref[...] = jnp.concatenate(dbs + dcs, axis=1)

    rev = lambda c: nc // per - 1 - c
    return _call_with_step(
        body, hosted, None, (xbc, xbc, xbc, dt, a_row, hs, dy), name=name, grid=(nc // per,),
        in_specs=[pl.BlockSpec((G, 512), lambda c: (rev(c), 0)), pl.BlockSpec((G, 256), lambda c: (rev(c), 2)),
                  pl.BlockSpec((G, 256), lambda c: (rev(c), 3)), pl.BlockSpec((G, 128), lambda c: (rev(c), 0)),
                  pl.BlockSpec((1, 128), lambda c: (0, 0)), pl.BlockSpec((per, 512, 128), lambda c: (rev(c), 0, 0)),
                  pl.BlockSpec((G, 512), lambda c: (rev(c), 0))],
        out_specs=[pl.BlockSpec((G, 512), lambda c: (rev(c), 0)), pl.BlockSpec((G, 512), lambda c: (rev(c), 0)),
                   pl.BlockSpec((G, 128), lambda c: (rev(c), 0)), pl.BlockSpec((1, 128), lambda c: (0, 0))],
        out_shape=[jax.ShapeDtypeStruct((S, 512), F32), jax.ShapeDtypeStruct((S, 512), F32),
                   jax.ShapeDtypeStruct((S, 128), F32), jax.ShapeDtypeStruct((1, 128), F32)],
        sem=("arbitrary",), scratch_shapes=[pltpu.VMEM((4, 128, 128), F32)])


HBM = pl.BlockSpec(memory_space=pl.ANY)


class _Step:
    def __init__(self, inputs, out_shapes, n_sems, start, finish, mid=None):
        self.inputs, self.out_shapes, self.n_sems = inputs, out_shapes, n_sems
        self.start, self.finish, self.mid = start, finish, mid
        self.alias = []


class _Shifted:
    def __init__(self, ref, off):
        self.ref, self.off = ref, off

    @property
    def at(self):
        return self

    def __getitem__(self, j):
        return self.ref.at[self.off + j]


def _merge_steps(steps):
    offs = [sum(s.n_sems for s in steps[:i]) for i in range(len(steps) + 1)]
    i_offs = [sum(len(s.inputs) for s in steps[:i]) for i in range(len(steps))]
    o_offs = [sum(len(s.out_shapes) for s in steps[:i]) for i in range(len(steps))]

    def phase(which):
        def run(ins, outs, sems):
            for s, off, i0, o0 in zip(steps, offs, i_offs, o_offs):
                fn = getattr(s, which)
                if fn is not None:
                    fn(ins[i0:i0 + len(s.inputs)], outs[o0:o0 + len(s.out_shapes)],
                       [_Shifted(sems[0], off), _Shifted(sems[1], off)])
        return run

    merged = _Step([a for s in steps for a in s.inputs], [o for s in steps for o in s.out_shapes], offs[-1],
                   phase("start"), phase("finish"), phase("mid") if any(s.mid for s in steps) else None)
    merged.alias = [(i0 + a, o0 + b) for s, i0, o0 in zip(steps, i_offs, o_offs) for a, b in s.alias]
    return merged


def _place():
    x, y, c = lax.axis_index("x"), lax.axis_index("y"), lax.axis_index("c")
    chips = [(1 - x, y), (x, 1 - y), (1 - x, 1 - y)]
    return x, y, c, chips


def _mesh_pos():
    return 2 * lax.axis_index("x") + lax.axis_index("y"), lax.axis_index("c")


def _chunks(rows, tile):
    return next(n for n in (4, 3, 2, 1) if rows % (n * tile) == 0)


def _remote(src, dst, sems, j, to):
    return pltpu.make_async_remote_copy(src_ref=src, dst_ref=dst, send_sem=sems[0].at[j], recv_sem=sems[1].at[j],
                                        device_id=to, device_id_type=MESH)


def _gather_step(wp):
    R, C = wp.shape
    H = R // 2
    nq = _chunks(H, 16)
    CH = H // nq

    def copies(ins, outs, sems):
        x, y, c, chips = _place()
        sib, me = (x, y, 1 - c), 2 * x + y
        w_ref, out_ref = ins[0], outs[0]

        def piece(k, hc, q):
            return out_ref.at[k, pl.ds(hc * H + q * CH, CH), :]

        sends, landed, fwds, fwd_landed = [], [], [], []
        for q in range(nq):
            for j, (px, py) in enumerate(chips):
                k = 2 * px + py
                sends.append(_remote(w_ref.at[pl.ds(c * H + q * CH, CH), :], piece(me, c, q), sems, j * nq + q,
                                     (px, py, c)))
                landed.append(_remote(piece(k, c, q), piece(k, c, q), sems, j * nq + q, (px, py, c)))
                fwds.append(_remote(piece(k, c, q), piece(k, c, q), sems, (3 + j) * nq + q, sib))
                fwd_landed.append(_remote(piece(k, 1 - c, q), piece(k, 1 - c, q), sems, (3 + j) * nq + q, sib))
        return sends, landed, fwds, fwd_landed

    def start(ins, outs, sems):
        for cp in copies(ins, outs, sems)[0]:
            cp.start()

    def mid(ins, outs, sems):
        _, landed, fwds, _ = copies(ins, outs, sems)
        for arrived, onward in zip(landed, fwds):
            arrived.wait_recv()
            onward.start()

    def finish(ins, outs, sems):
        sends, _, fwds, fwd_landed = copies(ins, outs, sems)
        for cp in fwd_landed:
            cp.wait_recv()
        for cp in sends + fwds:
            cp.wait_send()

    return _Step([wp], [jax.ShapeDtypeStruct((N_SHARD, R, C), wp.dtype)], 6 * nq, start, finish, mid)


def _pair_exchange_step(gp):
    n, R, C = gp.shape
    H = R // 2
    nq = _chunks(H, 8)
    CH = H // nq

    def copies(ins, outs, sems):
        x, y, c, _ = _place()
        return [_remote(ins[0].at[k, pl.ds((1 - c) * H + q * CH, CH), :], outs[0].at[k, pl.ds(q * CH, CH), :], sems,
                        k * nq + q, (x, y, 1 - c)) for k in range(n) for q in range(nq)]

    def start(ins, outs, sems):
        for cp in copies(ins, outs, sems):
            cp.start()

    def finish(ins, outs, sems):
        for cp in copies(ins, outs, sems):
            cp.wait()

    return _Step([gp], [jax.ShapeDtypeStruct((n, H, C), gp.dtype)], n * nq, start, finish)


def _chip_exchange_step(pb):
    n, H, C = pb.shape
    nq = _chunks(H, 16)
    CH = H // nq

    def copies(ins, outs, sems):
        x, y, c, chips = _place()
        return [_remote(ins[0].at[2 * px + py, pl.ds(q * CH, CH), :], outs[0].at[j, pl.ds(q * CH, CH), :], sems,
                        j * nq + q, (px, py, c)) for q in range(nq) for j, (px, py) in enumerate(chips)]

    def start(ins, outs, sems):
        for cp in copies(ins, outs, sems):
            cp.start()

    def finish(ins, outs, sems):
        for cp in copies(ins, outs, sems):
            cp.wait()

    return _Step([pb], [jax.ShapeDtypeStruct((3, H, C), pb.dtype)], 3 * nq, start, finish)


def _pair_fill_step(red):
    R, C = red.shape
    H = R // 2
    nq = _chunks(H, 8)
    CH = H // nq

    def copies(ins, outs, sems):
        x, y, c, _ = _place()
        return [_remote(ins[0].at[pl.ds(c * H + j * CH, CH), :], outs[0].at[pl.ds(c * H + j * CH, CH), :], sems, j,
                        (x, y, 1 - c)) for j in range(nq)]

    def start(ins, outs, sems):
        for cp in copies(ins, outs, sems):
            cp.start()

    def finish(ins, outs, sems):
        for cp in copies(ins, outs, sems):
            cp.wait()

    step = _Step([red], [jax.ShapeDtypeStruct((R, C), red.dtype)], nq, start, finish)
    step.alias = [(0, 0)]
    return step


def _sem_scratch(step):
    return [pltpu.SemaphoreType.DMA((step.n_sems,)), pltpu.SemaphoreType.DMA((step.n_sems,))]


def _run_step(step, name):
    ni, no = len(step.inputs), len(step.out_shapes)

    def body(*refs):
        ins, outs, sems = refs[:ni], refs[ni:ni + no], refs[ni + no:]
        step.start(ins, outs, sems)
        if step.mid is not None:
            step.mid(ins, outs, sems)
        step.finish(ins, outs, sems)

    return pl.pallas_call(body, name=name, in_specs=[HBM] * ni, out_specs=[HBM] * no, out_shape=step.out_shapes,
                          input_output_aliases=dict(step.alias),
                          scratch_shapes=_sem_scratch(step))(*step.inputs)


def _grid_flags(grid):
    ids = [pl.program_id(d) for d in range(len(grid))]
    first = functools.reduce(lambda a, b: a & b, [i == 0 for i in ids])
    last = functools.reduce(lambda a, b: a & b, [i == n - 1 for i, n in zip(ids, grid)])
    return first, last, last


def _call_with_step(core, step, flags, args, *, name, grid, in_specs, out_specs, out_shape, sem, scratch_shapes=(),
                    aliases=None):
    aliases = aliases or {}
    if step is None:
        return pl.pallas_call(core, name=name, grid=grid, in_specs=in_specs, out_specs=out_specs,
                              out_shape=out_shape, scratch_shapes=list(scratch_shapes),
                              input_output_aliases=aliases, compiler_params=_params(sem))(*args)
    n_in, n_out, n_scr = len(in_specs), len(out_specs), len(scratch_shapes)
    si, so = len(step.inputs), len(step.out_shapes)
    flags = flags or (lambda: _grid_flags(grid))
    aliases = {**aliases, **{n_in + a: n_out + b for a, b in step.alias}}

    def body(*refs):
        ins, s_ins = refs[:n_in], refs[n_in:n_in + si]
        outs = refs[n_in + si:n_in + si + n_out]
        s_outs = refs[n_in + si + n_out:n_in + si + n_out + so]
        scr = refs[n_in + si + n_out + so:n_in + si + n_out + so + n_scr]
        sems = refs[n_in + si + n_out + so + n_scr:]
        first, middle, last = flags()

        @pl.when(first)
        def _():
            step.start(s_ins, s_outs, sems)

        if step.mid is not None:
            @pl.when(middle)
            def _():
                step.mid(s_ins, s_outs, sems)

        core(*ins, *outs, *scr)

        @pl.when(last)
        def _():
            step.finish(s_ins, s_outs, sems)

    return pl.pallas_call(
        body, name=name, grid=grid, in_specs=list(in_specs) + [HBM] * si, out_specs=list(out_specs) + [HBM] * so,
        out_shape=list(out_shape) + list(step.out_shapes), scratch_shapes=list(scratch_shapes) + _sem_scratch(step),
        input_output_aliases=aliases, compiler_params=_params(("arbitrary",) * len(grid)))(*args, *step.inputs)


def _attn_flags(nq):
    h, qi = pl.program_id(0), pl.program_id(1)
    return ((h == 0) & (qi == 0), (h == MLA_HEADS - 1) & (qi == 0), (h == MLA_HEADS - 1) & (qi == nq - 1))


def _att_mask(s_t, q0, k0):
    krow = k0 + lax.broadcasted_iota(jnp.int32, s_t.shape, 0)
    qcol = q0 + lax.broadcasted_iota(jnp.int32, s_t.shape, 1)
    return jnp.where(krow <= qcol, s_t, NEG)


def _loop_blocks(lo, hi, step, carry):
    n = hi - lo

    def four(i, c):
        kb = lo + 4 * i
        return step(kb + 3, step(kb + 2, step(kb + 1, step(kb, c))))

    carry = lax.fori_loop(0, n // 4, four, carry)
    base = lo + 4 * (n // 4)
    carry = lax.cond(n % 4 >= 2, lambda c: step(base + 1, step(base, c)), lambda c: c, carry)
    return lax.cond(n % 2 == 1, lambda c: step(hi - 1, c), lambda c: c, carry)


def _rows(ref, blk, t):
    return ref[pl.ds(pl.multiple_of(blk * t, t), t), :]


def _cols(ref, blk, t):
    return ref[:, pl.ds(pl.multiple_of(blk * t, t), t)]


def _attn_fwd(q, k, v_t, *, name, hosted=None):
    S = q.shape[0]
    t = min(ATTN_TILE, S)
    nq = S // t

    def body(q_ref, k_ref, vt_ref, o_ref, lse_ref):
        qi = pl.program_id(1)
        qv = q_ref[...]

        def absorb(kb, carry, masked):
            m, l, acc = carry
            s_t = lax.dot_general(_rows(k_ref, kb, t), qv, NT, preferred_element_type=F32)
            if masked:
                s_t = _att_mask(s_t, qi * t, kb * t)
            m_new = jnp.maximum(m, jnp.max(s_t, axis=0, keepdims=True))
            p_t = jnp.exp(s_t - m_new)
            corr = jnp.exp(m - m_new)
            return (m_new, corr * l + jnp.sum(p_t, axis=0, keepdims=True),
                    corr * acc + lax.dot_general(_cols(vt_ref, kb, t), p_t.astype(BF16), NN,
                                                 preferred_element_type=F32))

        init = (jnp.full((1, t), NEG, F32), jnp.zeros((1, t), F32), jnp.zeros((LANES, t), F32))
        carry = _loop_blocks(0, qi, lambda kb, c: absorb(kb, c, False), init)
        m, l, acc = absorb(qi, carry, True)
        o_ref[...] = acc / l
        lse_ref[0] = m + jnp.log(l)

    return _call_with_step(
        body, hosted, lambda: _attn_flags(nq), (q, k, v_t), name=name, grid=(MLA_HEADS, nq),
        in_specs=[pl.BlockSpec((t, LANES), lambda h, qi: (qi, h)),
                  pl.BlockSpec((S, LANES), lambda h, qi: (0, h)),
                  pl.BlockSpec((LANES, S), lambda h, qi: (h, 0))],
        out_specs=[pl.BlockSpec((LANES, t), lambda h, qi: (h, qi)),
                   pl.BlockSpec((1, 1, t), lambda h, qi: (h, 0, qi))],
        out_shape=[jax.ShapeDtypeStruct((MLA_HEADS * LANES, S), F32), jax.ShapeDtypeStruct((MLA_HEADS, 1, S), F32)],
        sem=("parallel", "arbitrary"))


def _attn_bwd(q, k, v, o_t, do_t, lse, *, name, hosted=None):
    S = q.shape[0]
    t = min(ATTN_TILE, S)
    nq = S // t

    def body(q_ref, k_ref, v_ref, o_ref, do_ref, lse_ref, dq_ref, dk_ref, dv_ref, dkt_scr):
        qi = pl.program_id(1)

        @pl.when(qi == 0)
        def _():
            dkt_scr[...] = jnp.zeros_like(dkt_scr)
            dv_ref[...] = jnp.zeros_like(dv_ref)

        qv = q_ref[...]
        q_t = qv.T
        dov = do_ref[...]
        delta = jnp.sum(dov * o_ref[...], axis=0, keepdims=True)
        dob = dov.astype(BF16)
        lse_v = lse_ref[0]

        def step(kb, acc, masked):
            kt = _rows(k_ref, kb, t)
            s_t = lax.dot_general(kt, qv, NT, preferred_element_type=F32)
            if masked:
                s_t = _att_mask(s_t, qi * t, kb * t)
            p_t = jnp.exp(s_t - lse_v)
            dp_t = lax.dot_general(_rows(v_ref, kb, t), dob, NN, preferred_element_type=F32)
            ds_t = (p_t * (dp_t - delta)).astype(BF16)
            keys = pl.ds(pl.multiple_of(kb * t, t), t)
            dv_ref[:, keys] += lax.dot_general(dob, p_t.astype(BF16), NT, preferred_element_type=F32)
            dkt_scr[:, keys] += lax.dot_general(q_t, ds_t, NT, preferred_element_type=F32)
            return acc + lax.dot_general(kt, ds_t, TN, preferred_element_type=F32)

        acc = _loop_blocks(0, qi, lambda kb, c: step(kb, c, False), jnp.zeros((LANES, t), F32))
        dq_ref[...] = step(qi, acc, True).T

        @pl.when(qi == nq - 1)
        def _():
            dk_ref[...] = dkt_scr[...].T

    tile = pl.BlockSpec((t, LANES), lambda h, qi: (qi, h))
    tile_t = pl.BlockSpec((LANES, t), lambda h, qi: (h, qi))
    stat = pl.BlockSpec((1, 1, t), lambda h, qi: (h, 0, qi))
    seq = pl.BlockSpec((S, LANES), lambda h, qi: (0, h))
    seq_t = pl.BlockSpec((LANES, S), lambda h, qi: (h, 0))
    return _call_with_step(
        body, hosted, lambda: _attn_flags(nq), (q, k, v, o_t, do_t, lse), name=name, grid=(MLA_HEADS, nq),
        in_specs=[tile, seq, seq, tile_t, tile_t, stat],
        out_specs=[tile, seq, seq_t],
        out_shape=[jax.ShapeDtypeStruct((S, MLA_HEADS * LANES), F32), jax.ShapeDtypeStruct((S, MLA_HEADS * LANES), F32),
                   jax.ShapeDtypeStruct((MLA_HEADS * LANES, S), F32)],
        sem=("parallel", "arbitrary"), scratch_shapes=[pltpu.VMEM((LANES, S), F32)])


def _fn_ln(ctx, x, g, b):
    xhat, _ = _ln_stats(x)
    y = xhat * g + b
    return y, y


def _fn_conv_fwd(ctx, u, up, dtr, w8, cb, dtb):
    first = ctx.i == 0
    y = u * w8[3:4] + cb
    for s in (1, 2, 3):
        y = y + _shift_down(u, up, s, first) * w8[3 - s:4 - s]
    act = y * _sigmoid(y)
    v = dtr + dtb
    e = jnp.exp(-jnp.abs(v))
    one_p = 1.0 + e
    log1p = jnp.where(one_p == 1.0, e, jnp.log(one_p) * e / (one_p - 1.0))
    return y, act, jnp.maximum(v, 0.0) + log1p


def _fn_ssd_post(ctx, y, xs, z, dexp, g):
    yg = (y + xs * dexp) * (z * _sigmoid(z))
    outs = []
    for k in range(2):
        v = yg[:, 256 * k:256 * (k + 1)]
        outs.append(v * lax.rsqrt(_mean1(v * v) + RMS_EPS))
    return (jnp.concatenate(outs, axis=1) * g,)


def _fn_ssd_post_bwd(ctx, dyn, y, xs, z, dexp, g):
    yt = y + xs * dexp
    sig = _sigmoid(z)
    sz = z * sig
    yg = yt * sz
    dyh = dyn * g
    yh, dyg = [], []
    for k in range(2):
        sl = slice(256 * k, 256 * (k + 1))
        v = yg[:, sl]
        rs = lax.rsqrt(_mean1(v * v) + RMS_EPS)
        vh = v * rs
        yh.append(vh)
        dyg.append(rs * (dyh[:, sl] - vh * _mean1(dyh[:, sl] * vh)))
    yh = jnp.concatenate(yh, axis=1)
    dyg = jnp.concatenate(dyg, axis=1)
    dyt = dyg * sz
    dz = dyg * yt * (sig * (1.0 + z * (1.0 - sig)))
    return dyt, dz, dyt * dexp, _sum0(dyt * xs), _sum0(dyn * yh)


def _fn_mla_pre(ctx, ql, kvl, gq, gkv):
    return _rms_fwd(ql, gq), _rms_fwd(kvl, gkv)


def _fn_mla_pre_bwd(ctx, ql, kvl, dqn, dkvn_k, dkvn_v, gq, gkv):
    dql, dgq = _rms_bwd(ql, dqn, gq)
    dkvl, dgkv = _rms_bwd(kvl, dkvn_k + dkvn_v, gkv)
    return dql, dkvl, dgq, dgkv


def _fn_rope(ctx, qp, kn, kr, ta, tb, tc):
    kpe = _rope(kr, ta, tb, tc)
    qs, ks = [], []
    for h in range(MLA_HEADS):
        sl = slice(128 * h, 128 * (h + 1))
        qs.append(_rope(qp[:, sl], ta, tb, tc) * MLA_SCALE)
        ks.append(kn[:, sl] + kpe)
    return jnp.concatenate(qs, axis=1), jnp.concatenate(ks, axis=1)


def _fn_rope_bwd(ctx, dq, dk, ta, tb, tc):
    qs = []
    ksum = jnp.zeros_like(ta)
    for h in range(MLA_HEADS):
        sl = slice(128 * h, 128 * (h + 1))
        qs.append(_rope_bwd(dq[:, sl] * MLA_SCALE, ta, tb, tc))
        ksum = ksum + dk[:, sl]
    lane = _lane(ksum.shape)
    dkr = jnp.where((lane >= 64) & (lane < 96), _rope_bwd(ksum, ta, tb, tc), 0.0)
    return jnp.concatenate(qs, axis=1), dkr


def _mem_probs(qh, kh):
    s = _dot(qh, kh, NT) * MEM_SCALE
    p = jnp.exp(s - jnp.max(s, axis=1, keepdims=True))
    return p / jnp.sum(p, axis=1, keepdims=True)


def _fn_mem_fwd(ctx, q, km, vm):
    outs = []
    for h in range(MEM_HEADS):
        sl = slice(256 * h, 256 * (h + 1))
        outs.append(_dot(_mem_probs(q[:, sl], km[:, sl]), vm[:, sl]))
    return (jnp.concatenate(outs, axis=1),)


def _fn_mem_bwd(ctx, q, do, km, vm):
    dqs, dks, dvs = [], [], []
    for h in range(MEM_HEADS):
        sl = slice(256 * h, 256 * (h + 1))
        p = _mem_probs(q[:, sl], km[:, sl])
        dvs.append(_dot(p, do[:, sl], TN))
        dp = _dot(do[:, sl], vm[:, sl], NT)
        ds = p * (dp - jnp.sum(dp * p, axis=1, keepdims=True)) * MEM_SCALE
        dqs.append(_dot(ds, km[:, sl]))
        dks.append(_dot(ds, q[:, sl], TN))
    return jnp.concatenate(dqs, axis=1), jnp.concatenate(dks, axis=1), jnp.concatenate(dvs, axis=1)


def _fn_res_ln(ctx, h, r, g, b):
    xhat, _ = _ln_stats(ALPHA * h + r)
    y = xhat * g + b
    return y, y


def _fn_res_ln_bwd(ctx, h, r, d1, d2, g):
    xhat, rstd = _ln_stats(ALPHA * h + r)
    return _ln_bwd(xhat, rstd, ALPHA * d1 + d2, g)


def _fn_res2_ln(ctx, h, r1, r2, g, b):
    xhat, _ = _ln_stats(ALPHA * h + (r1 + r2))
    return (xhat * g + b,)


def _fn_res2_ln_bwd(ctx, h, r1, r2, d1, d2, g):
    xhat, rstd = _ln_stats(ALPHA * h + (r1 + r2))
    return _ln_bwd(xhat, rstd, ALPHA * d1 + d2, g)


def _fn_in_ln_bwd(ctx, x, d1, d2, g):
    xhat, rstd = _ln_stats(x)
    return _ln_bwd(xhat, rstd, ALPHA * d1 + d2, g)


def _fn_final(ctx, h2, ff, tgt, g, b):
    xhat, rstd = _ln_stats(ALPHA * h2 + ff)
    e = xhat * g + b - tgt
    loss = 0.5 * _sum0(jnp.sum(e * e, axis=1, keepdims=True)) / D_MODEL
    dx, dg, db = _ln_bwd(xhat, rstd, e / D_MODEL, g)
    return dx, dx, dg, db, loss


def _epi_du(da, u):
    return da * 2.0 * jnp.maximum(u.astype(F32), 0.0)


def _relu2(u):
    r = jnp.maximum(u.astype(F32), 0.0)
    return r * r


def _fn_conv_bwd_a(ctx, y, dxs1, dxs2, dbc, dtr, ddt, dtb):
    sig = _sigmoid(y)
    dact = jnp.concatenate([dxs1 + dxs2, dbc], axis=1)
    dyc = dact * (sig * (1.0 + y * (1.0 - sig)))
    ddtr = ddt * _sigmoid(dtr + dtb)
    return dyc, ddtr, _sum0(dyc), _sum0(ddtr)


def _fn_conv_bwd_b(ctx, d, dn, u, up, w8):
    first, last = ctx.i == 0, ctx.i == ctx.n - 1
    du = d * w8[3:4]
    row = lax.broadcasted_iota(jnp.int32, w8.shape, 0)
    dw = jnp.where(row == 3, _sum0(d * u), 0.0)
    for s in (1, 2, 3):
        du = du + _shift_up(d, dn, s, last) * w8[3 - s:4 - s]
        dw = dw + jnp.where(row == 3 - s, _sum0(d * _shift_down(u, up, s, first)), 0.0)
    return du, dw


def _fn_adam(ctx, w, g, m, v):
    m = ADAM_B1 * m + (1.0 - ADAM_B1) * g
    v = ADAM_B2 * v + (1.0 - ADAM_B2) * (g * g)
    m_hat = m / (1.0 - ADAM_B1 ** ADAM_STEP)
    v_hat = v / (1.0 - ADAM_B2 ** ADAM_STEP)
    return -ADAM_LR * (m_hat / (jnp.sqrt(v_hat) + ADAM_EPS) + ADAM_WD * w), m, v


def _z(r, c, dt):
    return jnp.zeros((r, c), dt)


W_IN_SHARD = 554
W_IN_GROUPS = [(0, 512, 1024), (512, 1536, 0), (1536, 1544, 1920), (1544, 1928, 1536), (1928, 2184, 2048),
               (2184, 2216, 2368)]


def _pad_w_in(ws):
    r, dt = ws.shape[1], ws.dtype

    def cols(a, b):
        out = []
        for k in range(N_SHARD):
            lo, hi = max(a, k * W_IN_SHARD), min(b, (k + 1) * W_IN_SHARD)
            if lo < hi:
                out.append(ws[k][:, lo - k * W_IN_SHARD:hi - k * W_IN_SHARD])
        return out

    return jnp.concatenate(cols(512, 1536) + cols(0, 512) + cols(1544, 1928) + cols(1536, 1544) + [_z(r, 120, dt)]
                           + cols(1928, 2184) + [_z(r, 64, dt)] + cols(2184, 2216) + [_z(r, 32, dt), _z(r, 128, dt)],
                           axis=1)


def _unpad_w_in(d):
    shards = []
    for k in range(N_SHARD):
        a, b = k * W_IN_SHARD, (k + 1) * W_IN_SHARD
        parts = []
        for o0, o1, p0 in W_IN_GROUPS:
            lo, hi = max(a, o0), min(b, o1)
            if lo < hi:
                parts.append(d[:, p0 + lo - o0:p0 + hi - o0])
        shards.append(jnp.concatenate(parts, axis=1))
    return jnp.stack(shards)


def _pad_heads(w, width):
    r = w.shape[0]
    w3 = w.reshape(r, MLA_HEADS, width)
    return jnp.pad(w3, ((0, 0), (0, 0), (0, 128 - width))).reshape(r, MLA_HEADS * 128)


def _row(v, width=None):
    v = v.reshape(1, -1).astype(F32)
    if width is not None and v.shape[1] < width:
        v = jnp.pad(v, ((0, 0), (0, width - v.shape[1])))
    return v


BIG = {
    "w_in": (1024, 2216, 1), "w_q_up": (384, 768, 1), "w_kv_up": (256, 1024, 1), "w_mix_out": (1024, 1024, 0),
    "w_mem_q": (1024, 1024, 0), "w_mem_k": (1024, 1024, 0), "w_mem_v": (1024, 1024, 0), "w_mem_o": (1024, 1024, 0),
    "w_up": (1024, 4096, 1), "w_down": (4096, 1024, 0), "conv_w": (4, 1024, 1),
}
BIG_ORDER = list(BIG)
SMALL_ORDER = ["ln_in_g", "ln_in_b", "conv_b", "dt_bias", "a_log", "d_skip", "ssd_norm_g", "q_norm_g", "kv_norm_g",
               "ln1_g", "ln1_b", "ln2_g", "ln2_b", "ln3_g", "ln3_b"]
N_SHARD = 4
N_DEV = 8
PACK_COLS = 1024
PACK_A_ROW = {"w_down": 0, "w_up": 1024, "w_mem_q": 2048, "w_mem_k": 2304, "w_mem_v": 2560, "w_mem_o": 2816,
              "w_mix_out": 3072}
PACK_A_ORDER = list(PACK_A_ROW)
PACK_A_ROWS = 3328
PACK_B_ORDER = ["w_q_up", "w_kv_up", "conv_w"]
PACK_B_ROWS = 160


def _shard_shape(name):
    r, c, ax = BIG[name]
    return (r // N_SHARD, c) if ax == 0 else (r, c // N_SHARD)


def _split_shards(name, full):
    r, c, ax = BIG[name]
    if ax == 0:
        return full.reshape(N_SHARD, -1)
    return full.reshape(r, N_SHARD, c // N_SHARD).transpose(1, 0, 2).reshape(N_SHARD, -1)


def _join_shards(name, parts):
    r, c, ax = BIG[name]
    if ax == 0:
        return parts.reshape(r, c)
    return parts.reshape(N_SHARD, r, c // N_SHARD).transpose(1, 0, 2).reshape(r, c)


def _small_all_reduce(g, step=None):
    r, cdim = g.shape
    si, so = (len(step.inputs), len(step.out_shapes)) if step else (0, 0)

    def body(g_ref, *refs):
        s_ins, out_ref, s_outs = refs[:si], refs[si], refs[si + 1:si + 1 + so]
        buf, send_sems, recv_sems = refs[si + 1 + so:si + 4 + so]
        s_sems = refs[si + 4 + so:]
        if step:
            step.start(s_ins, s_outs, s_sems)
        x, y, c, _ = _place()
        me = 4 * x + 2 * y + c
        buf[me] = g_ref[...]
        copies = []
        for d in range(1, N_DEV):
            to = me ^ d
            cp = pltpu.make_async_remote_copy(src_ref=g_ref, dst_ref=buf.at[me], send_sem=send_sems.at[d - 1],
                                              recv_sem=recv_sems.at[d - 1],
                                              device_id=(to // 4, (to // 2) % 2, to % 2), device_id_type=MESH)
            cp.start()
            copies.append(cp)
        for cp in copies:
            cp.wait()
        acc = buf[0]
        for d in range(1, N_DEV):
            acc = acc + buf[d]
        out_ref[...] = acc
        if step:
            step.finish(s_ins, s_outs, s_sems)

    res = pl.pallas_call(
        body, name="small_all_reduce",
        in_specs=[pl.BlockSpec(memory_space=pltpu.VMEM)] + [HBM] * si,
        out_specs=[pl.BlockSpec(memory_space=pltpu.VMEM)] + [HBM] * so,
        out_shape=[jax.ShapeDtypeStruct((r, cdim), F32)] + (list(step.out_shapes) if step else []),
        scratch_shapes=[pltpu.VMEM((N_DEV, r, cdim), F32), pltpu.SemaphoreType.DMA((N_DEV - 1,)),
                        pltpu.SemaphoreType.DMA((N_DEV - 1,))] + (_sem_scratch(step) if step else []),
    )(g, *(step.inputs if step else []))
    return res if step else res[0]


def _half_tile(h):
    return next(t for t in range(512, 0, -16) if h % t == 0)


def _pair_sum(gp, theirs, name):
    n, R, C = gp.shape
    H = R // 2
    tr = _half_tile(H)
    nb = H // tr

    def body(s_ref, g_ref, t_ref, o_ref):
        o_ref[...] = (g_ref[...] + t_ref[...]).astype(o_ref.dtype)

    def shard(k, s):
        return k + (k >= s[1]).astype(jnp.int32)

    me, c = _mesh_pos()
    return pl.pallas_call(
        body, name=name,
        grid_spec=pltpu.PrefetchScalarGridSpec(
            num_scalar_prefetch=1, grid=(n - 1, nb),
            in_specs=[pl.BlockSpec((1, tr, C), lambda k, i, s: (shard(k, s), s[0] * nb + i, 0)),
                      pl.BlockSpec((1, tr, C), lambda k, i, s: (shard(k, s), i, 0))],
            out_specs=pl.BlockSpec((1, tr, C), lambda k, i, s: (shard(k, s), i, 0))),
        out_shape=jax.ShapeDtypeStruct((n, H, C), BF16), compiler_params=_params(("arbitrary", "arbitrary")),
    )(jnp.stack([c, me]).astype(jnp.int32), gp, theirs)


def _chip_sum(gp, theirs, got, name):
    n, R, C = gp.shape
    H = R // 2
    tr = _half_tile(H)
    nb = H // tr

    def body(s_ref, g_ref, t_ref, r_ref, o_ref):
        acc = g_ref[0] + t_ref[0]
        for j in range(3):
            acc = acc + r_ref[j].astype(F32)
        o_ref[...] = acc

    me, c = _mesh_pos()
    return pl.pallas_call(
        body, name=name,
        grid_spec=pltpu.PrefetchScalarGridSpec(
            num_scalar_prefetch=1, grid=(nb,),
            in_specs=[pl.BlockSpec((1, tr, C), lambda i, s: (s[0], s[1] * nb + i, 0)),
                      pl.BlockSpec((1, tr, C), lambda i, s: (s[0], i, 0)),
                      pl.BlockSpec((3, tr, C), lambda i, s: (0, i, 0))],
            out_specs=pl.BlockSpec((tr, C), lambda i, s: (s[1] * nb + i, 0))),
        out_shape=jax.ShapeDtypeStruct((R, C), F32), compiler_params=_params(("arbitrary",)),
    )(jnp.stack([me, c]).astype(jnp.int32), gp, theirs, got)


def _unpack_group_b(g_c, g_b, own):
    me, _ = _mesh_pos()
    g_c = lax.dynamic_update_slice(g_c, own[0][None], (me, 0, 0))
    g_b = lax.dynamic_update_slice(g_b, own[1][None], (me, 0, 0)).reshape(N_SHARD, -1)
    WB, off = {"w_in": g_c}, 0
    for n in PACK_B_ORDER:
        sr, sc = _shard_shape(n)
        cnt = sr * sc
        if n == "conv_w":
            part = lax.bitcast_convert_type(g_b[:, off:off + 2 * cnt].reshape(N_SHARD, cnt, 2), F32)
            off += 2 * cnt
        else:
            part = g_b[:, off:off + cnt]
            off += cnt
        WB[n] = _join_shards(n, part)
    return WB


def _group_b_grads(dw_in_p, dw_q_p, dw_k_p, dw_v_pt, dconv_w8):
    return {
        "w_in": _unpad_w_in(dw_in_p),
        "w_q_up": dw_q_p.reshape(384, MLA_HEADS, 128)[:, :, :MLA_QK].reshape(384, MLA_HEADS * MLA_QK),
        "w_kv_up": jnp.concatenate([dw_k_p.reshape(MLA_KV_RANK, MLA_HEADS, 128)[:, :, :64],
                                    dw_v_pt.T.reshape(MLA_KV_RANK, MLA_HEADS, 128)[:, :, :64]], axis=2).reshape(
                                        MLA_KV_RANK, MLA_HEADS * 128),
        "conv_w": dconv_w8[0:4],
    }


def _pack_group_b(big_b):
    gflat = [_split_shards(n, big_b[n]) for n in PACK_B_ORDER]
    used = sum(f.shape[1] for f in gflat)
    gflat.append(jnp.zeros((N_SHARD, PACK_B_ROWS * PACK_COLS - used), F32))
    return big_b["w_in"], jnp.concatenate(gflat, axis=1).reshape(N_SHARD, PACK_B_ROWS, PACK_COLS)


def _local_step(x, mem, positions, target, WB, P, *, wp_a=None, g_a=None, wp_b=None):
    S = x.shape[0]
    tr = ROW_TILE
    dist = g_a is None
    g_in, b_in = _row(P["ln_in_g"]), _row(P["ln_in_b"])
    res = _rowwise(_fn_ln, [x], [g_in, b_in], [D_MODEL, (D_MODEL, BF16)], tr=tr, name="ln_in",
                   hosted=_merge_steps([_gather_step(w) for w in wp_b]) if dist else None)
    h0, h0_b = res[0], res[1]
    if dist:
        WB = _unpack_group_b(res[2], res[3], wp_b)
    P = {**P, "conv_w": WB["conv_w"]}
    w_in_p = _pad_w_in(WB["w_in"])
    w_q_p = _pad_heads(WB["w_q_up"], MLA_QK)
    w_kv3 = WB["w_kv_up"].reshape(MLA_KV_RANK, MLA_HEADS, 128)
    w_k_p = _pad_heads(w_kv3[:, :, :64].reshape(MLA_KV_RANK, 512), 64)
    w_v_p = _pad_heads(w_kv3[:, :, 64:].reshape(MLA_KV_RANK, 512), 64)
    w_v_pt = w_v_p.T
    conv_w8 = jnp.pad(P["conv_w"].astype(F32), ((0, 4), (0, 0)))
    conv_b = _row(P["conv_b"])
    dt_b = _row(P["dt_bias"], 128)
    a_head = -jnp.exp(P["a_log"].reshape(-1).astype(F32))
    a_row = _row(a_head, 128)
    dexp = jnp.repeat(P["d_skip"].reshape(-1).astype(F32), 64).reshape(1, 512)
    g_ssd, g_q, g_kv = _row(P["ssd_norm_g"]), _row(P["q_norm_g"]), _row(P["kv_norm_g"])
    g1, b1, g2, b2, g3, b3 = (_row(P[k]) for k in ("ln1_g", "ln1_b", "ln2_g", "ln2_b", "ln3_g", "ln3_b"))

    half = MLA_ROPE // 2
    inv_freq = jnp.power(ROPE_THETA, -jnp.arange(half, dtype=F32) / half)
    ang = positions.reshape(S, 1).astype(F32) * inv_freq
    cos, sin = jnp.cos(ang), jnp.sin(ang)
    zc = lambda n: jnp.zeros((S, n), F32)
    rope_a = jnp.concatenate([jnp.ones((S, 64), F32), cos, cos, zc(32)], axis=1)
    rope_b = jnp.concatenate([zc(80), sin, zc(32)], axis=1)
    rope_c = jnp.concatenate([zc(64), -sin, zc(48)], axis=1)

    proj = _mm(h0_b, w_in_p, form="nn", tn=IN_W // 2, name="mm_in")
    conv_y, xbc, dt = _rowwise(
        _fn_conv_fwd, [(proj,) + SEG_XBC, ("prev", proj) + SEG_XBC, (proj,) + SEG_DT], [conv_w8, conv_b, dt_b],
        [1024, 1024, 128], tr=tr, name="conv_fwd")
    y_ssd, hs = _ssd_fwd(xbc, dt, a_row, name="ssd_fwd")
    (y_n,) = _rowwise(_fn_ssd_post, [y_ssd, (xbc, 0, 512), (proj,) + SEG_Z], [dexp, g_ssd], [(512, BF16)], tr=tr,
                      name="ssd_post")
    q_n, kv_n = _rowwise(_fn_mla_pre, [(proj,) + SEG_QLAT, (proj,) + SEG_KVLAT], [g_q, g_kv], [384, 256], tr=tr,
                         name="mla_pre")
    qp = _mm(q_n, w_q_p, form="nn", name="mm_q_up")
    kn = _mm(kv_n, w_k_p, form="nn", name="mm_k_up")
    v_nat = _mm(kv_n, w_v_p, form="nn", out_dtype=BF16, name="mm_v_up")
    v_t = _mm(w_v_pt, kv_n, form="nt", out_dtype=BF16, name="mm_v_up_t")
    q_rot, k_full = _rowwise(_fn_rope, [qp, kn, (proj,) + SEG_KR, rope_a, rope_b, rope_c], [],
                             [(1024, BF16), (1024, BF16)], tr=tr, name="rope")
    res = _attn_fwd(q_rot, k_full, v_t, name="attn_fwd", hosted=_gather_step(wp_a) if dist else None)
    o_t, lse = res[0], res[1]
    if dist:
        g_a = lax.dynamic_update_slice(res[2], wp_a[None], (_mesh_pos()[0], 0, 0))
    r_mix = PACK_A_ROW["w_mix_out"]
    w_mix_o = jnp.pad(g_a[2:4, r_mix:r_mix + 256].reshape(MLA_HEADS, 64, D_MODEL),
                      ((0, 0), (0, 64), (0, 0))).reshape(MLA_HEADS * 128, D_MODEL)
    mix_o = _mm(o_t, w_mix_o, form="tn", name="mm_mix_o")
    mix_y = _mm(y_n, g_a, form="nn", b_pack="w_mix_out", name="mm_mix_y")
    (h1,) = _rowwise(_fn_res2_ln, [h0, mix_o, mix_y], [g1, b1], [D_MODEL], tr=tr, name="ln1")
    qm = _mm(h1, g_a, form="nn", b_pack="w_mem_q", out_dtype=BF16, name="mm_mem_q")
    km = _mm(mem, g_a, form="nn", b_pack="w_mem_k", out_dtype=BF16, name="mm_mem_k")
    vm = _mm(mem, g_a, form="nn", b_pack="w_mem_v", out_dtype=BF16, name="mm_mem_v")
    (om,) = _rowwise(_fn_mem_fwd, [qm], [km, vm], [(D_MODEL, BF16)], tr=tr, name="mem_fwd")
    xa = _mm(om, g_a, form="nn", b_pack="w_mem_o", name="mm_mem_o")
    h2, h2_b = _rowwise(_fn_res_ln, [h1, xa], [g2, b2], [D_MODEL, (D_MODEL, BF16)], tr=tr, name="ln2")
    u = _mm(h2_b, g_a, form="nn", b_pack="w_up", out_dtype=BF16, name="mm_up")
    ff = _mm(u, g_a, form="nn", a_pro=_relu2, b_pack="w_down", name="mm_down")

    gp = lax.empty((N_SHARD, PACK_A_ROWS, PACK_COLS), F32)
    dt3, dt3_b, dg3, db3, loss = _rowwise(_fn_final, [h2, ff, target], [g3, b3], [D_MODEL, (D_MODEL, BF16)],
                                          [(1, D_MODEL), (1, D_MODEL), (1, 128)], tr=tr, name="ln3_loss")
    du = _mm(dt3_b, g_a, form="nt", b_pack="w_down", epi=(_epi_du, u), out_dtype=BF16, name="mm_down_dx")
    gp = _mm(u, dt3_b, form="tn", a_pro=_relu2, out_pack=("w_down", gp), name="mm_down_dw")
    gp = _mm(h2_b, du, form="tn", out_pack=("w_up", gp), name="mm_up_dw")
    dh2 = _mm(du, g_a, form="nt", b_pack="w_up", name="mm_up_dx")
    dt2, dg2, db2 = _rowwise(_fn_res_ln_bwd, [h1, xa, dt3, dh2], [g2], [D_MODEL], [(1, D_MODEL)] * 2, tr=tr,
                             name="ln2_bwd")
    dom = _mm(dt2, g_a, form="nt", b_pack="w_mem_o", out_dtype=BF16, name="mm_mem_o_dx")
    gp = _mm(om, dt2, form="tn", out_pack=("w_mem_o", gp), name="mm_mem_o_dw")
    dqm, dkm, dvm = _rowwise(_fn_mem_bwd, [qm, dom], [km, vm], [(D_MODEL, BF16)], [(256, D_MODEL)] * 2, tr=tr,
                             name="mem_bwd")
    gp = _mm(h1, dqm, form="tn", out_pack=("w_mem_q", gp), name="mm_mem_q_dw")
    gp = _mm(mem, dkm, form="tn", out_pack=("w_mem_k", gp), name="mm_mem_k_dw")
    gp = _mm(mem, dvm, form="tn", out_pack=("w_mem_v", gp), name="mm_mem_v_dw")
    dh1 = _mm(dqm, g_a, form="nt", b_pack="w_mem_q", name="mm_mem_q_dx")
    dt1, dg1, db1 = _rowwise(_fn_res2_ln_bwd, [h0, mix_o, mix_y, dt2, dh1], [g1], [D_MODEL], [(1, D_MODEL)] * 2,
                             tr=tr, name="ln1_bwd")
    do_t = _mm(w_mix_o, dt1, form="nt", name="mm_mix_o_dx")
    dy_n = _mm(dt1, g_a, form="nt", b_pack="w_mix_out", b_rows=512, name="mm_mix_y_dx")
    dw_mix_o = _mm(o_t, dt1, form="nn", name="mm_mix_o_dw")
    gp = _mm(y_n, dt1, form="tn", out_pack=("w_mix_out", gp), name="mm_mix_y_dw")
    gp = lax.dynamic_update_slice(
        gp, dw_mix_o.reshape(MLA_HEADS, 128, D_MODEL)[:, :64].reshape(2, 256, D_MODEL), (2, r_mix, 0))
    dy_ssd, dz, dxs_skip, ddexp, dg_ssd = _rowwise(
        _fn_ssd_post_bwd, [dy_n, y_ssd, (xbc, 0, 512), (proj,) + SEG_Z], [dexp, g_ssd],
        [512, (512, BF16), 512], [(1, 512)] * 2, tr=tr, name="ssd_post_bwd")
    res = _ssd_bwd(xbc, dt, a_row, hs, dy_ssd, name="ssd_bwd", hosted=_pair_exchange_step(gp) if dist else None)
    dxs, dbc, ddt, da_head = res[0], res[1], res[2], res[3]
    chip_step = None
    if dist:
        theirs_a = res[4]
        chip_step = _chip_exchange_step(_pair_sum(gp, theirs_a, "pair_sum_a"))
    res = _attn_bwd(q_rot, k_full, v_nat, o_t, do_t, lse, name="attn_bwd", hosted=chip_step)
    dq_rot, dk, dv_t = res[0], res[1], res[2]
    if dist:
        gp = _chip_sum(gp, theirs_a, res[3], "chip_sum_a")
    dqp, dkr = _rowwise(_fn_rope_bwd, [dq_rot, dk, rope_a, rope_b, rope_c], [], [(1024, BF16), (128, BF16)], tr=tr,
                        name="rope_bwd")
    dw_q_p = _mm(q_n, dqp, form="tn", name="mm_q_up_dw")
    dq_n = _mm(dqp, w_q_p, form="nt", name="mm_q_up_dx")
    dw_k_p = _mm(kv_n, dk, form="tn", name="mm_k_up_dw")
    dkv_n1 = _mm(dk, w_k_p, form="nt", name="mm_k_up_dx")
    dw_v_pt = _mm(dv_t, kv_n, form="nn", name="mm_v_up_dw")
    dkv_n2 = _mm(dv_t, w_v_pt, form="tn", name="mm_v_up_dx")
    dq_lat, dkv_lat, dg_q, dg_kv = _rowwise(
        _fn_mla_pre_bwd, [(proj,) + SEG_QLAT, (proj,) + SEG_KVLAT, dq_n, dkv_n1, dkv_n2], [g_q, g_kv],
        [(384, BF16), (256, BF16)], [(1, 384), (1, 256)], tr=tr, name="mla_pre_bwd")
    dyc, ddtr, dconv_b, ddt_b = _rowwise(
        _fn_conv_bwd_a, [conv_y, dxs, dxs_skip, dbc, (proj,) + SEG_DT, ddt], [dt_b], [1024, (128, BF16)],
        [(1, 1024), (1, 128)], tr=tr, name="conv_bwd_a")
    dxbc, dconv_w8 = _rowwise(
        _fn_conv_bwd_b, [dyc, ("next", dyc, 0, 1024), (proj,) + SEG_XBC, ("prev", proj) + SEG_XBC], [conv_w8],
        [(1024, BF16)], [(8, 1024)], tr=tr, name="conv_bwd_b")
    dproj = jnp.concatenate([dxbc, dz, dq_lat, ddtr, dkv_lat, dkr, jnp.zeros((S, 128), BF16)], axis=1)
    res = _mm(h0_b, dproj, form="tn", tn=IN_W // 2, name="mm_in_dw", hosted=_pair_fill_step(gp) if dist else None)
    dw_in_p, red_a = (res[0], res[1]) if dist else (res, None)
    big_b = _group_b_grads(dw_in_p, dw_q_p, dw_k_p, dw_v_pt, dconv_w8)
    q_b = None
    if dist:
        gp_c, gp_b = _pack_group_b(big_b)
        dh0, theirs_c, theirs_b = _mm(dproj, w_in_p, form="nt", tk=IN_W // 2, name="mm_in_dx", hosted=_merge_steps(
            [_pair_exchange_step(gp_c), _pair_exchange_step(gp_b)]))
        q_b = ((gp_c, theirs_c, _pair_sum(gp_c, theirs_c, "pair_sum_w_in")),
               (gp_b, theirs_b, _pair_sum(gp_b, theirs_b, "pair_sum_b")))
        gp = red_a
    else:
        dh0 = _mm(dproj, w_in_p, form="nt", tk=IN_W // 2, name="mm_in_dx")
    grad_x, dg_in, db_in = _rowwise(_fn_in_ln_bwd, [x, dt1, dh0], [g_in], [D_MODEL], [(1, D_MODEL)] * 2, tr=tr,
                                    name="ln_in_bwd")

    small = {
        "ln_in_g": dg_in, "ln_in_b": db_in, "conv_b": dconv_b, "dt_bias": ddt_b[:, :8],
        "a_log": da_head[:, :8] * a_head.reshape(1, 8),
        "d_skip": ddexp.reshape(8, 64).sum(axis=1).reshape(1, 8),
        "ssd_norm_g": dg_ssd, "q_norm_g": dg_q, "kv_norm_g": dg_kv,
        "ln1_g": dg1, "ln1_b": db1, "ln2_g": dg2, "ln2_b": db2, "ln3_g": dg3, "ln3_b": db3,
    }
    return loss[0, 0], grad_x, (gp, q_b), big_b, small


def _adam(w, g, m, v, name):
    shape = w.shape
    w2, m2, v2 = (t.reshape(-1, shape[-1]) for t in (w, m, v))
    g2 = (g[0], 0, shape[-1], g[1]) if isinstance(g, tuple) else g.reshape(-1, shape[-1])
    d, mn, vn = _rowwise(_fn_adam, [w2, g2, m2, v2], [], [shape[-1]] * 3, tr=ROW_TILE, name=name)
    return d.reshape(shape), mn.reshape(shape), vn.reshape(shape)


def kernel(x, mem, positions, ln_in_g, ln_in_b, w_in, conv_w, conv_b, dt_bias, a_log, d_skip, ssd_norm_g, q_norm_g, w_q_up, kv_norm_g, w_kv_up, w_mix_out, ln1_g, ln1_b, w_mem_q, w_mem_k, w_mem_v, w_mem_o, ln2_g, ln2_b, w_up, w_down, ln3_g, ln3_b, loss_target, m_ln_in_g, m_ln_in_b, m_w_in, m_conv_w, m_conv_b, m_dt_bias, m_a_log, m_d_skip, m_ssd_norm_g, m_q_norm_g, m_w_q_up, m_kv_norm_g, m_w_kv_up, m_w_mix_out, m_ln1_g, m_ln1_b, m_w_mem_q, m_w_mem_k, m_w_mem_v, m_w_mem_o, m_ln2_g, m_ln2_b, m_w_up, m_w_down, m_ln3_g, m_ln3_b, v_ln_in_g, v_ln_in_b, v_w_in, v_conv_w, v_conv_b, v_dt_bias, v_a_log, v_d_skip, v_ssd_norm_g, v_q_norm_g, v_w_q_up, v_kv_norm_g, v_w_kv_up, v_w_mix_out, v_ln1_g, v_ln1_b, v_w_mem_q, v_w_mem_k, v_w_mem_v, v_w_mem_o, v_ln2_g, v_ln2_b, v_w_up, v_w_down, v_ln3_g, v_ln3_b):
    args = dict(locals())

    wp_a = jnp.concatenate([args[n].reshape(-1, PACK_COLS).astype(BF16) for n in PACK_A_ORDER], axis=0)
    flat = [args[n].reshape(-1).astype(BF16) for n in PACK_B_ORDER[:-1]]
    flat.append(lax.bitcast_convert_type(conv_w.reshape(-1), BF16).reshape(-1))
    used = sum(f.shape[0] for f in flat)
    flat.append(jnp.zeros((PACK_B_ROWS * PACK_COLS - used,), BF16))
    wp_b = jnp.concatenate(flat).reshape(PACK_B_ROWS, PACK_COLS)
    wp_c = w_in[0].astype(BF16)

    P = {n: args[n] for n in SMALL_ORDER}
    loss, grad_x, (red_a, ((gp_c, theirs_c, pb_c), (gp_b, theirs_b, pb_b))), _, gsmall = _local_step(
        x[0], mem[0], positions[0], loss_target[0], None, P, wp_a=wp_a, wp_b=(wp_c, wp_b))

    gs = jnp.concatenate([_row(gsmall[n], PACK_COLS) for n in SMALL_ORDER] + [_row(loss, PACK_COLS)], axis=0)
    gs, got_c, got_b = _small_all_reduce(gs, _merge_steps([_chip_exchange_step(pb_c), _chip_exchange_step(pb_b)]))
    loss = gs[len(SMALL_ORDER), 0]
    red_c, red_b = _run_step(_merge_steps([_pair_fill_step(_chip_sum(gp_c, theirs_c, got_c, "chip_sum_w_in")),
                                           _pair_fill_step(_chip_sum(gp_b, theirs_b, got_b, "chip_sum_b"))]),
                             "pair_fill_b")

    grads, deltas, new_m, new_v = {}, {}, {}, {}
    for n in PACK_A_ORDER:
        r0, (sr, _) = PACK_A_ROW[n], _shard_shape(n)
        grads[n] = red_a[r0:r0 + sr].reshape(args[n].shape)
        deltas[n], new_m[n], new_v[n] = _adam(args[n], (red_a, r0), args["m_" + n], args["v_" + n], "adam_" + n)
    grads["w_in"] = red_c.reshape(w_in.shape)
    deltas["w_in"], new_m["w_in"], new_v["w_in"] = _adam(w_in, grads["w_in"], m_w_in, v_w_in, "adam_w_in")
    red_b = red_b.reshape(-1)
    off = 0
    for n in PACK_B_ORDER:
        sr, sc = _shard_shape(n)
        grads[n] = red_b[off:off + sr * sc].reshape(args[n].shape)
        off += sr * sc
        deltas[n], new_m[n], new_v[n] = _adam(args[n], grads[n], args["m_" + n], args["v_" + n], "adam_" + n)
    pack = lambda pre: jnp.concatenate([_row(args[pre + n], PACK_COLS) for n in SMALL_ORDER]
                                       + [jnp.zeros((1, PACK_COLS), F32)], axis=0)
    ds, ms, vs = _rowwise(_fn_adam, [pack(""), gs, pack("m_"), pack("v_")], [], [PACK_COLS] * 3, tr=16,
                          name="adam_small")
    for i, n in enumerate(SMALL_ORDER):
        cnt = args[n].size
        take = lambda t: t[i, :cnt].reshape(args[n].shape)
        grads[n], deltas[n], new_m[n], new_v[n] = take(gs), take(ds), take(ms), take(vs)

    order = ["ln_in_g", "ln_in_b", "w_in", "conv_w", "conv_b", "dt_bias", "a_log", "d_skip", "ssd_norm_g",
             "q_norm_g", "w_q_up", "kv_norm_g", "w_kv_up", "w_mix_out", "ln1_g", "ln1_b", "w_mem_q", "w_mem_k",
             "w_mem_v", "w_mem_o", "ln2_g", "ln2_b", "w_up", "w_down", "ln3_g", "ln3_b"]
    return (loss, grad_x[None], *[grads[n] for n in order], *[deltas[n] for n in order],
            *[new_m[n] for n in order], *[new_v[n] for n in order])
```

```python
import functools

import jax
import jax.numpy as jnp
from jax import lax
from jax.experimental import pallas as pl
from jax.experimental.pallas import tpu as pltpu

F32 = jnp.float32
BF16 = jnp.bfloat16
MESH = pl.DeviceIdType.MESH

D_MODEL = 1024
SSD_CHUNK = 128
SSD_STATE = 128
MLA_HEADS = 8
MLA_ROPE = 32
MLA_QK = 96
MLA_KV_RANK = 256
ROPE_THETA = 10000.0
MEM_HEADS = 4
MEM_HEAD_DIM = 256
LN_EPS = 1e-5
RMS_EPS = 1e-6
ALPHA = 2.0 ** 0.25
ADAM_LR = 0.001
ADAM_B1 = 0.9
ADAM_B2 = 0.999
ADAM_EPS = 1e-08
ADAM_WD = 0.01
ADAM_STEP = 10

LANES = 128
IN_W = 2560
SEG_XBC = (0, 1024)
SEG_Z = (1024, 512)
SEG_QLAT = (1536, 384)
SEG_DT = (1920, 128)
SEG_KVLAT = (2048, 256)
SEG_KR = (2304, 128)
VMEM_LIMIT = 56 * 1024 * 1024
ATTN_TILE = 512
ROW_TILE = 512
SSD_PER_STEP = 2
NEG = -1e30
MLA_SCALE = MLA_QK ** -0.5
MEM_SCALE = MEM_HEAD_DIM ** -0.5

NN = (((1,), (0,)), ((), ()))
NT = (((1,), (1,)), ((), ()))
TN = (((0,), (0,)), ((), ()))


def _dot(a, b, dims=NN):
    return lax.dot_general(a.astype(BF16), b.astype(BF16), dims, preferred_element_type=F32)


def _dot_exact(a, b):
    return lax.dot_general(a, b, NN, precision=lax.Precision.HIGHEST, preferred_element_type=F32)


def _pick(dim, pref):
    t = min(pref, dim)
    t -= t % LANES
    while t >= LANES:
        if dim % t == 0:
            return t
        t -= LANES
    return dim


def _params(sem):
    return pltpu.CompilerParams(dimension_semantics=sem, vmem_limit_bytes=VMEM_LIMIT)


def _pack_caps(wname):
    r, c, ax = BIG[wname]
    if ax == 0:
        return (r if r <= 1024 else r // N_SHARD), c
    return r, c // N_SHARD


def _pack_block(wname, br, bc):
    r, c, ax = BIG[wname]
    r0 = PACK_A_ROW[wname]
    sr = r // N_SHARD if ax == 0 else r
    if ax == 0 and br > sr:
        assert br % sr == 0 and r0 % sr == 0
        return (br // sr, sr, bc), lambda rb, cb: (rb, r0 // sr, cb)
    assert r0 % br == 0
    if ax == 0:
        per = sr // br
        return (1, br, bc), lambda rb, cb: (rb // per, r0 // br + rb % per, cb)
    per = (c // N_SHARD) // bc
    return (1, br, bc), lambda rb, cb: (cb // per, r0 // br + rb, cb % per)


def _mm(a, b, *, form, name, a_pro=None, epi=None, out_dtype=F32, tm=1024, tn=1024, tk=1024, b_pack=None,
        b_rows=None, out_pack=None, hosted=None):
    b_shape = BIG[b_pack][:2] if b_pack else b.shape
    if b_pack and form == "nt":
        b_shape = (b_rows or b_shape[0], b_shape[1])
    if form == "nn":
        (m, k), (_, n) = a.shape, b_shape
    elif form == "nt":
        (m, k), (n, _) = a.shape, b_shape
    else:
        (k, m), (_, n) = a.shape, b_shape
    if b_pack:
        rcap, ccap = _pack_caps(b_pack)
        tk, tn = (min(tk, rcap), min(tn, ccap)) if form == "nn" else (min(tk, ccap), min(tn, rcap))
    if out_pack:
        rcap, ccap = _pack_caps(out_pack[0])
        tm, tn = min(tm, rcap), min(tn, ccap)
    tm, tn, tk = _pick(m, tm), _pick(n, tn), _pick(k, tk)
    dims = {"nn": NN, "nt": NT, "tn": TN}[form]
    nk = k // tk
    direct = out_dtype == F32 and epi is None
    n_extra = (1 if epi else 0) + (1 if out_pack else 0)

    def body(a_ref, b_ref, *rest):
        o_ref = rest[n_extra]
        acc_ref = o_ref if direct else rest[-1]

        @pl.when(pl.program_id(2) == 0)
        def _():
            acc_ref[...] = jnp.zeros_like(acc_ref)

        av = a_ref[...]
        if a_pro is not None:
            av = a_pro(av)
        bv = b_ref[...]
        acc_ref[...] += _dot(av, bv.reshape(-1, bv.shape[-1]), dims).reshape(acc_ref.shape)
        if not direct:
            @pl.when(pl.program_id(2) == nk - 1)
            def _():
                val = acc_ref[...]
                if epi is not None:
                    val = epi[0](val, rest[0][...])
                o_ref[...] = val.reshape(o_ref.shape).astype(o_ref.dtype)

    if form == "tn":
        a_spec = pl.BlockSpec((tk, tm), lambda i, j, kk: (kk, i))
    else:
        a_spec = pl.BlockSpec((tm, tk), lambda i, j, kk: (i, kk))
    if b_pack:
        shape, idx = _pack_block(b_pack, *((tk, tn) if form == "nn" else (tn, tk)))
        b_spec = pl.BlockSpec(shape, (lambda i, j, kk: idx(kk, j)) if form == "nn" else (lambda i, j, kk: idx(j, kk)))
    elif form == "nt":
        b_spec = pl.BlockSpec((tn, tk), lambda i, j, kk: (j, kk))
    else:
        b_spec = pl.BlockSpec((tk, tn), lambda i, j, kk: (kk, j))
    in_specs, args = [a_spec, b_spec], [a, b]
    out_spec = pl.BlockSpec((tm, tn), lambda i, j, kk: (i, j))
    out_sds, aliases = jax.ShapeDtypeStruct((m, n), out_dtype), {}
    if epi is not None:
        in_specs.append(out_spec)
        args.append(epi[1])
    if out_pack:
        wname, buf = out_pack
        shape, idx = _pack_block(wname, tm, tn)
        out_spec = pl.BlockSpec(shape, lambda i, j, kk: idx(i, j))
        out_sds, aliases = jax.ShapeDtypeStruct(buf.shape, buf.dtype), {len(args): 0}
        in_specs.append(HBM)
        args.append(buf)
    acc_shape = out_spec.block_shape if out_pack else (tm, tn)
    res = _call_with_step(
        body, hosted, None, args, name=name, grid=(m // tm, n // tn, nk), in_specs=in_specs, out_specs=[out_spec],
        out_shape=[out_sds], sem=("parallel", "parallel", "arbitrary"), aliases=aliases,
        scratch_shapes=[] if direct else [pltpu.VMEM(acc_shape, F32)])
    return res[0] if hosted is None else res


class _Ctx:
    def __init__(self, i, n):
        self.i, self.n = i, n


def _rowwise(fn, rows, consts, row_outs, acc_outs=(), *, tr, name, n_rows=None, hosted=None):
    norm = []
    for r in rows:
        kind = "tile"
        if isinstance(r, tuple) and isinstance(r[0], str):
            kind, r = r[0], r[1:]
        if kind == "cols":
            norm.append((kind, r[0], 0, r[0].shape[0], 0))
            continue
        row0 = 0
        if isinstance(r, tuple) and len(r) == 4:
            r, row0 = r[:3], r[3]
        arr, col0, width = r if isinstance(r, tuple) else (r, 0, r.shape[1])
        assert col0 % width == 0
        norm.append((kind, arr, col0 // width, width, row0))
    n_rows = n_rows or next(a.shape[0] for k, a, _, _, _ in norm if k == "tile")
    tr = min(tr, n_rows)
    while n_rows % tr:
        tr -= 8
    n = n_rows // tr
    arrs, specs = [], []
    for kind, arr, cb, width, row0 in norm:
        if kind == "tile":
            assert row0 % tr == 0
            specs.append(pl.BlockSpec((tr, width), lambda i, cb=cb, rb=row0 // tr: (i + rb, cb)))
        elif kind == "cols":
            specs.append(pl.BlockSpec((width, tr), lambda i: (0, i)))
        elif kind == "prev":
            specs.append(pl.BlockSpec((8, width), lambda i, cb=cb: (jnp.maximum(i * (tr // 8) - 1, 0), cb)))
        else:
            specs.append(pl.BlockSpec((8, width), lambda i, cb=cb: (jnp.minimum((i + 1) * (tr // 8), n_rows // 8 - 1), cb)))
        arrs.append(arr)
    for c in consts:
        specs.append(pl.BlockSpec(c.shape, lambda i, nd=c.ndim: (0,) * nd))
        arrs.append(c)
    n_in, n_ro = len(arrs), len(row_outs)
    out_shape, out_specs, aliases = [], [], {}
    for j, ro in enumerate(row_outs):
        w, dt, col0 = (ro + (None,))[:3] if isinstance(ro, tuple) else (ro, F32, None)
        if col0 is None:
            out_shape.append(jax.ShapeDtypeStruct((n_rows, w), dt))
            out_specs.append(pl.BlockSpec((tr, w), lambda i: (i, 0)))
        else:
            assert col0 % w == 0 and dt.shape[0] == n_rows
            out_shape.append(jax.ShapeDtypeStruct(dt.shape, dt.dtype))
            out_specs.append(pl.BlockSpec((tr, w), lambda i, cb=col0 // w: (i, cb)))
            aliases[len(arrs)] = j
            arrs.append(dt)
            specs.append(HBM)
    n_all = len(arrs)
    out_shape += [jax.ShapeDtypeStruct(s, F32) for s in acc_outs]
    out_specs += [pl.BlockSpec(s, lambda i: (0, 0)) for s in acc_outs]

    def body(*refs):
        i = pl.program_id(0)
        vals = [r[...] for r in refs[:n_in]]
        outs = fn(_Ctx(i, n), *vals)
        if not isinstance(outs, (tuple, list)):
            outs = (outs,)
        o_refs = refs[n_all:]
        for o_ref, o in zip(o_refs[:n_ro], outs[:n_ro]):
            o_ref[...] = o.astype(o_ref.dtype)
        if acc_outs:
            @pl.when(i == 0)
            def _():
                for o_ref in o_refs[n_ro:]:
                    o_ref[...] = jnp.zeros_like(o_ref)

            for o_ref, o in zip(o_refs[n_ro:], outs[n_ro:]):
                o_ref[...] += jnp.broadcast_to(o, o_ref.shape)

    return _call_with_step(body, hosted, None, arrs, name=name, grid=(n,), in_specs=specs, out_specs=out_specs,
                           out_shape=out_shape, sem=("arbitrary",), aliases=aliases)


def _sum0(v):
    return jnp.sum(v, axis=0, keepdims=True)


def _mean1(v):
    return jnp.mean(v, axis=-1, keepdims=True)


def _sigmoid(v):
    return 1.0 / (1.0 + jnp.exp(-v))


def _ln_stats(t):
    xc = t - _mean1(t)
    rstd = lax.rsqrt(_mean1(xc * xc) + LN_EPS)
    return xc * rstd, rstd


def _ln_bwd(xhat, rstd, dy, g):
    dxh = dy * g
    dx = rstd * (dxh - _mean1(dxh) - xhat * _mean1(dxh * xhat))
    return dx, _sum0(dy * xhat), _sum0(dy)


def _rms_fwd(v, g):
    return v * lax.rsqrt(_mean1(v * v) + RMS_EPS) * g


def _rms_bwd(v, dy, g):
    rs = lax.rsqrt(_mean1(v * v) + RMS_EPS)
    vh = v * rs
    dyg = dy * g
    return rs * (dyg - vh * _mean1(dyg * vh)), _sum0(dy * vh)


def _lane(shape):
    return lax.broadcasted_iota(jnp.int32, shape, len(shape) - 1)


def _shift_down(u, halo, s, is_first):
    tr = u.shape[0]
    rolled = pltpu.roll(u, s, 0)
    hr = jnp.where(is_first, 0.0, pltpu.roll(halo, s, 0))
    row = lax.broadcasted_iota(jnp.int32, hr.shape, 0)
    top = jnp.where(row < s, hr, rolled[0:8])
    if tr == 8:
        return top
    return jnp.concatenate([top, rolled[8:]], axis=0)


def _shift_up(d, halo, s, is_last):
    tr = d.shape[0]
    rolled = pltpu.roll(d, tr - s, 0)
    hr = jnp.where(is_last, 0.0, pltpu.roll(halo, 8 - s, 0))
    row = lax.broadcasted_iota(jnp.int32, hr.shape, 0)
    bot = jnp.where(row >= 8 - s, hr, rolled[tr - 8:])
    if tr == 8:
        return bot
    return jnp.concatenate([rolled[:tr - 8], bot], axis=0)


def _rope_tables(trig):
    t = jnp.concatenate([trig] * (LANES // trig.shape[0]), axis=0).T
    lane = _lane(t.shape)
    first, second = (lane >= 64) & (lane < 80), (lane >= 80) & (lane < 96)
    ta = jnp.where(lane < 64, 1.0, jnp.where(first, pltpu.roll(t, 64, 1), jnp.where(second, pltpu.roll(t, 80, 1), 0.0)))
    return ta, jnp.where(second, pltpu.roll(t, 64, 1), 0.0), jnp.where(first, -pltpu.roll(t, 48, 1), 0.0)


def _rope(v, ta, tb, tc):
    return v * ta + pltpu.roll(v, 16, 1) * tb + pltpu.roll(v, LANES - 16, 1) * tc


def _rope_bwd(d, ta, tb, tc):
    return d * ta + pltpu.roll(d * tb, LANES - 16, 1) + pltpu.roll(d * tc, 16, 1)


def _ssd_common(dtv, a_row):
    L = SSD_CHUNK
    a = dtv * a_row
    r = lax.broadcasted_iota(jnp.int32, (L, L), 0)
    c = lax.broadcasted_iota(jnp.int32, (L, L), 1)
    tril = r >= c
    cs = _dot_exact(tril.astype(F32), a)
    cs_t = cs.T
    cs_last = cs[L - 1:L, :]
    return dict(a=a, tril=tril, cs=cs, cs_t=cs_t, ecs=jnp.exp(cs), dte=jnp.exp(cs_last - cs),
                elast=jnp.exp(cs_last))


def _pair_sel(v, h0, lo):
    return jnp.where(lo, v[:, h0:h0 + 1], v[:, h0 + 1:h0 + 2])


def _ssd_pair(cm, h0, cb, xp, dtv, bmat, cmat, hp, lo):
    x = xp * _pair_sel(dtv, h0, lo)
    lam0 = jnp.exp(jnp.where(cm["tril"], cm["cs"][:, h0:h0 + 1] - cm["cs_t"][h0:h0 + 1, :], NEG))
    lam1 = jnp.exp(jnp.where(cm["tril"], cm["cs"][:, h0 + 1:h0 + 2] - cm["cs_t"][h0 + 1:h0 + 2, :], NEG))
    m0, m1 = cb * lam0, cb * lam1
    ydiag = jnp.where(lo, _dot(m0, x), _dot(m1, x))
    ecs_p = _pair_sel(cm["ecs"], h0, lo)
    dte_p = _pair_sel(cm["dte"], h0, lo)
    yoff = _dot(cmat, hp, NT) * ecs_p
    xd = x * dte_p
    st = _dot(xd, bmat, TN)
    rlo = lax.broadcasted_iota(jnp.int32, (LANES, SSD_STATE), 0) < 64
    decay = jnp.where(rlo, cm["elast"][:, h0:h0 + 1], cm["elast"][:, h0 + 1:h0 + 2])
    h_next = hp * decay + st
    return dict(x=x, lam0=lam0, lam1=lam1, m0=m0, m1=m1, y=ydiag + yoff, yoff=yoff, ecs_p=ecs_p, dte_p=dte_p,
                xd=xd, decay=decay, h_next=h_next)


def _ssd_fwd(xbc, dt, a_row, *, name):
    S = xbc.shape[0]
    L = SSD_CHUNK
    nc = S // L
    per = SSD_PER_STEP if nc % SSD_PER_STEP == 0 else 1
    G = per * L

    def body(xs_ref, bm_ref, cm_ref, dt_ref, a_ref, y_ref, hs_ref, h_scr):
        @pl.when(pl.program_id(0) == 0)
        def _():
            h_scr[...] = jnp.zeros_like(h_scr)

        lo = _lane((L, LANES)) < 64
        for sub in range(per):
            rows = slice(sub * L, (sub + 1) * L)
            dtv = dt_ref[rows, :]
            cm = _ssd_common(dtv, a_ref[...])
            ys = []
            for g in range(2):
                bmat = bm_ref[rows, g * 128:(g + 1) * 128]
                cmat = cm_ref[rows, g * 128:(g + 1) * 128]
                cb = _dot(cmat, bmat, NT)
                for pr in range(2):
                    p4 = 2 * g + pr
                    hp = h_scr[p4]
                    hs_ref[sub, p4 * 128:(p4 + 1) * 128, :] = hp
                    t = _ssd_pair(cm, 2 * p4, cb, xs_ref[rows, p4 * 128:(p4 + 1) * 128], dtv, bmat, cmat, hp, lo)
                    ys.append(t["y"])
                    h_scr[p4] = t["h_next"]
            y_ref[rows, :] = jnp.concatenate(ys, axis=1)

    return pl.pallas_call(
        body, name=name, grid=(nc // per,),
        in_specs=[pl.BlockSpec((G, 512), lambda c: (c, 0)), pl.BlockSpec((G, 256), lambda c: (c, 2)),
                  pl.BlockSpec((G, 256), lambda c: (c, 3)), pl.BlockSpec((G, 128), lambda c: (c, 0)),
                  pl.BlockSpec((1, 128), lambda c: (0, 0))],
        out_specs=[pl.BlockSpec((G, 512), lambda c: (c, 0)), pl.BlockSpec((per, 512, 128), lambda c: (c, 0, 0))],
        out_shape=[jax.ShapeDtypeStruct((S, 512), F32), jax.ShapeDtypeStruct((nc, 512, 128), F32)],
        scratch_shapes=[pltpu.VMEM((4, 128, 128), F32)],
        compiler_params=_params(("arbitrary",)),
    )(xbc, xbc, xbc, dt, a_row)


def _ssd_bwd(xbc, dt, a_row, hs, dy, *, name, hosted=None):
    S = xbc.shape[0]
    L = SSD_CHUNK
    nc = S // L
    per = SSD_PER_STEP if nc % SSD_PER_STEP == 0 else 1
    G = per * L

    def body(xs_ref, bm_ref, cm_ref, dt_ref, a_ref, hs_ref, dy_ref, dxs_ref, dbc_ref, ddt_ref, da_ref, g_scr):
        @pl.when(pl.program_id(0) == 0)
        def _():
            g_scr[...] = jnp.zeros_like(g_scr)
            da_ref[...] = jnp.zeros_like(da_ref)

        for sub in reversed(range(per)):
            rows = pl.ds(sub * L, L)
            chunk(xs_ref.at[rows, :], bm_ref.at[rows, :], cm_ref.at[rows, :], dt_ref.at[rows, :], a_ref,
                  hs_ref.at[pl.ds(sub, 1)], dy_ref.at[rows, :], dxs_ref.at[rows, :], dbc_ref.at[rows, :],
                  ddt_ref.at[rows, :], da_ref, g_scr)

    def chunk(xs_ref, bm_ref, cm_ref, dt_ref, a_ref, hs_ref, dy_ref, dxs_ref, dbc_ref, ddt_ref, da_ref, g_scr):
        dtv = dt_ref[...]
        a_row_v = a_ref[...]
        cm = _ssd_common(dtv, a_row_v)
        lo = _lane((L, LANES)) < 64
        lane_row = _lane((1, LANES))
        ri = lax.broadcasted_iota(jnp.int32, (L, L), 0)
        ci = lax.broadcasted_iota(jnp.int32, (L, L), 1)
        triu = (ri <= ci).astype(F32)
        stril = ri > ci

        def halves(v, mask):
            return (jnp.sum(jnp.where(mask, v, 0.0), axis=1, keepdims=True),
                    jnp.sum(jnp.where(mask, 0.0, v), axis=1, keepdims=True))

        i_all = jnp.zeros((L, LANES), F32)
        yo_all = jnp.zeros((L, LANES), F32)
        w_all = jnp.zeros((L, LANES), F32)
        ddt_x = jnp.zeros((L, LANES), F32)
        e_row = jnp.zeros((1, LANES), F32)
        rlo = lax.broadcasted_iota(jnp.int32, (LANES, SSD_STATE), 0) < 64
        dxs, dbs, dcs = [], [], []
        for g in range(2):
            bmat = bm_ref[:, g * 128:(g + 1) * 128]
            cmat = cm_ref[:, g * 128:(g + 1) * 128]
            cb = _dot(cmat, bmat, NT)
            dcb = jnp.zeros((L, L), F32)
            db = jnp.zeros((L, SSD_STATE), F32)
            dc = jnp.zeros((L, SSD_STATE), F32)
            for pr in range(2):
                p4 = 2 * g + pr
                h0 = 2 * p4
                hp = hs_ref[0, p4 * 128:(p4 + 1) * 128, :]
                xp = xs_ref[:, p4 * 128:(p4 + 1) * 128]
                t = _ssd_pair(cm, h0, cb, xp, dtv, bmat, cmat, hp, lo)
                gst = g_scr[p4]
                dyp = dy_ref[:, p4 * 128:(p4 + 1) * 128]
                dy0 = jnp.where(lo, dyp, 0.0)
                dy1 = dyp - dy0
                bg = _dot(bmat, gst, NT)
                dx = _dot(t["m0"], dy0, TN) + _dot(t["m1"], dy1, TN) + bg * t["dte_p"]
                dm0, dm1 = _dot(dy0, t["x"], NT), _dot(dy1, t["x"], NT)
                dcb = dcb + dm0 * t["lam0"] + dm1 * t["lam1"]
                dye = dyp * t["ecs_p"]
                dc = dc + _dot(dye, hp)
                db = db + _dot(t["xd"], gst)
                i0 = jnp.sum(jnp.where(stril, _dot(triu, dm0 * t["m0"]), 0.0), axis=1, keepdims=True)
                i1 = jnp.sum(jnp.where(stril, _dot(triu, dm1 * t["m1"]), 0.0), axis=1, keepdims=True)
                yo0, yo1 = halves(dyp * t["yoff"], lo)
                w0, w1 = halves(t["xd"] * bg, lo)
                gh = gst * (hp * t["decay"])
                e0 = _sum0(jnp.sum(jnp.where(rlo, gh, 0.0), axis=1, keepdims=True))
                e1 = _sum0(jnp.sum(jnp.where(rlo, 0.0, gh), axis=1, keepdims=True))
                x0, x1 = halves(dx * xp, lo)
                oh0 = (lane_row == h0).astype(F32)
                oh1 = (lane_row == h0 + 1).astype(F32)
                i_all = i_all + i0 * oh0 + i1 * oh1
                yo_all = yo_all + yo0 * oh0 + yo1 * oh1
                w_all = w_all + w0 * oh0 + w1 * oh1
                e_row = e_row + e0 * oh0 + e1 * oh1
                ddt_x = ddt_x + x0 * oh0 + x1 * oh1
                dxs.append(dx * _pair_sel(dtv, h0, lo))
                g_scr[p4] = gst * t["decay"] + _dot(dye, cmat, TN)
            dbs.append(db + _dot(dcb, cmat, TN))
            dcs.append(dc + _dot(dcb, bmat))
        da = i_all + _dot_exact(triu, yo_all) + _dot_exact(stril.astype(F32), w_all) + e_row
        ddt_ref[...] = da * a_row_v + ddt_x
        da_ref[...] += _sum0(da * dtv)
        dxs_ref[...] = jnp.concatenate(dxs, axis=1)
        dbc_ref[...] = jnp.concatenate(dbs + dcs, axis=1)

    rev = lambda c: nc // per - 1 - c
    return _call_with_step(
        body, hosted, None, (xbc, xbc, xbc, dt, a_row, hs, dy), name=name, grid=(nc // per,),
        in_specs=[pl.BlockSpec((G, 512), lambda c: (rev(c), 0)), pl.BlockSpec((G, 256), lambda c: (rev(c), 2)),
                  pl.BlockSpec((G, 256), lambda c: (rev(c), 3)), pl.BlockSpec((G, 128), lambda c: (rev(c), 0)),
                  pl.BlockSpec((1, 128), lambda c: (0, 0)), pl.BlockSpec((per, 512, 128), lambda c: (rev(c), 0, 0)),
                  pl.BlockSpec((G, 512), lambda c: (rev(c), 0))],
        out_specs=[pl.BlockSpec((G, 512), lambda c: (rev(c), 0)), pl.BlockSpec((G, 512), lambda c: (rev(c), 0)),
                   pl.BlockSpec((G, 128), lambda c: (rev(c), 0)), pl.BlockSpec((1, 128), lambda c: (0, 0))],
        out_shape=[jax.ShapeDtypeStruct((S, 512), F32), jax.ShapeDtypeStruct((S, 512), F32),
                   jax.ShapeDtypeStruct((S, 128), F32), jax.ShapeDtypeStruct((1, 128), F32)],
        sem=("arbitrary",), scratch_shapes=[pltpu.VMEM((4, 128, 128), F32)])


HBM = pl.BlockSpec(memory_space=pl.ANY)


class _Step:
    def __init__(self, inputs, out_shapes, n_sems, start, finish, mid=None):
        self.inputs, self.out_shapes, self.n_sems = inputs, out_shapes, n_sems
        self.start, self.finish, self.mid = start, finish, mid
        self.alias = []


class _Shifted:
    def __init__(self, ref, off):
        self.ref, self.off = ref, off

    @property
    def at(self):
        return self

    def __getitem__(self, j):
        return self.ref.at[self.off + j]


def _merge_steps(steps):
    offs = [sum(s.n_sems for s in steps[:i]) for i in range(len(steps) + 1)]
    i_offs = [sum(len(s.inputs) for s in steps[:i]) for i in range(len(steps))]
    o_offs = [sum(len(s.out_shapes) for s in steps[:i]) for i in range(len(steps))]

    def phase(which):
        def run(ins, outs, sems):
            for s, off, i0, o0 in zip(steps, offs, i_offs, o_offs):
                fn = getattr(s, which)
                if fn is not None:
                    fn(ins[i0:i0 + len(s.inputs)], outs[o0:o0 + len(s.out_shapes)],
                       [_Shifted(sems[0], off), _Shifted(sems[1], off)])
        return run

    merged = _Step([a for s in steps for a in s.inputs], [o for s in steps for o in s.out_shapes], offs[-1],
                   phase("start"), phase("finish"), phase("mid") if any(s.mid for s in steps) else None)
    merged.alias = [(i0 + a, o0 + b) for s, i0, o0 in zip(steps, i_offs, o_offs) for a, b in s.alias]
    return merged


def _place():
    x, y, c = lax.axis_index("x"), lax.axis_index("y"), lax.axis_index("c")
    chips = [(1 - x, y), (x, 1 - y), (1 - x, 1 - y)]
    return x, y, c, chips


def _mesh_pos():
    return 2 * lax.axis_index("x") + lax.axis_index("y"), lax.axis_index("c")


def _chunks(rows, tile):
    return next(n for n in (4, 3, 2, 1) if rows % (n * tile) == 0)


def _remote(src, dst, sems, j, to):
    return pltpu.make_async_remote_copy(src_ref=src, dst_ref=dst, send_sem=sems[0].at[j], recv_sem=sems[1].at[j],
                                        device_id=to, device_id_type=MESH)


def _gather_step(wp):
    R, C = wp.shape
    H = R // 2
    nq = _chunks(H, 16)
    CH = H // nq

    def copies(ins, outs, sems):
        x, y, c, chips = _place()
        sib, me = (x, y, 1 - c), 2 * x + y
        w_ref, out_ref = ins[0], outs[0]

        def piece(k, hc, q):
            return out_ref.at[k, pl.ds(hc * H + q * CH, CH), :]

        sends, landed, fwds, fwd_landed = [], [], [], []
        for q in range(nq):
            for j, (px, py) in enumerate(chips):
                k = 2 * px + py
                sends.append(_remote(w_ref.at[pl.ds(c * H + q * CH, CH), :], piece(me, c, q), sems, j * nq + q,
                                     (px, py, c)))
                landed.append(_remote(piece(k, c, q), piece(k, c, q), sems, j * nq + q, (px, py, c)))
                fwds.append(_remote(piece(k, c, q), piece(k, c, q), sems, (3 + j) * nq + q, sib))
                fwd_landed.append(_remote(piece(k, 1 - c, q), piece(k, 1 - c, q), sems, (3 + j) * nq + q, sib))
        return sends, landed, fwds, fwd_landed

    def start(ins, outs, sems):
        for cp in copies(ins, outs, sems)[0]:
            cp.start()

    def mid(ins, outs, sems):
        _, landed, fwds, _ = copies(ins, outs, sems)
        for arrived, onward in zip(landed, fwds):
            arrived.wait_recv()
            onward.start()

    def finish(ins, outs, sems):
        sends, _, fwds, fwd_landed = copies(ins, outs, sems)
        for cp in fwd_landed:
            cp.wait_recv()
        for cp in sends + fwds:
            cp.wait_send()

    return _Step([wp], [jax.ShapeDtypeStruct((N_SHARD, R, C), wp.dtype)], 6 * nq, start, finish, mid)


def _pair_exchange_step(gp):
    n, R, C = gp.shape
    H = R // 2
    nq = _chunks(H, 8)
    CH = H // nq

    def copies(ins, outs, sems):
        x, y, c, _ = _place()
        return [_remote(ins[0].at[k, pl.ds((1 - c) * H + q * CH, CH), :], outs[0].at[k, pl.ds(q * CH, CH), :], sems,
                        k * nq + q, (x, y, 1 - c)) for k in range(n) for q in range(nq)]

    def start(ins, outs, sems):
        for cp in copies(ins, outs, sems):
            cp.start()

    def finish(ins, outs, sems):
        for cp in copies(ins, outs, sems):
            cp.wait()

    return _Step([gp], [jax.ShapeDtypeStruct((n, H, C), gp.dtype)], n * nq, start, finish)


def _chip_exchange_step(pb):
    n, H, C = pb.shape
    nq = _chunks(H, 16)
    CH = H // nq

    def copies(ins, outs, sems):
        x, y, c, chips = _place()
        return [_remote(ins[0].at[2 * px + py, pl.ds(q * CH, CH), :], outs[0].at[j, pl.ds(q * CH, CH), :], sems,
                        j * nq + q, (px, py, c)) for q in range(nq) for j, (px, py) in enumerate(chips)]

    def start(ins, outs, sems):
        for cp in copies(ins, outs, sems):
            cp.start()

    def finish(ins, outs, sems):
        for cp in copies(ins, outs, sems):
            cp.wait()

    return _Step([pb], [jax.ShapeDtypeStruct((3, H, C), pb.dtype)], 3 * nq, start, finish)


def _pair_fill_step(red):
    R, C = red.shape
    H = R // 2
    nq = _chunks(H, 8)
    CH = H // nq

    def copies(ins, outs, sems):
        x, y, c, _ = _place()
        return [_remote(ins[0].at[pl.ds(c * H + j * CH, CH), :], outs[0].at[pl.ds(c * H + j * CH, CH), :], sems, j,
                        (x, y, 1 - c)) for j in range(nq)]

    def start(ins, outs, sems):
        for cp in copies(ins, outs, sems):
            cp.start()

    def finish(ins, outs, sems):
        for cp in copies(ins, outs, sems):
            cp.wait()

    step = _Step([red], [jax.ShapeDtypeStruct((R, C), red.dtype)], nq, start, finish)
    step.alias = [(0, 0)]
    return step


def _sem_scratch(step):
    return [pltpu.SemaphoreType.DMA((step.n_sems,)), pltpu.SemaphoreType.DMA((step.n_sems,))]


def _run_step(step, name):
    ni, no = len(step.inputs), len(step.out_shapes)

    def body(*refs):
        ins, outs, sems = refs[:ni], refs[ni:ni + no], refs[ni + no:]
        step.start(ins, outs, sems)
        if step.mid is not None:
            step.mid(ins, outs, sems)
        step.finish(ins, outs, sems)

    return pl.pallas_call(body, name=name, in_specs=[HBM] * ni, out_specs=[HBM] * no, out_shape=step.out_shapes,
                          input_output_aliases=dict(step.alias),
                          scratch_shapes=_sem_scratch(step))(*step.inputs)


def _grid_flags(grid):
    ids = [pl.program_id(d) for d in range(len(grid))]
    first = functools.reduce(lambda a, b: a & b, [i == 0 for i in ids])
    last = functools.reduce(lambda a, b: a & b, [i == n - 1 for i, n in zip(ids, grid)])
    return first, last, last


def _call_with_step(core, step, flags, args, *, name, grid, in_specs, out_specs, out_shape, sem, scratch_shapes=(),
                    aliases=None):
    aliases = aliases or {}
    if step is None:
        return pl.pallas_call(core, name=name, grid=grid, in_specs=in_specs, out_specs=out_specs,
                              out_shape=out_shape, scratch_shapes=list(scratch_shapes),
                              input_output_aliases=aliases, compiler_params=_params(sem))(*args)
    n_in, n_out, n_scr = len(in_specs), len(out_specs), len(scratch_shapes)
    si, so = len(step.inputs), len(step.out_shapes)
    flags = flags or (lambda: _grid_flags(grid))
    aliases = {**aliases, **{n_in + a: n_out + b for a, b in step.alias}}

    def body(*refs):
        ins, s_ins = refs[:n_in], refs[n_in:n_in + si]
        outs = refs[n_in + si:n_in + si + n_out]
        s_outs = refs[n_in + si + n_out:n_in + si + n_out + so]
        scr = refs[n_in + si + n_out + so:n_in + si + n_out + so + n_scr]
        sems = refs[n_in + si + n_out + so + n_scr:]
        first, middle, last = flags()

        @pl.when(first)
        def _():
            step.start(s_ins, s_outs, sems)

        if step.mid is not None:
            @pl.when(middle)
            def _():
                step.mid(s_ins, s_outs, sems)

        core(*ins, *outs, *scr)

        @pl.when(last)
        def _():
            step.finish(s_ins, s_outs, sems)

    return pl.pallas_call(
        body, name=name, grid=grid, in_specs=list(in_specs) + [HBM] * si, out_specs=list(out_specs) + [HBM] * so,
        out_shape=list(out_shape) + list(step.out_shapes), scratch_shapes=list(scratch_shapes) + _sem_scratch(step),
        input_output_aliases=aliases, compiler_params=_params(("arbitrary",) * len(grid)))(*args, *step.inputs)


def _attn_flags(nq):
    h, qi = pl.program_id(0), pl.program_id(1)
    return ((h == 0) & (qi == 0), (h == MLA_HEADS - 1) & (qi == 0), (h == MLA_HEADS - 1) & (qi == nq - 1))


def _att_mask(s_t, q0, k0):
    krow = k0 + lax.broadcasted_iota(jnp.int32, s_t.shape, 0)
    qcol = q0 + lax.broadcasted_iota(jnp.int32, s_t.shape, 1)
    return jnp.where(krow <= qcol, s_t, NEG)


def _loop_blocks(lo, hi, step, carry):
    n = hi - lo

    def four(i, c):
        kb = lo + 4 * i
        return step(kb + 3, step(kb + 2, step(kb + 1, step(kb, c))))

    carry = lax.fori_loop(0, n // 4, four, carry)
    base = lo + 4 * (n // 4)
    carry = lax.cond(n % 4 >= 2, lambda c: step(base + 1, step(base, c)), lambda c: c, carry)
    return lax.cond(n % 2 == 1, lambda c: step(hi - 1, c), lambda c: c, carry)


def _rows(ref, blk, t):
    return ref[pl.ds(pl.multiple_of(blk * t, t), t), :]


def _cols(ref, blk, t):
    return ref[:, pl.ds(pl.multiple_of(blk * t, t), t)]


def _attn_fwd(q, k, v_t, *, name, hosted=None):
    S = q.shape[0]
    t = min(ATTN_TILE, S)
    nq = S // t

    def body(q_ref, k_ref, vt_ref, o_ref, lse_ref):
        qi = pl.program_id(1)
        qv = q_ref[...]

        def absorb(kb, carry, masked):
            m, l, acc = carry
            s_t = lax.dot_general(_rows(k_ref, kb, t), qv, NT, preferred_element_type=F32)
            if masked:
                s_t = _att_mask(s_t, qi * t, kb * t)
            m_new = jnp.maximum(m, jnp.max(s_t, axis=0, keepdims=True))
            p_t = jnp.exp(s_t - m_new)
            corr = jnp.exp(m - m_new)
            return (m_new, corr * l + jnp.sum(p_t, axis=0, keepdims=True),
                    corr * acc + lax.dot_general(_cols(vt_ref, kb, t), p_t.astype(BF16), NN,
                                                 preferred_element_type=F32))

        init = (jnp.full((1, t), NEG, F32), jnp.zeros((1, t), F32), jnp.zeros((LANES, t), F32))
        carry = _loop_blocks(0, qi, lambda kb, c: absorb(kb, c, False), init)
        m, l, acc = absorb(qi, carry, True)
        o_ref[...] = acc / l
        lse_ref[0] = m + jnp.log(l)

    return _call_with_step(
        body, hosted, lambda: _attn_flags(nq), (q, k, v_t), name=name, grid=(MLA_HEADS, nq),
        in_specs=[pl.BlockSpec((t, LANES), lambda h, qi: (qi, h)),
                  pl.BlockSpec((S, LANES), lambda h, qi: (0, h)),
                  pl.BlockSpec((LANES, S), lambda h, qi: (h, 0))],
        out_specs=[pl.BlockSpec((LANES, t), lambda h, qi: (h, qi)),
                   pl.BlockSpec((1, 1, t), lambda h, qi: (h, 0, qi))],
        out_shape=[jax.ShapeDtypeStruct((MLA_HEADS * LANES, S), F32), jax.ShapeDtypeStruct((MLA_HEADS, 1, S), F32)],
        sem=("parallel", "arbitrary"))


def _attn_bwd(q, k, v, o_t, do_t, lse, *, name, hosted=None):
    S = q.shape[0]
    t = min(ATTN_TILE, S)
    nq = S // t

    def body(q_ref, k_ref, v_ref, o_ref, do_ref, lse_ref, dq_ref, dk_ref, dv_ref, dkt_scr):
        qi = pl.program_id(1)

        @pl.when(qi == 0)
        def _():
            dkt_scr[...] = jnp.zeros_like(dkt_scr)
            dv_ref[...] = jnp.zeros_like(dv_ref)

        qv = q_ref[...]
        q_t = qv.T
        dov = do_ref[...]
        delta = jnp.sum(dov * o_ref[...], axis=0, keepdims=True)
        dob = dov.astype(BF16)
        lse_v = lse_ref[0]

        def step(kb, acc, masked):
            kt = _rows(k_ref, kb, t)
            s_t = lax.dot_general(kt, qv, NT, preferred_element_type=F32)
            if masked:
                s_t = _att_mask(s_t, qi * t, kb * t)
            p_t = jnp.exp(s_t - lse_v)
            dp_t = lax.dot_general(_rows(v_ref, kb, t), dob, NN, preferred_element_type=F32)
            ds_t = (p_t * (dp_t - delta)).astype(BF16)
            keys = pl.ds(pl.multiple_of(kb * t, t), t)
            dv_ref[:, keys] += lax.dot_general(dob, p_t.astype(BF16), NT, preferred_element_type=F32)
            dkt_scr[:, keys] += lax.dot_general(q_t, ds_t, NT, preferred_element_type=F32)
            return acc + lax.dot_general(kt, ds_t, TN, preferred_element_type=F32)

        acc = _loop_blocks(0, qi, lambda kb, c: step(kb, c, False), jnp.zeros((LANES, t), F32))
        dq_ref[...] = step(qi, acc, True).T

        @pl.when(qi == nq - 1)
        def _():
            dk_ref[...] = dkt_scr[...].T

    tile = pl.BlockSpec((t, LANES), lambda h, qi: (qi, h))
    tile_t = pl.BlockSpec((LANES, t), lambda h, qi: (h, qi))
    stat = pl.BlockSpec((1, 1, t), lambda h, qi: (h, 0, qi))
    seq = pl.BlockSpec((S, LANES), lambda h, qi: (0, h))
    seq_t = pl.BlockSpec((LANES, S), lambda h, qi: (h, 0))
    return _call_with_step(
        body, hosted, lambda: _attn_flags(nq), (q, k, v, o_t, do_t, lse), name=name, grid=(MLA_HEADS, nq),
        in_specs=[tile, seq, seq, tile_t, tile_t, stat],
        out_specs=[tile, seq, seq_t],
        out_shape=[jax.ShapeDtypeStruct((S, MLA_HEADS * LANES), F32), jax.ShapeDtypeStruct((S, MLA_HEADS * LANES), F32),
                   jax.ShapeDtypeStruct((MLA_HEADS * LANES, S), F32)],
        sem=("parallel", "arbitrary"), scratch_shapes=[pltpu.VMEM((LANES, S), F32)])


def _fn_ln(ctx, x, g, b):
    xhat, _ = _ln_stats(x)
    y = xhat * g + b
    return y, y


def _fn_conv_fwd(ctx, u, up, dtr, w8, cb, dtb):
    first = ctx.i == 0
    y = u * w8[3:4] + cb
    for s in (1, 2, 3):
        y = y + _shift_down(u, up, s, first) * w8[3 - s:4 - s]
    act = y * _sigmoid(y)
    v = dtr + dtb
    e = jnp.exp(-jnp.abs(v))
    one_p = 1.0 + e
    log1p = jnp.where(one_p == 1.0, e, jnp.log(one_p) * e / (one_p - 1.0))
    return y, act, jnp.maximum(v, 0.0) + log1p


def _fn_ssd_post(ctx, y, xs, z, dexp, g):
    yg = (y + xs * dexp) * (z * _sigmoid(z))
    outs = []
    for k in range(2):
        v = yg[:, 256 * k:256 * (k + 1)]
        outs.append(v * lax.rsqrt(_mean1(v * v) + RMS_EPS))
    return (jnp.concatenate(outs, axis=1) * g,)


def _fn_ssd_post_bwd(ctx, dyn, y, xs, z, dexp, g):
    yt = y + xs * dexp
    sig = _sigmoid(z)
    sz = z * sig
    yg = yt * sz
    dyh = dyn * g
    yh, dyg = [], []
    for k in range(2):
        sl = slice(256 * k, 256 * (k + 1))
        v = yg[:, sl]
        rs = lax.rsqrt(_mean1(v * v) + RMS_EPS)
        vh = v * rs
        yh.append(vh)
        dyg.append(rs * (dyh[:, sl] - vh * _mean1(dyh[:, sl] * vh)))
    yh = jnp.concatenate(yh, axis=1)
    dyg = jnp.concatenate(dyg, axis=1)
    dyt = dyg * sz
    dz = dyg * yt * (sig * (1.0 + z * (1.0 - sig)))
    return dyt, dz, dyt * dexp, _sum0(dyt * xs), _sum0(dyn * yh)


def _fn_mla_pre(ctx, ql, kvl, gq, gkv):
    return _rms_fwd(ql, gq), _rms_fwd(kvl, gkv)


def _fn_mla_pre_bwd(ctx, ql, kvl, dqn, dkvn_k, dkvn_v, ddtr, gq, gkv):
    dql, dgq = _rms_bwd(ql, dqn, gq)
    dkvl, dgkv = _rms_bwd(kvl, dkvn_k + dkvn_v, gkv)
    return jnp.concatenate([dql, ddtr.astype(F32), dkvl], axis=1), dgq, dgkv


def _fn_rope(ctx, qp, kn, kr, trig):
    ta, tb, tc = _rope_tables(trig)
    kpe = _rope(kr, ta, tb, tc)
    qs, ks = [], []
    for h in range(MLA_HEADS):
        sl = slice(128 * h, 128 * (h + 1))
        qs.append(_rope(qp[:, sl], ta, tb, tc) * MLA_SCALE)
        ks.append(kn[:, sl] + kpe)
    return jnp.concatenate(qs, axis=1), jnp.concatenate(ks, axis=1)


def _fn_rope_bwd(ctx, dq, dk, trig):
    ta, tb, tc = _rope_tables(trig)
    qs = []
    ksum = jnp.zeros_like(ta)
    for h in range(MLA_HEADS):
        sl = slice(128 * h, 128 * (h + 1))
        qs.append(_rope_bwd(dq[:, sl] * MLA_SCALE, ta, tb, tc))
        ksum = ksum + dk[:, sl]
    lane = _lane(ksum.shape)
    dkr = jnp.where((lane >= 64) & (lane < 96), _rope_bwd(ksum, ta, tb, tc), 0.0)
    return jnp.concatenate(qs, axis=1), jnp.concatenate([dkr, jnp.zeros_like(dkr)], axis=1)


def _mem_probs(qh, kh):
    s = _dot(qh, kh, NT) * MEM_SCALE
    p = jnp.exp(s - jnp.max(s, axis=1, keepdims=True))
    return p / jnp.sum(p, axis=1, keepdims=True)


def _fn_mem_fwd(ctx, q, km, vm):
    outs = []
    for h in range(MEM_HEADS):
        sl = slice(256 * h, 256 * (h + 1))
        outs.append(_dot(_mem_probs(q[:, sl], km[:, sl]), vm[:, sl]))
    return (jnp.concatenate(outs, axis=1),)


def _fn_mem_bwd(ctx, q, do, km, vm):
    dqs, dks, dvs = [], [], []
    for h in range(MEM_HEADS):
        sl = slice(256 * h, 256 * (h + 1))
        p = _mem_probs(q[:, sl], km[:, sl])
        dvs.append(_dot(p, do[:, sl], TN))
        dp = _dot(do[:, sl], vm[:, sl], NT)
        ds = p * (dp - jnp.sum(dp * p, axis=1, keepdims=True)) * MEM_SCALE
        dqs.append(_dot(ds, km[:, sl]))
        dks.append(_dot(ds, q[:, sl], TN))
    return jnp.concatenate(dqs, axis=1), jnp.concatenate(dks, axis=1), jnp.concatenate(dvs, axis=1)


def _fn_res_ln(ctx, h, r, g, b):
    xhat, _ = _ln_stats(ALPHA * h + r)
    y = xhat * g + b
    return y, y


def _fn_res_ln_bwd(ctx, h, r, d1, d2, g):
    xhat, rstd = _ln_stats(ALPHA * h + r)
    return _ln_bwd(xhat, rstd, ALPHA * d1 + d2, g)


def _fn_res2_ln(ctx, h, r1, r2, g, b):
    xhat, _ = _ln_stats(ALPHA * h + (r1 + r2))
    return (xhat * g + b,)


def _fn_res2_ln_bwd(ctx, h, r1, r2, d1, d2, g):
    xhat, rstd = _ln_stats(ALPHA * h + (r1 + r2))
    return _ln_bwd(xhat, rstd, ALPHA * d1 + d2, g)


def _fn_in_ln_bwd(ctx, x, d1, d2, g):
    xhat, rstd = _ln_stats(x)
    return _ln_bwd(xhat, rstd, ALPHA * d1 + d2, g)


def _fn_final(ctx, h2, ff, tgt, g, b):
    xhat, rstd = _ln_stats(ALPHA * h2 + ff)
    e = xhat * g + b - tgt
    loss = 0.5 * _sum0(jnp.sum(e * e, axis=1, keepdims=True)) / D_MODEL
    dx, dg, db = _ln_bwd(xhat, rstd, e / D_MODEL, g)
    return dx, dx, dg, db, loss


def _epi_du(da, u):
    return da * 2.0 * jnp.maximum(u.astype(F32), 0.0)


def _relu2(u):
    r = jnp.maximum(u.astype(F32), 0.0)
    return r * r


def _fn_conv_bwd_a(ctx, y, dxs1, dxs2, dbc, dtr, ddt, dtb):
    sig = _sigmoid(y)
    dact = jnp.concatenate([dxs1 + dxs2, dbc], axis=1)
    dyc = dact * (sig * (1.0 + y * (1.0 - sig)))
    ddtr = ddt * _sigmoid(dtr + dtb)
    return dyc, ddtr, _sum0(dyc), _sum0(ddtr)


def _fn_conv_bwd_b(ctx, d, dn, u, up, w8):
    first, last = ctx.i == 0, ctx.i == ctx.n - 1
    du = d * w8[3:4]
    row = lax.broadcasted_iota(jnp.int32, w8.shape, 0)
    dw = jnp.where(row == 3, _sum0(d * u), 0.0)
    for s in (1, 2, 3):
        du = du + _shift_up(d, dn, s, last) * w8[3 - s:4 - s]
        dw = dw + jnp.where(row == 3 - s, _sum0(d * _shift_down(u, up, s, first)), 0.0)
    return du, dw


def _fn_adam(ctx, w, g, m, v):
    m = ADAM_B1 * m + (1.0 - ADAM_B1) * g
    v = ADAM_B2 * v + (1.0 - ADAM_B2) * (g * g)
    m_hat = m / (1.0 - ADAM_B1 ** ADAM_STEP)
    v_hat = v / (1.0 - ADAM_B2 ** ADAM_STEP)
    return -ADAM_LR * (m_hat / (jnp.sqrt(v_hat) + ADAM_EPS) + ADAM_WD * w), m, v


def _z(r, c, dt):
    return jnp.zeros((r, c), dt)


W_IN_SHARD = 554
W_IN_GROUPS = [(0, 512, 1024), (512, 1536, 0), (1536, 1544, 1920), (1544, 1928, 1536), (1928, 2184, 2048),
               (2184, 2216, 2368)]


def _pad_w_in(ws):
    r, dt = ws.shape[1], ws.dtype

    def cols(a, b):
        out = []
        for k in range(N_SHARD):
            lo, hi = max(a, k * W_IN_SHARD), min(b, (k + 1) * W_IN_SHARD)
            if lo < hi:
                out.append(ws[k][:, lo - k * W_IN_SHARD:hi - k * W_IN_SHARD])
        return out

    return jnp.concatenate(cols(512, 1536) + cols(0, 512) + cols(1544, 1928) + cols(1536, 1544) + [_z(r, 120, dt)]
                           + cols(1928, 2184) + [_z(r, 64, dt)] + cols(2184, 2216) + [_z(r, 32, dt), _z(r, 128, dt)],
                           axis=1)


def _unpad_w_in(d):
    shards = []
    for k in range(N_SHARD):
        a, b = k * W_IN_SHARD, (k + 1) * W_IN_SHARD
        parts = []
        for o0, o1, p0 in W_IN_GROUPS:
            lo, hi = max(a, o0), min(b, o1)
            if lo < hi:
                parts.append(d[:, p0 + lo - o0:p0 + hi - o0])
        shards.append(jnp.concatenate(parts, axis=1))
    return jnp.stack(shards)


def _pad_heads(w, width):
    r = w.shape[0]
    w3 = w.reshape(r, MLA_HEADS, width)
    return jnp.pad(w3, ((0, 0), (0, 0), (0, 128 - width))).reshape(r, MLA_HEADS * 128)


def _row(v, width=None):
    v = v.reshape(1, -1).astype(F32)
    if width is not None and v.shape[1] < width:
        v = jnp.pad(v, ((0, 0), (0, width - v.shape[1])))
    return v


BIG = {
    "w_in": (1024, 2216, 1), "w_q_up": (384, 768, 1), "w_kv_up": (256, 1024, 1), "w_mix_out": (1024, 1024, 0),
    "w_mem_q": (1024, 1024, 0), "w_mem_k": (1024, 1024, 0), "w_mem_v": (1024, 1024, 0), "w_mem_o": (1024, 1024, 0),
    "w_up": (1024, 4096, 1), "w_down": (4096, 1024, 0), "conv_w": (4, 1024, 1),
}
BIG_ORDER = list(BIG)
SMALL_ORDER = ["ln_in_g", "ln_in_b", "conv_b", "dt_bias", "a_log", "d_skip", "ssd_norm_g", "q_norm_g", "kv_norm_g",
               "ln1_g", "ln1_b", "ln2_g", "ln2_b", "ln3_g", "ln3_b"]
N_SHARD = 4
N_DEV = 8
PACK_COLS = 1024
PACK_A_ROW = {"w_down": 0, "w_up": 1024, "w_mem_q": 2048, "w_mem_k": 2304, "w_mem_v": 2560, "w_mem_o": 2816,
              "w_mix_out": 3072}
PACK_A_ORDER = list(PACK_A_ROW)
PACK_A_ROWS = 3328
PACK_B_ORDER = ["w_q_up", "w_kv_up", "conv_w"]
PACK_B_ROWS = 160


def _shard_shape(name):
    r, c, ax = BIG[name]
    return (r // N_SHARD, c) if ax == 0 else (r, c // N_SHARD)


def _split_shards(name, full):
    r, c, ax = BIG[name]
    if ax == 0:
        return full.reshape(N_SHARD, -1)
    return full.reshape(r, N_SHARD, c // N_SHARD).transpose(1, 0, 2).reshape(N_SHARD, -1)


def _join_shards(name, parts):
    r, c, ax = BIG[name]
    if ax == 0:
        return parts.reshape(r, c)
    return parts.reshape(N_SHARD, r, c // N_SHARD).transpose(1, 0, 2).reshape(r, c)


def _small_all_reduce(g, step=None):
    r, cdim = g.shape
    si, so = (len(step.inputs), len(step.out_shapes)) if step else (0, 0)

    def body(g_ref, *refs):
        s_ins, out_ref, s_outs = refs[:si], refs[si], refs[si + 1:si + 1 + so]
        buf, send_sems, recv_sems = refs[si + 1 + so:si + 4 + so]
        s_sems = refs[si + 4 + so:]
        if step:
            step.start(s_ins, s_outs, s_sems)
        x, y, c, _ = _place()
        me = 4 * x + 2 * y + c
        buf[me] = g_ref[...]
        copies = []
        for d in range(1, N_DEV):
            to = me ^ d
            cp = pltpu.make_async_remote_copy(src_ref=g_ref, dst_ref=buf.at[me], send_sem=send_sems.at[d - 1],
                                              recv_sem=recv_sems.at[d - 1],
                                              device_id=(to // 4, (to // 2) % 2, to % 2), device_id_type=MESH)
            cp.start()
            copies.append(cp)
        for cp in copies:
            cp.wait()
        acc = buf[0]
        for d in range(1, N_DEV):
            acc = acc + buf[d]
        out_ref[...] = acc
        if step:
            step.finish(s_ins, s_outs, s_sems)

    res = pl.pallas_call(
        body, name="small_all_reduce",
        in_specs=[pl.BlockSpec(memory_space=pltpu.VMEM)] + [HBM] * si,
        out_specs=[pl.BlockSpec(memory_space=pltpu.VMEM)] + [HBM] * so,
        out_shape=[jax.ShapeDtypeStruct((r, cdim), F32)] + (list(step.out_shapes) if step else []),
        scratch_shapes=[pltpu.VMEM((N_DEV, r, cdim), F32), pltpu.SemaphoreType.DMA((N_DEV - 1,)),
                        pltpu.SemaphoreType.DMA((N_DEV - 1,))] + (_sem_scratch(step) if step else []),
    )(g, *(step.inputs if step else []))
    return res if step else res[0]


def _half_tile(h):
    return next(t for t in range(512, 0, -16) if h % t == 0)


def _pair_sum(gp, theirs, name):
    n, R, C = gp.shape
    H = R // 2
    tr = _half_tile(H)
    nb = H // tr

    def body(s_ref, g_ref, t_ref, o_ref):
        o_ref[...] = (g_ref[...] + t_ref[...]).astype(o_ref.dtype)

    def shard(k, s):
        return k + (k >= s[1]).astype(jnp.int32)

    me, c = _mesh_pos()
    return pl.pallas_call(
        body, name=name,
        grid_spec=pltpu.PrefetchScalarGridSpec(
            num_scalar_prefetch=1, grid=(n - 1, nb),
            in_specs=[pl.BlockSpec((1, tr, C), lambda k, i, s: (shard(k, s), s[0] * nb + i, 0)),
                      pl.BlockSpec((1, tr, C), lambda k, i, s: (shard(k, s), i, 0))],
            out_specs=pl.BlockSpec((1, tr, C), lambda k, i, s: (shard(k, s), i, 0))),
        out_shape=jax.ShapeDtypeStruct((n, H, C), BF16), compiler_params=_params(("arbitrary", "arbitrary")),
    )(jnp.stack([c, me]).astype(jnp.int32), gp, theirs)


def _chip_sum(gp, theirs, got, name):
    n, R, C = gp.shape
    H = R // 2
    tr = _half_tile(H)
    nb = H // tr

    def body(s_ref, g_ref, t_ref, r_ref, o_ref):
        acc = g_ref[0] + t_ref[0]
        for j in range(3):
            acc = acc + r_ref[j].astype(F32)
        o_ref[...] = acc

    me, c = _mesh_pos()
    return pl.pallas_call(
        body, name=name,
        grid_spec=pltpu.PrefetchScalarGridSpec(
            num_scalar_prefetch=1, grid=(nb,),
            in_specs=[pl.BlockSpec((1, tr, C), lambda i, s: (s[0], s[1] * nb + i, 0)),
                      pl.BlockSpec((1, tr, C), lambda i, s: (s[0], i, 0)),
                      pl.BlockSpec((3, tr, C), lambda i, s: (0, i, 0))],
            out_specs=pl.BlockSpec((tr, C), lambda i, s: (s[1] * nb + i, 0))),
        out_shape=jax.ShapeDtypeStruct((R, C), F32), compiler_params=_params(("arbitrary",)),
    )(jnp.stack([me, c]).astype(jnp.int32), gp, theirs, got)


def _unpack_group_b(g_c, g_b, own):
    me, _ = _mesh_pos()
    g_c = lax.dynamic_update_slice(g_c, own[0][None], (me, 0, 0))
    g_b = lax.dynamic_update_slice(g_b, own[1][None], (me, 0, 0)).reshape(N_SHARD, -1)
    WB, off = {"w_in": g_c}, 0
    for n in PACK_B_ORDER:
        sr, sc = _shard_shape(n)
        cnt = sr * sc
        if n == "conv_w":
            part = lax.bitcast_convert_type(g_b[:, off:off + 2 * cnt].reshape(N_SHARD, cnt, 2), F32)
            off += 2 * cnt
        else:
            part = g_b[:, off:off + cnt]
            off += cnt
        WB[n] = _join_shards(n, part)
    return WB


def _group_b_grads(dw_in_p, dw_q_p, dw_k_p, dw_v_pt, dconv_w8):
    return {
        "w_in": _unpad_w_in(dw_in_p),
        "w_q_up": dw_q_p.reshape(384, MLA_HEADS, 128)[:, :, :MLA_QK].reshape(384, MLA_HEADS * MLA_QK),
        "w_kv_up": jnp.concatenate([dw_k_p.reshape(MLA_KV_RANK, MLA_HEADS, 128)[:, :, :64],
                                    dw_v_pt.T.reshape(MLA_KV_RANK, MLA_HEADS, 128)[:, :, :64]], axis=2).reshape(
                                        MLA_KV_RANK, MLA_HEADS * 128),
        "conv_w": dconv_w8[0:4],
    }


def _pack_group_b(big_b):
    gflat = [_split_shards(n, big_b[n]) for n in PACK_B_ORDER]
    used = sum(f.shape[1] for f in gflat)
    gflat.append(jnp.zeros((N_SHARD, PACK_B_ROWS * PACK_COLS - used), F32))
    return big_b["w_in"], jnp.concatenate(gflat, axis=1).reshape(N_SHARD, PACK_B_ROWS, PACK_COLS)


def _local_step(x, mem, positions, target, WB, P, *, wp_a=None, g_a=None, wp_b=None):
    S = x.shape[0]
    tr = ROW_TILE
    dist = g_a is None
    g_in, b_in = _row(P["ln_in_g"]), _row(P["ln_in_b"])
    res = _rowwise(_fn_ln, [x], [g_in, b_in], [D_MODEL, (D_MODEL, BF16)], tr=tr, name="ln_in",
                   hosted=_merge_steps([_gather_step(w) for w in wp_b]) if dist else None)
    h0, h0_b = res[0], res[1]
    if dist:
        WB = _unpack_group_b(res[2], res[3], wp_b)
    P = {**P, "conv_w": WB["conv_w"]}
    w_in_p = _pad_w_in(WB["w_in"])
    w_q_p = _pad_heads(WB["w_q_up"], MLA_QK)
    w_kv3 = WB["w_kv_up"].reshape(MLA_KV_RANK, MLA_HEADS, 128)
    w_k_p = _pad_heads(w_kv3[:, :, :64].reshape(MLA_KV_RANK, 512), 64)
    w_v_p = _pad_heads(w_kv3[:, :, 64:].reshape(MLA_KV_RANK, 512), 64)
    w_v_pt = w_v_p.T
    conv_w8 = jnp.pad(P["conv_w"].astype(F32), ((0, 4), (0, 0)))
    conv_b = _row(P["conv_b"])
    dt_b = _row(P["dt_bias"], 128)
    a_head = -jnp.exp(P["a_log"].reshape(-1).astype(F32))
    a_row = _row(a_head, 128)
    dexp = jnp.repeat(P["d_skip"].reshape(-1).astype(F32), 64).reshape(1, 512)
    g_ssd, g_q, g_kv = _row(P["ssd_norm_g"]), _row(P["q_norm_g"]), _row(P["kv_norm_g"])
    g1, b1, g2, b2, g3, b3 = (_row(P[k]) for k in ("ln1_g", "ln1_b", "ln2_g", "ln2_b", "ln3_g", "ln3_b"))

    half = MLA_ROPE // 2
    inv_freq = jnp.power(ROPE_THETA, -jnp.arange(half, dtype=F32) / half)
    ang = inv_freq.reshape(half, 1) * positions.reshape(1, S).astype(F32)
    trig = ("cols", jnp.concatenate([jnp.cos(ang), jnp.sin(ang)], axis=0))

    proj = _mm(h0_b, w_in_p, form="nn", tn=IN_W // 2, name="mm_in")
    conv_y, xbc, dt = _rowwise(
        _fn_conv_fwd, [(proj,) + SEG_XBC, ("prev", proj) + SEG_XBC, (proj,) + SEG_DT], [conv_w8, conv_b, dt_b],
        [1024, 1024, 128], tr=tr, name="conv_fwd")
    y_ssd, hs = _ssd_fwd(xbc, dt, a_row, name="ssd_fwd")
    (y_n,) = _rowwise(_fn_ssd_post, [y_ssd, (xbc, 0, 512), (proj,) + SEG_Z], [dexp, g_ssd], [(512, BF16)], tr=tr,
                      name="ssd_post")
    q_n, kv_n = _rowwise(_fn_mla_pre, [(proj,) + SEG_QLAT, (proj,) + SEG_KVLAT], [g_q, g_kv], [384, 256], tr=tr,
                         name="mla_pre")
    qp = _mm(q_n, w_q_p, form="nn", name="mm_q_up")
    kn = _mm(kv_n, w_k_p, form="nn", name="mm_k_up")
    v_nat = _mm(kv_n, w_v_p, form="nn", out_dtype=BF16, name="mm_v_up")
    v_t = _mm(w_v_pt, kv_n, form="nt", out_dtype=BF16, name="mm_v_up_t")
    q_rot, k_full = _rowwise(_fn_rope, [qp, kn, (proj,) + SEG_KR, trig], [],
                             [(1024, BF16), (1024, BF16)], tr=tr, name="rope")
    res = _attn_fwd(q_rot, k_full, v_t, name="attn_fwd", hosted=_gather_step(wp_a) if dist else None)
    o_t, lse = res[0], res[1]
    if dist:
        g_a = lax.dynamic_update_slice(res[2], wp_a[None], (_mesh_pos()[0], 0, 0))
    r_mix = PACK_A_ROW["w_mix_out"]
    w_mix_o = jnp.pad(g_a[2:4, r_mix:r_mix + 256].reshape(MLA_HEADS, 64, D_MODEL),
                      ((0, 0), (0, 64), (0, 0))).reshape(MLA_HEADS * 128, D_MODEL)
    mix_o = _mm(o_t, w_mix_o, form="tn", name="mm_mix_o")
    mix_y = _mm(y_n, g_a, form="nn", b_pack="w_mix_out", name="mm_mix_y")
    (h1,) = _rowwise(_fn_res2_ln, [h0, mix_o, mix_y], [g1, b1], [D_MODEL], tr=tr, name="ln1")
    qm = _mm(h1, g_a, form="nn", b_pack="w_mem_q", out_dtype=BF16, name="mm_mem_q")
    km = _mm(mem, g_a, form="nn", b_pack="w_mem_k", out_dtype=BF16, name="mm_mem_k")
    vm = _mm(mem, g_a, form="nn", b_pack="w_mem_v", out_dtype=BF16, name="mm_mem_v")
    (om,) = _rowwise(_fn_mem_fwd, [qm], [km, vm], [(D_MODEL, BF16)], tr=tr, name="mem_fwd")
    xa = _mm(om, g_a, form="nn", b_pack="w_mem_o", name="mm_mem_o")
    h2, h2_b = _rowwise(_fn_res_ln, [h1, xa], [g2, b2], [D_MODEL, (D_MODEL, BF16)], tr=tr, name="ln2")
    u = _mm(h2_b, g_a, form="nn", b_pack="w_up", out_dtype=BF16, name="mm_up")
    ff = _mm(u, g_a, form="nn", a_pro=_relu2, b_pack="w_down", name="mm_down")

    gp = lax.empty((N_SHARD, PACK_A_ROWS, PACK_COLS), F32)
    dt3, dt3_b, dg3, db3, loss = _rowwise(_fn_final, [h2, ff, target], [g3, b3], [D_MODEL, (D_MODEL, BF16)],
                                          [(1, D_MODEL), (1, D_MODEL), (1, 128)], tr=tr, name="ln3_loss")
    du = _mm(dt3_b, g_a, form="nt", b_pack="w_down", epi=(_epi_du, u), out_dtype=BF16, name="mm_down_dx")
    gp = _mm(u, dt3_b, form="tn", a_pro=_relu2, out_pack=("w_down", gp), name="mm_down_dw")
    gp = _mm(h2_b, du, form="tn", out_pack=("w_up", gp), name="mm_up_dw")
    dh2 = _mm(du, g_a, form="nt", b_pack="w_up", name="mm_up_dx")
    dt2, dg2, db2 = _rowwise(_fn_res_ln_bwd, [h1, xa, dt3, dh2], [g2], [D_MODEL], [(1, D_MODEL)] * 2, tr=tr,
                             name="ln2_bwd")
    dom = _mm(dt2, g_a, form="nt", b_pack="w_mem_o", out_dtype=BF16, name="mm_mem_o_dx")
    gp = _mm(om, dt2, form="tn", out_pack=("w_mem_o", gp), name="mm_mem_o_dw")
    dqm, dkm, dvm = _rowwise(_fn_mem_bwd, [qm, dom], [km, vm], [(D_MODEL, BF16)], [(256, D_MODEL)] * 2, tr=tr,
                             name="mem_bwd")
    gp = _mm(h1, dqm, form="tn", out_pack=("w_mem_q", gp), name="mm_mem_q_dw")
    gp = _mm(mem, dkm, form="tn", out_pack=("w_mem_k", gp), name="mm_mem_k_dw")
    gp = _mm(mem, dvm, form="tn", out_pack=("w_mem_v", gp), name="mm_mem_v_dw")
    dh1 = _mm(dqm, g_a, form="nt", b_pack="w_mem_q", name="mm_mem_q_dx")
    dt1, dg1, db1 = _rowwise(_fn_res2_ln_bwd, [h0, mix_o, mix_y, dt2, dh1], [g1], [D_MODEL], [(1, D_MODEL)] * 2,
                             tr=tr, name="ln1_bwd")
    do_t = _mm(w_mix_o, dt1, form="nt", name="mm_mix_o_dx")
    dy_n = _mm(dt1, g_a, form="nt", b_pack="w_mix_out", b_rows=512, name="mm_mix_y_dx")
    dw_mix_o = _mm(o_t, dt1, form="nn", name="mm_mix_o_dw")
    gp = _mm(y_n, dt1, form="tn", out_pack=("w_mix_out", gp), name="mm_mix_y_dw")
    gp = lax.dynamic_update_slice(
        gp, dw_mix_o.reshape(MLA_HEADS, 128, D_MODEL)[:, :64].reshape(2, 256, D_MODEL), (2, r_mix, 0))
    dproj = lax.empty((S, IN_W), BF16)
    dy_ssd, dproj, dxs_skip, ddexp, dg_ssd = _rowwise(
        _fn_ssd_post_bwd, [dy_n, y_ssd, (xbc, 0, 512), (proj,) + SEG_Z], [dexp, g_ssd],
        [512, (512, dproj, SEG_Z[0]), 512], [(1, 512)] * 2, tr=tr, name="ssd_post_bwd")
    res = _ssd_bwd(xbc, dt, a_row, hs, dy_ssd, name="ssd_bwd", hosted=_pair_exchange_step(gp) if dist else None)
    dxs, dbc, ddt, da_head = res[0], res[1], res[2], res[3]
    chip_step = None
    if dist:
        theirs_a = res[4]
        chip_step = _chip_exchange_step(_pair_sum(gp, theirs_a, "pair_sum_a"))
    res = _attn_bwd(q_rot, k_full, v_nat, o_t, do_t, lse, name="attn_bwd", hosted=chip_step)
    dq_rot, dk, dv_t = res[0], res[1], res[2]
    if dist:
        gp = _chip_sum(gp, theirs_a, res[3], "chip_sum_a")
    dqp, dproj = _rowwise(_fn_rope_bwd, [dq_rot, dk, trig], [], [(1024, BF16), (256, dproj, SEG_KR[0])], tr=tr,
                          name="rope_bwd")
    dw_q_p = _mm(q_n, dqp, form="tn", name="mm_q_up_dw")
    dq_n = _mm(dqp, w_q_p, form="nt", name="mm_q_up_dx")
    dw_k_p = _mm(kv_n, dk, form="tn", name="mm_k_up_dw")
    dkv_n1 = _mm(dk, w_k_p, form="nt", name="mm_k_up_dx")
    dw_v_pt = _mm(dv_t, kv_n, form="nn", name="mm_v_up_dw")
    dkv_n2 = _mm(dv_t, w_v_pt, form="tn", name="mm_v_up_dx")
    dyc, ddtr, dconv_b, ddt_b = _rowwise(
        _fn_conv_bwd_a, [conv_y, dxs, dxs_skip, dbc, (proj,) + SEG_DT, ddt], [dt_b], [1024, (128, BF16)],
        [(1, 1024), (1, 128)], tr=tr, name="conv_bwd_a")
    dproj, dg_q, dg_kv = _rowwise(
        _fn_mla_pre_bwd, [(proj,) + SEG_QLAT, (proj,) + SEG_KVLAT, dq_n, dkv_n1, dkv_n2, ddtr], [g_q, g_kv],
        [(SEG_KR[0] - SEG_QLAT[0], dproj, SEG_QLAT[0])], [(1, 384), (1, 256)], tr=tr, name="mla_pre_bwd")
    dproj, dconv_w8 = _rowwise(
        _fn_conv_bwd_b, [dyc, ("next", dyc, 0, 1024), (proj,) + SEG_XBC, ("prev", proj) + SEG_XBC], [conv_w8],
        [(1024, dproj, SEG_XBC[0])], [(8, 1024)], tr=tr, name="conv_bwd_b")
    res = _mm(h0_b, dproj, form="tn", tn=IN_W // 2, name="mm_in_dw", hosted=_pair_fill_step(gp) if dist else None)
    dw_in_p, red_a = (res[0], res[1]) if dist else (res, None)
    big_b = _group_b_grads(dw_in_p, dw_q_p, dw_k_p, dw_v_pt, dconv_w8)
    q_b = None
    if dist:
        gp_c, gp_b = _pack_group_b(big_b)
        dh0, theirs_c, theirs_b = _mm(dproj, w_in_p, form="nt", tk=IN_W // 2, name="mm_in_dx", hosted=_merge_steps(
            [_pair_exchange_step(gp_c), _pair_exchange_step(gp_b)]))
        q_b = ((gp_c, theirs_c, _pair_sum(gp_c, theirs_c, "pair_sum_w_in")),
               (gp_b, theirs_b, _pair_sum(gp_b, theirs_b, "pair_sum_b")))
        gp = red_a
    else:
        dh0 = _mm(dproj, w_in_p, form="nt", tk=IN_W // 2, name="mm_in_dx")
    grad_x, dg_in, db_in = _rowwise(_fn_in_ln_bwd, [x, dt1, dh0], [g_in], [D_MODEL], [(1, D_MODEL)] * 2, tr=tr,
                                    name="ln_in_bwd")

    small = {
        "ln_in_g": dg_in, "ln_in_b": db_in, "conv_b": dconv_b, "dt_bias": ddt_b[:, :8],
        "a_log": da_head[:, :8] * a_head.reshape(1, 8),
        "d_skip": ddexp.reshape(8, 64).sum(axis=1).reshape(1, 8),
        "ssd_norm_g": dg_ssd, "q_norm_g": dg_q, "kv_norm_g": dg_kv,
        "ln1_g": dg1, "ln1_b": db1, "ln2_g": dg2, "ln2_b": db2, "ln3_g": dg3, "ln3_b": db3,
    }
    return loss[0, 0], grad_x, (gp, q_b), big_b, small


def _adam(w, g, m, v, name):
    shape = w.shape
    w2, m2, v2 = (t.reshape(-1, shape[-1]) for t in (w, m, v))
    if isinstance(g, tuple):
        fn = lambda ctx, wv, gv, mv, vv: (*_fn_adam(ctx, wv, gv, mv, vv), gv)
        d, mn, vn, g = _rowwise(fn, [w2, (g[0], 0, shape[-1], g[1]), m2, v2], [], [shape[-1]] * 4, tr=ROW_TILE,
                                name=name)
    else:
        d, mn, vn = _rowwise(_fn_adam, [w2, g.reshape(-1, shape[-1]), m2, v2], [], [shape[-1]] * 3, tr=ROW_TILE,
                             name=name)
    return g.reshape(shape), d.reshape(shape), mn.reshape(shape), vn.reshape(shape)


def kernel(x, mem, positions, ln_in_g, ln_in_b, w_in, conv_w, conv_b, dt_bias, a_log, d_skip, ssd_norm_g, q_norm_g, w_q_up, kv_norm_g, w_kv_up, w_mix_out, ln1_g, ln1_b, w_mem_q, w_mem_k, w_mem_v, w_mem_o, ln2_g, ln2_b, w_up, w_down, ln3_g, ln3_b, loss_target, m_ln_in_g, m_ln_in_b, m_w_in, m_conv_w, m_conv_b, m_dt_bias, m_a_log, m_d_skip, m_ssd_norm_g, m_q_norm_g, m_w_q_up, m_kv_norm_g, m_w_kv_up, m_w_mix_out, m_ln1_g, m_ln1_b, m_w_mem_q, m_w_mem_k, m_w_mem_v, m_w_mem_o, m_ln2_g, m_ln2_b, m_w_up, m_w_down, m_ln3_g, m_ln3_b, v_ln_in_g, v_ln_in_b, v_w_in, v_conv_w, v_conv_b, v_dt_bias, v_a_log, v_d_skip, v_ssd_norm_g, v_q_norm_g, v_w_q_up, v_kv_norm_g, v_w_kv_up, v_w_mix_out, v_ln1_g, v_ln1_b, v_w_mem_q, v_w_mem_k, v_w_mem_v, v_w_mem_o, v_ln2_g, v_ln2_b, v_w_up, v_w_down, v_ln3_g, v_ln3_b):
    args = dict(locals())

    wp_a = jnp.concatenate([args[n].reshape(-1, PACK_COLS).astype(BF16) for n in PACK_A_ORDER], axis=0)
    flat = [args[n].reshape(-1).astype(BF16) for n in PACK_B_ORDER[:-1]]
    flat.append(lax.bitcast_convert_type(conv_w.reshape(-1), BF16).reshape(-1))
    used = sum(f.shape[0] for f in flat)
    flat.append(jnp.zeros((PACK_B_ROWS * PACK_COLS - used,), BF16))
    wp_b = jnp.concatenate(flat).reshape(PACK_B_ROWS, PACK_COLS)
    wp_c = w_in[0].astype(BF16)

    P = {n: args[n] for n in SMALL_ORDER}
    loss, grad_x, (red_a, ((gp_c, theirs_c, pb_c), (gp_b, theirs_b, pb_b))), _, gsmall = _local_step(
        x[0], mem[0], positions[0], loss_target[0], None, P, wp_a=wp_a, wp_b=(wp_c, wp_b))

    gs = jnp.concatenate([_row(gsmall[n], PACK_COLS) for n in SMALL_ORDER] + [_row(loss, PACK_COLS)], axis=0)
    gs, got_c, got_b = _small_all_reduce(gs, _merge_steps([_chip_exchange_step(pb_c), _chip_exchange_step(pb_b)]))
    loss = gs[len(SMALL_ORDER), 0]
    red_c, red_b = _run_step(_merge_steps([_pair_fill_step(_chip_sum(gp_c, theirs_c, got_c, "chip_sum_w_in")),
                                           _pair_fill_step(_chip_sum(gp_b, theirs_b, got_b, "chip_sum_b"))]),
                             "pair_fill_b")

    grads, deltas, new_m, new_v = {}, {}, {}, {}
    for n in PACK_A_ORDER:
        grads[n], deltas[n], new_m[n], new_v[n] = _adam(args[n], (red_a, PACK_A_ROW[n]), args["m_" + n],
                                                        args["v_" + n], "adam_" + n)
    flat_t = lambda t: jnp.transpose(t, (2, 0, 1)).reshape(-1, LANES)
    unflat_t = lambda t: jnp.transpose(t.reshape(W_IN_SHARD, 1, -1), (1, 2, 0))
    g_t = flat_t(red_c[None])
    grads["w_in"] = unflat_t(g_t)
    d_t, m_t, v_t = _rowwise(_fn_adam, [flat_t(w_in), g_t, flat_t(m_w_in), flat_t(v_w_in)], [], [LANES] * 3,
                             tr=g_t.shape[0] // 2, name="adam_w_in")
    deltas["w_in"], new_m["w_in"], new_v["w_in"] = unflat_t(d_t), unflat_t(m_t), unflat_t(v_t)
    red_b = red_b.reshape(-1)
    off = 0
    for n in PACK_B_ORDER:
        sr, sc = _shard_shape(n)
        grads[n], deltas[n], new_m[n], new_v[n] = _adam(args[n], red_b[off:off + sr * sc], args["m_" + n],
                                                        args["v_" + n], "adam_" + n)
        off += sr * sc
    pack = lambda pre: jnp.concatenate([_row(args[pre + n], PACK_COLS) for n in SMALL_ORDER]
                                       + [jnp.zeros((1, PACK_COLS), F32)], axis=0)
    ds, ms, vs = _rowwise(_fn_adam, [pack(""), gs, pack("m_"), pack("v_")], [], [PACK_COLS] * 3, tr=16,
                          name="adam_small")
    for i, n in enumerate(SMALL_ORDER):
        cnt = args[n].size
        take = lambda t: t[i, :cnt].reshape(args[n].shape)
        grads[n], deltas[n], new_m[n], new_v[n] = take(gs), take(ds), take(ms), take(vs)

    order = ["ln_in_g", "ln_in_b", "w_in", "conv_w", "conv_b", "dt_bias", "a_log", "d_skip", "ssd_norm_g",
             "q_norm_g", "w_q_up", "kv_norm_g", "w_kv_up", "w_mix_out", "ln1_g", "ln1_b", "w_mem_q", "w_mem_k",
             "w_mem_v", "w_mem_o", "ln2_g", "ln2_b", "w_up", "w_down", "ln3_g", "ln3_b"]
    return (loss, grad_x[None], *[grads[n] for n in order], *[deltas[n] for n in order],
            *[new_m[n] for n in order], *[new_v[n] for n in order])
```

```python
import functools

import jax
import jax.numpy as jnp
from jax import lax
from jax.experimental import pallas as pl
from jax.experimental.pallas import tpu as pltpu

F32 = jnp.float32
BF16 = jnp.bfloat16
MESH = pl.DeviceIdType.MESH

D_MODEL = 1024
SSD_CHUNK = 128
SSD_STATE = 128
MLA_HEADS = 8
MLA_ROPE = 32
MLA_QK = 96
MLA_KV_RANK = 256
ROPE_THETA = 10000.0
MEM_HEADS = 4
MEM_HEAD_DIM = 256
LN_EPS = 1e-5
RMS_EPS = 1e-6
ALPHA = 2.0 ** 0.25
ADAM_LR = 0.001
ADAM_B1 = 0.9
ADAM_B2 = 0.999
ADAM_EPS = 1e-08
ADAM_WD = 0.01
ADAM_STEP = 10

LANES = 128
IN_W = 2560
SEG_XBC = (0, 1024)
SEG_Z = (1024, 512)
SEG_QLAT = (1536, 384)
SEG_DT = (1920, 128)
SEG_KVLAT = (2048, 256)
SEG_KR = (2304, 128)
VMEM_LIMIT = 56 * 1024 * 1024
ATTN_TILE = 512
ROW_TILE = 512
SSD_PER_STEP = 2
NEG = -1e30
MLA_SCALE = MLA_QK ** -0.5
MEM_SCALE = MEM_HEAD_DIM ** -0.5

NN = (((1,), (0,)), ((), ()))
NT = (((1,), (1,)), ((), ()))
TN = (((0,), (0,)), ((), ()))


def _dot(a, b, dims=NN):
    return lax.dot_general(a.astype(BF16), b.astype(BF16), dims, preferred_element_type=F32)


def _dot_exact(a, b):
    return lax.dot_general(a, b, NN, precision=lax.Precision.HIGHEST, preferred_element_type=F32)


def _pick(dim, pref):
    t = min(pref, dim)
    t -= t % LANES
    while t >= LANES:
        if dim % t == 0:
            return t
        t -= LANES
    return dim


def _params(sem):
    return pltpu.CompilerParams(dimension_semantics=sem, vmem_limit_bytes=VMEM_LIMIT)


def _pack_caps(wname):
    r, c, ax = BIG[wname]
    if ax == 0:
        return (r if r <= 1024 else r // N_SHARD), c
    return r, c // N_SHARD


def _pack_block(wname, br, bc):
    r, c, ax = BIG[wname]
    r0 = PACK_A_ROW[wname]
    sr = r // N_SHARD if ax == 0 else r
    if ax == 0 and br > sr:
        assert br % sr == 0 and r0 % sr == 0
        return (br // sr, sr, bc), lambda rb, cb: (rb, r0 // sr, cb)
    assert r0 % br == 0
    if ax == 0:
        per = sr // br
        return (1, br, bc), lambda rb, cb: (rb // per, r0 // br + rb % per, cb)
    per = (c // N_SHARD) // bc
    return (1, br, bc), lambda rb, cb: (cb // per, r0 // br + rb, cb % per)


def _mm(a, b, *, form, name, a_pro=None, epi=None, out_dtype=F32, tm=1024, tn=1024, tk=1024, b_pack=None,
        b_rows=None, out_pack=None, hosted=None):
    b_shape = BIG[b_pack][:2] if b_pack else b.shape
    if b_pack and form == "nt":
        b_shape = (b_rows or b_shape[0], b_shape[1])
    if form == "nn":
        (m, k), (_, n) = a.shape, b_shape
    elif form == "nt":
        (m, k), (n, _) = a.shape, b_shape
    else:
        (k, m), (_, n) = a.shape, b_shape
    if b_pack:
        rcap, ccap = _pack_caps(b_pack)
        tk, tn = (min(tk, rcap), min(tn, ccap)) if form == "nn" else (min(tk, ccap), min(tn, rcap))
    if out_pack:
        rcap, ccap = _pack_caps(out_pack[0])
        tm, tn = min(tm, rcap), min(tn, ccap)
    tm, tn, tk = _pick(m, tm), _pick(n, tn), _pick(k, tk)
    dims = {"nn": NN, "nt": NT, "tn": TN}[form]
    nk = k // tk
    direct = out_dtype == F32 and epi is None
    n_extra = (1 if epi else 0) + (1 if out_pack else 0)

    def body(a_ref, b_ref, *rest):
        o_ref = rest[n_extra]
        acc_ref = o_ref if direct else rest[-1]

        @pl.when(pl.program_id(2) == 0)
        def _():
            acc_ref[...] = jnp.zeros_like(acc_ref)

        av = a_ref[...]
        if a_pro is not None:
            av = a_pro(av)
        bv = b_ref[...]
        acc_ref[...] += _dot(av, bv.reshape(-1, bv.shape[-1]), dims).reshape(acc_ref.shape)
        if not direct:
            @pl.when(pl.program_id(2) == nk - 1)
            def _():
                val = acc_ref[...]
                if epi is not None:
                    val = epi[0](val, rest[0][...])
                o_ref[...] = val.reshape(o_ref.shape).astype(o_ref.dtype)

    if form == "tn":
        a_spec = pl.BlockSpec((tk, tm), lambda i, j, kk: (kk, i))
    else:
        a_spec = pl.BlockSpec((tm, tk), lambda i, j, kk: (i, kk))
    if b_pack:
        shape, idx = _pack_block(b_pack, *((tk, tn) if form == "nn" else (tn, tk)))
        b_spec = pl.BlockSpec(shape, (lambda i, j, kk: idx(kk, j)) if form == "nn" else (lambda i, j, kk: idx(j, kk)))
    elif form == "nt":
        b_spec = pl.BlockSpec((tn, tk), lambda i, j, kk: (j, kk))
    else:
        b_spec = pl.BlockSpec((tk, tn), lambda i, j, kk: (kk, j))
    in_specs, args = [a_spec, b_spec], [a, b]
    out_spec = pl.BlockSpec((tm, tn), lambda i, j, kk: (i, j))
    out_sds, aliases = jax.ShapeDtypeStruct((m, n), out_dtype), {}
    if epi is not None:
        in_specs.append(out_spec)
        args.append(epi[1])
    if out_pack:
        wname, buf = out_pack
        shape, idx = _pack_block(wname, tm, tn)
        out_spec = pl.BlockSpec(shape, lambda i, j, kk: idx(i, j))
        out_sds, aliases = jax.ShapeDtypeStruct(buf.shape, buf.dtype), {len(args): 0}
        in_specs.append(HBM)
        args.append(buf)
    acc_shape = out_spec.block_shape if out_pack else (tm, tn)
    res = _call_with_step(
        body, hosted, None, args, name=name, grid=(m // tm, n // tn, nk), in_specs=in_specs, out_specs=[out_spec],
        out_shape=[out_sds], sem=("parallel", "parallel", "arbitrary"), aliases=aliases,
        scratch_shapes=[] if direct else [pltpu.VMEM(acc_shape, F32)])
    return res[0] if hosted is None else res


class _Ctx:
    def __init__(self, i, n):
        self.i, self.n = i, n


def _rowwise(fn, rows, consts, row_outs, acc_outs=(), *, tr, name, n_rows=None, hosted=None):
    norm = []
    for r in rows:
        kind = "tile"
        if isinstance(r, tuple) and isinstance(r[0], str):
            kind, r = r[0], r[1:]
        if kind == "cols":
            norm.append((kind, r[0], 0, r[0].shape[0], 0))
            continue
        row0 = 0
        if isinstance(r, tuple) and len(r) == 4:
            r, row0 = r[:3], r[3]
        arr, col0, width = r if isinstance(r, tuple) else (r, 0, r.shape[1])
        assert col0 % width == 0
        norm.append((kind, arr, col0 // width, width, row0))
    n_rows = n_rows or next(a.shape[0] for k, a, _, _, _ in norm if k == "tile")
    tr = min(tr, n_rows)
    while n_rows % tr:
        tr -= 8
    n = n_rows // tr
    arrs, specs = [], []
    for kind, arr, cb, width, row0 in norm:
        if kind == "tile":
            assert row0 % tr == 0
            specs.append(pl.BlockSpec((tr, width), lambda i, cb=cb, rb=row0 // tr: (i + rb, cb)))
        elif kind == "cols":
            specs.append(pl.BlockSpec((width, tr), lambda i: (0, i)))
        elif kind == "prev":
            specs.append(pl.BlockSpec((8, width), lambda i, cb=cb: (jnp.maximum(i * (tr // 8) - 1, 0), cb)))
        else:
            specs.append(pl.BlockSpec((8, width), lambda i, cb=cb: (jnp.minimum((i + 1) * (tr // 8), n_rows // 8 - 1), cb)))
        arrs.append(arr)
    for c in consts:
        specs.append(pl.BlockSpec(c.shape, lambda i, nd=c.ndim: (0,) * nd))
        arrs.append(c)
    n_in, n_ro = len(arrs), len(row_outs)
    out_shape, out_specs, aliases = [], [], {}
    for j, ro in enumerate(row_outs):
        w, dt, col0 = (ro + (None,))[:3] if isinstance(ro, tuple) else (ro, F32, None)
        if col0 is None:
            out_shape.append(jax.ShapeDtypeStruct((n_rows, w), dt))
            out_specs.append(pl.BlockSpec((tr, w), lambda i: (i, 0)))
        else:
            assert col0 % w == 0 and dt.shape[0] == n_rows
            out_shape.append(jax.ShapeDtypeStruct(dt.shape, dt.dtype))
            out_specs.append(pl.BlockSpec((tr, w), lambda i, cb=col0 // w: (i, cb)))
            aliases[len(arrs)] = j
            arrs.append(dt)
            specs.append(HBM)
    n_all = len(arrs)
    out_shape += [jax.ShapeDtypeStruct(s, F32) for s in acc_outs]
    out_specs += [pl.BlockSpec(s, lambda i: (0, 0)) for s in acc_outs]

    def body(*refs):
        i = pl.program_id(0)
        vals = [r[...] for r in refs[:n_in]]
        outs = fn(_Ctx(i, n), *vals)
        if not isinstance(outs, (tuple, list)):
            outs = (outs,)
        o_refs = refs[n_all:]
        for o_ref, o in zip(o_refs[:n_ro], outs[:n_ro]):
            o_ref[...] = o.astype(o_ref.dtype)
        if acc_outs:
            @pl.when(i == 0)
            def _():
                for o_ref in o_refs[n_ro:]:
                    o_ref[...] = jnp.zeros_like(o_ref)

            for o_ref, o in zip(o_refs[n_ro:], outs[n_ro:]):
                o_ref[...] += jnp.broadcast_to(o, o_ref.shape)

    return _call_with_step(body, hosted, None, arrs, name=name, grid=(n,), in_specs=specs, out_specs=out_specs,
                           out_shape=out_shape, sem=("arbitrary",), aliases=aliases)


def _sum0(v):
    return jnp.sum(v, axis=0, keepdims=True)


def _mean1(v):
    return jnp.mean(v, axis=-1, keepdims=True)


def _sigmoid(v):
    return 1.0 / (1.0 + jnp.exp(-v))


def _ln_stats(t):
    xc = t - _mean1(t)
    rstd = lax.rsqrt(_mean1(xc * xc) + LN_EPS)
    return xc * rstd, rstd


def _ln_bwd(xhat, rstd, dy, g):
    dxh = dy * g
    dx = rstd * (dxh - _mean1(dxh) - xhat * _mean1(dxh * xhat))
    return dx, _sum0(dy * xhat), _sum0(dy)


def _rms_fwd(v, g):
    return v * lax.rsqrt(_mean1(v * v) + RMS_EPS) * g


def _rms_bwd(v, dy, g):
    rs = lax.rsqrt(_mean1(v * v) + RMS_EPS)
    vh = v * rs
    dyg = dy * g
    return rs * (dyg - vh * _mean1(dyg * vh)), _sum0(dy * vh)


def _lane(shape):
    return lax.broadcasted_iota(jnp.int32, shape, len(shape) - 1)


def _shift_down(u, halo, s, is_first):
    tr = u.shape[0]
    rolled = pltpu.roll(u, s, 0)
    hr = jnp.where(is_first, 0.0, pltpu.roll(halo, s, 0))
    row = lax.broadcasted_iota(jnp.int32, hr.shape, 0)
    top = jnp.where(row < s, hr, rolled[0:8])
    if tr == 8:
        return top
    return jnp.concatenate([top, rolled[8:]], axis=0)


def _shift_up(d, halo, s, is_last):
    tr = d.shape[0]
    rolled = pltpu.roll(d, tr - s, 0)
    hr = jnp.where(is_last, 0.0, pltpu.roll(halo, 8 - s, 0))
    row = lax.broadcasted_iota(jnp.int32, hr.shape, 0)
    bot = jnp.where(row >= 8 - s, hr, rolled[tr - 8:])
    if tr == 8:
        return bot
    return jnp.concatenate([rolled[:tr - 8], bot], axis=0)


def _rope_tables(trig):
    t = jnp.concatenate([trig] * (LANES // trig.shape[0]), axis=0).T
    lane = _lane(t.shape)
    first, second = (lane >= 64) & (lane < 80), (lane >= 80) & (lane < 96)
    ta = jnp.where(lane < 64, 1.0, jnp.where(first, pltpu.roll(t, 64, 1), jnp.where(second, pltpu.roll(t, 80, 1), 0.0)))
    return ta, jnp.where(second, pltpu.roll(t, 64, 1), 0.0), jnp.where(first, -pltpu.roll(t, 48, 1), 0.0)


def _rope(v, ta, tb, tc):
    return v * ta + pltpu.roll(v, 16, 1) * tb + pltpu.roll(v, LANES - 16, 1) * tc


def _rope_bwd(d, ta, tb, tc):
    return d * ta + pltpu.roll(d * tb, LANES - 16, 1) + pltpu.roll(d * tc, 16, 1)


def _ssd_common(dtv, a_row):
    L = SSD_CHUNK
    a = dtv * a_row
    r = lax.broadcasted_iota(jnp.int32, (L, L), 0)
    c = lax.broadcasted_iota(jnp.int32, (L, L), 1)
    tril = r >= c
    cs = _dot_exact(tril.astype(F32), a)
    cs_t = cs.T
    cs_last = cs[L - 1:L, :]
    return dict(a=a, tril=tril, cs=cs, cs_t=cs_t, ecs=jnp.exp(cs), dte=jnp.exp(cs_last - cs),
                elast=jnp.exp(cs_last))


def _pair_sel(v, h0, lo):
    return jnp.where(lo, v[:, h0:h0 + 1], v[:, h0 + 1:h0 + 2])


def _ssd_pair(cm, h0, cb, xp, dtv, bmat, cmat, hp, lo):
    x = xp * _pair_sel(dtv, h0, lo)
    lam0 = jnp.exp(jnp.where(cm["tril"], cm["cs"][:, h0:h0 + 1] - cm["cs_t"][h0:h0 + 1, :], NEG))
    lam1 = jnp.exp(jnp.where(cm["tril"], cm["cs"][:, h0 + 1:h0 + 2] - cm["cs_t"][h0 + 1:h0 + 2, :], NEG))
    m0, m1 = cb * lam0, cb * lam1
    ydiag = jnp.where(lo, _dot(m0, x), _dot(m1, x))
    ecs_p = _pair_sel(cm["ecs"], h0, lo)
    dte_p = _pair_sel(cm["dte"], h0, lo)
    yoff = _dot(cmat, hp, NT) * ecs_p
    xd = x * dte_p
    st = _dot(xd, bmat, TN)
    rlo = lax.broadcasted_iota(jnp.int32, (LANES, SSD_STATE), 0) < 64
    decay = jnp.where(rlo, cm["elast"][:, h0:h0 + 1], cm["elast"][:, h0 + 1:h0 + 2])
    h_next = hp * decay + st
    return dict(x=x, lam0=lam0, lam1=lam1, m0=m0, m1=m1, y=ydiag + yoff, yoff=yoff, ecs_p=ecs_p, dte_p=dte_p,
                xd=xd, decay=decay, h_next=h_next)


def _ssd_fwd(xbc, dt, a_row, *, name):
    S = xbc.shape[0]
    L = SSD_CHUNK
    nc = S // L
    per = SSD_PER_STEP if nc % SSD_PER_STEP == 0 else 1
    G = per * L

    def body(xs_ref, bm_ref, cm_ref, dt_ref, a_ref, y_ref, hs_ref, h_scr):
        @pl.when(pl.program_id(0) == 0)
        def _():
            h_scr[...] = jnp.zeros_like(h_scr)

        lo = _lane((L, LANES)) < 64
        for sub in range(per):
            rows = slice(sub * L, (sub + 1) * L)
            dtv = dt_ref[rows, :]
            cm = _ssd_common(dtv, a_ref[...])
            ys = []
            for g in range(2):
                bmat = bm_ref[rows, g * 128:(g + 1) * 128]
                cmat = cm_ref[rows, g * 128:(g + 1) * 128]
                cb = _dot(cmat, bmat, NT)
                for pr in range(2):
                    p4 = 2 * g + pr
                    hp = h_scr[p4]
                    hs_ref[sub, p4 * 128:(p4 + 1) * 128, :] = hp
                    t = _ssd_pair(cm, 2 * p4, cb, xs_ref[rows, p4 * 128:(p4 + 1) * 128], dtv, bmat, cmat, hp, lo)
                    ys.append(t["y"])
                    h_scr[p4] = t["h_next"]
            y_ref[rows, :] = jnp.concatenate(ys, axis=1)

    return pl.pallas_call(
        body, name=name, grid=(nc // per,),
        in_specs=[pl.BlockSpec((G, 512), lambda c: (c, 0)), pl.BlockSpec((G, 256), lambda c: (c, 2)),
                  pl.BlockSpec((G, 256), lambda c: (c, 3)), pl.BlockSpec((G, 128), lambda c: (c, 0)),
                  pl.BlockSpec((1, 128), lambda c: (0, 0))],
        out_specs=[pl.BlockSpec((G, 512), lambda c: (c, 0)), pl.BlockSpec((per, 512, 128), lambda c: (c, 0, 0))],
        out_shape=[jax.ShapeDtypeStruct((S, 512), F32), jax.ShapeDtypeStruct((nc, 512, 128), F32)],
        scratch_shapes=[pltpu.VMEM((4, 128, 128), F32)],
        compiler_params=_params(("arbitrary",)),
    )(xbc, xbc, xbc, dt, a_row)


def _ssd_bwd(xbc, dt, a_row, hs, dy, *, name, hosted=None):
    S = xbc.shape[0]
    L = SSD_CHUNK
    nc = S // L
    per = SSD_PER_STEP if nc % SSD_PER_STEP == 0 else 1
    G = per * L

    def body(xs_ref, bm_ref, cm_ref, dt_ref, a_ref, hs_ref, dy_ref, dxs_ref, dbc_ref, ddt_ref, da_ref, g_scr):
        @pl.when(pl.program_id(0) == 0)
        def _():
            g_scr[...] = jnp.zeros_like(g_scr)
            da_ref[...] = jnp.zeros_like(da_ref)

        for sub in reversed(range(per)):
            rows = pl.ds(sub * L, L)
            chunk(xs_ref.at[rows, :], bm_ref.at[rows, :], cm_ref.at[rows, :], dt_ref.at[rows, :], a_ref,
                  hs_ref.at[pl.ds(sub, 1)], dy_ref.at[rows, :], dxs_ref.at[rows, :], dbc_ref.at[rows, :],
                  ddt_ref.at[rows, :], da_ref, g_scr)

    def chunk(xs_ref, bm_ref, cm_ref, dt_ref, a_ref, hs_ref, dy_ref, dxs_ref, dbc_ref, ddt_ref, da_ref, g_scr):
        dtv = dt_ref[...]
        a_row_v = a_ref[...]
        cm = _ssd_common(dtv, a_row_v)
        lo = _lane((L, LANES)) < 64
        lane_row = _lane((1, LANES))
        ri = lax.broadcasted_iota(jnp.int32, (L, L), 0)
        ci = lax.broadcasted_iota(jnp.int32, (L, L), 1)
        triu = (ri <= ci).astype(F32)
        stril = ri > ci

        def halves(v, mask):
            return (jnp.sum(jnp.where(mask, v, 0.0), axis=1, keepdims=True),
                    jnp.sum(jnp.where(mask, 0.0, v), axis=1, keepdims=True))

        i_all = jnp.zeros((L, LANES), F32)
        yo_all = jnp.zeros((L, LANES), F32)
        w_all = jnp.zeros((L, LANES), F32)
        ddt_x = jnp.zeros((L, LANES), F32)
        e_row = jnp.zeros((1, LANES), F32)
        rlo = lax.broadcasted_iota(jnp.int32, (LANES, SSD_STATE), 0) < 64
        dxs, dbs, dcs = [], [], []
        for g in range(2):
            bmat = bm_ref[:, g * 128:(g + 1) * 128]
            cmat = cm_ref[:, g * 128:(g + 1) * 128]
            cb = _dot(cmat, bmat, NT)
            dcb = jnp.zeros((L, L), F32)
            db = jnp.zeros((L, SSD_STATE), F32)
            dc = jnp.zeros((L, SSD_STATE), F32)
            for pr in range(2):
                p4 = 2 * g + pr
                h0 = 2 * p4
                hp = hs_ref[0, p4 * 128:(p4 + 1) * 128, :]
                xp = xs_ref[:, p4 * 128:(p4 + 1) * 128]
                t = _ssd_pair(cm, h0, cb, xp, dtv, bmat, cmat, hp, lo)
                gst = g_scr[p4]
                dyp = dy_ref[:, p4 * 128:(p4 + 1) * 128]
                dy0 = jnp.where(lo, dyp, 0.0)
                dy1 = dyp - dy0
                bg = _dot(bmat, gst, NT)
                dx = _dot(t["m0"], dy0, TN) + _dot(t["m1"], dy1, TN) + bg * t["dte_p"]
                dm0, dm1 = _dot(dy0, t["x"], NT), _dot(dy1, t["x"], NT)
                dcb = dcb + dm0 * t["lam0"] + dm1 * t["lam1"]
                dye = dyp * t["ecs_p"]
                dc = dc + _dot(dye, hp)
                db = db + _dot(t["xd"], gst)
                i0 = jnp.sum(jnp.where(stril, _dot(triu, dm0 * t["m0"]), 0.0), axis=1, keepdims=True)
                i1 = jnp.sum(jnp.where(stril, _dot(triu, dm1 * t["m1"]), 0.0), axis=1, keepdims=True)
                yo0, yo1 = halves(dyp * t["yoff"], lo)
                w0, w1 = halves(t["xd"] * bg, lo)
                gh = gst * (hp * t["decay"])
                e0 = _sum0(jnp.sum(jnp.where(rlo, gh, 0.0), axis=1, keepdims=True))
                e1 = _sum0(jnp.sum(jnp.where(rlo, 0.0, gh), axis=1, keepdims=True))
                x0, x1 = halves(dx * xp, lo)
                oh0 = (lane_row == h0).astype(F32)
                oh1 = (lane_row == h0 + 1).astype(F32)
                i_all = i_all + i0 * oh0 + i1 * oh1
                yo_all = yo_all + yo0 * oh0 + yo1 * oh1
                w_all = w_all + w0 * oh0 + w1 * oh1
                e_row = e_row + e0 * oh0 + e1 * oh1
                ddt_x = ddt_x + x0 * oh0 + x1 * oh1
                dxs.append(dx * _pair_sel(dtv, h0, lo))
                g_scr[p4] = gst * t["decay"] + _dot(dye, cmat, TN)
            dbs.append(db + _dot(dcb, cmat, TN))
            dcs.append(dc + _dot(dcb, bmat))
        da = i_all + _dot_exact(triu, yo_all) + _dot_exact(stril.astype(F32), w_all) + e_row
        ddt_ref[...] = da * a_row_v + ddt_x
        da_ref[...] += _sum0(da * dtv)
        dxs_ref[...] = jnp.concatenate(dxs, axis=1)
        dbc_ref[...] = jnp.concatenate(dbs + dcs, axis=1)

    rev = lambda c: nc // per - 1 - c
    return _call_with_step(
        body, hosted, None, (xbc, xbc, xbc, dt, a_row, hs, dy), name=name, grid=(nc // per,),
        in_specs=[pl.BlockSpec((G, 512), lambda c: (rev(c), 0)), pl.BlockSpec((G, 256), lambda c: (rev(c), 2)),
                  pl.BlockSpec((G, 256), lambda c: (rev(c), 3)), pl.BlockSpec((G, 128), lambda c: (rev(c), 0)),
                  pl.BlockSpec((1, 128), lambda c: (0, 0)), pl.BlockSpec((per, 512, 128), lambda c: (rev(c), 0, 0)),
                  pl.BlockSpec((G, 512), lambda c: (rev(c), 0))],
        out_specs=[pl.BlockSpec((G, 512), lambda c: (rev(c), 0)), pl.BlockSpec((G, 512), lambda c: (rev(c), 0)),
                   pl.BlockSpec((G, 128), lambda c: (rev(c), 0)), pl.BlockSpec((1, 128), lambda c: (0, 0))],
        out_shape=[jax.ShapeDtypeStruct((S, 512), F32), jax.ShapeDtypeStruct((S, 512), F32),
                   jax.ShapeDtypeStruct((S, 128), F32), jax.ShapeDtypeStruct((1, 128), F32)],
        sem=("arbitrary",), scratch_shapes=[pltpu.VMEM((4, 128, 128), F32)])


HBM = pl.BlockSpec(memory_space=pl.ANY)


class _Step:
    def __init__(self, inputs, out_shapes, n_sems, start, finish, mid=None):
        self.inputs, self.out_shapes, self.n_sems = inputs, out_shapes, n_sems
        self.start, self.finish, self.mid = start, finish, mid
        self.alias = []


class _Shifted:
    def __init__(self, ref, off):
        self.ref, self.off = ref, off

    @property
    def at(self):
        return self

    def __getitem__(self, j):
        return self.ref.at[self.off + j]


def _merge_steps(steps):
    offs = [sum(s.n_sems for s in steps[:i]) for i in range(len(steps) + 1)]
    i_offs = [sum(len(s.inputs) for s in steps[:i]) for i in range(len(steps))]
    o_offs = [sum(len(s.out_shapes) for s in steps[:i]) for i in range(len(steps))]

    def phase(which):
        def run(ins, outs, sems):
            for s, off, i0, o0 in zip(steps, offs, i_offs, o_offs):
                fn = getattr(s, which)
                if fn is not None:
                    fn(ins[i0:i0 + len(s.inputs)], outs[o0:o0 + len(s.out_shapes)],
                       [_Shifted(sems[0], off), _Shifted(sems[1], off)])
        return run

    merged = _Step([a for s in steps for a in s.inputs], [o for s in steps for o in s.out_shapes], offs[-1],
                   phase("start"), phase("finish"), phase("mid") if any(s.mid for s in steps) else None)
    merged.alias = [(i0 + a, o0 + b) for s, i0, o0 in zip(steps, i_offs, o_offs) for a, b in s.alias]
    return merged


def _place():
    x, y, c = lax.axis_index("x"), lax.axis_index("y"), lax.axis_index("c")
    chips = [(1 - x, y), (x, 1 - y), (1 - x, 1 - y)]
    return x, y, c, chips


def _mesh_pos():
    return 2 * lax.axis_index("x") + lax.axis_index("y"), lax.axis_index("c")


def _chunks(rows, tile):
    return next(n for n in (4, 3, 2, 1) if rows % (n * tile) == 0)


def _remote(src, dst, sems, j, to):
    return pltpu.make_async_remote_copy(src_ref=src, dst_ref=dst, send_sem=sems[0].at[j], recv_sem=sems[1].at[j],
                                        device_id=to, device_id_type=MESH)


def _own_slot(wp):
    return lax.dynamic_update_slice(lax.empty((N_SHARD,) + wp.shape, wp.dtype), wp[None], (_mesh_pos()[0], 0, 0))


def _gather_step(buf):
    _, R, C = buf.shape
    H = R // 2
    nq = _chunks(H, 16)
    CH = H // nq

    def copies(ins, outs, sems):
        x, y, c, chips = _place()
        sib, me = (x, y, 1 - c), 2 * x + y
        w_ref, out_ref = ins[0], outs[0]

        def piece(k, hc, q):
            return out_ref.at[k, pl.ds(hc * H + q * CH, CH), :]

        sends, landed, fwds, fwd_landed = [], [], [], []
        for q in range(nq):
            for j, (px, py) in enumerate(chips):
                k = 2 * px + py
                sends.append(_remote(w_ref.at[me, pl.ds(c * H + q * CH, CH), :], piece(me, c, q), sems, j * nq + q,
                                     (px, py, c)))
                landed.append(_remote(piece(k, c, q), piece(k, c, q), sems, j * nq + q, (px, py, c)))
                fwds.append(_remote(piece(k, c, q), piece(k, c, q), sems, (3 + j) * nq + q, sib))
                fwd_landed.append(_remote(piece(k, 1 - c, q), piece(k, 1 - c, q), sems, (3 + j) * nq + q, sib))
        return sends, landed, fwds, fwd_landed

    def start(ins, outs, sems):
        for cp in copies(ins, outs, sems)[0]:
            cp.start()

    def mid(ins, outs, sems):
        _, landed, fwds, _ = copies(ins, outs, sems)
        for arrived, onward in zip(landed, fwds):
            arrived.wait_recv()
            onward.start()

    def finish(ins, outs, sems):
        sends, _, fwds, fwd_landed = copies(ins, outs, sems)
        for cp in fwd_landed:
            cp.wait_recv()
        for cp in sends + fwds:
            cp.wait_send()

    step = _Step([buf], [jax.ShapeDtypeStruct(buf.shape, buf.dtype)], 6 * nq, start, finish, mid)
    step.alias = [(0, 0)]
    return step


def _pair_exchange_step(gp):
    n, R, C = gp.shape
    H = R // 2
    nq = _chunks(H, 8)
    CH = H // nq

    def copies(ins, outs, sems):
        x, y, c, _ = _place()
        return [_remote(ins[0].at[k, pl.ds((1 - c) * H + q * CH, CH), :], outs[0].at[k, pl.ds(q * CH, CH), :], sems,
                        k * nq + q, (x, y, 1 - c)) for k in range(n) for q in range(nq)]

    def start(ins, outs, sems):
        for cp in copies(ins, outs, sems):
            cp.start()

    def finish(ins, outs, sems):
        for cp in copies(ins, outs, sems):
            cp.wait()

    return _Step([gp], [jax.ShapeDtypeStruct((n, H, C), gp.dtype)], n * nq, start, finish)


def _chip_exchange_step(pb):
    n, H, C = pb.shape
    nq = _chunks(H, 16)
    CH = H // nq

    def copies(ins, outs, sems):
        x, y, c, chips = _place()
        return [_remote(ins[0].at[2 * px + py, pl.ds(q * CH, CH), :], outs[0].at[j, pl.ds(q * CH, CH), :], sems,
                        j * nq + q, (px, py, c)) for q in range(nq) for j, (px, py) in enumerate(chips)]

    def start(ins, outs, sems):
        for cp in copies(ins, outs, sems):
            cp.start()

    def finish(ins, outs, sems):
        for cp in copies(ins, outs, sems):
            cp.wait()

    return _Step([pb], [jax.ShapeDtypeStruct((3, H, C), pb.dtype)], 3 * nq, start, finish)


def _pair_fill_step(red):
    R, C = red.shape
    H = R // 2
    nq = _chunks(H, 8)
    CH = H // nq

    def copies(ins, outs, sems):
        x, y, c, _ = _place()
        return [_remote(ins[0].at[pl.ds(c * H + j * CH, CH), :], outs[0].at[pl.ds(c * H + j * CH, CH), :], sems, j,
                        (x, y, 1 - c)) for j in range(nq)]

    def start(ins, outs, sems):
        for cp in copies(ins, outs, sems):
            cp.start()

    def finish(ins, outs, sems):
        for cp in copies(ins, outs, sems):
            cp.wait()

    step = _Step([red], [jax.ShapeDtypeStruct((R, C), red.dtype)], nq, start, finish)
    step.alias = [(0, 0)]
    return step


def _sem_scratch(step):
    return [pltpu.SemaphoreType.DMA((step.n_sems,)), pltpu.SemaphoreType.DMA((step.n_sems,))]


def _run_step(step, name):
    ni, no = len(step.inputs), len(step.out_shapes)

    def body(*refs):
        ins, outs, sems = refs[:ni], refs[ni:ni + no], refs[ni + no:]
        step.start(ins, outs, sems)
        if step.mid is not None:
            step.mid(ins, outs, sems)
        step.finish(ins, outs, sems)

    return pl.pallas_call(body, name=name, in_specs=[HBM] * ni, out_specs=[HBM] * no, out_shape=step.out_shapes,
                          input_output_aliases=dict(step.alias),
                          scratch_shapes=_sem_scratch(step))(*step.inputs)


def _grid_flags(grid):
    ids = [pl.program_id(d) for d in range(len(grid))]
    first = functools.reduce(lambda a, b: a & b, [i == 0 for i in ids])
    last = functools.reduce(lambda a, b: a & b, [i == n - 1 for i, n in zip(ids, grid)])
    return first, last, last


def _call_with_step(core, step, flags, args, *, name, grid, in_specs, out_specs, out_shape, sem, scratch_shapes=(),
                    aliases=None):
    aliases = aliases or {}
    if step is None:
        return pl.pallas_call(core, name=name, grid=grid, in_specs=in_specs, out_specs=out_specs,
                              out_shape=out_shape, scratch_shapes=list(scratch_shapes),
                              input_output_aliases=aliases, compiler_params=_params(sem))(*args)
    n_in, n_out, n_scr = len(in_specs), len(out_specs), len(scratch_shapes)
    si, so = len(step.inputs), len(step.out_shapes)
    flags = flags or (lambda: _grid_flags(grid))
    aliases = {**aliases, **{n_in + a: n_out + b for a, b in step.alias}}

    def body(*refs):
        ins, s_ins = refs[:n_in], refs[n_in:n_in + si]
        outs = refs[n_in + si:n_in + si + n_out]
        s_outs = refs[n_in + si + n_out:n_in + si + n_out + so]
        scr = refs[n_in + si + n_out + so:n_in + si + n_out + so + n_scr]
        sems = refs[n_in + si + n_out + so + n_scr:]
        first, middle, last = flags()

        @pl.when(first)
        def _():
            step.start(s_ins, s_outs, sems)

        if step.mid is not None:
            @pl.when(middle)
            def _():
                step.mid(s_ins, s_outs, sems)

        core(*ins, *outs, *scr)

        @pl.when(last)
        def _():
            step.finish(s_ins, s_outs, sems)

    return pl.pallas_call(
        body, name=name, grid=grid, in_specs=list(in_specs) + [HBM] * si, out_specs=list(out_specs) + [HBM] * so,
        out_shape=list(out_shape) + list(step.out_shapes), scratch_shapes=list(scratch_shapes) + _sem_scratch(step),
        input_output_aliases=aliases, compiler_params=_params(("arbitrary",) * len(grid)))(*args, *step.inputs)


def _attn_flags(nq):
    h, qi = pl.program_id(0), pl.program_id(1)
    return ((h == 0) & (qi == 0), (h == MLA_HEADS - 1) & (qi == 0), (h == MLA_HEADS - 1) & (qi == nq - 1))


def _att_mask(s_t, q0, k0):
    krow = k0 + lax.broadcasted_iota(jnp.int32, s_t.shape, 0)
    qcol = q0 + lax.broadcasted_iota(jnp.int32, s_t.shape, 1)
    return jnp.where(krow <= qcol, s_t, NEG)


def _loop_blocks(lo, hi, step, carry):
    n = hi - lo

    def four(i, c):
        kb = lo + 4 * i
        return step(kb + 3, step(kb + 2, step(kb + 1, step(kb, c))))

    carry = lax.fori_loop(0, n // 4, four, carry)
    base = lo + 4 * (n // 4)
    carry = lax.cond(n % 4 >= 2, lambda c: step(base + 1, step(base, c)), lambda c: c, carry)
    return lax.cond(n % 2 == 1, lambda c: step(hi - 1, c), lambda c: c, carry)


def _rows(ref, blk, t):
    return ref[pl.ds(pl.multiple_of(blk * t, t), t), :]


def _cols(ref, blk, t):
    return ref[:, pl.ds(pl.multiple_of(blk * t, t), t)]


def _attn_fwd(q, k, v_t, *, name, hosted=None):
    S = q.shape[0]
    t = min(ATTN_TILE, S)
    nq = S // t

    def body(q_ref, k_ref, vt_ref, o_ref, lse_ref):
        qi = pl.program_id(1)
        qv = q_ref[...]

        def absorb(kb, carry, masked):
            m, l, acc = carry
            s_t = lax.dot_general(_rows(k_ref, kb, t), qv, NT, preferred_element_type=F32)
            if masked:
                s_t = _att_mask(s_t, qi * t, kb * t)
            m_new = jnp.maximum(m, jnp.max(s_t, axis=0, keepdims=True))
            p_t = jnp.exp(s_t - m_new)
            corr = jnp.exp(m - m_new)
            return (m_new, corr * l + jnp.sum(p_t, axis=0, keepdims=True),
                    corr * acc + lax.dot_general(_cols(vt_ref, kb, t), p_t.astype(BF16), NN,
                                                 preferred_element_type=F32))

        init = (jnp.full((1, t), NEG, F32), jnp.zeros((1, t), F32), jnp.zeros((LANES, t), F32))
        carry = _loop_blocks(0, qi, lambda kb, c: absorb(kb, c, False), init)
        m, l, acc = absorb(qi, carry, True)
        o_ref[...] = acc / l
        lse_ref[0] = m + jnp.log(l)

    return _call_with_step(
        body, hosted, lambda: _attn_flags(nq), (q, k, v_t), name=name, grid=(MLA_HEADS, nq),
        in_specs=[pl.BlockSpec((t, LANES), lambda h, qi: (qi, h)),
                  pl.BlockSpec((S, LANES), lambda h, qi: (0, h)),
                  pl.BlockSpec((LANES, S), lambda h, qi: (h, 0))],
        out_specs=[pl.BlockSpec((LANES, t), lambda h, qi: (h, qi)),
                   pl.BlockSpec((1, 1, t), lambda h, qi: (h, 0, qi))],
        out_shape=[jax.ShapeDtypeStruct((MLA_HEADS * LANES, S), F32), jax.ShapeDtypeStruct((MLA_HEADS, 1, S), F32)],
        sem=("parallel", "arbitrary"))


def _attn_bwd(q, k, v, o_t, do_t, lse, *, name, hosted=None):
    S = q.shape[0]
    t = min(ATTN_TILE, S)
    nq = S // t

    def body(q_ref, k_ref, v_ref, o_ref, do_ref, lse_ref, dq_ref, dk_ref, dv_ref, dkt_scr):
        qi = pl.program_id(1)

        @pl.when(qi == 0)
        def _():
            dkt_scr[...] = jnp.zeros_like(dkt_scr)
            dv_ref[...] = jnp.zeros_like(dv_ref)

        qv = q_ref[...]
        q_t = qv.T
        dov = do_ref[...]
        delta = jnp.sum(dov * o_ref[...], axis=0, keepdims=True)
        dob = dov.astype(BF16)
        lse_v = lse_ref[0]

        def step(kb, acc, masked):
            kt = _rows(k_ref, kb, t)
            s_t = lax.dot_general(kt, qv, NT, preferred_element_type=F32)
            if masked:
                s_t = _att_mask(s_t, qi * t, kb * t)
            p_t = jnp.exp(s_t - lse_v)
            dp_t = lax.dot_general(_rows(v_ref, kb, t), dob, NN, preferred_element_type=F32)
            ds_t = (p_t * (dp_t - delta)).astype(BF16)
            keys = pl.ds(pl.multiple_of(kb * t, t), t)
            dv_ref[:, keys] += lax.dot_general(dob, p_t.astype(BF16), NT, preferred_element_type=F32)
            dkt_scr[:, keys] += lax.dot_general(q_t, ds_t, NT, preferred_element_type=F32)
            return acc + lax.dot_general(kt, ds_t, TN, preferred_element_type=F32)

        acc = _loop_blocks(0, qi, lambda kb, c: step(kb, c, False), jnp.zeros((LANES, t), F32))
        dq_ref[...] = step(qi, acc, True).T

        @pl.when(qi == nq - 1)
        def _():
            dk_ref[...] = dkt_scr[...].T

    tile = pl.BlockSpec((t, LANES), lambda h, qi: (qi, h))
    tile_t = pl.BlockSpec((LANES, t), lambda h, qi: (h, qi))
    stat = pl.BlockSpec((1, 1, t), lambda h, qi: (h, 0, qi))
    seq = pl.BlockSpec((S, LANES), lambda h, qi: (0, h))
    seq_t = pl.BlockSpec((LANES, S), lambda h, qi: (h, 0))
    return _call_with_step(
        body, hosted, lambda: _attn_flags(nq), (q, k, v, o_t, do_t, lse), name=name, grid=(MLA_HEADS, nq),
        in_specs=[tile, seq, seq, tile_t, tile_t, stat],
        out_specs=[tile, seq, seq_t],
        out_shape=[jax.ShapeDtypeStruct((S, MLA_HEADS * LANES), F32), jax.ShapeDtypeStruct((S, MLA_HEADS * LANES), F32),
                   jax.ShapeDtypeStruct((MLA_HEADS * LANES, S), F32)],
        sem=("parallel", "arbitrary"), scratch_shapes=[pltpu.VMEM((LANES, S), F32)])


def _fn_ln(ctx, x, g, b):
    xhat, _ = _ln_stats(x)
    y = xhat * g + b
    return y, y


def _fn_conv_fwd(ctx, u, up, dtr, w8, cb, dtb):
    first = ctx.i == 0
    y = u * w8[3:4] + cb
    for s in (1, 2, 3):
        y = y + _shift_down(u, up, s, first) * w8[3 - s:4 - s]
    act = y * _sigmoid(y)
    v = dtr + dtb
    e = jnp.exp(-jnp.abs(v))
    one_p = 1.0 + e
    log1p = jnp.where(one_p == 1.0, e, jnp.log(one_p) * e / (one_p - 1.0))
    return y, act, jnp.maximum(v, 0.0) + log1p


def _fn_ssd_post(ctx, y, xs, z, dexp, g):
    yg = (y + xs * dexp) * (z * _sigmoid(z))
    outs = []
    for k in range(2):
        v = yg[:, 256 * k:256 * (k + 1)]
        outs.append(v * lax.rsqrt(_mean1(v * v) + RMS_EPS))
    return (jnp.concatenate(outs, axis=1) * g,)


def _fn_ssd_post_bwd(ctx, dyn, y, xs, z, dexp, g):
    yt = y + xs * dexp
    sig = _sigmoid(z)
    sz = z * sig
    yg = yt * sz
    dyh = dyn * g
    yh, dyg = [], []
    for k in range(2):
        sl = slice(256 * k, 256 * (k + 1))
        v = yg[:, sl]
        rs = lax.rsqrt(_mean1(v * v) + RMS_EPS)
        vh = v * rs
        yh.append(vh)
        dyg.append(rs * (dyh[:, sl] - vh * _mean1(dyh[:, sl] * vh)))
    yh = jnp.concatenate(yh, axis=1)
    dyg = jnp.concatenate(dyg, axis=1)
    dyt = dyg * sz
    dz = dyg * yt * (sig * (1.0 + z * (1.0 - sig)))
    return dyt, dz, dyt * dexp, _sum0(dyt * xs), _sum0(dyn * yh)


def _fn_mla_pre(ctx, ql, kvl, gq, gkv):
    return _rms_fwd(ql, gq), _rms_fwd(kvl, gkv)


def _fn_mla_pre_bwd(ctx, ql, kvl, dqn, dkvn_k, dkvn_v, ddtr, gq, gkv):
    dql, dgq = _rms_bwd(ql, dqn, gq)
    dkvl, dgkv = _rms_bwd(kvl, dkvn_k + dkvn_v, gkv)
    return jnp.concatenate([dql, ddtr.astype(F32), dkvl], axis=1), dgq, dgkv


def _fn_rope(ctx, qp, kn, kr, trig):
    ta, tb, tc = _rope_tables(trig)
    kpe = _rope(kr, ta, tb, tc)
    qs, ks = [], []
    for h in range(MLA_HEADS):
        sl = slice(128 * h, 128 * (h + 1))
        qs.append(_rope(qp[:, sl], ta, tb, tc) * MLA_SCALE)
        ks.append(kn[:, sl] + kpe)
    return jnp.concatenate(qs, axis=1), jnp.concatenate(ks, axis=1)


def _fn_rope_bwd(ctx, dq, dk, trig):
    ta, tb, tc = _rope_tables(trig)
    qs = []
    ksum = jnp.zeros_like(ta)
    for h in range(MLA_HEADS):
        sl = slice(128 * h, 128 * (h + 1))
        qs.append(_rope_bwd(dq[:, sl] * MLA_SCALE, ta, tb, tc))
        ksum = ksum + dk[:, sl]
    lane = _lane(ksum.shape)
    dkr = jnp.where((lane >= 64) & (lane < 96), _rope_bwd(ksum, ta, tb, tc), 0.0)
    return jnp.concatenate(qs, axis=1), jnp.concatenate([dkr, jnp.zeros_like(dkr)], axis=1)


def _mem_probs(qh, kh):
    s = _dot(qh, kh, NT) * MEM_SCALE
    p = jnp.exp(s - jnp.max(s, axis=1, keepdims=True))
    return p / jnp.sum(p, axis=1, keepdims=True)


def _fn_mem_fwd(ctx, q, km, vm):
    outs = []
    for h in range(MEM_HEADS):
        sl = slice(256 * h, 256 * (h + 1))
        outs.append(_dot(_mem_probs(q[:, sl], km[:, sl]), vm[:, sl]))
    return (jnp.concatenate(outs, axis=1),)


def _fn_mem_bwd(ctx, q, do, km, vm):
    dqs, dks, dvs = [], [], []
    for h in range(MEM_HEADS):
        sl = slice(256 * h, 256 * (h + 1))
        p = _mem_probs(q[:, sl], km[:, sl])
        dvs.append(_dot(p, do[:, sl], TN))
        dp = _dot(do[:, sl], vm[:, sl], NT)
        ds = p * (dp - jnp.sum(dp * p, axis=1, keepdims=True)) * MEM_SCALE
        dqs.append(_dot(ds, km[:, sl]))
        dks.append(_dot(ds, q[:, sl], TN))
    return jnp.concatenate(dqs, axis=1), jnp.concatenate(dks, axis=1), jnp.concatenate(dvs, axis=1)


def _fn_res_ln(ctx, h, r, g, b):
    xhat, _ = _ln_stats(ALPHA * h + r)
    y = xhat * g + b
    return y, y


def _fn_res_ln_bwd(ctx, h, r, d1, d2, g):
    xhat, rstd = _ln_stats(ALPHA * h + r)
    return _ln_bwd(xhat, rstd, ALPHA * d1 + d2, g)


def _fn_res2_ln(ctx, h, r1, r2, g, b):
    xhat, _ = _ln_stats(ALPHA * h + (r1 + r2))
    return (xhat * g + b,)


def _fn_res2_ln_bwd(ctx, h, r1, r2, d1, d2, g):
    xhat, rstd = _ln_stats(ALPHA * h + (r1 + r2))
    return _ln_bwd(xhat, rstd, ALPHA * d1 + d2, g)


def _fn_in_ln_bwd(ctx, x, d1, d2, g):
    xhat, rstd = _ln_stats(x)
    return _ln_bwd(xhat, rstd, ALPHA * d1 + d2, g)


def _fn_final(ctx, h2, ff, tgt, g, b):
    xhat, rstd = _ln_stats(ALPHA * h2 + ff)
    e = xhat * g + b - tgt
    loss = 0.5 * _sum0(jnp.sum(e * e, axis=1, keepdims=True)) / D_MODEL
    dx, dg, db = _ln_bwd(xhat, rstd, e / D_MODEL, g)
    return dx, dx, dg, db, loss


def _epi_du(da, u):
    return da * 2.0 * jnp.maximum(u.astype(F32), 0.0)


def _relu2(u):
    r = jnp.maximum(u.astype(F32), 0.0)
    return r * r


def _fn_conv_bwd_a(ctx, y, dxs1, dxs2, dbc, dtr, ddt, dtb):
    sig = _sigmoid(y)
    dact = jnp.concatenate([dxs1 + dxs2, dbc], axis=1)
    dyc = dact * (sig * (1.0 + y * (1.0 - sig)))
    ddtr = ddt * _sigmoid(dtr + dtb)
    return dyc, ddtr, _sum0(dyc), _sum0(ddtr)


def _fn_conv_bwd_b(ctx, d, dn, u, up, w8):
    first, last = ctx.i == 0, ctx.i == ctx.n - 1
    du = d * w8[3:4]
    row = lax.broadcasted_iota(jnp.int32, w8.shape, 0)
    dw = jnp.where(row == 3, _sum0(d * u), 0.0)
    for s in (1, 2, 3):
        du = du + _shift_up(d, dn, s, last) * w8[3 - s:4 - s]
        dw = dw + jnp.where(row == 3 - s, _sum0(d * _shift_down(u, up, s, first)), 0.0)
    return du, dw


def _fn_adam(ctx, w, g, m, v):
    m = ADAM_B1 * m + (1.0 - ADAM_B1) * g
    v = ADAM_B2 * v + (1.0 - ADAM_B2) * (g * g)
    m_hat = m / (1.0 - ADAM_B1 ** ADAM_STEP)
    v_hat = v / (1.0 - ADAM_B2 ** ADAM_STEP)
    return -ADAM_LR * (m_hat / (jnp.sqrt(v_hat) + ADAM_EPS) + ADAM_WD * w), m, v


def _z(r, c, dt):
    return jnp.zeros((r, c), dt)


W_IN_SHARD = 554
W_IN_GROUPS = [(0, 512, 1024), (512, 1536, 0), (1536, 1544, 1920), (1544, 1928, 1536), (1928, 2184, 2048),
               (2184, 2216, 2368)]


def _pad_w_in(ws):
    r, dt = ws.shape[1], ws.dtype

    def cols(a, b):
        out = []
        for k in range(N_SHARD):
            lo, hi = max(a, k * W_IN_SHARD), min(b, (k + 1) * W_IN_SHARD)
            if lo < hi:
                out.append(ws[k][:, lo - k * W_IN_SHARD:hi - k * W_IN_SHARD])
        return out

    return jnp.concatenate(cols(512, 1536) + cols(0, 512) + cols(1544, 1928) + cols(1536, 1544) + [_z(r, 120, dt)]
                           + cols(1928, 2184) + [_z(r, 64, dt)] + cols(2184, 2216) + [_z(r, 32, dt), _z(r, 128, dt)],
                           axis=1)


def _unpad_w_in(d):
    shards = []
    for k in range(N_SHARD):
        a, b = k * W_IN_SHARD, (k + 1) * W_IN_SHARD
        parts = []
        for o0, o1, p0 in W_IN_GROUPS:
            lo, hi = max(a, o0), min(b, o1)
            if lo < hi:
                parts.append(d[:, p0 + lo - o0:p0 + hi - o0])
        shards.append(jnp.concatenate(parts, axis=1))
    return jnp.stack(shards)


def _pad_heads(w, width):
    r = w.shape[0]
    w3 = w.reshape(r, MLA_HEADS, width)
    return jnp.pad(w3, ((0, 0), (0, 0), (0, 128 - width))).reshape(r, MLA_HEADS * 128)


def _row(v, width=None):
    v = v.reshape(1, -1).astype(F32)
    if width is not None and v.shape[1] < width:
        v = jnp.pad(v, ((0, 0), (0, width - v.shape[1])))
    return v


BIG = {
    "w_in": (1024, 2216, 1), "w_q_up": (384, 768, 1), "w_kv_up": (256, 1024, 1), "w_mix_out": (1024, 1024, 0),
    "w_mem_q": (1024, 1024, 0), "w_mem_k": (1024, 1024, 0), "w_mem_v": (1024, 1024, 0), "w_mem_o": (1024, 1024, 0),
    "w_up": (1024, 4096, 1), "w_down": (4096, 1024, 0), "conv_w": (4, 1024, 1),
}
BIG_ORDER = list(BIG)
SMALL_ORDER = ["ln_in_g", "ln_in_b", "conv_b", "dt_bias", "a_log", "d_skip", "ssd_norm_g", "q_norm_g", "kv_norm_g",
               "ln1_g", "ln1_b", "ln2_g", "ln2_b", "ln3_g", "ln3_b"]
N_SHARD = 4
N_DEV = 8
PACK_COLS = 1024
PACK_A_ROW = {"w_down": 0, "w_up": 1024, "w_mem_q": 2048, "w_mem_k": 2304, "w_mem_v": 2560, "w_mem_o": 2816,
              "w_mix_out": 3072}
PACK_A_ORDER = list(PACK_A_ROW)
PACK_A_ROWS = 3328
PACK_B_ORDER = ["w_q_up", "w_kv_up", "conv_w"]
PACK_B_ROWS = 160


def _shard_shape(name):
    r, c, ax = BIG[name]
    return (r // N_SHARD, c) if ax == 0 else (r, c // N_SHARD)


def _split_shards(name, full):
    r, c, ax = BIG[name]
    if ax == 0:
        return full.reshape(N_SHARD, -1)
    return full.reshape(r, N_SHARD, c // N_SHARD).transpose(1, 0, 2).reshape(N_SHARD, -1)


def _join_shards(name, parts):
    r, c, ax = BIG[name]
    if ax == 0:
        return parts.reshape(r, c)
    return parts.reshape(N_SHARD, r, c // N_SHARD).transpose(1, 0, 2).reshape(r, c)


def _small_all_reduce(g, step=None):
    r, cdim = g.shape
    si, so = (len(step.inputs), len(step.out_shapes)) if step else (0, 0)

    def body(g_ref, *refs):
        s_ins, out_ref, s_outs = refs[:si], refs[si], refs[si + 1:si + 1 + so]
        buf, send_sems, recv_sems = refs[si + 1 + so:si + 4 + so]
        s_sems = refs[si + 4 + so:]
        if step:
            step.start(s_ins, s_outs, s_sems)
        x, y, c, _ = _place()
        me = 4 * x + 2 * y + c
        buf[me] = g_ref[...]
        copies = []
        for d in range(1, N_DEV):
            to = me ^ d
            cp = pltpu.make_async_remote_copy(src_ref=g_ref, dst_ref=buf.at[me], send_sem=send_sems.at[d - 1],
                                              recv_sem=recv_sems.at[d - 1],
                                              device_id=(to // 4, (to // 2) % 2, to % 2), device_id_type=MESH)
            cp.start()
            copies.append(cp)
        for cp in copies:
            cp.wait()
        acc = buf[0]
        for d in range(1, N_DEV):
            acc = acc + buf[d]
        out_ref[...] = acc
        if step:
            step.finish(s_ins, s_outs, s_sems)

    res = pl.pallas_call(
        body, name="small_all_reduce",
        in_specs=[pl.BlockSpec(memory_space=pltpu.VMEM)] + [HBM] * si,
        out_specs=[pl.BlockSpec(memory_space=pltpu.VMEM)] + [HBM] * so,
        out_shape=[jax.ShapeDtypeStruct((r, cdim), F32)] + (list(step.out_shapes) if step else []),
        scratch_shapes=[pltpu.VMEM((N_DEV, r, cdim), F32), pltpu.SemaphoreType.DMA((N_DEV - 1,)),
                        pltpu.SemaphoreType.DMA((N_DEV - 1,))] + (_sem_scratch(step) if step else []),
    )(g, *(step.inputs if step else []))
    return res if step else res[0]


def _half_tile(h):
    return next(t for t in range(512, 0, -16) if h % t == 0)


def _pair_sum(gp, theirs, name):
    n, R, C = gp.shape
    H = R // 2
    tr = _half_tile(H)
    nb = H // tr

    def body(s_ref, g_ref, t_ref, o_ref):
        o_ref[...] = (g_ref[...] + t_ref[...]).astype(o_ref.dtype)

    def shard(k, s):
        return k + (k >= s[1]).astype(jnp.int32)

    me, c = _mesh_pos()
    return pl.pallas_call(
        body, name=name,
        grid_spec=pltpu.PrefetchScalarGridSpec(
            num_scalar_prefetch=1, grid=(n - 1, nb),
            in_specs=[pl.BlockSpec((1, tr, C), lambda k, i, s: (shard(k, s), s[0] * nb + i, 0)),
                      pl.BlockSpec((1, tr, C), lambda k, i, s: (shard(k, s), i, 0))],
            out_specs=pl.BlockSpec((1, tr, C), lambda k, i, s: (shard(k, s), i, 0))),
        out_shape=jax.ShapeDtypeStruct((n, H, C), BF16), compiler_params=_params(("arbitrary", "arbitrary")),
    )(jnp.stack([c, me]).astype(jnp.int32), gp, theirs)


def _chip_sum(gp, theirs, got, name):
    n, R, C = gp.shape
    H = R // 2
    tr = _half_tile(H)
    nb = H // tr

    def body(s_ref, g_ref, t_ref, r_ref, o_ref):
        acc = g_ref[0] + t_ref[0]
        for j in range(3):
            acc = acc + r_ref[j].astype(F32)
        o_ref[...] = acc

    me, c = _mesh_pos()
    return pl.pallas_call(
        body, name=name,
        grid_spec=pltpu.PrefetchScalarGridSpec(
            num_scalar_prefetch=1, grid=(nb,),
            in_specs=[pl.BlockSpec((1, tr, C), lambda i, s: (s[0], s[1] * nb + i, 0)),
                      pl.BlockSpec((1, tr, C), lambda i, s: (s[0], i, 0)),
                      pl.BlockSpec((3, tr, C), lambda i, s: (0, i, 0))],
            out_specs=pl.BlockSpec((tr, C), lambda i, s: (s[1] * nb + i, 0))),
        out_shape=jax.ShapeDtypeStruct((R, C), F32), compiler_params=_params(("arbitrary",)),
    )(jnp.stack([me, c]).astype(jnp.int32), gp, theirs, got)


def _unpack_group_b(g_c, g_b):
    g_b = g_b.reshape(N_SHARD, -1)
    WB, off = {"w_in": g_c}, 0
    for n in PACK_B_ORDER:
        sr, sc = _shard_shape(n)
        cnt = sr * sc
        if n == "conv_w":
            part = lax.bitcast_convert_type(g_b[:, off:off + 2 * cnt].reshape(N_SHARD, cnt, 2), F32)
            off += 2 * cnt
        else:
            part = g_b[:, off:off + cnt]
            off += cnt
        WB[n] = _join_shards(n, part)
    return WB


def _group_b_grads(dw_in_p, dw_q_p, dw_k_p, dw_v_pt, dconv_w8):
    return {
        "w_in": _unpad_w_in(dw_in_p),
        "w_q_up": dw_q_p.reshape(384, MLA_HEADS, 128)[:, :, :MLA_QK].reshape(384, MLA_HEADS * MLA_QK),
        "w_kv_up": jnp.concatenate([dw_k_p.reshape(MLA_KV_RANK, MLA_HEADS, 128)[:, :, :64],
                                    dw_v_pt.T.reshape(MLA_KV_RANK, MLA_HEADS, 128)[:, :, :64]], axis=2).reshape(
                                        MLA_KV_RANK, MLA_HEADS * 128),
        "conv_w": dconv_w8[0:4],
    }


def _pack_group_b(big_b):
    gflat = [_split_shards(n, big_b[n]) for n in PACK_B_ORDER]
    used = sum(f.shape[1] for f in gflat)
    gflat.append(jnp.zeros((N_SHARD, PACK_B_ROWS * PACK_COLS - used), F32))
    return big_b["w_in"], jnp.concatenate(gflat, axis=1).reshape(N_SHARD, PACK_B_ROWS, PACK_COLS)


def _local_step(x, mem, positions, target, WB, P, *, wp_a=None, g_a=None, wp_b=None):
    S = x.shape[0]
    tr = ROW_TILE
    dist = g_a is None
    g_in, b_in = _row(P["ln_in_g"]), _row(P["ln_in_b"])
    res = _rowwise(_fn_ln, [x], [g_in, b_in], [D_MODEL, (D_MODEL, BF16)], tr=tr, name="ln_in",
                   hosted=_merge_steps([_gather_step(_own_slot(w)) for w in wp_b]) if dist else None)
    h0, h0_b = res[0], res[1]
    if dist:
        WB = _unpack_group_b(res[2], res[3])
    P = {**P, "conv_w": WB["conv_w"]}
    w_in_p = _pad_w_in(WB["w_in"])
    w_q_p = _pad_heads(WB["w_q_up"], MLA_QK)
    w_kv3 = WB["w_kv_up"].reshape(MLA_KV_RANK, MLA_HEADS, 128)
    w_k_p = _pad_heads(w_kv3[:, :, :64].reshape(MLA_KV_RANK, 512), 64)
    w_v_p = _pad_heads(w_kv3[:, :, 64:].reshape(MLA_KV_RANK, 512), 64)
    w_v_pt = w_v_p.T
    conv_w8 = jnp.pad(P["conv_w"].astype(F32), ((0, 4), (0, 0)))
    conv_b = _row(P["conv_b"])
    dt_b = _row(P["dt_bias"], 128)
    a_head = -jnp.exp(P["a_log"].reshape(-1).astype(F32))
    a_row = _row(a_head, 128)
    dexp = jnp.repeat(P["d_skip"].reshape(-1).astype(F32), 64).reshape(1, 512)
    g_ssd, g_q, g_kv = _row(P["ssd_norm_g"]), _row(P["q_norm_g"]), _row(P["kv_norm_g"])
    g1, b1, g2, b2, g3, b3 = (_row(P[k]) for k in ("ln1_g", "ln1_b", "ln2_g", "ln2_b", "ln3_g", "ln3_b"))

    half = MLA_ROPE // 2
    inv_freq = jnp.power(ROPE_THETA, -jnp.arange(half, dtype=F32) / half)
    ang = inv_freq.reshape(half, 1) * positions.reshape(1, S).astype(F32)
    trig = ("cols", jnp.concatenate([jnp.cos(ang), jnp.sin(ang)], axis=0))

    proj = _mm(h0_b, w_in_p, form="nn", tn=IN_W // 2, name="mm_in")
    conv_y, xbc, dt = _rowwise(
        _fn_conv_fwd, [(proj,) + SEG_XBC, ("prev", proj) + SEG_XBC, (proj,) + SEG_DT], [conv_w8, conv_b, dt_b],
        [1024, 1024, 128], tr=tr, name="conv_fwd")
    y_ssd, hs = _ssd_fwd(xbc, dt, a_row, name="ssd_fwd")
    (y_n,) = _rowwise(_fn_ssd_post, [y_ssd, (xbc, 0, 512), (proj,) + SEG_Z], [dexp, g_ssd], [(512, BF16)], tr=tr,
                      name="ssd_post")
    q_n, kv_n = _rowwise(_fn_mla_pre, [(proj,) + SEG_QLAT, (proj,) + SEG_KVLAT], [g_q, g_kv], [384, 256], tr=tr,
                         name="mla_pre")
    qp = _mm(q_n, w_q_p, form="nn", name="mm_q_up")
    kn = _mm(kv_n, w_k_p, form="nn", name="mm_k_up")
    v_nat = _mm(kv_n, w_v_p, form="nn", out_dtype=BF16, name="mm_v_up")
    v_t = _mm(w_v_pt, kv_n, form="nt", out_dtype=BF16, name="mm_v_up_t")
    q_rot, k_full = _rowwise(_fn_rope, [qp, kn, (proj,) + SEG_KR, trig], [],
                             [(1024, BF16), (1024, BF16)], tr=tr, name="rope")
    res = _attn_fwd(q_rot, k_full, v_t, name="attn_fwd", hosted=_gather_step(_own_slot(wp_a)) if dist else None)
    o_t, lse = res[0], res[1]
    if dist:
        g_a = res[2]
    r_mix = PACK_A_ROW["w_mix_out"]
    w_mix_o = jnp.pad(g_a[2:4, r_mix:r_mix + 256].reshape(MLA_HEADS, 64, D_MODEL),
                      ((0, 0), (0, 64), (0, 0))).reshape(MLA_HEADS * 128, D_MODEL)
    mix_o = _mm(o_t, w_mix_o, form="tn", name="mm_mix_o")
    mix_y = _mm(y_n, g_a, form="nn", b_pack="w_mix_out", name="mm_mix_y")
    (h1,) = _rowwise(_fn_res2_ln, [h0, mix_o, mix_y], [g1, b1], [D_MODEL], tr=tr, name="ln1")
    qm = _mm(h1, g_a, form="nn", b_pack="w_mem_q", out_dtype=BF16, name="mm_mem_q")
    km = _mm(mem, g_a, form="nn", b_pack="w_mem_k", out_dtype=BF16, name="mm_mem_k")
    vm = _mm(mem, g_a, form="nn", b_pack="w_mem_v", out_dtype=BF16, name="mm_mem_v")
    (om,) = _rowwise(_fn_mem_fwd, [qm], [km, vm], [(D_MODEL, BF16)], tr=tr, name="mem_fwd")
    xa = _mm(om, g_a, form="nn", b_pack="w_mem_o", name="mm_mem_o")
    h2, h2_b = _rowwise(_fn_res_ln, [h1, xa], [g2, b2], [D_MODEL, (D_MODEL, BF16)], tr=tr, name="ln2")
    u = _mm(h2_b, g_a, form="nn", b_pack="w_up", out_dtype=BF16, name="mm_up")
    ff = _mm(u, g_a, form="nn", a_pro=_relu2, b_pack="w_down", name="mm_down")

    gp = lax.empty((N_SHARD, PACK_A_ROWS, PACK_COLS), F32)
    dt3, dt3_b, dg3, db3, loss = _rowwise(_fn_final, [h2, ff, target], [g3, b3], [D_MODEL, (D_MODEL, BF16)],
                                          [(1, D_MODEL), (1, D_MODEL), (1, 128)], tr=tr, name="ln3_loss")
    du = _mm(dt3_b, g_a, form="nt", b_pack="w_down", epi=(_epi_du, u), out_dtype=BF16, name="mm_down_dx")
    gp = _mm(u, dt3_b, form="tn", a_pro=_relu2, out_pack=("w_down", gp), name="mm_down_dw")
    gp = _mm(h2_b, du, form="tn", out_pack=("w_up", gp), name="mm_up_dw")
    dh2 = _mm(du, g_a, form="nt", b_pack="w_up", name="mm_up_dx")
    dt2, dg2, db2 = _rowwise(_fn_res_ln_bwd, [h1, xa, dt3, dh2], [g2], [D_MODEL], [(1, D_MODEL)] * 2, tr=tr,
                             name="ln2_bwd")
    dom = _mm(dt2, g_a, form="nt", b_pack="w_mem_o", out_dtype=BF16, name="mm_mem_o_dx")
    gp = _mm(om, dt2, form="tn", out_pack=("w_mem_o", gp), name="mm_mem_o_dw")
    dqm, dkm, dvm = _rowwise(_fn_mem_bwd, [qm, dom], [km, vm], [(D_MODEL, BF16)], [(256, D_MODEL)] * 2, tr=tr,
                             name="mem_bwd")
    gp = _mm(h1, dqm, form="tn", out_pack=("w_mem_q", gp), name="mm_mem_q_dw")
    gp = _mm(mem, dkm, form="tn", out_pack=("w_mem_k", gp), name="mm_mem_k_dw")
    gp = _mm(mem, dvm, form="tn", out_pack=("w_mem_v", gp), name="mm_mem_v_dw")
    dh1 = _mm(dqm, g_a, form="nt", b_pack="w_mem_q", name="mm_mem_q_dx")
    dt1, dg1, db1 = _rowwise(_fn_res2_ln_bwd, [h0, mix_o, mix_y, dt2, dh1], [g1], [D_MODEL], [(1, D_MODEL)] * 2,
                             tr=tr, name="ln1_bwd")
    do_t = _mm(w_mix_o, dt1, form="nt", name="mm_mix_o_dx")
    dy_n = _mm(dt1, g_a, form="nt", b_pack="w_mix_out", b_rows=512, name="mm_mix_y_dx")
    dw_mix_o = _mm(o_t, dt1, form="nn", name="mm_mix_o_dw")
    gp = _mm(y_n, dt1, form="tn", out_pack=("w_mix_out", gp), name="mm_mix_y_dw")
    gp = lax.dynamic_update_slice(
        gp, dw_mix_o.reshape(MLA_HEADS, 128, D_MODEL)[:, :64].reshape(2, 256, D_MODEL), (2, r_mix, 0))
    dproj = lax.empty((S, IN_W), BF16)
    dy_ssd, dproj, dxs_skip, ddexp, dg_ssd = _rowwise(
        _fn_ssd_post_bwd, [dy_n, y_ssd, (xbc, 0, 512), (proj,) + SEG_Z], [dexp, g_ssd],
        [512, (512, dproj, SEG_Z[0]), 512], [(1, 512)] * 2, tr=tr, name="ssd_post_bwd")
    res = _ssd_bwd(xbc, dt, a_row, hs, dy_ssd, name="ssd_bwd", hosted=_pair_exchange_step(gp) if dist else None)
    dxs, dbc, ddt, da_head = res[0], res[1], res[2], res[3]
    chip_step = None
    if dist:
        theirs_a = res[4]
        chip_step = _chip_exchange_step(_pair_sum(gp, theirs_a, "pair_sum_a"))
    res = _attn_bwd(q_rot, k_full, v_nat, o_t, do_t, lse, name="attn_bwd", hosted=chip_step)
    dq_rot, dk, dv_t = res[0], res[1], res[2]
    if dist:
        gp = _chip_sum(gp, theirs_a, res[3], "chip_sum_a")
    dqp, dproj = _rowwise(_fn_rope_bwd, [dq_rot, dk, trig], [], [(1024, BF16), (256, dproj, SEG_KR[0])], tr=tr,
                          name="rope_bwd")
    dw_q_p = _mm(q_n, dqp, form="tn", name="mm_q_up_dw")
    dq_n = _mm(dqp, w_q_p, form="nt", name="mm_q_up_dx")
    dw_k_p = _mm(kv_n, dk, form="tn", name="mm_k_up_dw")
    dkv_n1 = _mm(dk, w_k_p, form="nt", name="mm_k_up_dx")
    dw_v_pt = _mm(dv_t, kv_n, form="nn", name="mm_v_up_dw")
    dkv_n2 = _mm(dv_t, w_v_pt, form="tn", name="mm_v_up_dx")
    dyc, ddtr, dconv_b, ddt_b = _rowwise(
        _fn_conv_bwd_a, [conv_y, dxs, dxs_skip, dbc, (proj,) + SEG_DT, ddt], [dt_b], [1024, (128, BF16)],
        [(1, 1024), (1, 128)], tr=tr, name="conv_bwd_a")
    dproj, dg_q, dg_kv = _rowwise(
        _fn_mla_pre_bwd, [(proj,) + SEG_QLAT, (proj,) + SEG_KVLAT, dq_n, dkv_n1, dkv_n2, ddtr], [g_q, g_kv],
        [(SEG_KR[0] - SEG_QLAT[0], dproj, SEG_QLAT[0])], [(1, 384), (1, 256)], tr=tr, name="mla_pre_bwd")
    dproj, dconv_w8 = _rowwise(
        _fn_conv_bwd_b, [dyc, ("next", dyc, 0, 1024), (proj,) + SEG_XBC, ("prev", proj) + SEG_XBC], [conv_w8],
        [(1024, dproj, SEG_XBC[0])], [(8, 1024)], tr=tr, name="conv_bwd_b")
    res = _mm(h0_b, dproj, form="tn", tn=IN_W // 2, name="mm_in_dw", hosted=_pair_fill_step(gp) if dist else None)
    dw_in_p, red_a = (res[0], res[1]) if dist else (res, None)
    big_b = _group_b_grads(dw_in_p, dw_q_p, dw_k_p, dw_v_pt, dconv_w8)
    q_b = None
    if dist:
        gp_c, gp_b = _pack_group_b(big_b)
        dh0, theirs_c, theirs_b = _mm(dproj, w_in_p, form="nt", tk=IN_W // 2, name="mm_in_dx", hosted=_merge_steps(
            [_pair_exchange_step(gp_c), _pair_exchange_step(gp_b)]))
        q_b = ((gp_c, theirs_c, _pair_sum(gp_c, theirs_c, "pair_sum_w_in")),
               (gp_b, theirs_b, _pair_sum(gp_b, theirs_b, "pair_sum_b")))
        gp = red_a
    else:
        dh0 = _mm(dproj, w_in_p, form="nt", tk=IN_W // 2, name="mm_in_dx")
    grad_x, dg_in, db_in = _rowwise(_fn_in_ln_bwd, [x, dt1, dh0], [g_in], [D_MODEL], [(1, D_MODEL)] * 2, tr=tr,
                                    name="ln_in_bwd")

    small = {
        "ln_in_g": dg_in, "ln_in_b": db_in, "conv_b": dconv_b, "dt_bias": ddt_b[:, :8],
        "a_log": da_head[:, :8] * a_head.reshape(1, 8),
        "d_skip": ddexp.reshape(8, 64).sum(axis=1).reshape(1, 8),
        "ssd_norm_g": dg_ssd, "q_norm_g": dg_q, "kv_norm_g": dg_kv,
        "ln1_g": dg1, "ln1_b": db1, "ln2_g": dg2, "ln2_b": db2, "ln3_g": dg3, "ln3_b": db3,
    }
    return loss[0, 0], grad_x, (gp, q_b), big_b, small


def _adam(w, g, m, v, name):
    shape = w.shape
    w2, m2, v2 = (t.reshape(-1, shape[-1]) for t in (w, m, v))
    if isinstance(g, tuple):
        fn = lambda ctx, wv, gv, mv, vv: (*_fn_adam(ctx, wv, gv, mv, vv), gv)
        d, mn, vn, g = _rowwise(fn, [w2, (g[0], 0, shape[-1], g[1]), m2, v2], [], [shape[-1]] * 4, tr=ROW_TILE,
                                name=name)
    else:
        d, mn, vn = _rowwise(_fn_adam, [w2, g.reshape(-1, shape[-1]), m2, v2], [], [shape[-1]] * 3, tr=ROW_TILE,
                             name=name)
    return g.reshape(shape), d.reshape(shape), mn.reshape(shape), vn.reshape(shape)


def _adam_columns(w, g, m, v, name):
    cols = w.shape[0]
    step = cols // 2 if cols % 2 == 0 else cols
    blk = pl.BlockSpec((step,) + w.shape[1:], lambda i: (i, 0, 0))

    def body(w_ref, g_ref, m_ref, v_ref, d_ref, mo_ref, vo_ref):
        d_ref[...], mo_ref[...], vo_ref[...] = _fn_adam(None, w_ref[...], g_ref[...], m_ref[...], v_ref[...])

    return _call_with_step(body, None, None, [w, g, m, v], name=name, grid=(cols // step,), in_specs=[blk] * 4,
                           out_specs=[blk] * 3, out_shape=[jax.ShapeDtypeStruct(w.shape, F32)] * 3, sem=("arbitrary",))


def kernel(x, mem, positions, ln_in_g, ln_in_b, w_in, conv_w, conv_b, dt_bias, a_log, d_skip, ssd_norm_g, q_norm_g, w_q_up, kv_norm_g, w_kv_up, w_mix_out, ln1_g, ln1_b, w_mem_q, w_mem_k, w_mem_v, w_mem_o, ln2_g, ln2_b, w_up, w_down, ln3_g, ln3_b, loss_target, m_ln_in_g, m_ln_in_b, m_w_in, m_conv_w, m_conv_b, m_dt_bias, m_a_log, m_d_skip, m_ssd_norm_g, m_q_norm_g, m_w_q_up, m_kv_norm_g, m_w_kv_up, m_w_mix_out, m_ln1_g, m_ln1_b, m_w_mem_q, m_w_mem_k, m_w_mem_v, m_w_mem_o, m_ln2_g, m_ln2_b, m_w_up, m_w_down, m_ln3_g, m_ln3_b, v_ln_in_g, v_ln_in_b, v_w_in, v_conv_w, v_conv_b, v_dt_bias, v_a_log, v_d_skip, v_ssd_norm_g, v_q_norm_g, v_w_q_up, v_kv_norm_g, v_w_kv_up, v_w_mix_out, v_ln1_g, v_ln1_b, v_w_mem_q, v_w_mem_k, v_w_mem_v, v_w_mem_o, v_ln2_g, v_ln2_b, v_w_up, v_w_down, v_ln3_g, v_ln3_b):
    args = dict(locals())

    wp_a = jnp.concatenate([args[n].reshape(-1, PACK_COLS).astype(BF16) for n in PACK_A_ORDER], axis=0)
    flat = [args[n].reshape(-1).astype(BF16) for n in PACK_B_ORDER[:-1]]
    flat.append(lax.bitcast_convert_type(conv_w.reshape(-1), BF16).reshape(-1))
    used = sum(f.shape[0] for f in flat)
    flat.append(jnp.zeros((PACK_B_ROWS * PACK_COLS - used,), BF16))
    wp_b = jnp.concatenate(flat).reshape(PACK_B_ROWS, PACK_COLS)
    wp_c = w_in[0].astype(BF16)

    P = {n: args[n] for n in SMALL_ORDER}
    loss, grad_x, (red_a, ((gp_c, theirs_c, pb_c), (gp_b, theirs_b, pb_b))), _, gsmall = _local_step(
        x[0], mem[0], positions[0], loss_target[0], None, P, wp_a=wp_a, wp_b=(wp_c, wp_b))

    gs = jnp.concatenate([_row(gsmall[n], PACK_COLS) for n in SMALL_ORDER] + [_row(loss, PACK_COLS)], axis=0)
    gs, got_c, got_b = _small_all_reduce(gs, _merge_steps([_chip_exchange_step(pb_c), _chip_exchange_step(pb_b)]))
    loss = gs[len(SMALL_ORDER), 0]
    red_c, red_b = _run_step(_merge_steps([_pair_fill_step(_chip_sum(gp_c, theirs_c, got_c, "chip_sum_w_in")),
                                           _pair_fill_step(_chip_sum(gp_b, theirs_b, got_b, "chip_sum_b"))]),
                             "pair_fill_b")

    grads, deltas, new_m, new_v = {}, {}, {}, {}
    for n in PACK_A_ORDER:
        grads[n], deltas[n], new_m[n], new_v[n] = _adam(args[n], (red_a, PACK_A_ROW[n]), args["m_" + n],
                                                        args["v_" + n], "adam_" + n)
    to_cols = lambda t: jnp.transpose(t, (2, 0, 1))
    from_cols = lambda t: jnp.transpose(t, (1, 2, 0))
    g_t = to_cols(red_c[None])
    grads["w_in"] = from_cols(g_t)
    deltas["w_in"], new_m["w_in"], new_v["w_in"] = map(
        from_cols, _adam_columns(to_cols(w_in), g_t, to_cols(m_w_in), to_cols(v_w_in), "adam_w_in"))
    red_b = red_b.reshape(-1)
    off = 0
    for n in PACK_B_ORDER:
        sr, sc = _shard_shape(n)
        grads[n], deltas[n], new_m[n], new_v[n] = _adam(args[n], red_b[off:off + sr * sc], args["m_" + n],
                                                        args["v_" + n], "adam_" + n)
        off += sr * sc
    pack = lambda pre: jnp.concatenate([_row(args[pre + n], PACK_COLS) for n in SMALL_ORDER]
                                       + [jnp.zeros((1, PACK_COLS), F32)], axis=0)
    ds, ms, vs = _rowwise(_fn_adam, [pack(""), gs, pack("m_"), pack("v_")], [], [PACK_COLS] * 3, tr=16,
                          name="adam_small")
    for i, n in enumerate(SMALL_ORDER):
        cnt = args[n].size
        take = lambda t: t[i, :cnt].reshape(args[n].shape)
        grads[n], deltas[n], new_m[n], new_v[n] = take(gs), take(ds), take(ms), take(vs)

    order = ["ln_in_g", "ln_in_b", "w_in", "conv_w", "conv_b", "dt_bias", "a_log", "d_skip", "ssd_norm_g",
             "q_norm_g", "w_q_up", "kv_norm_g", "w_kv_up", "w_mix_out", "ln1_g", "ln1_b", "w_mem_q", "w_mem_k",
             "w_mem_v", "w_mem_o", "ln2_g", "ln2_b", "w_up", "w_down", "ln3_g", "ln3_b"]
    return (loss, grad_x[None], *[grads[n] for n in order], *[deltas[n] for n in order],
            *[new_m[n] for n in order], *[new_v[n] for n in order])
```

```python
import functools

import jax
import jax.numpy as jnp
from jax import lax
from jax.experimental import pallas as pl
from jax.experimental.pallas import tpu as pltpu

F32 = jnp.float32
BF16 = jnp.bfloat16
MESH = pl.DeviceIdType.MESH

D_MODEL = 1024
SSD_CHUNK = 128
SSD_STATE = 128
MLA_HEADS = 8
MLA_ROPE = 32
MLA_QK = 96
MLA_KV_RANK = 256
ROPE_THETA = 10000.0
MEM_HEADS = 4
MEM_HEAD_DIM = 256
LN_EPS = 1e-5
RMS_EPS = 1e-6
ALPHA = 2.0 ** 0.25
ADAM_LR = 0.001
ADAM_B1 = 0.9
ADAM_B2 = 0.999
ADAM_EPS = 1e-08
ADAM_WD = 0.01
ADAM_STEP = 10

LANES = 128
IN_W = 2560
SEG_XBC = (0, 1024)
SEG_Z = (1024, 512)
SEG_QLAT = (1536, 384)
SEG_DT = (1920, 128)
SEG_KVLAT = (2048, 256)
SEG_KR = (2304, 128)
VMEM_LIMIT = 56 * 1024 * 1024
ATTN_TILE = 512
ROW_TILE = 512
SSD_PER_STEP = 2
NEG = -1e30
MLA_SCALE = MLA_QK ** -0.5
MEM_SCALE = MEM_HEAD_DIM ** -0.5

NN = (((1,), (0,)), ((), ()))
NT = (((1,), (1,)), ((), ()))
TN = (((0,), (0,)), ((), ()))


def _dot(a, b, dims=NN):
    return lax.dot_general(a.astype(BF16), b.astype(BF16), dims, preferred_element_type=F32)


def _dot_exact(a, b):
    return lax.dot_general(a, b, NN, precision=lax.Precision.HIGHEST, preferred_element_type=F32)


def _pick(dim, pref):
    t = min(pref, dim)
    t -= t % LANES
    while t >= LANES:
        if dim % t == 0:
            return t
        t -= LANES
    return dim


def _params(sem):
    return pltpu.CompilerParams(dimension_semantics=sem, vmem_limit_bytes=VMEM_LIMIT)


def _pack_caps(wname):
    r, c, ax = BIG[wname]
    if ax == 0:
        return (r if r <= 1024 else r // N_SHARD), c
    return r, c // N_SHARD


def _pack_block(wname, br, bc):
    r, c, ax = BIG[wname]
    r0 = PACK_A_ROW[wname]
    sr = r // N_SHARD if ax == 0 else r
    if ax == 0 and br > sr:
        assert br % sr == 0 and r0 % sr == 0
        return (br // sr, sr, bc), lambda rb, cb: (rb, r0 // sr, cb)
    assert r0 % br == 0
    if ax == 0:
        per = sr // br
        return (1, br, bc), lambda rb, cb: (rb // per, r0 // br + rb % per, cb)
    per = (c // N_SHARD) // bc
    return (1, br, bc), lambda rb, cb: (cb // per, r0 // br + rb, cb % per)


def _mm(a, b, *, form, name, a_pro=None, epi=None, out_dtype=F32, tm=1024, tn=1024, tk=1024, b_pack=None,
        b_rows=None, out_pack=None, hosted=None):
    b_shape = BIG[b_pack][:2] if b_pack else b.shape
    if b_pack and form == "nt":
        b_shape = (b_rows or b_shape[0], b_shape[1])
    if form == "nn":
        (m, k), (_, n) = a.shape, b_shape
    elif form == "nt":
        (m, k), (n, _) = a.shape, b_shape
    else:
        (k, m), (_, n) = a.shape, b_shape
    if b_pack:
        rcap, ccap = _pack_caps(b_pack)
        tk, tn = (min(tk, rcap), min(tn, ccap)) if form == "nn" else (min(tk, ccap), min(tn, rcap))
    if out_pack:
        rcap, ccap = _pack_caps(out_pack[0])
        tm, tn = min(tm, rcap), min(tn, ccap)
    tm, tn, tk = _pick(m, tm), _pick(n, tn), _pick(k, tk)
    dims = {"nn": NN, "nt": NT, "tn": TN}[form]
    nk = k // tk
    direct = out_dtype == F32 and epi is None
    n_extra = (1 if epi else 0) + (1 if out_pack else 0)

    def body(a_ref, b_ref, *rest):
        o_ref = rest[n_extra]
        acc_ref = o_ref if direct else rest[-1]

        @pl.when(pl.program_id(2) == 0)
        def _():
            acc_ref[...] = jnp.zeros_like(acc_ref)

        av = a_ref[...]
        if a_pro is not None:
            av = a_pro(av)
        bv = b_ref[...]
        acc_ref[...] += _dot(av, bv.reshape(-1, bv.shape[-1]), dims).reshape(acc_ref.shape)
        if not direct:
            @pl.when(pl.program_id(2) == nk - 1)
            def _():
                val = acc_ref[...]
                if epi is not None:
                    val = epi[0](val, rest[0][...])
                o_ref[...] = val.reshape(o_ref.shape).astype(o_ref.dtype)

    if form == "tn":
        a_spec = pl.BlockSpec((tk, tm), lambda i, j, kk: (kk, i))
    else:
        a_spec = pl.BlockSpec((tm, tk), lambda i, j, kk: (i, kk))
    if b_pack:
        shape, idx = _pack_block(b_pack, *((tk, tn) if form == "nn" else (tn, tk)))
        b_spec = pl.BlockSpec(shape, (lambda i, j, kk: idx(kk, j)) if form == "nn" else (lambda i, j, kk: idx(j, kk)))
    elif form == "nt":
        b_spec = pl.BlockSpec((tn, tk), lambda i, j, kk: (j, kk))
    else:
        b_spec = pl.BlockSpec((tk, tn), lambda i, j, kk: (kk, j))
    in_specs, args = [a_spec, b_spec], [a, b]
    out_spec = pl.BlockSpec((tm, tn), lambda i, j, kk: (i, j))
    out_sds, aliases = jax.ShapeDtypeStruct((m, n), out_dtype), {}
    if epi is not None:
        in_specs.append(out_spec)
        args.append(epi[1])
    if out_pack:
        wname, buf = out_pack
        shape, idx = _pack_block(wname, tm, tn)
        out_spec = pl.BlockSpec(shape, lambda i, j, kk: idx(i, j))
        out_sds, aliases = jax.ShapeDtypeStruct(buf.shape, buf.dtype), {len(args): 0}
        in_specs.append(HBM)
        args.append(buf)
    acc_shape = out_spec.block_shape if out_pack else (tm, tn)
    res = _call_with_step(
        body, hosted, None, args, name=name, grid=(m // tm, n // tn, nk), in_specs=in_specs, out_specs=[out_spec],
        out_shape=[out_sds], sem=("parallel", "parallel", "arbitrary"), aliases=aliases,
        scratch_shapes=[] if direct else [pltpu.VMEM(acc_shape, F32)])
    return res[0] if hosted is None else res


class _Ctx:
    def __init__(self, i, n):
        self.i, self.n = i, n


def _rowwise(fn, rows, consts, row_outs, acc_outs=(), *, tr, name, n_rows=None, hosted=None):
    norm = []
    for r in rows:
        kind = "tile"
        if isinstance(r, tuple) and isinstance(r[0], str):
            kind, r = r[0], r[1:]
        if kind == "cols":
            norm.append((kind, r[0], 0, r[0].shape[0], 0))
            continue
        row0 = 0
        if isinstance(r, tuple) and len(r) == 4:
            r, row0 = r[:3], r[3]
        arr, col0, width = r if isinstance(r, tuple) else (r, 0, r.shape[1])
        assert col0 % width == 0
        norm.append((kind, arr, col0 // width, width, row0))
    n_rows = n_rows or next(a.shape[0] for k, a, _, _, _ in norm if k == "tile")
    tr = min(tr, n_rows)
    while n_rows % tr:
        tr -= 8
    n = n_rows // tr
    arrs, specs = [], []
    for kind, arr, cb, width, row0 in norm:
        if kind == "tile":
            assert row0 % tr == 0
            specs.append(pl.BlockSpec((tr, width), lambda i, cb=cb, rb=row0 // tr: (i + rb, cb)))
        elif kind == "cols":
            specs.append(pl.BlockSpec((width, tr), lambda i: (0, i)))
        elif kind == "prev":
            specs.append(pl.BlockSpec((8, width), lambda i, cb=cb: (jnp.maximum(i * (tr // 8) - 1, 0), cb)))
        else:
            specs.append(pl.BlockSpec((8, width), lambda i, cb=cb: (jnp.minimum((i + 1) * (tr // 8), n_rows // 8 - 1), cb)))
        arrs.append(arr)
    for c in consts:
        specs.append(pl.BlockSpec(c.shape, lambda i, nd=c.ndim: (0,) * nd))
        arrs.append(c)
    n_in, n_ro = len(arrs), len(row_outs)
    out_shape, out_specs, aliases = [], [], {}
    for j, ro in enumerate(row_outs):
        w, dt, col0 = (ro + (None,))[:3] if isinstance(ro, tuple) else (ro, F32, None)
        if col0 is None:
            out_shape.append(jax.ShapeDtypeStruct((n_rows, w), dt))
            out_specs.append(pl.BlockSpec((tr, w), lambda i: (i, 0)))
        else:
            assert col0 % w == 0 and dt.shape[0] == n_rows
            out_shape.append(jax.ShapeDtypeStruct(dt.shape, dt.dtype))
            out_specs.append(pl.BlockSpec((tr, w), lambda i, cb=col0 // w: (i, cb)))
            aliases[len(arrs)] = j
            arrs.append(dt)
            specs.append(HBM)
    n_all = len(arrs)
    out_shape += [jax.ShapeDtypeStruct(s, F32) for s in acc_outs]
    out_specs += [pl.BlockSpec(s, lambda i: (0, 0)) for s in acc_outs]

    def body(*refs):
        i = pl.program_id(0)
        vals = [r[...] for r in refs[:n_in]]
        outs = fn(_Ctx(i, n), *vals)
        if not isinstance(outs, (tuple, list)):
            outs = (outs,)
        o_refs = refs[n_all:]
        for o_ref, o in zip(o_refs[:n_ro], outs[:n_ro]):
            o_ref[...] = o.astype(o_ref.dtype)
        if acc_outs:
            @pl.when(i == 0)
            def _():
                for o_ref in o_refs[n_ro:]:
                    o_ref[...] = jnp.zeros_like(o_ref)

            for o_ref, o in zip(o_refs[n_ro:], outs[n_ro:]):
                o_ref[...] += jnp.broadcast_to(o, o_ref.shape)

    return _call_with_step(body, hosted, None, arrs, name=name, grid=(n,), in_specs=specs, out_specs=out_specs,
                           out_shape=out_shape, sem=("arbitrary",), aliases=aliases)


def _sum0(v):
    return jnp.sum(v, axis=0, keepdims=True)


def _mean1(v):
    return jnp.mean(v, axis=-1, keepdims=True)


def _sigmoid(v):
    return 1.0 / (1.0 + jnp.exp(-v))


def _ln_stats(t):
    xc = t - _mean1(t)
    rstd = lax.rsqrt(_mean1(xc * xc) + LN_EPS)
    return xc * rstd, rstd


def _ln_bwd(xhat, rstd, dy, g):
    dxh = dy * g
    dx = rstd * (dxh - _mean1(dxh) - xhat * _mean1(dxh * xhat))
    return dx, _sum0(dy * xhat), _sum0(dy)


def _rms_fwd(v, g):
    return v * lax.rsqrt(_mean1(v * v) + RMS_EPS) * g


def _rms_bwd(v, dy, g):
    rs = lax.rsqrt(_mean1(v * v) + RMS_EPS)
    vh = v * rs
    dyg = dy * g
    return rs * (dyg - vh * _mean1(dyg * vh)), _sum0(dy * vh)


def _lane(shape):
    return lax.broadcasted_iota(jnp.int32, shape, len(shape) - 1)


def _shift_down(u, halo, s, is_first):
    tr = u.shape[0]
    rolled = pltpu.roll(u, s, 0)
    hr = jnp.where(is_first, 0.0, pltpu.roll(halo, s, 0))
    row = lax.broadcasted_iota(jnp.int32, hr.shape, 0)
    top = jnp.where(row < s, hr, rolled[0:8])
    if tr == 8:
        return top
    return jnp.concatenate([top, rolled[8:]], axis=0)


def _shift_up(d, halo, s, is_last):
    tr = d.shape[0]
    rolled = pltpu.roll(d, tr - s, 0)
    hr = jnp.where(is_last, 0.0, pltpu.roll(halo, 8 - s, 0))
    row = lax.broadcasted_iota(jnp.int32, hr.shape, 0)
    bot = jnp.where(row >= 8 - s, hr, rolled[tr - 8:])
    if tr == 8:
        return bot
    return jnp.concatenate([rolled[:tr - 8], bot], axis=0)


def _rope_tables(trig):
    t = jnp.concatenate([trig] * (LANES // trig.shape[0]), axis=0).T
    lane = _lane(t.shape)
    first, second = (lane >= 64) & (lane < 80), (lane >= 80) & (lane < 96)
    ta = jnp.where(lane < 64, 1.0, jnp.where(first, pltpu.roll(t, 64, 1), jnp.where(second, pltpu.roll(t, 80, 1), 0.0)))
    return ta, jnp.where(second, pltpu.roll(t, 64, 1), 0.0), jnp.where(first, -pltpu.roll(t, 48, 1), 0.0)


def _rope(v, ta, tb, tc):
    return v * ta + pltpu.roll(v, 16, 1) * tb + pltpu.roll(v, LANES - 16, 1) * tc


def _rope_bwd(d, ta, tb, tc):
    return d * ta + pltpu.roll(d * tb, LANES - 16, 1) + pltpu.roll(d * tc, 16, 1)


def _ssd_common(dtv, a_row):
    L = SSD_CHUNK
    a = dtv * a_row
    r = lax.broadcasted_iota(jnp.int32, (L, L), 0)
    c = lax.broadcasted_iota(jnp.int32, (L, L), 1)
    tril = r >= c
    cs = _dot_exact(tril.astype(F32), a)
    cs_t = cs.T
    cs_last = cs[L - 1:L, :]
    return dict(a=a, tril=tril, cs=cs, cs_t=cs_t, ecs=jnp.exp(cs), dte=jnp.exp(cs_last - cs),
                elast=jnp.exp(cs_last))


def _pair_sel(v, h0, lo):
    return jnp.where(lo, v[:, h0:h0 + 1], v[:, h0 + 1:h0 + 2])


def _ssd_pair(cm, h0, cb, xp, dtv, bmat, cmat, hp, lo):
    x = xp * _pair_sel(dtv, h0, lo)
    lam0 = jnp.exp(jnp.where(cm["tril"], cm["cs"][:, h0:h0 + 1] - cm["cs_t"][h0:h0 + 1, :], NEG))
    lam1 = jnp.exp(jnp.where(cm["tril"], cm["cs"][:, h0 + 1:h0 + 2] - cm["cs_t"][h0 + 1:h0 + 2, :], NEG))
    m0, m1 = cb * lam0, cb * lam1
    ydiag = jnp.where(lo, _dot(m0, x), _dot(m1, x))
    ecs_p = _pair_sel(cm["ecs"], h0, lo)
    dte_p = _pair_sel(cm["dte"], h0, lo)
    yoff = _dot(cmat, hp, NT) * ecs_p
    xd = x * dte_p
    st = _dot(xd, bmat, TN)
    rlo = lax.broadcasted_iota(jnp.int32, (LANES, SSD_STATE), 0) < 64
    decay = jnp.where(rlo, cm["elast"][:, h0:h0 + 1], cm["elast"][:, h0 + 1:h0 + 2])
    h_next = hp * decay + st
    return dict(x=x, lam0=lam0, lam1=lam1, m0=m0, m1=m1, y=ydiag + yoff, yoff=yoff, ecs_p=ecs_p, dte_p=dte_p,
                xd=xd, decay=decay, h_next=h_next)


def _ssd_fwd(xbc, dt, a_row, *, name):
    S = xbc.shape[0]
    L = SSD_CHUNK
    nc = S // L
    per = SSD_PER_STEP if nc % SSD_PER_STEP == 0 else 1
    G = per * L

    def body(xs_ref, bm_ref, cm_ref, dt_ref, a_ref, y_ref, hs_ref, h_scr):
        @pl.when(pl.program_id(0) == 0)
        def _():
            h_scr[...] = jnp.zeros_like(h_scr)

        lo = _lane((L, LANES)) < 64
        for sub in range(per):
            rows = slice(sub * L, (sub + 1) * L)
            dtv = dt_ref[rows, :]
            cm = _ssd_common(dtv, a_ref[...])
            ys = []
            for g in range(2):
                bmat = bm_ref[rows, g * 128:(g + 1) * 128]
                cmat = cm_ref[rows, g * 128:(g + 1) * 128]
                cb = _dot(cmat, bmat, NT)
                for pr in range(2):
                    p4 = 2 * g + pr
                    hp = h_scr[p4]
                    hs_ref[sub, p4 * 128:(p4 + 1) * 128, :] = hp
                    t = _ssd_pair(cm, 2 * p4, cb, xs_ref[rows, p4 * 128:(p4 + 1) * 128], dtv, bmat, cmat, hp, lo)
                    ys.append(t["y"])
                    h_scr[p4] = t["h_next"]
            y_ref[rows, :] = jnp.concatenate(ys, axis=1)

    return pl.pallas_call(
        body, name=name, grid=(nc // per,),
        in_specs=[pl.BlockSpec((G, 512), lambda c: (c, 0)), pl.BlockSpec((G, 256), lambda c: (c, 2)),
                  pl.BlockSpec((G, 256), lambda c: (c, 3)), pl.BlockSpec((G, 128), lambda c: (c, 0)),
                  pl.BlockSpec((1, 128), lambda c: (0, 0))],
        out_specs=[pl.BlockSpec((G, 512), lambda c: (c, 0)), pl.BlockSpec((per, 512, 128), lambda c: (c, 0, 0))],
        out_shape=[jax.ShapeDtypeStruct((S, 512), F32), jax.ShapeDtypeStruct((nc, 512, 128), F32)],
        scratch_shapes=[pltpu.VMEM((4, 128, 128), F32)],
        compiler_params=_params(("arbitrary",)),
    )(xbc, xbc, xbc, dt, a_row)


def _ssd_bwd(xbc, dt, a_row, hs, dy, *, name, hosted=None):
    S = xbc.shape[0]
    L = SSD_CHUNK
    nc = S // L
    per = SSD_PER_STEP if nc % SSD_PER_STEP == 0 else 1
    G = per * L

    def body(xs_ref, bm_ref, cm_ref, dt_ref, a_ref, hs_ref, dy_ref, dxs_ref, dbc_ref, ddt_ref, da_ref, g_scr):
        @pl.when(pl.program_id(0) == 0)
        def _():
            g_scr[...] = jnp.zeros_like(g_scr)
            da_ref[...] = jnp.zeros_like(da_ref)

        for sub in reversed(range(per)):
            rows = pl.ds(sub * L, L)
            chunk(xs_ref.at[rows, :], bm_ref.at[rows, :], cm_ref.at[rows, :], dt_ref.at[rows, :], a_ref,
                  hs_ref.at[pl.ds(sub, 1)], dy_ref.at[rows, :], dxs_ref.at[rows, :], dbc_ref.at[rows, :],
                  ddt_ref.at[rows, :], da_ref, g_scr)

    def chunk(xs_ref, bm_ref, cm_ref, dt_ref, a_ref, hs_ref, dy_ref, dxs_ref, dbc_ref, ddt_ref, da_ref, g_scr):
        dtv = dt_ref[...]
        a_row_v = a_ref[...]
        cm = _ssd_common(dtv, a_row_v)
        lo = _lane((L, LANES)) < 64
        lane_row = _lane((1, LANES))
        ri = lax.broadcasted_iota(jnp.int32, (L, L), 0)
        ci = lax.broadcasted_iota(jnp.int32, (L, L), 1)
        triu = (ri <= ci).astype(F32)
        stril = ri > ci

        def halves(v, mask):
            return (jnp.sum(jnp.where(mask, v, 0.0), axis=1, keepdims=True),
                    jnp.sum(jnp.where(mask, 0.0, v), axis=1, keepdims=True))

        i_all = jnp.zeros((L, LANES), F32)
        yo_all = jnp.zeros((L, LANES), F32)
        w_all = jnp.zeros((L, LANES), F32)
        ddt_x = jnp.zeros((L, LANES), F32)
        e_row = jnp.zeros((1, LANES), F32)
        rlo = lax.broadcasted_iota(jnp.int32, (LANES, SSD_STATE), 0) < 64
        dxs, dbs, dcs = [], [], []
        for g in range(2):
            bmat = bm_ref[:, g * 128:(g + 1) * 128]
            cmat = cm_ref[:, g * 128:(g + 1) * 128]
            cb = _dot(cmat, bmat, NT)
            dcb = jnp.zeros((L, L), F32)
            db = jnp.zeros((L, SSD_STATE), F32)
            dc = jnp.zeros((L, SSD_STATE), F32)
            for pr in range(2):
                p4 = 2 * g + pr
                h0 = 2 * p4
                hp = hs_ref[0, p4 * 128:(p4 + 1) * 128, :]
                xp = xs_ref[:, p4 * 128:(p4 + 1) * 128]
                t = _ssd_pair(cm, h0, cb, xp, dtv, bmat, cmat, hp, lo)
                gst = g_scr[p4]
                dyp = dy_ref[:, p4 * 128:(p4 + 1) * 128]
                dy0 = jnp.where(lo, dyp, 0.0)
                dy1 = dyp - dy0
                bg = _dot(bmat, gst, NT)
                dx = _dot(t["m0"], dy0, TN) + _dot(t["m1"], dy1, TN) + bg * t["dte_p"]
                dm0, dm1 = _dot(dy0, t["x"], NT), _dot(dy1, t["x"], NT)
                dcb = dcb + dm0 * t["lam0"] + dm1 * t["lam1"]
                dye = dyp * t["ecs_p"]
                dc = dc + _dot(dye, hp)
                db = db + _dot(t["xd"], gst)
                i0 = jnp.sum(jnp.where(stril, _dot(triu, dm0 * t["m0"]), 0.0), axis=1, keepdims=True)
                i1 = jnp.sum(jnp.where(stril, _dot(triu, dm1 * t["m1"]), 0.0), axis=1, keepdims=True)
                yo0, yo1 = halves(dyp * t["yoff"], lo)
                w0, w1 = halves(t["xd"] * bg, lo)
                gh = gst * (hp * t["decay"])
                e0 = _sum0(jnp.sum(jnp.where(rlo, gh, 0.0), axis=1, keepdims=True))
                e1 = _sum0(jnp.sum(jnp.where(rlo, 0.0, gh), axis=1, keepdims=True))
                x0, x1 = halves(dx * xp, lo)
                oh0 = (lane_row == h0).astype(F32)
                oh1 = (lane_row == h0 + 1).astype(F32)
                i_all = i_all + i0 * oh0 + i1 * oh1
                yo_all = yo_all + yo0 * oh0 + yo1 * oh1
                w_all = w_all + w0 * oh0 + w1 * oh1
                e_row = e_row + e0 * oh0 + e1 * oh1
                ddt_x = ddt_x + x0 * oh0 + x1 * oh1
                dxs.append(dx * _pair_sel(dtv, h0, lo))
                g_scr[p4] = gst * t["decay"] + _dot(dye, cmat, TN)
            dbs.append(db + _dot(dcb, cmat, TN))
            dcs.append(dc + _dot(dcb, bmat))
        da = i_all + _dot_exact(triu, yo_all) + _dot_exact(stril.astype(F32), w_all) + e_row
        ddt_ref[...] = da * a_row_v + ddt_x
        da_ref[...] += _sum0(da * dtv)
        dxs_ref[...] = jnp.concatenate(dxs, axis=1)
        dbc_ref[...] = jnp.concatenate(dbs + dcs, axis=1)

    rev = lambda c: nc // per - 1 - c
    return _call_with_step(
        body, hosted, None, (xbc, xbc, xbc, dt, a_row, hs, dy), name=name, grid=(nc // per,),
        in_specs=[pl.BlockSpec((G, 512), lambda c: (rev(c), 0)), pl.BlockSpec((G, 256), lambda c: (rev(c), 2)),
                  pl.BlockSpec((G, 256), lambda c: (rev(c), 3)), pl.BlockSpec((G, 128), lambda c: (rev(c), 0)),
                  pl.BlockSpec((1, 128), lambda c: (0, 0)), pl.BlockSpec((per, 512, 128), lambda c: (rev(c), 0, 0)),
                  pl.BlockSpec((G, 512), lambda c: (rev(c), 0))],
        out_specs=[pl.BlockSpec((G, 512), lambda c: (rev(c), 0)), pl.BlockSpec((G, 512), lambda c: (rev(c), 0)),
                   pl.BlockSpec((G, 128), lambda c: (rev(c), 0)), pl.BlockSpec((1, 128), lambda c: (0, 0))],
        out_shape=[jax.ShapeDtypeStruct((S, 512), F32), jax.ShapeDtypeStruct((S, 512), F32),
                   jax.ShapeDtypeStruct((S, 128), F32), jax.ShapeDtypeStruct((1, 128), F32)],
        sem=("arbitrary",), scratch_shapes=[pltpu.VMEM((4, 128, 128), F32)])


HBM = pl.BlockSpec(memory_space=pl.ANY)


class _Step:
    def __init__(self, inputs, out_shapes, n_sems, start, finish, mid=None):
        self.inputs, self.out_shapes, self.n_sems = inputs, out_shapes, n_sems
        self.start, self.finish, self.mid = start, finish, mid
        self.alias = []


class _Shifted:
    def __init__(self, ref, off):
        self.ref, self.off = ref, off

    @property
    def at(self):
        return self

    def __getitem__(self, j):
        return self.ref.at[self.off + j]


def _merge_steps(steps):
    offs = [sum(s.n_sems for s in steps[:i]) for i in range(len(steps) + 1)]
    i_offs = [sum(len(s.inputs) for s in steps[:i]) for i in range(len(steps))]
    o_offs = [sum(len(s.out_shapes) for s in steps[:i]) for i in range(len(steps))]

    def phase(which):
        def run(ins, outs, sems):
            for s, off, i0, o0 in zip(steps, offs, i_offs, o_offs):
                fn = getattr(s, which)
                if fn is not None:
                    fn(ins[i0:i0 + len(s.inputs)], outs[o0:o0 + len(s.out_shapes)],
                       [_Shifted(sems[0], off), _Shifted(sems[1], off)])
        return run

    merged = _Step([a for s in steps for a in s.inputs], [o for s in steps for o in s.out_shapes], offs[-1],
                   phase("start"), phase("finish"), phase("mid") if any(s.mid for s in steps) else None)
    merged.alias = [(i0 + a, o0 + b) for s, i0, o0 in zip(steps, i_offs, o_offs) for a, b in s.alias]
    return merged


def _place():
    x, y, c = lax.axis_index("x"), lax.axis_index("y"), lax.axis_index("c")
    chips = [(1 - x, y), (x, 1 - y), (1 - x, 1 - y)]
    return x, y, c, chips


def _mesh_pos():
    return 2 * lax.axis_index("x") + lax.axis_index("y"), lax.axis_index("c")


def _chunks(rows, tile):
    return next(n for n in (4, 3, 2, 1) if rows % (n * tile) == 0)


def _remote(src, dst, sems, j, to):
    return pltpu.make_async_remote_copy(src_ref=src, dst_ref=dst, send_sem=sems[0].at[j], recv_sem=sems[1].at[j],
                                        device_id=to, device_id_type=MESH)


def _own_slot(wp):
    return lax.dynamic_update_slice(lax.empty((N_SHARD,) + wp.shape, wp.dtype), wp[None], (_mesh_pos()[0], 0, 0))


def _gather_step(buf):
    _, R, C = buf.shape
    H = R // 2
    nq = _chunks(H, 16)
    CH = H // nq

    def copies(ins, outs, sems):
        x, y, c, chips = _place()
        sib, me = (x, y, 1 - c), 2 * x + y
        w_ref, out_ref = ins[0], outs[0]

        def piece(k, hc, q):
            return out_ref.at[k, pl.ds(hc * H + q * CH, CH), :]

        sends, landed, fwds, fwd_landed = [], [], [], []
        for q in range(nq):
            for j, (px, py) in enumerate(chips):
                k = 2 * px + py
                sends.append(_remote(w_ref.at[me, pl.ds(c * H + q * CH, CH), :], piece(me, c, q), sems, j * nq + q,
                                     (px, py, c)))
                landed.append(_remote(piece(k, c, q), piece(k, c, q), sems, j * nq + q, (px, py, c)))
                fwds.append(_remote(piece(k, c, q), piece(k, c, q), sems, (3 + j) * nq + q, sib))
                fwd_landed.append(_remote(piece(k, 1 - c, q), piece(k, 1 - c, q), sems, (3 + j) * nq + q, sib))
        return sends, landed, fwds, fwd_landed

    def start(ins, outs, sems):
        for cp in copies(ins, outs, sems)[0]:
            cp.start()

    def mid(ins, outs, sems):
        _, landed, fwds, _ = copies(ins, outs, sems)
        for arrived, onward in zip(landed, fwds):
            arrived.wait_recv()
            onward.start()

    def finish(ins, outs, sems):
        sends, _, fwds, fwd_landed = copies(ins, outs, sems)
        for cp in fwd_landed:
            cp.wait_recv()
        for cp in sends + fwds:
            cp.wait_send()

    step = _Step([buf], [jax.ShapeDtypeStruct(buf.shape, buf.dtype)], 6 * nq, start, finish, mid)
    step.alias = [(0, 0)]
    return step


def _pair_exchange_step(gp):
    n, R, C = gp.shape
    H = R // 2
    nq = _chunks(H, 8)
    CH = H // nq

    def copies(ins, outs, sems):
        x, y, c, _ = _place()
        return [_remote(ins[0].at[k, pl.ds((1 - c) * H + q * CH, CH), :], outs[0].at[k, pl.ds(q * CH, CH), :], sems,
                        k * nq + q, (x, y, 1 - c)) for k in range(n) for q in range(nq)]

    def start(ins, outs, sems):
        for cp in copies(ins, outs, sems):
            cp.start()

    def finish(ins, outs, sems):
        for cp in copies(ins, outs, sems):
            cp.wait()

    return _Step([gp], [jax.ShapeDtypeStruct((n, H, C), gp.dtype)], n * nq, start, finish)


def _chip_exchange_step(pb):
    n, H, C = pb.shape
    nq = _chunks(H, 16)
    CH = H // nq

    def copies(ins, outs, sems):
        x, y, c, chips = _place()
        return [_remote(ins[0].at[2 * px + py, pl.ds(q * CH, CH), :], outs[0].at[j, pl.ds(q * CH, CH), :], sems,
                        j * nq + q, (px, py, c)) for q in range(nq) for j, (px, py) in enumerate(chips)]

    def start(ins, outs, sems):
        for cp in copies(ins, outs, sems):
            cp.start()

    def finish(ins, outs, sems):
        for cp in copies(ins, outs, sems):
            cp.wait()

    return _Step([pb], [jax.ShapeDtypeStruct((3, H, C), pb.dtype)], 3 * nq, start, finish)


def _pair_fill_step(red):
    R, C = red.shape
    H = R // 2
    nq = _chunks(H, 8)
    CH = H // nq

    def copies(ins, outs, sems):
        x, y, c, _ = _place()
        return [_remote(ins[0].at[pl.ds(c * H + j * CH, CH), :], outs[0].at[pl.ds(c * H + j * CH, CH), :], sems, j,
                        (x, y, 1 - c)) for j in range(nq)]

    def start(ins, outs, sems):
        for cp in copies(ins, outs, sems):
            cp.start()

    def finish(ins, outs, sems):
        for cp in copies(ins, outs, sems):
            cp.wait()

    step = _Step([red], [jax.ShapeDtypeStruct((R, C), red.dtype)], nq, start, finish)
    step.alias = [(0, 0)]
    return step


def _sem_scratch(step):
    return [pltpu.SemaphoreType.DMA((step.n_sems,)), pltpu.SemaphoreType.DMA((step.n_sems,))]


def _run_step(step, name):
    ni, no = len(step.inputs), len(step.out_shapes)

    def body(*refs):
        ins, outs, sems = refs[:ni], refs[ni:ni + no], refs[ni + no:]
        step.start(ins, outs, sems)
        if step.mid is not None:
            step.mid(ins, outs, sems)
        step.finish(ins, outs, sems)

    return pl.pallas_call(body, name=name, in_specs=[HBM] * ni, out_specs=[HBM] * no, out_shape=step.out_shapes,
                          input_output_aliases=dict(step.alias),
                          scratch_shapes=_sem_scratch(step))(*step.inputs)


def _grid_flags(grid):
    ids = [pl.program_id(d) for d in range(len(grid))]
    first = functools.reduce(lambda a, b: a & b, [i == 0 for i in ids])
    last = functools.reduce(lambda a, b: a & b, [i == n - 1 for i, n in zip(ids, grid)])
    return first, last, last


def _call_with_step(core, step, flags, args, *, name, grid, in_specs, out_specs, out_shape, sem, scratch_shapes=(),
                    aliases=None):
    aliases = aliases or {}
    if step is None:
        return pl.pallas_call(core, name=name, grid=grid, in_specs=in_specs, out_specs=out_specs,
                              out_shape=out_shape, scratch_shapes=list(scratch_shapes),
                              input_output_aliases=aliases, compiler_params=_params(sem))(*args)
    n_in, n_out, n_scr = len(in_specs), len(out_specs), len(scratch_shapes)
    si, so = len(step.inputs), len(step.out_shapes)
    flags = flags or (lambda: _grid_flags(grid))
    aliases = {**aliases, **{n_in + a: n_out + b for a, b in step.alias}}

    def body(*refs):
        ins, s_ins = refs[:n_in], refs[n_in:n_in + si]
        outs = refs[n_in + si:n_in + si + n_out]
        s_outs = refs[n_in + si + n_out:n_in + si + n_out + so]
        scr = refs[n_in + si + n_out + so:n_in + si + n_out + so + n_scr]
        sems = refs[n_in + si + n_out + so + n_scr:]
        first, middle, last = flags()

        @pl.when(first)
        def _():
            step.start(s_ins, s_outs, sems)

        if step.mid is not None:
            @pl.when(middle)
            def _():
                step.mid(s_ins, s_outs, sems)

        core(*ins, *outs, *scr)

        @pl.when(last)
        def _():
            step.finish(s_ins, s_outs, sems)

    return pl.pallas_call(
        body, name=name, grid=grid, in_specs=list(in_specs) + [HBM] * si, out_specs=list(out_specs) + [HBM] * so,
        out_shape=list(out_shape) + list(step.out_shapes), scratch_shapes=list(scratch_shapes) + _sem_scratch(step),
        input_output_aliases=aliases, compiler_params=_params(("arbitrary",) * len(grid)))(*args, *step.inputs)


def _attn_flags(nq):
    h, qi = pl.program_id(0), pl.program_id(1)
    return ((h == 0) & (qi == 0), (h == MLA_HEADS - 1) & (qi == 0), (h == MLA_HEADS - 1) & (qi == nq - 1))


def _att_mask(s_t, q0, k0):
    krow = k0 + lax.broadcasted_iota(jnp.int32, s_t.shape, 0)
    qcol = q0 + lax.broadcasted_iota(jnp.int32, s_t.shape, 1)
    return jnp.where(krow <= qcol, s_t, NEG)


def _loop_blocks(lo, hi, step, carry):
    n = hi - lo

    def four(i, c):
        kb = lo + 4 * i
        return step(kb + 3, step(kb + 2, step(kb + 1, step(kb, c))))

    carry = lax.fori_loop(0, n // 4, four, carry)
    base = lo + 4 * (n // 4)
    carry = lax.cond(n % 4 >= 2, lambda c: step(base + 1, step(base, c)), lambda c: c, carry)
    return lax.cond(n % 2 == 1, lambda c: step(hi - 1, c), lambda c: c, carry)


def _rows(ref, blk, t):
    return ref[pl.ds(pl.multiple_of(blk * t, t), t), :]


def _cols(ref, blk, t):
    return ref[:, pl.ds(pl.multiple_of(blk * t, t), t)]


def _attn_fwd(q, k, v_t, *, name, hosted=None):
    S = q.shape[0]
    t = min(ATTN_TILE, S)
    nq = S // t

    def body(q_ref, k_ref, vt_ref, o_ref, lse_ref):
        qi = pl.program_id(1)
        qv = q_ref[...]

        def absorb(kb, carry, masked):
            m, l, acc = carry
            s_t = lax.dot_general(_rows(k_ref, kb, t), qv, NT, preferred_element_type=F32)
            if masked:
                s_t = _att_mask(s_t, qi * t, kb * t)
            m_new = jnp.maximum(m, jnp.max(s_t, axis=0, keepdims=True))
            p_t = jnp.exp(s_t - m_new)
            corr = jnp.exp(m - m_new)
            return (m_new, corr * l + jnp.sum(p_t, axis=0, keepdims=True),
                    corr * acc + lax.dot_general(_cols(vt_ref, kb, t), p_t.astype(BF16), NN,
                                                 preferred_element_type=F32))

        init = (jnp.full((1, t), NEG, F32), jnp.zeros((1, t), F32), jnp.zeros((LANES, t), F32))
        carry = _loop_blocks(0, qi, lambda kb, c: absorb(kb, c, False), init)
        m, l, acc = absorb(qi, carry, True)
        o_ref[...] = acc / l
        lse_ref[0] = m + jnp.log(l)

    return _call_with_step(
        body, hosted, lambda: _attn_flags(nq), (q, k, v_t), name=name, grid=(MLA_HEADS, nq),
        in_specs=[pl.BlockSpec((t, LANES), lambda h, qi: (qi, h)),
                  pl.BlockSpec((S, LANES), lambda h, qi: (0, h)),
                  pl.BlockSpec((LANES, S), lambda h, qi: (h, 0))],
        out_specs=[pl.BlockSpec((LANES, t), lambda h, qi: (h, qi)),
                   pl.BlockSpec((1, 1, t), lambda h, qi: (h, 0, qi))],
        out_shape=[jax.ShapeDtypeStruct((MLA_HEADS * LANES, S), F32), jax.ShapeDtypeStruct((MLA_HEADS, 1, S), F32)],
        sem=("parallel", "arbitrary"))


def _attn_bwd(q, k, v, o_t, do_t, lse, *, name, hosted=None):
    S = q.shape[0]
    t = min(ATTN_TILE, S)
    nq = S // t

    def body(q_ref, k_ref, v_ref, o_ref, do_ref, lse_ref, dq_ref, dk_ref, dv_ref, dkt_scr):
        qi = pl.program_id(1)

        @pl.when(qi == 0)
        def _():
            dkt_scr[...] = jnp.zeros_like(dkt_scr)
            dv_ref[...] = jnp.zeros_like(dv_ref)

        qv = q_ref[...]
        q_t = qv.T
        dov = do_ref[...]
        delta = jnp.sum(dov * o_ref[...], axis=0, keepdims=True)
        dob = dov.astype(BF16)
        lse_v = lse_ref[0]

        def step(kb, acc, masked):
            kt = _rows(k_ref, kb, t)
            s_t = lax.dot_general(kt, qv, NT, preferred_element_type=F32)
            if masked:
                s_t = _att_mask(s_t, qi * t, kb * t)
            p_t = jnp.exp(s_t - lse_v)
            dp_t = lax.dot_general(_rows(v_ref, kb, t), dob, NN, preferred_element_type=F32)
            ds_t = (p_t * (dp_t - delta)).astype(BF16)
            keys = pl.ds(pl.multiple_of(kb * t, t), t)
            dv_ref[:, keys] += lax.dot_general(dob, p_t.astype(BF16), NT, preferred_element_type=F32)
            dkt_scr[:, keys] += lax.dot_general(q_t, ds_t, NT, preferred_element_type=F32)
            return acc + lax.dot_general(kt, ds_t, TN, preferred_element_type=F32)

        acc = _loop_blocks(0, qi, lambda kb, c: step(kb, c, False), jnp.zeros((LANES, t), F32))
        dq_ref[...] = step(qi, acc, True).T

        @pl.when(qi == nq - 1)
        def _():
            dk_ref[...] = dkt_scr[...].T

    tile = pl.BlockSpec((t, LANES), lambda h, qi: (qi, h))
    tile_t = pl.BlockSpec((LANES, t), lambda h, qi: (h, qi))
    stat = pl.BlockSpec((1, 1, t), lambda h, qi: (h, 0, qi))
    seq = pl.BlockSpec((S, LANES), lambda h, qi: (0, h))
    seq_t = pl.BlockSpec((LANES, S), lambda h, qi: (h, 0))
    return _call_with_step(
        body, hosted, lambda: _attn_flags(nq), (q, k, v, o_t, do_t, lse), name=name, grid=(MLA_HEADS, nq),
        in_specs=[tile, seq, seq, tile_t, tile_t, stat],
        out_specs=[tile, seq, seq_t],
        out_shape=[jax.ShapeDtypeStruct((S, MLA_HEADS * LANES), F32), jax.ShapeDtypeStruct((S, MLA_HEADS * LANES), F32),
                   jax.ShapeDtypeStruct((MLA_HEADS * LANES, S), F32)],
        sem=("parallel", "arbitrary"), scratch_shapes=[pltpu.VMEM((LANES, S), F32)])


def _fn_ln(ctx, x, g, b):
    xhat, _ = _ln_stats(x)
    y = xhat * g + b
    return y, y


def _fn_conv_fwd(ctx, u, up, dtr, w8, cb, dtb):
    first = ctx.i == 0
    y = u * w8[3:4] + cb
    for s in (1, 2, 3):
        y = y + _shift_down(u, up, s, first) * w8[3 - s:4 - s]
    act = y * _sigmoid(y)
    v = dtr + dtb
    e = jnp.exp(-jnp.abs(v))
    one_p = 1.0 + e
    log1p = jnp.where(one_p == 1.0, e, jnp.log(one_p) * e / (one_p - 1.0))
    return y, act, jnp.maximum(v, 0.0) + log1p


def _fn_ssd_post(ctx, y, xs, z, dexp, g):
    yg = (y + xs * dexp) * (z * _sigmoid(z))
    outs = []
    for k in range(2):
        v = yg[:, 256 * k:256 * (k + 1)]
        outs.append(v * lax.rsqrt(_mean1(v * v) + RMS_EPS))
    return (jnp.concatenate(outs, axis=1) * g,)


def _fn_ssd_post_bwd(ctx, dyn, y, xs, z, dexp, g):
    yt = y + xs * dexp
    sig = _sigmoid(z)
    sz = z * sig
    yg = yt * sz
    dyh = dyn * g
    yh, dyg = [], []
    for k in range(2):
        sl = slice(256 * k, 256 * (k + 1))
        v = yg[:, sl]
        rs = lax.rsqrt(_mean1(v * v) + RMS_EPS)
        vh = v * rs
        yh.append(vh)
        dyg.append(rs * (dyh[:, sl] - vh * _mean1(dyh[:, sl] * vh)))
    yh = jnp.concatenate(yh, axis=1)
    dyg = jnp.concatenate(dyg, axis=1)
    dyt = dyg * sz
    dz = dyg * yt * (sig * (1.0 + z * (1.0 - sig)))
    return dyt, dz, dyt * dexp, _sum0(dyt * xs), _sum0(dyn * yh)


def _fn_mla_pre(ctx, ql, kvl, gq, gkv):
    return _rms_fwd(ql, gq), _rms_fwd(kvl, gkv)


def _fn_mla_pre_bwd(ctx, ql, kvl, dqn, dkvn_k, dkvn_v, ddtr, gq, gkv):
    dql, dgq = _rms_bwd(ql, dqn, gq)
    dkvl, dgkv = _rms_bwd(kvl, dkvn_k + dkvn_v, gkv)
    return jnp.concatenate([dql, ddtr.astype(F32), dkvl], axis=1), dgq, dgkv


def _fn_rope(ctx, qp, kn, kr, trig):
    ta, tb, tc = _rope_tables(trig)
    kpe = _rope(kr, ta, tb, tc)
    qs, ks = [], []
    for h in range(MLA_HEADS):
        sl = slice(128 * h, 128 * (h + 1))
        qs.append(_rope(qp[:, sl], ta, tb, tc) * MLA_SCALE)
        ks.append(kn[:, sl] + kpe)
    return jnp.concatenate(qs, axis=1), jnp.concatenate(ks, axis=1)


def _fn_rope_bwd(ctx, dq, dk, trig):
    ta, tb, tc = _rope_tables(trig)
    qs = []
    ksum = jnp.zeros_like(ta)
    for h in range(MLA_HEADS):
        sl = slice(128 * h, 128 * (h + 1))
        qs.append(_rope_bwd(dq[:, sl] * MLA_SCALE, ta, tb, tc))
        ksum = ksum + dk[:, sl]
    lane = _lane(ksum.shape)
    dkr = jnp.where((lane >= 64) & (lane < 96), _rope_bwd(ksum, ta, tb, tc), 0.0)
    return jnp.concatenate(qs, axis=1), jnp.concatenate([dkr, jnp.zeros_like(dkr)], axis=1)


def _mem_probs(qh, kh):
    s = _dot(qh, kh, NT) * MEM_SCALE
    p = jnp.exp(s - jnp.max(s, axis=1, keepdims=True))
    return p / jnp.sum(p, axis=1, keepdims=True)


def _fn_mem_fwd(ctx, q, km, vm):
    outs = []
    for h in range(MEM_HEADS):
        sl = slice(256 * h, 256 * (h + 1))
        outs.append(_dot(_mem_probs(q[:, sl], km[:, sl]), vm[:, sl]))
    return (jnp.concatenate(outs, axis=1),)


def _fn_mem_bwd(ctx, q, do, km, vm):
    dqs, dks, dvs = [], [], []
    for h in range(MEM_HEADS):
        sl = slice(256 * h, 256 * (h + 1))
        p = _mem_probs(q[:, sl], km[:, sl])
        dvs.append(_dot(p, do[:, sl], TN))
        dp = _dot(do[:, sl], vm[:, sl], NT)
        ds = p * (dp - jnp.sum(dp * p, axis=1, keepdims=True)) * MEM_SCALE
        dqs.append(_dot(ds, km[:, sl]))
        dks.append(_dot(ds, q[:, sl], TN))
    return jnp.concatenate(dqs, axis=1), jnp.concatenate(dks, axis=1), jnp.concatenate(dvs, axis=1)


def _fn_res_ln(ctx, h, r, g, b):
    xhat, _ = _ln_stats(ALPHA * h + r)
    y = xhat * g + b
    return y, y


def _fn_res_ln_bwd(ctx, h, r, d1, d2, g):
    xhat, rstd = _ln_stats(ALPHA * h + r)
    return _ln_bwd(xhat, rstd, ALPHA * d1 + d2, g)


def _fn_res2_ln(ctx, h, r1, r2, g, b):
    xhat, _ = _ln_stats(ALPHA * h + (r1 + r2))
    return (xhat * g + b,)


def _fn_res2_ln_bwd(ctx, h, r1, r2, d1, d2, g):
    xhat, rstd = _ln_stats(ALPHA * h + (r1 + r2))
    return _ln_bwd(xhat, rstd, ALPHA * d1 + d2, g)


def _fn_in_ln_bwd(ctx, x, d1, d2, g):
    xhat, rstd = _ln_stats(x)
    return _ln_bwd(xhat, rstd, ALPHA * d1 + d2, g)


def _fn_final(ctx, h2, ff, tgt, g, b):
    xhat, rstd = _ln_stats(ALPHA * h2 + ff)
    e = xhat * g + b - tgt
    loss = 0.5 * _sum0(jnp.sum(e * e, axis=1, keepdims=True)) / D_MODEL
    dx, dg, db = _ln_bwd(xhat, rstd, e / D_MODEL, g)
    return dx, dx, dg, db, loss


def _epi_du(da, u):
    return da * 2.0 * jnp.maximum(u.astype(F32), 0.0)


def _relu2(u):
    r = jnp.maximum(u.astype(F32), 0.0)
    return r * r


def _fn_conv_bwd_a(ctx, y, dxs1, dxs2, dbc, dtr, ddt, dtb):
    sig = _sigmoid(y)
    dact = jnp.concatenate([dxs1 + dxs2, dbc], axis=1)
    dyc = dact * (sig * (1.0 + y * (1.0 - sig)))
    ddtr = ddt * _sigmoid(dtr + dtb)
    return dyc, ddtr, _sum0(dyc), _sum0(ddtr)


def _fn_conv_bwd_b(ctx, d, dn, u, up, w8):
    first, last = ctx.i == 0, ctx.i == ctx.n - 1
    du = d * w8[3:4]
    row = lax.broadcasted_iota(jnp.int32, w8.shape, 0)
    dw = jnp.where(row == 3, _sum0(d * u), 0.0)
    for s in (1, 2, 3):
        du = du + _shift_up(d, dn, s, last) * w8[3 - s:4 - s]
        dw = dw + jnp.where(row == 3 - s, _sum0(d * _shift_down(u, up, s, first)), 0.0)
    return du, dw


def _fn_adam(ctx, w, g, m, v):
    m = ADAM_B1 * m + (1.0 - ADAM_B1) * g
    v = ADAM_B2 * v + (1.0 - ADAM_B2) * (g * g)
    m_hat = m / (1.0 - ADAM_B1 ** ADAM_STEP)
    v_hat = v / (1.0 - ADAM_B2 ** ADAM_STEP)
    return -ADAM_LR * (m_hat / (jnp.sqrt(v_hat) + ADAM_EPS) + ADAM_WD * w), m, v


def _z(r, c, dt):
    return jnp.zeros((r, c), dt)


W_IN_SHARD = 554
W_IN_GROUPS = [(0, 512, 1024), (512, 1536, 0), (1536, 1544, 1920), (1544, 1928, 1536), (1928, 2184, 2048),
               (2184, 2216, 2368)]


def _pad_w_in(ws):
    r, dt = ws.shape[1], ws.dtype

    def cols(a, b):
        out = []
        for k in range(N_SHARD):
            lo, hi = max(a, k * W_IN_SHARD), min(b, (k + 1) * W_IN_SHARD)
            if lo < hi:
                out.append(ws[k][:, lo - k * W_IN_SHARD:hi - k * W_IN_SHARD])
        return out

    return jnp.concatenate(cols(512, 1536) + cols(0, 512) + cols(1544, 1928) + cols(1536, 1544) + [_z(r, 120, dt)]
                           + cols(1928, 2184) + [_z(r, 64, dt)] + cols(2184, 2216) + [_z(r, 32, dt), _z(r, 128, dt)],
                           axis=1)


def _unpad_w_in(d):
    shards = []
    for k in range(N_SHARD):
        a, b = k * W_IN_SHARD, (k + 1) * W_IN_SHARD
        parts = []
        for o0, o1, p0 in W_IN_GROUPS:
            lo, hi = max(a, o0), min(b, o1)
            if lo < hi:
                parts.append(d[:, p0 + lo - o0:p0 + hi - o0])
        shards.append(jnp.concatenate(parts, axis=1))
    return jnp.stack(shards)


def _pad_heads(w, width):
    r = w.shape[0]
    w3 = w.reshape(r, MLA_HEADS, width)
    return jnp.pad(w3, ((0, 0), (0, 0), (0, 128 - width))).reshape(r, MLA_HEADS * 128)


def _row(v, width=None):
    v = v.reshape(1, -1).astype(F32)
    if width is not None and v.shape[1] < width:
        v = jnp.pad(v, ((0, 0), (0, width - v.shape[1])))
    return v


BIG = {
    "w_in": (1024, 2216, 1), "w_q_up": (384, 768, 1), "w_kv_up": (256, 1024, 1), "w_mix_out": (1024, 1024, 0),
    "w_mem_q": (1024, 1024, 0), "w_mem_k": (1024, 1024, 0), "w_mem_v": (1024, 1024, 0), "w_mem_o": (1024, 1024, 0),
    "w_up": (1024, 4096, 1), "w_down": (4096, 1024, 0), "conv_w": (4, 1024, 1),
}
BIG_ORDER = list(BIG)
SMALL_ORDER = ["ln_in_g", "ln_in_b", "conv_b", "dt_bias", "a_log", "d_skip", "ssd_norm_g", "q_norm_g", "kv_norm_g",
               "ln1_g", "ln1_b", "ln2_g", "ln2_b", "ln3_g", "ln3_b"]
N_SHARD = 4
N_DEV = 8
PACK_COLS = 1024
PACK_A_ROW = {"w_down": 0, "w_up": 1024, "w_mem_q": 2048, "w_mem_k": 2304, "w_mem_v": 2560, "w_mem_o": 2816,
              "w_mix_out": 3072}
PACK_A_ORDER = list(PACK_A_ROW)
PACK_A_ROWS = 3328
PACK_B_ORDER = ["w_q_up", "w_kv_up", "conv_w"]
PACK_B_ROWS = 160


def _shard_shape(name):
    r, c, ax = BIG[name]
    return (r // N_SHARD, c) if ax == 0 else (r, c // N_SHARD)


def _split_shards(name, full):
    r, c, ax = BIG[name]
    if ax == 0:
        return full.reshape(N_SHARD, -1)
    return full.reshape(r, N_SHARD, c // N_SHARD).transpose(1, 0, 2).reshape(N_SHARD, -1)


def _join_shards(name, parts):
    r, c, ax = BIG[name]
    if ax == 0:
        return parts.reshape(r, c)
    return parts.reshape(N_SHARD, r, c // N_SHARD).transpose(1, 0, 2).reshape(r, c)


def _small_all_reduce(g, step=None):
    r, cdim = g.shape
    si, so = (len(step.inputs), len(step.out_shapes)) if step else (0, 0)

    def body(g_ref, *refs):
        s_ins, out_ref, s_outs = refs[:si], refs[si], refs[si + 1:si + 1 + so]
        buf, send_sems, recv_sems = refs[si + 1 + so:si + 4 + so]
        s_sems = refs[si + 4 + so:]
        if step:
            step.start(s_ins, s_outs, s_sems)
        x, y, c, _ = _place()
        me = 4 * x + 2 * y + c
        buf[me] = g_ref[...]
        copies = []
        for d in range(1, N_DEV):
            to = me ^ d
            cp = pltpu.make_async_remote_copy(src_ref=g_ref, dst_ref=buf.at[me], send_sem=send_sems.at[d - 1],
                                              recv_sem=recv_sems.at[d - 1],
                                              device_id=(to // 4, (to // 2) % 2, to % 2), device_id_type=MESH)
            cp.start()
            copies.append(cp)
        for cp in copies:
            cp.wait()
        acc = buf[0]
        for d in range(1, N_DEV):
            acc = acc + buf[d]
        out_ref[...] = acc
        if step:
            step.finish(s_ins, s_outs, s_sems)

    res = pl.pallas_call(
        body, name="small_all_reduce",
        in_specs=[pl.BlockSpec(memory_space=pltpu.VMEM)] + [HBM] * si,
        out_specs=[pl.BlockSpec(memory_space=pltpu.VMEM)] + [HBM] * so,
        out_shape=[jax.ShapeDtypeStruct((r, cdim), F32)] + (list(step.out_shapes) if step else []),
        scratch_shapes=[pltpu.VMEM((N_DEV, r, cdim), F32), pltpu.SemaphoreType.DMA((N_DEV - 1,)),
                        pltpu.SemaphoreType.DMA((N_DEV - 1,))] + (_sem_scratch(step) if step else []),
    )(g, *(step.inputs if step else []))
    return res if step else res[0]


def _half_tile(h):
    return next(t for t in range(512, 0, -16) if h % t == 0)


def _pair_sum(gp, theirs, name):
    n, R, C = gp.shape
    H = R // 2
    tr = _half_tile(H)
    nb = H // tr

    def body(s_ref, g_ref, t_ref, o_ref):
        o_ref[...] = (g_ref[...] + t_ref[...]).astype(o_ref.dtype)

    def shard(k, s):
        return k + (k >= s[1]).astype(jnp.int32)

    me, c = _mesh_pos()
    return pl.pallas_call(
        body, name=name,
        grid_spec=pltpu.PrefetchScalarGridSpec(
            num_scalar_prefetch=1, grid=(n - 1, nb),
            in_specs=[pl.BlockSpec((1, tr, C), lambda k, i, s: (shard(k, s), s[0] * nb + i, 0)),
                      pl.BlockSpec((1, tr, C), lambda k, i, s: (shard(k, s), i, 0))],
            out_specs=pl.BlockSpec((1, tr, C), lambda k, i, s: (shard(k, s), i, 0))),
        out_shape=jax.ShapeDtypeStruct((n, H, C), BF16), compiler_params=_params(("arbitrary", "arbitrary")),
    )(jnp.stack([c, me]).astype(jnp.int32), gp, theirs)


def _chip_sum(gp, theirs, got, name):
    n, R, C = gp.shape
    H = R // 2
    tr = _half_tile(H)
    nb = H // tr

    def body(s_ref, g_ref, t_ref, r_ref, o_ref):
        acc = g_ref[0] + t_ref[0]
        for j in range(3):
            acc = acc + r_ref[j].astype(F32)
        o_ref[...] = acc

    me, c = _mesh_pos()
    return pl.pallas_call(
        body, name=name,
        grid_spec=pltpu.PrefetchScalarGridSpec(
            num_scalar_prefetch=1, grid=(nb,),
            in_specs=[pl.BlockSpec((1, tr, C), lambda i, s: (s[0], s[1] * nb + i, 0)),
                      pl.BlockSpec((1, tr, C), lambda i, s: (s[0], i, 0)),
                      pl.BlockSpec((3, tr, C), lambda i, s: (0, i, 0))],
            out_specs=pl.BlockSpec((tr, C), lambda i, s: (s[1] * nb + i, 0))),
        out_shape=jax.ShapeDtypeStruct((R, C), F32), compiler_params=_params(("arbitrary",)),
    )(jnp.stack([me, c]).astype(jnp.int32), gp, theirs, got)


def _unpack_group_b(g_c, g_b):
    g_b = g_b.reshape(N_SHARD, -1)
    WB, off = {"w_in": g_c}, 0
    for n in PACK_B_ORDER:
        sr, sc = _shard_shape(n)
        cnt = sr * sc
        if n == "conv_w":
            part = lax.bitcast_convert_type(g_b[:, off:off + 2 * cnt].reshape(N_SHARD, cnt, 2), F32)
            off += 2 * cnt
        else:
            part = g_b[:, off:off + cnt]
            off += cnt
        WB[n] = _join_shards(n, part)
    return WB


def _group_b_grads(dw_in_p, dw_q_p, dw_k_p, dw_v_pt, dconv_w8):
    return {
        "w_in": _unpad_w_in(dw_in_p),
        "w_q_up": dw_q_p.reshape(384, MLA_HEADS, 128)[:, :, :MLA_QK].reshape(384, MLA_HEADS * MLA_QK),
        "w_kv_up": jnp.concatenate([dw_k_p.reshape(MLA_KV_RANK, MLA_HEADS, 128)[:, :, :64],
                                    dw_v_pt.T.reshape(MLA_KV_RANK, MLA_HEADS, 128)[:, :, :64]], axis=2).reshape(
                                        MLA_KV_RANK, MLA_HEADS * 128),
        "conv_w": dconv_w8[0:4],
    }


def _pack_group_b(big_b):
    rows = [_split_shards(n, big_b[n]).reshape(N_SHARD, -1, PACK_COLS) for n in PACK_B_ORDER]
    used = sum(f.shape[1] for f in rows)
    rows.append(jnp.zeros((N_SHARD, PACK_B_ROWS - used, PACK_COLS), F32))
    return big_b["w_in"], jnp.concatenate(rows, axis=1)


def _local_step(x, mem, positions, target, WB, P, *, wp_a=None, g_a=None, wp_b=None):
    S = x.shape[0]
    tr = ROW_TILE
    dist = g_a is None
    g_in, b_in = _row(P["ln_in_g"]), _row(P["ln_in_b"])
    res = _rowwise(_fn_ln, [x], [g_in, b_in], [D_MODEL, (D_MODEL, BF16)], tr=tr, name="ln_in",
                   hosted=_merge_steps([_gather_step(_own_slot(w)) for w in wp_b]) if dist else None)
    h0, h0_b = res[0], res[1]
    if dist:
        WB = _unpack_group_b(res[2], res[3])
    P = {**P, "conv_w": WB["conv_w"]}
    w_in_p = _pad_w_in(WB["w_in"])
    w_q_p = _pad_heads(WB["w_q_up"], MLA_QK)
    w_kv3 = WB["w_kv_up"].reshape(MLA_KV_RANK, MLA_HEADS, 128)
    w_k_p = _pad_heads(w_kv3[:, :, :64].reshape(MLA_KV_RANK, 512), 64)
    w_v_p = _pad_heads(w_kv3[:, :, 64:].reshape(MLA_KV_RANK, 512), 64)
    w_v_pt = w_v_p.T
    conv_w8 = jnp.pad(P["conv_w"].astype(F32), ((0, 4), (0, 0)))
    conv_b = _row(P["conv_b"])
    dt_b = _row(P["dt_bias"], 128)
    a_head = -jnp.exp(P["a_log"].reshape(-1).astype(F32))
    a_row = _row(a_head, 128)
    dexp = jnp.repeat(P["d_skip"].reshape(-1).astype(F32), 64).reshape(1, 512)
    g_ssd, g_q, g_kv = _row(P["ssd_norm_g"]), _row(P["q_norm_g"]), _row(P["kv_norm_g"])
    g1, b1, g2, b2, g3, b3 = (_row(P[k]) for k in ("ln1_g", "ln1_b", "ln2_g", "ln2_b", "ln3_g", "ln3_b"))

    half = MLA_ROPE // 2
    inv_freq = jnp.power(ROPE_THETA, -jnp.arange(half, dtype=F32) / half)
    ang = inv_freq.reshape(half, 1) * positions.reshape(1, S).astype(F32)
    trig = ("cols", jnp.concatenate([jnp.cos(ang), jnp.sin(ang)], axis=0))

    proj = _mm(h0_b, w_in_p, form="nn", tn=IN_W // 2, name="mm_in")
    conv_y, xbc, dt = _rowwise(
        _fn_conv_fwd, [(proj,) + SEG_XBC, ("prev", proj) + SEG_XBC, (proj,) + SEG_DT], [conv_w8, conv_b, dt_b],
        [1024, 1024, 128], tr=tr, name="conv_fwd")
    y_ssd, hs = _ssd_fwd(xbc, dt, a_row, name="ssd_fwd")
    (y_n,) = _rowwise(_fn_ssd_post, [y_ssd, (xbc, 0, 512), (proj,) + SEG_Z], [dexp, g_ssd], [(512, BF16)], tr=tr,
                      name="ssd_post")
    q_n, kv_n = _rowwise(_fn_mla_pre, [(proj,) + SEG_QLAT, (proj,) + SEG_KVLAT], [g_q, g_kv], [384, 256], tr=tr,
                         name="mla_pre")
    qp = _mm(q_n, w_q_p, form="nn", name="mm_q_up")
    kn = _mm(kv_n, w_k_p, form="nn", name="mm_k_up")
    v_nat = _mm(kv_n, w_v_p, form="nn", out_dtype=BF16, name="mm_v_up")
    v_t = _mm(w_v_pt, kv_n, form="nt", out_dtype=BF16, name="mm_v_up_t")
    q_rot, k_full = _rowwise(_fn_rope, [qp, kn, (proj,) + SEG_KR, trig], [],
                             [(1024, BF16), (1024, BF16)], tr=tr, name="rope")
    res = _attn_fwd(q_rot, k_full, v_t, name="attn_fwd", hosted=_gather_step(_own_slot(wp_a)) if dist else None)
    o_t, lse = res[0], res[1]
    if dist:
        g_a = res[2]
    r_mix = PACK_A_ROW["w_mix_out"]
    w_mix_o = jnp.pad(g_a[2:4, r_mix:r_mix + 256].reshape(MLA_HEADS, 64, D_MODEL),
                      ((0, 0), (0, 64), (0, 0))).reshape(MLA_HEADS * 128, D_MODEL)
    mix_o = _mm(o_t, w_mix_o, form="tn", name="mm_mix_o")
    mix_y = _mm(y_n, g_a, form="nn", b_pack="w_mix_out", name="mm_mix_y")
    (h1,) = _rowwise(_fn_res2_ln, [h0, mix_o, mix_y], [g1, b1], [D_MODEL], tr=tr, name="ln1")
    qm = _mm(h1, g_a, form="nn", b_pack="w_mem_q", out_dtype=BF16, name="mm_mem_q")
    km = _mm(mem, g_a, form="nn", b_pack="w_mem_k", out_dtype=BF16, name="mm_mem_k")
    vm = _mm(mem, g_a, form="nn", b_pack="w_mem_v", out_dtype=BF16, name="mm_mem_v")
    (om,) = _rowwise(_fn_mem_fwd, [qm], [km, vm], [(D_MODEL, BF16)], tr=tr, name="mem_fwd")
    xa = _mm(om, g_a, form="nn", b_pack="w_mem_o", name="mm_mem_o")
    h2, h2_b = _rowwise(_fn_res_ln, [h1, xa], [g2, b2], [D_MODEL, (D_MODEL, BF16)], tr=tr, name="ln2")
    u = _mm(h2_b, g_a, form="nn", b_pack="w_up", out_dtype=BF16, name="mm_up")
    ff = _mm(u, g_a, form="nn", a_pro=_relu2, b_pack="w_down", name="mm_down")

    gp = lax.empty((N_SHARD, PACK_A_ROWS, PACK_COLS), F32)
    dt3, dt3_b, dg3, db3, loss = _rowwise(_fn_final, [h2, ff, target], [g3, b3], [D_MODEL, (D_MODEL, BF16)],
                                          [(1, D_MODEL), (1, D_MODEL), (1, 128)], tr=tr, name="ln3_loss")
    du = _mm(dt3_b, g_a, form="nt", b_pack="w_down", epi=(_epi_du, u), out_dtype=BF16, name="mm_down_dx")
    gp = _mm(u, dt3_b, form="tn", a_pro=_relu2, out_pack=("w_down", gp), name="mm_down_dw")
    gp = _mm(h2_b, du, form="tn", out_pack=("w_up", gp), name="mm_up_dw")
    dh2 = _mm(du, g_a, form="nt", b_pack="w_up", name="mm_up_dx")
    dt2, dg2, db2 = _rowwise(_fn_res_ln_bwd, [h1, xa, dt3, dh2], [g2], [D_MODEL], [(1, D_MODEL)] * 2, tr=tr,
                             name="ln2_bwd")
    dom = _mm(dt2, g_a, form="nt", b_pack="w_mem_o", out_dtype=BF16, name="mm_mem_o_dx")
    gp = _mm(om, dt2, form="tn", out_pack=("w_mem_o", gp), name="mm_mem_o_dw")
    dqm, dkm, dvm = _rowwise(_fn_mem_bwd, [qm, dom], [km, vm], [(D_MODEL, BF16)], [(256, D_MODEL)] * 2, tr=tr,
                             name="mem_bwd")
    gp = _mm(h1, dqm, form="tn", out_pack=("w_mem_q", gp), name="mm_mem_q_dw")
    gp = _mm(mem, dkm, form="tn", out_pack=("w_mem_k", gp), name="mm_mem_k_dw")
    gp = _mm(mem, dvm, form="tn", out_pack=("w_mem_v", gp), name="mm_mem_v_dw")
    dh1 = _mm(dqm, g_a, form="nt", b_pack="w_mem_q", name="mm_mem_q_dx")
    dt1, dg1, db1 = _rowwise(_fn_res2_ln_bwd, [h0, mix_o, mix_y, dt2, dh1], [g1], [D_MODEL], [(1, D_MODEL)] * 2,
                             tr=tr, name="ln1_bwd")
    do_t = _mm(w_mix_o, dt1, form="nt", name="mm_mix_o_dx")
    dy_n = _mm(dt1, g_a, form="nt", b_pack="w_mix_out", b_rows=512, name="mm_mix_y_dx")
    dw_mix_o = _mm(o_t, dt1, form="nn", name="mm_mix_o_dw")
    gp = _mm(y_n, dt1, form="tn", out_pack=("w_mix_out", gp), name="mm_mix_y_dw")
    gp = lax.dynamic_update_slice(
        gp, dw_mix_o.reshape(MLA_HEADS, 128, D_MODEL)[:, :64].reshape(2, 256, D_MODEL), (2, r_mix, 0))
    dproj = lax.empty((S, IN_W), BF16)
    dy_ssd, dproj, dxs_skip, ddexp, dg_ssd = _rowwise(
        _fn_ssd_post_bwd, [dy_n, y_ssd, (xbc, 0, 512), (proj,) + SEG_Z], [dexp, g_ssd],
        [512, (512, dproj, SEG_Z[0]), 512], [(1, 512)] * 2, tr=tr, name="ssd_post_bwd")
    res = _ssd_bwd(xbc, dt, a_row, hs, dy_ssd, name="ssd_bwd", hosted=_pair_exchange_step(gp) if dist else None)
    dxs, dbc, ddt, da_head = res[0], res[1], res[2], res[3]
    chip_step = None
    if dist:
        theirs_a = res[4]
        chip_step = _chip_exchange_step(_pair_sum(gp, theirs_a, "pair_sum_a"))
    res = _attn_bwd(q_rot, k_full, v_nat, o_t, do_t, lse, name="attn_bwd", hosted=chip_step)
    dq_rot, dk, dv_t = res[0], res[1], res[2]
    if dist:
        gp = _chip_sum(gp, theirs_a, res[3], "chip_sum_a")
    dqp, dproj = _rowwise(_fn_rope_bwd, [dq_rot, dk, trig], [], [(1024, BF16), (256, dproj, SEG_KR[0])], tr=tr,
                          name="rope_bwd")
    dw_q_p = _mm(q_n, dqp, form="tn", name="mm_q_up_dw")
    dq_n = _mm(dqp, w_q_p, form="nt", name="mm_q_up_dx")
    dw_k_p = _mm(kv_n, dk, form="tn", name="mm_k_up_dw")
    dkv_n1 = _mm(dk, w_k_p, form="nt", name="mm_k_up_dx")
    dw_v_pt = _mm(dv_t, kv_n, form="nn", name="mm_v_up_dw")
    dkv_n2 = _mm(dv_t, w_v_pt, form="tn", name="mm_v_up_dx")
    dyc, ddtr, dconv_b, ddt_b = _rowwise(
        _fn_conv_bwd_a, [conv_y, dxs, dxs_skip, dbc, (proj,) + SEG_DT, ddt], [dt_b], [1024, (128, BF16)],
        [(1, 1024), (1, 128)], tr=tr, name="conv_bwd_a")
    dproj, dg_q, dg_kv = _rowwise(
        _fn_mla_pre_bwd, [(proj,) + SEG_QLAT, (proj,) + SEG_KVLAT, dq_n, dkv_n1, dkv_n2, ddtr], [g_q, g_kv],
        [(SEG_KR[0] - SEG_QLAT[0], dproj, SEG_QLAT[0])], [(1, 384), (1, 256)], tr=tr, name="mla_pre_bwd")
    dproj, dconv_w8 = _rowwise(
        _fn_conv_bwd_b, [dyc, ("next", dyc, 0, 1024), (proj,) + SEG_XBC, ("prev", proj) + SEG_XBC], [conv_w8],
        [(1024, dproj, SEG_XBC[0])], [(8, 1024)], tr=tr, name="conv_bwd_b")
    res = _mm(h0_b, dproj, form="tn", tn=IN_W // 2, name="mm_in_dw", hosted=_pair_fill_step(gp) if dist else None)
    dw_in_p, red_a = (res[0], res[1]) if dist else (res, None)
    big_b = _group_b_grads(dw_in_p, dw_q_p, dw_k_p, dw_v_pt, dconv_w8)
    q_b = None
    if dist:
        gp_c, gp_b = _pack_group_b(big_b)
        dh0, theirs_c, theirs_b = _mm(dproj, w_in_p, form="nt", tk=IN_W // 2, name="mm_in_dx", hosted=_merge_steps(
            [_pair_exchange_step(gp_c), _pair_exchange_step(gp_b)]))
        q_b = ((gp_c, theirs_c, _pair_sum(gp_c, theirs_c, "pair_sum_w_in")),
               (gp_b, theirs_b, _pair_sum(gp_b, theirs_b, "pair_sum_b")))
        gp = red_a
    else:
        dh0 = _mm(dproj, w_in_p, form="nt", tk=IN_W // 2, name="mm_in_dx")
    grad_x, dg_in, db_in = _rowwise(_fn_in_ln_bwd, [x, dt1, dh0], [g_in], [D_MODEL], [(1, D_MODEL)] * 2, tr=tr,
                                    name="ln_in_bwd")

    small = {
        "ln_in_g": dg_in, "ln_in_b": db_in, "conv_b": dconv_b, "dt_bias": ddt_b[:, :8],
        "a_log": da_head[:, :8] * a_head.reshape(1, 8),
        "d_skip": ddexp.reshape(8, 64).sum(axis=1).reshape(1, 8),
        "ssd_norm_g": dg_ssd, "q_norm_g": dg_q, "kv_norm_g": dg_kv,
        "ln1_g": dg1, "ln1_b": db1, "ln2_g": dg2, "ln2_b": db2, "ln3_g": dg3, "ln3_b": db3,
    }
    return loss[0, 0], grad_x, (gp, q_b), big_b, small


def _adam(w, g, m, v, name):
    shape = w.shape
    w2, m2, v2 = (t.reshape(-1, shape[-1]) for t in (w, m, v))
    if isinstance(g, tuple):
        fn = lambda ctx, wv, gv, mv, vv: (*_fn_adam(ctx, wv, gv, mv, vv), gv)
        d, mn, vn, g = _rowwise(fn, [w2, (g[0], 0, shape[-1], g[1]), m2, v2], [], [shape[-1]] * 4, tr=ROW_TILE,
                                name=name)
    else:
        d, mn, vn = _rowwise(_fn_adam, [w2, g.reshape(-1, shape[-1]), m2, v2], [], [shape[-1]] * 3, tr=ROW_TILE,
                             name=name)
    return g.reshape(shape), d.reshape(shape), mn.reshape(shape), vn.reshape(shape)


def _adam_columns(w, g, m, v, name):
    cols = w.shape[0]
    step = cols // 2 if cols % 2 == 0 else cols
    blk = pl.BlockSpec((step,) + w.shape[1:], lambda i: (i, 0, 0))

    def body(w_ref, g_ref, m_ref, v_ref, d_ref, mo_ref, vo_ref):
        d_ref[...], mo_ref[...], vo_ref[...] = _fn_adam(None, w_ref[...], g_ref[...], m_ref[...], v_ref[...])

    return _call_with_step(body, None, None, [w, g, m, v], name=name, grid=(cols // step,), in_specs=[blk] * 4,
                           out_specs=[blk] * 3, out_shape=[jax.ShapeDtypeStruct(w.shape, F32)] * 3, sem=("arbitrary",))


def kernel(x, mem, positions, ln_in_g, ln_in_b, w_in, conv_w, conv_b, dt_bias, a_log, d_skip, ssd_norm_g, q_norm_g, w_q_up, kv_norm_g, w_kv_up, w_mix_out, ln1_g, ln1_b, w_mem_q, w_mem_k, w_mem_v, w_mem_o, ln2_g, ln2_b, w_up, w_down, ln3_g, ln3_b, loss_target, m_ln_in_g, m_ln_in_b, m_w_in, m_conv_w, m_conv_b, m_dt_bias, m_a_log, m_d_skip, m_ssd_norm_g, m_q_norm_g, m_w_q_up, m_kv_norm_g, m_w_kv_up, m_w_mix_out, m_ln1_g, m_ln1_b, m_w_mem_q, m_w_mem_k, m_w_mem_v, m_w_mem_o, m_ln2_g, m_ln2_b, m_w_up, m_w_down, m_ln3_g, m_ln3_b, v_ln_in_g, v_ln_in_b, v_w_in, v_conv_w, v_conv_b, v_dt_bias, v_a_log, v_d_skip, v_ssd_norm_g, v_q_norm_g, v_w_q_up, v_kv_norm_g, v_w_kv_up, v_w_mix_out, v_ln1_g, v_ln1_b, v_w_mem_q, v_w_mem_k, v_w_mem_v, v_w_mem_o, v_ln2_g, v_ln2_b, v_w_up, v_w_down, v_ln3_g, v_ln3_b):
    args = dict(locals())

    wp_a = jnp.concatenate([args[n].reshape(-1, PACK_COLS).astype(BF16) for n in PACK_A_ORDER], axis=0)
    flat = [args[n].reshape(-1).astype(BF16) for n in PACK_B_ORDER[:-1]]
    flat.append(lax.bitcast_convert_type(conv_w.reshape(-1), BF16).reshape(-1))
    used = sum(f.shape[0] for f in flat)
    flat.append(jnp.zeros((PACK_B_ROWS * PACK_COLS - used,), BF16))
    wp_b = jnp.concatenate(flat).reshape(PACK_B_ROWS, PACK_COLS)
    wp_c = w_in[0].astype(BF16)

    P = {n: args[n] for n in SMALL_ORDER}
    loss, grad_x, (red_a, ((gp_c, theirs_c, pb_c), (gp_b, theirs_b, pb_b))), _, gsmall = _local_step(
        x[0], mem[0], positions[0], loss_target[0], None, P, wp_a=wp_a, wp_b=(wp_c, wp_b))

    gs = jnp.concatenate([_row(gsmall[n], PACK_COLS) for n in SMALL_ORDER] + [_row(loss, PACK_COLS)], axis=0)
    gs, got_c, got_b = _small_all_reduce(gs, _merge_steps([_chip_exchange_step(pb_c), _chip_exchange_step(pb_b)]))
    loss = gs[len(SMALL_ORDER), 0]
    red_c, red_b = _run_step(_merge_steps([_pair_fill_step(_chip_sum(gp_c, theirs_c, got_c, "chip_sum_w_in")),
                                           _pair_fill_step(_chip_sum(gp_b, theirs_b, got_b, "chip_sum_b"))]),
                             "pair_fill_b")

    grads, deltas, new_m, new_v = {}, {}, {}, {}
    for n in PACK_A_ORDER:
        grads[n], deltas[n], new_m[n], new_v[n] = _adam(args[n], (red_a, PACK_A_ROW[n]), args["m_" + n],
                                                        args["v_" + n], "adam_" + n)
    to_cols = lambda t: jnp.transpose(t, (2, 0, 1))
    from_cols = lambda t: jnp.transpose(t, (1, 2, 0))
    g_t = to_cols(red_c[None])
    grads["w_in"] = from_cols(g_t)
    deltas["w_in"], new_m["w_in"], new_v["w_in"] = map(
        from_cols, _adam_columns(to_cols(w_in), g_t, to_cols(m_w_in), to_cols(v_w_in), "adam_w_in"))
    off = 0
    for n in PACK_B_ORDER:
        sr, sc = _shard_shape(n)
        rows = sr * sc // PACK_COLS
        leaves = [args[n], red_b[off:off + rows].reshape(args[n].shape), args["m_" + n], args["v_" + n]]
        off += rows
        flip = (lambda t: jnp.transpose(t, (0, 2, 1))) if sc % LANES else (lambda t: t)
        grads[n], deltas[n], new_m[n], new_v[n] = map(flip, _adam(*map(flip, leaves), "adam_" + n))
    pack = lambda pre: jnp.concatenate([_row(args[pre + n], PACK_COLS) for n in SMALL_ORDER]
                                       + [jnp.zeros((1, PACK_COLS), F32)], axis=0)
    ds, ms, vs = _rowwise(_fn_adam, [pack(""), gs, pack("m_"), pack("v_")], [], [PACK_COLS] * 3, tr=16,
                          name="adam_small")
    for i, n in enumerate(SMALL_ORDER):
        cnt = args[n].size
        take = lambda t: t[i, :cnt].reshape(args[n].shape)
        grads[n], deltas[n], new_m[n], new_v[n] = take(gs), take(ds), take(ms), take(vs)

    order = ["ln_in_g", "ln_in_b", "w_in", "conv_w", "conv_b", "dt_bias", "a_log", "d_skip", "ssd_norm_g",
             "q_norm_g", "w_q_up", "kv_norm_g", "w_kv_up", "w_mix_out", "ln1_g", "ln1_b", "w_mem_q", "w_mem_k",
             "w_mem_v", "w_mem_o", "ln2_g", "ln2_b", "w_up", "w_down", "ln3_g", "ln3_b"]
    return (loss, grad_x[None], *[grads[n] for n in order], *[deltas[n] for n in order],
            *[new_m[n] for n in order], *[new_v[n] for n in order])
```

```python
import functools

import jax
import jax.numpy as jnp
from jax import lax
from jax.experimental import pallas as pl
from jax.experimental.pallas import tpu as pltpu

F32 = jnp.float32
BF16 = jnp.bfloat16
MESH = pl.DeviceIdType.MESH

D_MODEL = 1024
SSD_CHUNK = 128
SSD_STATE = 128
MLA_HEADS = 8
MLA_ROPE = 32
MLA_QK = 96
MLA_KV_RANK = 256
ROPE_THETA = 10000.0
MEM_HEADS = 4
MEM_HEAD_DIM = 256
LN_EPS = 1e-5
RMS_EPS = 1e-6
ALPHA = 2.0 ** 0.25
ADAM_LR = 0.001
ADAM_B1 = 0.9
ADAM_B2 = 0.999
ADAM_EPS = 1e-08
ADAM_WD = 0.01
ADAM_STEP = 10

LANES = 128
IN_W = 2560
SEG_XBC = (0, 1024)
SEG_Z = (1024, 512)
SEG_QLAT = (1536, 384)
SEG_DT = (1920, 128)
SEG_KVLAT = (2048, 256)
SEG_KR = (2304, 128)
VMEM_LIMIT = 56 * 1024 * 1024
ATTN_TILE = 512
ROW_TILE = 512
SSD_PER_STEP = 2
NEG = -1e30
MLA_SCALE = MLA_QK ** -0.5
MEM_SCALE = MEM_HEAD_DIM ** -0.5

NN = (((1,), (0,)), ((), ()))
NT = (((1,), (1,)), ((), ()))
TN = (((0,), (0,)), ((), ()))


def _dot(a, b, dims=NN):
    return lax.dot_general(a.astype(BF16), b.astype(BF16), dims, preferred_element_type=F32)


def _dot_exact(a, b):
    return lax.dot_general(a, b, NN, precision=lax.Precision.HIGHEST, preferred_element_type=F32)


def _pick(dim, pref):
    t = min(pref, dim)
    t -= t % LANES
    while t >= LANES:
        if dim % t == 0:
            return t
        t -= LANES
    return dim


def _params(sem):
    return pltpu.CompilerParams(dimension_semantics=sem, vmem_limit_bytes=VMEM_LIMIT)


def _pack_caps(wname):
    r, c, ax = BIG[wname]
    if ax == 0:
        return (r if r <= 1024 else r // N_SHARD), c
    return r, c // N_SHARD


def _pack_block(wname, br, bc):
    r, c, ax = BIG[wname]
    r0 = PACK_A_ROW[wname]
    sr = r // N_SHARD if ax == 0 else r
    if ax == 0 and br > sr:
        assert br % sr == 0 and r0 % sr == 0
        return (br // sr, sr, bc), lambda rb, cb: (rb, r0 // sr, cb)
    assert r0 % br == 0
    if ax == 0:
        per = sr // br
        return (1, br, bc), lambda rb, cb: (rb // per, r0 // br + rb % per, cb)
    per = (c // N_SHARD) // bc
    return (1, br, bc), lambda rb, cb: (cb // per, r0 // br + rb, cb % per)


def _mm(a, b, *, form, name, a_pro=None, epi=None, out_dtype=F32, tm=1024, tn=1024, tk=1024, b_pack=None,
        b_rows=None, out_pack=None, hosted=None):
    b_shape = BIG[b_pack][:2] if b_pack else b.shape
    if b_pack and form == "nt":
        b_shape = (b_rows or b_shape[0], b_shape[1])
    if form == "nn":
        (m, k), (_, n) = a.shape, b_shape
    elif form == "nt":
        (m, k), (n, _) = a.shape, b_shape
    else:
        (k, m), (_, n) = a.shape, b_shape
    if b_pack:
        rcap, ccap = _pack_caps(b_pack)
        tk, tn = (min(tk, rcap), min(tn, ccap)) if form == "nn" else (min(tk, ccap), min(tn, rcap))
    if out_pack:
        rcap, ccap = _pack_caps(out_pack[0])
        tm, tn = min(tm, rcap), min(tn, ccap)
    tm, tn, tk = _pick(m, tm), _pick(n, tn), _pick(k, tk)
    dims = {"nn": NN, "nt": NT, "tn": TN}[form]
    nk = k // tk
    direct = out_dtype == F32 and epi is None
    n_extra = (1 if epi else 0) + (1 if out_pack else 0)

    def body(a_ref, b_ref, *rest):
        o_ref = rest[n_extra]
        acc_ref = o_ref if direct else rest[-1]

        @pl.when(pl.program_id(2) == 0)
        def _():
            acc_ref[...] = jnp.zeros_like(acc_ref)

        av = a_ref[...]
        if a_pro is not None:
            av = a_pro(av)
        bv = b_ref[...]
        acc_ref[...] += _dot(av, bv.reshape(-1, bv.shape[-1]), dims).reshape(acc_ref.shape)
        if not direct:
            @pl.when(pl.program_id(2) == nk - 1)
            def _():
                val = acc_ref[...]
                if epi is not None:
                    val = epi[0](val, rest[0][...])
                o_ref[...] = val.reshape(o_ref.shape).astype(o_ref.dtype)

    if form == "tn":
        a_spec = pl.BlockSpec((tk, tm), lambda i, j, kk: (kk, i))
    else:
        a_spec = pl.BlockSpec((tm, tk), lambda i, j, kk: (i, kk))
    if b_pack:
        shape, idx = _pack_block(b_pack, *((tk, tn) if form == "nn" else (tn, tk)))
        b_spec = pl.BlockSpec(shape, (lambda i, j, kk: idx(kk, j)) if form == "nn" else (lambda i, j, kk: idx(j, kk)))
    elif form == "nt":
        b_spec = pl.BlockSpec((tn, tk), lambda i, j, kk: (j, kk))
    else:
        b_spec = pl.BlockSpec((tk, tn), lambda i, j, kk: (kk, j))
    in_specs, args = [a_spec, b_spec], [a, b]
    out_spec = pl.BlockSpec((tm, tn), lambda i, j, kk: (i, j))
    out_sds, aliases = jax.ShapeDtypeStruct((m, n), out_dtype), {}
    if epi is not None:
        in_specs.append(out_spec)
        args.append(epi[1])
    if out_pack:
        wname, buf = out_pack
        shape, idx = _pack_block(wname, tm, tn)
        out_spec = pl.BlockSpec(shape, lambda i, j, kk: idx(i, j))
        out_sds, aliases = jax.ShapeDtypeStruct(buf.shape, buf.dtype), {len(args): 0}
        in_specs.append(HBM)
        args.append(buf)
    acc_shape = out_spec.block_shape if out_pack else (tm, tn)
    res = _call_with_step(
        body, hosted, None, args, name=name, grid=(m // tm, n // tn, nk), in_specs=in_specs, out_specs=[out_spec],
        out_shape=[out_sds], sem=("parallel", "parallel", "arbitrary"), aliases=aliases,
        scratch_shapes=[] if direct else [pltpu.VMEM(acc_shape, F32)])
    return res[0] if hosted is None else res


class _Ctx:
    def __init__(self, i, n):
        self.i, self.n = i, n


def _rowwise(fn, rows, consts, row_outs, acc_outs=(), *, tr, name, n_rows=None, hosted=None):
    norm = []
    for r in rows:
        kind = "tile"
        if isinstance(r, tuple) and isinstance(r[0], str):
            kind, r = r[0], r[1:]
        if kind == "cols":
            norm.append((kind, r[0], 0, r[0].shape[0], 0))
            continue
        row0 = 0
        if isinstance(r, tuple) and len(r) == 4:
            r, row0 = r[:3], r[3]
        arr, col0, width = r if isinstance(r, tuple) else (r, 0, r.shape[1])
        assert col0 % width == 0
        norm.append((kind, arr, col0 // width, width, row0))
    n_rows = n_rows or next(a.shape[0] for k, a, _, _, _ in norm if k == "tile")
    tr = min(tr, n_rows)
    while n_rows % tr:
        tr -= 8
    n = n_rows // tr
    arrs, specs = [], []
    for kind, arr, cb, width, row0 in norm:
        if kind == "tile":
            assert row0 % tr == 0
            specs.append(pl.BlockSpec((tr, width), lambda i, cb=cb, rb=row0 // tr: (i + rb, cb)))
        elif kind == "cols":
            specs.append(pl.BlockSpec((width, tr), lambda i: (0, i)))
        elif kind == "prev":
            specs.append(pl.BlockSpec((8, width), lambda i, cb=cb: (jnp.maximum(i * (tr // 8) - 1, 0), cb)))
        else:
            specs.append(pl.BlockSpec((8, width), lambda i, cb=cb: (jnp.minimum((i + 1) * (tr // 8), n_rows // 8 - 1), cb)))
        arrs.append(arr)
    for c in consts:
        specs.append(pl.BlockSpec(c.shape, lambda i, nd=c.ndim: (0,) * nd))
        arrs.append(c)
    n_in, n_ro = len(arrs), len(row_outs)
    out_shape, out_specs, aliases = [], [], {}
    for j, ro in enumerate(row_outs):
        w, dt, col0 = (ro + (None,))[:3] if isinstance(ro, tuple) else (ro, F32, None)
        if col0 is None:
            out_shape.append(jax.ShapeDtypeStruct((n_rows, w), dt))
            out_specs.append(pl.BlockSpec((tr, w), lambda i: (i, 0)))
        else:
            assert col0 % w == 0 and dt.shape[0] == n_rows
            out_shape.append(jax.ShapeDtypeStruct(dt.shape, dt.dtype))
            out_specs.append(pl.BlockSpec((tr, w), lambda i, cb=col0 // w: (i, cb)))
            aliases[len(arrs)] = j
            arrs.append(dt)
            specs.append(HBM)
    n_all = len(arrs)
    out_shape += [jax.ShapeDtypeStruct(s, F32) for s in acc_outs]
    out_specs += [pl.BlockSpec(s, lambda i: (0, 0)) for s in acc_outs]

    def body(*refs):
        i = pl.program_id(0)
        vals = [r[...] for r in refs[:n_in]]
        outs = fn(_Ctx(i, n), *vals)
        if not isinstance(outs, (tuple, list)):
            outs = (outs,)
        o_refs = refs[n_all:]
        for o_ref, o in zip(o_refs[:n_ro], outs[:n_ro]):
            o_ref[...] = o.astype(o_ref.dtype)
        if acc_outs:
            @pl.when(i == 0)
            def _():
                for o_ref in o_refs[n_ro:]:
                    o_ref[...] = jnp.zeros_like(o_ref)

            for o_ref, o in zip(o_refs[n_ro:], outs[n_ro:]):
                o_ref[...] += jnp.broadcast_to(o, o_ref.shape)

    return _call_with_step(body, hosted, None, arrs, name=name, grid=(n,), in_specs=specs, out_specs=out_specs,
                           out_shape=out_shape, sem=("arbitrary",), aliases=aliases)


def _sum0(v):
    return jnp.sum(v, axis=0, keepdims=True)


def _mean1(v):
    return jnp.mean(v, axis=-1, keepdims=True)


def _sigmoid(v):
    return 1.0 / (1.0 + jnp.exp(-v))


def _ln_stats(t):
    xc = t - _mean1(t)
    rstd = lax.rsqrt(_mean1(xc * xc) + LN_EPS)
    return xc * rstd, rstd


def _ln_bwd(xhat, rstd, dy, g):
    dxh = dy * g
    dx = rstd * (dxh - _mean1(dxh) - xhat * _mean1(dxh * xhat))
    return dx, _sum0(dy * xhat), _sum0(dy)


def _rms_fwd(v, g):
    return v * lax.rsqrt(_mean1(v * v) + RMS_EPS) * g


def _rms_bwd(v, dy, g):
    rs = lax.rsqrt(_mean1(v * v) + RMS_EPS)
    vh = v * rs
    dyg = dy * g
    return rs * (dyg - vh * _mean1(dyg * vh)), _sum0(dy * vh)


def _lane(shape):
    return lax.broadcasted_iota(jnp.int32, shape, len(shape) - 1)


def _shift_down(u, halo, s, is_first):
    tr = u.shape[0]
    rolled = pltpu.roll(u, s, 0)
    hr = jnp.where(is_first, 0.0, pltpu.roll(halo, s, 0))
    row = lax.broadcasted_iota(jnp.int32, hr.shape, 0)
    top = jnp.where(row < s, hr, rolled[0:8])
    if tr == 8:
        return top
    return jnp.concatenate([top, rolled[8:]], axis=0)


def _shift_up(d, halo, s, is_last):
    tr = d.shape[0]
    rolled = pltpu.roll(d, tr - s, 0)
    hr = jnp.where(is_last, 0.0, pltpu.roll(halo, 8 - s, 0))
    row = lax.broadcasted_iota(jnp.int32, hr.shape, 0)
    bot = jnp.where(row >= 8 - s, hr, rolled[tr - 8:])
    if tr == 8:
        return bot
    return jnp.concatenate([rolled[:tr - 8], bot], axis=0)


def _rope_tables(trig):
    t = jnp.concatenate([trig] * (LANES // trig.shape[0]), axis=0).T
    lane = _lane(t.shape)
    first, second = (lane >= 64) & (lane < 80), (lane >= 80) & (lane < 96)
    ta = jnp.where(lane < 64, 1.0, jnp.where(first, pltpu.roll(t, 64, 1), jnp.where(second, pltpu.roll(t, 80, 1), 0.0)))
    return ta, jnp.where(second, pltpu.roll(t, 64, 1), 0.0), jnp.where(first, -pltpu.roll(t, 48, 1), 0.0)


def _rope(v, ta, tb, tc):
    return v * ta + pltpu.roll(v, 16, 1) * tb + pltpu.roll(v, LANES - 16, 1) * tc


def _rope_bwd(d, ta, tb, tc):
    return d * ta + pltpu.roll(d * tb, LANES - 16, 1) + pltpu.roll(d * tc, 16, 1)


def _ssd_common(dtv, a_row):
    L = SSD_CHUNK
    a = dtv * a_row
    r = lax.broadcasted_iota(jnp.int32, (L, L), 0)
    c = lax.broadcasted_iota(jnp.int32, (L, L), 1)
    tril = r >= c
    cs = _dot_exact(tril.astype(F32), a)
    cs_t = cs.T
    cs_last = cs[L - 1:L, :]
    return dict(a=a, tril=tril, cs=cs, cs_t=cs_t, ecs=jnp.exp(cs), dte=jnp.exp(cs_last - cs),
                elast=jnp.exp(cs_last))


def _pair_sel(v, h0, lo):
    return jnp.where(lo, v[:, h0:h0 + 1], v[:, h0 + 1:h0 + 2])


def _ssd_pair(cm, h0, cb, xp, dtv, bmat, cmat, hp, lo):
    x = xp * _pair_sel(dtv, h0, lo)
    lam0 = jnp.exp(jnp.where(cm["tril"], cm["cs"][:, h0:h0 + 1] - cm["cs_t"][h0:h0 + 1, :], NEG))
    lam1 = jnp.exp(jnp.where(cm["tril"], cm["cs"][:, h0 + 1:h0 + 2] - cm["cs_t"][h0 + 1:h0 + 2, :], NEG))
    m0, m1 = cb * lam0, cb * lam1
    ydiag = jnp.where(lo, _dot(m0, x), _dot(m1, x))
    ecs_p = _pair_sel(cm["ecs"], h0, lo)
    dte_p = _pair_sel(cm["dte"], h0, lo)
    yoff = _dot(cmat, hp, NT) * ecs_p
    xd = x * dte_p
    st = _dot(xd, bmat, TN)
    rlo = lax.broadcasted_iota(jnp.int32, (LANES, SSD_STATE), 0) < 64
    decay = jnp.where(rlo, cm["elast"][:, h0:h0 + 1], cm["elast"][:, h0 + 1:h0 + 2])
    h_next = hp * decay + st
    return dict(x=x, lam0=lam0, lam1=lam1, m0=m0, m1=m1, y=ydiag + yoff, yoff=yoff, ecs_p=ecs_p, dte_p=dte_p,
                xd=xd, decay=decay, h_next=h_next)


def _ssd_fwd(xbc, dt, a_row, *, name):
    S = xbc.shape[0]
    L = SSD_CHUNK
    nc = S // L
    per = SSD_PER_STEP if nc % SSD_PER_STEP == 0 else 1
    G = per * L

    def body(xs_ref, bm_ref, cm_ref, dt_ref, a_ref, y_ref, hs_ref, h_scr):
        @pl.when(pl.program_id(0) == 0)
        def _():
            h_scr[...] = jnp.zeros_like(h_scr)

        lo = _lane((L, LANES)) < 64
        for sub in range(per):
            rows = slice(sub * L, (sub + 1) * L)
            dtv = dt_ref[rows, :]
            cm = _ssd_common(dtv, a_ref[...])
            ys = []
            for g in range(2):
                bmat = bm_ref[rows, g * 128:(g + 1) * 128]
                cmat = cm_ref[rows, g * 128:(g + 1) * 128]
                cb = _dot(cmat, bmat, NT)
                for pr in range(2):
                    p4 = 2 * g + pr
                    hp = h_scr[p4]
                    hs_ref[sub, p4 * 128:(p4 + 1) * 128, :] = hp
                    t = _ssd_pair(cm, 2 * p4, cb, xs_ref[rows, p4 * 128:(p4 + 1) * 128], dtv, bmat, cmat, hp, lo)
                    ys.append(t["y"])
                    h_scr[p4] = t["h_next"]
            y_ref[rows, :] = jnp.concatenate(ys, axis=1)

    return pl.pallas_call(
        body, name=name, grid=(nc // per,),
        in_specs=[pl.BlockSpec((G, 512), lambda c: (c, 0)), pl.BlockSpec((G, 256), lambda c: (c, 2)),
                  pl.BlockSpec((G, 256), lambda c: (c, 3)), pl.BlockSpec((G, 128), lambda c: (c, 0)),
                  pl.BlockSpec((1, 128), lambda c: (0, 0))],
        out_specs=[pl.BlockSpec((G, 512), lambda c: (c, 0)), pl.BlockSpec((per, 512, 128), lambda c: (c, 0, 0))],
        out_shape=[jax.ShapeDtypeStruct((S, 512), F32), jax.ShapeDtypeStruct((nc, 512, 128), F32)],
        scratch_shapes=[pltpu.VMEM((4, 128, 128), F32)],
        compiler_params=_params(("arbitrary",)),
    )(xbc, xbc, xbc, dt, a_row)


def _ssd_bwd(xbc, dt, a_row, hs, dy, *, name, hosted=None):
    S = xbc.shape[0]
    L = SSD_CHUNK
    nc = S // L
    per = SSD_PER_STEP if nc % SSD_PER_STEP == 0 else 1
    G = per * L

    def body(xs_ref, bm_ref, cm_ref, dt_ref, a_ref, hs_ref, dy_ref, dxs_ref, dbc_ref, ddt_ref, da_ref, g_scr):
        @pl.when(pl.program_id(0) == 0)
        def _():
            g_scr[...] = jnp.zeros_like(g_scr)
            da_ref[...] = jnp.zeros_like(da_ref)

        for sub in reversed(range(per)):
            rows = pl.ds(sub * L, L)
            chunk(xs_ref.at[rows, :], bm_ref.at[rows, :], cm_ref.at[rows, :], dt_ref.at[rows, :], a_ref,
                  hs_ref.at[pl.ds(sub, 1)], dy_ref.at[rows, :], dxs_ref.at[rows, :], dbc_ref.at[rows, :],
                  ddt_ref.at[rows, :], da_ref, g_scr)

    def chunk(xs_ref, bm_ref, cm_ref, dt_ref, a_ref, hs_ref, dy_ref, dxs_ref, dbc_ref, ddt_ref, da_ref, g_scr):
        dtv = dt_ref[...]
        a_row_v = a_ref[...]
        cm = _ssd_common(dtv, a_row_v)
        lo = _lane((L, LANES)) < 64
        lane_row = _lane((1, LANES))
        ri = lax.broadcasted_iota(jnp.int32, (L, L), 0)
        ci = lax.broadcasted_iota(jnp.int32, (L, L), 1)
        triu = (ri <= ci).astype(F32)
        stril = ri > ci

        def halves(v, mask):
            return (jnp.sum(jnp.where(mask, v, 0.0), axis=1, keepdims=True),
                    jnp.sum(jnp.where(mask, 0.0, v), axis=1, keepdims=True))

        i_all = jnp.zeros((L, LANES), F32)
        yo_all = jnp.zeros((L, LANES), F32)
        w_all = jnp.zeros((L, LANES), F32)
        ddt_x = jnp.zeros((L, LANES), F32)
        e_row = jnp.zeros((1, LANES), F32)
        rlo = lax.broadcasted_iota(jnp.int32, (LANES, SSD_STATE), 0) < 64
        dxs, dbs, dcs = [], [], []
        for g in range(2):
            bmat = bm_ref[:, g * 128:(g + 1) * 128]
            cmat = cm_ref[:, g * 128:(g + 1) * 128]
            cb = _dot(cmat, bmat, NT)
            dcb = jnp.zeros((L, L), F32)
            db = jnp.zeros((L, SSD_STATE), F32)
            dc = jnp.zeros((L, SSD_STATE), F32)
            for pr in range(2):
                p4 = 2 * g + pr
                h0 = 2 * p4
                hp = hs_ref[0, p4 * 128:(p4 + 1) * 128, :]
                xp = xs_ref[:, p4 * 128:(p4 + 1) * 128]
                t = _ssd_pair(cm, h0, cb, xp, dtv, bmat, cmat, hp, lo)
                gst = g_scr[p4]
                dyp = dy_ref[:, p4 * 128:(p4 + 1) * 128]
                dy0 = jnp.where(lo, dyp, 0.0)
                dy1 = dyp - dy0
                bg = _dot(bmat, gst, NT)
                dx = _dot(t["m0"], dy0, TN) + _dot(t["m1"], dy1, TN) + bg * t["dte_p"]
                dm0, dm1 = _dot(dy0, t["x"], NT), _dot(dy1, t["x"], NT)
                dcb = dcb + dm0 * t["lam0"] + dm1 * t["lam1"]
                dye = dyp * t["ecs_p"]
                dc = dc + _dot(dye, hp)
                db = db + _dot(t["xd"], gst)
                i0 = jnp.sum(jnp.where(stril, _dot(triu, dm0 * t["m0"]), 0.0), axis=1, keepdims=True)
                i1 = jnp.sum(jnp.where(stril, _dot(triu, dm1 * t["m1"]), 0.0), axis=1, keepdims=True)
                yo0, yo1 = halves(dyp * t["yoff"], lo)
                w0, w1 = halves(t["xd"] * bg, lo)
                gh = gst * (hp * t["decay"])
                e0 = _sum0(jnp.sum(jnp.where(rlo, gh, 0.0), axis=1, keepdims=True))
                e1 = _sum0(jnp.sum(jnp.where(rlo, 0.0, gh), axis=1, keepdims=True))
                x0, x1 = halves(dx * xp, lo)
                oh0 = (lane_row == h0).astype(F32)
                oh1 = (lane_row == h0 + 1).astype(F32)
                i_all = i_all + i0 * oh0 + i1 * oh1
                yo_all = yo_all + yo0 * oh0 + yo1 * oh1
                w_all = w_all + w0 * oh0 + w1 * oh1
                e_row = e_row + e0 * oh0 + e1 * oh1
                ddt_x = ddt_x + x0 * oh0 + x1 * oh1
                dxs.append(dx * _pair_sel(dtv, h0, lo))
                g_scr[p4] = gst * t["decay"] + _dot(dye, cmat, TN)
            dbs.append(db + _dot(dcb, cmat, TN))
            dcs.append(dc + _dot(dcb, bmat))
        da = i_all + _dot_exact(triu, yo_all) + _dot_exact(stril.astype(F32), w_all) + e_row
        ddt_ref[...] = da * a_row_v + ddt_x
        da_ref[...] += _sum0(da * dtv)
        dxs_ref[...] = jnp.concatenate(dxs, axis=1)
        dbc_ref[...] = jnp.concatenate(dbs + dcs, axis=1)

    rev = lambda c: nc // per - 1 - c
    return _call_with_step(
        body, hosted, None, (xbc, xbc, xbc, dt, a_row, hs, dy), name=name, grid=(nc // per,),
        in_specs=[pl.BlockSpec((G, 512), lambda c: (rev(c), 0)), pl.BlockSpec((G, 256), lambda c: (rev(c), 2)),
                  pl.BlockSpec((G, 256), lambda c: (rev(c), 3)), pl.BlockSpec((G, 128), lambda c: (rev(c), 0)),
                  pl.BlockSpec((1, 128), lambda c: (0, 0)), pl.BlockSpec((per, 512, 128), lambda c: (rev(c), 0, 0)),
                  pl.BlockSpec((G, 512), lambda c: (rev(c), 0))],
        out_specs=[pl.BlockSpec((G, 512), lambda c: (rev(c), 0)), pl.BlockSpec((G, 512), lambda c: (rev(c), 0)),
                   pl.BlockSpec((G, 128), lambda c: (rev(c), 0)), pl.BlockSpec((1, 128), lambda c: (0, 0))],
        out_shape=[jax.ShapeDtypeStruct((S, 512), F32), jax.ShapeDtypeStruct((S, 512), F32),
                   jax.ShapeDtypeStruct((S, 128), F32), jax.ShapeDtypeStruct((1, 128), F32)],
        sem=("arbitrary",), scratch_shapes=[pltpu.VMEM((4, 128, 128), F32)])


HBM = pl.BlockSpec(memory_space=pl.ANY)


class _Step:
    def __init__(self, inputs, out_shapes, n_sems, start, finish, mid=None):
        self.inputs, self.out_shapes, self.n_sems = inputs, out_shapes, n_sems
        self.start, self.finish, self.mid = start, finish, mid
        self.alias = []


class _Shifted:
    def __init__(self, ref, off):
        self.ref, self.off = ref, off

    @property
    def at(self):
        return self

    def __getitem__(self, j):
        return self.ref.at[self.off + j]


def _merge_steps(steps):
    offs = [sum(s.n_sems for s in steps[:i]) for i in range(len(steps) + 1)]
    i_offs = [sum(len(s.inputs) for s in steps[:i]) for i in range(len(steps))]
    o_offs = [sum(len(s.out_shapes) for s in steps[:i]) for i in range(len(steps))]

    def phase(which):
        def run(ins, outs, sems):
            for s, off, i0, o0 in zip(steps, offs, i_offs, o_offs):
                fn = getattr(s, which)
                if fn is not None:
                    fn(ins[i0:i0 + len(s.inputs)], outs[o0:o0 + len(s.out_shapes)],
                       [_Shifted(sems[0], off), _Shifted(sems[1], off)])
        return run

    merged = _Step([a for s in steps for a in s.inputs], [o for s in steps for o in s.out_shapes], offs[-1],
                   phase("start"), phase("finish"), phase("mid") if any(s.mid for s in steps) else None)
    merged.alias = [(i0 + a, o0 + b) for s, i0, o0 in zip(steps, i_offs, o_offs) for a, b in s.alias]
    return merged


def _place():
    x, y, c = lax.axis_index("x"), lax.axis_index("y"), lax.axis_index("c")
    chips = [(1 - x, y), (x, 1 - y), (1 - x, 1 - y)]
    return x, y, c, chips


def _mesh_pos():
    return 2 * lax.axis_index("x") + lax.axis_index("y"), lax.axis_index("c")


def _chunks(rows, tile):
    return next(n for n in (4, 3, 2, 1) if rows % (n * tile) == 0)


def _remote(src, dst, sems, j, to):
    return pltpu.make_async_remote_copy(src_ref=src, dst_ref=dst, send_sem=sems[0].at[j], recv_sem=sems[1].at[j],
                                        device_id=to, device_id_type=MESH)


def _own_slot(wp):
    return lax.dynamic_update_slice(lax.empty((N_SHARD,) + wp.shape, wp.dtype), wp[None], (_mesh_pos()[0], 0, 0))


def _gather_step(buf):
    _, R, C = buf.shape
    H = R // 2
    nq = _chunks(H, 16)
    CH = H // nq

    def copies(ins, outs, sems):
        x, y, c, chips = _place()
        sib, me = (x, y, 1 - c), 2 * x + y
        w_ref, out_ref = ins[0], outs[0]

        def piece(k, hc, q):
            return out_ref.at[k, pl.ds(hc * H + q * CH, CH), :]

        sends, landed, fwds, fwd_landed = [], [], [], []
        for q in range(nq):
            for j, (px, py) in enumerate(chips):
                k = 2 * px + py
                sends.append(_remote(w_ref.at[me, pl.ds(c * H + q * CH, CH), :], piece(me, c, q), sems, j * nq + q,
                                     (px, py, c)))
                landed.append(_remote(piece(k, c, q), piece(k, c, q), sems, j * nq + q, (px, py, c)))
                fwds.append(_remote(piece(k, c, q), piece(k, c, q), sems, (3 + j) * nq + q, sib))
                fwd_landed.append(_remote(piece(k, 1 - c, q), piece(k, 1 - c, q), sems, (3 + j) * nq + q, sib))
        return sends, landed, fwds, fwd_landed

    def start(ins, outs, sems):
        for cp in copies(ins, outs, sems)[0]:
            cp.start()

    def mid(ins, outs, sems):
        _, landed, fwds, _ = copies(ins, outs, sems)
        for arrived, onward in zip(landed, fwds):
            arrived.wait_recv()
            onward.start()

    def finish(ins, outs, sems):
        sends, _, fwds, fwd_landed = copies(ins, outs, sems)
        for cp in fwd_landed:
            cp.wait_recv()
        for cp in sends + fwds:
            cp.wait_send()

    step = _Step([buf], [jax.ShapeDtypeStruct(buf.shape, buf.dtype)], 6 * nq, start, finish, mid)
    step.alias = [(0, 0)]
    return step


def _pair_exchange_step(gp):
    n, R, C = gp.shape
    H = R // 2
    nq = _chunks(H, 8)
    CH = H // nq

    def copies(ins, outs, sems):
        x, y, c, _ = _place()
        return [_remote(ins[0].at[k, pl.ds((1 - c) * H + q * CH, CH), :], outs[0].at[k, pl.ds(q * CH, CH), :], sems,
                        k * nq + q, (x, y, 1 - c)) for k in range(n) for q in range(nq)]

    def start(ins, outs, sems):
        for cp in copies(ins, outs, sems):
            cp.start()

    def finish(ins, outs, sems):
        for cp in copies(ins, outs, sems):
            cp.wait()

    return _Step([gp], [jax.ShapeDtypeStruct((n, H, C), gp.dtype)], n * nq, start, finish)


def _chip_exchange_step(pb):
    n, H, C = pb.shape
    nq = _chunks(H, 16)
    CH = H // nq

    def copies(ins, outs, sems):
        x, y, c, chips = _place()
        return [_remote(ins[0].at[2 * px + py, pl.ds(q * CH, CH), :], outs[0].at[j, pl.ds(q * CH, CH), :], sems,
                        j * nq + q, (px, py, c)) for q in range(nq) for j, (px, py) in enumerate(chips)]

    def start(ins, outs, sems):
        for cp in copies(ins, outs, sems):
            cp.start()

    def finish(ins, outs, sems):
        for cp in copies(ins, outs, sems):
            cp.wait()

    return _Step([pb], [jax.ShapeDtypeStruct((3, H, C), pb.dtype)], 3 * nq, start, finish)


def _pair_fill_step(red):
    R, C = red.shape
    H = R // 2
    nq = _chunks(H, 8)
    CH = H // nq

    def copies(ins, outs, sems):
        x, y, c, _ = _place()
        return [_remote(ins[0].at[pl.ds(c * H + j * CH, CH), :], outs[0].at[pl.ds(c * H + j * CH, CH), :], sems, j,
                        (x, y, 1 - c)) for j in range(nq)]

    def start(ins, outs, sems):
        for cp in copies(ins, outs, sems):
            cp.start()

    def finish(ins, outs, sems):
        for cp in copies(ins, outs, sems):
            cp.wait()

    step = _Step([red], [jax.ShapeDtypeStruct((R, C), red.dtype)], nq, start, finish)
    step.alias = [(0, 0)]
    return step


def _sem_scratch(step):
    return [pltpu.SemaphoreType.DMA((step.n_sems,)), pltpu.SemaphoreType.DMA((step.n_sems,))]


def _run_step(step, name):
    ni, no = len(step.inputs), len(step.out_shapes)

    def body(*refs):
        ins, outs, sems = refs[:ni], refs[ni:ni + no], refs[ni + no:]
        step.start(ins, outs, sems)
        if step.mid is not None:
            step.mid(ins, outs, sems)
        step.finish(ins, outs, sems)

    return pl.pallas_call(body, name=name, in_specs=[HBM] * ni, out_specs=[HBM] * no, out_shape=step.out_shapes,
                          input_output_aliases=dict(step.alias),
                          scratch_shapes=_sem_scratch(step))(*step.inputs)


def _grid_flags(grid):
    ids = [pl.program_id(d) for d in range(len(grid))]
    first = functools.reduce(lambda a, b: a & b, [i == 0 for i in ids])
    last = functools.reduce(lambda a, b: a & b, [i == n - 1 for i, n in zip(ids, grid)])
    return first, last, last


def _call_with_step(core, step, flags, args, *, name, grid, in_specs, out_specs, out_shape, sem, scratch_shapes=(),
                    aliases=None):
    aliases = aliases or {}
    if step is None:
        return pl.pallas_call(core, name=name, grid=grid, in_specs=in_specs, out_specs=out_specs,
                              out_shape=out_shape, scratch_shapes=list(scratch_shapes),
                              input_output_aliases=aliases, compiler_params=_params(sem))(*args)
    n_in, n_out, n_scr = len(in_specs), len(out_specs), len(scratch_shapes)
    si, so = len(step.inputs), len(step.out_shapes)
    flags = flags or (lambda: _grid_flags(grid))
    aliases = {**aliases, **{n_in + a: n_out + b for a, b in step.alias}}

    def body(*refs):
        ins, s_ins = refs[:n_in], refs[n_in:n_in + si]
        outs = refs[n_in + si:n_in + si + n_out]
        s_outs = refs[n_in + si + n_out:n_in + si + n_out + so]
        scr = refs[n_in + si + n_out + so:n_in + si + n_out + so + n_scr]
        sems = refs[n_in + si + n_out + so + n_scr:]
        first, middle, last = flags()

        @pl.when(first)
        def _():
            step.start(s_ins, s_outs, sems)

        if step.mid is not None:
            @pl.when(middle)
            def _():
                step.mid(s_ins, s_outs, sems)

        core(*ins, *outs, *scr)

        @pl.when(last)
        def _():
            step.finish(s_ins, s_outs, sems)

    return pl.pallas_call(
        body, name=name, grid=grid, in_specs=list(in_specs) + [HBM] * si, out_specs=list(out_specs) + [HBM] * so,
        out_shape=list(out_shape) + list(step.out_shapes), scratch_shapes=list(scratch_shapes) + _sem_scratch(step),
        input_output_aliases=aliases, compiler_params=_params(("arbitrary",) * len(grid)))(*args, *step.inputs)


def _attn_flags(nq):
    h, qi = pl.program_id(0), pl.program_id(1)
    return ((h == 0) & (qi == 0), (h == MLA_HEADS - 1) & (qi == 0), (h == MLA_HEADS - 1) & (qi == nq - 1))


def _att_mask(s_t, q0, k0):
    krow = k0 + lax.broadcasted_iota(jnp.int32, s_t.shape, 0)
    qcol = q0 + lax.broadcasted_iota(jnp.int32, s_t.shape, 1)
    return jnp.where(krow <= qcol, s_t, NEG)


def _loop_blocks(lo, hi, step, carry):
    n = hi - lo

    def four(i, c):
        kb = lo + 4 * i
        return step(kb + 3, step(kb + 2, step(kb + 1, step(kb, c))))

    carry = lax.fori_loop(0, n // 4, four, carry)
    base = lo + 4 * (n // 4)
    carry = lax.cond(n % 4 >= 2, lambda c: step(base + 1, step(base, c)), lambda c: c, carry)
    return lax.cond(n % 2 == 1, lambda c: step(hi - 1, c), lambda c: c, carry)


def _rows(ref, blk, t):
    return ref[pl.ds(pl.multiple_of(blk * t, t), t), :]


def _cols(ref, blk, t):
    return ref[:, pl.ds(pl.multiple_of(blk * t, t), t)]


def _attn_fwd(q, k, v_t, *, name, hosted=None):
    S = q.shape[0]
    t = min(ATTN_TILE, S)
    nq = S // t

    def body(q_ref, k_ref, vt_ref, o_ref, lse_ref):
        qi = pl.program_id(1)
        qv = q_ref[...]

        def absorb(kb, carry, masked):
            m, l, acc = carry
            s_t = lax.dot_general(_rows(k_ref, kb, t), qv, NT, preferred_element_type=F32)
            if masked:
                s_t = _att_mask(s_t, qi * t, kb * t)
            m_new = jnp.maximum(m, jnp.max(s_t, axis=0, keepdims=True))
            p_t = jnp.exp(s_t - m_new)
            corr = jnp.exp(m - m_new)
            return (m_new, corr * l + jnp.sum(p_t, axis=0, keepdims=True),
                    corr * acc + lax.dot_general(_cols(vt_ref, kb, t), p_t.astype(BF16), NN,
                                                 preferred_element_type=F32))

        init = (jnp.full((1, t), NEG, F32), jnp.zeros((1, t), F32), jnp.zeros((LANES, t), F32))
        carry = _loop_blocks(0, qi, lambda kb, c: absorb(kb, c, False), init)
        m, l, acc = absorb(qi, carry, True)
        o_ref[...] = acc / l
        lse_ref[0] = m + jnp.log(l)

    return _call_with_step(
        body, hosted, lambda: _attn_flags(nq), (q, k, v_t), name=name, grid=(MLA_HEADS, nq),
        in_specs=[pl.BlockSpec((t, LANES), lambda h, qi: (qi, h)),
                  pl.BlockSpec((S, LANES), lambda h, qi: (0, h)),
                  pl.BlockSpec((LANES, S), lambda h, qi: (h, 0))],
        out_specs=[pl.BlockSpec((LANES, t), lambda h, qi: (h, qi)),
                   pl.BlockSpec((1, 1, t), lambda h, qi: (h, 0, qi))],
        out_shape=[jax.ShapeDtypeStruct((MLA_HEADS * LANES, S), F32), jax.ShapeDtypeStruct((MLA_HEADS, 1, S), F32)],
        sem=("parallel", "arbitrary"))


def _attn_bwd(q, k, v, o_t, do_t, lse, *, name, hosted=None):
    S = q.shape[0]
    t = min(ATTN_TILE, S)
    nq = S // t

    def body(q_ref, k_ref, v_ref, o_ref, do_ref, lse_ref, dq_ref, dk_ref, dv_ref, dkt_scr):
        qi = pl.program_id(1)

        @pl.when(qi == 0)
        def _():
            dkt_scr[...] = jnp.zeros_like(dkt_scr)
            dv_ref[...] = jnp.zeros_like(dv_ref)

        qv = q_ref[...]
        q_t = qv.T
        dov = do_ref[...]
        delta = jnp.sum(dov * o_ref[...], axis=0, keepdims=True)
        dob = dov.astype(BF16)
        lse_v = lse_ref[0]

        def step(kb, acc, masked):
            kt = _rows(k_ref, kb, t)
            s_t = lax.dot_general(kt, qv, NT, preferred_element_type=F32)
            if masked:
                s_t = _att_mask(s_t, qi * t, kb * t)
            p_t = jnp.exp(s_t - lse_v)
            dp_t = lax.dot_general(_rows(v_ref, kb, t), dob, NN, preferred_element_type=F32)
            ds_t = (p_t * (dp_t - delta)).astype(BF16)
            keys = pl.ds(pl.multiple_of(kb * t, t), t)
            dv_ref[:, keys] += lax.dot_general(dob, p_t.astype(BF16), NT, preferred_element_type=F32)
            dkt_scr[:, keys] += lax.dot_general(q_t, ds_t, NT, preferred_element_type=F32)
            return acc + lax.dot_general(kt, ds_t, TN, preferred_element_type=F32)

        acc = _loop_blocks(0, qi, lambda kb, c: step(kb, c, False), jnp.zeros((LANES, t), F32))
        dq_ref[...] = step(qi, acc, True).T

        @pl.when(qi == nq - 1)
        def _():
            dk_ref[...] = dkt_scr[...].T

    tile = pl.BlockSpec((t, LANES), lambda h, qi: (qi, h))
    tile_t = pl.BlockSpec((LANES, t), lambda h, qi: (h, qi))
    stat = pl.BlockSpec((1, 1, t), lambda h, qi: (h, 0, qi))
    seq = pl.BlockSpec((S, LANES), lambda h, qi: (0, h))
    seq_t = pl.BlockSpec((LANES, S), lambda h, qi: (h, 0))
    return _call_with_step(
        body, hosted, lambda: _attn_flags(nq), (q, k, v, o_t, do_t, lse), name=name, grid=(MLA_HEADS, nq),
        in_specs=[tile, seq, seq, tile_t, tile_t, stat],
        out_specs=[tile, seq, seq_t],
        out_shape=[jax.ShapeDtypeStruct((S, MLA_HEADS * LANES), F32), jax.ShapeDtypeStruct((S, MLA_HEADS * LANES), F32),
                   jax.ShapeDtypeStruct((MLA_HEADS * LANES, S), F32)],
        sem=("parallel", "arbitrary"), scratch_shapes=[pltpu.VMEM((LANES, S), F32)])


def _fn_ln(ctx, x, g, b):
    xhat, _ = _ln_stats(x)
    y = xhat * g + b
    return y, y


def _fn_conv_fwd(ctx, u, up, dtr, w8, cb, dtb):
    first = ctx.i == 0
    y = u * w8[3:4] + cb
    for s in (1, 2, 3):
        y = y + _shift_down(u, up, s, first) * w8[3 - s:4 - s]
    act = y * _sigmoid(y)
    v = dtr + dtb
    e = jnp.exp(-jnp.abs(v))
    one_p = 1.0 + e
    log1p = jnp.where(one_p == 1.0, e, jnp.log(one_p) * e / (one_p - 1.0))
    return y, act, jnp.maximum(v, 0.0) + log1p


def _fn_ssd_post(ctx, y, xs, z, dexp, g):
    yg = (y + xs * dexp) * (z * _sigmoid(z))
    outs = []
    for k in range(2):
        v = yg[:, 256 * k:256 * (k + 1)]
        outs.append(v * lax.rsqrt(_mean1(v * v) + RMS_EPS))
    return (jnp.concatenate(outs, axis=1) * g,)


def _fn_ssd_post_bwd(ctx, dyn, y, xs, z, dexp, g):
    yt = y + xs * dexp
    sig = _sigmoid(z)
    sz = z * sig
    yg = yt * sz
    dyh = dyn * g
    yh, dyg = [], []
    for k in range(2):
        sl = slice(256 * k, 256 * (k + 1))
        v = yg[:, sl]
        rs = lax.rsqrt(_mean1(v * v) + RMS_EPS)
        vh = v * rs
        yh.append(vh)
        dyg.append(rs * (dyh[:, sl] - vh * _mean1(dyh[:, sl] * vh)))
    yh = jnp.concatenate(yh, axis=1)
    dyg = jnp.concatenate(dyg, axis=1)
    dyt = dyg * sz
    dz = dyg * yt * (sig * (1.0 + z * (1.0 - sig)))
    return dyt, dz, dyt * dexp, _sum0(dyt * xs), _sum0(dyn * yh)


def _fn_mla_pre(ctx, ql, kvl, gq, gkv):
    return _rms_fwd(ql, gq), _rms_fwd(kvl, gkv)


def _fn_mla_pre_bwd(ctx, ql, kvl, dqn, dkvn_k, dkvn_v, ddtr, gq, gkv):
    dql, dgq = _rms_bwd(ql, dqn, gq)
    dkvl, dgkv = _rms_bwd(kvl, dkvn_k + dkvn_v, gkv)
    return jnp.concatenate([dql, ddtr.astype(F32), dkvl], axis=1), dgq, dgkv


def _fn_rope(ctx, qp, kn, kr, trig):
    ta, tb, tc = _rope_tables(trig)
    kpe = _rope(kr, ta, tb, tc)
    qs, ks = [], []
    for h in range(MLA_HEADS):
        sl = slice(128 * h, 128 * (h + 1))
        qs.append(_rope(qp[:, sl], ta, tb, tc) * MLA_SCALE)
        ks.append(kn[:, sl] + kpe)
    return jnp.concatenate(qs, axis=1), jnp.concatenate(ks, axis=1)


def _fn_rope_bwd(ctx, dq, dk, trig):
    ta, tb, tc = _rope_tables(trig)
    qs = []
    ksum = jnp.zeros_like(ta)
    for h in range(MLA_HEADS):
        sl = slice(128 * h, 128 * (h + 1))
        qs.append(_rope_bwd(dq[:, sl] * MLA_SCALE, ta, tb, tc))
        ksum = ksum + dk[:, sl]
    lane = _lane(ksum.shape)
    dkr = jnp.where((lane >= 64) & (lane < 96), _rope_bwd(ksum, ta, tb, tc), 0.0)
    return jnp.concatenate(qs, axis=1), jnp.concatenate([dkr, jnp.zeros_like(dkr)], axis=1)


def _mem_probs(qh, kh):
    s = _dot(qh, kh, NT) * MEM_SCALE
    p = jnp.exp(s - jnp.max(s, axis=1, keepdims=True))
    return p / jnp.sum(p, axis=1, keepdims=True)


def _fn_mem_fwd(ctx, q, km, vm):
    outs = []
    for h in range(MEM_HEADS):
        sl = slice(256 * h, 256 * (h + 1))
        outs.append(_dot(_mem_probs(q[:, sl], km[:, sl]), vm[:, sl]))
    return (jnp.concatenate(outs, axis=1),)


def _fn_mem_bwd(ctx, q, do, km, vm):
    dqs, dks, dvs = [], [], []
    for h in range(MEM_HEADS):
        sl = slice(256 * h, 256 * (h + 1))
        p = _mem_probs(q[:, sl], km[:, sl])
        dvs.append(_dot(p, do[:, sl], TN))
        dp = _dot(do[:, sl], vm[:, sl], NT)
        ds = p * (dp - jnp.sum(dp * p, axis=1, keepdims=True)) * MEM_SCALE
        dqs.append(_dot(ds, km[:, sl]))
        dks.append(_dot(ds, q[:, sl], TN))
    return jnp.concatenate(dqs, axis=1), jnp.concatenate(dks, axis=1), jnp.concatenate(dvs, axis=1)


def _fn_res_ln(ctx, h, r, g, b):
    xhat, _ = _ln_stats(ALPHA * h + r)
    y = xhat * g + b
    return y, y


def _fn_res_ln_bwd(ctx, h, r, d1, d2, g):
    xhat, rstd = _ln_stats(ALPHA * h + r)
    return _ln_bwd(xhat, rstd, ALPHA * d1 + d2, g)


def _fn_res2_ln(ctx, h, r1, r2, g, b):
    xhat, _ = _ln_stats(ALPHA * h + (r1 + r2))
    return (xhat * g + b,)


def _fn_res2_ln_bwd(ctx, h, r1, r2, d1, d2, g):
    xhat, rstd = _ln_stats(ALPHA * h + (r1 + r2))
    return _ln_bwd(xhat, rstd, ALPHA * d1 + d2, g)


def _fn_in_ln_bwd(ctx, x, d1, d2, g):
    xhat, rstd = _ln_stats(x)
    return _ln_bwd(xhat, rstd, ALPHA * d1 + d2, g)


def _fn_final(ctx, h2, ff, tgt, g, b):
    xhat, rstd = _ln_stats(ALPHA * h2 + ff)
    e = xhat * g + b - tgt
    loss = 0.5 * _sum0(jnp.sum(e * e, axis=1, keepdims=True)) / D_MODEL
    dx, dg, db = _ln_bwd(xhat, rstd, e / D_MODEL, g)
    return dx, dx, dg, db, loss


def _epi_du(da, u):
    return da * 2.0 * jnp.maximum(u.astype(F32), 0.0)


def _relu2(u):
    r = jnp.maximum(u.astype(F32), 0.0)
    return r * r


def _fn_conv_bwd_a(ctx, y, dxs1, dxs2, dbc, dtr, ddt, dtb):
    sig = _sigmoid(y)
    dact = jnp.concatenate([dxs1 + dxs2, dbc], axis=1)
    dyc = dact * (sig * (1.0 + y * (1.0 - sig)))
    ddtr = ddt * _sigmoid(dtr + dtb)
    return dyc, ddtr, _sum0(dyc), _sum0(ddtr)


def _fn_conv_bwd_b(ctx, d, dn, u, up, w8):
    first, last = ctx.i == 0, ctx.i == ctx.n - 1
    du = d * w8[3:4]
    row = lax.broadcasted_iota(jnp.int32, w8.shape, 0)
    dw = jnp.where(row == 3, _sum0(d * u), 0.0)
    for s in (1, 2, 3):
        du = du + _shift_up(d, dn, s, last) * w8[3 - s:4 - s]
        dw = dw + jnp.where(row == 3 - s, _sum0(d * _shift_down(u, up, s, first)), 0.0)
    return du, dw


def _fn_adam(ctx, w, g, m, v):
    m = ADAM_B1 * m + (1.0 - ADAM_B1) * g
    v = ADAM_B2 * v + (1.0 - ADAM_B2) * (g * g)
    m_hat = m / (1.0 - ADAM_B1 ** ADAM_STEP)
    v_hat = v / (1.0 - ADAM_B2 ** ADAM_STEP)
    return -ADAM_LR * (m_hat / (jnp.sqrt(v_hat) + ADAM_EPS) + ADAM_WD * w), m, v


def _z(r, c, dt):
    return jnp.zeros((r, c), dt)


W_IN_SHARD = 554
W_IN_GROUPS = [(0, 512, 1024), (512, 1536, 0), (1536, 1544, 1920), (1544, 1928, 1536), (1928, 2184, 2048),
               (2184, 2216, 2368)]


def _pad_w_in(ws):
    r, dt = ws.shape[1], ws.dtype

    def cols(a, b):
        out = []
        for k in range(N_SHARD):
            lo, hi = max(a, k * W_IN_SHARD), min(b, (k + 1) * W_IN_SHARD)
            if lo < hi:
                out.append(ws[k][:, lo - k * W_IN_SHARD:hi - k * W_IN_SHARD])
        return out

    return jnp.concatenate(cols(512, 1536) + cols(0, 512) + cols(1544, 1928) + cols(1536, 1544) + [_z(r, 120, dt)]
                           + cols(1928, 2184) + [_z(r, 64, dt)] + cols(2184, 2216) + [_z(r, 32, dt), _z(r, 128, dt)],
                           axis=1)


def _unpad_w_in(d):
    shards = []
    for k in range(N_SHARD):
        a, b = k * W_IN_SHARD, (k + 1) * W_IN_SHARD
        parts = []
        for o0, o1, p0 in W_IN_GROUPS:
            lo, hi = max(a, o0), min(b, o1)
            if lo < hi:
                parts.append(d[:, p0 + lo - o0:p0 + hi - o0])
        shards.append(jnp.concatenate(parts, axis=1))
    return jnp.stack(shards)


def _pad_heads(w, width):
    r = w.shape[0]
    w3 = w.reshape(r, MLA_HEADS, width)
    return jnp.pad(w3, ((0, 0), (0, 0), (0, 128 - width))).reshape(r, MLA_HEADS * 128)


def _row(v, width=None):
    v = v.reshape(1, -1).astype(F32)
    if width is not None and v.shape[1] < width:
        v = jnp.pad(v, ((0, 0), (0, width - v.shape[1])))
    return v


BIG = {
    "w_in": (1024, 2216, 1), "w_q_up": (384, 768, 1), "w_kv_up": (256, 1024, 1), "w_mix_out": (1024, 1024, 0),
    "w_mem_q": (1024, 1024, 0), "w_mem_k": (1024, 1024, 0), "w_mem_v": (1024, 1024, 0), "w_mem_o": (1024, 1024, 0),
    "w_up": (1024, 4096, 1), "w_down": (4096, 1024, 0), "conv_w": (4, 1024, 1),
}
BIG_ORDER = list(BIG)
SMALL_ORDER = ["ln_in_g", "ln_in_b", "conv_b", "dt_bias", "a_log", "d_skip", "ssd_norm_g", "q_norm_g", "kv_norm_g",
               "ln1_g", "ln1_b", "ln2_g", "ln2_b", "ln3_g", "ln3_b"]
N_SHARD = 4
N_DEV = 8
PACK_COLS = 1024
PACK_A_ROW = {"w_down": 0, "w_up": 1024, "w_mem_q": 2048, "w_mem_k": 2304, "w_mem_v": 2560, "w_mem_o": 2816,
              "w_mix_out": 3072}
PACK_A_ORDER = list(PACK_A_ROW)
PACK_A_ROWS = 3328
PACK_B_ORDER = ["w_q_up", "w_kv_up", "conv_w"]
PACK_B_ROWS = 160


def _shard_shape(name):
    r, c, ax = BIG[name]
    return (r // N_SHARD, c) if ax == 0 else (r, c // N_SHARD)


def _split_shards(name, full):
    r, c, ax = BIG[name]
    if ax == 0:
        return full.reshape(N_SHARD, -1)
    return full.reshape(r, N_SHARD, c // N_SHARD).transpose(1, 0, 2).reshape(N_SHARD, -1)


def _join_shards(name, parts):
    r, c, ax = BIG[name]
    if ax == 0:
        return parts.reshape(r, c)
    return parts.reshape(N_SHARD, r, c // N_SHARD).transpose(1, 0, 2).reshape(r, c)


def _small_all_reduce(g, step=None):
    r, cdim = g.shape
    si, so = (len(step.inputs), len(step.out_shapes)) if step else (0, 0)

    def body(g_ref, *refs):
        s_ins, out_ref, s_outs = refs[:si], refs[si], refs[si + 1:si + 1 + so]
        buf, send_sems, recv_sems = refs[si + 1 + so:si + 4 + so]
        s_sems = refs[si + 4 + so:]
        if step:
            step.start(s_ins, s_outs, s_sems)
        x, y, c, _ = _place()
        me = 4 * x + 2 * y + c
        buf[me] = g_ref[...]
        copies = []
        for d in range(1, N_DEV):
            to = me ^ d
            cp = pltpu.make_async_remote_copy(src_ref=g_ref, dst_ref=buf.at[me], send_sem=send_sems.at[d - 1],
                                              recv_sem=recv_sems.at[d - 1],
                                              device_id=(to // 4, (to // 2) % 2, to % 2), device_id_type=MESH)
            cp.start()
            copies.append(cp)
        for cp in copies:
            cp.wait()
        acc = buf[0]
        for d in range(1, N_DEV):
            acc = acc + buf[d]
        out_ref[...] = acc
        if step:
            step.finish(s_ins, s_outs, s_sems)

    res = pl.pallas_call(
        body, name="small_all_reduce",
        in_specs=[pl.BlockSpec(memory_space=pltpu.VMEM)] + [HBM] * si,
        out_specs=[pl.BlockSpec(memory_space=pltpu.VMEM)] + [HBM] * so,
        out_shape=[jax.ShapeDtypeStruct((r, cdim), F32)] + (list(step.out_shapes) if step else []),
        scratch_shapes=[pltpu.VMEM((N_DEV, r, cdim), F32), pltpu.SemaphoreType.DMA((N_DEV - 1,)),
                        pltpu.SemaphoreType.DMA((N_DEV - 1,))] + (_sem_scratch(step) if step else []),
    )(g, *(step.inputs if step else []))
    return res if step else res[0]


def _half_tile(h):
    return next(t for t in range(512, 0, -16) if h % t == 0)


def _pair_sum(gp, theirs, name):
    n, R, C = gp.shape
    H = R // 2
    tr = _half_tile(H)
    nb = H // tr

    def body(s_ref, g_ref, t_ref, o_ref):
        o_ref[...] = (g_ref[...] + t_ref[...]).astype(o_ref.dtype)

    def shard(k, s):
        return k + (k >= s[1]).astype(jnp.int32)

    me, c = _mesh_pos()
    return pl.pallas_call(
        body, name=name,
        grid_spec=pltpu.PrefetchScalarGridSpec(
            num_scalar_prefetch=1, grid=(n - 1, nb),
            in_specs=[pl.BlockSpec((1, tr, C), lambda k, i, s: (shard(k, s), s[0] * nb + i, 0)),
                      pl.BlockSpec((1, tr, C), lambda k, i, s: (shard(k, s), i, 0))],
            out_specs=pl.BlockSpec((1, tr, C), lambda k, i, s: (shard(k, s), i, 0))),
        out_shape=jax.ShapeDtypeStruct((n, H, C), BF16), compiler_params=_params(("arbitrary", "arbitrary")),
    )(jnp.stack([c, me]).astype(jnp.int32), gp, theirs)


def _chip_sum(gp, theirs, got, name):
    n, R, C = gp.shape
    H = R // 2
    tr = _half_tile(H)
    nb = H // tr

    def body(s_ref, g_ref, t_ref, r_ref, o_ref):
        acc = g_ref[0] + t_ref[0]
        for j in range(3):
            acc = acc + r_ref[j].astype(F32)
        o_ref[...] = acc

    me, c = _mesh_pos()
    return pl.pallas_call(
        body, name=name,
        grid_spec=pltpu.PrefetchScalarGridSpec(
            num_scalar_prefetch=1, grid=(nb,),
            in_specs=[pl.BlockSpec((1, tr, C), lambda i, s: (s[0], s[1] * nb + i, 0)),
                      pl.BlockSpec((1, tr, C), lambda i, s: (s[0], i, 0)),
                      pl.BlockSpec((3, tr, C), lambda i, s: (0, i, 0))],
            out_specs=pl.BlockSpec((tr, C), lambda i, s: (s[1] * nb + i, 0))),
        out_shape=jax.ShapeDtypeStruct((R, C), F32), compiler_params=_params(("arbitrary",)),
    )(jnp.stack([me, c]).astype(jnp.int32), gp, theirs, got)


def _unpack_group_b(g_c, g_b):
    g_b = g_b.reshape(N_SHARD, -1)
    WB, off = {"w_in": g_c}, 0
    for n in PACK_B_ORDER:
        sr, sc = _shard_shape(n)
        cnt = sr * sc
        if n == "conv_w":
            part = lax.bitcast_convert_type(g_b[:, off:off + 2 * cnt].reshape(N_SHARD, cnt, 2), F32)
            off += 2 * cnt
        else:
            part = g_b[:, off:off + cnt]
            off += cnt
        WB[n] = _join_shards(n, part)
    return WB


def _group_b_grads(dw_in_p, dw_q_p, dw_k_p, dw_v_pt, dconv_w8):
    return {
        "w_in": _unpad_w_in(dw_in_p),
        "w_q_up": dw_q_p.reshape(384, MLA_HEADS, 128)[:, :, :MLA_QK].reshape(384, MLA_HEADS * MLA_QK),
        "w_kv_up": jnp.concatenate([dw_k_p.reshape(MLA_KV_RANK, MLA_HEADS, 128)[:, :, :64],
                                    dw_v_pt.T.reshape(MLA_KV_RANK, MLA_HEADS, 128)[:, :, :64]], axis=2).reshape(
                                        MLA_KV_RANK, MLA_HEADS * 128),
        "conv_w": dconv_w8[0:4],
    }


def _pack_group_b(big_b):
    rows = [_split_shards(n, big_b[n]).reshape(N_SHARD, -1, PACK_COLS) for n in PACK_B_ORDER]
    used = sum(f.shape[1] for f in rows)
    rows.append(jnp.zeros((N_SHARD, PACK_B_ROWS - used, PACK_COLS), F32))
    return big_b["w_in"], jnp.concatenate(rows, axis=1)


def _local_step(x, mem, positions, target, WB, P, *, wp_a=None, g_a=None, wp_b=None):
    S = x.shape[0]
    tr = ROW_TILE
    dist = g_a is None
    g_in, b_in = _row(P["ln_in_g"]), _row(P["ln_in_b"])
    res = _rowwise(_fn_ln, [x], [g_in, b_in], [D_MODEL, (D_MODEL, BF16)], tr=tr, name="ln_in",
                   hosted=_merge_steps([_gather_step(_own_slot(w)) for w in wp_b]) if dist else None)
    h0, h0_b = res[0], res[1]
    if dist:
        WB = _unpack_group_b(res[2], res[3])
    P = {**P, "conv_w": WB["conv_w"]}
    w_in_p = _pad_w_in(WB["w_in"])
    w_q_p = _pad_heads(WB["w_q_up"], MLA_QK)
    w_kv3 = WB["w_kv_up"].reshape(MLA_KV_RANK, MLA_HEADS, 128)
    w_k_p = _pad_heads(w_kv3[:, :, :64].reshape(MLA_KV_RANK, 512), 64)
    w_v_p = _pad_heads(w_kv3[:, :, 64:].reshape(MLA_KV_RANK, 512), 64)
    w_v_pt = w_v_p.T
    conv_w8 = jnp.pad(P["conv_w"].astype(F32), ((0, 4), (0, 0)))
    conv_b = _row(P["conv_b"])
    dt_b = _row(P["dt_bias"], 128)
    a_head = -jnp.exp(P["a_log"].reshape(-1).astype(F32))
    a_row = _row(a_head, 128)
    dexp = jnp.repeat(P["d_skip"].reshape(-1).astype(F32), 64).reshape(1, 512)
    g_ssd, g_q, g_kv = _row(P["ssd_norm_g"]), _row(P["q_norm_g"]), _row(P["kv_norm_g"])
    g1, b1, g2, b2, g3, b3 = (_row(P[k]) for k in ("ln1_g", "ln1_b", "ln2_g", "ln2_b", "ln3_g", "ln3_b"))

    half = MLA_ROPE // 2
    inv_freq = jnp.power(ROPE_THETA, -jnp.arange(half, dtype=F32) / half)
    ang = inv_freq.reshape(half, 1) * positions.reshape(1, S).astype(F32)
    trig = ("cols", jnp.concatenate([jnp.cos(ang), jnp.sin(ang)], axis=0))

    proj = _mm(h0_b, w_in_p, form="nn", tn=IN_W // 2, name="mm_in")
    conv_y, xbc, dt = _rowwise(
        _fn_conv_fwd, [(proj,) + SEG_XBC, ("prev", proj) + SEG_XBC, (proj,) + SEG_DT], [conv_w8, conv_b, dt_b],
        [1024, 1024, 128], tr=tr, name="conv_fwd")
    y_ssd, hs = _ssd_fwd(xbc, dt, a_row, name="ssd_fwd")
    (y_n,) = _rowwise(_fn_ssd_post, [y_ssd, (xbc, 0, 512), (proj,) + SEG_Z], [dexp, g_ssd], [(512, BF16)], tr=tr,
                      name="ssd_post")
    q_n, kv_n = _rowwise(_fn_mla_pre, [(proj,) + SEG_QLAT, (proj,) + SEG_KVLAT], [g_q, g_kv], [384, 256], tr=tr,
                         name="mla_pre")
    qp = _mm(q_n, w_q_p, form="nn", name="mm_q_up")
    kn = _mm(kv_n, w_k_p, form="nn", name="mm_k_up")
    v_nat = _mm(kv_n, w_v_p, form="nn", out_dtype=BF16, name="mm_v_up")
    v_t = _mm(w_v_pt, kv_n, form="nt", out_dtype=BF16, name="mm_v_up_t")
    q_rot, k_full = _rowwise(_fn_rope, [qp, kn, (proj,) + SEG_KR, trig], [],
                             [(1024, BF16), (1024, BF16)], tr=tr, name="rope")
    res = _attn_fwd(q_rot, k_full, v_t, name="attn_fwd", hosted=_gather_step(_own_slot(wp_a)) if dist else None)
    o_t, lse = res[0], res[1]
    if dist:
        g_a = res[2]
    r_mix = PACK_A_ROW["w_mix_out"]
    w_mix_o = jnp.pad(g_a[2:4, r_mix:r_mix + 256].reshape(MLA_HEADS, 64, D_MODEL),
                      ((0, 0), (0, 64), (0, 0))).reshape(MLA_HEADS * 128, D_MODEL)
    mix_o = _mm(o_t, w_mix_o, form="tn", name="mm_mix_o")
    mix_y = _mm(y_n, g_a, form="nn", b_pack="w_mix_out", name="mm_mix_y")
    (h1,) = _rowwise(_fn_res2_ln, [h0, mix_o, mix_y], [g1, b1], [D_MODEL], tr=tr, name="ln1")
    qm = _mm(h1, g_a, form="nn", b_pack="w_mem_q", out_dtype=BF16, name="mm_mem_q")
    km = _mm(mem, g_a, form="nn", b_pack="w_mem_k", out_dtype=BF16, name="mm_mem_k")
    vm = _mm(mem, g_a, form="nn", b_pack="w_mem_v", out_dtype=BF16, name="mm_mem_v")
    (om,) = _rowwise(_fn_mem_fwd, [qm], [km, vm], [(D_MODEL, BF16)], tr=tr, name="mem_fwd")
    xa = _mm(om, g_a, form="nn", b_pack="w_mem_o", name="mm_mem_o")
    h2, h2_b = _rowwise(_fn_res_ln, [h1, xa], [g2, b2], [D_MODEL, (D_MODEL, BF16)], tr=tr, name="ln2")
    u = _mm(h2_b, g_a, form="nn", b_pack="w_up", out_dtype=BF16, name="mm_up")
    ff = _mm(u, g_a, form="nn", a_pro=_relu2, b_pack="w_down", name="mm_down")

    gp = lax.empty((N_SHARD, PACK_A_ROWS, PACK_COLS), F32)
    dt3, dt3_b, dg3, db3, loss = _rowwise(_fn_final, [h2, ff, target], [g3, b3], [D_MODEL, (D_MODEL, BF16)],
                                          [(1, D_MODEL), (1, D_MODEL), (1, 128)], tr=tr, name="ln3_loss")
    du = _mm(dt3_b, g_a, form="nt", b_pack="w_down", epi=(_epi_du, u), out_dtype=BF16, name="mm_down_dx")
    gp = _mm(u, dt3_b, form="tn", a_pro=_relu2, out_pack=("w_down", gp), name="mm_down_dw")
    gp = _mm(h2_b, du, form="tn", out_pack=("w_up", gp), name="mm_up_dw")
    dh2 = _mm(du, g_a, form="nt", b_pack="w_up", name="mm_up_dx")
    dt2, dg2, db2 = _rowwise(_fn_res_ln_bwd, [h1, xa, dt3, dh2], [g2], [D_MODEL], [(1, D_MODEL)] * 2, tr=tr,
                             name="ln2_bwd")
    dom = _mm(dt2, g_a, form="nt", b_pack="w_mem_o", out_dtype=BF16, name="mm_mem_o_dx")
    gp = _mm(om, dt2, form="tn", out_pack=("w_mem_o", gp), name="mm_mem_o_dw")
    dqm, dkm, dvm = _rowwise(_fn_mem_bwd, [qm, dom], [km, vm], [(D_MODEL, BF16)], [(256, D_MODEL)] * 2, tr=tr,
                             name="mem_bwd")
    gp = _mm(h1, dqm, form="tn", out_pack=("w_mem_q", gp), name="mm_mem_q_dw")
    gp = _mm(mem, dkm, form="tn", out_pack=("w_mem_k", gp), name="mm_mem_k_dw")
    gp = _mm(mem, dvm, form="tn", out_pack=("w_mem_v", gp), name="mm_mem_v_dw")
    dh1 = _mm(dqm, g_a, form="nt", b_pack="w_mem_q", name="mm_mem_q_dx")
    dt1, dg1, db1 = _rowwise(_fn_res2_ln_bwd, [h0, mix_o, mix_y, dt2, dh1], [g1], [D_MODEL], [(1, D_MODEL)] * 2,
                             tr=tr, name="ln1_bwd")
    do_t = _mm(w_mix_o, dt1, form="nt", name="mm_mix_o_dx")
    dy_n = _mm(dt1, g_a, form="nt", b_pack="w_mix_out", b_rows=512, name="mm_mix_y_dx")
    dw_mix_o = _mm(o_t, dt1, form="nn", name="mm_mix_o_dw")
    gp = _mm(y_n, dt1, form="tn", out_pack=("w_mix_out", gp), name="mm_mix_y_dw")
    gp = lax.dynamic_update_slice(
        gp, dw_mix_o.reshape(MLA_HEADS, 128, D_MODEL)[:, :64].reshape(2, 256, D_MODEL), (2, r_mix, 0))
    dproj = lax.empty((S, IN_W), BF16)
    dy_ssd, dproj, dxs_skip, ddexp, dg_ssd = _rowwise(
        _fn_ssd_post_bwd, [dy_n, y_ssd, (xbc, 0, 512), (proj,) + SEG_Z], [dexp, g_ssd],
        [512, (512, dproj, SEG_Z[0]), 512], [(1, 512)] * 2, tr=tr, name="ssd_post_bwd")
    res = _ssd_bwd(xbc, dt, a_row, hs, dy_ssd, name="ssd_bwd", hosted=_pair_exchange_step(gp) if dist else None)
    dxs, dbc, ddt, da_head = res[0], res[1], res[2], res[3]
    chip_step = None
    if dist:
        theirs_a = res[4]
        chip_step = _chip_exchange_step(_pair_sum(gp, theirs_a, "pair_sum_a"))
    res = _attn_bwd(q_rot, k_full, v_nat, o_t, do_t, lse, name="attn_bwd", hosted=chip_step)
    dq_rot, dk, dv_t = res[0], res[1], res[2]
    if dist:
        gp = _chip_sum(gp, theirs_a, res[3], "chip_sum_a")
    dqp, dproj = _rowwise(_fn_rope_bwd, [dq_rot, dk, trig], [], [(1024, BF16), (256, dproj, SEG_KR[0])], tr=tr,
                          name="rope_bwd")
    dw_q_p = _mm(q_n, dqp, form="tn", name="mm_q_up_dw")
    dq_n = _mm(dqp, w_q_p, form="nt", name="mm_q_up_dx")
    dw_k_p = _mm(kv_n, dk, form="tn", name="mm_k_up_dw")
    dkv_n1 = _mm(dk, w_k_p, form="nt", name="mm_k_up_dx")
    dw_v_pt = _mm(dv_t, kv_n, form="nn", name="mm_v_up_dw")
    dkv_n2 = _mm(dv_t, w_v_pt, form="tn", name="mm_v_up_dx")
    dyc, ddtr, dconv_b, ddt_b = _rowwise(
        _fn_conv_bwd_a, [conv_y, dxs, dxs_skip, dbc, (proj,) + SEG_DT, ddt], [dt_b], [1024, (128, BF16)],
        [(1, 1024), (1, 128)], tr=tr, name="conv_bwd_a")
    dproj, dg_q, dg_kv = _rowwise(
        _fn_mla_pre_bwd, [(proj,) + SEG_QLAT, (proj,) + SEG_KVLAT, dq_n, dkv_n1, dkv_n2, ddtr], [g_q, g_kv],
        [(SEG_KR[0] - SEG_QLAT[0], dproj, SEG_QLAT[0])], [(1, 384), (1, 256)], tr=tr, name="mla_pre_bwd")
    dproj, dconv_w8 = _rowwise(
        _fn_conv_bwd_b, [dyc, ("next", dyc, 0, 1024), (proj,) + SEG_XBC, ("prev", proj) + SEG_XBC], [conv_w8],
        [(1024, dproj, SEG_XBC[0])], [(8, 1024)], tr=tr, name="conv_bwd_b")
    res = _mm(h0_b, dproj, form="tn", tn=IN_W // 2, name="mm_in_dw", hosted=_pair_fill_step(gp) if dist else None)
    dw_in_p, red_a = (res[0], res[1]) if dist else (res, None)
    big_b = _group_b_grads(dw_in_p, dw_q_p, dw_k_p, dw_v_pt, dconv_w8)
    q_b = None
    if dist:
        gp_c, gp_b = _pack_group_b(big_b)
        theirs_c, theirs_b = _run_step(_merge_steps([_pair_exchange_step(gp_c), _pair_exchange_step(gp_b)]),
                                       "pair_exchange_b")
        dh0, got_c, got_b = _mm(dproj, w_in_p, form="nt", tk=IN_W // 2, name="mm_in_dx", hosted=_merge_steps(
            [_chip_exchange_step(_pair_sum(gp_c, theirs_c, "pair_sum_w_in")),
             _chip_exchange_step(_pair_sum(gp_b, theirs_b, "pair_sum_b"))]))
        q_b = ((gp_c, theirs_c, got_c), (gp_b, theirs_b, got_b))
        gp = red_a
    else:
        dh0 = _mm(dproj, w_in_p, form="nt", tk=IN_W // 2, name="mm_in_dx")
    grad_x, dg_in, db_in = _rowwise(_fn_in_ln_bwd, [x, dt1, dh0], [g_in], [D_MODEL], [(1, D_MODEL)] * 2, tr=tr,
                                    name="ln_in_bwd")

    small = {
        "ln_in_g": dg_in, "ln_in_b": db_in, "conv_b": dconv_b, "dt_bias": ddt_b[:, :8],
        "a_log": da_head[:, :8] * a_head.reshape(1, 8),
        "d_skip": ddexp.reshape(8, 64).sum(axis=1).reshape(1, 8),
        "ssd_norm_g": dg_ssd, "q_norm_g": dg_q, "kv_norm_g": dg_kv,
        "ln1_g": dg1, "ln1_b": db1, "ln2_g": dg2, "ln2_b": db2, "ln3_g": dg3, "ln3_b": db3,
    }
    return loss[0, 0], grad_x, (gp, q_b), big_b, small


def _adam(w, g, m, v, name):
    shape = w.shape
    w2, m2, v2 = (t.reshape(-1, shape[-1]) for t in (w, m, v))
    if isinstance(g, tuple):
        fn = lambda ctx, wv, gv, mv, vv: (*_fn_adam(ctx, wv, gv, mv, vv), gv)
        d, mn, vn, g = _rowwise(fn, [w2, (g[0], 0, shape[-1], g[1]), m2, v2], [], [shape[-1]] * 4, tr=ROW_TILE,
                                name=name)
    else:
        d, mn, vn = _rowwise(_fn_adam, [w2, g.reshape(-1, shape[-1]), m2, v2], [], [shape[-1]] * 3, tr=ROW_TILE,
                             name=name)
    return g.reshape(shape), d.reshape(shape), mn.reshape(shape), vn.reshape(shape)


def _adam_columns(w, g, m, v, name):
    cols = w.shape[0]
    step = cols // 2 if cols % 2 == 0 else cols
    blk = pl.BlockSpec((step,) + w.shape[1:], lambda i: (i, 0, 0))

    def body(w_ref, g_ref, m_ref, v_ref, d_ref, mo_ref, vo_ref):
        d_ref[...], mo_ref[...], vo_ref[...] = _fn_adam(None, w_ref[...], g_ref[...], m_ref[...], v_ref[...])

    return _call_with_step(body, None, None, [w, g, m, v], name=name, grid=(cols // step,), in_specs=[blk] * 4,
                           out_specs=[blk] * 3, out_shape=[jax.ShapeDtypeStruct(w.shape, F32)] * 3, sem=("arbitrary",))


def kernel(x, mem, positions, ln_in_g, ln_in_b, w_in, conv_w, conv_b, dt_bias, a_log, d_skip, ssd_norm_g, q_norm_g, w_q_up, kv_norm_g, w_kv_up, w_mix_out, ln1_g, ln1_b, w_mem_q, w_mem_k, w_mem_v, w_mem_o, ln2_g, ln2_b, w_up, w_down, ln3_g, ln3_b, loss_target, m_ln_in_g, m_ln_in_b, m_w_in, m_conv_w, m_conv_b, m_dt_bias, m_a_log, m_d_skip, m_ssd_norm_g, m_q_norm_g, m_w_q_up, m_kv_norm_g, m_w_kv_up, m_w_mix_out, m_ln1_g, m_ln1_b, m_w_mem_q, m_w_mem_k, m_w_mem_v, m_w_mem_o, m_ln2_g, m_ln2_b, m_w_up, m_w_down, m_ln3_g, m_ln3_b, v_ln_in_g, v_ln_in_b, v_w_in, v_conv_w, v_conv_b, v_dt_bias, v_a_log, v_d_skip, v_ssd_norm_g, v_q_norm_g, v_w_q_up, v_kv_norm_g, v_w_kv_up, v_w_mix_out, v_ln1_g, v_ln1_b, v_w_mem_q, v_w_mem_k, v_w_mem_v, v_w_mem_o, v_ln2_g, v_ln2_b, v_w_up, v_w_down, v_ln3_g, v_ln3_b):
    args = dict(locals())

    wp_a = jnp.concatenate([args[n].reshape(-1, PACK_COLS).astype(BF16) for n in PACK_A_ORDER], axis=0)
    flat = [args[n].reshape(-1).astype(BF16) for n in PACK_B_ORDER[:-1]]
    flat.append(lax.bitcast_convert_type(conv_w.reshape(-1), BF16).reshape(-1))
    used = sum(f.shape[0] for f in flat)
    flat.append(jnp.zeros((PACK_B_ROWS * PACK_COLS - used,), BF16))
    wp_b = jnp.concatenate(flat).reshape(PACK_B_ROWS, PACK_COLS)
    wp_c = w_in[0].astype(BF16)

    P = {n: args[n] for n in SMALL_ORDER}
    loss, grad_x, (red_a, ((gp_c, theirs_c, got_c), (gp_b, theirs_b, got_b))), _, gsmall = _local_step(
        x[0], mem[0], positions[0], loss_target[0], None, P, wp_a=wp_a, wp_b=(wp_c, wp_b))

    gs = _small_all_reduce(
        jnp.concatenate([_row(gsmall[n], PACK_COLS) for n in SMALL_ORDER] + [_row(loss, PACK_COLS)], axis=0))
    loss = gs[len(SMALL_ORDER), 0]
    red_c, red_b = _run_step(_merge_steps([_pair_fill_step(_chip_sum(gp_c, theirs_c, got_c, "chip_sum_w_in")),
                                           _pair_fill_step(_chip_sum(gp_b, theirs_b, got_b, "chip_sum_b"))]),
                             "pair_fill_b")

    grads, deltas, new_m, new_v = {}, {}, {}, {}
    for n in PACK_A_ORDER:
        grads[n], deltas[n], new_m[n], new_v[n] = _adam(args[n], (red_a, PACK_A_ROW[n]), args["m_" + n],
                                                        args["v_" + n], "adam_" + n)
    to_cols = lambda t: jnp.transpose(t, (2, 0, 1))
    from_cols = lambda t: jnp.transpose(t, (1, 2, 0))
    g_t = to_cols(red_c[None])
    grads["w_in"] = from_cols(g_t)
    deltas["w_in"], new_m["w_in"], new_v["w_in"] = map(
        from_cols, _adam_columns(to_cols(w_in), g_t, to_cols(m_w_in), to_cols(v_w_in), "adam_w_in"))
    off = 0
    for n in PACK_B_ORDER:
        sr, sc = _shard_shape(n)
        rows = sr * sc // PACK_COLS
        leaves = [args[n], red_b[off:off + rows].reshape(args[n].shape), args["m_" + n], args["v_" + n]]
        off += rows
        flip = (lambda t: jnp.transpose(t, (0, 2, 1))) if sc % LANES else (lambda t: t)
        grads[n], deltas[n], new_m[n], new_v[n] = map(flip, _adam(*map(flip, leaves), "adam_" + n))
    pack = lambda pre: jnp.concatenate([_row(args[pre + n], PACK_COLS) for n in SMALL_ORDER]
                                       + [jnp.zeros((1, PACK_COLS), F32)], axis=0)
    ds, ms, vs = _rowwise(_fn_adam, [pack(""), gs, pack("m_"), pack("v_")], [], [PACK_COLS] * 3, tr=16,
                          name="adam_small")
    for i, n in enumerate(SMALL_ORDER):
        cnt = args[n].size
        take = lambda t: t[i, :cnt].reshape(args[n].shape)
        grads[n], deltas[n], new_m[n], new_v[n] = take(gs), take(ds), take(ms), take(vs)

    order = ["ln_in_g", "ln_in_b", "w_in", "conv_w", "conv_b", "dt_bias", "a_log", "d_skip", "ssd_norm_g",
             "q_norm_g", "w_q_up", "kv_norm_g", "w_kv_up", "w_mix_out", "ln1_g", "ln1_b", "w_mem_q", "w_mem_k",
             "w_mem_v", "w_mem_o", "ln2_g", "ln2_b", "w_up", "w_down", "ln3_g", "ln3_b"]
    return (loss, grad_x[None], *[grads[n] for n in order], *[deltas[n] for n in order],
            *[new_m[n] for n in order], *[new_v[n] for n in order])
```

```python
import functools

import jax
import jax.numpy as jnp
from jax import lax
from jax.experimental import pallas as pl
from jax.experimental.pallas import tpu as pltpu

F32 = jnp.float32
BF16 = jnp.bfloat16
MESH = pl.DeviceIdType.MESH

D_MODEL = 1024
SSD_CHUNK = 128
SSD_STATE = 128
MLA_HEADS = 8
MLA_ROPE = 32
MLA_QK = 96
MLA_KV_RANK = 256
ROPE_THETA = 10000.0
MEM_HEADS = 4
MEM_HEAD_DIM = 256
LN_EPS = 1e-5
RMS_EPS = 1e-6
ALPHA = 2.0 ** 0.25
ADAM_LR = 0.001
ADAM_B1 = 0.9
ADAM_B2 = 0.999
ADAM_EPS = 1e-08
ADAM_WD = 0.01
ADAM_STEP = 10

LANES = 128
IN_W = 2560
SEG_XBC = (0, 1024)
SEG_Z = (1024, 512)
SEG_QLAT = (1536, 384)
SEG_DT = (1920, 128)
SEG_KVLAT = (2048, 256)
SEG_KR = (2304, 128)
VMEM_LIMIT = 56 * 1024 * 1024
ATTN_TILE = 512
ROW_TILE = 1024
SSD_PER_STEP = 2
NEG = -1e30
MLA_SCALE = MLA_QK ** -0.5
MEM_SCALE = MEM_HEAD_DIM ** -0.5

NN = (((1,), (0,)), ((), ()))
NT = (((1,), (1,)), ((), ()))
TN = (((0,), (0,)), ((), ()))


def _dot(a, b, dims=NN):
    return lax.dot_general(a.astype(BF16), b.astype(BF16), dims, preferred_element_type=F32)


def _dot_exact(a, b):
    return lax.dot_general(a, b, NN, precision=lax.Precision.HIGHEST, preferred_element_type=F32)


def _pick(dim, pref):
    t = min(pref, dim)
    t -= t % LANES
    while t >= LANES:
        if dim % t == 0:
            return t
        t -= LANES
    return dim


def _params(sem):
    return pltpu.CompilerParams(dimension_semantics=sem, vmem_limit_bytes=VMEM_LIMIT)


def _pack_caps(wname):
    r, c, ax = BIG[wname]
    if ax == 0:
        return (r if r <= 1024 else r // N_SHARD), c
    return r, c // N_SHARD


def _pack_block(wname, br, bc):
    r, c, ax = BIG[wname]
    r0 = PACK_A_ROW[wname]
    sr = r // N_SHARD if ax == 0 else r
    if ax == 0 and br > sr:
        assert br % sr == 0 and r0 % sr == 0
        return (br // sr, sr, bc), lambda rb, cb: (rb, r0 // sr, cb)
    assert r0 % br == 0
    if ax == 0:
        per = sr // br
        return (1, br, bc), lambda rb, cb: (rb // per, r0 // br + rb % per, cb)
    per = (c // N_SHARD) // bc
    return (1, br, bc), lambda rb, cb: (cb // per, r0 // br + rb, cb % per)


def _mm(a, b, *, form, name, a_pro=None, epi=None, out_dtype=F32, tm=1024, tn=1024, tk=1024, b_pack=None,
        b_rows=None, out_pack=None, hosted=None):
    b_shape = BIG[b_pack][:2] if b_pack else b.shape
    if b_pack and form == "nt":
        b_shape = (b_rows or b_shape[0], b_shape[1])
    if form == "nn":
        (m, k), (_, n) = a.shape, b_shape
    elif form == "nt":
        (m, k), (n, _) = a.shape, b_shape
    else:
        (k, m), (_, n) = a.shape, b_shape
    if b_pack:
        rcap, ccap = _pack_caps(b_pack)
        tk, tn = (min(tk, rcap), min(tn, ccap)) if form == "nn" else (min(tk, ccap), min(tn, rcap))
    if out_pack:
        rcap, ccap = _pack_caps(out_pack[0])
        tm, tn = min(tm, rcap), min(tn, ccap)
    tm, tn, tk = _pick(m, tm), _pick(n, tn), _pick(k, tk)
    dims = {"nn": NN, "nt": NT, "tn": TN}[form]
    nk = k // tk
    direct = out_dtype == F32 and epi is None
    n_extra = (1 if epi else 0) + (1 if out_pack else 0)

    def body(a_ref, b_ref, *rest):
        o_ref = rest[n_extra]
        acc_ref = o_ref if direct else rest[-1]

        @pl.when(pl.program_id(2) == 0)
        def _():
            acc_ref[...] = jnp.zeros_like(acc_ref)

        av = a_ref[...]
        if a_pro is not None:
            av = a_pro(av)
        bv = b_ref[...]
        acc_ref[...] += _dot(av, bv.reshape(-1, bv.shape[-1]), dims).reshape(acc_ref.shape)
        if not direct:
            @pl.when(pl.program_id(2) == nk - 1)
            def _():
                val = acc_ref[...]
                if epi is not None:
                    val = epi[0](val, rest[0][...])
                o_ref[...] = val.reshape(o_ref.shape).astype(o_ref.dtype)

    if form == "tn":
        a_spec = pl.BlockSpec((tk, tm), lambda i, j, kk: (kk, i))
    else:
        a_spec = pl.BlockSpec((tm, tk), lambda i, j, kk: (i, kk))
    if b_pack:
        shape, idx = _pack_block(b_pack, *((tk, tn) if form == "nn" else (tn, tk)))
        b_spec = pl.BlockSpec(shape, (lambda i, j, kk: idx(kk, j)) if form == "nn" else (lambda i, j, kk: idx(j, kk)))
    elif form == "nt":
        b_spec = pl.BlockSpec((tn, tk), lambda i, j, kk: (j, kk))
    else:
        b_spec = pl.BlockSpec((tk, tn), lambda i, j, kk: (kk, j))
    in_specs, args = [a_spec, b_spec], [a, b]
    out_spec = pl.BlockSpec((tm, tn), lambda i, j, kk: (i, j))
    out_sds, aliases = jax.ShapeDtypeStruct((m, n), out_dtype), {}
    if epi is not None:
        in_specs.append(out_spec)
        args.append(epi[1])
    if out_pack:
        wname, buf = out_pack
        shape, idx = _pack_block(wname, tm, tn)
        out_spec = pl.BlockSpec(shape, lambda i, j, kk: idx(i, j))
        out_sds, aliases = jax.ShapeDtypeStruct(buf.shape, buf.dtype), {len(args): 0}
        in_specs.append(HBM)
        args.append(buf)
    acc_shape = out_spec.block_shape if out_pack else (tm, tn)
    res = _call_with_step(
        body, hosted, None, args, name=name, grid=(m // tm, n // tn, nk), in_specs=in_specs, out_specs=[out_spec],
        out_shape=[out_sds], sem=("parallel", "parallel", "arbitrary"), aliases=aliases,
        scratch_shapes=[] if direct else [pltpu.VMEM(acc_shape, F32)])
    return res[0] if hosted is None else res


class _Ctx:
    def __init__(self, i, n):
        self.i, self.n = i, n


def _rowwise(fn, rows, consts, row_outs, acc_outs=(), *, tr, name, n_rows=None, hosted=None):
    norm = []
    for r in rows:
        kind = "tile"
        if isinstance(r, tuple) and isinstance(r[0], str):
            kind, r = r[0], r[1:]
        if kind == "cols":
            norm.append((kind, r[0], 0, r[0].shape[0], 0))
            continue
        row0 = 0
        if isinstance(r, tuple) and len(r) == 4:
            r, row0 = r[:3], r[3]
        arr, col0, width = r if isinstance(r, tuple) else (r, 0, r.shape[1])
        assert col0 % width == 0
        norm.append((kind, arr, col0 // width, width, row0))
    n_rows = n_rows or next(a.shape[0] for k, a, _, _, _ in norm if k == "tile")
    tr = min(tr, n_rows)
    row_bytes = sum(w * a.dtype.itemsize for k, a, _, w, _ in norm if k in ("tile", "cols"))
    row_bytes += sum((ro[0] * ro[1].dtype.itemsize if len(ro) == 3 else ro[0] * jnp.dtype(ro[1]).itemsize)
                     if isinstance(ro, tuple) else ro * 4 for ro in row_outs)
    while 3 * tr * row_bytes > VMEM_LIMIT and tr % 16 == 0:
        tr //= 2
    while n_rows % tr:
        tr -= 8
    n = n_rows // tr
    arrs, specs = [], []
    for kind, arr, cb, width, row0 in norm:
        if kind == "tile":
            assert row0 % tr == 0
            specs.append(pl.BlockSpec((tr, width), lambda i, cb=cb, rb=row0 // tr: (i + rb, cb)))
        elif kind == "cols":
            specs.append(pl.BlockSpec((width, tr), lambda i: (0, i)))
        elif kind == "prev":
            specs.append(pl.BlockSpec((8, width), lambda i, cb=cb: (jnp.maximum(i * (tr // 8) - 1, 0), cb)))
        else:
            specs.append(pl.BlockSpec((8, width), lambda i, cb=cb: (jnp.minimum((i + 1) * (tr // 8), n_rows // 8 - 1), cb)))
        arrs.append(arr)
    for c in consts:
        specs.append(pl.BlockSpec(c.shape, lambda i, nd=c.ndim: (0,) * nd))
        arrs.append(c)
    n_in, n_ro = len(arrs), len(row_outs)
    out_shape, out_specs, aliases = [], [], {}
    for j, ro in enumerate(row_outs):
        w, dt, col0 = (ro + (None,))[:3] if isinstance(ro, tuple) else (ro, F32, None)
        if col0 is None:
            out_shape.append(jax.ShapeDtypeStruct((n_rows, w), dt))
            out_specs.append(pl.BlockSpec((tr, w), lambda i: (i, 0)))
        else:
            assert col0 % w == 0 and dt.shape[0] == n_rows
            out_shape.append(jax.ShapeDtypeStruct(dt.shape, dt.dtype))
            out_specs.append(pl.BlockSpec((tr, w), lambda i, cb=col0 // w: (i, cb)))
            aliases[len(arrs)] = j
            arrs.append(dt)
            specs.append(HBM)
    n_all = len(arrs)
    out_shape += [jax.ShapeDtypeStruct(s, F32) for s in acc_outs]
    out_specs += [pl.BlockSpec(s, lambda i: (0, 0)) for s in acc_outs]

    def body(*refs):
        i = pl.program_id(0)
        vals = [r[...] for r in refs[:n_in]]
        outs = fn(_Ctx(i, n), *vals)
        if not isinstance(outs, (tuple, list)):
            outs = (outs,)
        o_refs = refs[n_all:]
        for o_ref, o in zip(o_refs[:n_ro], outs[:n_ro]):
            o_ref[...] = o.astype(o_ref.dtype)
        if acc_outs:
            @pl.when(i == 0)
            def _():
                for o_ref in o_refs[n_ro:]:
                    o_ref[...] = jnp.zeros_like(o_ref)

            for o_ref, o in zip(o_refs[n_ro:], outs[n_ro:]):
                o_ref[...] += jnp.broadcast_to(o, o_ref.shape)

    return _call_with_step(body, hosted, None, arrs, name=name, grid=(n,), in_specs=specs, out_specs=out_specs,
                           out_shape=out_shape, sem=("arbitrary",), aliases=aliases)


def _sum0(v):
    return jnp.sum(v, axis=0, keepdims=True)


def _mean1(v):
    return jnp.mean(v, axis=-1, keepdims=True)


def _sigmoid(v):
    return 1.0 / (1.0 + jnp.exp(-v))


def _ln_stats(t):
    xc = t - _mean1(t)
    rstd = lax.rsqrt(_mean1(xc * xc) + LN_EPS)
    return xc * rstd, rstd


def _ln_bwd(xhat, rstd, dy, g):
    dxh = dy * g
    dx = rstd * (dxh - _mean1(dxh) - xhat * _mean1(dxh * xhat))
    return dx, _sum0(dy * xhat), _sum0(dy)


def _rms_fwd(v, g):
    return v * lax.rsqrt(_mean1(v * v) + RMS_EPS) * g


def _rms_bwd(v, dy, g):
    rs = lax.rsqrt(_mean1(v * v) + RMS_EPS)
    vh = v * rs
    dyg = dy * g
    return rs * (dyg - vh * _mean1(dyg * vh)), _sum0(dy * vh)


def _lane(shape):
    return lax.broadcasted_iota(jnp.int32, shape, len(shape) - 1)


def _shift_down(u, halo, s, is_first):
    tr = u.shape[0]
    rolled = pltpu.roll(u, s, 0)
    hr = jnp.where(is_first, 0.0, pltpu.roll(halo, s, 0))
    row = lax.broadcasted_iota(jnp.int32, hr.shape, 0)
    top = jnp.where(row < s, hr, rolled[0:8])
    if tr == 8:
        return top
    return jnp.concatenate([top, rolled[8:]], axis=0)


def _shift_up(d, halo, s, is_last):
    tr = d.shape[0]
    rolled = pltpu.roll(d, tr - s, 0)
    hr = jnp.where(is_last, 0.0, pltpu.roll(halo, 8 - s, 0))
    row = lax.broadcasted_iota(jnp.int32, hr.shape, 0)
    bot = jnp.where(row >= 8 - s, hr, rolled[tr - 8:])
    if tr == 8:
        return bot
    return jnp.concatenate([rolled[:tr - 8], bot], axis=0)


def _rope_tables(trig):
    t = jnp.concatenate([trig] * (LANES // trig.shape[0]), axis=0).T
    lane = _lane(t.shape)
    first, second = (lane >= 64) & (lane < 80), (lane >= 80) & (lane < 96)
    ta = jnp.where(lane < 64, 1.0, jnp.where(first, pltpu.roll(t, 64, 1), jnp.where(second, pltpu.roll(t, 80, 1), 0.0)))
    return ta, jnp.where(second, pltpu.roll(t, 64, 1), 0.0), jnp.where(first, -pltpu.roll(t, 48, 1), 0.0)


def _rope(v, ta, tb, tc):
    return v * ta + pltpu.roll(v, 16, 1) * tb + pltpu.roll(v, LANES - 16, 1) * tc


def _rope_bwd(d, ta, tb, tc):
    return d * ta + pltpu.roll(d * tb, LANES - 16, 1) + pltpu.roll(d * tc, 16, 1)


def _ssd_common(dtv, a_row):
    L = SSD_CHUNK
    a = dtv * a_row
    r = lax.broadcasted_iota(jnp.int32, (L, L), 0)
    c = lax.broadcasted_iota(jnp.int32, (L, L), 1)
    tril = r >= c
    cs = _dot_exact(tril.astype(F32), a)
    cs_t = cs.T
    cs_last = cs[L - 1:L, :]
    return dict(a=a, tril=tril, cs=cs, cs_t=cs_t, ecs=jnp.exp(cs), dte=jnp.exp(cs_last - cs),
                elast=jnp.exp(cs_last))


def _pair_sel(v, h0, lo):
    return jnp.where(lo, v[:, h0:h0 + 1], v[:, h0 + 1:h0 + 2])


def _ssd_pair(cm, h0, cb, xp, dtv, bmat, cmat, hp, lo):
    x = xp * _pair_sel(dtv, h0, lo)
    lam0 = jnp.exp(jnp.where(cm["tril"], cm["cs"][:, h0:h0 + 1] - cm["cs_t"][h0:h0 + 1, :], NEG))
    lam1 = jnp.exp(jnp.where(cm["tril"], cm["cs"][:, h0 + 1:h0 + 2] - cm["cs_t"][h0 + 1:h0 + 2, :], NEG))
    m0, m1 = cb * lam0, cb * lam1
    ydiag = jnp.where(lo, _dot(m0, x), _dot(m1, x))
    ecs_p = _pair_sel(cm["ecs"], h0, lo)
    dte_p = _pair_sel(cm["dte"], h0, lo)
    yoff = _dot(cmat, hp, NT) * ecs_p
    xd = x * dte_p
    st = _dot(xd, bmat, TN)
    rlo = lax.broadcasted_iota(jnp.int32, (LANES, SSD_STATE), 0) < 64
    decay = jnp.where(rlo, cm["elast"][:, h0:h0 + 1], cm["elast"][:, h0 + 1:h0 + 2])
    h_next = hp * decay + st
    return dict(x=x, lam0=lam0, lam1=lam1, m0=m0, m1=m1, y=ydiag + yoff, yoff=yoff, ecs_p=ecs_p, dte_p=dte_p,
                xd=xd, decay=decay, h_next=h_next)


def _ssd_fwd(xbc, dt, a_row, *, name):
    S = xbc.shape[0]
    L = SSD_CHUNK
    nc = S // L
    per = SSD_PER_STEP if nc % SSD_PER_STEP == 0 else 1
    G = per * L

    def body(xs_ref, bm_ref, cm_ref, dt_ref, a_ref, y_ref, hs_ref, h_scr):
        @pl.when(pl.program_id(0) == 0)
        def _():
            h_scr[...] = jnp.zeros_like(h_scr)

        lo = _lane((L, LANES)) < 64
        for sub in range(per):
            rows = slice(sub * L, (sub + 1) * L)
            dtv = dt_ref[rows, :]
            cm = _ssd_common(dtv, a_ref[...])
            ys = []
            for g in range(2):
                bmat = bm_ref[rows, g * 128:(g + 1) * 128]
                cmat = cm_ref[rows, g * 128:(g + 1) * 128]
                cb = _dot(cmat, bmat, NT)
                for pr in range(2):
                    p4 = 2 * g + pr
                    hp = h_scr[p4]
                    hs_ref[sub, p4 * 128:(p4 + 1) * 128, :] = hp
                    t = _ssd_pair(cm, 2 * p4, cb, xs_ref[rows, p4 * 128:(p4 + 1) * 128], dtv, bmat, cmat, hp, lo)
                    ys.append(t["y"])
                    h_scr[p4] = t["h_next"]
            y_ref[rows, :] = jnp.concatenate(ys, axis=1)

    return pl.pallas_call(
        body, name=name, grid=(nc // per,),
        in_specs=[pl.BlockSpec((G, 512), lambda c: (c, 0)), pl.BlockSpec((G, 256), lambda c: (c, 2)),
                  pl.BlockSpec((G, 256), lambda c: (c, 3)), pl.BlockSpec((G, 128), lambda c: (c, 0)),
                  pl.BlockSpec((1, 128), lambda c: (0, 0))],
        out_specs=[pl.BlockSpec((G, 512), lambda c: (c, 0)), pl.BlockSpec((per, 512, 128), lambda c: (c, 0, 0))],
        out_shape=[jax.ShapeDtypeStruct((S, 512), F32), jax.ShapeDtypeStruct((nc, 512, 128), F32)],
        scratch_shapes=[pltpu.VMEM((4, 128, 128), F32)],
        compiler_params=_params(("arbitrary",)),
    )(xbc, xbc, xbc, dt, a_row)


def _ssd_bwd(xbc, dt, a_row, hs, dy, *, name, hosted=None):
    S = xbc.shape[0]
    L = SSD_CHUNK
    nc = S // L
    per = SSD_PER_STEP if nc % SSD_PER_STEP == 0 else 1
    G = per * L

    def body(xs_ref, bm_ref, cm_ref, dt_ref, a_ref, hs_ref, dy_ref, dxs_ref, dbc_ref, ddt_ref, da_ref, g_scr):
        @pl.when(pl.program_id(0) == 0)
        def _():
            g_scr[...] = jnp.zeros_like(g_scr)
            da_ref[...] = jnp.zeros_like(da_ref)

        for sub in reversed(range(per)):
            rows = pl.ds(sub * L, L)
            chunk(xs_ref.at[rows, :], bm_ref.at[rows, :], cm_ref.at[rows, :], dt_ref.at[rows, :], a_ref,
                  hs_ref.at[pl.ds(sub, 1)], dy_ref.at[rows, :], dxs_ref.at[rows, :], dbc_ref.at[rows, :],
                  ddt_ref.at[rows, :], da_ref, g_scr)

    def chunk(xs_ref, bm_ref, cm_ref, dt_ref, a_ref, hs_ref, dy_ref, dxs_ref, dbc_ref, ddt_ref, da_ref, g_scr):
        dtv = dt_ref[...]
        a_row_v = a_ref[...]
        cm = _ssd_common(dtv, a_row_v)
        lo = _lane((L, LANES)) < 64
        lane_row = _lane((1, LANES))
        ri = lax.broadcasted_iota(jnp.int32, (L, L), 0)
        ci = lax.broadcasted_iota(jnp.int32, (L, L), 1)
        triu = (ri <= ci).astype(F32)
        stril = ri > ci

        def halves(v, mask):
            return (jnp.sum(jnp.where(mask, v, 0.0), axis=1, keepdims=True),
                    jnp.sum(jnp.where(mask, 0.0, v), axis=1, keepdims=True))

        i_all = jnp.zeros((L, LANES), F32)
        yo_all = jnp.zeros((L, LANES), F32)
        w_all = jnp.zeros((L, LANES), F32)
        ddt_x = jnp.zeros((L, LANES), F32)
        e_row = jnp.zeros((1, LANES), F32)
        rlo = lax.broadcasted_iota(jnp.int32, (LANES, SSD_STATE), 0) < 64
        dxs, dbs, dcs = [], [], []
        for g in range(2):
            bmat = bm_ref[:, g * 128:(g + 1) * 128]
            cmat = cm_ref[:, g * 128:(g + 1) * 128]
            cb = _dot(cmat, bmat, NT)
            dcb = jnp.zeros((L, L), F32)
            db = jnp.zeros((L, SSD_STATE), F32)
            dc = jnp.zeros((L, SSD_STATE), F32)
            for pr in range(2):
                p4 = 2 * g + pr
                h0 = 2 * p4
                hp = hs_ref[0, p4 * 128:(p4 + 1) * 128, :]
                xp = xs_ref[:, p4 * 128:(p4 + 1) * 128]
                t = _ssd_pair(cm, h0, cb, xp, dtv, bmat, cmat, hp, lo)
                gst = g_scr[p4]
                dyp = dy_ref[:, p4 * 128:(p4 + 1) * 128]
                dy0 = jnp.where(lo, dyp, 0.0)
                dy1 = dyp - dy0
                bg = _dot(bmat, gst, NT)
                dx = _dot(t["m0"], dy0, TN) + _dot(t["m1"], dy1, TN) + bg * t["dte_p"]
                dm0, dm1 = _dot(dy0, t["x"], NT), _dot(dy1, t["x"], NT)
                dcb = dcb + dm0 * t["lam0"] + dm1 * t["lam1"]
                dye = dyp * t["ecs_p"]
                dc = dc + _dot(dye, hp)
                db = db + _dot(t["xd"], gst)
                i0 = jnp.sum(jnp.where(stril, _dot(triu, dm0 * t["m0"]), 0.0), axis=1, keepdims=True)
                i1 = jnp.sum(jnp.where(stril, _dot(triu, dm1 * t["m1"]), 0.0), axis=1, keepdims=True)
                yo0, yo1 = halves(dyp * t["yoff"], lo)
                w0, w1 = halves(t["xd"] * bg, lo)
                gh = gst * (hp * t["decay"])
                e0 = _sum0(jnp.sum(jnp.where(rlo, gh, 0.0), axis=1, keepdims=True))
                e1 = _sum0(jnp.sum(jnp.where(rlo, 0.0, gh), axis=1, keepdims=True))
                x0, x1 = halves(dx * xp, lo)
                oh0 = (lane_row == h0).astype(F32)
                oh1 = (lane_row == h0 + 1).astype(F32)
                i_all = i_all + i0 * oh0 + i1 * oh1
                yo_all = yo_all + yo0 * oh0 + yo1 * oh1
                w_all = w_all + w0 * oh0 + w1 * oh1
                e_row = e_row + e0 * oh0 + e1 * oh1
                ddt_x = ddt_x + x0 * oh0 + x1 * oh1
                dxs.append(dx * _pair_sel(dtv, h0, lo))
                g_scr[p4] = gst * t["decay"] + _dot(dye, cmat, TN)
            dbs.append(db + _dot(dcb, cmat, TN))
            dcs.append(dc + _dot(dcb, bmat))
        da = i_all + _dot_exact(triu, yo_all) + _dot_exact(stril.astype(F32), w_all) + e_row
        ddt_ref[...] = da * a_row_v + ddt_x
        da_ref[...] += _sum0(da * dtv)
        dxs_ref[...] = jnp.concatenate(dxs, axis=1)
        dbc_ref[...] = jnp.concatenate(dbs + dcs, axis=1)

    rev = lambda c: nc // per - 1 - c
    return _call_with_step(
        body, hosted, None, (xbc, xbc, xbc, dt, a_row, hs, dy), name=name, grid=(nc // per,),
        in_specs=[pl.BlockSpec((G, 512), lambda c: (rev(c), 0)), pl.BlockSpec((G, 256), lambda c: (rev(c), 2)),
                  pl.BlockSpec((G, 256), lambda c: (rev(c), 3)), pl.BlockSpec((G, 128), lambda c: (rev(c), 0)),
                  pl.BlockSpec((1, 128), lambda c: (0, 0)), pl.BlockSpec((per, 512, 128), lambda c: (rev(c), 0, 0)),
                  pl.BlockSpec((G, 512), lambda c: (rev(c), 0))],
        out_specs=[pl.BlockSpec((G, 512), lambda c: (rev(c), 0)), pl.BlockSpec((G, 512), lambda c: (rev(c), 0)),
                   pl.BlockSpec((G, 128), lambda c: (rev(c), 0)), pl.BlockSpec((1, 128), lambda c: (0, 0))],
        out_shape=[jax.ShapeDtypeStruct((S, 512), F32), jax.ShapeDtypeStruct((S, 512), F32),
                   jax.ShapeDtypeStruct((S, 128), F32), jax.ShapeDtypeStruct((1, 128), F32)],
        sem=("arbitrary",), scratch_shapes=[pltpu.VMEM((4, 128, 128), F32)])


HBM = pl.BlockSpec(memory_space=pl.ANY)


class _Step:
    def __init__(self, inputs, out_shapes, n_sems, start, finish, mid=None):
        self.inputs, self.out_shapes, self.n_sems = inputs, out_shapes, n_sems
        self.start, self.finish, self.mid = start, finish, mid
        self.alias = []


class _Shifted:
    def __init__(self, ref, off):
        self.ref, self.off = ref, off

    @property
    def at(self):
        return self

    def __getitem__(self, j):
        return self.ref.at[self.off + j]


def _merge_steps(steps):
    offs = [sum(s.n_sems for s in steps[:i]) for i in range(len(steps) + 1)]
    i_offs = [sum(len(s.inputs) for s in steps[:i]) for i in range(len(steps))]
    o_offs = [sum(len(s.out_shapes) for s in steps[:i]) for i in range(len(steps))]

    def phase(which):
        def run(ins, outs, sems):
            for s, off, i0, o0 in zip(steps, offs, i_offs, o_offs):
                fn = getattr(s, which)
                if fn is not None:
                    fn(ins[i0:i0 + len(s.inputs)], outs[o0:o0 + len(s.out_shapes)],
                       [_Shifted(sems[0], off), _Shifted(sems[1], off)])
        return run

    merged = _Step([a for s in steps for a in s.inputs], [o for s in steps for o in s.out_shapes], offs[-1],
                   phase("start"), phase("finish"), phase("mid") if any(s.mid for s in steps) else None)
    merged.alias = [(i0 + a, o0 + b) for s, i0, o0 in zip(steps, i_offs, o_offs) for a, b in s.alias]
    return merged


def _place():
    x, y, c = lax.axis_index("x"), lax.axis_index("y"), lax.axis_index("c")
    chips = [(1 - x, y), (x, 1 - y), (1 - x, 1 - y)]
    return x, y, c, chips


def _mesh_pos():
    return 2 * lax.axis_index("x") + lax.axis_index("y"), lax.axis_index("c")


def _chunks(rows, tile):
    return next(n for n in (4, 3, 2, 1) if rows % (n * tile) == 0)


def _remote(src, dst, sems, j, to):
    return pltpu.make_async_remote_copy(src_ref=src, dst_ref=dst, send_sem=sems[0].at[j], recv_sem=sems[1].at[j],
                                        device_id=to, device_id_type=MESH)


def _own_slot(wp):
    return lax.dynamic_update_slice(lax.empty((N_SHARD,) + wp.shape, wp.dtype), wp[None], (_mesh_pos()[0], 0, 0))


def _gather_step(buf):
    _, R, C = buf.shape
    H = R // 2
    nq = _chunks(H, 16)
    CH = H // nq

    def copies(ins, outs, sems):
        x, y, c, chips = _place()
        sib, me = (x, y, 1 - c), 2 * x + y
        w_ref, out_ref = ins[0], outs[0]

        def piece(k, hc, q):
            return out_ref.at[k, pl.ds(hc * H + q * CH, CH), :]

        sends, landed, fwds, fwd_landed = [], [], [], []
        for q in range(nq):
            for j, (px, py) in enumerate(chips):
                k = 2 * px + py
                sends.append(_remote(w_ref.at[me, pl.ds(c * H + q * CH, CH), :], piece(me, c, q), sems, j * nq + q,
                                     (px, py, c)))
                landed.append(_remote(piece(k, c, q), piece(k, c, q), sems, j * nq + q, (px, py, c)))
                fwds.append(_remote(piece(k, c, q), piece(k, c, q), sems, (3 + j) * nq + q, sib))
                fwd_landed.append(_remote(piece(k, 1 - c, q), piece(k, 1 - c, q), sems, (3 + j) * nq + q, sib))
        return sends, landed, fwds, fwd_landed

    def start(ins, outs, sems):
        for cp in copies(ins, outs, sems)[0]:
            cp.start()

    def mid(ins, outs, sems):
        _, landed, fwds, _ = copies(ins, outs, sems)
        for arrived, onward in zip(landed, fwds):
            arrived.wait_recv()
            onward.start()

    def finish(ins, outs, sems):
        sends, _, fwds, fwd_landed = copies(ins, outs, sems)
        for cp in fwd_landed:
            cp.wait_recv()
        for cp in sends + fwds:
            cp.wait_send()

    step = _Step([buf], [jax.ShapeDtypeStruct(buf.shape, buf.dtype)], 6 * nq, start, finish, mid)
    step.alias = [(0, 0)]
    return step


def _pair_exchange_step(gp):
    n, R, C = gp.shape
    H = R // 2
    nq = _chunks(H, 8)
    CH = H // nq

    def copies(ins, outs, sems):
        x, y, c, _ = _place()
        return [_remote(ins[0].at[k, pl.ds((1 - c) * H + q * CH, CH), :], outs[0].at[k, pl.ds(q * CH, CH), :], sems,
                        k * nq + q, (x, y, 1 - c)) for k in range(n) for q in range(nq)]

    def start(ins, outs, sems):
        for cp in copies(ins, outs, sems):
            cp.start()

    def finish(ins, outs, sems):
        for cp in copies(ins, outs, sems):
            cp.wait()

    return _Step([gp], [jax.ShapeDtypeStruct((n, H, C), gp.dtype)], n * nq, start, finish)


def _chip_exchange_step(pb):
    n, H, C = pb.shape
    nq = _chunks(H, 16)
    CH = H // nq

    def copies(ins, outs, sems):
        x, y, c, chips = _place()
        return [_remote(ins[0].at[2 * px + py, pl.ds(q * CH, CH), :], outs[0].at[j, pl.ds(q * CH, CH), :], sems,
                        j * nq + q, (px, py, c)) for q in range(nq) for j, (px, py) in enumerate(chips)]

    def start(ins, outs, sems):
        for cp in copies(ins, outs, sems):
            cp.start()

    def finish(ins, outs, sems):
        for cp in copies(ins, outs, sems):
            cp.wait()

    return _Step([pb], [jax.ShapeDtypeStruct((3, H, C), pb.dtype)], 3 * nq, start, finish)


def _pair_fill_step(red):
    R, C = red.shape
    H = R // 2
    nq = _chunks(H, 8)
    CH = H // nq

    def copies(ins, outs, sems):
        x, y, c, _ = _place()
        return [_remote(ins[0].at[pl.ds(c * H + j * CH, CH), :], outs[0].at[pl.ds(c * H + j * CH, CH), :], sems, j,
                        (x, y, 1 - c)) for j in range(nq)]

    def start(ins, outs, sems):
        for cp in copies(ins, outs, sems):
            cp.start()

    def finish(ins, outs, sems):
        for cp in copies(ins, outs, sems):
            cp.wait()

    step = _Step([red], [jax.ShapeDtypeStruct((R, C), red.dtype)], nq, start, finish)
    step.alias = [(0, 0)]
    return step


def _sem_scratch(step):
    return [pltpu.SemaphoreType.DMA((step.n_sems,)), pltpu.SemaphoreType.DMA((step.n_sems,))]


def _run_step(step, name):
    ni, no = len(step.inputs), len(step.out_shapes)

    def body(*refs):
        ins, outs, sems = refs[:ni], refs[ni:ni + no], refs[ni + no:]
        step.start(ins, outs, sems)
        if step.mid is not None:
            step.mid(ins, outs, sems)
        step.finish(ins, outs, sems)

    return pl.pallas_call(body, name=name, in_specs=[HBM] * ni, out_specs=[HBM] * no, out_shape=step.out_shapes,
                          input_output_aliases=dict(step.alias),
                          scratch_shapes=_sem_scratch(step))(*step.inputs)


def _grid_flags(grid):
    ids = [pl.program_id(d) for d in range(len(grid))]
    first = functools.reduce(lambda a, b: a & b, [i == 0 for i in ids])
    last = functools.reduce(lambda a, b: a & b, [i == n - 1 for i, n in zip(ids, grid)])
    return first, last, last


def _call_with_step(core, step, flags, args, *, name, grid, in_specs, out_specs, out_shape, sem, scratch_shapes=(),
                    aliases=None):
    aliases = aliases or {}
    if step is None:
        return pl.pallas_call(core, name=name, grid=grid, in_specs=in_specs, out_specs=out_specs,
                              out_shape=out_shape, scratch_shapes=list(scratch_shapes),
                              input_output_aliases=aliases, compiler_params=_params(sem))(*args)
    n_in, n_out, n_scr = len(in_specs), len(out_specs), len(scratch_shapes)
    si, so = len(step.inputs), len(step.out_shapes)
    flags = flags or (lambda: _grid_flags(grid))
    aliases = {**aliases, **{n_in + a: n_out + b for a, b in step.alias}}

    def body(*refs):
        ins, s_ins = refs[:n_in], refs[n_in:n_in + si]
        outs = refs[n_in + si:n_in + si + n_out]
        s_outs = refs[n_in + si + n_out:n_in + si + n_out + so]
        scr = refs[n_in + si + n_out + so:n_in + si + n_out + so + n_scr]
        sems = refs[n_in + si + n_out + so + n_scr:]
        first, middle, last = flags()

        @pl.when(first)
        def _():
            step.start(s_ins, s_outs, sems)

        if step.mid is not None:
            @pl.when(middle)
            def _():
                step.mid(s_ins, s_outs, sems)

        core(*ins, *outs, *scr)

        @pl.when(last)
        def _():
            step.finish(s_ins, s_outs, sems)

    return pl.pallas_call(
        body, name=name, grid=grid, in_specs=list(in_specs) + [HBM] * si, out_specs=list(out_specs) + [HBM] * so,
        out_shape=list(out_shape) + list(step.out_shapes), scratch_shapes=list(scratch_shapes) + _sem_scratch(step),
        input_output_aliases=aliases, compiler_params=_params(("arbitrary",) * len(grid)))(*args, *step.inputs)


def _attn_flags(nq):
    h, qi = pl.program_id(0), pl.program_id(1)
    return ((h == 0) & (qi == 0), (h == MLA_HEADS - 1) & (qi == 0), (h == MLA_HEADS - 1) & (qi == nq - 1))


def _att_mask(s_t, q0, k0):
    krow = k0 + lax.broadcasted_iota(jnp.int32, s_t.shape, 0)
    qcol = q0 + lax.broadcasted_iota(jnp.int32, s_t.shape, 1)
    return jnp.where(krow <= qcol, s_t, NEG)


def _loop_blocks(lo, hi, step, carry):
    n = hi - lo

    def four(i, c):
        kb = lo + 4 * i
        return step(kb + 3, step(kb + 2, step(kb + 1, step(kb, c))))

    carry = lax.fori_loop(0, n // 4, four, carry)
    base = lo + 4 * (n // 4)
    carry = lax.cond(n % 4 >= 2, lambda c: step(base + 1, step(base, c)), lambda c: c, carry)
    return lax.cond(n % 2 == 1, lambda c: step(hi - 1, c), lambda c: c, carry)


def _rows(ref, blk, t):
    return ref[pl.ds(pl.multiple_of(blk * t, t), t), :]


def _cols(ref, blk, t):
    return ref[:, pl.ds(pl.multiple_of(blk * t, t), t)]


def _attn_fwd(q, k, v_t, *, name, hosted=None):
    S = q.shape[0]
    t = min(ATTN_TILE, S)
    nq = S // t

    def body(q_ref, k_ref, vt_ref, o_ref, lse_ref):
        qi = pl.program_id(1)
        qv = q_ref[...]

        def absorb(kb, carry, masked):
            m, l, acc = carry
            s_t = lax.dot_general(_rows(k_ref, kb, t), qv, NT, preferred_element_type=F32)
            if masked:
                s_t = _att_mask(s_t, qi * t, kb * t)
            m_new = jnp.maximum(m, jnp.max(s_t, axis=0, keepdims=True))
            p_t = jnp.exp(s_t - m_new)
            corr = jnp.exp(m - m_new)
            return (m_new, corr * l + jnp.sum(p_t, axis=0, keepdims=True),
                    corr * acc + lax.dot_general(_cols(vt_ref, kb, t), p_t.astype(BF16), NN,
                                                 preferred_element_type=F32))

        init = (jnp.full((1, t), NEG, F32), jnp.zeros((1, t), F32), jnp.zeros((LANES, t), F32))
        carry = _loop_blocks(0, qi, lambda kb, c: absorb(kb, c, False), init)
        m, l, acc = absorb(qi, carry, True)
        o_ref[...] = acc / l
        lse_ref[0] = m + jnp.log(l)

    return _call_with_step(
        body, hosted, lambda: _attn_flags(nq), (q, k, v_t), name=name, grid=(MLA_HEADS, nq),
        in_specs=[pl.BlockSpec((t, LANES), lambda h, qi: (qi, h)),
                  pl.BlockSpec((S, LANES), lambda h, qi: (0, h)),
                  pl.BlockSpec((LANES, S), lambda h, qi: (h, 0))],
        out_specs=[pl.BlockSpec((LANES, t), lambda h, qi: (h, qi)),
                   pl.BlockSpec((1, 1, t), lambda h, qi: (h, 0, qi))],
        out_shape=[jax.ShapeDtypeStruct((MLA_HEADS * LANES, S), F32), jax.ShapeDtypeStruct((MLA_HEADS, 1, S), F32)],
        sem=("parallel", "arbitrary"))


def _attn_bwd(q, k, v, o_t, do_t, lse, *, name, hosted=None):
    S = q.shape[0]
    t = min(ATTN_TILE, S)
    nq = S // t

    def body(q_ref, k_ref, v_ref, o_ref, do_ref, lse_ref, dq_ref, dk_ref, dv_ref, dkt_scr):
        qi = pl.program_id(1)

        @pl.when(qi == 0)
        def _():
            dkt_scr[...] = jnp.zeros_like(dkt_scr)
            dv_ref[...] = jnp.zeros_like(dv_ref)

        qv = q_ref[...]
        q_t = qv.T
        dov = do_ref[...]
        delta = jnp.sum(dov * o_ref[...], axis=0, keepdims=True)
        dob = dov.astype(BF16)
        lse_v = lse_ref[0]

        def step(kb, acc, masked):
            kt = _rows(k_ref, kb, t)
            s_t = lax.dot_general(kt, qv, NT, preferred_element_type=F32)
            if masked:
                s_t = _att_mask(s_t, qi * t, kb * t)
            p_t = jnp.exp(s_t - lse_v)
            dp_t = lax.dot_general(_rows(v_ref, kb, t), dob, NN, preferred_element_type=F32)
            ds_t = (p_t * (dp_t - delta)).astype(BF16)
            keys = pl.ds(pl.multiple_of(kb * t, t), t)
            dv_ref[:, keys] += lax.dot_general(dob, p_t.astype(BF16), NT, preferred_element_type=F32)
            dkt_scr[:, keys] += lax.dot_general(q_t, ds_t, NT, preferred_element_type=F32)
            return acc + lax.dot_general(kt, ds_t, TN, preferred_element_type=F32)

        acc = _loop_blocks(0, qi, lambda kb, c: step(kb, c, False), jnp.zeros((LANES, t), F32))
        dq_ref[...] = step(qi, acc, True).T

        @pl.when(qi == nq - 1)
        def _():
            dk_ref[...] = dkt_scr[...].T

    tile = pl.BlockSpec((t, LANES), lambda h, qi: (qi, h))
    tile_t = pl.BlockSpec((LANES, t), lambda h, qi: (h, qi))
    stat = pl.BlockSpec((1, 1, t), lambda h, qi: (h, 0, qi))
    seq = pl.BlockSpec((S, LANES), lambda h, qi: (0, h))
    seq_t = pl.BlockSpec((LANES, S), lambda h, qi: (h, 0))
    return _call_with_step(
        body, hosted, lambda: _attn_flags(nq), (q, k, v, o_t, do_t, lse), name=name, grid=(MLA_HEADS, nq),
        in_specs=[tile, seq, seq, tile_t, tile_t, stat],
        out_specs=[tile, seq, seq_t],
        out_shape=[jax.ShapeDtypeStruct((S, MLA_HEADS * LANES), F32), jax.ShapeDtypeStruct((S, MLA_HEADS * LANES), F32),
                   jax.ShapeDtypeStruct((MLA_HEADS * LANES, S), F32)],
        sem=("parallel", "arbitrary"), scratch_shapes=[pltpu.VMEM((LANES, S), F32)])


def _fn_ln(ctx, x, g, b):
    xhat, _ = _ln_stats(x)
    y = xhat * g + b
    return y, y


def _fn_conv_fwd(ctx, u, up, dtr, w8, cb, dtb):
    first = ctx.i == 0
    y = u * w8[3:4] + cb
    for s in (1, 2, 3):
        y = y + _shift_down(u, up, s, first) * w8[3 - s:4 - s]
    act = y * _sigmoid(y)
    v = dtr + dtb
    e = jnp.exp(-jnp.abs(v))
    one_p = 1.0 + e
    log1p = jnp.where(one_p == 1.0, e, jnp.log(one_p) * e / (one_p - 1.0))
    return y, act, jnp.maximum(v, 0.0) + log1p


def _fn_ssd_post(ctx, y, xs, z, dexp, g):
    yg = (y + xs * dexp) * (z * _sigmoid(z))
    outs = []
    for k in range(2):
        v = yg[:, 256 * k:256 * (k + 1)]
        outs.append(v * lax.rsqrt(_mean1(v * v) + RMS_EPS))
    return (jnp.concatenate(outs, axis=1) * g,)


def _fn_ssd_post_bwd(ctx, dyn, y, xs, z, dexp, g):
    yt = y + xs * dexp
    sig = _sigmoid(z)
    sz = z * sig
    yg = yt * sz
    dyh = dyn * g
    yh, dyg = [], []
    for k in range(2):
        sl = slice(256 * k, 256 * (k + 1))
        v = yg[:, sl]
        rs = lax.rsqrt(_mean1(v * v) + RMS_EPS)
        vh = v * rs
        yh.append(vh)
        dyg.append(rs * (dyh[:, sl] - vh * _mean1(dyh[:, sl] * vh)))
    yh = jnp.concatenate(yh, axis=1)
    dyg = jnp.concatenate(dyg, axis=1)
    dyt = dyg * sz
    dz = dyg * yt * (sig * (1.0 + z * (1.0 - sig)))
    return dyt, dz, dyt * dexp, _sum0(dyt * xs), _sum0(dyn * yh)


def _fn_mla_pre(ctx, ql, kvl, gq, gkv):
    return _rms_fwd(ql, gq), _rms_fwd(kvl, gkv)


def _fn_mla_pre_bwd(ctx, ql, kvl, dqn, dkvn_k, dkvn_v, ddtr, gq, gkv):
    dql, dgq = _rms_bwd(ql, dqn, gq)
    dkvl, dgkv = _rms_bwd(kvl, dkvn_k + dkvn_v, gkv)
    return jnp.concatenate([dql, ddtr.astype(F32), dkvl], axis=1), dgq, dgkv


def _fn_rope(ctx, qp, kn, kr, trig):
    ta, tb, tc = _rope_tables(trig)
    kpe = _rope(kr, ta, tb, tc)
    qs, ks = [], []
    for h in range(MLA_HEADS):
        sl = slice(128 * h, 128 * (h + 1))
        qs.append(_rope(qp[:, sl], ta, tb, tc) * MLA_SCALE)
        ks.append(kn[:, sl] + kpe)
    return jnp.concatenate(qs, axis=1), jnp.concatenate(ks, axis=1)


def _fn_rope_bwd(ctx, dq, dk, trig):
    ta, tb, tc = _rope_tables(trig)
    qs = []
    ksum = jnp.zeros_like(ta)
    for h in range(MLA_HEADS):
        sl = slice(128 * h, 128 * (h + 1))
        qs.append(_rope_bwd(dq[:, sl] * MLA_SCALE, ta, tb, tc))
        ksum = ksum + dk[:, sl]
    lane = _lane(ksum.shape)
    dkr = jnp.where((lane >= 64) & (lane < 96), _rope_bwd(ksum, ta, tb, tc), 0.0)
    return jnp.concatenate(qs, axis=1), jnp.concatenate([dkr, jnp.zeros_like(dkr)], axis=1)


def _mem_probs(qh, kh):
    s = _dot(qh, kh, NT) * MEM_SCALE
    p = jnp.exp(s - jnp.max(s, axis=1, keepdims=True))
    return p / jnp.sum(p, axis=1, keepdims=True)


def _fn_mem_fwd(ctx, q, km, vm):
    outs = []
    for h in range(MEM_HEADS):
        sl = slice(256 * h, 256 * (h + 1))
        outs.append(_dot(_mem_probs(q[:, sl], km[:, sl]), vm[:, sl]))
    return (jnp.concatenate(outs, axis=1),)


def _fn_mem_bwd(ctx, q, do, km, vm):
    dqs, dks, dvs = [], [], []
    for h in range(MEM_HEADS):
        sl = slice(256 * h, 256 * (h + 1))
        p = _mem_probs(q[:, sl], km[:, sl])
        dvs.append(_dot(p, do[:, sl], TN))
        dp = _dot(do[:, sl], vm[:, sl], NT)
        ds = p * (dp - jnp.sum(dp * p, axis=1, keepdims=True)) * MEM_SCALE
        dqs.append(_dot(ds, km[:, sl]))
        dks.append(_dot(ds, q[:, sl], TN))
    return jnp.concatenate(dqs, axis=1), jnp.concatenate(dks, axis=1), jnp.concatenate(dvs, axis=1)


def _fn_res_ln(ctx, h, r, g, b):
    xhat, _ = _ln_stats(ALPHA * h + r)
    y = xhat * g + b
    return y, y


def _fn_res_ln_bwd(ctx, h, r, d1, d2, g):
    xhat, rstd = _ln_stats(ALPHA * h + r)
    return _ln_bwd(xhat, rstd, ALPHA * d1 + d2, g)


def _fn_res2_ln(ctx, h, r1, r2, g, b):
    xhat, _ = _ln_stats(ALPHA * h + (r1 + r2))
    return (xhat * g + b,)


def _fn_res2_ln_bwd(ctx, h, r1, r2, d1, d2, g):
    xhat, rstd = _ln_stats(ALPHA * h + (r1 + r2))
    return _ln_bwd(xhat, rstd, ALPHA * d1 + d2, g)


def _fn_in_ln_bwd(ctx, x, d1, d2, g):
    xhat, rstd = _ln_stats(x)
    return _ln_bwd(xhat, rstd, ALPHA * d1 + d2, g)


def _fn_final(ctx, h2, ff, tgt, g, b):
    xhat, rstd = _ln_stats(ALPHA * h2 + ff)
    e = xhat * g + b - tgt
    loss = 0.5 * _sum0(jnp.sum(e * e, axis=1, keepdims=True)) / D_MODEL
    dx, dg, db = _ln_bwd(xhat, rstd, e / D_MODEL, g)
    return dx, dx, dg, db, loss


def _epi_du(da, u):
    return da * 2.0 * jnp.maximum(u.astype(F32), 0.0)


def _relu2(u):
    r = jnp.maximum(u.astype(F32), 0.0)
    return r * r


def _fn_conv_bwd_a(ctx, y, dxs1, dxs2, dbc, dtr, ddt, dtb):
    sig = _sigmoid(y)
    dact = jnp.concatenate([dxs1 + dxs2, dbc], axis=1)
    dyc = dact * (sig * (1.0 + y * (1.0 - sig)))
    ddtr = ddt * _sigmoid(dtr + dtb)
    return dyc, ddtr, _sum0(dyc), _sum0(ddtr)


def _fn_conv_bwd_b(ctx, d, dn, u, up, w8):
    first, last = ctx.i == 0, ctx.i == ctx.n - 1
    du = d * w8[3:4]
    row = lax.broadcasted_iota(jnp.int32, w8.shape, 0)
    dw = jnp.where(row == 3, _sum0(d * u), 0.0)
    for s in (1, 2, 3):
        du = du + _shift_up(d, dn, s, last) * w8[3 - s:4 - s]
        dw = dw + jnp.where(row == 3 - s, _sum0(d * _shift_down(u, up, s, first)), 0.0)
    return du, dw


def _fn_adam(ctx, w, g, m, v):
    m = ADAM_B1 * m + (1.0 - ADAM_B1) * g
    v = ADAM_B2 * v + (1.0 - ADAM_B2) * (g * g)
    m_hat = m / (1.0 - ADAM_B1 ** ADAM_STEP)
    v_hat = v / (1.0 - ADAM_B2 ** ADAM_STEP)
    return -ADAM_LR * (m_hat / (jnp.sqrt(v_hat) + ADAM_EPS) + ADAM_WD * w), m, v


def _z(r, c, dt):
    return jnp.zeros((r, c), dt)


W_IN_SHARD = 554
W_IN_GROUPS = [(0, 512, 1024), (512, 1536, 0), (1536, 1544, 1920), (1544, 1928, 1536), (1928, 2184, 2048),
               (2184, 2216, 2368)]


def _pad_w_in(ws):
    r, dt = ws.shape[1], ws.dtype

    def cols(a, b):
        out = []
        for k in range(N_SHARD):
            lo, hi = max(a, k * W_IN_SHARD), min(b, (k + 1) * W_IN_SHARD)
            if lo < hi:
                out.append(ws[k][:, lo - k * W_IN_SHARD:hi - k * W_IN_SHARD])
        return out

    return jnp.concatenate(cols(512, 1536) + cols(0, 512) + cols(1544, 1928) + cols(1536, 1544) + [_z(r, 120, dt)]
                           + cols(1928, 2184) + [_z(r, 64, dt)] + cols(2184, 2216) + [_z(r, 32, dt), _z(r, 128, dt)],
                           axis=1)


def _unpad_w_in(d):
    shards = []
    for k in range(N_SHARD):
        a, b = k * W_IN_SHARD, (k + 1) * W_IN_SHARD
        parts = []
        for o0, o1, p0 in W_IN_GROUPS:
            lo, hi = max(a, o0), min(b, o1)
            if lo < hi:
                parts.append(d[:, p0 + lo - o0:p0 + hi - o0])
        shards.append(jnp.concatenate(parts, axis=1))
    return jnp.stack(shards)


def _pad_heads(w, width):
    r = w.shape[0]
    w3 = w.reshape(r, MLA_HEADS, width)
    return jnp.pad(w3, ((0, 0), (0, 0), (0, 128 - width))).reshape(r, MLA_HEADS * 128)


def _row(v, width=None):
    v = v.reshape(1, -1).astype(F32)
    if width is not None and v.shape[1] < width:
        v = jnp.pad(v, ((0, 0), (0, width - v.shape[1])))
    return v


BIG = {
    "w_in": (1024, 2216, 1), "w_q_up": (384, 768, 1), "w_kv_up": (256, 1024, 1), "w_mix_out": (1024, 1024, 0),
    "w_mem_q": (1024, 1024, 0), "w_mem_k": (1024, 1024, 0), "w_mem_v": (1024, 1024, 0), "w_mem_o": (1024, 1024, 0),
    "w_up": (1024, 4096, 1), "w_down": (4096, 1024, 0), "conv_w": (4, 1024, 1),
}
BIG_ORDER = list(BIG)
SMALL_ORDER = ["ln_in_g", "ln_in_b", "conv_b", "dt_bias", "a_log", "d_skip", "ssd_norm_g", "q_norm_g", "kv_norm_g",
               "ln1_g", "ln1_b", "ln2_g", "ln2_b", "ln3_g", "ln3_b"]
N_SHARD = 4
N_DEV = 8
PACK_COLS = 1024
PACK_A_ROW = {"w_down": 0, "w_up": 1024, "w_mem_q": 2048, "w_mem_k": 2304, "w_mem_v": 2560, "w_mem_o": 2816,
              "w_mix_out": 3072}
PACK_A_ORDER = list(PACK_A_ROW)
PACK_A_ROWS = 3328
PACK_B_ORDER = ["w_q_up", "w_kv_up", "conv_w"]
PACK_B_ROWS = 160


def _shard_shape(name):
    r, c, ax = BIG[name]
    return (r // N_SHARD, c) if ax == 0 else (r, c // N_SHARD)


def _split_shards(name, full):
    r, c, ax = BIG[name]
    if ax == 0:
        return full.reshape(N_SHARD, -1)
    return full.reshape(r, N_SHARD, c // N_SHARD).transpose(1, 0, 2).reshape(N_SHARD, -1)


def _join_shards(name, parts):
    r, c, ax = BIG[name]
    if ax == 0:
        return parts.reshape(r, c)
    return parts.reshape(N_SHARD, r, c // N_SHARD).transpose(1, 0, 2).reshape(r, c)


def _small_all_reduce(g, step=None):
    r, cdim = g.shape
    si, so = (len(step.inputs), len(step.out_shapes)) if step else (0, 0)

    def body(g_ref, *refs):
        s_ins, out_ref, s_outs = refs[:si], refs[si], refs[si + 1:si + 1 + so]
        buf, send_sems, recv_sems = refs[si + 1 + so:si + 4 + so]
        s_sems = refs[si + 4 + so:]
        if step:
            step.start(s_ins, s_outs, s_sems)
        x, y, c, _ = _place()
        me = 4 * x + 2 * y + c
        buf[me] = g_ref[...]
        copies = []
        for d in range(1, N_DEV):
            to = me ^ d
            cp = pltpu.make_async_remote_copy(src_ref=g_ref, dst_ref=buf.at[me], send_sem=send_sems.at[d - 1],
                                              recv_sem=recv_sems.at[d - 1],
                                              device_id=(to // 4, (to // 2) % 2, to % 2), device_id_type=MESH)
            cp.start()
            copies.append(cp)
        for cp in copies:
            cp.wait()
        acc = buf[0]
        for d in range(1, N_DEV):
            acc = acc + buf[d]
        out_ref[...] = acc
        if step:
            step.finish(s_ins, s_outs, s_sems)

    res = pl.pallas_call(
        body, name="small_all_reduce",
        in_specs=[pl.BlockSpec(memory_space=pltpu.VMEM)] + [HBM] * si,
        out_specs=[pl.BlockSpec(memory_space=pltpu.VMEM)] + [HBM] * so,
        out_shape=[jax.ShapeDtypeStruct((r, cdim), F32)] + (list(step.out_shapes) if step else []),
        scratch_shapes=[pltpu.VMEM((N_DEV, r, cdim), F32), pltpu.SemaphoreType.DMA((N_DEV - 1,)),
                        pltpu.SemaphoreType.DMA((N_DEV - 1,))] + (_sem_scratch(step) if step else []),
    )(g, *(step.inputs if step else []))
    return res if step else res[0]


def _half_tile(h):
    return next(t for t in range(512, 0, -16) if h % t == 0)


def _pair_sum(gp, theirs, name):
    n, R, C = gp.shape
    H = R // 2
    tr = _half_tile(H)
    nb = H // tr

    def body(s_ref, g_ref, t_ref, o_ref):
        o_ref[...] = (g_ref[...] + t_ref[...]).astype(o_ref.dtype)

    def shard(k, s):
        return k + (k >= s[1]).astype(jnp.int32)

    me, c = _mesh_pos()
    return pl.pallas_call(
        body, name=name,
        grid_spec=pltpu.PrefetchScalarGridSpec(
            num_scalar_prefetch=1, grid=(n - 1, nb),
            in_specs=[pl.BlockSpec((1, tr, C), lambda k, i, s: (shard(k, s), s[0] * nb + i, 0)),
                      pl.BlockSpec((1, tr, C), lambda k, i, s: (shard(k, s), i, 0))],
            out_specs=pl.BlockSpec((1, tr, C), lambda k, i, s: (shard(k, s), i, 0))),
        out_shape=jax.ShapeDtypeStruct((n, H, C), BF16), compiler_params=_params(("arbitrary", "arbitrary")),
    )(jnp.stack([c, me]).astype(jnp.int32), gp, theirs)


def _chip_sum(gp, theirs, got, name):
    n, R, C = gp.shape
    H = R // 2
    tr = _half_tile(H)
    nb = H // tr

    def body(s_ref, g_ref, t_ref, r_ref, o_ref):
        acc = g_ref[0] + t_ref[0]
        for j in range(3):
            acc = acc + r_ref[j].astype(F32)
        o_ref[...] = acc

    me, c = _mesh_pos()
    return pl.pallas_call(
        body, name=name,
        grid_spec=pltpu.PrefetchScalarGridSpec(
            num_scalar_prefetch=1, grid=(nb,),
            in_specs=[pl.BlockSpec((1, tr, C), lambda i, s: (s[0], s[1] * nb + i, 0)),
                      pl.BlockSpec((1, tr, C), lambda i, s: (s[0], i, 0)),
                      pl.BlockSpec((3, tr, C), lambda i, s: (0, i, 0))],
            out_specs=pl.BlockSpec((tr, C), lambda i, s: (s[1] * nb + i, 0))),
        out_shape=jax.ShapeDtypeStruct((R, C), F32), compiler_params=_params(("arbitrary",)),
    )(jnp.stack([me, c]).astype(jnp.int32), gp, theirs, got)


def _unpack_group_b(g_c, g_b):
    g_b = g_b.reshape(N_SHARD, -1)
    WB, off = {"w_in": g_c}, 0
    for n in PACK_B_ORDER:
        sr, sc = _shard_shape(n)
        cnt = sr * sc
        if n == "conv_w":
            part = lax.bitcast_convert_type(g_b[:, off:off + 2 * cnt].reshape(N_SHARD, cnt, 2), F32)
            off += 2 * cnt
        else:
            part = g_b[:, off:off + cnt]
            off += cnt
        WB[n] = _join_shards(n, part)
    return WB


def _group_b_grads(dw_in_p, dw_q_p, dw_k_p, dw_v_pt, dconv_w8):
    return {
        "w_in": _unpad_w_in(dw_in_p),
        "w_q_up": dw_q_p.reshape(384, MLA_HEADS, 128)[:, :, :MLA_QK].reshape(384, MLA_HEADS * MLA_QK),
        "w_kv_up": jnp.concatenate([dw_k_p.reshape(MLA_KV_RANK, MLA_HEADS, 128)[:, :, :64],
                                    dw_v_pt.T.reshape(MLA_KV_RANK, MLA_HEADS, 128)[:, :, :64]], axis=2).reshape(
                                        MLA_KV_RANK, MLA_HEADS * 128),
        "conv_w": dconv_w8[0:4],
    }


def _pack_group_b(big_b):
    rows = [_split_shards(n, big_b[n]).reshape(N_SHARD, -1, PACK_COLS) for n in PACK_B_ORDER]
    used = sum(f.shape[1] for f in rows)
    rows.append(jnp.zeros((N_SHARD, PACK_B_ROWS - used, PACK_COLS), F32))
    return big_b["w_in"], jnp.concatenate(rows, axis=1)


def _local_step(x, mem, positions, target, WB, P, *, wp_a=None, g_a=None, wp_b=None):
    S = x.shape[0]
    tr = ROW_TILE
    dist = g_a is None
    g_in, b_in = _row(P["ln_in_g"]), _row(P["ln_in_b"])
    res = _rowwise(_fn_ln, [x], [g_in, b_in], [D_MODEL, (D_MODEL, BF16)], tr=tr, name="ln_in",
                   hosted=_merge_steps([_gather_step(_own_slot(w)) for w in wp_b]) if dist else None)
    h0, h0_b = res[0], res[1]
    if dist:
        WB = _unpack_group_b(res[2], res[3])
    P = {**P, "conv_w": WB["conv_w"]}
    w_in_p = _pad_w_in(WB["w_in"])
    w_q_p = _pad_heads(WB["w_q_up"], MLA_QK)
    w_kv3 = WB["w_kv_up"].reshape(MLA_KV_RANK, MLA_HEADS, 128)
    w_k_p = _pad_heads(w_kv3[:, :, :64].reshape(MLA_KV_RANK, 512), 64)
    w_v_p = _pad_heads(w_kv3[:, :, 64:].reshape(MLA_KV_RANK, 512), 64)
    w_v_pt = w_v_p.T
    conv_w8 = jnp.pad(P["conv_w"].astype(F32), ((0, 4), (0, 0)))
    conv_b = _row(P["conv_b"])
    dt_b = _row(P["dt_bias"], 128)
    a_head = -jnp.exp(P["a_log"].reshape(-1).astype(F32))
    a_row = _row(a_head, 128)
    dexp = jnp.repeat(P["d_skip"].reshape(-1).astype(F32), 64).reshape(1, 512)
    g_ssd, g_q, g_kv = _row(P["ssd_norm_g"]), _row(P["q_norm_g"]), _row(P["kv_norm_g"])
    g1, b1, g2, b2, g3, b3 = (_row(P[k]) for k in ("ln1_g", "ln1_b", "ln2_g", "ln2_b", "ln3_g", "ln3_b"))

    half = MLA_ROPE // 2
    inv_freq = jnp.power(ROPE_THETA, -jnp.arange(half, dtype=F32) / half)
    ang = inv_freq.reshape(half, 1) * positions.reshape(1, S).astype(F32)
    trig = ("cols", jnp.concatenate([jnp.cos(ang), jnp.sin(ang)], axis=0))

    proj = _mm(h0_b, w_in_p, form="nn", tn=IN_W // 2, name="mm_in")
    conv_y, xbc, dt = _rowwise(
        _fn_conv_fwd, [(proj,) + SEG_XBC, ("prev", proj) + SEG_XBC, (proj,) + SEG_DT], [conv_w8, conv_b, dt_b],
        [1024, 1024, 128], tr=tr, name="conv_fwd")
    y_ssd, hs = _ssd_fwd(xbc, dt, a_row, name="ssd_fwd")
    (y_n,) = _rowwise(_fn_ssd_post, [y_ssd, (xbc, 0, 512), (proj,) + SEG_Z], [dexp, g_ssd], [(512, BF16)], tr=tr,
                      name="ssd_post")
    q_n, kv_n = _rowwise(_fn_mla_pre, [(proj,) + SEG_QLAT, (proj,) + SEG_KVLAT], [g_q, g_kv], [384, 256], tr=tr,
                         name="mla_pre")
    qp = _mm(q_n, w_q_p, form="nn", name="mm_q_up")
    kn = _mm(kv_n, w_k_p, form="nn", name="mm_k_up")
    v_nat = _mm(kv_n, w_v_p, form="nn", out_dtype=BF16, name="mm_v_up")
    v_t = _mm(w_v_pt, kv_n, form="nt", out_dtype=BF16, name="mm_v_up_t")
    q_rot, k_full = _rowwise(_fn_rope, [qp, kn, (proj,) + SEG_KR, trig], [],
                             [(1024, BF16), (1024, BF16)], tr=tr, name="rope")
    res = _attn_fwd(q_rot, k_full, v_t, name="attn_fwd", hosted=_gather_step(_own_slot(wp_a)) if dist else None)
    o_t, lse = res[0], res[1]
    if dist:
        g_a = res[2]
    r_mix = PACK_A_ROW["w_mix_out"]
    w_mix_o = jnp.pad(g_a[2:4, r_mix:r_mix + 256].reshape(MLA_HEADS, 64, D_MODEL),
                      ((0, 0), (0, 64), (0, 0))).reshape(MLA_HEADS * 128, D_MODEL)
    mix_o = _mm(o_t, w_mix_o, form="tn", name="mm_mix_o")
    mix_y = _mm(y_n, g_a, form="nn", b_pack="w_mix_out", name="mm_mix_y")
    (h1,) = _rowwise(_fn_res2_ln, [h0, mix_o, mix_y], [g1, b1], [D_MODEL], tr=tr, name="ln1")
    qm = _mm(h1, g_a, form="nn", b_pack="w_mem_q", out_dtype=BF16, name="mm_mem_q")
    km = _mm(mem, g_a, form="nn", b_pack="w_mem_k", out_dtype=BF16, name="mm_mem_k")
    vm = _mm(mem, g_a, form="nn", b_pack="w_mem_v", out_dtype=BF16, name="mm_mem_v")
    (om,) = _rowwise(_fn_mem_fwd, [qm], [km, vm], [(D_MODEL, BF16)], tr=tr, name="mem_fwd")
    xa = _mm(om, g_a, form="nn", b_pack="w_mem_o", name="mm_mem_o")
    h2, h2_b = _rowwise(_fn_res_ln, [h1, xa], [g2, b2], [D_MODEL, (D_MODEL, BF16)], tr=tr, name="ln2")
    u = _mm(h2_b, g_a, form="nn", b_pack="w_up", out_dtype=BF16, name="mm_up")
    ff = _mm(u, g_a, form="nn", a_pro=_relu2, b_pack="w_down", name="mm_down")

    gp = lax.empty((N_SHARD, PACK_A_ROWS, PACK_COLS), F32)
    dt3, dt3_b, dg3, db3, loss = _rowwise(_fn_final, [h2, ff, target], [g3, b3], [D_MODEL, (D_MODEL, BF16)],
                                          [(1, D_MODEL), (1, D_MODEL), (1, 128)], tr=tr, name="ln3_loss")
    du = _mm(dt3_b, g_a, form="nt", b_pack="w_down", epi=(_epi_du, u), out_dtype=BF16, name="mm_down_dx")
    gp = _mm(u, dt3_b, form="tn", a_pro=_relu2, out_pack=("w_down", gp), name="mm_down_dw")
    gp = _mm(h2_b, du, form="tn", out_pack=("w_up", gp), name="mm_up_dw")
    dh2 = _mm(du, g_a, form="nt", b_pack="w_up", name="mm_up_dx")
    dt2, dg2, db2 = _rowwise(_fn_res_ln_bwd, [h1, xa, dt3, dh2], [g2], [D_MODEL], [(1, D_MODEL)] * 2, tr=tr,
                             name="ln2_bwd")
    dom = _mm(dt2, g_a, form="nt", b_pack="w_mem_o", out_dtype=BF16, name="mm_mem_o_dx")
    gp = _mm(om, dt2, form="tn", out_pack=("w_mem_o", gp), name="mm_mem_o_dw")
    dqm, dkm, dvm = _rowwise(_fn_mem_bwd, [qm, dom], [km, vm], [(D_MODEL, BF16)], [(256, D_MODEL)] * 2, tr=tr,
                             name="mem_bwd")
    gp = _mm(h1, dqm, form="tn", out_pack=("w_mem_q", gp), name="mm_mem_q_dw")
    gp = _mm(mem, dkm, form="tn", out_pack=("w_mem_k", gp), name="mm_mem_k_dw")
    gp = _mm(mem, dvm, form="tn", out_pack=("w_mem_v", gp), name="mm_mem_v_dw")
    dh1 = _mm(dqm, g_a, form="nt", b_pack="w_mem_q", name="mm_mem_q_dx")
    dt1, dg1, db1 = _rowwise(_fn_res2_ln_bwd, [h0, mix_o, mix_y, dt2, dh1], [g1], [D_MODEL], [(1, D_MODEL)] * 2,
                             tr=tr, name="ln1_bwd")
    do_t = _mm(w_mix_o, dt1, form="nt", name="mm_mix_o_dx")
    dy_n = _mm(dt1, g_a, form="nt", b_pack="w_mix_out", b_rows=512, name="mm_mix_y_dx")
    dw_mix_o = _mm(o_t, dt1, form="nn", name="mm_mix_o_dw")
    gp = _mm(y_n, dt1, form="tn", out_pack=("w_mix_out", gp), name="mm_mix_y_dw")
    gp = lax.dynamic_update_slice(
        gp, dw_mix_o.reshape(MLA_HEADS, 128, D_MODEL)[:, :64].reshape(2, 256, D_MODEL), (2, r_mix, 0))
    dproj = lax.empty((S, IN_W), BF16)
    dy_ssd, dproj, dxs_skip, ddexp, dg_ssd = _rowwise(
        _fn_ssd_post_bwd, [dy_n, y_ssd, (xbc, 0, 512), (proj,) + SEG_Z], [dexp, g_ssd],
        [512, (512, dproj, SEG_Z[0]), 512], [(1, 512)] * 2, tr=tr, name="ssd_post_bwd")
    res = _ssd_bwd(xbc, dt, a_row, hs, dy_ssd, name="ssd_bwd", hosted=_pair_exchange_step(gp) if dist else None)
    dxs, dbc, ddt, da_head = res[0], res[1], res[2], res[3]
    chip_step = None
    if dist:
        theirs_a = res[4]
        chip_step = _chip_exchange_step(_pair_sum(gp, theirs_a, "pair_sum_a"))
    res = _attn_bwd(q_rot, k_full, v_nat, o_t, do_t, lse, name="attn_bwd", hosted=chip_step)
    dq_rot, dk, dv_t = res[0], res[1], res[2]
    if dist:
        gp = _chip_sum(gp, theirs_a, res[3], "chip_sum_a")
    dqp, dproj = _rowwise(_fn_rope_bwd, [dq_rot, dk, trig], [], [(1024, BF16), (256, dproj, SEG_KR[0])], tr=tr,
                          name="rope_bwd")
    dw_q_p = _mm(q_n, dqp, form="tn", name="mm_q_up_dw")
    dq_n = _mm(dqp, w_q_p, form="nt", name="mm_q_up_dx")
    dw_k_p = _mm(kv_n, dk, form="tn", name="mm_k_up_dw")
    dkv_n1 = _mm(dk, w_k_p, form="nt", name="mm_k_up_dx")
    dw_v_pt = _mm(dv_t, kv_n, form="nn", name="mm_v_up_dw")
    dkv_n2 = _mm(dv_t, w_v_pt, form="tn", name="mm_v_up_dx")
    dyc, ddtr, dconv_b, ddt_b = _rowwise(
        _fn_conv_bwd_a, [conv_y, dxs, dxs_skip, dbc, (proj,) + SEG_DT, ddt], [dt_b], [1024, (128, BF16)],
        [(1, 1024), (1, 128)], tr=tr, name="conv_bwd_a")
    dproj, dg_q, dg_kv = _rowwise(
        _fn_mla_pre_bwd, [(proj,) + SEG_QLAT, (proj,) + SEG_KVLAT, dq_n, dkv_n1, dkv_n2, ddtr], [g_q, g_kv],
        [(SEG_KR[0] - SEG_QLAT[0], dproj, SEG_QLAT[0])], [(1, 384), (1, 256)], tr=tr, name="mla_pre_bwd")
    dproj, dconv_w8 = _rowwise(
        _fn_conv_bwd_b, [dyc, ("next", dyc, 0, 1024), (proj,) + SEG_XBC, ("prev", proj) + SEG_XBC], [conv_w8],
        [(1024, dproj, SEG_XBC[0])], [(8, 1024)], tr=tr, name="conv_bwd_b")
    res = _mm(h0_b, dproj, form="tn", tn=IN_W // 2, name="mm_in_dw", hosted=_pair_fill_step(gp) if dist else None)
    dw_in_p, red_a = (res[0], res[1]) if dist else (res, None)
    big_b = _group_b_grads(dw_in_p, dw_q_p, dw_k_p, dw_v_pt, dconv_w8)
    q_b = None
    if dist:
        gp_c, gp_b = _pack_group_b(big_b)
        theirs_c, theirs_b = _run_step(_merge_steps([_pair_exchange_step(gp_c), _pair_exchange_step(gp_b)]),
                                       "pair_exchange_b")
        dh0, got_c, got_b = _mm(dproj, w_in_p, form="nt", tk=IN_W // 2, name="mm_in_dx", hosted=_merge_steps(
            [_chip_exchange_step(_pair_sum(gp_c, theirs_c, "pair_sum_w_in")),
             _chip_exchange_step(_pair_sum(gp_b, theirs_b, "pair_sum_b"))]))
        q_b = ((gp_c, theirs_c, got_c), (gp_b, theirs_b, got_b))
        gp = red_a
    else:
        dh0 = _mm(dproj, w_in_p, form="nt", tk=IN_W // 2, name="mm_in_dx")
    grad_x, dg_in, db_in = _rowwise(_fn_in_ln_bwd, [x, dt1, dh0], [g_in], [D_MODEL], [(1, D_MODEL)] * 2, tr=tr,
                                    name="ln_in_bwd")

    small = {
        "ln_in_g": dg_in, "ln_in_b": db_in, "conv_b": dconv_b, "dt_bias": ddt_b[:, :8],
        "a_log": da_head[:, :8] * a_head.reshape(1, 8),
        "d_skip": ddexp.reshape(8, 64).sum(axis=1).reshape(1, 8),
        "ssd_norm_g": dg_ssd, "q_norm_g": dg_q, "kv_norm_g": dg_kv,
        "ln1_g": dg1, "ln1_b": db1, "ln2_g": dg2, "ln2_b": db2, "ln3_g": dg3, "ln3_b": db3,
    }
    return loss[0, 0], grad_x, (gp, q_b), big_b, small


def _adam(w, g, m, v, name):
    shape = w.shape
    w2, m2, v2 = (t.reshape(-1, shape[-1]) for t in (w, m, v))
    if isinstance(g, tuple):
        fn = lambda ctx, wv, gv, mv, vv: (*_fn_adam(ctx, wv, gv, mv, vv), gv)
        d, mn, vn, g = _rowwise(fn, [w2, (g[0], 0, shape[-1], g[1]), m2, v2], [], [shape[-1]] * 4, tr=ROW_TILE,
                                name=name)
    else:
        d, mn, vn = _rowwise(_fn_adam, [w2, g.reshape(-1, shape[-1]), m2, v2], [], [shape[-1]] * 3, tr=ROW_TILE,
                             name=name)
    return g.reshape(shape), d.reshape(shape), mn.reshape(shape), vn.reshape(shape)


def _adam_columns(w, g, m, v, name):
    cols = w.shape[0]
    step = cols // 2 if cols % 2 == 0 else cols
    blk = pl.BlockSpec((step,) + w.shape[1:], lambda i: (i, 0, 0))

    def body(w_ref, g_ref, m_ref, v_ref, d_ref, mo_ref, vo_ref):
        d_ref[...], mo_ref[...], vo_ref[...] = _fn_adam(None, w_ref[...], g_ref[...], m_ref[...], v_ref[...])

    return _call_with_step(body, None, None, [w, g, m, v], name=name, grid=(cols // step,), in_specs=[blk] * 4,
                           out_specs=[blk] * 3, out_shape=[jax.ShapeDtypeStruct(w.shape, F32)] * 3, sem=("arbitrary",))


def kernel(x, mem, positions, ln_in_g, ln_in_b, w_in, conv_w, conv_b, dt_bias, a_log, d_skip, ssd_norm_g, q_norm_g, w_q_up, kv_norm_g, w_kv_up, w_mix_out, ln1_g, ln1_b, w_mem_q, w_mem_k, w_mem_v, w_mem_o, ln2_g, ln2_b, w_up, w_down, ln3_g, ln3_b, loss_target, m_ln_in_g, m_ln_in_b, m_w_in, m_conv_w, m_conv_b, m_dt_bias, m_a_log, m_d_skip, m_ssd_norm_g, m_q_norm_g, m_w_q_up, m_kv_norm_g, m_w_kv_up, m_w_mix_out, m_ln1_g, m_ln1_b, m_w_mem_q, m_w_mem_k, m_w_mem_v, m_w_mem_o, m_ln2_g, m_ln2_b, m_w_up, m_w_down, m_ln3_g, m_ln3_b, v_ln_in_g, v_ln_in_b, v_w_in, v_conv_w, v_conv_b, v_dt_bias, v_a_log, v_d_skip, v_ssd_norm_g, v_q_norm_g, v_w_q_up, v_kv_norm_g, v_w_kv_up, v_w_mix_out, v_ln1_g, v_ln1_b, v_w_mem_q, v_w_mem_k, v_w_mem_v, v_w_mem_o, v_ln2_g, v_ln2_b, v_w_up, v_w_down, v_ln3_g, v_ln3_b):
    args = dict(locals())

    wp_a = jnp.concatenate([args[n].reshape(-1, PACK_COLS).astype(BF16) for n in PACK_A_ORDER], axis=0)
    flat = [args[n].reshape(-1).astype(BF16) for n in PACK_B_ORDER[:-1]]
    flat.append(lax.bitcast_convert_type(conv_w.reshape(-1), BF16).reshape(-1))
    used = sum(f.shape[0] for f in flat)
    flat.append(jnp.zeros((PACK_B_ROWS * PACK_COLS - used,), BF16))
    wp_b = jnp.concatenate(flat).reshape(PACK_B_ROWS, PACK_COLS)
    wp_c = w_in[0].astype(BF16)

    P = {n: args[n] for n in SMALL_ORDER}
    loss, grad_x, (red_a, ((gp_c, theirs_c, got_c), (gp_b, theirs_b, got_b))), _, gsmall = _local_step(
        x[0], mem[0], positions[0], loss_target[0], None, P, wp_a=wp_a, wp_b=(wp_c, wp_b))

    gs = _small_all_reduce(
        jnp.concatenate([_row(gsmall[n], PACK_COLS) for n in SMALL_ORDER] + [_row(loss, PACK_COLS)], axis=0))
    loss = gs[len(SMALL_ORDER), 0]
    red_c, red_b = _run_step(_merge_steps([_pair_fill_step(_chip_sum(gp_c, theirs_c, got_c, "chip_sum_w_in")),
                                           _pair_fill_step(_chip_sum(gp_b, theirs_b, got_b, "chip_sum_b"))]),
                             "pair_fill_b")

    grads, deltas, new_m, new_v = {}, {}, {}, {}
    for n in PACK_A_ORDER:
        grads[n], deltas[n], new_m[n], new_v[n] = _adam(args[n], (red_a, PACK_A_ROW[n]), args["m_" + n],
                                                        args["v_" + n], "adam_" + n)
    to_cols = lambda t: jnp.transpose(t, (2, 0, 1))
    from_cols = lambda t: jnp.transpose(t, (1, 2, 0))
    g_t = to_cols(red_c[None])
    grads["w_in"] = from_cols(g_t)
    deltas["w_in"], new_m["w_in"], new_v["w_in"] = map(
        from_cols, _adam_columns(to_cols(w_in), g_t, to_cols(m_w_in), to_cols(v_w_in), "adam_w_in"))
    off = 0
    for n in PACK_B_ORDER:
        sr, sc = _shard_shape(n)
        rows = sr * sc // PACK_COLS
        leaves = [args[n], red_b[off:off + rows].reshape(args[n].shape), args["m_" + n], args["v_" + n]]
        off += rows
        flip = (lambda t: jnp.transpose(t, (0, 2, 1))) if sc % LANES else (lambda t: t)
        grads[n], deltas[n], new_m[n], new_v[n] = map(flip, _adam(*map(flip, leaves), "adam_" + n))
    pack = lambda pre: jnp.concatenate([_row(args[pre + n], PACK_COLS) for n in SMALL_ORDER]
                                       + [jnp.zeros((1, PACK_COLS), F32)], axis=0)
    ds, ms, vs = _rowwise(_fn_adam, [pack(""), gs, pack("m_"), pack("v_")], [], [PACK_COLS] * 3, tr=16,
                          name="adam_small")
    for i, n in enumerate(SMALL_ORDER):
        cnt = args[n].size
        take = lambda t: t[i, :cnt].reshape(args[n].shape)
        grads[n], deltas[n], new_m[n], new_v[n] = take(gs), take(ds), take(ms), take(vs)

    order = ["ln_in_g", "ln_in_b", "w_in", "conv_w", "conv_b", "dt_bias", "a_log", "d_skip", "ssd_norm_g",
             "q_norm_g", "w_q_up", "kv_norm_g", "w_kv_up", "w_mix_out", "ln1_g", "ln1_b", "w_mem_q", "w_mem_k",
             "w_mem_v", "w_mem_o", "ln2_g", "ln2_b", "w_up", "w_down", "ln3_g", "ln3_b"]
    return (loss, grad_x[None], *[grads[n] for n in order], *[deltas[n] for n in order],
            *[new_m[n] for n in order], *[new_v[n] for n in order])
```

```python
import functools

import jax
import jax.numpy as jnp
from jax import lax
from jax.experimental import pallas as pl
from jax.experimental.pallas import tpu as pltpu

F32 = jnp.float32
BF16 = jnp.bfloat16
MESH = pl.DeviceIdType.MESH

D_MODEL = 1024
SSD_CHUNK = 128
SSD_STATE = 128
MLA_HEADS = 8
MLA_ROPE = 32
MLA_QK = 96
MLA_KV_RANK = 256
ROPE_THETA = 10000.0
MEM_HEADS = 4
MEM_HEAD_DIM = 256
LN_EPS = 1e-5
RMS_EPS = 1e-6
ALPHA = 2.0 ** 0.25
ADAM_LR = 0.001
ADAM_B1 = 0.9
ADAM_B2 = 0.999
ADAM_EPS = 1e-08
ADAM_WD = 0.01
ADAM_STEP = 10

LANES = 128
IN_W = 2560
SEG_XBC = (0, 1024)
SEG_Z = (1024, 512)
SEG_QLAT = (1536, 384)
SEG_DT = (1920, 128)
SEG_KVLAT = (2048, 256)
SEG_KR = (2304, 128)
VMEM_LIMIT = 56 * 1024 * 1024
ATTN_TILE = 512
ROW_TILE = 1024
SSD_PER_STEP = 2
NEG = -1e30
MLA_SCALE = MLA_QK ** -0.5
MEM_SCALE = MEM_HEAD_DIM ** -0.5

NN = (((1,), (0,)), ((), ()))
NT = (((1,), (1,)), ((), ()))
TN = (((0,), (0,)), ((), ()))


def _dot(a, b, dims=NN):
    return lax.dot_general(a.astype(BF16), b.astype(BF16), dims, preferred_element_type=F32)


def _dot_exact(a, b):
    return lax.dot_general(a, b, NN, precision=lax.Precision.HIGHEST, preferred_element_type=F32)


def _pick(dim, pref):
    t = min(pref, dim)
    t -= t % LANES
    while t >= LANES:
        if dim % t == 0:
            return t
        t -= LANES
    return dim


def _params(sem):
    return pltpu.CompilerParams(dimension_semantics=sem, vmem_limit_bytes=VMEM_LIMIT)


def _pack_caps(wname):
    r, c, ax = BIG[wname]
    if ax == 0:
        return (r if r <= 1024 else r // N_SHARD), c
    return r, c // N_SHARD


def _pack_block(wname, br, bc):
    r, c, ax = BIG[wname]
    r0 = PACK_A_ROW[wname]
    sr = r // N_SHARD if ax == 0 else r
    if ax == 0 and br > sr:
        assert br % sr == 0 and r0 % sr == 0
        return (br // sr, sr, bc), lambda rb, cb: (rb, r0 // sr, cb)
    assert r0 % br == 0
    if ax == 0:
        per = sr // br
        return (1, br, bc), lambda rb, cb: (rb // per, r0 // br + rb % per, cb)
    per = (c // N_SHARD) // bc
    return (1, br, bc), lambda rb, cb: (cb // per, r0 // br + rb, cb % per)


def _mm(a, b, *, form, name, a_pro=None, epi=None, out_dtype=F32, tm=1024, tn=1024, tk=1024, b_pack=None,
        b_rows=None, out_pack=None, hosted=None):
    b_shape = BIG[b_pack][:2] if b_pack else b.shape
    if b_pack and form == "nt":
        b_shape = (b_rows or b_shape[0], b_shape[1])
    if form == "nn":
        (m, k), (_, n) = a.shape, b_shape
    elif form == "nt":
        (m, k), (n, _) = a.shape, b_shape
    else:
        (k, m), (_, n) = a.shape, b_shape
    if b_pack:
        rcap, ccap = _pack_caps(b_pack)
        tk, tn = (min(tk, rcap), min(tn, ccap)) if form == "nn" else (min(tk, ccap), min(tn, rcap))
    if out_pack:
        rcap, ccap = _pack_caps(out_pack[0])
        tm, tn = min(tm, rcap), min(tn, ccap)
    tm, tn, tk = _pick(m, tm), _pick(n, tn), _pick(k, tk)
    dims = {"nn": NN, "nt": NT, "tn": TN}[form]
    nk = k // tk
    direct = out_dtype == F32 and epi is None
    n_extra = (1 if epi else 0) + (1 if out_pack else 0)

    def body(a_ref, b_ref, *rest):
        o_ref = rest[n_extra]
        acc_ref = o_ref if direct else rest[-1]

        @pl.when(pl.program_id(2) == 0)
        def _():
            acc_ref[...] = jnp.zeros_like(acc_ref)

        av = a_ref[...]
        if a_pro is not None:
            av = a_pro(av)
        bv = b_ref[...]
        acc_ref[...] += _dot(av, bv.reshape(-1, bv.shape[-1]), dims).reshape(acc_ref.shape)
        if not direct:
            @pl.when(pl.program_id(2) == nk - 1)
            def _():
                val = acc_ref[...]
                if epi is not None:
                    val = epi[0](val, rest[0][...])
                o_ref[...] = val.reshape(o_ref.shape).astype(o_ref.dtype)

    if form == "tn":
        a_spec = pl.BlockSpec((tk, tm), lambda i, j, kk: (kk, i))
    else:
        a_spec = pl.BlockSpec((tm, tk), lambda i, j, kk: (i, kk))
    if b_pack:
        shape, idx = _pack_block(b_pack, *((tk, tn) if form == "nn" else (tn, tk)))
        b_spec = pl.BlockSpec(shape, (lambda i, j, kk: idx(kk, j)) if form == "nn" else (lambda i, j, kk: idx(j, kk)))
    elif form == "nt":
        b_spec = pl.BlockSpec((tn, tk), lambda i, j, kk: (j, kk))
    else:
        b_spec = pl.BlockSpec((tk, tn), lambda i, j, kk: (kk, j))
    in_specs, args = [a_spec, b_spec], [a, b]
    out_spec = pl.BlockSpec((tm, tn), lambda i, j, kk: (i, j))
    out_sds, aliases = jax.ShapeDtypeStruct((m, n), out_dtype), {}
    if epi is not None:
        in_specs.append(out_spec)
        args.append(epi[1])
    if out_pack:
        wname, buf = out_pack
        shape, idx = _pack_block(wname, tm, tn)
        out_spec = pl.BlockSpec(shape, lambda i, j, kk: idx(i, j))
        out_sds, aliases = jax.ShapeDtypeStruct(buf.shape, buf.dtype), {len(args): 0}
        in_specs.append(HBM)
        args.append(buf)
    acc_shape = out_spec.block_shape if out_pack else (tm, tn)
    res = _call_with_step(
        body, hosted, None, args, name=name, grid=(m // tm, n // tn, nk), in_specs=in_specs, out_specs=[out_spec],
        out_shape=[out_sds], sem=("parallel", "parallel", "arbitrary"), aliases=aliases,
        scratch_shapes=[] if direct else [pltpu.VMEM(acc_shape, F32)])
    return res[0] if hosted is None else res


class _Ctx:
    def __init__(self, i, n):
        self.i, self.n = i, n


def _rowwise(fn, rows, consts, row_outs, acc_outs=(), *, tr, name, n_rows=None, hosted=None):
    norm = []
    for r in rows:
        kind = "tile"
        if isinstance(r, tuple) and isinstance(r[0], str):
            kind, r = r[0], r[1:]
        if kind == "cols":
            norm.append((kind, r[0], 0, r[0].shape[0], 0))
            continue
        row0 = 0
        if isinstance(r, tuple) and len(r) == 4:
            r, row0 = r[:3], r[3]
        arr, col0, width = r if isinstance(r, tuple) else (r, 0, r.shape[1])
        assert col0 % width == 0
        norm.append((kind, arr, col0 // width, width, row0))
    n_rows = n_rows or next(a.shape[0] for k, a, _, _, _ in norm if k == "tile")
    tr = min(tr, n_rows)
    row_bytes = sum(w * a.dtype.itemsize for k, a, _, w, _ in norm if k in ("tile", "cols"))
    row_bytes += sum((ro[0] * ro[1].dtype.itemsize if len(ro) == 3 else ro[0] * jnp.dtype(ro[1]).itemsize)
                     if isinstance(ro, tuple) else ro * 4 for ro in row_outs)
    while 3 * tr * row_bytes > VMEM_LIMIT and tr % 16 == 0:
        tr //= 2
    while n_rows % tr:
        tr -= 8
    n = n_rows // tr
    arrs, specs = [], []
    for kind, arr, cb, width, row0 in norm:
        if kind == "tile":
            assert row0 % tr == 0
            specs.append(pl.BlockSpec((tr, width), lambda i, cb=cb, rb=row0 // tr: (i + rb, cb)))
        elif kind == "cols":
            specs.append(pl.BlockSpec((width, tr), lambda i: (0, i)))
        elif kind == "prev":
            specs.append(pl.BlockSpec((8, width), lambda i, cb=cb: (jnp.maximum(i * (tr // 8) - 1, 0), cb)))
        else:
            specs.append(pl.BlockSpec((8, width), lambda i, cb=cb: (jnp.minimum((i + 1) * (tr // 8), n_rows // 8 - 1), cb)))
        arrs.append(arr)
    for c in consts:
        specs.append(pl.BlockSpec(c.shape, lambda i, nd=c.ndim: (0,) * nd))
        arrs.append(c)
    n_in, n_ro = len(arrs), len(row_outs)
    out_shape, out_specs, aliases = [], [], {}
    for j, ro in enumerate(row_outs):
        w, dt, col0 = (ro + (None,))[:3] if isinstance(ro, tuple) else (ro, F32, None)
        if col0 is None:
            out_shape.append(jax.ShapeDtypeStruct((n_rows, w), dt))
            out_specs.append(pl.BlockSpec((tr, w), lambda i: (i, 0)))
        else:
            assert col0 % w == 0 and dt.shape[0] == n_rows
            out_shape.append(jax.ShapeDtypeStruct(dt.shape, dt.dtype))
            out_specs.append(pl.BlockSpec((tr, w), lambda i, cb=col0 // w: (i, cb)))
            aliases[len(arrs)] = j
            arrs.append(dt)
            specs.append(HBM)
    n_all = len(arrs)
    out_shape += [jax.ShapeDtypeStruct(s, F32) for s in acc_outs]
    out_specs += [pl.BlockSpec(s, lambda i: (0, 0)) for s in acc_outs]

    def body(*refs):
        i = pl.program_id(0)
        vals = [r[...] for r in refs[:n_in]]
        outs = fn(_Ctx(i, n), *vals)
        if not isinstance(outs, (tuple, list)):
            outs = (outs,)
        o_refs = refs[n_all:]
        for o_ref, o in zip(o_refs[:n_ro], outs[:n_ro]):
            o_ref[...] = o.astype(o_ref.dtype)
        if acc_outs:
            @pl.when(i == 0)
            def _():
                for o_ref in o_refs[n_ro:]:
                    o_ref[...] = jnp.zeros_like(o_ref)

            for o_ref, o in zip(o_refs[n_ro:], outs[n_ro:]):
                o_ref[...] += jnp.broadcast_to(o, o_ref.shape)

    return _call_with_step(body, hosted, None, arrs, name=name, grid=(n,), in_specs=specs, out_specs=out_specs,
                           out_shape=out_shape, sem=("arbitrary",), aliases=aliases)


def _sum0(v):
    return jnp.sum(v, axis=0, keepdims=True)


def _mean1(v):
    return jnp.mean(v, axis=-1, keepdims=True)


def _sigmoid(v):
    return 1.0 / (1.0 + jnp.exp(-v))


def _ln_stats(t):
    xc = t - _mean1(t)
    rstd = lax.rsqrt(_mean1(xc * xc) + LN_EPS)
    return xc * rstd, rstd


def _ln_bwd(xhat, rstd, dy, g):
    dxh = dy * g
    dx = rstd * (dxh - _mean1(dxh) - xhat * _mean1(dxh * xhat))
    return dx, _sum0(dy * xhat), _sum0(dy)


def _rms_fwd(v, g):
    return v * lax.rsqrt(_mean1(v * v) + RMS_EPS) * g


def _rms_bwd(v, dy, g):
    rs = lax.rsqrt(_mean1(v * v) + RMS_EPS)
    vh = v * rs
    dyg = dy * g
    return rs * (dyg - vh * _mean1(dyg * vh)), _sum0(dy * vh)


def _lane(shape):
    return lax.broadcasted_iota(jnp.int32, shape, len(shape) - 1)


def _shift_down(u, halo, s, is_first):
    tr = u.shape[0]
    rolled = pltpu.roll(u, s, 0)
    hr = jnp.where(is_first, 0.0, pltpu.roll(halo, s, 0))
    row = lax.broadcasted_iota(jnp.int32, hr.shape, 0)
    top = jnp.where(row < s, hr, rolled[0:8])
    if tr == 8:
        return top
    return jnp.concatenate([top, rolled[8:]], axis=0)


def _shift_up(d, halo, s, is_last):
    tr = d.shape[0]
    rolled = pltpu.roll(d, tr - s, 0)
    hr = jnp.where(is_last, 0.0, pltpu.roll(halo, 8 - s, 0))
    row = lax.broadcasted_iota(jnp.int32, hr.shape, 0)
    bot = jnp.where(row >= 8 - s, hr, rolled[tr - 8:])
    if tr == 8:
        return bot
    return jnp.concatenate([rolled[:tr - 8], bot], axis=0)


def _rope_tables(trig):
    t = jnp.concatenate([trig] * (LANES // trig.shape[0]), axis=0).T
    lane = _lane(t.shape)
    first, second = (lane >= 64) & (lane < 80), (lane >= 80) & (lane < 96)
    ta = jnp.where(lane < 64, 1.0, jnp.where(first, pltpu.roll(t, 64, 1), jnp.where(second, pltpu.roll(t, 80, 1), 0.0)))
    return ta, jnp.where(second, pltpu.roll(t, 64, 1), 0.0), jnp.where(first, -pltpu.roll(t, 48, 1), 0.0)


def _rope(v, ta, tb, tc):
    return v * ta + pltpu.roll(v, 16, 1) * tb + pltpu.roll(v, LANES - 16, 1) * tc


def _rope_bwd(d, ta, tb, tc):
    return d * ta + pltpu.roll(d * tb, LANES - 16, 1) + pltpu.roll(d * tc, 16, 1)


def _ssd_common(dtv, a_row):
    L = SSD_CHUNK
    a = dtv * a_row
    r = lax.broadcasted_iota(jnp.int32, (L, L), 0)
    c = lax.broadcasted_iota(jnp.int32, (L, L), 1)
    tril = r >= c
    cs = _dot_exact(tril.astype(F32), a)
    cs_t = cs.T
    cs_last = cs[L - 1:L, :]
    return dict(a=a, tril=tril, cs=cs, cs_t=cs_t, ecs=jnp.exp(cs), dte=jnp.exp(cs_last - cs),
                elast=jnp.exp(cs_last))


def _pair_sel(v, h0, lo):
    return jnp.where(lo, v[:, h0:h0 + 1], v[:, h0 + 1:h0 + 2])


def _ssd_pair(cm, h0, cb, xp, dtv, bmat, cmat, hp, lo):
    x = xp * _pair_sel(dtv, h0, lo)
    lam0 = jnp.exp(jnp.where(cm["tril"], cm["cs"][:, h0:h0 + 1] - cm["cs_t"][h0:h0 + 1, :], NEG))
    lam1 = jnp.exp(jnp.where(cm["tril"], cm["cs"][:, h0 + 1:h0 + 2] - cm["cs_t"][h0 + 1:h0 + 2, :], NEG))
    m0, m1 = cb * lam0, cb * lam1
    ydiag = jnp.where(lo, _dot(m0, x), _dot(m1, x))
    ecs_p = _pair_sel(cm["ecs"], h0, lo)
    dte_p = _pair_sel(cm["dte"], h0, lo)
    yoff = _dot(cmat, hp, NT) * ecs_p
    xd = x * dte_p
    st = _dot(xd, bmat, TN)
    rlo = lax.broadcasted_iota(jnp.int32, (LANES, SSD_STATE), 0) < 64
    decay = jnp.where(rlo, cm["elast"][:, h0:h0 + 1], cm["elast"][:, h0 + 1:h0 + 2])
    h_next = hp * decay + st
    return dict(x=x, lam0=lam0, lam1=lam1, m0=m0, m1=m1, y=ydiag + yoff, yoff=yoff, ecs_p=ecs_p, dte_p=dte_p,
                xd=xd, decay=decay, h_next=h_next)


def _ssd_fwd(xbc, dt, a_row, *, name):
    S = xbc.shape[0]
    L = SSD_CHUNK
    nc = S // L
    per = SSD_PER_STEP if nc % SSD_PER_STEP == 0 else 1
    G = per * L

    def body(xs_ref, bm_ref, cm_ref, dt_ref, a_ref, y_ref, hs_ref, h_scr):
        @pl.when(pl.program_id(0) == 0)
        def _():
            h_scr[...] = jnp.zeros_like(h_scr)

        lo = _lane((L, LANES)) < 64
        for sub in range(per):
            rows = slice(sub * L, (sub + 1) * L)
            dtv = dt_ref[rows, :]
            cm = _ssd_common(dtv, a_ref[...])
            ys = []
            for g in range(2):
                bmat = bm_ref[rows, g * 128:(g + 1) * 128]
                cmat = cm_ref[rows, g * 128:(g + 1) * 128]
                cb = _dot(cmat, bmat, NT)
                for pr in range(2):
                    p4 = 2 * g + pr
                    hp = h_scr[p4]
                    hs_ref[sub, p4 * 128:(p4 + 1) * 128, :] = hp
                    t = _ssd_pair(cm, 2 * p4, cb, xs_ref[rows, p4 * 128:(p4 + 1) * 128], dtv, bmat, cmat, hp, lo)
                    ys.append(t["y"])
                    h_scr[p4] = t["h_next"]
            y_ref[rows, :] = jnp.concatenate(ys, axis=1)

    return pl.pallas_call(
        body, name=name, grid=(nc // per,),
        in_specs=[pl.BlockSpec((G, 512), lambda c: (c, 0)), pl.BlockSpec((G, 256), lambda c: (c, 2)),
                  pl.BlockSpec((G, 256), lambda c: (c, 3)), pl.BlockSpec((G, 128), lambda c: (c, 0)),
                  pl.BlockSpec((1, 128), lambda c: (0, 0))],
        out_specs=[pl.BlockSpec((G, 512), lambda c: (c, 0)), pl.BlockSpec((per, 512, 128), lambda c: (c, 0, 0))],
        out_shape=[jax.ShapeDtypeStruct((S, 512), F32), jax.ShapeDtypeStruct((nc, 512, 128), F32)],
        scratch_shapes=[pltpu.VMEM((4, 128, 128), F32)],
        compiler_params=_params(("arbitrary",)),
    )(xbc, xbc, xbc, dt, a_row)


def _ssd_bwd(xbc, dt, a_row, hs, dy, *, name, hosted=None):
    S = xbc.shape[0]
    L = SSD_CHUNK
    nc = S // L
    per = SSD_PER_STEP if nc % SSD_PER_STEP == 0 else 1
    G = per * L

    def body(xs_ref, bm_ref, cm_ref, dt_ref, a_ref, hs_ref, dy_ref, dxs_ref, dbc_ref, ddt_ref, da_ref, g_scr):
        @pl.when(pl.program_id(0) == 0)
        def _():
            g_scr[...] = jnp.zeros_like(g_scr)
            da_ref[...] = jnp.zeros_like(da_ref)

        for sub in reversed(range(per)):
            rows = pl.ds(sub * L, L)
            chunk(xs_ref.at[rows, :], bm_ref.at[rows, :], cm_ref.at[rows, :], dt_ref.at[rows, :], a_ref,
                  hs_ref.at[pl.ds(sub, 1)], dy_ref.at[rows, :], dxs_ref.at[rows, :], dbc_ref.at[rows, :],
                  ddt_ref.at[rows, :], da_ref, g_scr)

    def chunk(xs_ref, bm_ref, cm_ref, dt_ref, a_ref, hs_ref, dy_ref, dxs_ref, dbc_ref, ddt_ref, da_ref, g_scr):
        dtv = dt_ref[...]
        a_row_v = a_ref[...]
        cm = _ssd_common(dtv, a_row_v)
        lo = _lane((L, LANES)) < 64
        lane_row = _lane((1, LANES))
        ri = lax.broadcasted_iota(jnp.int32, (L, L), 0)
        ci = lax.broadcasted_iota(jnp.int32, (L, L), 1)
        triu = (ri <= ci).astype(F32)
        stril = ri > ci

        def halves(v, mask):
            return (jnp.sum(jnp.where(mask, v, 0.0), axis=1, keepdims=True),
                    jnp.sum(jnp.where(mask, 0.0, v), axis=1, keepdims=True))

        i_all = jnp.zeros((L, LANES), F32)
        yo_all = jnp.zeros((L, LANES), F32)
        w_all = jnp.zeros((L, LANES), F32)
        ddt_x = jnp.zeros((L, LANES), F32)
        e_row = jnp.zeros((1, LANES), F32)
        rlo = lax.broadcasted_iota(jnp.int32, (LANES, SSD_STATE), 0) < 64
        dxs, dbs, dcs = [], [], []
        for g in range(2):
            bmat = bm_ref[:, g * 128:(g + 1) * 128]
            cmat = cm_ref[:, g * 128:(g + 1) * 128]
            cb = _dot(cmat, bmat, NT)
            dcb = jnp.zeros((L, L), F32)
            db = jnp.zeros((L, SSD_STATE), F32)
            dc = jnp.zeros((L, SSD_STATE), F32)
            for pr in range(2):
                p4 = 2 * g + pr
                h0 = 2 * p4
                hp = hs_ref[0, p4 * 128:(p4 + 1) * 128, :]
                xp = xs_ref[:, p4 * 128:(p4 + 1) * 128]
                t = _ssd_pair(cm, h0, cb, xp, dtv, bmat, cmat, hp, lo)
                gst = g_scr[p4]
                dyp = dy_ref[:, p4 * 128:(p4 + 1) * 128]
                dy0 = jnp.where(lo, dyp, 0.0)
                dy1 = dyp - dy0
                bg = _dot(bmat, gst, NT)
                dx = _dot(t["m0"], dy0, TN) + _dot(t["m1"], dy1, TN) + bg * t["dte_p"]
                dm0, dm1 = _dot(dy0, t["x"], NT), _dot(dy1, t["x"], NT)
                dcb = dcb + dm0 * t["lam0"] + dm1 * t["lam1"]
                dye = dyp * t["ecs_p"]
                dc = dc + _dot(dye, hp)
                db = db + _dot(t["xd"], gst)
                i0 = jnp.sum(jnp.where(stril, _dot(triu, dm0 * t["m0"]), 0.0), axis=1, keepdims=True)
                i1 = jnp.sum(jnp.where(stril, _dot(triu, dm1 * t["m1"]), 0.0), axis=1, keepdims=True)
                yo0, yo1 = halves(dyp * t["yoff"], lo)
                w0, w1 = halves(t["xd"] * bg, lo)
                gh = gst * (hp * t["decay"])
                e0 = _sum0(jnp.sum(jnp.where(rlo, gh, 0.0), axis=1, keepdims=True))
                e1 = _sum0(jnp.sum(jnp.where(rlo, 0.0, gh), axis=1, keepdims=True))
                x0, x1 = halves(dx * xp, lo)
                oh0 = (lane_row == h0).astype(F32)
                oh1 = (lane_row == h0 + 1).astype(F32)
                i_all = i_all + i0 * oh0 + i1 * oh1
                yo_all = yo_all + yo0 * oh0 + yo1 * oh1
                w_all = w_all + w0 * oh0 + w1 * oh1
                e_row = e_row + e0 * oh0 + e1 * oh1
                ddt_x = ddt_x + x0 * oh0 + x1 * oh1
                dxs.append(dx * _pair_sel(dtv, h0, lo))
                g_scr[p4] = gst * t["decay"] + _dot(dye, cmat, TN)
            dbs.append(db + _dot(dcb, cmat, TN))
            dcs.append(dc + _dot(dcb, bmat))
        da = i_all + _dot_exact(triu, yo_all) + _dot_exact(stril.astype(F32), w_all) + e_row
        ddt_ref[...] = da * a_row_v + ddt_x
        da_ref[...] += _sum0(da * dtv)
        dxs_ref[...] = jnp.concatenate(dxs, axis=1)
        dbc_ref[...] = jnp.concatenate(dbs + dcs, axis=1)

    rev = lambda c: nc // per - 1 - c
    return _call_with_step(
        body, hosted, None, (xbc, xbc, xbc, dt, a_row, hs, dy), name=name, grid=(nc // per,),
        in_specs=[pl.BlockSpec((G, 512), lambda c: (rev(c), 0)), pl.BlockSpec((G, 256), lambda c: (rev(c), 2)),
                  pl.BlockSpec((G, 256), lambda c: (rev(c), 3)), pl.BlockSpec((G, 128), lambda c: (rev(c), 0)),
                  pl.BlockSpec((1, 128), lambda c: (0, 0)), pl.BlockSpec((per, 512, 128), lambda c: (rev(c), 0, 0)),
                  pl.BlockSpec((G, 512), lambda c: (rev(c), 0))],
        out_specs=[pl.BlockSpec((G, 512), lambda c: (rev(c), 0)), pl.BlockSpec((G, 512), lambda c: (rev(c), 0)),
                   pl.BlockSpec((G, 128), lambda c: (rev(c), 0)), pl.BlockSpec((1, 128), lambda c: (0, 0))],
        out_shape=[jax.ShapeDtypeStruct((S, 512), F32), jax.ShapeDtypeStruct((S, 512), F32),
                   jax.ShapeDtypeStruct((S, 128), F32), jax.ShapeDtypeStruct((1, 128), F32)],
        sem=("arbitrary",), scratch_shapes=[pltpu.VMEM((4, 128, 128), F32)])


HBM = pl.BlockSpec(memory_space=pl.ANY)


class _Step:
    def __init__(self, inputs, out_shapes, n_sems, start, finish, mid=None):
        self.inputs, self.out_shapes, self.n_sems = inputs, out_shapes, n_sems
        self.start, self.finish, self.mid = start, finish, mid
        self.alias = []


class _Shifted:
    def __init__(self, ref, off):
        self.ref, self.off = ref, off

    @property
    def at(self):
        return self

    def __getitem__(self, j):
        return self.ref.at[self.off + j]


def _merge_steps(steps):
    offs = [sum(s.n_sems for s in steps[:i]) for i in range(len(steps) + 1)]
    i_offs = [sum(len(s.inputs) for s in steps[:i]) for i in range(len(steps))]
    o_offs = [sum(len(s.out_shapes) for s in steps[:i]) for i in range(len(steps))]

    def phase(which):
        def run(ins, outs, sems):
            for s, off, i0, o0 in zip(steps, offs, i_offs, o_offs):
                fn = getattr(s, which)
                if fn is not None:
                    fn(ins[i0:i0 + len(s.inputs)], outs[o0:o0 + len(s.out_shapes)],
                       [_Shifted(sems[0], off), _Shifted(sems[1], off)])
        return run

    merged = _Step([a for s in steps for a in s.inputs], [o for s in steps for o in s.out_shapes], offs[-1],
                   phase("start"), phase("finish"), phase("mid") if any(s.mid for s in steps) else None)
    merged.alias = [(i0 + a, o0 + b) for s, i0, o0 in zip(steps, i_offs, o_offs) for a, b in s.alias]
    return merged


def _place():
    x, y, c = lax.axis_index("x"), lax.axis_index("y"), lax.axis_index("c")
    chips = [(1 - x, y), (x, 1 - y), (1 - x, 1 - y)]
    return x, y, c, chips


def _mesh_pos():
    return 2 * lax.axis_index("x") + lax.axis_index("y"), lax.axis_index("c")


def _chunks(rows, tile):
    return next(n for n in (4, 3, 2, 1) if rows % (n * tile) == 0)


def _remote(src, dst, sems, j, to):
    return pltpu.make_async_remote_copy(src_ref=src, dst_ref=dst, send_sem=sems[0].at[j], recv_sem=sems[1].at[j],
                                        device_id=to, device_id_type=MESH)


def _own_slot(wp):
    return lax.dynamic_update_slice(lax.empty((N_SHARD,) + wp.shape, wp.dtype), wp[None], (_mesh_pos()[0], 0, 0))


def _gather_step(buf):
    _, R, C = buf.shape
    H = R // 2
    nq = _chunks(H, 16)
    CH = H // nq

    def copies(ins, outs, sems):
        x, y, c, chips = _place()
        sib, me = (x, y, 1 - c), 2 * x + y
        w_ref, out_ref = ins[0], outs[0]

        def piece(k, hc, q):
            return out_ref.at[k, pl.ds(hc * H + q * CH, CH), :]

        sends, landed, fwds, fwd_landed = [], [], [], []
        for q in range(nq):
            for j, (px, py) in enumerate(chips):
                k = 2 * px + py
                sends.append(_remote(w_ref.at[me, pl.ds(c * H + q * CH, CH), :], piece(me, c, q), sems, j * nq + q,
                                     (px, py, c)))
                landed.append(_remote(piece(k, c, q), piece(k, c, q), sems, j * nq + q, (px, py, c)))
                fwds.append(_remote(piece(k, c, q), piece(k, c, q), sems, (3 + j) * nq + q, sib))
                fwd_landed.append(_remote(piece(k, 1 - c, q), piece(k, 1 - c, q), sems, (3 + j) * nq + q, sib))
        return sends, landed, fwds, fwd_landed

    def start(ins, outs, sems):
        for cp in copies(ins, outs, sems)[0]:
            cp.start()

    def mid(ins, outs, sems):
        _, landed, fwds, _ = copies(ins, outs, sems)
        for arrived, onward in zip(landed, fwds):
            arrived.wait_recv()
            onward.start()

    def finish(ins, outs, sems):
        sends, _, fwds, fwd_landed = copies(ins, outs, sems)
        for cp in fwd_landed:
            cp.wait_recv()
        for cp in sends + fwds:
            cp.wait_send()

    step = _Step([buf], [jax.ShapeDtypeStruct(buf.shape, buf.dtype)], 6 * nq, start, finish, mid)
    step.alias = [(0, 0)]
    return step


def _pair_exchange_step(gp):
    n, R, C = gp.shape
    H = R // 2
    nq = _chunks(H, 8)
    CH = H // nq

    def copies(ins, outs, sems):
        x, y, c, _ = _place()
        return [_remote(ins[0].at[k, pl.ds((1 - c) * H + q * CH, CH), :], outs[0].at[k, pl.ds(q * CH, CH), :], sems,
                        k * nq + q, (x, y, 1 - c)) for k in range(n) for q in range(nq)]

    def start(ins, outs, sems):
        for cp in copies(ins, outs, sems):
            cp.start()

    def finish(ins, outs, sems):
        for cp in copies(ins, outs, sems):
            cp.wait()

    return _Step([gp], [jax.ShapeDtypeStruct((n, H, C), gp.dtype)], n * nq, start, finish)


def _chip_exchange_step(pb):
    n, H, C = pb.shape
    nq = _chunks(H, 16)
    CH = H // nq

    def copies(ins, outs, sems):
        x, y, c, chips = _place()
        return [_remote(ins[0].at[2 * px + py, pl.ds(q * CH, CH), :], outs[0].at[j, pl.ds(q * CH, CH), :], sems,
                        j * nq + q, (px, py, c)) for q in range(nq) for j, (px, py) in enumerate(chips)]

    def start(ins, outs, sems):
        for cp in copies(ins, outs, sems):
            cp.start()

    def finish(ins, outs, sems):
        for cp in copies(ins, outs, sems):
            cp.wait()

    return _Step([pb], [jax.ShapeDtypeStruct((3, H, C), pb.dtype)], 3 * nq, start, finish)


def _pair_fill_step(red):
    R, C = red.shape
    H = R // 2
    nq = _chunks(H, 8)
    CH = H // nq

    def copies(ins, outs, sems):
        x, y, c, _ = _place()
        return [_remote(ins[0].at[pl.ds(c * H + j * CH, CH), :], outs[0].at[pl.ds(c * H + j * CH, CH), :], sems, j,
                        (x, y, 1 - c)) for j in range(nq)]

    def start(ins, outs, sems):
        for cp in copies(ins, outs, sems):
            cp.start()

    def finish(ins, outs, sems):
        for cp in copies(ins, outs, sems):
            cp.wait()

    step = _Step([red], [jax.ShapeDtypeStruct((R, C), red.dtype)], nq, start, finish)
    step.alias = [(0, 0)]
    return step


def _sem_scratch(step):
    return [pltpu.SemaphoreType.DMA((step.n_sems,)), pltpu.SemaphoreType.DMA((step.n_sems,))]


def _run_step(step, name):
    ni, no = len(step.inputs), len(step.out_shapes)

    def body(*refs):
        ins, outs, sems = refs[:ni], refs[ni:ni + no], refs[ni + no:]
        step.start(ins, outs, sems)
        if step.mid is not None:
            step.mid(ins, outs, sems)
        step.finish(ins, outs, sems)

    return pl.pallas_call(body, name=name, in_specs=[HBM] * ni, out_specs=[HBM] * no, out_shape=step.out_shapes,
                          input_output_aliases=dict(step.alias),
                          scratch_shapes=_sem_scratch(step))(*step.inputs)


def _grid_flags(grid):
    ids = [pl.program_id(d) for d in range(len(grid))]
    first = functools.reduce(lambda a, b: a & b, [i == 0 for i in ids])
    last = functools.reduce(lambda a, b: a & b, [i == n - 1 for i, n in zip(ids, grid)])
    return first, last, last


def _call_with_step(core, step, flags, args, *, name, grid, in_specs, out_specs, out_shape, sem, scratch_shapes=(),
                    aliases=None):
    aliases = aliases or {}
    if step is None:
        return pl.pallas_call(core, name=name, grid=grid, in_specs=in_specs, out_specs=out_specs,
                              out_shape=out_shape, scratch_shapes=list(scratch_shapes),
                              input_output_aliases=aliases, compiler_params=_params(sem))(*args)
    n_in, n_out, n_scr = len(in_specs), len(out_specs), len(scratch_shapes)
    si, so = len(step.inputs), len(step.out_shapes)
    flags = flags or (lambda: _grid_flags(grid))
    aliases = {**aliases, **{n_in + a: n_out + b for a, b in step.alias}}

    def body(*refs):
        ins, s_ins = refs[:n_in], refs[n_in:n_in + si]
        outs = refs[n_in + si:n_in + si + n_out]
        s_outs = refs[n_in + si + n_out:n_in + si + n_out + so]
        scr = refs[n_in + si + n_out + so:n_in + si + n_out + so + n_scr]
        sems = refs[n_in + si + n_out + so + n_scr:]
        first, middle, last = flags()

        @pl.when(first)
        def _():
            step.start(s_ins, s_outs, sems)

        if step.mid is not None:
            @pl.when(middle)
            def _():
                step.mid(s_ins, s_outs, sems)

        core(*ins, *outs, *scr)

        @pl.when(last)
        def _():
            step.finish(s_ins, s_outs, sems)

    return pl.pallas_call(
        body, name=name, grid=grid, in_specs=list(in_specs) + [HBM] * si, out_specs=list(out_specs) + [HBM] * so,
        out_shape=list(out_shape) + list(step.out_shapes), scratch_shapes=list(scratch_shapes) + _sem_scratch(step),
        input_output_aliases=aliases, compiler_params=_params(("arbitrary",) * len(grid)))(*args, *step.inputs)


def _attn_flags(nq):
    h, qi = pl.program_id(0), pl.program_id(1)
    return ((h == 0) & (qi == 0), (h == MLA_HEADS - 1) & (qi == 0), (h == MLA_HEADS - 1) & (qi == nq - 1))


def _att_mask(s_t, q0, k0):
    krow = k0 + lax.broadcasted_iota(jnp.int32, s_t.shape, 0)
    qcol = q0 + lax.broadcasted_iota(jnp.int32, s_t.shape, 1)
    return jnp.where(krow <= qcol, s_t, NEG)


def _loop_blocks(lo, hi, step, carry):
    n = hi - lo

    def four(i, c):
        kb = lo + 4 * i
        return step(kb + 3, step(kb + 2, step(kb + 1, step(kb, c))))

    carry = lax.fori_loop(0, n // 4, four, carry)
    base = lo + 4 * (n // 4)
    carry = lax.cond(n % 4 >= 2, lambda c: step(base + 1, step(base, c)), lambda c: c, carry)
    return lax.cond(n % 2 == 1, lambda c: step(hi - 1, c), lambda c: c, carry)


def _rows(ref, blk, t):
    return ref[pl.ds(pl.multiple_of(blk * t, t), t), :]


def _cols(ref, blk, t):
    return ref[:, pl.ds(pl.multiple_of(blk * t, t), t)]


def _attn_fwd(q, k, v_t, *, name, hosted=None):
    S = q.shape[0]
    t = min(ATTN_TILE, S)
    nq = S // t

    def body(q_ref, k_ref, vt_ref, o_ref, lse_ref):
        qi = pl.program_id(1)
        qv = q_ref[...]

        def absorb(kb, carry, masked):
            m, l, acc = carry
            s_t = lax.dot_general(_rows(k_ref, kb, t), qv, NT, preferred_element_type=F32)
            if masked:
                s_t = _att_mask(s_t, qi * t, kb * t)
            m_new = jnp.maximum(m, jnp.max(s_t, axis=0, keepdims=True))
            p_t = jnp.exp(s_t - m_new)
            corr = jnp.exp(m - m_new)
            return (m_new, corr * l + jnp.sum(p_t, axis=0, keepdims=True),
                    corr * acc + lax.dot_general(_cols(vt_ref, kb, t), p_t.astype(BF16), NN,
                                                 preferred_element_type=F32))

        init = (jnp.full((1, t), NEG, F32), jnp.zeros((1, t), F32), jnp.zeros((LANES, t), F32))
        carry = _loop_blocks(0, qi, lambda kb, c: absorb(kb, c, False), init)
        m, l, acc = absorb(qi, carry, True)
        o_ref[...] = acc / l
        lse_ref[0] = m + jnp.log(l)

    return _call_with_step(
        body, hosted, lambda: _attn_flags(nq), (q, k, v_t), name=name, grid=(MLA_HEADS, nq),
        in_specs=[pl.BlockSpec((t, LANES), lambda h, qi: (qi, h)),
                  pl.BlockSpec((S, LANES), lambda h, qi: (0, h)),
                  pl.BlockSpec((LANES, S), lambda h, qi: (h, 0))],
        out_specs=[pl.BlockSpec((LANES, t), lambda h, qi: (h, qi)),
                   pl.BlockSpec((1, 1, t), lambda h, qi: (h, 0, qi))],
        out_shape=[jax.ShapeDtypeStruct((MLA_HEADS * LANES, S), F32), jax.ShapeDtypeStruct((MLA_HEADS, 1, S), F32)],
        sem=("parallel", "arbitrary"))


def _attn_bwd(q, k, v, o_t, do_t, lse, *, name, hosted=None):
    S = q.shape[0]
    t = min(ATTN_TILE, S)
    nq = S // t

    def body(q_ref, k_ref, v_ref, o_ref, do_ref, lse_ref, dq_ref, dk_ref, dv_ref, dkt_scr):
        qi = pl.program_id(1)

        @pl.when(qi == 0)
        def _():
            dkt_scr[...] = jnp.zeros_like(dkt_scr)
            dv_ref[...] = jnp.zeros_like(dv_ref)

        qv = q_ref[...]
        q_t = qv.T
        dov = do_ref[...]
        delta = jnp.sum(dov * o_ref[...], axis=0, keepdims=True)
        dob = dov.astype(BF16)
        lse_v = lse_ref[0]

        def step(kb, acc, masked):
            kt = _rows(k_ref, kb, t)
            s_t = lax.dot_general(kt, qv, NT, preferred_element_type=F32)
            if masked:
                s_t = _att_mask(s_t, qi * t, kb * t)
            p_t = jnp.exp(s_t - lse_v)
            dp_t = lax.dot_general(_rows(v_ref, kb, t), dob, NN, preferred_element_type=F32)
            ds_t = (p_t * (dp_t - delta)).astype(BF16)
            keys = pl.ds(pl.multiple_of(kb * t, t), t)
            dv_ref[:, keys] += lax.dot_general(dob, p_t.astype(BF16), NT, preferred_element_type=F32)
            dkt_scr[:, keys] += lax.dot_general(q_t, ds_t, NT, preferred_element_type=F32)
            return acc + lax.dot_general(kt, ds_t, TN, preferred_element_type=F32)

        acc = _loop_blocks(0, qi, lambda kb, c: step(kb, c, False), jnp.zeros((LANES, t), F32))
        dq_ref[...] = step(qi, acc, True).T

        @pl.when(qi == nq - 1)
        def _():
            dk_ref[...] = dkt_scr[...].T

    tile = pl.BlockSpec((t, LANES), lambda h, qi: (qi, h))
    tile_t = pl.BlockSpec((LANES, t), lambda h, qi: (h, qi))
    stat = pl.BlockSpec((1, 1, t), lambda h, qi: (h, 0, qi))
    seq = pl.BlockSpec((S, LANES), lambda h, qi: (0, h))
    seq_t = pl.BlockSpec((LANES, S), lambda h, qi: (h, 0))
    return _call_with_step(
        body, hosted, lambda: _attn_flags(nq), (q, k, v, o_t, do_t, lse), name=name, grid=(MLA_HEADS, nq),
        in_specs=[tile, seq, seq, tile_t, tile_t, stat],
        out_specs=[tile, seq, seq_t],
        out_shape=[jax.ShapeDtypeStruct((S, MLA_HEADS * LANES), F32), jax.ShapeDtypeStruct((S, MLA_HEADS * LANES), F32),
                   jax.ShapeDtypeStruct((MLA_HEADS * LANES, S), F32)],
        sem=("parallel", "arbitrary"), scratch_shapes=[pltpu.VMEM((LANES, S), F32)])


def _fn_ln(ctx, x, g, b):
    xhat, _ = _ln_stats(x)
    y = xhat * g + b
    return y, y


def _fn_conv_fwd(ctx, u, up, dtr, w8, cb, dtb):
    first = ctx.i == 0
    y = u * w8[3:4] + cb
    for s in (1, 2, 3):
        y = y + _shift_down(u, up, s, first) * w8[3 - s:4 - s]
    act = y * _sigmoid(y)
    v = dtr + dtb
    e = jnp.exp(-jnp.abs(v))
    one_p = 1.0 + e
    log1p = jnp.where(one_p == 1.0, e, jnp.log(one_p) * e / (one_p - 1.0))
    return y, act, jnp.maximum(v, 0.0) + log1p


def _fn_ssd_post(ctx, y, xs, z, dexp, g):
    yg = (y + xs * dexp) * (z * _sigmoid(z))
    outs = []
    for k in range(2):
        v = yg[:, 256 * k:256 * (k + 1)]
        outs.append(v * lax.rsqrt(_mean1(v * v) + RMS_EPS))
    return (jnp.concatenate(outs, axis=1) * g,)


def _fn_ssd_post_bwd(ctx, dyn, y, xs, z, dexp, g):
    yt = y + xs * dexp
    sig = _sigmoid(z)
    sz = z * sig
    yg = yt * sz
    dyh = dyn * g
    yh, dyg = [], []
    for k in range(2):
        sl = slice(256 * k, 256 * (k + 1))
        v = yg[:, sl]
        rs = lax.rsqrt(_mean1(v * v) + RMS_EPS)
        vh = v * rs
        yh.append(vh)
        dyg.append(rs * (dyh[:, sl] - vh * _mean1(dyh[:, sl] * vh)))
    yh = jnp.concatenate(yh, axis=1)
    dyg = jnp.concatenate(dyg, axis=1)
    dyt = dyg * sz
    dz = dyg * yt * (sig * (1.0 + z * (1.0 - sig)))
    return dyt, dz, dyt * dexp, _sum0(dyt * xs), _sum0(dyn * yh)


def _fn_mla_pre(ctx, ql, kvl, gq, gkv):
    return _rms_fwd(ql, gq), _rms_fwd(kvl, gkv)


def _fn_mla_pre_bwd(ctx, ql, kvl, dqn, dkvn_k, dkvn_v, ddtr, gq, gkv):
    dql, dgq = _rms_bwd(ql, dqn, gq)
    dkvl, dgkv = _rms_bwd(kvl, dkvn_k + dkvn_v, gkv)
    return jnp.concatenate([dql, ddtr.astype(F32), dkvl], axis=1), dgq, dgkv


def _fn_rope(ctx, qp, kn, kr, trig):
    ta, tb, tc = _rope_tables(trig)
    kpe = _rope(kr, ta, tb, tc)
    qs, ks = [], []
    for h in range(MLA_HEADS):
        sl = slice(128 * h, 128 * (h + 1))
        qs.append(_rope(qp[:, sl], ta, tb, tc) * MLA_SCALE)
        ks.append(kn[:, sl] + kpe)
    return jnp.concatenate(qs, axis=1), jnp.concatenate(ks, axis=1)


def _fn_rope_bwd(ctx, dq, dk, trig):
    ta, tb, tc = _rope_tables(trig)
    qs = []
    ksum = jnp.zeros_like(ta)
    for h in range(MLA_HEADS):
        sl = slice(128 * h, 128 * (h + 1))
        qs.append(_rope_bwd(dq[:, sl] * MLA_SCALE, ta, tb, tc))
        ksum = ksum + dk[:, sl]
    lane = _lane(ksum.shape)
    dkr = jnp.where((lane >= 64) & (lane < 96), _rope_bwd(ksum, ta, tb, tc), 0.0)
    return jnp.concatenate(qs, axis=1), jnp.concatenate([dkr, jnp.zeros_like(dkr)], axis=1)


def _mem_probs(qh, kh):
    s = _dot(qh, kh, NT) * MEM_SCALE
    p = jnp.exp(s - jnp.max(s, axis=1, keepdims=True))
    return p / jnp.sum(p, axis=1, keepdims=True)


def _fn_mem_fwd(ctx, q, km, vm):
    outs = []
    for h in range(MEM_HEADS):
        sl = slice(256 * h, 256 * (h + 1))
        outs.append(_dot(_mem_probs(q[:, sl], km[:, sl]), vm[:, sl]))
    return (jnp.concatenate(outs, axis=1),)


def _fn_mem_bwd(ctx, q, do, km, vm):
    dqs, dks, dvs = [], [], []
    for h in range(MEM_HEADS):
        sl = slice(256 * h, 256 * (h + 1))
        p = _mem_probs(q[:, sl], km[:, sl])
        dvs.append(_dot(p, do[:, sl], TN))
        dp = _dot(do[:, sl], vm[:, sl], NT)
        ds = p * (dp - jnp.sum(dp * p, axis=1, keepdims=True)) * MEM_SCALE
        dqs.append(_dot(ds, km[:, sl]))
        dks.append(_dot(ds, q[:, sl], TN))
    return jnp.concatenate(dqs, axis=1), jnp.concatenate(dks, axis=1), jnp.concatenate(dvs, axis=1)


def _fn_res_ln(ctx, h, r, g, b):
    xhat, _ = _ln_stats(ALPHA * h + r)
    y = xhat * g + b
    return y, y


def _fn_res_ln_bwd(ctx, h, r, d1, d2, g):
    xhat, rstd = _ln_stats(ALPHA * h + r)
    dx, dg, db = _ln_bwd(xhat, rstd, ALPHA * d1 + d2, g)
    return dx, dx, dg, db


def _fn_res2_ln(ctx, h, r1, r2, g, b):
    xhat, _ = _ln_stats(ALPHA * h + (r1 + r2))
    y = xhat * g + b
    return y, y


def _fn_res2_ln_bwd(ctx, h, r1, r2, d1, d2, g):
    xhat, rstd = _ln_stats(ALPHA * h + (r1 + r2))
    dx, dg, db = _ln_bwd(xhat, rstd, ALPHA * d1 + d2, g)
    return dx, dx, dg, db


def _fn_in_ln_bwd(ctx, x, d1, d2, g):
    xhat, rstd = _ln_stats(x)
    return _ln_bwd(xhat, rstd, ALPHA * d1 + d2, g)


def _fn_final(ctx, h2, ff, tgt, g, b):
    xhat, rstd = _ln_stats(ALPHA * h2 + ff)
    e = xhat * g + b - tgt
    loss = 0.5 * _sum0(jnp.sum(e * e, axis=1, keepdims=True)) / D_MODEL
    dx, dg, db = _ln_bwd(xhat, rstd, e / D_MODEL, g)
    return dx, dx, dg, db, loss


def _epi_du(da, u):
    return da * 2.0 * jnp.maximum(u.astype(F32), 0.0)


def _relu2(u):
    r = jnp.maximum(u.astype(F32), 0.0)
    return r * r


def _fn_conv_bwd_a(ctx, y, dxs1, dxs2, dbc, dtr, ddt, dtb):
    sig = _sigmoid(y)
    dact = jnp.concatenate([dxs1 + dxs2, dbc], axis=1)
    dyc = dact * (sig * (1.0 + y * (1.0 - sig)))
    ddtr = ddt * _sigmoid(dtr + dtb)
    return dyc, ddtr, _sum0(dyc), _sum0(ddtr)


def _fn_conv_bwd_b(ctx, d, dn, u, up, w8):
    first, last = ctx.i == 0, ctx.i == ctx.n - 1
    du = d * w8[3:4]
    row = lax.broadcasted_iota(jnp.int32, w8.shape, 0)
    dw = jnp.where(row == 3, _sum0(d * u), 0.0)
    for s in (1, 2, 3):
        du = du + _shift_up(d, dn, s, last) * w8[3 - s:4 - s]
        dw = dw + jnp.where(row == 3 - s, _sum0(d * _shift_down(u, up, s, first)), 0.0)
    return du, dw


def _fn_adam(ctx, w, g, m, v):
    m = ADAM_B1 * m + (1.0 - ADAM_B1) * g
    v = ADAM_B2 * v + (1.0 - ADAM_B2) * (g * g)
    m_hat = m / (1.0 - ADAM_B1 ** ADAM_STEP)
    v_hat = v / (1.0 - ADAM_B2 ** ADAM_STEP)
    return -ADAM_LR * (m_hat / (jnp.sqrt(v_hat) + ADAM_EPS) + ADAM_WD * w), m, v


def _z(r, c, dt):
    return jnp.zeros((r, c), dt)


W_IN_SHARD = 554
W_IN_GROUPS = [(0, 512, 1024), (512, 1536, 0), (1536, 1544, 1920), (1544, 1928, 1536), (1928, 2184, 2048),
               (2184, 2216, 2368)]


def _pad_w_in(ws):
    r, dt = ws.shape[1], ws.dtype

    def cols(a, b):
        out = []
        for k in range(N_SHARD):
            lo, hi = max(a, k * W_IN_SHARD), min(b, (k + 1) * W_IN_SHARD)
            if lo < hi:
                out.append(ws[k][:, lo - k * W_IN_SHARD:hi - k * W_IN_SHARD])
        return out

    return jnp.concatenate(cols(512, 1536) + cols(0, 512) + cols(1544, 1928) + cols(1536, 1544) + [_z(r, 120, dt)]
                           + cols(1928, 2184) + [_z(r, 64, dt)] + cols(2184, 2216) + [_z(r, 32, dt), _z(r, 128, dt)],
                           axis=1)


def _unpad_w_in(d):
    shards = []
    for k in range(N_SHARD):
        a, b = k * W_IN_SHARD, (k + 1) * W_IN_SHARD
        parts = []
        for o0, o1, p0 in W_IN_GROUPS:
            lo, hi = max(a, o0), min(b, o1)
            if lo < hi:
                parts.append(d[:, p0 + lo - o0:p0 + hi - o0])
        shards.append(jnp.concatenate(parts, axis=1))
    return jnp.stack(shards)


def _pad_heads(w, width):
    r = w.shape[0]
    w3 = w.reshape(r, MLA_HEADS, width)
    return jnp.pad(w3, ((0, 0), (0, 0), (0, 128 - width))).reshape(r, MLA_HEADS * 128)


def _row(v, width=None):
    v = v.reshape(1, -1).astype(F32)
    if width is not None and v.shape[1] < width:
        v = jnp.pad(v, ((0, 0), (0, width - v.shape[1])))
    return v


BIG = {
    "w_in": (1024, 2216, 1), "w_q_up": (384, 768, 1), "w_kv_up": (256, 1024, 1), "w_mix_out": (1024, 1024, 0),
    "w_mem_q": (1024, 1024, 0), "w_mem_k": (1024, 1024, 0), "w_mem_v": (1024, 1024, 0), "w_mem_o": (1024, 1024, 0),
    "w_up": (1024, 4096, 1), "w_down": (4096, 1024, 0), "conv_w": (4, 1024, 1),
}
BIG_ORDER = list(BIG)
SMALL_ORDER = ["ln_in_g", "ln_in_b", "conv_b", "dt_bias", "a_log", "d_skip", "ssd_norm_g", "q_norm_g", "kv_norm_g",
               "ln1_g", "ln1_b", "ln2_g", "ln2_b", "ln3_g", "ln3_b"]
N_SHARD = 4
N_DEV = 8
PACK_COLS = 1024
PACK_A_ROW = {"w_down": 0, "w_up": 1024, "w_mem_q": 2048, "w_mem_k": 2304, "w_mem_v": 2560, "w_mem_o": 2816,
              "w_mix_out": 3072}
PACK_A_ORDER = list(PACK_A_ROW)
PACK_A_ROWS = 3328
PACK_B_ORDER = ["w_q_up", "w_kv_up", "conv_w"]
PACK_B_ROWS = 160


def _shard_shape(name):
    r, c, ax = BIG[name]
    return (r // N_SHARD, c) if ax == 0 else (r, c // N_SHARD)


def _split_shards(name, full):
    r, c, ax = BIG[name]
    if ax == 0:
        return full.reshape(N_SHARD, -1)
    return full.reshape(r, N_SHARD, c // N_SHARD).transpose(1, 0, 2).reshape(N_SHARD, -1)


def _join_shards(name, parts):
    r, c, ax = BIG[name]
    if ax == 0:
        return parts.reshape(r, c)
    return parts.reshape(N_SHARD, r, c // N_SHARD).transpose(1, 0, 2).reshape(r, c)


def _small_all_reduce(g, step=None):
    r, cdim = g.shape
    si, so = (len(step.inputs), len(step.out_shapes)) if step else (0, 0)

    def body(g_ref, *refs):
        s_ins, out_ref, s_outs = refs[:si], refs[si], refs[si + 1:si + 1 + so]
        buf, send_sems, recv_sems = refs[si + 1 + so:si + 4 + so]
        s_sems = refs[si + 4 + so:]
        if step:
            step.start(s_ins, s_outs, s_sems)
        x, y, c, _ = _place()
        me = 4 * x + 2 * y + c
        buf[me] = g_ref[...]
        copies = []
        for d in range(1, N_DEV):
            to = me ^ d
            cp = pltpu.make_async_remote_copy(src_ref=g_ref, dst_ref=buf.at[me], send_sem=send_sems.at[d - 1],
                                              recv_sem=recv_sems.at[d - 1],
                                              device_id=(to // 4, (to // 2) % 2, to % 2), device_id_type=MESH)
            cp.start()
            copies.append(cp)
        for cp in copies:
            cp.wait()
        acc = buf[0]
        for d in range(1, N_DEV):
            acc = acc + buf[d]
        out_ref[...] = acc
        if step:
            step.finish(s_ins, s_outs, s_sems)

    res = pl.pallas_call(
        body, name="small_all_reduce",
        in_specs=[pl.BlockSpec(memory_space=pltpu.VMEM)] + [HBM] * si,
        out_specs=[pl.BlockSpec(memory_space=pltpu.VMEM)] + [HBM] * so,
        out_shape=[jax.ShapeDtypeStruct((r, cdim), F32)] + (list(step.out_shapes) if step else []),
        scratch_shapes=[pltpu.VMEM((N_DEV, r, cdim), F32), pltpu.SemaphoreType.DMA((N_DEV - 1,)),
                        pltpu.SemaphoreType.DMA((N_DEV - 1,))] + (_sem_scratch(step) if step else []),
    )(g, *(step.inputs if step else []))
    return res if step else res[0]


def _half_tile(h):
    return next(t for t in range(512, 0, -16) if h % t == 0)


def _pair_sum(gp, theirs, name):
    n, R, C = gp.shape
    H = R // 2
    tr = _half_tile(H)
    nb = H // tr

    def body(s_ref, g_ref, t_ref, o_ref):
        o_ref[...] = (g_ref[...] + t_ref[...]).astype(o_ref.dtype)

    def shard(k, s):
        return k + (k >= s[1]).astype(jnp.int32)

    me, c = _mesh_pos()
    return pl.pallas_call(
        body, name=name,
        grid_spec=pltpu.PrefetchScalarGridSpec(
            num_scalar_prefetch=1, grid=(n - 1, nb),
            in_specs=[pl.BlockSpec((1, tr, C), lambda k, i, s: (shard(k, s), s[0] * nb + i, 0)),
                      pl.BlockSpec((1, tr, C), lambda k, i, s: (shard(k, s), i, 0))],
            out_specs=pl.BlockSpec((1, tr, C), lambda k, i, s: (shard(k, s), i, 0))),
        out_shape=jax.ShapeDtypeStruct((n, H, C), BF16), compiler_params=_params(("arbitrary", "arbitrary")),
    )(jnp.stack([c, me]).astype(jnp.int32), gp, theirs)


def _chip_sum(gp, theirs, got, name):
    n, R, C = gp.shape
    H = R // 2
    tr = _half_tile(H)
    nb = H // tr

    def body(s_ref, g_ref, t_ref, r_ref, o_ref):
        acc = g_ref[0] + t_ref[0]
        for j in range(3):
            acc = acc + r_ref[j].astype(F32)
        o_ref[...] = acc

    me, c = _mesh_pos()
    return pl.pallas_call(
        body, name=name,
        grid_spec=pltpu.PrefetchScalarGridSpec(
            num_scalar_prefetch=1, grid=(nb,),
            in_specs=[pl.BlockSpec((1, tr, C), lambda i, s: (s[0], s[1] * nb + i, 0)),
                      pl.BlockSpec((1, tr, C), lambda i, s: (s[0], i, 0)),
                      pl.BlockSpec((3, tr, C), lambda i, s: (0, i, 0))],
            out_specs=pl.BlockSpec((tr, C), lambda i, s: (s[1] * nb + i, 0))),
        out_shape=jax.ShapeDtypeStruct((R, C), F32), compiler_params=_params(("arbitrary",)),
    )(jnp.stack([me, c]).astype(jnp.int32), gp, theirs, got)


def _unpack_group_b(g_c, g_b):
    g_b = g_b.reshape(N_SHARD, -1)
    WB, off = {"w_in": g_c}, 0
    for n in PACK_B_ORDER:
        sr, sc = _shard_shape(n)
        cnt = sr * sc
        if n == "conv_w":
            part = lax.bitcast_convert_type(g_b[:, off:off + 2 * cnt].reshape(N_SHARD, cnt, 2), F32)
            off += 2 * cnt
        else:
            part = g_b[:, off:off + cnt]
            off += cnt
        WB[n] = _join_shards(n, part)
    return WB


def _group_b_grads(dw_in_p, dw_q_p, dw_k_p, dw_v_pt, dconv_w8):
    return {
        "w_in": _unpad_w_in(dw_in_p),
        "w_q_up": dw_q_p.reshape(384, MLA_HEADS, 128)[:, :, :MLA_QK].reshape(384, MLA_HEADS * MLA_QK),
        "w_kv_up": jnp.concatenate([dw_k_p.reshape(MLA_KV_RANK, MLA_HEADS, 128)[:, :, :64],
                                    dw_v_pt.T.reshape(MLA_KV_RANK, MLA_HEADS, 128)[:, :, :64]], axis=2).reshape(
                                        MLA_KV_RANK, MLA_HEADS * 128),
        "conv_w": dconv_w8[0:4],
    }


def _pack_group_b(big_b):
    rows = [_split_shards(n, big_b[n]).reshape(N_SHARD, -1, PACK_COLS) for n in PACK_B_ORDER]
    used = sum(f.shape[1] for f in rows)
    rows.append(jnp.zeros((N_SHARD, PACK_B_ROWS - used, PACK_COLS), F32))
    return big_b["w_in"], jnp.concatenate(rows, axis=1)


def _local_step(x, mem, positions, target, WB, P, *, wp_a=None, g_a=None, wp_b=None):
    S = x.shape[0]
    tr = ROW_TILE
    dist = g_a is None
    g_in, b_in = _row(P["ln_in_g"]), _row(P["ln_in_b"])
    res = _rowwise(_fn_ln, [x], [g_in, b_in], [D_MODEL, (D_MODEL, BF16)], tr=tr, name="ln_in",
                   hosted=_merge_steps([_gather_step(_own_slot(w)) for w in wp_b]) if dist else None)
    h0, h0_b = res[0], res[1]
    if dist:
        WB = _unpack_group_b(res[2], res[3])
    P = {**P, "conv_w": WB["conv_w"]}
    w_in_p = _pad_w_in(WB["w_in"])
    w_q_p = _pad_heads(WB["w_q_up"], MLA_QK)
    w_kv3 = WB["w_kv_up"].reshape(MLA_KV_RANK, MLA_HEADS, 128)
    w_k_p = _pad_heads(w_kv3[:, :, :64].reshape(MLA_KV_RANK, 512), 64)
    w_v_p = _pad_heads(w_kv3[:, :, 64:].reshape(MLA_KV_RANK, 512), 64)
    w_v_pt = w_v_p.T
    conv_w8 = jnp.pad(P["conv_w"].astype(F32), ((0, 4), (0, 0)))
    conv_b = _row(P["conv_b"])
    dt_b = _row(P["dt_bias"], 128)
    a_head = -jnp.exp(P["a_log"].reshape(-1).astype(F32))
    a_row = _row(a_head, 128)
    dexp = jnp.repeat(P["d_skip"].reshape(-1).astype(F32), 64).reshape(1, 512)
    g_ssd, g_q, g_kv = _row(P["ssd_norm_g"]), _row(P["q_norm_g"]), _row(P["kv_norm_g"])
    g1, b1, g2, b2, g3, b3 = (_row(P[k]) for k in ("ln1_g", "ln1_b", "ln2_g", "ln2_b", "ln3_g", "ln3_b"))

    half = MLA_ROPE // 2
    inv_freq = jnp.power(ROPE_THETA, -jnp.arange(half, dtype=F32) / half)
    ang = inv_freq.reshape(half, 1) * positions.reshape(1, S).astype(F32)
    trig = ("cols", jnp.concatenate([jnp.cos(ang), jnp.sin(ang)], axis=0))

    proj = _mm(h0_b, w_in_p, form="nn", tn=IN_W // 2, name="mm_in")
    conv_y, xbc, dt = _rowwise(
        _fn_conv_fwd, [(proj,) + SEG_XBC, ("prev", proj) + SEG_XBC, (proj,) + SEG_DT], [conv_w8, conv_b, dt_b],
        [1024, 1024, 128], tr=tr, name="conv_fwd")
    y_ssd, hs = _ssd_fwd(xbc, dt, a_row, name="ssd_fwd")
    (y_n,) = _rowwise(_fn_ssd_post, [y_ssd, (xbc, 0, 512), (proj,) + SEG_Z], [dexp, g_ssd], [(512, BF16)], tr=tr,
                      name="ssd_post")
    q_n, kv_n = _rowwise(_fn_mla_pre, [(proj,) + SEG_QLAT, (proj,) + SEG_KVLAT], [g_q, g_kv],
                         [(384, BF16), (256, BF16)], tr=tr, name="mla_pre")
    qp = _mm(q_n, w_q_p, form="nn", name="mm_q_up")
    kn = _mm(kv_n, w_k_p, form="nn", name="mm_k_up")
    v_nat = _mm(kv_n, w_v_p, form="nn", out_dtype=BF16, name="mm_v_up")
    v_t = _mm(w_v_pt, kv_n, form="nt", out_dtype=BF16, name="mm_v_up_t")
    q_rot, k_full = _rowwise(_fn_rope, [qp, kn, (proj,) + SEG_KR, trig], [],
                             [(1024, BF16), (1024, BF16)], tr=tr, name="rope")
    res = _attn_fwd(q_rot, k_full, v_t, name="attn_fwd", hosted=_gather_step(_own_slot(wp_a)) if dist else None)
    o_t, lse = res[0], res[1]
    if dist:
        g_a = res[2]
    r_mix = PACK_A_ROW["w_mix_out"]
    w_mix_o = jnp.pad(g_a[2:4, r_mix:r_mix + 256].reshape(MLA_HEADS, 64, D_MODEL),
                      ((0, 0), (0, 64), (0, 0))).reshape(MLA_HEADS * 128, D_MODEL)
    mix_o = _mm(o_t, w_mix_o, form="tn", name="mm_mix_o")
    mix_y = _mm(y_n, g_a, form="nn", b_pack="w_mix_out", name="mm_mix_y")
    h1, h1_b = _rowwise(_fn_res2_ln, [h0, mix_o, mix_y], [g1, b1], [D_MODEL, (D_MODEL, BF16)], tr=tr, name="ln1")
    qm = _mm(h1_b, g_a, form="nn", b_pack="w_mem_q", out_dtype=BF16, name="mm_mem_q")
    km = _mm(mem, g_a, form="nn", b_pack="w_mem_k", out_dtype=BF16, name="mm_mem_k")
    vm = _mm(mem, g_a, form="nn", b_pack="w_mem_v", out_dtype=BF16, name="mm_mem_v")
    (om,) = _rowwise(_fn_mem_fwd, [qm], [km, vm], [(D_MODEL, BF16)], tr=tr, name="mem_fwd")
    xa = _mm(om, g_a, form="nn", b_pack="w_mem_o", name="mm_mem_o")
    h2, h2_b = _rowwise(_fn_res_ln, [h1, xa], [g2, b2], [D_MODEL, (D_MODEL, BF16)], tr=tr, name="ln2")
    u = _mm(h2_b, g_a, form="nn", b_pack="w_up", out_dtype=BF16, name="mm_up")
    ff = _mm(u, g_a, form="nn", a_pro=_relu2, b_pack="w_down", name="mm_down")

    gp = lax.empty((N_SHARD, PACK_A_ROWS, PACK_COLS), F32)
    dt3, dt3_b, dg3, db3, loss = _rowwise(_fn_final, [h2, ff, target], [g3, b3], [D_MODEL, (D_MODEL, BF16)],
                                          [(1, D_MODEL), (1, D_MODEL), (1, 128)], tr=tr, name="ln3_loss")
    du = _mm(dt3_b, g_a, form="nt", b_pack="w_down", epi=(_epi_du, u), out_dtype=BF16, name="mm_down_dx")
    gp = _mm(u, dt3_b, form="tn", a_pro=_relu2, out_pack=("w_down", gp), name="mm_down_dw")
    gp = _mm(h2_b, du, form="tn", out_pack=("w_up", gp), name="mm_up_dw")
    dh2 = _mm(du, g_a, form="nt", b_pack="w_up", name="mm_up_dx")
    dt2, dt2_b, dg2, db2 = _rowwise(_fn_res_ln_bwd, [h1, xa, dt3, dh2], [g2], [D_MODEL, (D_MODEL, BF16)],
                                    [(1, D_MODEL)] * 2, tr=tr, name="ln2_bwd")
    dom = _mm(dt2_b, g_a, form="nt", b_pack="w_mem_o", out_dtype=BF16, name="mm_mem_o_dx")
    gp = _mm(om, dt2_b, form="tn", out_pack=("w_mem_o", gp), name="mm_mem_o_dw")
    dqm, dkm, dvm = _rowwise(_fn_mem_bwd, [qm, dom], [km, vm], [(D_MODEL, BF16)], [(256, D_MODEL)] * 2, tr=tr,
                             name="mem_bwd")
    gp = _mm(h1_b, dqm, form="tn", out_pack=("w_mem_q", gp), name="mm_mem_q_dw")
    gp = _mm(mem, dkm, form="tn", out_pack=("w_mem_k", gp), name="mm_mem_k_dw")
    gp = _mm(mem, dvm, form="tn", out_pack=("w_mem_v", gp), name="mm_mem_v_dw")
    dh1 = _mm(dqm, g_a, form="nt", b_pack="w_mem_q", name="mm_mem_q_dx")
    dt1, dt1_b, dg1, db1 = _rowwise(_fn_res2_ln_bwd, [h0, mix_o, mix_y, dt2, dh1], [g1], [D_MODEL, (D_MODEL, BF16)],
                                    [(1, D_MODEL)] * 2, tr=tr, name="ln1_bwd")
    do_t = _mm(w_mix_o, dt1_b, form="nt", name="mm_mix_o_dx")
    dy_n = _mm(dt1_b, g_a, form="nt", b_pack="w_mix_out", b_rows=512, name="mm_mix_y_dx")
    dw_mix_o = _mm(o_t, dt1_b, form="nn", name="mm_mix_o_dw")
    gp = _mm(y_n, dt1_b, form="tn", out_pack=("w_mix_out", gp), name="mm_mix_y_dw")
    gp = lax.dynamic_update_slice(
        gp, dw_mix_o.reshape(MLA_HEADS, 128, D_MODEL)[:, :64].reshape(2, 256, D_MODEL), (2, r_mix, 0))
    dproj = lax.empty((S, IN_W), BF16)
    dy_ssd, dproj, dxs_skip, ddexp, dg_ssd = _rowwise(
        _fn_ssd_post_bwd, [dy_n, y_ssd, (xbc, 0, 512), (proj,) + SEG_Z], [dexp, g_ssd],
        [512, (512, dproj, SEG_Z[0]), 512], [(1, 512)] * 2, tr=tr, name="ssd_post_bwd")
    res = _ssd_bwd(xbc, dt, a_row, hs, dy_ssd, name="ssd_bwd", hosted=_pair_exchange_step(gp) if dist else None)
    dxs, dbc, ddt, da_head = res[0], res[1], res[2], res[3]
    chip_step = None
    if dist:
        theirs_a = res[4]
        chip_step = _chip_exchange_step(_pair_sum(gp, theirs_a, "pair_sum_a"))
    res = _attn_bwd(q_rot, k_full, v_nat, o_t, do_t, lse, name="attn_bwd", hosted=chip_step)
    dq_rot, dk, dv_t = res[0], res[1], res[2]
    if dist:
        gp = _chip_sum(gp, theirs_a, res[3], "chip_sum_a")
    dqp, dproj = _rowwise(_fn_rope_bwd, [dq_rot, dk, trig], [], [(1024, BF16), (256, dproj, SEG_KR[0])], tr=tr,
                          name="rope_bwd")
    dw_q_p = _mm(q_n, dqp, form="tn", name="mm_q_up_dw")
    dq_n = _mm(dqp, w_q_p, form="nt", name="mm_q_up_dx")
    dw_k_p = _mm(kv_n, dk, form="tn", name="mm_k_up_dw")
    dkv_n1 = _mm(dk, w_k_p, form="nt", name="mm_k_up_dx")
    dw_v_pt = _mm(dv_t, kv_n, form="nn", name="mm_v_up_dw")
    dkv_n2 = _mm(dv_t, w_v_pt, form="tn", name="mm_v_up_dx")
    dyc, ddtr, dconv_b, ddt_b = _rowwise(
        _fn_conv_bwd_a, [conv_y, dxs, dxs_skip, dbc, (proj,) + SEG_DT, ddt], [dt_b], [1024, (128, BF16)],
        [(1, 1024), (1, 128)], tr=tr, name="conv_bwd_a")
    dproj, dg_q, dg_kv = _rowwise(
        _fn_mla_pre_bwd, [(proj,) + SEG_QLAT, (proj,) + SEG_KVLAT, dq_n, dkv_n1, dkv_n2, ddtr], [g_q, g_kv],
        [(SEG_KR[0] - SEG_QLAT[0], dproj, SEG_QLAT[0])], [(1, 384), (1, 256)], tr=tr, name="mla_pre_bwd")
    dproj, dconv_w8 = _rowwise(
        _fn_conv_bwd_b, [dyc, ("next", dyc, 0, 1024), (proj,) + SEG_XBC, ("prev", proj) + SEG_XBC], [conv_w8],
        [(1024, dproj, SEG_XBC[0])], [(8, 1024)], tr=tr, name="conv_bwd_b")
    res = _mm(h0_b, dproj, form="tn", tn=IN_W // 2, name="mm_in_dw", hosted=_pair_fill_step(gp) if dist else None)
    dw_in_p, red_a = (res[0], res[1]) if dist else (res, None)
    big_b = _group_b_grads(dw_in_p, dw_q_p, dw_k_p, dw_v_pt, dconv_w8)
    q_b = None
    if dist:
        gp_c, gp_b = _pack_group_b(big_b)
        theirs_c, theirs_b = _run_step(_merge_steps([_pair_exchange_step(gp_c), _pair_exchange_step(gp_b)]),
                                       "pair_exchange_b")
        dh0, got_c, got_b = _mm(dproj, w_in_p, form="nt", tk=IN_W // 2, name="mm_in_dx", hosted=_merge_steps(
            [_chip_exchange_step(_pair_sum(gp_c, theirs_c, "pair_sum_w_in")),
             _chip_exchange_step(_pair_sum(gp_b, theirs_b, "pair_sum_b"))]))
        q_b = ((gp_c, theirs_c, got_c), (gp_b, theirs_b, got_b))
        gp = red_a
    else:
        dh0 = _mm(dproj, w_in_p, form="nt", tk=IN_W // 2, name="mm_in_dx")
    grad_x, dg_in, db_in = _rowwise(_fn_in_ln_bwd, [x, dt1, dh0], [g_in], [D_MODEL], [(1, D_MODEL)] * 2, tr=tr,
                                    name="ln_in_bwd")

    small = {
        "ln_in_g": dg_in, "ln_in_b": db_in, "conv_b": dconv_b, "dt_bias": ddt_b[:, :8],
        "a_log": da_head[:, :8] * a_head.reshape(1, 8),
        "d_skip": ddexp.reshape(8, 64).sum(axis=1).reshape(1, 8),
        "ssd_norm_g": dg_ssd, "q_norm_g": dg_q, "kv_norm_g": dg_kv,
        "ln1_g": dg1, "ln1_b": db1, "ln2_g": dg2, "ln2_b": db2, "ln3_g": dg3, "ln3_b": db3,
    }
    return loss[0, 0], grad_x, (gp, q_b), big_b, small


def _adam(w, g, m, v, name):
    shape = w.shape
    w2, m2, v2 = (t.reshape(-1, shape[-1]) for t in (w, m, v))
    if isinstance(g, tuple):
        fn = lambda ctx, wv, gv, mv, vv: (*_fn_adam(ctx, wv, gv, mv, vv), gv)
        d, mn, vn, g = _rowwise(fn, [w2, (g[0], 0, shape[-1], g[1]), m2, v2], [], [shape[-1]] * 4, tr=ROW_TILE,
                                name=name)
    else:
        d, mn, vn = _rowwise(_fn_adam, [w2, g.reshape(-1, shape[-1]), m2, v2], [], [shape[-1]] * 3, tr=ROW_TILE,
                             name=name)
    return g.reshape(shape), d.reshape(shape), mn.reshape(shape), vn.reshape(shape)


def _adam_columns(w, g, m, v, name):
    cols = w.shape[0]
    step = cols // 2 if cols % 2 == 0 else cols
    blk = pl.BlockSpec((step,) + w.shape[1:], lambda i: (i, 0, 0))

    def body(w_ref, g_ref, m_ref, v_ref, d_ref, mo_ref, vo_ref):
        d_ref[...], mo_ref[...], vo_ref[...] = _fn_adam(None, w_ref[...], g_ref[...], m_ref[...], v_ref[...])

    return _call_with_step(body, None, None, [w, g, m, v], name=name, grid=(cols // step,), in_specs=[blk] * 4,
                           out_specs=[blk] * 3, out_shape=[jax.ShapeDtypeStruct(w.shape, F32)] * 3, sem=("arbitrary",))


def kernel(x, mem, positions, ln_in_g, ln_in_b, w_in, conv_w, conv_b, dt_bias, a_log, d_skip, ssd_norm_g, q_norm_g, w_q_up, kv_norm_g, w_kv_up, w_mix_out, ln1_g, ln1_b, w_mem_q, w_mem_k, w_mem_v, w_mem_o, ln2_g, ln2_b, w_up, w_down, ln3_g, ln3_b, loss_target, m_ln_in_g, m_ln_in_b, m_w_in, m_conv_w, m_conv_b, m_dt_bias, m_a_log, m_d_skip, m_ssd_norm_g, m_q_norm_g, m_w_q_up, m_kv_norm_g, m_w_kv_up, m_w_mix_out, m_ln1_g, m_ln1_b, m_w_mem_q, m_w_mem_k, m_w_mem_v, m_w_mem_o, m_ln2_g, m_ln2_b, m_w_up, m_w_down, m_ln3_g, m_ln3_b, v_ln_in_g, v_ln_in_b, v_w_in, v_conv_w, v_conv_b, v_dt_bias, v_a_log, v_d_skip, v_ssd_norm_g, v_q_norm_g, v_w_q_up, v_kv_norm_g, v_w_kv_up, v_w_mix_out, v_ln1_g, v_ln1_b, v_w_mem_q, v_w_mem_k, v_w_mem_v, v_w_mem_o, v_ln2_g, v_ln2_b, v_w_up, v_w_down, v_ln3_g, v_ln3_b):
    args = dict(locals())

    wp_a = jnp.concatenate([args[n].reshape(-1, PACK_COLS).astype(BF16) for n in PACK_A_ORDER], axis=0)
    flat = [args[n].reshape(-1).astype(BF16) for n in PACK_B_ORDER[:-1]]
    flat.append(lax.bitcast_convert_type(conv_w.reshape(-1), BF16).reshape(-1))
    used = sum(f.shape[0] for f in flat)
    flat.append(jnp.zeros((PACK_B_ROWS * PACK_COLS - used,), BF16))
    wp_b = jnp.concatenate(flat).reshape(PACK_B_ROWS, PACK_COLS)
    wp_c = w_in[0].astype(BF16)

    P = {n: args[n] for n in SMALL_ORDER}
    loss, grad_x, (red_a, ((gp_c, theirs_c, got_c), (gp_b, theirs_b, got_b))), _, gsmall = _local_step(
        x[0], mem[0], positions[0], loss_target[0], None, P, wp_a=wp_a, wp_b=(wp_c, wp_b))

    gs = _small_all_reduce(
        jnp.concatenate([_row(gsmall[n], PACK_COLS) for n in SMALL_ORDER] + [_row(loss, PACK_COLS)], axis=0))
    loss = gs[len(SMALL_ORDER), 0]
    red_c, red_b = _run_step(_merge_steps([_pair_fill_step(_chip_sum(gp_c, theirs_c, got_c, "chip_sum_w_in")),
                                           _pair_fill_step(_chip_sum(gp_b, theirs_b, got_b, "chip_sum_b"))]),
                             "pair_fill_b")

    grads, deltas, new_m, new_v = {}, {}, {}, {}
    for n in PACK_A_ORDER:
        grads[n], deltas[n], new_m[n], new_v[n] = _adam(args[n], (red_a, PACK_A_ROW[n]), args["m_" + n],
                                                        args["v_" + n], "adam_" + n)
    to_cols = lambda t: jnp.transpose(t, (2, 0, 1))
    from_cols = lambda t: jnp.transpose(t, (1, 2, 0))
    g_t = to_cols(red_c[None])
    grads["w_in"] = from_cols(g_t)
    deltas["w_in"], new_m["w_in"], new_v["w_in"] = map(
        from_cols, _adam_columns(to_cols(w_in), g_t, to_cols(m_w_in), to_cols(v_w_in), "adam_w_in"))
    off = 0
    for n in PACK_B_ORDER:
        sr, sc = _shard_shape(n)
        rows = sr * sc // PACK_COLS
        leaves = [args[n], red_b[off:off + rows].reshape(args[n].shape), args["m_" + n], args["v_" + n]]
        off += rows
        flip = (lambda t: jnp.transpose(t, (0, 2, 1))) if sc % LANES else (lambda t: t)
        grads[n], deltas[n], new_m[n], new_v[n] = map(flip, _adam(*map(flip, leaves), "adam_" + n))
    pack = lambda pre: jnp.concatenate([_row(args[pre + n], PACK_COLS) for n in SMALL_ORDER]
                                       + [jnp.zeros((1, PACK_COLS), F32)], axis=0)
    ds, ms, vs = _rowwise(_fn_adam, [pack(""), gs, pack("m_"), pack("v_")], [], [PACK_COLS] * 3, tr=16,
                          name="adam_small")
    for i, n in enumerate(SMALL_ORDER):
        cnt = args[n].size
        take = lambda t: t[i, :cnt].reshape(args[n].shape)
        grads[n], deltas[n], new_m[n], new_v[n] = take(gs), take(ds), take(ms), take(vs)

    order = ["ln_in_g", "ln_in_b", "w_in", "conv_w", "conv_b", "dt_bias", "a_log", "d_skip", "ssd_norm_g",
             "q_norm_g", "w_q_up", "kv_norm_g", "w_kv_up", "w_mix_out", "ln1_g", "ln1_b", "w_mem_q", "w_mem_k",
             "w_mem_v", "w_mem_o", "ln2_g", "ln2_b", "w_up", "w_down", "ln3_g", "ln3_b"]
    return (loss, grad_x[None], *[grads[n] for n in order], *[deltas[n] for n in order],
            *[new_m[n] for n in order], *[new_v[n] for n in order])
```
